```python
import jax, jax.numpy as jnp
from jax import lax
import numpy as np


D_MODEL = 1024
BATCH = 8
SEQ = 8192
DEPTH = 2

HEAD_DIM = 64
CONV_CH = D_MODEL // 4
CONV_WIDTH = 31
ATT_WIDTH = 3 * D_MODEL // 8
N_ATT_HEADS = ATT_WIDTH // HEAD_DIM
DILATION_PAIRS = ((128, 1), (512, 4), (2048, 16))
ATT_BLOCK = 128
ALIBI_MAX_EXP = 8.0
MASK_VALUE = -1e30
REC_WIDTH = 3 * D_MODEL // 8
REC_KEY_DIM = 64
REC_VAL_DIM = 64
N_REC_HEADS = REC_WIDTH // REC_VAL_DIM
REC_KEY_WIDTH = N_REC_HEADS * REC_KEY_DIM
REC_CHUNK = 64
F_TINY = 1e-30
MIX_WIDTH = CONV_CH + ATT_WIDTH + REC_WIDTH
IN_SPLITS = (CONV_CH, CONV_CH, ATT_WIDTH, ATT_WIDTH, ATT_WIDTH,
             REC_KEY_WIDTH, REC_KEY_WIDTH, REC_KEY_WIDTH, REC_WIDTH, REC_WIDTH)
IN_COLS = sum(IN_SPLITS)
D_FF = ((8 * D_MODEL // 3 + 127) // 128) * 128
FFN_CONV_WIDTH = 3
N_MOD = 6
EPS = 1e-6

kernel_name = "hybrid_conv_dilattn_hgrn2_encoder"


def _rmsnorm(x, w):
    xf = x.astype(jnp.float32)
    y = xf * lax.rsqrt(jnp.mean(xf * xf, axis=-1, keepdims=True) + EPS)
    return (y * w.astype(jnp.float32)).astype(x.dtype)


def _layernorm(x, w, b):
    xf = x.astype(jnp.float32)
    mu = jnp.mean(xf, axis=-1, keepdims=True)
    var = jnp.mean(jnp.square(xf - mu), axis=-1, keepdims=True)
    y = (xf - mu) * lax.rsqrt(var + EPS)
    return (y * w.astype(jnp.float32) + b.astype(jnp.float32)).astype(x.dtype)


def _depthwise_conv(x, w, b=None):
    k_w, ch = w.shape
    y = lax.conv_general_dilated(
        x, w[:, None, :].astype(x.dtype), window_strides=(1,),
        padding=[(k_w // 2, k_w // 2)],
        dimension_numbers=('NWC', 'WIO', 'NWC'), feature_group_count=ch)
    if b is not None:
        y = y + b.astype(x.dtype)
    return y


def _dilated_window_attention(q, k, v, slopes, window, dilation):
    B, S, H, Dh = q.shape
    half = window // (2 * dilation)
    L = S // dilation

    def to_sub(t):
        return t.reshape(B, L, dilation, H, Dh).transpose(0, 2, 1, 3, 4).reshape(B * dilation, L, H, Dh)

    qs, ks, vs = to_sub(q), to_sub(k), to_sub(v)
    blk = min(ATT_BLOCK, L)
    nb = -(-L // blk)
    Lp = nb * blk
    span = blk + 2 * half
    qs = jnp.pad(qs, ((0, 0), (0, Lp - L), (0, 0), (0, 0)))
    pad_k = ((0, 0), (half, half + Lp - L), (0, 0), (0, 0))
    ks = jnp.pad(ks, pad_k)
    vs = jnp.pad(vs, pad_k)
    idx = (np.arange(nb) * blk)[:, None] + np.arange(span)[None, :]
    kb = ks[:, idx]
    vb = vs[:, idx]
    qb = qs.reshape(B * dilation, nb, blk, H, Dh)
    rel = np.arange(span)[None, :] - half - np.arange(blk)[:, None]
    key_pos = idx - half
    valid = ((np.abs(rel) <= half)[None]
             & (key_pos[:, None, :] >= 0) & (key_pos[:, None, :] < L))
    dist = jnp.asarray(np.abs(rel) * dilation, jnp.float32)
    scores = jnp.einsum('bnqhd,bnkhd->bnhqk', qb, kb) * (Dh ** -0.5)
    scores = scores - slopes[:, None, None] * dist
    scores = jnp.where(jnp.asarray(valid)[:, None], scores, MASK_VALUE)
    lse = jax.nn.logsumexp(scores, axis=-1)
    p = jnp.exp(scores - lse[..., None])
    o = jnp.einsum('bnhqk,bnkhd->bnqhd', p, vb).reshape(B * dilation, Lp, H, Dh)[:, :L]
    lse = lse.transpose(0, 1, 3, 2).reshape(B * dilation, Lp, H)[:, :L]

    def from_sub(t):
        t5 = t.reshape((B, dilation, L) + t.shape[2:])
        t5 = jnp.moveaxis(t5, 1, 2)
        return t5.reshape((B, S) + t5.shape[3:])

    return from_sub(o), from_sub(lse)


def _mixture_of_dilated_attention(q, k, v):
    slopes = jnp.asarray(2.0 ** (-ALIBI_MAX_EXP * np.arange(1, N_ATT_HEADS + 1) / N_ATT_HEADS), jnp.float32)
    outs, lses = [], []
    for window, dilation in DILATION_PAIRS:
        o_g, l_g = _dilated_window_attention(q, k, v, slopes, window, dilation)
        outs.append(o_g)
        lses.append(l_g)
    wts = jax.nn.softmax(jnp.stack(lses, 0), axis=0)
    return jnp.einsum('gbsh,gbshd->bshd', wts, jnp.stack(outs, 0))


def _hgrn2_scan(q, k, v, logf):
    B, S, H, dk = q.shape
    dv = v.shape[-1]
    n = S // REC_CHUNK

    def chunks(t):
        return t.reshape(B, n, REC_CHUNK, H, t.shape[-1]).transpose(1, 0, 3, 2, 4)

    qc, kc, vc = chunks(q), chunks(k), chunks(v)
    bc = jnp.cumsum(chunks(logf), axis=3)
    lower = jnp.asarray(np.tril(np.ones((REC_CHUNK, REC_CHUNK), bool)))[:, :, None]

    def step(state, inp):
        qt, kt, vt, bt = inp
        diff = bt[:, :, :, None, :] - bt[:, :, None, :, :]
        decay = jnp.where(lower, jnp.exp(jnp.where(lower, diff, 0.0)), 0.0)
        scores = jnp.einsum('bhtk,bhsk,bhtsk->bhts', qt, kt, decay)
        o = (jnp.einsum('bhts,bhsv->bhtv', scores, vt)
             + jnp.einsum('bhtk,bhkv->bhtv', qt * jnp.exp(bt), state))
        b_last = bt[:, :, -1:, :]
        state = (jnp.exp(b_last[:, :, 0, :])[..., None] * state
                 + jnp.einsum('bhsk,bhsv->bhkv', kt * jnp.exp(b_last - bt), vt))
        return state, o

    state0 = jnp.zeros((B, H, dk, dv), q.dtype)
    _, o = lax.scan(step, state0, (qc, kc, vc, bc))
    return o.transpose(1, 0, 3, 2, 4).reshape(B, S, H, dv)


def _hgrn2_gate(z, lb):
    f = lb + (1.0 - lb) * jax.nn.sigmoid(z)
    logf = jnp.log(jnp.maximum(f, F_TINY))
    k = (1.0 - lb) * jax.nn.sigmoid(-z)
    return logf, k


def _token_mixers(h, w_in, conv_w, conv_b, ln_w, ln_b, lb_fwd, lb_bwd, rec_norm_w, w_out):
    B, S, _ = h.shape
    f32 = jnp.float32
    proj = h @ w_in.astype(h.dtype)
    cuts = [int(c) for c in np.cumsum(IN_SPLITS)[:-1]]
    (a_val, a_gate, q_att, k_att, v_att,
     q_rec, z_fwd, z_bwd, i_rec, g_rec) = jnp.split(proj, cuts, axis=-1)

    a = a_val * jax.nn.sigmoid(a_gate)
    a = _depthwise_conv(a, conv_w, conv_b)
    a = jax.nn.silu(_layernorm(a, ln_w, ln_b))

    def heads(t, d):
        return t.astype(f32).reshape(B, S, -1, d)
    att = _mixture_of_dilated_attention(heads(q_att, HEAD_DIM), heads(k_att, HEAD_DIM), heads(v_att, HEAD_DIM))
    att = att.reshape(B, S, ATT_WIDTH).astype(h.dtype)

    qr = heads(jax.nn.silu(q_rec.astype(f32)), REC_KEY_DIM)
    vr = heads(i_rec, REC_VAL_DIM)
    logf_f, k_f = _hgrn2_gate(z_fwd.astype(f32), lb_fwd)
    logf_b, k_b = _hgrn2_gate(z_bwd.astype(f32), lb_bwd)
    o_f = _hgrn2_scan(qr, heads(k_f, REC_KEY_DIM), vr, heads(logf_f, REC_KEY_DIM))
    flip = lambda t: jnp.flip(t, axis=1)
    o_b = flip(_hgrn2_scan(flip(qr), flip(heads(k_b, REC_KEY_DIM)), flip(vr), flip(heads(logf_b, REC_KEY_DIM))))
    o_r = o_f + o_b
    o_r = o_r * lax.rsqrt(jnp.mean(o_r * o_r, axis=-1, keepdims=True) + EPS)
    o_r = o_r.reshape(B, S, REC_WIDTH) * rec_norm_w.astype(f32)
    rec = (o_r * jax.nn.silu(g_rec.astype(f32))).astype(h.dtype)

    mixed = jnp.concatenate([a, att, rec], axis=-1)
    return mixed @ w_out.astype(h.dtype)


def _conv_ffn(h, w_up, conv_w, w_down):
    u = h @ w_up.astype(h.dtype)
    u = _depthwise_conv(u, conv_w)
    gate, val = jnp.split(u, 2, axis=-1)
    return (jax.nn.gelu(gate, approximate=False) * val) @ w_down.astype(h.dtype)


def _fwd_setup_inputs(seed: int = 0) -> dict:
    key = jax.random.key(seed)
    ks = jax.random.split(key, 18)
    f32 = jnp.float32
    D = D_MODEL

    def nrm(k, shape, scale):
        return jax.random.normal(k, shape, f32) * scale

    return {
        "x": nrm(ks[0], (BATCH, SEQ, D), 1.0),
        "c": nrm(ks[1], (BATCH, D), 1.0),
        "w_ada": nrm(ks[2], (DEPTH, D, N_MOD * D), 0.5 * D ** -0.5),
        "b_ada": nrm(ks[3], (DEPTH, N_MOD * D), 0.02),
        "norm1_w": 1.0 + nrm(ks[4], (DEPTH, D), 0.05),
        "w_in": nrm(ks[5], (DEPTH, D, IN_COLS), D ** -0.5),
        "conv_a_w": nrm(ks[6], (DEPTH, CONV_WIDTH, CONV_CH), CONV_WIDTH ** -0.5),
        "conv_a_b": nrm(ks[7], (DEPTH, CONV_CH), 0.02),
        "ln_a_w": 1.0 + nrm(ks[8], (DEPTH, CONV_CH), 0.05),
        "ln_a_b": nrm(ks[9], (DEPTH, CONV_CH), 0.02),
        "lb_gamma": nrm(ks[10], (DEPTH, 2, REC_KEY_WIDTH), 1.0),
        "rec_norm_w": 1.0 + nrm(ks[11], (DEPTH, REC_WIDTH), 0.05),
        "w_out": nrm(ks[12], (DEPTH, MIX_WIDTH, D), MIX_WIDTH ** -0.5),
        "norm2_w": 1.0 + nrm(ks[13], (DEPTH, D), 0.05),
        "w_up": nrm(ks[14], (DEPTH, D, 2 * D_FF), D ** -0.5),
        "conv_f_w": nrm(ks[15], (DEPTH, FFN_CONV_WIDTH, 2 * D_FF), FFN_CONV_WIDTH ** -0.5),
        "w_down": nrm(ks[16], (DEPTH, D_FF, D), D_FF ** -0.5),
        "final_norm_w": 1.0 + nrm(ks[17], (D,), 0.05),
    }


def _fwd_reference(x, c, w_ada, b_ada, norm1_w, w_in, conv_a_w, conv_a_b, ln_a_w, ln_a_b,
              lb_gamma, rec_norm_w, w_out, norm2_w, w_up, conv_f_w, w_down, final_norm_w):
    p = jax.nn.softmax(lb_gamma.astype(jnp.float32), axis=0)
    lower_bounds = jnp.cumsum(p, axis=0) - p[0:1]
    cond = jax.nn.silu(c)
    for l in range(DEPTH):
        mod = (cond @ w_ada[l] + b_ada[l]).astype(x.dtype)[:, None, :]
        sh1, sc1, g1, sh2, sc2, g2 = jnp.split(mod, N_MOD, axis=-1)
        h = _rmsnorm(x, norm1_w[l]) * (1.0 + sc1) + sh1
        x = x + g1 * _token_mixers(h, w_in[l], conv_a_w[l], conv_a_b[l], ln_a_w[l], ln_a_b[l],
                                   lower_bounds[l, 0], lower_bounds[l, 1], rec_norm_w[l], w_out[l])
        h = _rmsnorm(x, norm2_w[l]) * (1.0 + sc2) + sh2
        x = x + g2 * _conv_ffn(h, w_up[l], conv_f_w[l], w_down[l])
    return _rmsnorm(x, final_norm_w)


import jax as _jax
import jax.numpy as _jnp

TWIN_FORMAT = 'train_step'
FWD_PARAMS = ['x', 'c', 'w_ada', 'b_ada', 'norm1_w', 'w_in', 'conv_a_w', 'conv_a_b', 'ln_a_w', 'ln_a_b', 'lb_gamma', 'rec_norm_w', 'w_out', 'norm2_w', 'w_up', 'conv_f_w', 'w_down', 'final_norm_w']
TWIN_WEIGHTS = ['w_ada', 'b_ada', 'norm1_w', 'w_in', 'conv_a_w', 'conv_a_b', 'ln_a_w', 'ln_a_b', 'lb_gamma', 'rec_norm_w', 'w_out', 'norm2_w', 'w_up', 'conv_f_w', 'w_down', 'final_norm_w']
TWIN_DIFF_INPUT = 'x'
TWIN_INPUTS = ['x', 'c', 'w_ada', 'b_ada', 'norm1_w', 'w_in', 'conv_a_w', 'conv_a_b', 'ln_a_w', 'ln_a_b', 'lb_gamma', 'rec_norm_w', 'w_out', 'norm2_w', 'w_up', 'conv_f_w', 'w_down', 'final_norm_w', 'loss_target', 'm_w_ada', 'm_b_ada', 'm_norm1_w', 'm_w_in', 'm_conv_a_w', 'm_conv_a_b', 'm_ln_a_w', 'm_ln_a_b', 'm_lb_gamma', 'm_rec_norm_w', 'm_w_out', 'm_norm2_w', 'm_w_up', 'm_conv_f_w', 'm_w_down', 'm_final_norm_w', 'v_w_ada', 'v_b_ada', 'v_norm1_w', 'v_w_in', 'v_conv_a_w', 'v_conv_a_b', 'v_ln_a_w', 'v_ln_a_b', 'v_lb_gamma', 'v_rec_norm_w', 'v_w_out', 'v_norm2_w', 'v_w_up', 'v_conv_f_w', 'v_w_down', 'v_final_norm_w']
TWIN_OUTPUTS = ['loss', 'grad_x', 'grad_w_ada', 'grad_b_ada', 'grad_norm1_w', 'grad_w_in', 'grad_conv_a_w', 'grad_conv_a_b', 'grad_ln_a_w', 'grad_ln_a_b', 'grad_lb_gamma', 'grad_rec_norm_w', 'grad_w_out', 'grad_norm2_w', 'grad_w_up', 'grad_conv_f_w', 'grad_w_down', 'grad_final_norm_w', 'delta_w_ada', 'delta_b_ada', 'delta_norm1_w', 'delta_w_in', 'delta_conv_a_w', 'delta_conv_a_b', 'delta_ln_a_w', 'delta_ln_a_b', 'delta_lb_gamma', 'delta_rec_norm_w', 'delta_w_out', 'delta_norm2_w', 'delta_w_up', 'delta_conv_f_w', 'delta_w_down', 'delta_final_norm_w', 'new_m_w_ada', 'new_m_b_ada', 'new_m_norm1_w', 'new_m_w_in', 'new_m_conv_a_w', 'new_m_conv_a_b', 'new_m_ln_a_w', 'new_m_ln_a_b', 'new_m_lb_gamma', 'new_m_rec_norm_w', 'new_m_w_out', 'new_m_norm2_w', 'new_m_w_up', 'new_m_conv_f_w', 'new_m_w_down', 'new_m_final_norm_w', 'new_v_w_ada', 'new_v_b_ada', 'new_v_norm1_w', 'new_v_w_in', 'new_v_conv_a_w', 'new_v_conv_a_b', 'new_v_ln_a_w', 'new_v_ln_a_b', 'new_v_lb_gamma', 'new_v_rec_norm_w', 'new_v_w_out', 'new_v_norm2_w', 'new_v_w_up', 'new_v_conv_f_w', 'new_v_w_down', 'new_v_final_norm_w']
TWIN_LEAF_KINDS = {'loss': 'loss', 'grad_x': 'grad_x', 'grad_w_ada': 'grad_w', 'grad_b_ada': 'grad_w', 'grad_norm1_w': 'grad_w', 'grad_w_in': 'grad_w', 'grad_conv_a_w': 'grad_w', 'grad_conv_a_b': 'grad_w', 'grad_ln_a_w': 'grad_w', 'grad_ln_a_b': 'grad_w', 'grad_lb_gamma': 'grad_w', 'grad_rec_norm_w': 'grad_w', 'grad_w_out': 'grad_w', 'grad_norm2_w': 'grad_w', 'grad_w_up': 'grad_w', 'grad_conv_f_w': 'grad_w', 'grad_w_down': 'grad_w', 'grad_final_norm_w': 'grad_w', 'delta_w_ada': 'delta_w', 'delta_b_ada': 'delta_w', 'delta_norm1_w': 'delta_w', 'delta_w_in': 'delta_w', 'delta_conv_a_w': 'delta_w', 'delta_conv_a_b': 'delta_w', 'delta_ln_a_w': 'delta_w', 'delta_ln_a_b': 'delta_w', 'delta_lb_gamma': 'delta_w', 'delta_rec_norm_w': 'delta_w', 'delta_w_out': 'delta_w', 'delta_norm2_w': 'delta_w', 'delta_w_up': 'delta_w', 'delta_conv_f_w': 'delta_w', 'delta_w_down': 'delta_w', 'delta_final_norm_w': 'delta_w', 'new_m_w_ada': 'new_m', 'new_m_b_ada': 'new_m', 'new_m_norm1_w': 'new_m', 'new_m_w_in': 'new_m', 'new_m_conv_a_w': 'new_m', 'new_m_conv_a_b': 'new_m', 'new_m_ln_a_w': 'new_m', 'new_m_ln_a_b': 'new_m', 'new_m_lb_gamma': 'new_m', 'new_m_rec_norm_w': 'new_m', 'new_m_w_out': 'new_m', 'new_m_norm2_w': 'new_m', 'new_m_w_up': 'new_m', 'new_m_conv_f_w': 'new_m', 'new_m_w_down': 'new_m', 'new_m_final_norm_w': 'new_m', 'new_v_w_ada': 'new_v', 'new_v_b_ada': 'new_v', 'new_v_norm1_w': 'new_v', 'new_v_w_in': 'new_v', 'new_v_conv_a_w': 'new_v', 'new_v_conv_a_b': 'new_v', 'new_v_ln_a_w': 'new_v', 'new_v_ln_a_b': 'new_v', 'new_v_lb_gamma': 'new_v', 'new_v_rec_norm_w': 'new_v', 'new_v_w_out': 'new_v', 'new_v_norm2_w': 'new_v', 'new_v_w_up': 'new_v', 'new_v_conv_f_w': 'new_v', 'new_v_w_down': 'new_v', 'new_v_final_norm_w': 'new_v'}


def _forward(args):
    return _fwd_reference(*[args[k] for k in FWD_PARAMS])


def _output_shape():
    def fwd():
        inp = _fwd_setup_inputs(0)
        return _fwd_reference(*[inp[k] for k in FWD_PARAMS])
    out = _jax.eval_shape(fwd)
    return out.shape, out.dtype

N_MICROBATCH = 1
ADAM_LR = 0.001
ADAM_B1 = 0.9
ADAM_B2 = 0.999
ADAM_EPS = 1e-08
ADAM_WD = 0.01
ADAM_STEP = 10
PER_EXAMPLE_BATCH_AXIS = {'x': 0, 'c': 0, 'loss_target': 0}
SHARED_INPUTS = []
_WEIGHT_DTYPES = {'w_ada': _jnp.float32, 'b_ada': _jnp.float32, 'norm1_w': _jnp.float32, 'w_in': _jnp.float32, 'conv_a_w': _jnp.float32, 'conv_a_b': _jnp.float32, 'ln_a_w': _jnp.float32, 'ln_a_b': _jnp.float32, 'lb_gamma': _jnp.float32, 'rec_norm_w': _jnp.float32, 'w_out': _jnp.float32, 'norm2_w': _jnp.float32, 'w_up': _jnp.float32, 'conv_f_w': _jnp.float32, 'w_down': _jnp.float32, 'final_norm_w': _jnp.float32}
MOMENT_SCALE = {'w_ada': 9.091622e-02, 'b_ada': 1.715282e-01, 'norm1_w': 5.362415e-02, 'w_in': 3.183800e-02, 'conv_a_w': 4.880585e-02, 'conv_a_b': 1.008415e-01, 'ln_a_w': 6.813479e-02, 'ln_a_b': 6.213153e-02, 'lb_gamma': 3.009513e-03, 'rec_norm_w': 5.845552e-02, 'w_out': 4.554905e-02, 'norm2_w': 8.636256e-02, 'w_up': 3.570721e-02, 'conv_f_w': 3.781024e-02, 'w_down': 5.863677e-02, 'final_norm_w': 6.418442e+01}


def _to_microbatches(a, axis):
    t = _jnp.moveaxis(a, axis, 0)
    t = t.reshape((N_MICROBATCH, t.shape[0] // N_MICROBATCH) + t.shape[1:])
    return _jnp.moveaxis(t, 1, axis + 1)


def setup_inputs(seed: int = 0) -> dict:
    inp = _fwd_setup_inputs(seed)
    key = _jax.random.fold_in(_jax.random.key(seed), 7919)
    shape, _ = _output_shape()
    out = dict(inp)
    out["loss_target"] = _jax.random.normal(_jax.random.fold_in(key, 0), shape, _jnp.float32)
    for i, name in enumerate(TWIN_WEIGHTS):
        w = inp[name].astype(_jnp.float32)
        if MOMENT_SCALE is None:
            s = _jnp.sqrt(_jnp.mean(_jnp.square(w)) + 1e-30)
        else:
            s = MOMENT_SCALE[name]
        km, kv = _jax.random.split(_jax.random.fold_in(key, i + 1))
        out[name] = w
        out["m_" + name] = s * _jax.random.normal(km, w.shape, _jnp.float32)
        out["v_" + name] = (s * s) * _jax.random.uniform(kv, w.shape, _jnp.float32, 0.5, 1.5)
    if N_MICROBATCH > 1:
        for name, axis in PER_EXAMPLE_BATCH_AXIS.items():
            out[name] = _to_microbatches(out[name], axis)
    return {'x': out['x'], 'c': out['c'], 'w_ada': out['w_ada'], 'b_ada': out['b_ada'], 'norm1_w': out['norm1_w'], 'w_in': out['w_in'], 'conv_a_w': out['conv_a_w'], 'conv_a_b': out['conv_a_b'], 'ln_a_w': out['ln_a_w'], 'ln_a_b': out['ln_a_b'], 'lb_gamma': out['lb_gamma'], 'rec_norm_w': out['rec_norm_w'], 'w_out': out['w_out'], 'norm2_w': out['norm2_w'], 'w_up': out['w_up'], 'conv_f_w': out['conv_f_w'], 'w_down': out['w_down'], 'final_norm_w': out['final_norm_w'], 'loss_target': out['loss_target'], 'm_w_ada': out['m_w_ada'], 'm_b_ada': out['m_b_ada'], 'm_norm1_w': out['m_norm1_w'], 'm_w_in': out['m_w_in'], 'm_conv_a_w': out['m_conv_a_w'], 'm_conv_a_b': out['m_conv_a_b'], 'm_ln_a_w': out['m_ln_a_w'], 'm_ln_a_b': out['m_ln_a_b'], 'm_lb_gamma': out['m_lb_gamma'], 'm_rec_norm_w': out['m_rec_norm_w'], 'm_w_out': out['m_w_out'], 'm_norm2_w': out['m_norm2_w'], 'm_w_up': out['m_w_up'], 'm_conv_f_w': out['m_conv_f_w'], 'm_w_down': out['m_w_down'], 'm_final_norm_w': out['m_final_norm_w'], 'v_w_ada': out['v_w_ada'], 'v_b_ada': out['v_b_ada'], 'v_norm1_w': out['v_norm1_w'], 'v_w_in': out['v_w_in'], 'v_conv_a_w': out['v_conv_a_w'], 'v_conv_a_b': out['v_conv_a_b'], 'v_ln_a_w': out['v_ln_a_w'], 'v_ln_a_b': out['v_ln_a_b'], 'v_lb_gamma': out['v_lb_gamma'], 'v_rec_norm_w': out['v_rec_norm_w'], 'v_w_out': out['v_w_out'], 'v_norm2_w': out['v_norm2_w'], 'v_w_up': out['v_w_up'], 'v_conv_f_w': out['v_conv_f_w'], 'v_w_down': out['v_w_down'], 'v_final_norm_w': out['v_final_norm_w']}


def _loss(weights, diff, rest, loss_target):
    with _jax.named_scope("forward"):
        args = {**rest, TWIN_DIFF_INPUT: diff, **{k: w.astype(_WEIGHT_DTYPES[k]) for k, w in weights.items()}}
        y = _forward(args)
    with _jax.named_scope("loss_head"):
        err = _jnp.square(y.astype(_jnp.float32) - loss_target)
        return 0.5 * _jnp.sum(_jnp.mean(err, axis=-1)) if err.ndim else 0.5 * err


def _adamw(w, g, m, v):
    m = ADAM_B1 * m + (1.0 - ADAM_B1) * g
    v = ADAM_B2 * v + (1.0 - ADAM_B2) * _jnp.square(g)
    m_hat = m / (1.0 - ADAM_B1 ** ADAM_STEP)
    v_hat = v / (1.0 - ADAM_B2 ** ADAM_STEP)
    delta = -ADAM_LR * (m_hat / (_jnp.sqrt(v_hat) + ADAM_EPS) + ADAM_WD * w)
    return delta, m, v


def reference(x, c, w_ada, b_ada, norm1_w, w_in, conv_a_w, conv_a_b, ln_a_w, ln_a_b, lb_gamma, rec_norm_w, w_out, norm2_w, w_up, conv_f_w, w_down, final_norm_w, loss_target, m_w_ada, m_b_ada, m_norm1_w, m_w_in, m_conv_a_w, m_conv_a_b, m_ln_a_w, m_ln_a_b, m_lb_gamma, m_rec_norm_w, m_w_out, m_norm2_w, m_w_up, m_conv_f_w, m_w_down, m_final_norm_w, v_w_ada, v_b_ada, v_norm1_w, v_w_in, v_conv_a_w, v_conv_a_b, v_ln_a_w, v_ln_a_b, v_lb_gamma, v_rec_norm_w, v_w_out, v_norm2_w, v_w_up, v_conv_f_w, v_w_down, v_final_norm_w):
    given = dict(x=x, c=c, w_ada=w_ada, b_ada=b_ada, norm1_w=norm1_w, w_in=w_in, conv_a_w=conv_a_w, conv_a_b=conv_a_b, ln_a_w=ln_a_w, ln_a_b=ln_a_b, lb_gamma=lb_gamma, rec_norm_w=rec_norm_w, w_out=w_out, norm2_w=norm2_w, w_up=w_up, conv_f_w=conv_f_w, w_down=w_down, final_norm_w=final_norm_w, loss_target=loss_target, m_w_ada=m_w_ada, m_b_ada=m_b_ada, m_norm1_w=m_norm1_w, m_w_in=m_w_in, m_conv_a_w=m_conv_a_w, m_conv_a_b=m_conv_a_b, m_ln_a_w=m_ln_a_w, m_ln_a_b=m_ln_a_b, m_lb_gamma=m_lb_gamma, m_rec_norm_w=m_rec_norm_w, m_w_out=m_w_out, m_norm2_w=m_norm2_w, m_w_up=m_w_up, m_conv_f_w=m_conv_f_w, m_w_down=m_w_down, m_final_norm_w=m_final_norm_w, v_w_ada=v_w_ada, v_b_ada=v_b_ada, v_norm1_w=v_norm1_w, v_w_in=v_w_in, v_conv_a_w=v_conv_a_w, v_conv_a_b=v_conv_a_b, v_ln_a_w=v_ln_a_w, v_ln_a_b=v_ln_a_b, v_lb_gamma=v_lb_gamma, v_rec_norm_w=v_rec_norm_w, v_w_out=v_w_out, v_norm2_w=v_norm2_w, v_w_up=v_w_up, v_conv_f_w=v_conv_f_w, v_w_down=v_w_down, v_final_norm_w=v_final_norm_w)
    weights = {n: given[n] for n in TWIN_WEIGHTS}
    shared = {n: given[n] for n in SHARED_INPUTS}
    per_example = {n: given[n] for n in ['x', 'c']}
    grad_fn = _jax.value_and_grad(_loss, argnums=(0, 1))

    def one_microbatch(ex, loss_target):
        ex = dict(ex)
        diff = ex.pop(TWIN_DIFF_INPUT)
        return grad_fn(weights, diff, {**shared, **ex}, loss_target)

    if N_MICROBATCH == 1:
        loss, (grad_w, grad_x) = one_microbatch(per_example, given["loss_target"])
    else:
        def body(carry, xs):
            loss_sum, grad_sum = carry
            l_k, (gw_k, gx_k) = one_microbatch(xs[0], xs[1])
            with _jax.named_scope("update"):
                return (loss_sum + l_k, _jax.tree.map(_jnp.add, grad_sum, gw_k)), gx_k

        init = (_jnp.zeros((), _jnp.float32), _jax.tree.map(_jnp.zeros_like, weights))
        (loss, grad_w), grad_x = _jax.lax.scan(body, init, (per_example, given["loss_target"]))
    with _jax.named_scope("update"):
        delta_w, new_m, new_v = {}, {}, {}
        for n in TWIN_WEIGHTS:
            delta_w[n], new_m[n], new_v[n] = _adamw(weights[n], grad_w[n], given["m_" + n], given["v_" + n])
    return (loss, grad_x, *[grad_w[n] for n in TWIN_WEIGHTS], *[delta_w[n] for n in TWIN_WEIGHTS],
            *[new_m[n] for n in TWIN_WEIGHTS], *[new_v[n] for n in TWIN_WEIGHTS])
```

```python
import functools

import numpy as np
import jax
import jax.numpy as jnp
from jax import lax
from jax.experimental import pallas as pl
from jax.experimental.pallas import tpu as pltpu

F32 = jnp.float32
BF16 = jnp.bfloat16
HP = lax.Precision.HIGHEST
MESH = pl.DeviceIdType.MESH

N_DEV = 8
D = 1024
DEPTH = 2
CONV_CH = 256
CONV_W = 31
CONV_W_PAD = 32
ATT_W = 384
REC_W = 384
N_HEADS = 6
HEAD = 64
HEAD_SHIFT = 6
HALF_BAND = 64
ATT_BLK = 128
DILATIONS = (1, 4, 16)
ALIBI_SLOPES = tuple(float(2.0 ** (-8.0 * (h + 1) / N_HEADS)) for h in range(N_HEADS))
MASK_VALUE = -1e30
REC_CHUNK = 64
EXP_CLAMP = 80.0
F_TINY = 1e-30
IN_COLS = 3584
D_FF = 2816
FFN_CONV_W = 3
N_MOD = 6
EPS = 1e-6
ADAM_LR, ADAM_B1, ADAM_B2, ADAM_EPS, ADAM_WD, ADAM_STEP = 0.001, 0.9, 0.999, 1e-08, 0.01, 10

VMEM_LIMIT_BYTES = 48 * 1024 * 1024
SUBLANES_F32 = 8
LANES = 128

QA, KA, VA, QR, ZF, ZB, IR, GR = range(8)
AV_BLK, AG_BLK = 12, 13
CONV_COLS = 2 * CONV_CH


def _cparams(sem=None):
    kw = dict(vmem_limit_bytes=VMEM_LIMIT_BYTES)
    if sem is not None:
        kw["dimension_semantics"] = sem
    return pltpu.CompilerParams(**kw)


def _iota(shape, dim):
    return lax.broadcasted_iota(jnp.int32, shape, dim)


def _dot(a, b, dims, precision=None):
    return lax.dot_general(a, b, (dims, ((), ())), precision=precision, preferred_element_type=F32)


def _dot_nn(a, b, precision=None):
    return _dot(a, b, ((1,), (0,)), precision)


def _dot_nt(a, b, precision=None):
    return _dot(a, b, ((1,), (1,)), precision)


def _dot_tn(a, b, precision=None):
    return _dot(a, b, ((0,), (0,)), precision)


def _c0(j):
    return 0


def _pick(n, cands):
    for c in cands:
        if n % c == 0:
            return c
    return n


def _matmul(name, a, b, mode, out_dtype=F32):
    if mode == "nn":
        (M, K), (_, N) = a.shape, b.shape
    elif mode == "nt":
        (M, K), (N, _) = a.shape, b.shape
    else:
        (K, M), (_, N) = a.shape, b.shape
    tm = _pick(M, (512, 256, 128))
    tn = _pick(N, (512, 256, 128))
    tk = K if K <= 4096 else _pick(K, (512, 256))
    nk = K // tk
    if mode == "nn":
        a_spec = pl.BlockSpec((tm, tk), lambda i, j, k: (i, k))
        b_spec = pl.BlockSpec((tk, tn), lambda i, j, k: (k, j))
        dims = ((1,), (0,))
    elif mode == "nt":
        a_spec = pl.BlockSpec((tm, tk), lambda i, j, k: (i, k))
        b_spec = pl.BlockSpec((tn, tk), lambda i, j, k: (j, k))
        dims = ((1,), (1,))
    else:
        a_spec = pl.BlockSpec((tk, tm), lambda i, j, k: (k, i))
        b_spec = pl.BlockSpec((tk, tn), lambda i, j, k: (k, j))
        dims = ((0,), (0,))

    def body(a_ref, b_ref, o_ref, acc_ref):
        k = pl.program_id(2)
        part = _dot(a_ref[...].astype(BF16), b_ref[...].astype(BF16), dims)

        @pl.when(k == 0)
        def _():
            acc_ref[...] = part

        @pl.when(k > 0)
        def _():
            acc_ref[...] += part

        @pl.when(k == nk - 1)
        def _():
            o_ref[...] = acc_ref[...].astype(o_ref.dtype)

    return pl.pallas_call(
        body, name=name, grid=(M // tm, N // tn, nk),
        in_specs=[a_spec, b_spec],
        out_specs=pl.BlockSpec((tm, tn), lambda i, j, k: (i, j)),
        out_shape=jax.ShapeDtypeStruct((M, N), out_dtype),
        scratch_shapes=[pltpu.VMEM((tm, tn), F32)],
        compiler_params=_cparams(("parallel", "parallel", "arbitrary")),
    )(a, b)


def _rowwise(name, fn, S, ts, tiles, params=(), outs=(), accs=(), halo=0, ncb=1):
    in_specs, args = [], []
    for arr, w, jm, with_halo in tiles:
        if with_halo:
            hb, nhb = ts // halo, S // halo
            in_specs += [
                pl.BlockSpec((halo, w), lambda j, i, jm=jm, hb=hb: (jnp.maximum(i * hb - 1, 0), jm(j))),
                pl.BlockSpec((ts, w), lambda j, i, jm=jm: (i, jm(j))),
                pl.BlockSpec((halo, w), lambda j, i, jm=jm, hb=hb, nhb=nhb: (jnp.minimum((i + 1) * hb, nhb - 1), jm(j))),
            ]
            args += [arr, arr, arr]
        else:
            in_specs.append(pl.BlockSpec((ts, w), lambda j, i, jm=jm: (i, jm(j))))
            args.append(arr)
    for arr, r, w, jm in params:
        in_specs.append(pl.BlockSpec((r, w), lambda j, i, jm=jm: (0, jm(j))))
        args.append(arr)
    out_specs, out_shape = [], []
    for w, dt, jm, tw in outs:
        out_specs.append(pl.BlockSpec((ts, w), lambda j, i, jm=jm: (i, jm(j))))
        out_shape.append(jax.ShapeDtypeStruct((S, tw), dt))
    for r, w, jm, tw in accs:
        out_specs.append(pl.BlockSpec((r, w), lambda j, i, jm=jm: (0, jm(j))))
        out_shape.append(jax.ShapeDtypeStruct((r, tw), F32))
    n_tiles, n_params, n_outs = len(tiles), len(params), len(outs)

    def body(*refs):
        i = pl.program_id(1)
        pos, vals = 0, []
        for _, w, _, with_halo in tiles:
            if with_halo:
                ext = jnp.concatenate([refs[pos][...], refs[pos + 1][...], refs[pos + 2][...]], axis=0)
                row = i * ts - halo + _iota((ts + 2 * halo, 1), 0)
                vals.append(jnp.where((row >= 0) & (row < S), ext, jnp.zeros_like(ext)))
                pos += 3
            else:
                vals.append(refs[pos][...])
                pos += 1
        prefs = refs[pos:pos + n_params]
        orefs = refs[pos + n_params:pos + n_params + n_outs]
        arefs = refs[pos + n_params + n_outs:]

        @pl.when(i == 0)
        def _():
            for r in arefs:
                r[...] = jnp.zeros_like(r)

        fn(i, vals, prefs, orefs, arefs)

    res = pl.pallas_call(
        body, name=name, grid=(ncb, S // ts),
        in_specs=in_specs, out_specs=out_specs, out_shape=out_shape,
        compiler_params=_cparams(("arbitrary", "arbitrary")),
    )(*args)
    return res


def _vmem_call(name, fn, ins, out_shapes):
    n_in = len(ins)

    def body(*refs):
        vals = fn(*[r[...] for r in refs[:n_in]])
        for r, v in zip(refs[n_in:], vals):
            r[...] = v.astype(r.dtype)

    return pl.pallas_call(
        body, name=name,
        out_shape=[jax.ShapeDtypeStruct(s, dt) for s, dt in out_shapes],
        compiler_params=_cparams(),
    )(*ins)


def _rms(x, w):
    return x * lax.rsqrt(jnp.mean(x * x, axis=-1, keepdims=True) + EPS) * w


def _normmod_f(x, nw, sc, sh):
    return _rms(x, nw) * (1.0 + sc) + sh


def _row_params(*vecs):
    return [(v, 1, v.shape[1], _c0) for v in vecs]


def _normmod_fwd(x, nw, sc, sh):
    S = x.shape[0]

    def fn(i, vals, p, o, a):
        o[0][...] = _normmod_f(vals[0], p[0][...], p[1][...], p[2][...]).astype(BF16)

    return _rowwise("normmod_fwd", fn, S, 512, [(x, D, _c0, False)], _row_params(nw, sc, sh), [(D, BF16, _c0, D)])[0]


def _normmod_bwd(x, gh, gres, nw, sc, sh):
    S = x.shape[0]

    def fn(i, vals, p, o, a):
        _, vjp = jax.vjp(_normmod_f, vals[0], p[0][...], p[1][...], p[2][...])
        gx, gnw, gsc, gsh = vjp(vals[1])
        o[0][...] = gx + vals[2]
        a[0][...] += gnw
        a[1][...] += gsc
        a[2][...] += gsh

    return _rowwise("normmod_bwd", fn, S, 256, [(x, D, _c0, False), (gh, D, _c0, False), (gres, D, _c0, False)],
                    _row_params(nw, sc, sh), [(D, F32, _c0, D)], [(1, D, _c0, D)] * 3)


def _gate_add(x, y, g):
    S = x.shape[0]

    def fn(i, vals, p, o, a):
        o[0][...] = vals[0] + p[0][...] * vals[1]

    return _rowwise("gate_add", fn, S, 512, [(x, D, _c0, False), (y, D, _c0, False)], _row_params(g), [(D, F32, _c0, D)])[0]


def _gate_bwd(gx, y, g):
    S = gx.shape[0]

    def fn(i, vals, p, o, a):
        o[0][...] = (vals[0] * p[0][...]).astype(BF16)
        a[0][...] += jnp.sum(vals[0] * vals[1], axis=0, keepdims=True)

    return _rowwise("gate_bwd", fn, S, 512, [(gx, D, _c0, False), (y, D, _c0, False)], _row_params(g),
                    [(D, BF16, _c0, D)], [(1, D, _c0, D)])


def _loss_head(x, tgt, fw):
    S = x.shape[0]

    def fn(i, vals, p, o, a):
        y, vjp = jax.vjp(_rms, vals[0], p[0][...])
        err = y - vals[1]
        gx, gfw = vjp(err * (1.0 / D))
        o[0][...] = gx
        a[0][...] += gfw
        part = 0.5 * jnp.sum(jnp.mean(err * err, axis=-1, keepdims=True), axis=0, keepdims=True)
        a[1][...] += jnp.broadcast_to(part, (1, LANES))

    return _rowwise("loss_head", fn, S, 256, [(x, D, _c0, False), (tgt, D, _c0, False)], _row_params(fw),
                    [(D, F32, _c0, D)], [(1, D, _c0, D), (1, LANES, _c0, LANES)])


CONV_HALO = 16
CONV_TS = 512


def _shifted(ext, shift, ts, halo):
    n = ext.shape[0]
    s = shift % n
    r = ext if s == 0 else pltpu.roll(ext, s, 0)
    return r[halo:halo + ts]


def _ln_silu(a, w, b):
    mu = jnp.mean(a, axis=-1, keepdims=True)
    var = jnp.mean(jnp.square(a - mu), axis=-1, keepdims=True)
    y = (a - mu) * lax.rsqrt(var + EPS) * w + b
    return y * jax.nn.sigmoid(y)


def _conv_a_fwd(proj, w_pad, b, lnw, lnb):
    S = proj.shape[0]
    ts, H = min(CONV_TS, S), CONV_HALO

    def fn(i, vals, p, o, a):
        a0 = vals[0] * jax.nn.sigmoid(vals[1])
        acc = jnp.zeros((ts, CONV_CH), F32) + p[1][...]
        for k in range(CONV_W):
            acc = acc + _shifted(a0, CONV_W // 2 - k, ts, H) * p[0][pl.ds(k, 1), :]
        o[0][...] = acc
        o[1][...] = _ln_silu(acc, p[2][...], p[3][...]).astype(BF16)

    tiles = [(proj, CONV_CH, lambda j: AV_BLK, True), (proj, CONV_CH, lambda j: AG_BLK, True)]
    params = [(w_pad, CONV_W_PAD, CONV_CH, _c0)] + _row_params(b, lnw, lnb)
    return _rowwise("conv_a_fwd", fn, S, ts, tiles, params, [(CONV_CH, F32, _c0, CONV_CH), (CONV_CH, BF16, _c0, CONV_CH)], halo=H)


def _conv_a_bwd(proj, a1, gmixed, w_pad, lnw, lnb):
    S = proj.shape[0]
    ts, H = min(CONV_TS, S), CONV_HALO

    def fn(i, vals, p, o, a):
        av, ag, a1e, ge = vals
        lw, lb = p[1][...], p[2][...]
        _, vjp_e = jax.vjp(lambda t: _ln_silu(t, lw, lb), a1e)
        (ga1e,) = vjp_e(ge)
        c = slice(H, H + ts)
        _, vjp_c = jax.vjp(_ln_silu, a1e[c], lw, lb)
        ga1, glw, glb = vjp_c(ge[c])
        a[1][...] += jnp.sum(ga1, axis=0, keepdims=True)
        a[2][...] += glw
        a[3][...] += glb
        sg = jax.nn.sigmoid(ag)
        a0 = av * sg
        ga0 = jnp.zeros((ts, CONV_CH), F32)
        for k in range(CONV_W):
            a[0][pl.ds(k, 1), :] += jnp.sum(ga1 * _shifted(a0, CONV_W // 2 - k, ts, H), axis=0, keepdims=True)
            ga0 = ga0 + _shifted(ga1e, k - CONV_W // 2, ts, H) * p[0][pl.ds(k, 1), :]
        sgc, avc = sg[c], av[c]
        o[0][...] = (ga0 * sgc).astype(BF16)
        o[1][...] = (ga0 * avc * sgc * (1.0 - sgc)).astype(BF16)

    tiles = [(proj, CONV_CH, lambda j: AV_BLK, True), (proj, CONV_CH, lambda j: AG_BLK, True),
             (a1, CONV_CH, _c0, True), (gmixed, CONV_CH, lambda j: 3, True)]
    params = [(w_pad, CONV_W_PAD, CONV_CH, _c0)] + _row_params(lnw, lnb)
    outs = [(CONV_CH, BF16, _c0, CONV_CH), (CONV_CH, BF16, _c0, CONV_CH)]
    accs = [(CONV_W_PAD, CONV_CH, _c0, CONV_CH)] + [(1, CONV_CH, _c0, CONV_CH)] * 3
    return _rowwise("conv_a_bwd", fn, S, ts, tiles, params, outs, accs, halo=H)


FFN_HALO = 8
FFN_TS = 512
FFN_CB = 256
FFN_NCB = D_FF // FFN_CB


def _gelu_mul(g, v):
    return 0.5 * g * (1.0 + lax.erf(g * (2.0 ** -0.5))) * v


def _ffn_mid_fwd(u, cw):
    S = u.shape[0]
    ts, H = min(FFN_TS, S), FFN_HALO

    def conv(ext, w_ref):
        acc = jnp.zeros((ts, FFN_CB), F32)
        for k in range(FFN_CONV_W):
            acc = acc + _shifted(ext, 1 - k, ts, H) * w_ref[pl.ds(k, 1), :]
        return acc

    def fn(i, vals, p, o, a):
        o[0][...] = _gelu_mul(conv(vals[0], p[0]), conv(vals[1], p[1])).astype(BF16)

    tiles = [(u, FFN_CB, lambda j: j, True), (u, FFN_CB, lambda j: j + FFN_NCB, True)]
    params = [(cw, FFN_CONV_W, FFN_CB, lambda j: j), (cw, FFN_CONV_W, FFN_CB, lambda j: j + FFN_NCB)]
    return _rowwise("ffn_mid_fwd", fn, S, ts, tiles, params, [(FFN_CB, BF16, lambda j: j, D_FF)], halo=H, ncb=FFN_NCB)[0]


def _ffn_mid_bwd(u, gact, cw):
    S = u.shape[0]
    ts, H = min(FFN_TS, S), FFN_HALO
    n = ts + 2 * H

    def conv_all(ext, w_ref):
        acc = jnp.zeros((n, FFN_CB), F32)
        for k in range(FFN_CONV_W):
            s = (1 - k) % n
            acc = acc + (ext if s == 0 else pltpu.roll(ext, s, 0)) * w_ref[pl.ds(k, 1), :]
        return acc

    def fn(i, vals, p, o, a):
        ug, uv, ga = vals
        _, vjp = jax.vjp(_gelu_mul, conv_all(ug, p[0]), conv_all(uv, p[1]))
        gcg, gcv = vjp(ga)
        c = slice(H, H + ts)
        for half, (gc, ue) in enumerate(((gcg, ug), (gcv, uv))):
            gu = jnp.zeros((ts, FFN_CB), F32)
            for k in range(FFN_CONV_W):
                gu = gu + _shifted(gc, k - 1, ts, H) * p[half][pl.ds(k, 1), :]
                a[half][pl.ds(k, 1), :] += jnp.sum(gc[c] * _shifted(ue, 1 - k, ts, H), axis=0, keepdims=True)
            o[half][...] = gu.astype(BF16)

    tiles = [(u, FFN_CB, lambda j: j, True), (u, FFN_CB, lambda j: j + FFN_NCB, True), (gact, FFN_CB, lambda j: j, True)]
    params = [(cw, FFN_CONV_W, FFN_CB, lambda j: j), (cw, FFN_CONV_W, FFN_CB, lambda j: j + FFN_NCB)]
    outs = [(FFN_CB, BF16, lambda j: j, D_FF)] * 2
    accs = [(FFN_CONV_W, FFN_CB, lambda j: j, D_FF)] * 2
    return _rowwise("ffn_mid_bwd", fn, S, ts, tiles, params, outs, accs, halo=H, ncb=FFN_NCB)


LD_W = 16


def _to_sub(t, d):
    S, C = t.shape
    return t.reshape(S // d, d, C).transpose(1, 0, 2)


def _from_sub(t):
    d, L, C = t.shape
    return t.transpose(1, 0, 2).reshape(d * L, C)


def _halo_specs(width, col, blk, hb, nhb):
    per = blk // hb
    return [
        pl.BlockSpec((None, hb, width), lambda r, i: (r, jnp.maximum(i * per - 1, 0), col)),
        pl.BlockSpec((None, blk, width), lambda r, i: (r, i, col)),
        pl.BlockSpec((None, hb, width), lambda r, i: (r, jnp.minimum((i + 1) * per, nhb - 1), col)),
    ]


def _head_lane_mask(h):
    return (_iota((1, ATT_W), 1) >> HEAD_SHIFT) == h


def _pick_lane(t, lane):
    return jnp.sum(jnp.where(_iota((1, t.shape[1]), 1) == lane, t, 0.0), axis=1, keepdims=True)


def _attn_fwd(qkv, dil):
    d, L, _ = qkv.shape
    blk, hb = min(ATT_BLK, L), HALF_BAND
    span = blk + 2 * hb

    def body(q_ref, kp, kc, kn, vp, vc, vn, o_ref, l_ref):
        i = pl.program_id(1)
        q = q_ref[...]
        k = jnp.concatenate([kp[...], kc[...], kn[...]], axis=0)
        v = jnp.concatenate([vp[...], vc[...], vn[...]], axis=0)
        rel = _iota((blk, span), 1) - hb - _iota((blk, span), 0)
        kpos = i * blk - hb + _iota((blk, span), 1)
        valid = (jnp.abs(rel) <= hb) & (kpos >= 0) & (kpos < L)
        dist = jnp.abs(rel).astype(F32) * float(dil)
        o = jnp.zeros((blk, ATT_W), F32)
        lse = jnp.zeros((blk, 8), F32)
        for h in range(N_HEADS):
            mh = _head_lane_mask(h)
            s = _dot_nt(jnp.where(mh, q, jnp.zeros_like(q)), k) * (HEAD ** -0.5) - ALIBI_SLOPES[h] * dist
            s = jnp.where(valid, s, MASK_VALUE)
            m = jnp.max(s, axis=1, keepdims=True)
            p = jnp.exp(s - m)
            l = jnp.sum(p, axis=1, keepdims=True)
            o = o + jnp.where(mh, _dot_nn(p.astype(BF16), v) / l, 0.0)
            lse = lse + jnp.where(_iota((1, 8), 1) == h, m + jnp.log(l), 0.0)
        o_ref[...] = o
        l_ref[...] = lse

    nhb = L // hb
    in_specs = ([pl.BlockSpec((None, blk, ATT_W), lambda r, i: (r, i, 0))]
                + _halo_specs(ATT_W, 1, blk, hb, nhb) + _halo_specs(ATT_W, 2, blk, hb, nhb))
    return pl.pallas_call(
        body, name=f"attn_fwd_d{dil}", grid=(d, L // blk), in_specs=in_specs,
        out_specs=[pl.BlockSpec((None, blk, ATT_W), lambda r, i: (r, i, 0)), pl.BlockSpec((None, blk, 8), lambda r, i: (r, i, 0))],
        out_shape=[jax.ShapeDtypeStruct((d, L, ATT_W), F32), jax.ShapeDtypeStruct((d, L, 8), F32)],
        compiler_params=_cparams(("parallel", "parallel")),
    )(*([qkv] * 7))


def _attn_bwd(qkv, do, ld, dil):
    d, L, _ = qkv.shape
    blk, hb = min(ATT_BLK, L), HALF_BAND
    span = blk + 2 * hb
    scale = HEAD ** -0.5

    def body(qp, qc, qn, kp, kc, kn, vp, vc, vn, gp, gc, gn, lp, lc, ln, dq_ref, dk_ref, dv_ref):
        i = pl.program_id(1)
        cat = lambda a, b, c: jnp.concatenate([a[...], b[...], c[...]], axis=0)
        q, k, v, g, l = qc[...], kc[...], vc[...], gc[...], lc[...]
        qe, ke, ve, ge, le = cat(qp, qc, qn), cat(kp, kc, kn), cat(vp, vc, vn), cat(gp, gc, gn), cat(lp, lc, ln)
        rel_q = _iota((blk, span), 1) - hb - _iota((blk, span), 0)
        kpos = i * blk - hb + _iota((blk, span), 1)
        valid_q = (jnp.abs(rel_q) <= hb) & (kpos >= 0) & (kpos < L)
        dist_q = jnp.abs(rel_q).astype(F32) * float(dil)
        rel_k = _iota((span, blk), 1) + hb - _iota((span, blk), 0)
        qpos = i * blk - hb + _iota((span, blk), 0)
        valid_k = (jnp.abs(rel_k) <= hb) & (qpos >= 0) & (qpos < L)
        dist_k = jnp.abs(rel_k).astype(F32) * float(dil)
        dq = jnp.zeros((blk, ATT_W), F32)
        dk = jnp.zeros((blk, ATT_W), F32)
        dv = jnp.zeros((blk, ATT_W), F32)
        for h in range(N_HEADS):
            mh = _head_lane_mask(h)
            zero = lambda t: jnp.where(mh, t, jnp.zeros_like(t))
            s = _dot_nt(zero(q), ke) * scale - ALIBI_SLOPES[h] * dist_q
            p = jnp.where(valid_q, jnp.exp(s - _pick_lane(l, h)), 0.0)
            ds = p * (_dot_nt(zero(g), ve) - _pick_lane(l, 8 + h))
            dq = dq + jnp.where(mh, _dot_nn(ds.astype(BF16), ke), 0.0)
            s = _dot_nt(zero(qe), k) * scale - ALIBI_SLOPES[h] * dist_k
            p = jnp.where(valid_k, jnp.exp(s - _pick_lane(le, h)), 0.0)
            dv = dv + jnp.where(mh, _dot_tn(p.astype(BF16), ge), 0.0)
            ds = p * (_dot_nt(zero(ge), v) - _pick_lane(le, 8 + h))
            dk = dk + jnp.where(mh, _dot_tn(ds.astype(BF16), qe), 0.0)
        dq_ref[...] = dq * scale
        dk_ref[...] = dk * scale
        dv_ref[...] = dv

    nhb = L // hb
    in_specs = (_halo_specs(ATT_W, 0, blk, hb, nhb) + _halo_specs(ATT_W, 1, blk, hb, nhb) + _halo_specs(ATT_W, 2, blk, hb, nhb)
                + _halo_specs(ATT_W, 0, blk, hb, nhb) + _halo_specs(LD_W, 0, blk, hb, nhb))
    o_spec = pl.BlockSpec((None, blk, ATT_W), lambda r, i: (r, i, 0))
    return pl.pallas_call(
        body, name=f"attn_bwd_d{dil}", grid=(d, L // blk), in_specs=in_specs,
        out_specs=[o_spec] * 3, out_shape=[jax.ShapeDtypeStruct((d, L, ATT_W), F32)] * 3,
        compiler_params=_cparams(("parallel", "parallel")),
    )(*([qkv] * 9 + [do] * 3 + [ld] * 3))


def _head_expand(t8):
    e = ((_iota((8, ATT_W), 1) >> HEAD_SHIFT) == _iota((8, ATT_W), 0)).astype(F32)
    return _dot_nn(t8, e, HP)


def _attn_merge(os, ls):
    S = os[0].shape[0]

    def fn(i, vals, p, o, a):
        o3, l3 = vals[:3], vals[3:]
        m = jnp.maximum(jnp.maximum(l3[0], l3[1]), l3[2])
        e3 = [jnp.exp(l - m) for l in l3]
        den = e3[0] + e3[1] + e3[2]
        out = jnp.zeros((o3[0].shape[0], ATT_W), F32)
        for ob, e in zip(o3, e3):
            out = out + _head_expand(e / den) * ob
        o[0][...] = out
        o[1][...] = m + jnp.log(den)

    tiles = [(t, ATT_W, _c0, False) for t in os] + [(t, 8, _c0, False) for t in ls]
    return _rowwise("attn_merge", fn, S, 512, tiles, (), [(ATT_W, F32, _c0, ATT_W), (8, F32, _c0, 8)])


def _attn_bwd_prep(gmixed, att, lse):
    S = att.shape[0]

    def fn(i, vals, p, o, a):
        g, out, l8 = vals
        place_l = (_iota((8, LD_W), 1) == _iota((8, LD_W), 0)).astype(F32)
        place_d = ((_iota((ATT_W, LD_W), 0) >> HEAD_SHIFT) + 8 == _iota((ATT_W, LD_W), 1)).astype(F32)
        o[0][...] = g.astype(BF16)
        o[1][...] = _dot_nn(l8, place_l, HP) + _dot_nn(g * out, place_d, HP)

    tiles = [(gmixed, ATT_W, _c0, False), (att, ATT_W, _c0, False), (lse, 8, _c0, False)]
    return _rowwise("attn_bwd_prep", fn, S, 512, tiles, (), [(ATT_W, BF16, _c0, ATT_W), (LD_W, F32, _c0, LD_W)])


def _sum3_bf16(ts_, S, width):
    def fn(i, vals, p, o, a):
        o[0][...] = (vals[0] + vals[1] + vals[2]).astype(BF16)

    return _rowwise("sum3", fn, S, 512, [(t, width, _c0, False) for t in ts_], (), [(width, BF16, _c0, width)])[0]


def _block_diag_mask():
    return ((_iota((REC_W, REC_W), 0) >> HEAD_SHIFT) == (_iota((REC_W, REC_W), 1) >> HEAD_SHIFT)).astype(F32)


def _rep_heads(t):
    return jnp.concatenate([t] * N_HEADS, axis=0)


def _hgrn_chunk(qr, z, iv, lb, st, reverse):
    C = REC_CHUNK
    r, c = _iota((C, C), 0), _iota((C, C), 1)
    if reverse:
        t_cum, t_mid = (c >= r), (c >= C // 2)
    else:
        t_cum, t_mid = (c <= r), (c < C // 2)
    f = lb + (1.0 - lb) * jax.nn.sigmoid(z)
    logf = jnp.log(jnp.maximum(f, F_TINY))
    k = (1.0 - lb) * jax.nn.sigmoid(-z)
    q = qr * jax.nn.sigmoid(qr)
    b = _dot_nn(t_cum.astype(F32), logf, HP)
    bm = _dot_nn(t_mid.astype(F32), logf, HP)
    bl = _dot_nn(jnp.ones((C, C), F32), logf, HP)
    qt = q * jnp.exp(jnp.minimum(b - bm, EXP_CLAMP))
    kt = k * jnp.exp(jnp.minimum(bm - b, EXP_CLAMP))
    qh = q * jnp.exp(b)
    kh = k * jnp.exp(bl - b)
    bd = _block_diag_mask()
    k_bd = (_rep_heads(kt) * bd).astype(BF16)
    v_bd = (_rep_heads(iv) * bd).astype(BF16)
    st_bd = _rep_heads(st) * bd
    a = _dot_nt(qt.astype(BF16), k_bd)
    s_in = _iota((C, REC_W), 1) & (HEAD - 1)
    t_in = _iota((C, REC_W), 0)
    a = jnp.where((s_in >= t_in) if reverse else (s_in <= t_in), a, 0.0)
    o = _dot_nn(a.astype(BF16), v_bd) + _dot_nt(qh.astype(BF16), st_bd.astype(BF16))
    kv = _dot_tn(iv.astype(BF16), kh.astype(BF16))
    st_bd = st_bd * _rep_heads(jnp.exp(bl)) + kv * bd
    st_new = st_bd[0:HEAD]
    for h in range(1, N_HEADS):
        st_new = st_new + st_bd[h * HEAD:(h + 1) * HEAD]
    return o, st_new


def _hgrn_specs(n, order, blocks):
    return [pl.BlockSpec((REC_CHUNK, REC_W), lambda i, b=b: (order(i), b)) for b in blocks]


def _hgrn_fwd(proj, lb, z_blk, reverse):
    S = proj.shape[0]
    n = S // REC_CHUNK
    order = (lambda i: n - 1 - i) if reverse else (lambda i: i)

    def body(q_ref, z_ref, v_ref, lb_ref, o_ref, st_ref, st_scr):
        @pl.when(pl.program_id(0) == 0)
        def _():
            st_scr[...] = jnp.zeros_like(st_scr)

        st = st_scr[...]
        st_ref[...] = st
        o, st_new = _hgrn_chunk(q_ref[...], z_ref[...], v_ref[...], lb_ref[...], st, reverse)
        o_ref[...] = o
        st_scr[...] = st_new

    return pl.pallas_call(
        body, name="hgrn_rev_fwd" if reverse else "hgrn_fwd_fwd", grid=(n,),
        in_specs=_hgrn_specs(n, order, (QR, z_blk, IR)) + [pl.BlockSpec((1, REC_W), lambda i: (0, 0))],
        out_specs=[pl.BlockSpec((REC_CHUNK, REC_W), lambda i: (order(i), 0)),
                   pl.BlockSpec((None, HEAD, REC_W), lambda i: (order(i), 0, 0))],
        out_shape=[jax.ShapeDtypeStruct((S, REC_W), F32), jax.ShapeDtypeStruct((n, HEAD, REC_W), F32)],
        scratch_shapes=[pltpu.VMEM((HEAD, REC_W), F32)],
        compiler_params=_cparams(("arbitrary",)),
    )(proj, proj, proj, lb)


def _hgrn_bwd(proj, lb, states, go, z_blk, reverse):
    S = proj.shape[0]
    n = S // REC_CHUNK
    order = (lambda i: i) if reverse else (lambda i: n - 1 - i)

    def body(q_ref, z_ref, v_ref, lb_ref, st_ref, go_ref, gq_ref, gz_ref, gv_ref, glb_ref, gst_scr):
        @pl.when(pl.program_id(0) == 0)
        def _():
            gst_scr[...] = jnp.zeros_like(gst_scr)
            glb_ref[...] = jnp.zeros_like(glb_ref)

        chunk = functools.partial(_hgrn_chunk, reverse=reverse)
        _, vjp = jax.vjp(chunk, q_ref[...], z_ref[...], v_ref[...], lb_ref[...], st_ref[...])
        gq, gz, gv, glb, gst = vjp((go_ref[...], gst_scr[...]))
        gq_ref[...] = gq
        gz_ref[...] = gz
        gv_ref[...] = gv
        glb_ref[...] += glb
        gst_scr[...] = gst

    row_spec = pl.BlockSpec((REC_CHUNK, REC_W), lambda i: (order(i), 0))
    return pl.pallas_call(
        body, name="hgrn_rev_bwd" if reverse else "hgrn_fwd_bwd", grid=(n,),
        in_specs=(_hgrn_specs(n, order, (QR, z_blk, IR)) + [pl.BlockSpec((1, REC_W), lambda i: (0, 0))]
                  + [pl.BlockSpec((None, HEAD, REC_W), lambda i: (order(i), 0, 0)), row_spec]),
        out_specs=[row_spec] * 3 + [pl.BlockSpec((1, REC_W), lambda i: (0, 0))],
        out_shape=[jax.ShapeDtypeStruct((S, REC_W), F32)] * 3 + [jax.ShapeDtypeStruct((1, REC_W), F32)],
        scratch_shapes=[pltpu.VMEM((HEAD, REC_W), F32)],
        compiler_params=_cparams(("arbitrary",)),
    )(proj, proj, proj, lb, states, go)


def _hgrn_post_f(of, ob, gr, rnw):
    o = of + ob
    ms = _dot_nn(o * o, _block_diag_mask() * (1.0 / HEAD), HP)
    return o * lax.rsqrt(ms + EPS) * rnw * (gr * jax.nn.sigmoid(gr))


def _hgrn_post_fwd(of, ob, proj, rnw):
    S = of.shape[0]

    def fn(i, vals, p, o, a):
        o[0][...] = _hgrn_post_f(vals[0], vals[1], vals[2], p[0][...]).astype(BF16)

    tiles = [(of, REC_W, _c0, False), (ob, REC_W, _c0, False), (proj, REC_W, lambda j: GR, False)]
    return _rowwise("hgrn_post_fwd", fn, S, 512, tiles, _row_params(rnw), [(REC_W, BF16, _c0, REC_W)])[0]


def _hgrn_post_bwd(of, ob, proj, gmixed, rnw):
    S = of.shape[0]

    def fn(i, vals, p, o, a):
        _, vjp = jax.vjp(_hgrn_post_f, vals[0], vals[1], vals[2], p[0][...])
        go, _, ggr, grnw = vjp(vals[3])
        o[0][...] = go
        o[1][...] = ggr
        a[0][...] += grnw

    tiles = [(of, REC_W, _c0, False), (ob, REC_W, _c0, False), (proj, REC_W, lambda j: GR, False),
             (gmixed, REC_W, lambda j: 1, False)]
    return _rowwise("hgrn_post_bwd", fn, S, 256, tiles, _row_params(rnw),
                    [(REC_W, F32, _c0, REC_W), (REC_W, F32, _c0, REC_W)], [(1, REC_W, _c0, REC_W)])


def _lower_bounds_f(g0, g1):
    m = jnp.maximum(g0, g1)
    e0, e1 = jnp.exp(g0 - m), jnp.exp(g1 - m)
    return e1 / (e0 + e1)


def _adamw(name, w, m, v, gparts):
    R, C = w.shape
    P = gparts.shape[0]
    tr = R if R * C * 4 * (P + 7) * 2 <= VMEM_LIMIT_BYTES // 2 else _pick(R, (256, 128, 64, 32, 16, 8))

    def body(w_ref, m_ref, v_ref, gp_ref, g_ref, d_ref, nm_ref, nv_ref):
        g = gp_ref[0]
        for p in range(1, P):
            g = g + gp_ref[p]
        w_ = w_ref[...]
        nm = ADAM_B1 * m_ref[...] + (1.0 - ADAM_B1) * g
        nv = ADAM_B2 * v_ref[...] + (1.0 - ADAM_B2) * jnp.square(g)
        m_hat = nm / (1.0 - ADAM_B1 ** ADAM_STEP)
        v_hat = nv / (1.0 - ADAM_B2 ** ADAM_STEP)
        g_ref[...] = g
        d_ref[...] = -ADAM_LR * (m_hat / (jnp.sqrt(v_hat) + ADAM_EPS) + ADAM_WD * w_)
        nm_ref[...] = nm
        nv_ref[...] = nv

    spec = pl.BlockSpec((tr, C), lambda i: (i, 0))
    return pl.pallas_call(
        body, name=name, grid=(R // tr,),
        in_specs=[spec, spec, spec, pl.BlockSpec((P, tr, C), lambda i: (0, i, 0))],
        out_specs=[spec] * 4, out_shape=[jax.ShapeDtypeStruct((R, C), F32)] * 4,
        compiler_params=_cparams(("parallel",)),
    )(w, m, v, gparts)


def _place():
    return lax.axis_index("x"), lax.axis_index("y"), lax.axis_index("c")


def _index_of(p):
    return 4 * p[0] + 2 * p[1] + p[2]


def _allgather_small(name, rows):
    m_per, n = rows.shape

    def body(x_ref, out_ref, send_sems, recv_sems, local_sem):
        x, y, c = _place()
        me, sibling = (x, y, c), (x, y, 1 - c)
        chips = [(1 - x, y), (x, 1 - y), (1 - x, 1 - y)]

        def blk(p):
            return out_ref.at[pl.ds(_index_of(p) * m_per, m_per), :]

        def copy(k, block, to, src=None):
            return pltpu.make_async_remote_copy(
                src_ref=blk(block) if src is None else src, dst_ref=blk(block),
                send_sem=send_sems.at[k], recv_sem=recv_sems.at[k], device_id=to, device_id_type=MESH)

        mine = pltpu.make_async_copy(x_ref, blk(me), local_sem)
        mine.start()
        first = [copy(0, me, sibling, src=x_ref)]
        first += [copy(1 + j, me, (*chip, c), src=x_ref) for j, chip in enumerate(chips)]
        for cp in first:
            cp.start()
        passed = [copy(4 + j, (*chip, c), sibling) for j, chip in enumerate(chips)]
        for j, chip in enumerate(chips):
            copy(1 + j, (*chip, c), me).wait_recv()
            passed[j].start()
        copy(0, sibling, me).wait_recv()
        for j, chip in enumerate(chips):
            copy(4 + j, (*chip, 1 - c), me).wait_recv()
        for cp in first + passed:
            cp.wait_send()
        mine.wait()

    return pl.pallas_call(
        body, name=name,
        out_shape=jax.ShapeDtypeStruct((N_DEV * m_per, n), rows.dtype),
        in_specs=[pl.BlockSpec(memory_space=pltpu.VMEM)],
        out_specs=pl.BlockSpec(memory_space=pltpu.VMEM),
        scratch_shapes=[pltpu.SemaphoreType.DMA((7,)), pltpu.SemaphoreType.DMA((7,)), pltpu.SemaphoreType.DMA],
        compiler_params=_cparams(),
    )(rows)


def _allgather_big(name, arrs):
    na = len(arrs)

    def body(*refs):
        ins, outs = refs[:na], refs[na:2 * na]
        send_sems, recv_sems, local_sems = refs[2 * na:]
        x, y, c = _place()
        me, sibling = (x, y, c), (x, y, 1 - c)
        chips = [(1 - x, y), (x, 1 - y), (1 - x, 1 - y)]

        def copy(a, k, block, to, src=None):
            dst = outs[a].at[_index_of(block)]
            return pltpu.make_async_remote_copy(
                src_ref=dst if src is None else src, dst_ref=dst,
                send_sem=send_sems.at[a, k], recv_sem=recv_sems.at[a, k], device_id=to, device_id_type=MESH)

        mine = [pltpu.make_async_copy(ins[a], outs[a].at[_index_of(me)], local_sems.at[a]) for a in range(na)]
        for cp in mine:
            cp.start()
        sent = []
        for a in range(na):
            sent.append(copy(a, 0, me, sibling, src=ins[a]))
            sent += [copy(a, 1 + j, me, (*chip, c), src=ins[a]) for j, chip in enumerate(chips)]
        for cp in sent:
            cp.start()
        for j, chip in enumerate(chips):
            for a in range(na):
                copy(a, 1 + j, (*chip, c), me).wait_recv()
                fwd = copy(a, 4 + j, (*chip, c), sibling)
                fwd.start()
                sent.append(fwd)
        for a in range(na):
            copy(a, 0, sibling, me).wait_recv()
            for j, chip in enumerate(chips):
                copy(a, 4 + j, (*chip, 1 - c), me).wait_recv()
        for cp in sent:
            cp.wait_send()
        for cp in mine:
            cp.wait()

    any_spec = pl.BlockSpec(memory_space=pl.ANY)
    return pl.pallas_call(
        body, name=name,
        out_shape=[jax.ShapeDtypeStruct((N_DEV,) + a.shape, a.dtype) for a in arrs],
        in_specs=[any_spec] * na, out_specs=[any_spec] * na,
        scratch_shapes=[pltpu.SemaphoreType.DMA((na, 7)), pltpu.SemaphoreType.DMA((na, 7)), pltpu.SemaphoreType.DMA((na,))],
        compiler_params=_cparams(),
    )(*arrs)


def _scatter_parts(name, parts):
    na = len(parts)

    def body(*refs):
        ins, outs = refs[:na], refs[na:2 * na]
        send_sems, recv_sems, local_sems = refs[2 * na:]
        x, y, c = _place()
        me = _index_of((x, y, c))
        flips = [(k >> 2 & 1, k >> 1 & 1, k & 1) for k in range(1, N_DEV)]
        peers = [(1 - x if fx else x, 1 - y if fy else y, 1 - c if fc else c) for fx, fy, fc in flips]
        mine = [pltpu.make_async_copy(ins[a].at[me], outs[a].at[me], local_sems.at[a]) for a in range(na)]
        for cp in mine:
            cp.start()
        sent = []
        for a in range(na):
            for k, peer in enumerate(peers):
                sent.append(pltpu.make_async_remote_copy(
                    src_ref=ins[a].at[_index_of(peer)], dst_ref=outs[a].at[me],
                    send_sem=send_sems.at[a, k], recv_sem=recv_sems.at[a, k], device_id=peer, device_id_type=MESH))
        for cp in sent:
            cp.start()
        for a in range(na):
            for k, peer in enumerate(peers):
                slot = outs[a].at[_index_of(peer)]
                pltpu.make_async_remote_copy(
                    src_ref=slot, dst_ref=slot, send_sem=send_sems.at[a, k], recv_sem=recv_sems.at[a, k],
                    device_id=peer, device_id_type=MESH).wait_recv()
        for cp in sent:
            cp.wait_send()
        for cp in mine:
            cp.wait()

    any_spec = pl.BlockSpec(memory_space=pl.ANY)
    return pl.pallas_call(
        body, name=name,
        out_shape=[jax.ShapeDtypeStruct(p.shape, p.dtype) for p in parts],
        in_specs=[any_spec] * na, out_specs=[any_spec] * na,
        scratch_shapes=[pltpu.SemaphoreType.DMA((na, 7)), pltpu.SemaphoreType.DMA((na, 7)), pltpu.SemaphoreType.DMA((na,))],
        compiler_params=_cparams(),
    )(*parts)


def _gather_row(name, vec, width):
    n = vec.shape[0]
    rows = jnp.pad(vec, (0, width - n)).reshape(SUBLANES_F32, width // SUBLANES_F32)
    return _allgather_small(name, rows).reshape(N_DEV, width)[:, :n]


def _layer_fwd(x, mod, w):
    sh1, sc1, g1, sh2, sc2, g2 = [mod[i:i + 1] for i in range(N_MOD)]
    S = x.shape[0]
    h1 = _normmod_fwd(x, w["norm1_w"], sc1, sh1)
    proj = _matmul("proj_in", h1, w["w_in"], "nn")
    a1, a_out = _conv_a_fwd(proj, w["conv_a_w"], w["conv_a_b"], w["ln_a_w"], w["ln_a_b"])
    qkv = proj[:, :3 * ATT_W].astype(BF16)
    os, ls = [], []
    for dil in DILATIONS:
        o_b, l_b = _attn_fwd(_to_sub(qkv, dil), dil)
        os.append(_from_sub(o_b))
        ls.append(_from_sub(l_b))
    att, lse = _attn_merge(os, ls)
    of, st_f = _hgrn_fwd(proj, w["lb_f"], ZF, False)
    ob, st_b = _hgrn_fwd(proj, w["lb_b"], ZB, True)
    rec = _hgrn_post_fwd(of, ob, proj, w["rec_norm_w"])
    mixed = jnp.concatenate([att.astype(BF16), rec, a_out], axis=1)
    y1 = _matmul("proj_out", mixed, w["w_out"], "nn")
    x2 = _gate_add(x, y1, g1)
    h2 = _normmod_fwd(x2, w["norm2_w"], sc2, sh2)
    u = _matmul("ffn_up", h2, w["w_up"], "nn")
    act = _ffn_mid_fwd(u, w["conv_f_w"])
    y2 = _matmul("ffn_down", act, w["w_down"], "nn")
    x3 = _gate_add(x2, y2, g2)
    saved = dict(x=x, h1=h1, proj=proj, a1=a1, qkv=qkv, att=att, lse=lse, of=of, ob=ob, st_f=st_f, st_b=st_b,
                 mixed=mixed, y1=y1, x2=x2, h2=h2, u=u, act=act, y2=y2)
    return x3, saved


def _layer_bwd(gx3, mod, w, s):
    sh1, sc1, g1, sh2, sc2, g2 = [mod[i:i + 1] for i in range(N_MOD)]
    S = gx3.shape[0]
    g = {}
    gy2, gg2 = _gate_bwd(gx3, s["y2"], g2)
    gact = _matmul("ffn_down_dx", gy2, w["w_down"], "nt")
    g["w_down"] = _matmul("ffn_down_dw", s["act"], gy2, "tn")
    gu_g, gu_v, gcw_g, gcw_v = _ffn_mid_bwd(s["u"], gact, w["conv_f_w"])
    gu = jnp.concatenate([gu_g, gu_v], axis=1)
    g["conv_f_w"] = jnp.concatenate([gcw_g, gcw_v], axis=1)
    gh2 = _matmul("ffn_up_dx", gu, w["w_up"], "nt")
    g["w_up"] = _matmul("ffn_up_dw", s["h2"], gu, "tn")
    gx2, g["norm2_w"], gsc2, gsh2 = _normmod_bwd(s["x2"], gh2, gx3, w["norm2_w"], sc2, sh2)
    gy1, gg1 = _gate_bwd(gx2, s["y1"], g1)
    gmixed = _matmul("proj_out_dx", gy1, w["w_out"], "nt")
    g["w_out"] = _matmul("proj_out_dw", s["mixed"], gy1, "tn")
    go, ggr, g["rec_norm_w"] = _hgrn_post_bwd(s["of"], s["ob"], s["proj"], gmixed, w["rec_norm_w"])
    gq_f, gz_f, gv_f, g["lb_f"] = _hgrn_bwd(s["proj"], w["lb_f"], s["st_f"], go, ZF, False)
    gq_b, gz_b, gv_b, g["lb_b"] = _hgrn_bwd(s["proj"], w["lb_b"], s["st_b"], go, ZB, True)
    do, ld = _attn_bwd_prep(gmixed, s["att"], s["lse"])
    gqkv = [[], [], []]
    for dil in DILATIONS:
        grads = _attn_bwd(_to_sub(s["qkv"], dil), _to_sub(do, dil), _to_sub(ld, dil), dil)
        for lst, t in zip(gqkv, grads):
            lst.append(_from_sub(t))
    gq_a, gk_a, gv_a = [_sum3_bf16(lst, S, ATT_W) for lst in gqkv]
    gav, gag, gcw, g["conv_a_b"], g["ln_a_w"], g["ln_a_b"] = _conv_a_bwd(
        s["proj"], s["a1"], gmixed, w["conv_a_w"], w["ln_a_w"], w["ln_a_b"])
    g["conv_a_w"] = gcw[:CONV_W]
    gproj = jnp.concatenate([gq_a, gk_a, gv_a, (gq_f + gq_b).astype(BF16), gz_f.astype(BF16), gz_b.astype(BF16),
                             (gv_f + gv_b).astype(BF16), ggr.astype(BF16), gav, gag], axis=1)
    gh1 = _matmul("proj_in_dx", gproj, w["w_in"], "nt")
    g["w_in"] = _matmul("proj_in_dw", s["h1"], gproj, "tn")
    gx, g["norm1_w"], gsc1, gsh1 = _normmod_bwd(s["x"], gh1, gx2, w["norm1_w"], sc1, sh1)
    gmod = jnp.concatenate([gsh1, gsc1, gg1, gsh2, gsc2, gg2], axis=0)
    return gx, gmod, g


def _permute_in_cols(t):
    return jnp.concatenate([t[..., CONV_COLS:], t[..., :CONV_COLS]], axis=-1)


def _unpermute_in_cols(t):
    return jnp.concatenate([t[..., IN_COLS - CONV_COLS:], t[..., :IN_COLS - CONV_COLS]], axis=-1)


def _cols_from_gathered(t, lead):
    nd = t.ndim
    perm = tuple(range(1, nd - 1)) + (0, nd - 1)
    t = t.transpose(perm)
    return t.reshape(t.shape[:-2] + (t.shape[-2] * t.shape[-1],))


def _cols_to_parts(t):
    L, R, C = t.shape
    return t.reshape(L * R, N_DEV, C // N_DEV).transpose(1, 0, 2)


SMALL_REPL = (("norm1_w", D), ("conv_a_b", CONV_CH), ("ln_a_w", CONV_CH), ("ln_a_b", CONV_CH),
              ("rec_norm_w", REC_W), ("norm2_w", D))


def kernel(x, c, w_ada, b_ada, norm1_w, w_in, conv_a_w, conv_a_b, ln_a_w, ln_a_b, lb_gamma, rec_norm_w, w_out, norm2_w, w_up, conv_f_w, w_down, final_norm_w, loss_target, m_w_ada, m_b_ada, m_norm1_w, m_w_in, m_conv_a_w, m_conv_a_b, m_ln_a_w, m_ln_a_b, m_lb_gamma, m_rec_norm_w, m_w_out, m_norm2_w, m_w_up, m_conv_f_w, m_w_down, m_final_norm_w, v_w_ada, v_b_ada, v_norm1_w, v_w_in, v_conv_a_w, v_conv_a_b, v_ln_a_w, v_ln_a_b, v_lb_gamma, v_rec_norm_w, v_w_out, v_norm2_w, v_w_up, v_conv_f_w, v_w_down, v_final_norm_w):
    px, py, pc = _place()
    me = _index_of((px, py, pc))
    xs, tgt = x[0], loss_target[0]
    S = xs.shape[0]
    ada_cols = w_ada.shape[2]

    big = [w_in.reshape(DEPTH * D, -1), w_up.reshape(DEPTH * D, -1), w_out.reshape(-1, D), w_down.reshape(-1, D)]
    g_in, g_up, g_out, g_down = _allgather_big("gather_weights", [t.astype(BF16) for t in big])
    w_in_f = _permute_in_cols(_cols_from_gathered(g_in.reshape(N_DEV, DEPTH, D, -1), 1))
    w_up_f = _cols_from_gathered(g_up.reshape(N_DEV, DEPTH, D, -1), 1)
    w_out_f = g_out.reshape(N_DEV, DEPTH, D // N_DEV, D).transpose(1, 0, 2, 3).reshape(DEPTH, D, D)
    w_out_f = jnp.concatenate([w_out_f[:, CONV_CH:], w_out_f[:, :CONV_CH]], axis=1)
    w_down_f = g_down.reshape(N_DEV, DEPTH, D_FF // N_DEV, D).transpose(1, 0, 2, 3).reshape(DEPTH, D_FF, D)

    small_in = jnp.concatenate([c.reshape(-1), conv_a_w.reshape(-1), lb_gamma.reshape(-1), conv_f_w.reshape(-1)])
    gs = _gather_row("gather_small", small_in, 8192)
    o1 = D
    o2 = o1 + conv_a_w.size
    o3 = o2 + lb_gamma.size
    c_all = gs[:, :o1]
    conv_a_f = _cols_from_gathered(gs[:, o1:o2].reshape(N_DEV, DEPTH, CONV_W, -1), 1)
    lb_gamma_f = _cols_from_gathered(gs[:, o2:o3].reshape(N_DEV, DEPTH, 2, -1), 1)
    conv_f_f = _cols_from_gathered(gs[:, o3:].reshape(N_DEV, DEPTH, FFN_CONV_W, -1), 1)
    conv_a_pad = jnp.pad(conv_a_f, ((0, 0), (0, CONV_W_PAD - CONV_W), (0, 0)))

    b_loc = lax.dynamic_slice_in_dim(b_ada, me * ada_cols, ada_cols, axis=1)

    def mod_fn(c_all_, w_, b_):
        cond = c_all_ * jax.nn.sigmoid(c_all_)
        return (jnp.concatenate([_dot_nn(cond, w_[l], HP) + b_[l] for l in range(DEPTH)], axis=1),)

    (mod_part,) = _vmem_call("ada_mod", mod_fn, [c_all, w_ada, b_loc[:, None, :]], [((N_DEV, DEPTH * ada_cols), F32)])
    gm = _allgather_small("gather_mod", mod_part).reshape(N_DEV, N_DEV, DEPTH, ada_cols)
    mod = lax.dynamic_index_in_dim(gm, me, axis=1, keepdims=False)
    mod = mod.transpose(1, 0, 2).reshape(DEPTH, N_MOD, D)

    (lb1,) = _vmem_call("lower_bounds", lambda a, b: (_lower_bounds_f(a, b),), [lb_gamma_f[0], lb_gamma_f[1]], [((2, REC_W), F32)])
    lb4 = jnp.concatenate([jnp.zeros_like(lb1), lb1], axis=0)

    def layer_weights(l):
        row = lambda t: t[l].reshape(1, -1)
        return dict(norm1_w=row(norm1_w), w_in=w_in_f[l], conv_a_w=conv_a_pad[l], conv_a_b=row(conv_a_b), ln_a_w=row(ln_a_w),
                    ln_a_b=row(ln_a_b), lb_f=lb4[2 * l:2 * l + 1], lb_b=lb4[2 * l + 1:2 * l + 2], rec_norm_w=row(rec_norm_w),
                    w_out=w_out_f[l], norm2_w=row(norm2_w), w_up=w_up_f[l], conv_f_w=conv_f_f[l], w_down=w_down_f[l])

    ws = [layer_weights(l) for l in range(DEPTH)]
    h, saved = xs, []
    for l in range(DEPTH):
        h, s = _layer_fwd(h, mod[l], ws[l])
        saved.append(s)
    gh, g_final, loss_row = _loss_head(h, tgt, final_norm_w.reshape(1, D))
    loss = lax.psum(loss_row[0, 0], ("x", "y", "c"))
    gmods, gws = [None] * DEPTH, [None] * DEPTH
    for l in reversed(range(DEPTH)):
        gh, gmods[l], gws[l] = _layer_bwd(gh, mod[l], ws[l], saved[l])
    grad_x = gh[None]

    glb1 = jnp.concatenate([gws[1]["lb_f"], gws[1]["lb_b"]], axis=0)

    def lb_bwd_fn(a, b, g1):
        _, vjp = jax.vjp(_lower_bounds_f, a, b)
        return vjp(g1)

    g_lb_gamma = jnp.stack(_vmem_call("lower_bounds_bwd", lb_bwd_fn, [lb_gamma_f[0], lb_gamma_f[1], glb1], [((2, REC_W), F32)] * 2))
    pieces = [jnp.stack(gmods).reshape(-1)]
    for l in range(DEPTH):
        pieces += [gws[l][n].reshape(-1) for n, _ in SMALL_REPL]
    pieces += [g_final.reshape(-1)]
    pieces += [jnp.stack([gws[l]["conv_a_w"] for l in range(DEPTH)]).reshape(-1), g_lb_gamma.reshape(-1),
               jnp.stack([gws[l]["conv_f_w"] for l in range(DEPTH)]).reshape(-1)]
    small_g = jnp.concatenate(pieces)
    n_small = small_g.shape[0]
    gsm = _gather_row("gather_small_grads", small_g, 71680)
    n_mod = DEPTH * N_MOD * D
    gmod_all = gsm[:, :n_mod].reshape(N_DEV, DEPTH, N_MOD * D)
    gmod_loc = lax.dynamic_slice_in_dim(gmod_all, me * ada_cols, ada_cols, axis=2).transpose(1, 0, 2)

    def small_fn(gsm_, c_all_, gm_):
        cond = c_all_ * jax.nn.sigmoid(c_all_)
        gw = jnp.concatenate([_dot_tn(cond, gm_[l], HP) for l in range(DEPTH)], axis=0)
        return jnp.sum(gsm_, axis=0, keepdims=True), gw

    tot, g_w_ada = _vmem_call("small_grads", small_fn, [gsm, c_all, gmod_loc],
                              [((1, n_small), F32), ((DEPTH * D, ada_cols), F32)])
    tot = tot[0]
    grads = {"w_ada": g_w_ada.reshape(DEPTH, D, ada_cols), "b_ada": tot[:n_mod].reshape(DEPTH, N_MOD * D)}
    pos = n_mod
    per_layer = {n: [] for n, _ in SMALL_REPL}
    for l in range(DEPTH):
        for n, width in SMALL_REPL:
            per_layer[n].append(tot[pos:pos + width])
            pos += width
    for n, _ in SMALL_REPL:
        grads[n] = jnp.stack(per_layer[n])
    grads["final_norm_w"] = tot[pos:pos + D]
    pos += D
    n_ca, n_lb, n_cf = DEPTH * CONV_W * CONV_CH, DEPTH * 2 * REC_W, DEPTH * FFN_CONV_W * 2 * D_FF
    g_ca = tot[pos:pos + n_ca].reshape(DEPTH, CONV_W, CONV_CH)
    g_lb = tot[pos + n_ca:pos + n_ca + n_lb].reshape(DEPTH, 2, REC_W)
    g_cf = tot[pos + n_ca + n_lb:pos + n_ca + n_lb + n_cf].reshape(DEPTH, FFN_CONV_W, 2 * D_FF)
    grads["conv_a_w"] = lax.dynamic_slice_in_dim(g_ca, me * conv_a_w.shape[2], conv_a_w.shape[2], axis=2)
    grads["lb_gamma"] = lax.dynamic_slice_in_dim(g_lb, me * lb_gamma.shape[2], lb_gamma.shape[2], axis=2)
    grads["conv_f_w"] = lax.dynamic_slice_in_dim(g_cf, me * conv_f_w.shape[2], conv_f_w.shape[2], axis=2)

    gw_in = _unpermute_in_cols(jnp.stack([gws[l]["w_in"] for l in range(DEPTH)]))
    gw_up = jnp.stack([gws[l]["w_up"] for l in range(DEPTH)])
    gw_out = jnp.stack([gws[l]["w_out"] for l in range(DEPTH)])
    gw_out = jnp.concatenate([gw_out[:, D - CONV_CH:], gw_out[:, :D - CONV_CH]], axis=1)
    gw_down = jnp.stack([gws[l]["w_down"] for l in range(DEPTH)])
    rows_to_parts = lambda t: t.reshape(DEPTH, N_DEV, -1, D).transpose(1, 0, 2, 3).reshape(N_DEV, -1, D)
    parts = [_cols_to_parts(gw_in), _cols_to_parts(gw_up), rows_to_parts(gw_out), rows_to_parts(gw_down)]
    r_in, r_up, r_out, r_down = _scatter_parts("scatter_grads", parts)

    given = dict(w_ada=(w_ada, m_w_ada, v_w_ada), b_ada=(b_ada, m_b_ada, v_b_ada), norm1_w=(norm1_w, m_norm1_w, v_norm1_w),
                 w_in=(w_in, m_w_in, v_w_in), conv_a_w=(conv_a_w, m_conv_a_w, v_conv_a_w), conv_a_b=(conv_a_b, m_conv_a_b, v_conv_a_b),
                 ln_a_w=(ln_a_w, m_ln_a_w, v_ln_a_w), ln_a_b=(ln_a_b, m_ln_a_b, v_ln_a_b), lb_gamma=(lb_gamma, m_lb_gamma, v_lb_gamma),
                 rec_norm_w=(rec_norm_w, m_rec_norm_w, v_rec_norm_w), w_out=(w_out, m_w_out, v_w_out),
                 norm2_w=(norm2_w, m_norm2_w, v_norm2_w), w_up=(w_up, m_w_up, v_w_up), conv_f_w=(conv_f_w, m_conv_f_w, v_conv_f_w),
                 w_down=(w_down, m_w_down, v_w_down), final_norm_w=(final_norm_w, m_final_norm_w, v_final_norm_w))
    big_parts = dict(w_in=r_in, w_up=r_up, w_out=r_out, w_down=r_down)
    names = list(given)
    res = {}
    for n in names:
        w_, m_, v_ = given[n]
        shape = w_.shape
        C = shape[-1]
        two_d = lambda t: t.reshape(-1, C)
        gp = big_parts[n] if n in big_parts else two_d(grads[n])[None]
        res[n] = [t.reshape(shape) for t in _adamw("adamw_" + n, two_d(w_), two_d(m_), two_d(v_), gp)]
    return (loss, grad_x, *[res[n][0] for n in names], *[res[n][1] for n in names],
            *[res[n][2] for n in names], *[res[n][3] for n in names])
```

```python
import functools

import numpy as np
import jax
import jax.numpy as jnp
from jax import lax
from jax.experimental import pallas as pl
from jax.experimental.pallas import tpu as pltpu

F32 = jnp.float32
BF16 = jnp.bfloat16
HP = lax.Precision.HIGHEST
MESH = pl.DeviceIdType.MESH

N_DEV = 8
D = 1024
DEPTH = 2
CONV_CH = 256
CONV_W = 31
CONV_W_PAD = 32
ATT_W = 384
REC_W = 384
N_HEADS = 6
HEAD = 64
HEAD_SHIFT = 6
HALF_BAND = 64
ATT_BLK = 128
DILATIONS = (1, 4, 16)
ALIBI_SLOPES = tuple(float(2.0 ** (-8.0 * (h + 1) / N_HEADS)) for h in range(N_HEADS))
MASK_VALUE = -1e30
REC_CHUNK = 64
EXP_CLAMP = 80.0
F_TINY = 1e-30
IN_COLS = 3584
D_FF = 2816
FFN_CONV_W = 3
N_MOD = 6
EPS = 1e-6
ADAM_LR, ADAM_B1, ADAM_B2, ADAM_EPS, ADAM_WD, ADAM_STEP = 0.001, 0.9, 0.999, 1e-08, 0.01, 10

VMEM_LIMIT_BYTES = 56 * 1024 * 1024
SUBLANES_F32 = 8
LANES = 128

QA, KA, VA, QR, ZF, ZB, IR, GR = range(8)
AV_BLK, AG_BLK = 12, 13
CONV_COLS = 2 * CONV_CH


def _cparams(sem=None):
    kw = dict(vmem_limit_bytes=VMEM_LIMIT_BYTES)
    if sem is not None:
        kw["dimension_semantics"] = sem
    return pltpu.CompilerParams(**kw)


def _iota(shape, dim):
    return lax.broadcasted_iota(jnp.int32, shape, dim)


def _dot(a, b, dims, precision=None):
    return lax.dot_general(a, b, (dims, ((), ())), precision=precision, preferred_element_type=F32)


def _dot_nn(a, b, precision=None):
    return _dot(a, b, ((1,), (0,)), precision)


def _dot_nt(a, b, precision=None):
    return _dot(a, b, ((1,), (1,)), precision)


def _dot_tn(a, b, precision=None):
    return _dot(a, b, ((0,), (0,)), precision)


def _c0(j):
    return 0


def _pick(n, cands):
    for c in cands:
        if n % c == 0:
            return c
    return n


MATMUL_OUT_TILE_BYTES = 8 * 1024 * 1024


def _div_lanes(n, cap):
    best = None
    for d in range(LANES, min(n, cap) + 1, LANES):
        if n % d == 0:
            best = d
    return best if best is not None else n


def _matmul_tiles(mode, M, N, K):
    if mode == "nn":
        tm = _pick(M, (1024, 512, 256, 128))
        return tm, _div_lanes(N, MATMUL_OUT_TILE_BYTES // (4 * tm)), K
    if mode == "nt":
        return _pick(M, (512, 256, 128)), N, K
    tm = _div_lanes(M, 1408)
    return tm, _div_lanes(N, MATMUL_OUT_TILE_BYTES // (4 * tm)), _pick(K, (1024, 512, 256))


def _matmul(name, a, b, mode, out_dtype=F32):
    if mode == "nn":
        (M, K), (_, N) = a.shape, b.shape
    elif mode == "nt":
        (M, K), (N, _) = a.shape, b.shape
    else:
        (K, M), (_, N) = a.shape, b.shape
    tm, tn, tk = _matmul_tiles(mode, M, N, K)
    nk = K // tk
    if mode == "nn":
        a_spec = pl.BlockSpec((tm, tk), lambda i, j, k: (i, k))
        b_spec = pl.BlockSpec((tk, tn), lambda i, j, k: (k, j))
        dims = ((1,), (0,))
    elif mode == "nt":
        a_spec = pl.BlockSpec((tm, tk), lambda i, j, k: (i, k))
        b_spec = pl.BlockSpec((tn, tk), lambda i, j, k: (j, k))
        dims = ((1,), (1,))
    else:
        a_spec = pl.BlockSpec((tk, tm), lambda i, j, k: (k, i))
        b_spec = pl.BlockSpec((tk, tn), lambda i, j, k: (k, j))
        dims = ((0,), (0,))

    def body_whole(a_ref, b_ref, o_ref):
        o_ref[...] = _dot(a_ref[...].astype(BF16), b_ref[...].astype(BF16), dims).astype(o_ref.dtype)

    def body(a_ref, b_ref, o_ref, acc_ref):
        k = pl.program_id(2)
        part = _dot(a_ref[...].astype(BF16), b_ref[...].astype(BF16), dims)

        @pl.when(k == 0)
        def _():
            acc_ref[...] = part

        @pl.when(k > 0)
        def _():
            acc_ref[...] += part

        @pl.when(k == nk - 1)
        def _():
            o_ref[...] = acc_ref[...].astype(o_ref.dtype)

    return pl.pallas_call(
        body_whole if nk == 1 else body, name=name, grid=(M // tm, N // tn, nk),
        in_specs=[a_spec, b_spec],
        out_specs=pl.BlockSpec((tm, tn), lambda i, j, k: (i, j)),
        out_shape=jax.ShapeDtypeStruct((M, N), out_dtype),
        scratch_shapes=[] if nk == 1 else [pltpu.VMEM((tm, tn), F32)],
        compiler_params=_cparams(("parallel", "parallel", "arbitrary")),
    )(a, b)


def _rowwise(name, fn, S, ts, tiles, params=(), outs=(), accs=(), halo=0, ncb=1):
    in_specs, args = [], []
    for arr, w, jm, with_halo in tiles:
        if with_halo:
            hb, nhb = ts // halo, S // halo
            in_specs += [
                pl.BlockSpec((halo, w), lambda j, i, jm=jm, hb=hb: (jnp.maximum(i * hb - 1, 0), jm(j))),
                pl.BlockSpec((ts, w), lambda j, i, jm=jm: (i, jm(j))),
                pl.BlockSpec((halo, w), lambda j, i, jm=jm, hb=hb, nhb=nhb: (jnp.minimum((i + 1) * hb, nhb - 1), jm(j))),
            ]
            args += [arr, arr, arr]
        else:
            in_specs.append(pl.BlockSpec((ts, w), lambda j, i, jm=jm: (i, jm(j))))
            args.append(arr)
    for arr, r, w, jm in params:
        in_specs.append(pl.BlockSpec((r, w), lambda j, i, jm=jm: (0, jm(j))))
        args.append(arr)
    out_specs, out_shape = [], []
    for w, dt, jm, tw in outs:
        out_specs.append(pl.BlockSpec((ts, w), lambda j, i, jm=jm: (i, jm(j))))
        out_shape.append(jax.ShapeDtypeStruct((S, tw), dt))
    for r, w, jm, tw in accs:
        out_specs.append(pl.BlockSpec((r, w), lambda j, i, jm=jm: (0, jm(j))))
        out_shape.append(jax.ShapeDtypeStruct((r, tw), F32))
    n_tiles, n_params, n_outs = len(tiles), len(params), len(outs)

    def body(*refs):
        i = pl.program_id(1)
        pos, vals = 0, []
        for _, w, _, with_halo in tiles:
            if with_halo:
                before, after = refs[pos][...], refs[pos + 2][...]
                before = jnp.where(i > 0, before, jnp.zeros_like(before))
                after = jnp.where(i < S // ts - 1, after, jnp.zeros_like(after))
                vals.append(jnp.concatenate([before, refs[pos + 1][...], after], axis=0))
                pos += 3
            else:
                vals.append(refs[pos][...])
                pos += 1
        prefs = refs[pos:pos + n_params]
        orefs = refs[pos + n_params:pos + n_params + n_outs]
        arefs = refs[pos + n_params + n_outs:]

        @pl.when(i == 0)
        def _():
            for r in arefs:
                r[...] = jnp.zeros_like(r)

        fn(i, vals, prefs, orefs, arefs)

    res = pl.pallas_call(
        body, name=name, grid=(ncb, S // ts),
        in_specs=in_specs, out_specs=out_specs, out_shape=out_shape,
        compiler_params=_cparams(("arbitrary", "arbitrary")),
    )(*args)
    return res


def _vmem_call(name, fn, ins, out_shapes):
    n_in = len(ins)

    def body(*refs):
        vals = fn(*[r[...] for r in refs[:n_in]])
        for r, v in zip(refs[n_in:], vals):
            r[...] = v.astype(r.dtype)

    return pl.pallas_call(
        body, name=name,
        out_shape=[jax.ShapeDtypeStruct(s, dt) for s, dt in out_shapes],
        compiler_params=_cparams(),
    )(*ins)


def _rms(x, w):
    return x * lax.rsqrt(jnp.mean(x * x, axis=-1, keepdims=True) + EPS) * w


def _normmod_f(x, nw, sc, sh):
    return _rms(x, nw) * (1.0 + sc) + sh


def _row_params(*vecs):
    return [(v, 1, v.shape[1], _c0) for v in vecs]


def _normmod_fwd(x, nw, sc, sh):
    S = x.shape[0]

    def fn(i, vals, p, o, a):
        o[0][...] = _normmod_f(vals[0], p[0][...], p[1][...], p[2][...]).astype(BF16)

    return _rowwise("normmod_fwd", fn, S, 512, [(x, D, _c0, False)], _row_params(nw, sc, sh), [(D, BF16, _c0, D)])[0]


def _normmod_bwd(x, gh, gres, nw, sc, sh):
    S = x.shape[0]

    def fn(i, vals, p, o, a):
        _, vjp = jax.vjp(_normmod_f, vals[0], p[0][...], p[1][...], p[2][...])
        gx, gnw, gsc, gsh = vjp(vals[1])
        o[0][...] = gx + vals[2]
        a[0][...] += gnw
        a[1][...] += gsc
        a[2][...] += gsh

    return _rowwise("normmod_bwd", fn, S, 256, [(x, D, _c0, False), (gh, D, _c0, False), (gres, D, _c0, False)],
                    _row_params(nw, sc, sh), [(D, F32, _c0, D)], [(1, D, _c0, D)] * 3)


def _gate_add(x, y, g):
    S = x.shape[0]

    def fn(i, vals, p, o, a):
        o[0][...] = vals[0] + p[0][...] * vals[1]

    return _rowwise("gate_add", fn, S, 512, [(x, D, _c0, False), (y, D, _c0, False)], _row_params(g), [(D, F32, _c0, D)])[0]


def _gate_bwd(gx, y, g):
    S = gx.shape[0]

    def fn(i, vals, p, o, a):
        o[0][...] = (vals[0] * p[0][...]).astype(BF16)
        a[0][...] += jnp.sum(vals[0] * vals[1], axis=0, keepdims=True)

    return _rowwise("gate_bwd", fn, S, 512, [(gx, D, _c0, False), (y, D, _c0, False)], _row_params(g),
                    [(D, BF16, _c0, D)], [(1, D, _c0, D)])


def _loss_head(x, tgt, fw):
    S = x.shape[0]

    def fn(i, vals, p, o, a):
        y, vjp = jax.vjp(_rms, vals[0], p[0][...])
        err = y - vals[1]
        gx, gfw = vjp(err * (1.0 / D))
        o[0][...] = gx
        a[0][...] += gfw
        part = 0.5 * jnp.sum(jnp.mean(err * err, axis=-1, keepdims=True), axis=0, keepdims=True)
        a[1][...] += jnp.broadcast_to(part, (1, LANES))

    return _rowwise("loss_head", fn, S, 256, [(x, D, _c0, False), (tgt, D, _c0, False)], _row_params(fw),
                    [(D, F32, _c0, D)], [(1, D, _c0, D), (1, LANES, _c0, LANES)])


CONV_HALO = 16
CONV_TS = 512


def _shifted(ext, shift, ts, halo):
    n = ext.shape[0]
    s = shift % n
    r = ext if s == 0 else pltpu.roll(ext, s, 0)
    return r[halo:halo + ts]


def _ln_silu(a, w, b):
    mu = jnp.mean(a, axis=-1, keepdims=True)
    var = jnp.mean(jnp.square(a - mu), axis=-1, keepdims=True)
    y = (a - mu) * lax.rsqrt(var + EPS) * w + b
    return y * jax.nn.sigmoid(y)


def _conv_a_fwd(proj, w_pad, b, lnw, lnb):
    S = proj.shape[0]
    ts, H = min(CONV_TS, S), CONV_HALO

    def fn(i, vals, p, o, a):
        a0 = vals[0] * jax.nn.sigmoid(vals[1])
        acc = jnp.zeros((ts, CONV_CH), F32) + p[1][...]
        for k in range(CONV_W):
            acc = acc + _shifted(a0, CONV_W // 2 - k, ts, H) * p[0][pl.ds(k, 1), :]
        o[0][...] = acc
        o[1][...] = _ln_silu(acc, p[2][...], p[3][...]).astype(BF16)

    tiles = [(proj, CONV_CH, lambda j: AV_BLK, True), (proj, CONV_CH, lambda j: AG_BLK, True)]
    params = [(w_pad, CONV_W_PAD, CONV_CH, _c0)] + _row_params(b, lnw, lnb)
    return _rowwise("conv_a_fwd", fn, S, ts, tiles, params, [(CONV_CH, F32, _c0, CONV_CH), (CONV_CH, BF16, _c0, CONV_CH)], halo=H)


def _conv_a_bwd(proj, a1, gmixed, w_pad, lnw, lnb):
    S = proj.shape[0]
    ts, H = min(CONV_TS, S), CONV_HALO

    def fn(i, vals, p, o, a):
        av, ag, a1e, ge = vals
        lw, lb = p[1][...], p[2][...]
        _, vjp_e = jax.vjp(lambda t: _ln_silu(t, lw, lb), a1e)
        (ga1e,) = vjp_e(ge)
        c = slice(H, H + ts)
        _, vjp_c = jax.vjp(_ln_silu, a1e[c], lw, lb)
        ga1, glw, glb = vjp_c(ge[c])
        a[1][...] += jnp.sum(ga1, axis=0, keepdims=True)
        a[2][...] += glw
        a[3][...] += glb
        sg = jax.nn.sigmoid(ag)
        a0 = av * sg
        ga0 = jnp.zeros((ts, CONV_CH), F32)
        for k in range(CONV_W):
            a[0][pl.ds(k, 1), :] += jnp.sum(ga1 * _shifted(a0, CONV_W // 2 - k, ts, H), axis=0, keepdims=True)
            ga0 = ga0 + _shifted(ga1e, k - CONV_W // 2, ts, H) * p[0][pl.ds(k, 1), :]
        sgc, avc = sg[c], av[c]
        o[0][...] = (ga0 * sgc).astype(BF16)
        o[1][...] = (ga0 * avc * sgc * (1.0 - sgc)).astype(BF16)

    tiles = [(proj, CONV_CH, lambda j: AV_BLK, True), (proj, CONV_CH, lambda j: AG_BLK, True),
             (a1, CONV_CH, _c0, True), (gmixed, CONV_CH, lambda j: 3, True)]
    params = [(w_pad, CONV_W_PAD, CONV_CH, _c0)] + _row_params(lnw, lnb)
    outs = [(CONV_CH, BF16, _c0, CONV_CH), (CONV_CH, BF16, _c0, CONV_CH)]
    accs = [(CONV_W_PAD, CONV_CH, _c0, CONV_CH)] + [(1, CONV_CH, _c0, CONV_CH)] * 3
    return _rowwise("conv_a_bwd", fn, S, ts, tiles, params, outs, accs, halo=H)


FFN_HALO = 8
FFN_TS = 512
FFN_CB = 256
FFN_NCB = D_FF // FFN_CB


def _gelu_mul(g, v):
    return 0.5 * g * (1.0 + lax.erf(g * (2.0 ** -0.5))) * v


def _ffn_mid_fwd(u, cw):
    S = u.shape[0]
    ts, H = min(FFN_TS, S), FFN_HALO

    def conv(ext, w_ref):
        acc = jnp.zeros((ts, FFN_CB), F32)
        for k in range(FFN_CONV_W):
            acc = acc + _shifted(ext, 1 - k, ts, H) * w_ref[pl.ds(k, 1), :]
        return acc

    def fn(i, vals, p, o, a):
        o[0][...] = _gelu_mul(conv(vals[0], p[0]), conv(vals[1], p[1])).astype(BF16)

    tiles = [(u, FFN_CB, lambda j: j, True), (u, FFN_CB, lambda j: j + FFN_NCB, True)]
    params = [(cw, FFN_CONV_W, FFN_CB, lambda j: j), (cw, FFN_CONV_W, FFN_CB, lambda j: j + FFN_NCB)]
    return _rowwise("ffn_mid_fwd", fn, S, ts, tiles, params, [(FFN_CB, BF16, lambda j: j, D_FF)], halo=H, ncb=FFN_NCB)[0]


def _ffn_mid_bwd(u, gact, cw):
    S = u.shape[0]
    ts, H = min(FFN_TS, S), FFN_HALO
    n = ts + 2 * H

    def conv_all(ext, w_ref):
        acc = jnp.zeros((n, FFN_CB), F32)
        for k in range(FFN_CONV_W):
            s = (1 - k) % n
            acc = acc + (ext if s == 0 else pltpu.roll(ext, s, 0)) * w_ref[pl.ds(k, 1), :]
        return acc

    def fn(i, vals, p, o, a):
        ug, uv, ga = vals
        _, vjp = jax.vjp(_gelu_mul, conv_all(ug, p[0]), conv_all(uv, p[1]))
        gcg, gcv = vjp(ga)
        c = slice(H, H + ts)
        for half, (gc, ue) in enumerate(((gcg, ug), (gcv, uv))):
            gu = jnp.zeros((ts, FFN_CB), F32)
            for k in range(FFN_CONV_W):
                gu = gu + _shifted(gc, k - 1, ts, H) * p[half][pl.ds(k, 1), :]
                a[half][pl.ds(k, 1), :] += jnp.sum(gc[c] * _shifted(ue, 1 - k, ts, H), axis=0, keepdims=True)
            o[half][...] = gu.astype(BF16)

    tiles = [(u, FFN_CB, lambda j: j, True), (u, FFN_CB, lambda j: j + FFN_NCB, True), (gact, FFN_CB, lambda j: j, True)]
    params = [(cw, FFN_CONV_W, FFN_CB, lambda j: j), (cw, FFN_CONV_W, FFN_CB, lambda j: j + FFN_NCB)]
    outs = [(FFN_CB, BF16, lambda j: j, D_FF)] * 2
    accs = [(FFN_CONV_W, FFN_CB, lambda j: j, D_FF)] * 2
    return _rowwise("ffn_mid_bwd", fn, S, ts, tiles, params, outs, accs, halo=H, ncb=FFN_NCB)


LD_W = 16


def _to_sub(t, d):
    S, C = t.shape
    return t.reshape(S // d, d, C).transpose(1, 0, 2)


def _from_sub(t):
    d, L, C = t.shape
    return t.transpose(1, 0, 2).reshape(d * L, C)


def _halo_specs(width, col, blk, hb, nhb):
    per = blk // hb
    return [
        pl.BlockSpec((None, hb, width), lambda r, i: (r, jnp.maximum(i * per - 1, 0), col)),
        pl.BlockSpec((None, blk, width), lambda r, i: (r, i, col)),
        pl.BlockSpec((None, hb, width), lambda r, i: (r, jnp.minimum((i + 1) * per, nhb - 1), col)),
    ]


def _to_sub_heads(t, d):
    S, C = t.shape
    return t.reshape(S // d, d, C // HEAD, HEAD).transpose(1, 2, 0, 3)


def _from_sub_heads(t):
    d, n, L, _ = t.shape
    return t.transpose(2, 0, 1, 3).reshape(d * L, n * HEAD)


def _head_halo_specs(group, blk, hb, nhb):
    per = blk // hb
    return [
        pl.BlockSpec((None, N_HEADS, hb, HEAD), lambda r, i: (r, group, jnp.maximum(i * per - 1, 0), 0)),
        pl.BlockSpec((None, N_HEADS, blk, HEAD), lambda r, i: (r, group, i, 0)),
        pl.BlockSpec((None, N_HEADS, hb, HEAD), lambda r, i: (r, group, jnp.minimum((i + 1) * per, nhb - 1), 0)),
    ]


def _pick_lane(t, lane):
    return jnp.sum(jnp.where(_iota((1, t.shape[1]), 1) == lane, t, 0.0), axis=1, keepdims=True)


def _attn_fwd(qkv, dil):
    d, _, L, _ = qkv.shape
    blk, hb = min(ATT_BLK, L), HALF_BAND
    span = blk + 2 * hb

    def body(q_ref, kp, kc, kn, vp, vc, vn, o_ref, l_ref):
        i = pl.program_id(1)
        rel = _iota((blk, span), 1) - hb - _iota((blk, span), 0)
        kpos = i * blk - hb + _iota((blk, span), 1)
        valid = (jnp.abs(rel) <= hb) & (kpos >= 0) & (kpos < L)
        dist = jnp.abs(rel).astype(F32) * float(dil)
        lse = jnp.zeros((blk, 8), F32)
        for h in range(N_HEADS):
            k = jnp.concatenate([kp[h], kc[h], kn[h]], axis=0)
            v = jnp.concatenate([vp[h], vc[h], vn[h]], axis=0)
            s = _dot_nt(q_ref[h], k) * (HEAD ** -0.5) - ALIBI_SLOPES[h] * dist
            s = jnp.where(valid, s, MASK_VALUE)
            m = jnp.max(s, axis=1, keepdims=True)
            p = jnp.exp(s - m)
            l = jnp.sum(p, axis=1, keepdims=True)
            o_ref[h] = _dot_nn(p.astype(BF16), v) / l
            lse = lse + jnp.where(_iota((1, 8), 1) == h, m + jnp.log(l), 0.0)
        l_ref[...] = lse

    nhb = L // hb
    head_spec = pl.BlockSpec((None, N_HEADS, blk, HEAD), lambda r, i: (r, 0, i, 0))
    in_specs = [head_spec] + _head_halo_specs(1, blk, hb, nhb) + _head_halo_specs(2, blk, hb, nhb)
    return pl.pallas_call(
        body, name=f"attn_fwd_d{dil}", grid=(d, L // blk), in_specs=in_specs,
        out_specs=[head_spec, pl.BlockSpec((None, blk, 8), lambda r, i: (r, i, 0))],
        out_shape=[jax.ShapeDtypeStruct((d, N_HEADS, L, HEAD), F32), jax.ShapeDtypeStruct((d, L, 8), F32)],
        compiler_params=_cparams(("parallel", "parallel")),
    )(*([qkv] * 7))


def _attn_bwd(qkv, do, ld, dil):
    d, _, L, _ = qkv.shape
    blk, hb = min(ATT_BLK, L), HALF_BAND
    span = blk + 2 * hb
    scale = HEAD ** -0.5

    def body(qp, qc, qn, kp, kc, kn, vp, vc, vn, gp, gc, gn, lp, lc, ln, dq_ref, dk_ref, dv_ref):
        i = pl.program_id(1)
        l = lc[...]
        le = jnp.concatenate([lp[...], l, ln[...]], axis=0)
        rel_q = _iota((blk, span), 1) - hb - _iota((blk, span), 0)
        kpos = i * blk - hb + _iota((blk, span), 1)
        valid_q = (jnp.abs(rel_q) <= hb) & (kpos >= 0) & (kpos < L)
        dist_q = jnp.abs(rel_q).astype(F32) * float(dil)
        rel_k = _iota((span, blk), 1) + hb - _iota((span, blk), 0)
        qpos = i * blk - hb + _iota((span, blk), 0)
        valid_k = (jnp.abs(rel_k) <= hb) & (qpos >= 0) & (qpos < L)
        dist_k = jnp.abs(rel_k).astype(F32) * float(dil)
        for h in range(N_HEADS):
            cat = lambda a, b, c: jnp.concatenate([a[h], b[h], c[h]], axis=0)
            q, k, v, g = qc[h], kc[h], vc[h], gc[h]
            qe, ke, ve, ge = cat(qp, qc, qn), cat(kp, kc, kn), cat(vp, vc, vn), cat(gp, gc, gn)
            s = _dot_nt(q, ke) * scale - ALIBI_SLOPES[h] * dist_q
            p = jnp.where(valid_q, jnp.exp(s - _pick_lane(l, h)), 0.0)
            ds = p * (_dot_nt(g, ve) - _pick_lane(l, 8 + h))
            dq_ref[h] = _dot_nn(ds.astype(BF16), ke) * scale
            s = _dot_nt(qe, k) * scale - ALIBI_SLOPES[h] * dist_k
            p = jnp.where(valid_k, jnp.exp(s - _pick_lane(le, h)), 0.0)
            dv_ref[h] = _dot_tn(p.astype(BF16), ge)
            ds = p * (_dot_nt(ge, v) - _pick_lane(le, 8 + h))
            dk_ref[h] = _dot_tn(ds.astype(BF16), qe) * scale

    nhb = L // hb
    in_specs = (_head_halo_specs(0, blk, hb, nhb) + _head_halo_specs(1, blk, hb, nhb) + _head_halo_specs(2, blk, hb, nhb)
                + _head_halo_specs(0, blk, hb, nhb) + _halo_specs(LD_W, 0, blk, hb, nhb))
    o_spec = pl.BlockSpec((None, N_HEADS, blk, HEAD), lambda r, i: (r, 0, i, 0))
    return pl.pallas_call(
        body, name=f"attn_bwd_d{dil}", grid=(d, L // blk), in_specs=in_specs,
        out_specs=[o_spec] * 3, out_shape=[jax.ShapeDtypeStruct((d, N_HEADS, L, HEAD), F32)] * 3,
        compiler_params=_cparams(("parallel", "parallel")),
    )(*([qkv] * 9 + [do] * 3 + [ld] * 3))


def _head_expand(t8):
    e = ((_iota((8, ATT_W), 1) >> HEAD_SHIFT) == _iota((8, ATT_W), 0)).astype(F32)
    return _dot_nn(t8, e, HP)


def _attn_merge(os, ls):
    S = os[0].shape[0]

    def fn(i, vals, p, o, a):
        o3, l3 = vals[:3], vals[3:]
        m = jnp.maximum(jnp.maximum(l3[0], l3[1]), l3[2])
        e3 = [jnp.exp(l - m) for l in l3]
        den = e3[0] + e3[1] + e3[2]
        out = jnp.zeros((o3[0].shape[0], ATT_W), F32)
        for ob, e in zip(o3, e3):
            out = out + _head_expand(e / den) * ob
        o[0][...] = out
        o[1][...] = m + jnp.log(den)

    tiles = [(t, ATT_W, _c0, False) for t in os] + [(t, 8, _c0, False) for t in ls]
    return _rowwise("attn_merge", fn, S, 512, tiles, (), [(ATT_W, F32, _c0, ATT_W), (8, F32, _c0, 8)])


def _attn_bwd_prep(gmixed, att, lse):
    S = att.shape[0]

    def fn(i, vals, p, o, a):
        g, out, l8 = vals
        place_l = (_iota((8, LD_W), 1) == _iota((8, LD_W), 0)).astype(F32)
        place_d = ((_iota((ATT_W, LD_W), 0) >> HEAD_SHIFT) + 8 == _iota((ATT_W, LD_W), 1)).astype(F32)
        o[0][...] = g.astype(BF16)
        o[1][...] = _dot_nn(l8, place_l, HP) + _dot_nn(g * out, place_d, HP)

    tiles = [(gmixed, ATT_W, _c0, False), (att, ATT_W, _c0, False), (lse, 8, _c0, False)]
    return _rowwise("attn_bwd_prep", fn, S, 512, tiles, (), [(ATT_W, BF16, _c0, ATT_W), (LD_W, F32, _c0, LD_W)])


def _sum3_bf16(ts_, S, width):
    def fn(i, vals, p, o, a):
        o[0][...] = (vals[0] + vals[1] + vals[2]).astype(BF16)

    return _rowwise("sum3", fn, S, 512, [(t, width, _c0, False) for t in ts_], (), [(width, BF16, _c0, width)])[0]


def _block_diag_mask():
    return ((_iota((REC_W, REC_W), 0) >> HEAD_SHIFT) == (_iota((REC_W, REC_W), 1) >> HEAD_SHIFT)).astype(F32)


def _rep_heads(t):
    return jnp.concatenate([t] * N_HEADS, axis=0)


def _hgrn_chunk(qr, z, iv, lb, st, reverse):
    C = REC_CHUNK
    r, c = _iota((C, C), 0), _iota((C, C), 1)
    if reverse:
        t_cum, t_mid = (c >= r), (c >= C // 2)
    else:
        t_cum, t_mid = (c <= r), (c < C // 2)
    f = lb + (1.0 - lb) * jax.nn.sigmoid(z)
    logf = jnp.log(jnp.maximum(f, F_TINY))
    k = (1.0 - lb) * jax.nn.sigmoid(-z)
    q = qr * jax.nn.sigmoid(qr)
    b = _dot_nn(t_cum.astype(F32), logf, HP)
    bm = _dot_nn(t_mid.astype(F32), logf, HP)
    bl = _dot_nn(jnp.ones((C, C), F32), logf, HP)
    qt = q * jnp.exp(jnp.minimum(b - bm, EXP_CLAMP))
    kt = k * jnp.exp(jnp.minimum(bm - b, EXP_CLAMP))
    qh = q * jnp.exp(b)
    kh = k * jnp.exp(bl - b)
    bd = _block_diag_mask()
    k_bd = (_rep_heads(kt) * bd).astype(BF16)
    v_bd = (_rep_heads(iv) * bd).astype(BF16)
    st_bd = _rep_heads(st) * bd
    a = _dot_nt(qt.astype(BF16), k_bd)
    s_in = _iota((C, REC_W), 1) & (HEAD - 1)
    t_in = _iota((C, REC_W), 0)
    a = jnp.where((s_in >= t_in) if reverse else (s_in <= t_in), a, 0.0)
    o = _dot_nn(a.astype(BF16), v_bd) + _dot_nt(qh.astype(BF16), st_bd.astype(BF16))
    kv = _dot_tn(iv.astype(BF16), kh.astype(BF16))
    st_bd = st_bd * _rep_heads(jnp.exp(bl)) + kv * bd
    st_new = st_bd[0:HEAD]
    for h in range(1, N_HEADS):
        st_new = st_new + st_bd[h * HEAD:(h + 1) * HEAD]
    return o, st_new


REC_CHUNKS_PER_STEP = 8
REC_ROWS = REC_CHUNKS_PER_STEP * REC_CHUNK


def _hgrn_specs(order, blocks):
    return [pl.BlockSpec((REC_ROWS, REC_W), lambda i, b=b: (order(i), b)) for b in blocks]


def _chunk_rows(j):
    return pl.ds(pl.multiple_of(j * REC_CHUNK, REC_CHUNK), REC_CHUNK)


def _hgrn_fwd(proj, lb, z_blk, reverse):
    S = proj.shape[0]
    nb = S // REC_ROWS
    order = (lambda i: nb - 1 - i) if reverse else (lambda i: i)

    def body(q_ref, z_ref, v_ref, lb_ref, o_ref, st_ref, st_scr):
        @pl.when(pl.program_id(0) == 0)
        def _():
            st_scr[...] = jnp.zeros_like(st_scr)

        def step(t, carry):
            j = REC_CHUNKS_PER_STEP - 1 - t if reverse else t
            rows = _chunk_rows(j)
            st = st_scr[...]
            st_ref[j] = st
            o, st_new = _hgrn_chunk(q_ref[rows, :], z_ref[rows, :], v_ref[rows, :], lb_ref[...], st, reverse)
            o_ref[rows, :] = o
            st_scr[...] = st_new
            return carry

        lax.fori_loop(0, REC_CHUNKS_PER_STEP, step, 0)

    return pl.pallas_call(
        body, name="hgrn_rev_fwd" if reverse else "hgrn_fwd_fwd", grid=(nb,),
        in_specs=_hgrn_specs(order, (QR, z_blk, IR)) + [pl.BlockSpec((1, REC_W), lambda i: (0, 0))],
        out_specs=[pl.BlockSpec((REC_ROWS, REC_W), lambda i: (order(i), 0)),
                   pl.BlockSpec((REC_CHUNKS_PER_STEP, HEAD, REC_W), lambda i: (order(i), 0, 0))],
        out_shape=[jax.ShapeDtypeStruct((S, REC_W), F32), jax.ShapeDtypeStruct((S // REC_CHUNK, HEAD, REC_W), F32)],
        scratch_shapes=[pltpu.VMEM((HEAD, REC_W), F32)],
        compiler_params=_cparams(("arbitrary",)),
    )(proj, proj, proj, lb)


def _hgrn_bwd(proj, lb, states, go, z_blk, reverse):
    S = proj.shape[0]
    nb = S // REC_ROWS
    order = (lambda i: i) if reverse else (lambda i: nb - 1 - i)

    def body(q_ref, z_ref, v_ref, lb_ref, st_ref, go_ref, gq_ref, gz_ref, gv_ref, glb_ref, gst_scr):
        @pl.when(pl.program_id(0) == 0)
        def _():
            gst_scr[...] = jnp.zeros_like(gst_scr)
            glb_ref[...] = jnp.zeros_like(glb_ref)

        chunk = functools.partial(_hgrn_chunk, reverse=reverse)

        def step(t, carry):
            j = t if reverse else REC_CHUNKS_PER_STEP - 1 - t
            rows = _chunk_rows(j)
            _, vjp = jax.vjp(chunk, q_ref[rows, :], z_ref[rows, :], v_ref[rows, :], lb_ref[...], st_ref[j])
            gq, gz, gv, glb, gst = vjp((go_ref[rows, :], gst_scr[...]))
            gq_ref[rows, :] = gq
            gz_ref[rows, :] = gz
            gv_ref[rows, :] = gv
            glb_ref[...] += glb
            gst_scr[...] = gst
            return carry

        lax.fori_loop(0, REC_CHUNKS_PER_STEP, step, 0)

    row_spec = pl.BlockSpec((REC_ROWS, REC_W), lambda i: (order(i), 0))
    return pl.pallas_call(
        body, name="hgrn_rev_bwd" if reverse else "hgrn_fwd_bwd", grid=(nb,),
        in_specs=(_hgrn_specs(order, (QR, z_blk, IR)) + [pl.BlockSpec((1, REC_W), lambda i: (0, 0))]
                  + [pl.BlockSpec((REC_CHUNKS_PER_STEP, HEAD, REC_W), lambda i: (order(i), 0, 0)), row_spec]),
        out_specs=[row_spec] * 3 + [pl.BlockSpec((1, REC_W), lambda i: (0, 0))],
        out_shape=[jax.ShapeDtypeStruct((S, REC_W), F32)] * 3 + [jax.ShapeDtypeStruct((1, REC_W), F32)],
        scratch_shapes=[pltpu.VMEM((HEAD, REC_W), F32)],
        compiler_params=_cparams(("arbitrary",)),
    )(proj, proj, proj, lb, states, go)


def _hgrn_post_f(of, ob, gr, rnw):
    o = of + ob
    ms = _dot_nn(o * o, _block_diag_mask() * (1.0 / HEAD), HP)
    return o * lax.rsqrt(ms + EPS) * rnw * (gr * jax.nn.sigmoid(gr))


def _hgrn_post_fwd(of, ob, proj, rnw):
    S = of.shape[0]

    def fn(i, vals, p, o, a):
        o[0][...] = _hgrn_post_f(vals[0], vals[1], vals[2], p[0][...]).astype(BF16)

    tiles = [(of, REC_W, _c0, False), (ob, REC_W, _c0, False), (proj, REC_W, lambda j: GR, False)]
    return _rowwise("hgrn_post_fwd", fn, S, 512, tiles, _row_params(rnw), [(REC_W, BF16, _c0, REC_W)])[0]


def _hgrn_post_bwd(of, ob, proj, gmixed, rnw):
    S = of.shape[0]

    def fn(i, vals, p, o, a):
        _, vjp = jax.vjp(_hgrn_post_f, vals[0], vals[1], vals[2], p[0][...])
        go, _, ggr, grnw = vjp(vals[3])
        o[0][...] = go
        o[1][...] = ggr
        a[0][...] += grnw

    tiles = [(of, REC_W, _c0, False), (ob, REC_W, _c0, False), (proj, REC_W, lambda j: GR, False),
             (gmixed, REC_W, lambda j: 1, False)]
    return _rowwise("hgrn_post_bwd", fn, S, 256, tiles, _row_params(rnw),
                    [(REC_W, F32, _c0, REC_W), (REC_W, F32, _c0, REC_W)], [(1, REC_W, _c0, REC_W)])


def _lower_bounds_f(g0, g1):
    m = jnp.maximum(g0, g1)
    e0, e1 = jnp.exp(g0 - m), jnp.exp(g1 - m)
    return e1 / (e0 + e1)


def _adamw(name, w, m, v, gparts):
    R, C = w.shape
    P = gparts.shape[0]
    tr = R if R * C * 4 * (P + 7) * 2 <= VMEM_LIMIT_BYTES // 2 else _pick(R, (256, 128, 64, 32, 16, 8))

    def body(w_ref, m_ref, v_ref, gp_ref, g_ref, d_ref, nm_ref, nv_ref):
        g = gp_ref[0].astype(F32)
        for p in range(1, P):
            g = g + gp_ref[p].astype(F32)
        w_ = w_ref[...]
        nm = ADAM_B1 * m_ref[...] + (1.0 - ADAM_B1) * g
        nv = ADAM_B2 * v_ref[...] + (1.0 - ADAM_B2) * jnp.square(g)
        m_hat = nm / (1.0 - ADAM_B1 ** ADAM_STEP)
        v_hat = nv / (1.0 - ADAM_B2 ** ADAM_STEP)
        g_ref[...] = g
        d_ref[...] = -ADAM_LR * (m_hat / (jnp.sqrt(v_hat) + ADAM_EPS) + ADAM_WD * w_)
        nm_ref[...] = nm
        nv_ref[...] = nv

    spec = pl.BlockSpec((tr, C), lambda i: (i, 0))
    return pl.pallas_call(
        body, name=name, grid=(R // tr,),
        in_specs=[spec, spec, spec, pl.BlockSpec((P, tr, C), lambda i: (0, i, 0))],
        out_specs=[spec] * 4, out_shape=[jax.ShapeDtypeStruct((R, C), F32)] * 4,
        compiler_params=_cparams(("parallel",)),
    )(w, m, v, gparts)


def _place():
    return lax.axis_index("x"), lax.axis_index("y"), lax.axis_index("c")


def _index_of(p):
    return 4 * p[0] + 2 * p[1] + p[2]


def _allgather_small(name, rows):
    m_per, n = rows.shape

    def body(x_ref, out_ref, send_sems, recv_sems, local_sem):
        x, y, c = _place()
        me, sibling = (x, y, c), (x, y, 1 - c)
        chips = [(1 - x, y), (x, 1 - y), (1 - x, 1 - y)]

        def blk(p):
            return out_ref.at[pl.ds(_index_of(p) * m_per, m_per), :]

        def copy(k, block, to, src=None):
            return pltpu.make_async_remote_copy(
                src_ref=blk(block) if src is None else src, dst_ref=blk(block),
                send_sem=send_sems.at[k], recv_sem=recv_sems.at[k], device_id=to, device_id_type=MESH)

        mine = pltpu.make_async_copy(x_ref, blk(me), local_sem)
        mine.start()
        first = [copy(0, me, sibling, src=x_ref)]
        first += [copy(1 + j, me, (*chip, c), src=x_ref) for j, chip in enumerate(chips)]
        for cp in first:
            cp.start()
        passed = [copy(4 + j, (*chip, c), sibling) for j, chip in enumerate(chips)]
        for j, chip in enumerate(chips):
            copy(1 + j, (*chip, c), me).wait_recv()
            passed[j].start()
        copy(0, sibling, me).wait_recv()
        for j, chip in enumerate(chips):
            copy(4 + j, (*chip, 1 - c), me).wait_recv()
        for cp in first + passed:
            cp.wait_send()
        mine.wait()

    return pl.pallas_call(
        body, name=name,
        out_shape=jax.ShapeDtypeStruct((N_DEV * m_per, n), rows.dtype),
        in_specs=[pl.BlockSpec(memory_space=pltpu.VMEM)],
        out_specs=pl.BlockSpec(memory_space=pltpu.VMEM),
        scratch_shapes=[pltpu.SemaphoreType.DMA((7,)), pltpu.SemaphoreType.DMA((7,)), pltpu.SemaphoreType.DMA],
        compiler_params=_cparams(),
    )(rows)


def _allgather_big(name, arrs):
    na = len(arrs)

    def body(*refs):
        ins, outs = refs[:na], refs[na:2 * na]
        send_sems, recv_sems, local_sems = refs[2 * na:]
        x, y, c = _place()
        me, sibling = (x, y, c), (x, y, 1 - c)
        chips = [(1 - x, y), (x, 1 - y), (1 - x, 1 - y)]

        def copy(a, k, block, to, src=None):
            dst = outs[a].at[_index_of(block)]
            return pltpu.make_async_remote_copy(
                src_ref=dst if src is None else src, dst_ref=dst,
                send_sem=send_sems.at[a, k], recv_sem=recv_sems.at[a, k], device_id=to, device_id_type=MESH)

        mine = [pltpu.make_async_copy(ins[a], outs[a].at[_index_of(me)], local_sems.at[a]) for a in range(na)]
        for cp in mine:
            cp.start()
        sent = []
        for a in range(na):
            sent.append(copy(a, 0, me, sibling, src=ins[a]))
            sent += [copy(a, 1 + j, me, (*chip, c), src=ins[a]) for j, chip in enumerate(chips)]
        for cp in sent:
            cp.start()
        for j, chip in enumerate(chips):
            for a in range(na):
                copy(a, 1 + j, (*chip, c), me).wait_recv()
                fwd = copy(a, 4 + j, (*chip, c), sibling)
                fwd.start()
                sent.append(fwd)
        for a in range(na):
            copy(a, 0, sibling, me).wait_recv()
            for j, chip in enumerate(chips):
                copy(a, 4 + j, (*chip, 1 - c), me).wait_recv()
        for cp in sent:
            cp.wait_send()
        for cp in mine:
            cp.wait()

    any_spec = pl.BlockSpec(memory_space=pl.ANY)
    return pl.pallas_call(
        body, name=name,
        out_shape=[jax.ShapeDtypeStruct((N_DEV,) + a.shape, a.dtype) for a in arrs],
        in_specs=[any_spec] * na, out_specs=[any_spec] * na,
        scratch_shapes=[pltpu.SemaphoreType.DMA((na, 7)), pltpu.SemaphoreType.DMA((na, 7)), pltpu.SemaphoreType.DMA((na,))],
        compiler_params=_cparams(),
    )(*arrs)


def _scatter_parts(name, parts):
    na = len(parts)

    def body(*refs):
        ins, outs = refs[:na], refs[na:2 * na]
        send_sems, recv_sems, local_sems = refs[2 * na:]
        x, y, c = _place()
        me = _index_of((x, y, c))
        flips = [(k >> 2 & 1, k >> 1 & 1, k & 1) for k in range(1, N_DEV)]
        peers = [(1 - x if fx else x, 1 - y if fy else y, 1 - c if fc else c) for fx, fy, fc in flips]
        mine = [pltpu.make_async_copy(ins[a].at[me], outs[a].at[me], local_sems.at[a]) for a in range(na)]
        for cp in mine:
            cp.start()
        sent = []
        for a in range(na):
            for k, peer in enumerate(peers):
                sent.append(pltpu.make_async_remote_copy(
                    src_ref=ins[a].at[_index_of(peer)], dst_ref=outs[a].at[me],
                    send_sem=send_sems.at[a, k], recv_sem=recv_sems.at[a, k], device_id=peer, device_id_type=MESH))
        for cp in sent:
            cp.start()
        for a in range(na):
            for k, peer in enumerate(peers):
                slot = outs[a].at[_index_of(peer)]
                pltpu.make_async_remote_copy(
                    src_ref=slot, dst_ref=slot, send_sem=send_sems.at[a, k], recv_sem=recv_sems.at[a, k],
                    device_id=peer, device_id_type=MESH).wait_recv()
        for cp in sent:
            cp.wait_send()
        for cp in mine:
            cp.wait()

    any_spec = pl.BlockSpec(memory_space=pl.ANY)
    return pl.pallas_call(
        body, name=name,
        out_shape=[jax.ShapeDtypeStruct(p.shape, p.dtype) for p in parts],
        in_specs=[any_spec] * na, out_specs=[any_spec] * na,
        scratch_shapes=[pltpu.SemaphoreType.DMA((na, 7)), pltpu.SemaphoreType.DMA((na, 7)), pltpu.SemaphoreType.DMA((na,))],
        compiler_params=_cparams(),
    )(*parts)


def _gather_row(name, vec, width):
    n = vec.shape[0]
    rows = jnp.pad(vec, (0, width - n)).reshape(SUBLANES_F32, width // SUBLANES_F32)
    return _allgather_small(name, rows).reshape(N_DEV, width)[:, :n]


def _layer_fwd(x, mod, w):
    sh1, sc1, g1, sh2, sc2, g2 = [mod[i:i + 1] for i in range(N_MOD)]
    S = x.shape[0]
    h1 = _normmod_fwd(x, w["norm1_w"], sc1, sh1)
    proj = _matmul("proj_in", h1, w["w_in"], "nn")
    a1, a_out = _conv_a_fwd(proj, w["conv_a_w"], w["conv_a_b"], w["ln_a_w"], w["ln_a_b"])
    qkv = proj[:, :3 * ATT_W].astype(BF16)
    os, ls = [], []
    for dil in DILATIONS:
        o_b, l_b = _attn_fwd(_to_sub_heads(qkv, dil), dil)
        os.append(_from_sub_heads(o_b))
        ls.append(_from_sub(l_b))
    att, lse = _attn_merge(os, ls)
    of, st_f = _hgrn_fwd(proj, w["lb_f"], ZF, False)
    ob, st_b = _hgrn_fwd(proj, w["lb_b"], ZB, True)
    rec = _hgrn_post_fwd(of, ob, proj, w["rec_norm_w"])
    mixed = jnp.concatenate([att.astype(BF16), rec, a_out], axis=1)
    y1 = _matmul("proj_out", mixed, w["w_out"], "nn")
    x2 = _gate_add(x, y1, g1)
    h2 = _normmod_fwd(x2, w["norm2_w"], sc2, sh2)
    u = _matmul("ffn_up", h2, w["w_up"], "nn")
    act = _ffn_mid_fwd(u, w["conv_f_w"])
    y2 = _matmul("ffn_down", act, w["w_down"], "nn")
    x3 = _gate_add(x2, y2, g2)
    saved = dict(x=x, h1=h1, proj=proj, a1=a1, qkv=qkv, att=att, lse=lse, of=of, ob=ob, st_f=st_f, st_b=st_b,
                 mixed=mixed, y1=y1, x2=x2, h2=h2, u=u, act=act, y2=y2)
    return x3, saved


def _layer_bwd(gx3, mod, w, s):
    sh1, sc1, g1, sh2, sc2, g2 = [mod[i:i + 1] for i in range(N_MOD)]
    S = gx3.shape[0]
    g = {}
    gy2, gg2 = _gate_bwd(gx3, s["y2"], g2)
    gact = _matmul("ffn_down_dx", gy2, w["w_down"], "nt")
    g["w_down"] = _matmul("ffn_down_dw", s["act"], gy2, "tn")
    gu_g, gu_v, gcw_g, gcw_v = _ffn_mid_bwd(s["u"], gact, w["conv_f_w"])
    gu = jnp.concatenate([gu_g, gu_v], axis=1)
    g["conv_f_w"] = jnp.concatenate([gcw_g, gcw_v], axis=1)
    gh2 = _matmul("ffn_up_dx", gu, w["w_up"], "nt")
    g["w_up"] = _matmul("ffn_up_dw", s["h2"], gu, "tn")
    gx2, g["norm2_w"], gsc2, gsh2 = _normmod_bwd(s["x2"], gh2, gx3, w["norm2_w"], sc2, sh2)
    gy1, gg1 = _gate_bwd(gx2, s["y1"], g1)
    gmixed = _matmul("proj_out_dx", gy1, w["w_out"], "nt")
    g["w_out"] = _matmul("proj_out_dw", s["mixed"], gy1, "tn")
    go, ggr, g["rec_norm_w"] = _hgrn_post_bwd(s["of"], s["ob"], s["proj"], gmixed, w["rec_norm_w"])
    gq_f, gz_f, gv_f, g["lb_f"] = _hgrn_bwd(s["proj"], w["lb_f"], s["st_f"], go, ZF, False)
    gq_b, gz_b, gv_b, g["lb_b"] = _hgrn_bwd(s["proj"], w["lb_b"], s["st_b"], go, ZB, True)
    do, ld = _attn_bwd_prep(gmixed, s["att"], s["lse"])
    gqkv = [[], [], []]
    for dil in DILATIONS:
        grads = _attn_bwd(_to_sub_heads(s["qkv"], dil), _to_sub_heads(do, dil), _to_sub(ld, dil), dil)
        for lst, t in zip(gqkv, grads):
            lst.append(_from_sub_heads(t))
    gq_a, gk_a, gv_a = [_sum3_bf16(lst, S, ATT_W) for lst in gqkv]
    gav, gag, gcw, g["conv_a_b"], g["ln_a_w"], g["ln_a_b"] = _conv_a_bwd(
        s["proj"], s["a1"], gmixed, w["conv_a_w"], w["ln_a_w"], w["ln_a_b"])
    g["conv_a_w"] = gcw[:CONV_W]
    gproj = jnp.concatenate([gq_a, gk_a, gv_a, (gq_f + gq_b).astype(BF16), gz_f.astype(BF16), gz_b.astype(BF16),
                             (gv_f + gv_b).astype(BF16), ggr.astype(BF16), gav, gag], axis=1)
    gh1 = _matmul("proj_in_dx", gproj, w["w_in"], "nt")
    g["w_in"] = _matmul("proj_in_dw", s["h1"], gproj, "tn")
    gx, g["norm1_w"], gsc1, gsh1 = _normmod_bwd(s["x"], gh1, gx2, w["norm1_w"], sc1, sh1)
    gmod = jnp.concatenate([gsh1, gsc1, gg1, gsh2, gsc2, gg2], axis=0)
    return gx, gmod, g


def _permute_in_cols(t):
    return jnp.concatenate([t[..., CONV_COLS:], t[..., :CONV_COLS]], axis=-1)


def _unpermute_in_cols(t):
    return jnp.concatenate([t[..., IN_COLS - CONV_COLS:], t[..., :IN_COLS - CONV_COLS]], axis=-1)


def _cols_from_gathered(t, lead):
    nd = t.ndim
    perm = tuple(range(1, nd - 1)) + (0, nd - 1)
    t = t.transpose(perm)
    return t.reshape(t.shape[:-2] + (t.shape[-2] * t.shape[-1],))


def _cols_to_parts(t):
    L, R, C = t.shape
    return t.reshape(L * R, N_DEV, C // N_DEV).transpose(1, 0, 2)


SMALL_REPL = (("norm1_w", D), ("conv_a_b", CONV_CH), ("ln_a_w", CONV_CH), ("ln_a_b", CONV_CH),
              ("rec_norm_w", REC_W), ("norm2_w", D))


def kernel(x, c, w_ada, b_ada, norm1_w, w_in, conv_a_w, conv_a_b, ln_a_w, ln_a_b, lb_gamma, rec_norm_w, w_out, norm2_w, w_up, conv_f_w, w_down, final_norm_w, loss_target, m_w_ada, m_b_ada, m_norm1_w, m_w_in, m_conv_a_w, m_conv_a_b, m_ln_a_w, m_ln_a_b, m_lb_gamma, m_rec_norm_w, m_w_out, m_norm2_w, m_w_up, m_conv_f_w, m_w_down, m_final_norm_w, v_w_ada, v_b_ada, v_norm1_w, v_w_in, v_conv_a_w, v_conv_a_b, v_ln_a_w, v_ln_a_b, v_lb_gamma, v_rec_norm_w, v_w_out, v_norm2_w, v_w_up, v_conv_f_w, v_w_down, v_final_norm_w):
    px, py, pc = _place()
    me = _index_of((px, py, pc))
    xs, tgt = x[0], loss_target[0]
    S = xs.shape[0]
    ada_cols = w_ada.shape[2]

    big = [w_in.reshape(DEPTH * D, -1), w_up.reshape(DEPTH * D, -1), w_out.reshape(-1, D), w_down.reshape(-1, D)]
    g_in, g_up, g_out, g_down = _allgather_big("gather_weights", [t.astype(BF16) for t in big])
    w_in_f = _permute_in_cols(_cols_from_gathered(g_in.reshape(N_DEV, DEPTH, D, -1), 1))
    w_up_f = _cols_from_gathered(g_up.reshape(N_DEV, DEPTH, D, -1), 1)
    w_out_f = g_out.reshape(N_DEV, DEPTH, D // N_DEV, D).transpose(1, 0, 2, 3).reshape(DEPTH, D, D)
    w_out_f = jnp.concatenate([w_out_f[:, CONV_CH:], w_out_f[:, :CONV_CH]], axis=1)
    w_down_f = g_down.reshape(N_DEV, DEPTH, D_FF // N_DEV, D).transpose(1, 0, 2, 3).reshape(DEPTH, D_FF, D)

    small_in = jnp.concatenate([c.reshape(-1), conv_a_w.reshape(-1), lb_gamma.reshape(-1), conv_f_w.reshape(-1)])
    gs = _gather_row("gather_small", small_in, 8192)
    o1 = D
    o2 = o1 + conv_a_w.size
    o3 = o2 + lb_gamma.size
    c_all = gs[:, :o1]
    conv_a_f = _cols_from_gathered(gs[:, o1:o2].reshape(N_DEV, DEPTH, CONV_W, -1), 1)
    lb_gamma_f = _cols_from_gathered(gs[:, o2:o3].reshape(N_DEV, DEPTH, 2, -1), 1)
    conv_f_f = _cols_from_gathered(gs[:, o3:].reshape(N_DEV, DEPTH, FFN_CONV_W, -1), 1)
    conv_a_pad = jnp.pad(conv_a_f, ((0, 0), (0, CONV_W_PAD - CONV_W), (0, 0)))

    b_loc = lax.dynamic_slice_in_dim(b_ada, me * ada_cols, ada_cols, axis=1)

    def mod_fn(c_all_, w_, b_):
        cond = c_all_ * jax.nn.sigmoid(c_all_)
        return (jnp.concatenate([_dot_nn(cond, w_[l], HP) + b_[l] for l in range(DEPTH)], axis=1),)

    (mod_part,) = _vmem_call("ada_mod", mod_fn, [c_all, w_ada, b_loc[:, None, :]], [((N_DEV, DEPTH * ada_cols), F32)])
    gm = _allgather_small("gather_mod", mod_part).reshape(N_DEV, N_DEV, DEPTH, ada_cols)
    mod = lax.dynamic_index_in_dim(gm, me, axis=1, keepdims=False)
    mod = mod.transpose(1, 0, 2).reshape(DEPTH, N_MOD, D)

    (lb1,) = _vmem_call("lower_bounds", lambda a, b: (_lower_bounds_f(a, b),), [lb_gamma_f[0], lb_gamma_f[1]], [((2, REC_W), F32)])
    lb4 = jnp.concatenate([jnp.zeros_like(lb1), lb1], axis=0)

    def layer_weights(l):
        row = lambda t: t[l].reshape(1, -1)
        return dict(norm1_w=row(norm1_w), w_in=w_in_f[l], conv_a_w=conv_a_pad[l], conv_a_b=row(conv_a_b), ln_a_w=row(ln_a_w),
                    ln_a_b=row(ln_a_b), lb_f=lb4[2 * l:2 * l + 1], lb_b=lb4[2 * l + 1:2 * l + 2], rec_norm_w=row(rec_norm_w),
                    w_out=w_out_f[l], norm2_w=row(norm2_w), w_up=w_up_f[l], conv_f_w=conv_f_f[l], w_down=w_down_f[l])

    ws = [layer_weights(l) for l in range(DEPTH)]
    h, saved = xs, []
    for l in range(DEPTH):
        h, s = _layer_fwd(h, mod[l], ws[l])
        saved.append(s)
    gh, g_final, loss_row = _loss_head(h, tgt, final_norm_w.reshape(1, D))
    loss = lax.psum(loss_row[0, 0], ("x", "y", "c"))
    gmods, gws = [None] * DEPTH, [None] * DEPTH
    for l in reversed(range(DEPTH)):
        gh, gmods[l], gws[l] = _layer_bwd(gh, mod[l], ws[l], saved[l])
    grad_x = gh[None]

    glb1 = jnp.concatenate([gws[1]["lb_f"], gws[1]["lb_b"]], axis=0)

    def lb_bwd_fn(a, b, g1):
        _, vjp = jax.vjp(_lower_bounds_f, a, b)
        return vjp(g1)

    g_lb_gamma = jnp.stack(_vmem_call("lower_bounds_bwd", lb_bwd_fn, [lb_gamma_f[0], lb_gamma_f[1], glb1], [((2, REC_W), F32)] * 2))
    pieces = [jnp.stack(gmods).reshape(-1)]
    for l in range(DEPTH):
        pieces += [gws[l][n].reshape(-1) for n, _ in SMALL_REPL]
    pieces += [g_final.reshape(-1)]
    pieces += [jnp.stack([gws[l]["conv_a_w"] for l in range(DEPTH)]).reshape(-1), g_lb_gamma.reshape(-1),
               jnp.stack([gws[l]["conv_f_w"] for l in range(DEPTH)]).reshape(-1)]
    small_g = jnp.concatenate(pieces)
    n_small = small_g.shape[0]
    gsm = _gather_row("gather_small_grads", small_g, 71680)
    n_mod = DEPTH * N_MOD * D
    gmod_all = gsm[:, :n_mod].reshape(N_DEV, DEPTH, N_MOD * D)
    gmod_loc = lax.dynamic_slice_in_dim(gmod_all, me * ada_cols, ada_cols, axis=2).transpose(1, 0, 2)

    def small_fn(gsm_, c_all_, gm_):
        cond = c_all_ * jax.nn.sigmoid(c_all_)
        gw = jnp.concatenate([_dot_tn(cond, gm_[l], HP) for l in range(DEPTH)], axis=0)
        return jnp.sum(gsm_, axis=0, keepdims=True), gw

    tot, g_w_ada = _vmem_call("small_grads", small_fn, [gsm, c_all, gmod_loc],
                              [((1, n_small), F32), ((DEPTH * D, ada_cols), F32)])
    tot = tot[0]
    grads = {"w_ada": g_w_ada.reshape(DEPTH, D, ada_cols), "b_ada": tot[:n_mod].reshape(DEPTH, N_MOD * D)}
    pos = n_mod
    per_layer = {n: [] for n, _ in SMALL_REPL}
    for l in range(DEPTH):
        for n, width in SMALL_REPL:
            per_layer[n].append(tot[pos:pos + width])
            pos += width
    for n, _ in SMALL_REPL:
        grads[n] = jnp.stack(per_layer[n])
    grads["final_norm_w"] = tot[pos:pos + D]
    pos += D
    n_ca, n_lb, n_cf = DEPTH * CONV_W * CONV_CH, DEPTH * 2 * REC_W, DEPTH * FFN_CONV_W * 2 * D_FF
    g_ca = tot[pos:pos + n_ca].reshape(DEPTH, CONV_W, CONV_CH)
    g_lb = tot[pos + n_ca:pos + n_ca + n_lb].reshape(DEPTH, 2, REC_W)
    g_cf = tot[pos + n_ca + n_lb:pos + n_ca + n_lb + n_cf].reshape(DEPTH, FFN_CONV_W, 2 * D_FF)
    grads["conv_a_w"] = lax.dynamic_slice_in_dim(g_ca, me * conv_a_w.shape[2], conv_a_w.shape[2], axis=2)
    grads["lb_gamma"] = lax.dynamic_slice_in_dim(g_lb, me * lb_gamma.shape[2], lb_gamma.shape[2], axis=2)
    grads["conv_f_w"] = lax.dynamic_slice_in_dim(g_cf, me * conv_f_w.shape[2], conv_f_w.shape[2], axis=2)

    gw_in = _unpermute_in_cols(jnp.stack([gws[l]["w_in"] for l in range(DEPTH)]))
    gw_up = jnp.stack([gws[l]["w_up"] for l in range(DEPTH)])
    gw_out = jnp.stack([gws[l]["w_out"] for l in range(DEPTH)])
    gw_out = jnp.concatenate([gw_out[:, D - CONV_CH:], gw_out[:, :D - CONV_CH]], axis=1)
    gw_down = jnp.stack([gws[l]["w_down"] for l in range(DEPTH)])
    rows_to_parts = lambda t: t.reshape(DEPTH, N_DEV, -1, D).transpose(1, 0, 2, 3).reshape(N_DEV, -1, D)
    parts = [_cols_to_parts(gw_in), _cols_to_parts(gw_up), rows_to_parts(gw_out), rows_to_parts(gw_down)]
    r_in, r_up, r_out, r_down = _scatter_parts("scatter_grads", [t.astype(BF16) for t in parts])

    given = dict(w_ada=(w_ada, m_w_ada, v_w_ada), b_ada=(b_ada, m_b_ada, v_b_ada), norm1_w=(norm1_w, m_norm1_w, v_norm1_w),
                 w_in=(w_in, m_w_in, v_w_in), conv_a_w=(conv_a_w, m_conv_a_w, v_conv_a_w), conv_a_b=(conv_a_b, m_conv_a_b, v_conv_a_b),
                 ln_a_w=(ln_a_w, m_ln_a_w, v_ln_a_w), ln_a_b=(ln_a_b, m_ln_a_b, v_ln_a_b), lb_gamma=(lb_gamma, m_lb_gamma, v_lb_gamma),
                 rec_norm_w=(rec_norm_w, m_rec_norm_w, v_rec_norm_w), w_out=(w_out, m_w_out, v_w_out),
                 norm2_w=(norm2_w, m_norm2_w, v_norm2_w), w_up=(w_up, m_w_up, v_w_up), conv_f_w=(conv_f_w, m_conv_f_w, v_conv_f_w),
                 w_down=(w_down, m_w_down, v_w_down), final_norm_w=(final_norm_w, m_final_norm_w, v_final_norm_w))
    big_parts = dict(w_in=r_in, w_up=r_up, w_out=r_out, w_down=r_down)
    names = list(given)
    res = {}
    for n in names:
        w_, m_, v_ = given[n]
        shape = w_.shape
        C = shape[-1]
        two_d = lambda t: t.reshape(-1, C)
        gp = big_parts[n] if n in big_parts else two_d(grads[n])[None]
        res[n] = [t.reshape(shape) for t in _adamw("adamw_" + n, two_d(w_), two_d(m_), two_d(v_), gp)]
    return (loss, grad_x, *[res[n][0] for n in names], *[res[n][1] for n in names],
            *[res[n][2] for n in names], *[res[n][3] for n in names])
```

```python
import functools

import numpy as np
import jax
import jax.numpy as jnp
from jax import lax
from jax.experimental import pallas as pl
from jax.experimental.pallas import tpu as pltpu

F32 = jnp.float32
BF16 = jnp.bfloat16
HP = lax.Precision.HIGHEST
MESH = pl.DeviceIdType.MESH

N_DEV = 8
D = 1024
DEPTH = 2
CONV_CH = 256
CONV_W = 31
CONV_W_PAD = 32
ATT_W = 384
REC_W = 384
N_HEADS = 6
HEAD = 64
HEAD_SHIFT = 6
HALF_BAND = 64
ATT_BLK = 128
DILATIONS = (1, 4, 16)
ALIBI_SLOPES = tuple(float(2.0 ** (-8.0 * (h + 1) / N_HEADS)) for h in range(N_HEADS))
MASK_VALUE = -1e30
REC_CHUNK = 64
EXP_CLAMP = 80.0
F_TINY = 1e-30
IN_COLS = 3584
IN_COLS_PAD = 3840
PROJ_BLOCKS = IN_COLS_PAD // 384
D_FF = 2816
FFN_CONV_W = 3
N_MOD = 6
EPS = 1e-6
ADAM_LR, ADAM_B1, ADAM_B2, ADAM_EPS, ADAM_WD, ADAM_STEP = 0.001, 0.9, 0.999, 1e-08, 0.01, 10

VMEM_LIMIT_BYTES = 56 * 1024 * 1024
SUBLANES_F32 = 8
LANES = 128

QA, KA, VA, QR, ZF, ZB, IR, GR = range(8)
AV_BLK, AG_BLK = 12, 13
CONV_COLS = 2 * CONV_CH


def _cparams(sem=None):
    kw = dict(vmem_limit_bytes=VMEM_LIMIT_BYTES)
    if sem is not None:
        kw["dimension_semantics"] = sem
    return pltpu.CompilerParams(**kw)


def _iota(shape, dim):
    return lax.broadcasted_iota(jnp.int32, shape, dim)


def _dot(a, b, dims, precision=None):
    return lax.dot_general(a, b, (dims, ((), ())), precision=precision, preferred_element_type=F32)


def _dot_nn(a, b, precision=None):
    return _dot(a, b, ((1,), (0,)), precision)


def _dot_nt(a, b, precision=None):
    return _dot(a, b, ((1,), (1,)), precision)


def _dot_tn(a, b, precision=None):
    return _dot(a, b, ((0,), (0,)), precision)


def _c0(j):
    return 0


def _pick(n, cands):
    for c in cands:
        if n % c == 0:
            return c
    return n


MATMUL_OUT_TILE_BYTES = 8 * 1024 * 1024


def _div_lanes(n, cap):
    best = None
    for d in range(LANES, min(n, cap) + 1, LANES):
        if n % d == 0:
            best = d
    return best if best is not None else n


def _matmul_tiles(mode, M, N, K):
    if mode == "nn":
        tm = _pick(M, (1024, 512, 256, 128))
        return tm, _div_lanes(N, MATMUL_OUT_TILE_BYTES // (4 * tm)), K
    if mode == "nt":
        return _pick(M, (512, 256, 128)), N, K
    tm = _div_lanes(M, 1408)
    return tm, _div_lanes(N, MATMUL_OUT_TILE_BYTES // (4 * tm)), _pick(K, (1024, 512, 256))


def _matmul(name, a, b, mode, out_dtype=F32):
    if mode == "nn":
        (M, K), (_, N) = a.shape, b.shape
    elif mode == "nt":
        (M, K), (N, _) = a.shape, b.shape
    else:
        (K, M), (_, N) = a.shape, b.shape
    tm, tn, tk = _matmul_tiles(mode, M, N, K)
    nk = K // tk
    if mode == "nn":
        a_spec = pl.BlockSpec((tm, tk), lambda i, j, k: (i, k))
        b_spec = pl.BlockSpec((tk, tn), lambda i, j, k: (k, j))
        dims = ((1,), (0,))
    elif mode == "nt":
        a_spec = pl.BlockSpec((tm, tk), lambda i, j, k: (i, k))
        b_spec = pl.BlockSpec((tn, tk), lambda i, j, k: (j, k))
        dims = ((1,), (1,))
    else:
        a_spec = pl.BlockSpec((tk, tm), lambda i, j, k: (k, i))
        b_spec = pl.BlockSpec((tk, tn), lambda i, j, k: (k, j))
        dims = ((0,), (0,))

    def body_whole(a_ref, b_ref, o_ref):
        o_ref[...] = _dot(a_ref[...].astype(BF16), b_ref[...].astype(BF16), dims).astype(o_ref.dtype)

    def body(a_ref, b_ref, o_ref, acc_ref):
        k = pl.program_id(2)
        part = _dot(a_ref[...].astype(BF16), b_ref[...].astype(BF16), dims)

        @pl.when(k == 0)
        def _():
            acc_ref[...] = part

        @pl.when(k > 0)
        def _():
            acc_ref[...] += part

        @pl.when(k == nk - 1)
        def _():
            o_ref[...] = acc_ref[...].astype(o_ref.dtype)

    return pl.pallas_call(
        body_whole if nk == 1 else body, name=name, grid=(M // tm, N // tn, nk),
        in_specs=[a_spec, b_spec],
        out_specs=pl.BlockSpec((tm, tn), lambda i, j, k: (i, j)),
        out_shape=jax.ShapeDtypeStruct((M, N), out_dtype),
        scratch_shapes=[] if nk == 1 else [pltpu.VMEM((tm, tn), F32)],
        compiler_params=_cparams(("parallel", "parallel", "arbitrary")),
    )(a, b)


def _rowwise(name, fn, S, ts, tiles, params=(), outs=(), accs=(), halo=0, ncb=1):
    in_specs, args = [], []
    for arr, w, jm, with_halo in tiles:
        if with_halo:
            hb, nhb = ts // halo, S // halo
            in_specs += [
                pl.BlockSpec((halo, w), lambda j, i, jm=jm, hb=hb: (jnp.maximum(i * hb - 1, 0), jm(j))),
                pl.BlockSpec((ts, w), lambda j, i, jm=jm: (i, jm(j))),
                pl.BlockSpec((halo, w), lambda j, i, jm=jm, hb=hb, nhb=nhb: (jnp.minimum((i + 1) * hb, nhb - 1), jm(j))),
            ]
            args += [arr, arr, arr]
        else:
            in_specs.append(pl.BlockSpec((ts, w), lambda j, i, jm=jm: (i, jm(j))))
            args.append(arr)
    for arr, r, w, jm in params:
        in_specs.append(pl.BlockSpec((r, w), lambda j, i, jm=jm: (0, jm(j))))
        args.append(arr)
    out_specs, out_shape = [], []
    for w, dt, jm, tw in outs:
        out_specs.append(pl.BlockSpec((ts, w), lambda j, i, jm=jm: (i, jm(j))))
        out_shape.append(jax.ShapeDtypeStruct((S, tw), dt))
    for r, w, jm, tw in accs:
        out_specs.append(pl.BlockSpec((r, w), lambda j, i, jm=jm: (0, jm(j))))
        out_shape.append(jax.ShapeDtypeStruct((r, tw), F32))
    n_tiles, n_params, n_outs = len(tiles), len(params), len(outs)

    def body(*refs):
        i = pl.program_id(1)
        pos, vals = 0, []
        for _, w, _, with_halo in tiles:
            if with_halo:
                before, after = refs[pos][...], refs[pos + 2][...]
                before = jnp.where(i > 0, before, jnp.zeros_like(before))
                after = jnp.where(i < S // ts - 1, after, jnp.zeros_like(after))
                vals.append(jnp.concatenate([before, refs[pos + 1][...], after], axis=0))
                pos += 3
            else:
                vals.append(refs[pos][...])
                pos += 1
        prefs = refs[pos:pos + n_params]
        orefs = refs[pos + n_params:pos + n_params + n_outs]
        arefs = refs[pos + n_params + n_outs:]

        @pl.when(i == 0)
        def _():
            for r in arefs:
                r[...] = jnp.zeros_like(r)

        fn(i, vals, prefs, orefs, arefs)

    res = pl.pallas_call(
        body, name=name, grid=(ncb, S // ts),
        in_specs=in_specs, out_specs=out_specs, out_shape=out_shape,
        compiler_params=_cparams(("arbitrary", "arbitrary")),
    )(*args)
    return res


def _vmem_call(name, fn, ins, out_shapes):
    n_in = len(ins)

    def body(*refs):
        vals = fn(*[r[...] for r in refs[:n_in]])
        for r, v in zip(refs[n_in:], vals):
            r[...] = v.astype(r.dtype)

    return pl.pallas_call(
        body, name=name,
        out_shape=[jax.ShapeDtypeStruct(s, dt) for s, dt in out_shapes],
        compiler_params=_cparams(),
    )(*ins)


def _rms(x, w):
    return x * lax.rsqrt(jnp.mean(x * x, axis=-1, keepdims=True) + EPS) * w


def _normmod_f(x, nw, sc, sh):
    return _rms(x, nw) * (1.0 + sc) + sh


def _row_params(*vecs):
    return [(v, 1, v.shape[1], _c0) for v in vecs]


def _normmod_fwd(x, nw, sc, sh):
    S = x.shape[0]

    def fn(i, vals, p, o, a):
        o[0][...] = _normmod_f(vals[0], p[0][...], p[1][...], p[2][...]).astype(BF16)

    return _rowwise("normmod_fwd", fn, S, 512, [(x, D, _c0, False)], _row_params(nw, sc, sh), [(D, BF16, _c0, D)])[0]


def _normmod_bwd(x, gh, gres, nw, sc, sh):
    S = x.shape[0]

    def fn(i, vals, p, o, a):
        _, vjp = jax.vjp(_normmod_f, vals[0], p[0][...], p[1][...], p[2][...])
        gx, gnw, gsc, gsh = vjp(vals[1])
        o[0][...] = gx + vals[2]
        a[0][...] += gnw
        a[1][...] += gsc
        a[2][...] += gsh

    return _rowwise("normmod_bwd", fn, S, 256, [(x, D, _c0, False), (gh, D, _c0, False), (gres, D, _c0, False)],
                    _row_params(nw, sc, sh), [(D, F32, _c0, D)], [(1, D, _c0, D)] * 3)


def _gate_add(x, y, g):
    S = x.shape[0]

    def fn(i, vals, p, o, a):
        o[0][...] = vals[0] + p[0][...] * vals[1]

    return _rowwise("gate_add", fn, S, 512, [(x, D, _c0, False), (y, D, _c0, False)], _row_params(g), [(D, F32, _c0, D)])[0]


def _gate_bwd(gx, y, g):
    S = gx.shape[0]

    def fn(i, vals, p, o, a):
        o[0][...] = (vals[0] * p[0][...]).astype(BF16)
        a[0][...] += jnp.sum(vals[0] * vals[1], axis=0, keepdims=True)

    return _rowwise("gate_bwd", fn, S, 512, [(gx, D, _c0, False), (y, D, _c0, False)], _row_params(g),
                    [(D, BF16, _c0, D)], [(1, D, _c0, D)])


def _loss_head(x, tgt, fw):
    S = x.shape[0]

    def fn(i, vals, p, o, a):
        y, vjp = jax.vjp(_rms, vals[0], p[0][...])
        err = y - vals[1]
        gx, gfw = vjp(err * (1.0 / D))
        o[0][...] = gx
        a[0][...] += gfw
        part = 0.5 * jnp.sum(jnp.mean(err * err, axis=-1, keepdims=True), axis=0, keepdims=True)
        a[1][...] += jnp.broadcast_to(part, (1, LANES))

    return _rowwise("loss_head", fn, S, 256, [(x, D, _c0, False), (tgt, D, _c0, False)], _row_params(fw),
                    [(D, F32, _c0, D)], [(1, D, _c0, D), (1, LANES, _c0, LANES)])


CONV_HALO = 16
CONV_TS = 512


def _shifted(ext, shift, ts, halo):
    n = ext.shape[0]
    s = shift % n
    r = ext if s == 0 else pltpu.roll(ext, s, 0)
    return r[halo:halo + ts]


def _ln_silu(a, w, b):
    mu = jnp.mean(a, axis=-1, keepdims=True)
    var = jnp.mean(jnp.square(a - mu), axis=-1, keepdims=True)
    y = (a - mu) * lax.rsqrt(var + EPS) * w + b
    return y * jax.nn.sigmoid(y)


def _conv_a_fwd(proj, w_pad, b, lnw, lnb):
    S = proj.shape[0]
    ts, H = min(CONV_TS, S), CONV_HALO

    def fn(i, vals, p, o, a):
        a0 = vals[0] * jax.nn.sigmoid(vals[1])
        acc = jnp.zeros((ts, CONV_CH), F32) + p[1][...]
        for k in range(CONV_W):
            acc = acc + _shifted(a0, CONV_W // 2 - k, ts, H) * p[0][pl.ds(k, 1), :]
        o[0][...] = acc
        o[1][...] = _ln_silu(acc, p[2][...], p[3][...]).astype(BF16)

    tiles = [(proj, CONV_CH, lambda j: AV_BLK, True), (proj, CONV_CH, lambda j: AG_BLK, True)]
    params = [(w_pad, CONV_W_PAD, CONV_CH, _c0)] + _row_params(b, lnw, lnb)
    return _rowwise("conv_a_fwd", fn, S, ts, tiles, params, [(CONV_CH, F32, _c0, CONV_CH), (CONV_CH, BF16, _c0, CONV_CH)], halo=H)


def _conv_a_bwd(proj, a1, gmixed, w_pad, lnw, lnb):
    S = proj.shape[0]
    ts, H = min(CONV_TS, S), CONV_HALO

    def fn(i, vals, p, o, a):
        av, ag, a1e, ge = vals
        lw, lb = p[1][...], p[2][...]
        _, vjp_e = jax.vjp(lambda t: _ln_silu(t, lw, lb), a1e)
        (ga1e,) = vjp_e(ge)
        c = slice(H, H + ts)
        _, vjp_c = jax.vjp(_ln_silu, a1e[c], lw, lb)
        ga1, glw, glb = vjp_c(ge[c])
        a[1][...] += jnp.sum(ga1, axis=0, keepdims=True)
        a[2][...] += glw
        a[3][...] += glb
        sg = jax.nn.sigmoid(ag)
        a0 = av * sg
        ga0 = jnp.zeros((ts, CONV_CH), F32)
        for k in range(CONV_W):
            a[0][pl.ds(k, 1), :] += jnp.sum(ga1 * _shifted(a0, CONV_W // 2 - k, ts, H), axis=0, keepdims=True)
            ga0 = ga0 + _shifted(ga1e, k - CONV_W // 2, ts, H) * p[0][pl.ds(k, 1), :]
        sgc, avc = sg[c], av[c]
        o[0][...] = (ga0 * sgc).astype(BF16)
        o[1][...] = (ga0 * avc * sgc * (1.0 - sgc)).astype(BF16)

    tiles = [(proj, CONV_CH, lambda j: AV_BLK, True), (proj, CONV_CH, lambda j: AG_BLK, True),
             (a1, CONV_CH, _c0, True), (gmixed, CONV_CH, lambda j: 3, True)]
    params = [(w_pad, CONV_W_PAD, CONV_CH, _c0)] + _row_params(lnw, lnb)
    outs = [(CONV_CH, BF16, _c0, CONV_CH), (CONV_CH, BF16, _c0, CONV_CH)]
    accs = [(CONV_W_PAD, CONV_CH, _c0, CONV_CH)] + [(1, CONV_CH, _c0, CONV_CH)] * 3
    return _rowwise("conv_a_bwd", fn, S, ts, tiles, params, outs, accs, halo=H)


FFN_HALO = 8
FFN_TS = 512
FFN_CB = 256
FFN_NCB = D_FF // FFN_CB


def _gelu_mul(g, v):
    return 0.5 * g * (1.0 + lax.erf(g * (2.0 ** -0.5))) * v


def _ffn_mid_fwd(u, cw):
    S = u.shape[0]
    ts, H = min(FFN_TS, S), FFN_HALO

    def conv(ext, w_ref):
        acc = jnp.zeros((ts, FFN_CB), F32)
        for k in range(FFN_CONV_W):
            acc = acc + _shifted(ext, 1 - k, ts, H) * w_ref[pl.ds(k, 1), :]
        return acc

    def fn(i, vals, p, o, a):
        o[0][...] = _gelu_mul(conv(vals[0], p[0]), conv(vals[1], p[1])).astype(BF16)

    tiles = [(u, FFN_CB, lambda j: j, True), (u, FFN_CB, lambda j: j + FFN_NCB, True)]
    params = [(cw, FFN_CONV_W, FFN_CB, lambda j: j), (cw, FFN_CONV_W, FFN_CB, lambda j: j + FFN_NCB)]
    return _rowwise("ffn_mid_fwd", fn, S, ts, tiles, params, [(FFN_CB, BF16, lambda j: j, D_FF)], halo=H, ncb=FFN_NCB)[0]


def _ffn_mid_bwd(u, gact, cw):
    S = u.shape[0]
    ts, H = min(FFN_TS, S), FFN_HALO
    n = ts + 2 * H

    def conv_all(ext, w_ref):
        acc = jnp.zeros((n, FFN_CB), F32)
        for k in range(FFN_CONV_W):
            s = (1 - k) % n
            acc = acc + (ext if s == 0 else pltpu.roll(ext, s, 0)) * w_ref[pl.ds(k, 1), :]
        return acc

    def fn(i, vals, p, o, a):
        ug, uv, ga = vals
        _, vjp = jax.vjp(_gelu_mul, conv_all(ug, p[0]), conv_all(uv, p[1]))
        gcg, gcv = vjp(ga)
        c = slice(H, H + ts)
        for half, (gc, ue) in enumerate(((gcg, ug), (gcv, uv))):
            gu = jnp.zeros((ts, FFN_CB), F32)
            for k in range(FFN_CONV_W):
                gu = gu + _shifted(gc, k - 1, ts, H) * p[half][pl.ds(k, 1), :]
                a[half][pl.ds(k, 1), :] += jnp.sum(gc[c] * _shifted(ue, 1 - k, ts, H), axis=0, keepdims=True)
            o[half][...] = gu.astype(BF16)

    tiles = [(u, FFN_CB, lambda j: j, True), (u, FFN_CB, lambda j: j + FFN_NCB, True), (gact, FFN_CB, lambda j: j, True)]
    params = [(cw, FFN_CONV_W, FFN_CB, lambda j: j), (cw, FFN_CONV_W, FFN_CB, lambda j: j + FFN_NCB)]
    outs = [(FFN_CB, BF16, lambda j: j, D_FF)] * 2
    accs = [(FFN_CONV_W, FFN_CB, lambda j: j, D_FF)] * 2
    return _rowwise("ffn_mid_bwd", fn, S, ts, tiles, params, outs, accs, halo=H, ncb=FFN_NCB)


LD_W = LANES
PAIR_W = 2 * HEAD
N_PAIRS = N_HEADS // 2


def _sub_view(t, d):
    S, C = t.shape
    return t.reshape(S // d, d * C)


def _sub_halo_specs(width, col, blk, hb, nhb):
    per = blk // hb
    return [
        pl.BlockSpec((hb, width), lambda r, i: (jnp.maximum(i * per - 1, 0), col(r))),
        pl.BlockSpec((blk, width), lambda r, i: (i, col(r))),
        pl.BlockSpec((hb, width), lambda r, i: (jnp.minimum((i + 1) * per, nhb - 1), col(r))),
    ]


def _pick_lane(t, lane):
    return jnp.sum(jnp.where(_iota((1, t.shape[1]), 1) == lane, t, 0.0), axis=1, keepdims=True)


def _pair_mask(h2):
    return (_iota((1, PAIR_W), 1) >> HEAD_SHIFT) == h2


def _cat_bf16(a, b, c):
    return jnp.concatenate([a[...], b[...], c[...]], axis=0).astype(BF16)


def _attn_fwd(proj, dil):
    S = proj.shape[0]
    L = S // dil
    blk, hb = min(ATT_BLK, L), HALF_BAND
    span = blk + 2 * hb

    def body(q_ref, kp, kc, kn, vp, vc, vn, o_ref, l_ref):
        i = pl.program_id(1)
        rel = _iota((blk, span), 1) - hb - _iota((blk, span), 0)
        kpos = i * blk - hb + _iota((blk, span), 1)
        valid = (jnp.abs(rel) <= hb) & (kpos >= 0) & (kpos < L)
        dist = jnp.abs(rel).astype(F32) * float(dil)
        q, k, v = q_ref[...].astype(BF16), _cat_bf16(kp, kc, kn), _cat_bf16(vp, vc, vn)
        lse = jnp.zeros((blk, LD_W), F32)
        for pr in range(N_PAIRS):
            sl = slice(pr * PAIR_W, (pr + 1) * PAIR_W)
            qp, kpair, vpair = q[:, sl], k[:, sl], v[:, sl]
            o = jnp.zeros((blk, PAIR_W), F32)
            for h2 in range(2):
                h, mask = 2 * pr + h2, _pair_mask(h2)
                s = _dot_nt(jnp.where(mask, qp, jnp.zeros_like(qp)), kpair) * (HEAD ** -0.5) - ALIBI_SLOPES[h] * dist
                s = jnp.where(valid, s, MASK_VALUE)
                m = jnp.max(s, axis=1, keepdims=True)
                p = jnp.exp(s - m)
                l = jnp.sum(p, axis=1, keepdims=True)
                o = jnp.where(mask, _dot_nn(p.astype(BF16), vpair) / l, o)
                lse = lse + jnp.where(_iota((1, LD_W), 1) == h, m + jnp.log(l), 0.0)
            o_ref[:, sl] = o
        l_ref[...] = lse

    nhb = L // hb
    view = _sub_view(proj, dil)
    in_specs = ([pl.BlockSpec((blk, ATT_W), lambda r, i: (i, r * PROJ_BLOCKS + QA))]
                + _sub_halo_specs(ATT_W, lambda r: r * PROJ_BLOCKS + KA, blk, hb, nhb)
                + _sub_halo_specs(ATT_W, lambda r: r * PROJ_BLOCKS + VA, blk, hb, nhb))
    o, lse = pl.pallas_call(
        body, name=f"attn_fwd_d{dil}", grid=(dil, L // blk), in_specs=in_specs,
        out_specs=[pl.BlockSpec((blk, ATT_W), lambda r, i: (i, r)), pl.BlockSpec((blk, LD_W), lambda r, i: (i, r))],
        out_shape=[jax.ShapeDtypeStruct((L, dil * ATT_W), F32), jax.ShapeDtypeStruct((L, dil * LD_W), F32)],
        compiler_params=_cparams(("parallel", "parallel")),
    )(*([view] * 7))
    return o.reshape(S, ATT_W), lse.reshape(S, LD_W)


def _attn_bwd(proj, do, ld, dil):
    S = proj.shape[0]
    L = S // dil
    blk, hb = min(ATT_BLK, L), HALF_BAND
    span = blk + 2 * hb
    scale = HEAD ** -0.5

    def body(qp, qc, qn, kp, kc, kn, vp, vc, vn, gp, gc, gn, lp, lc, ln, dq_ref, dk_ref, dv_ref):
        i = pl.program_id(1)
        l = lc[...]
        le = jnp.concatenate([lp[...], l, ln[...]], axis=0)
        rel_q = _iota((blk, span), 1) - hb - _iota((blk, span), 0)
        kpos = i * blk - hb + _iota((blk, span), 1)
        valid_q = (jnp.abs(rel_q) <= hb) & (kpos >= 0) & (kpos < L)
        dist_q = jnp.abs(rel_q).astype(F32) * float(dil)
        rel_k = _iota((span, blk), 1) + hb - _iota((span, blk), 0)
        qpos = i * blk - hb + _iota((span, blk), 0)
        valid_k = (jnp.abs(rel_k) <= hb) & (qpos >= 0) & (qpos < L)
        dist_k = jnp.abs(rel_k).astype(F32) * float(dil)
        q_all, k_all, v_all, g_all = qc[...].astype(BF16), kc[...].astype(BF16), vc[...].astype(BF16), gc[...]
        qe_all, ke_all, ve_all = _cat_bf16(qp, qc, qn), _cat_bf16(kp, kc, kn), _cat_bf16(vp, vc, vn)
        ge_all = _cat_bf16(gp, gc, gn)
        for pr in range(N_PAIRS):
            sl = slice(pr * PAIR_W, (pr + 1) * PAIR_W)
            q, k, v, g = q_all[:, sl], k_all[:, sl], v_all[:, sl], g_all[:, sl]
            qe, ke, ve, ge = qe_all[:, sl], ke_all[:, sl], ve_all[:, sl], ge_all[:, sl]
            dq = jnp.zeros((blk, PAIR_W), F32)
            dk = jnp.zeros((blk, PAIR_W), F32)
            dv = jnp.zeros((blk, PAIR_W), F32)
            for h2 in range(2):
                h, mask = 2 * pr + h2, _pair_mask(h2)
                only = lambda t: jnp.where(mask, t, jnp.zeros_like(t))
                s = _dot_nt(only(q), ke) * scale - ALIBI_SLOPES[h] * dist_q
                p = jnp.where(valid_q, jnp.exp(s - _pick_lane(l, h)), 0.0)
                ds = p * (_dot_nt(only(g), ve) - _pick_lane(l, 8 + h))
                dq = jnp.where(mask, _dot_nn(ds.astype(BF16), ke), dq)
                s = _dot_nt(only(qe), k) * scale - ALIBI_SLOPES[h] * dist_k
                p = jnp.where(valid_k, jnp.exp(s - _pick_lane(le, h)), 0.0)
                dv = jnp.where(mask, _dot_tn(p.astype(BF16), ge), dv)
                ds = p * (_dot_nt(only(ge), v) - _pick_lane(le, 8 + h))
                dk = jnp.where(mask, _dot_tn(ds.astype(BF16), qe), dk)
            dq_ref[:, sl] = dq * scale
            dk_ref[:, sl] = dk * scale
            dv_ref[:, sl] = dv

    nhb = L // hb
    pview, gview, lview = _sub_view(proj, dil), _sub_view(do, dil), _sub_view(ld, dil)
    in_specs = (_sub_halo_specs(ATT_W, lambda r: r * PROJ_BLOCKS + QA, blk, hb, nhb)
                + _sub_halo_specs(ATT_W, lambda r: r * PROJ_BLOCKS + KA, blk, hb, nhb)
                + _sub_halo_specs(ATT_W, lambda r: r * PROJ_BLOCKS + VA, blk, hb, nhb)
                + _sub_halo_specs(ATT_W, lambda r: r, blk, hb, nhb) + _sub_halo_specs(LD_W, lambda r: r, blk, hb, nhb))
    o_spec = pl.BlockSpec((blk, ATT_W), lambda r, i: (i, r))
    res = pl.pallas_call(
        body, name=f"attn_bwd_d{dil}", grid=(dil, L // blk), in_specs=in_specs,
        out_specs=[o_spec] * 3, out_shape=[jax.ShapeDtypeStruct((L, dil * ATT_W), F32)] * 3,
        compiler_params=_cparams(("parallel", "parallel")),
    )(*([pview] * 9 + [gview] * 3 + [lview] * 3))
    return [t.reshape(S, ATT_W) for t in res]


def _head_expand(t):
    e = ((_iota((LD_W, ATT_W), 1) >> HEAD_SHIFT) == _iota((LD_W, ATT_W), 0)).astype(F32)
    return _dot_nn(t, e, HP)


def _attn_merge(os, ls):
    S = os[0].shape[0]

    def fn(i, vals, p, o, a):
        o3, l3 = vals[:3], vals[3:]
        m = jnp.maximum(jnp.maximum(l3[0], l3[1]), l3[2])
        e3 = [jnp.exp(l - m) for l in l3]
        den = e3[0] + e3[1] + e3[2]
        out = jnp.zeros((o3[0].shape[0], ATT_W), F32)
        for ob, e in zip(o3, e3):
            out = out + _head_expand(e / den) * ob
        o[0][...] = out
        o[1][...] = m + jnp.log(den)

    tiles = [(t, ATT_W, _c0, False) for t in os] + [(t, LD_W, _c0, False) for t in ls]
    return _rowwise("attn_merge", fn, S, 512, tiles, (), [(ATT_W, F32, _c0, ATT_W), (LD_W, F32, _c0, LD_W)])


def _attn_bwd_prep(gmixed, att, lse):
    S = att.shape[0]

    def fn(i, vals, p, o, a):
        g, out, lse_row = vals
        place_d = ((_iota((ATT_W, LD_W), 0) >> HEAD_SHIFT) + 8 == _iota((ATT_W, LD_W), 1)).astype(F32)
        o[0][...] = g.astype(BF16)
        o[1][...] = jnp.where(_iota((1, LD_W), 1) < 8, lse_row, 0.0) + _dot_nn(g * out, place_d, HP)

    tiles = [(gmixed, ATT_W, _c0, False), (att, ATT_W, _c0, False), (lse, LD_W, _c0, False)]
    return _rowwise("attn_bwd_prep", fn, S, 512, tiles, (), [(ATT_W, BF16, _c0, ATT_W), (LD_W, F32, _c0, LD_W)])


def _sum3_bf16(ts_, S, width):
    def fn(i, vals, p, o, a):
        o[0][...] = (vals[0] + vals[1] + vals[2]).astype(BF16)

    return _rowwise("sum3", fn, S, 512, [(t, width, _c0, False) for t in ts_], (), [(width, BF16, _c0, width)])[0]


def _block_diag_mask():
    return ((_iota((REC_W, REC_W), 0) >> HEAD_SHIFT) == (_iota((REC_W, REC_W), 1) >> HEAD_SHIFT)).astype(F32)


def _rep_heads(t):
    return jnp.concatenate([t] * N_HEADS, axis=0)


def _hgrn_chunk(qr, z, iv, lb, st, reverse):
    C = REC_CHUNK
    r, c = _iota((C, C), 0), _iota((C, C), 1)
    t_cum = (c >= r) if reverse else (c <= r)
    mid_row, last_row = (C // 2, 0) if reverse else (C // 2 - 1, C - 1)
    f = lb + (1.0 - lb) * jax.nn.sigmoid(z)
    logf = jnp.log(jnp.maximum(f, F_TINY))
    k = (1.0 - lb) * jax.nn.sigmoid(-z)
    q = qr * jax.nn.sigmoid(qr)
    b = _dot_nn(t_cum.astype(F32), logf, HP)
    row = _iota((C, 1), 0)
    bm = jnp.sum(jnp.where(row == mid_row, b, 0.0), axis=0, keepdims=True)
    bl = jnp.sum(jnp.where(row == last_row, b, 0.0), axis=0, keepdims=True)
    qt = q * jnp.exp(jnp.minimum(b - bm, EXP_CLAMP))
    kt = k * jnp.exp(jnp.minimum(bm - b, EXP_CLAMP))
    qh = q * jnp.exp(b)
    kh = k * jnp.exp(bl - b)
    lam = jnp.exp(bl)
    bd = ((_iota((PAIR_W, PAIR_W), 0) >> HEAD_SHIFT) == (_iota((PAIR_W, PAIR_W), 1) >> HEAD_SHIFT)).astype(F32)
    s_in = _iota((C, PAIR_W), 1) & (HEAD - 1)
    t_in = _iota((C, PAIR_W), 0)
    tri = (s_in >= t_in) if reverse else (s_in <= t_in)
    twice = lambda t: jnp.concatenate([t, t], axis=0)
    outs, states = [], []
    for pr in range(N_PAIRS):
        sl = slice(pr * PAIR_W, (pr + 1) * PAIR_W)
        k_bd = (twice(kt[:, sl]) * bd).astype(BF16)
        v_bd = (twice(iv[:, sl]) * bd).astype(BF16)
        st_bd = twice(st[:, sl]) * bd
        a = jnp.where(tri, _dot_nt(qt[:, sl].astype(BF16), k_bd), 0.0)
        outs.append(_dot_nn(a.astype(BF16), v_bd) + _dot_nt(qh[:, sl].astype(BF16), st_bd.astype(BF16)))
        kv = _dot_tn(iv[:, sl].astype(BF16), kh[:, sl].astype(BF16))
        st_bd = st_bd * lam[:, sl] + kv * bd
        states.append(st_bd[0:HEAD] + st_bd[HEAD:PAIR_W])
    return jnp.concatenate(outs, axis=1), jnp.concatenate(states, axis=1)


REC_CHUNKS_PER_STEP = 8
REC_ROWS = REC_CHUNKS_PER_STEP * REC_CHUNK


def _hgrn_specs(order, blocks):
    return [pl.BlockSpec((REC_ROWS, REC_W), lambda i, b=b: (order(i), b)) for b in blocks]


def _chunk_rows(j):
    return pl.ds(pl.multiple_of(j * REC_CHUNK, REC_CHUNK), REC_CHUNK)


def _hgrn_fwd(proj, lb, z_blk, reverse):
    S = proj.shape[0]
    nb = S // REC_ROWS
    order = (lambda i: nb - 1 - i) if reverse else (lambda i: i)

    def body(q_ref, z_ref, v_ref, lb_ref, o_ref, st_ref, st_scr):
        @pl.when(pl.program_id(0) == 0)
        def _():
            st_scr[...] = jnp.zeros_like(st_scr)

        def step(t, carry):
            j = REC_CHUNKS_PER_STEP - 1 - t if reverse else t
            rows = _chunk_rows(j)
            st = st_scr[...]
            st_ref[j] = st
            o, st_new = _hgrn_chunk(q_ref[rows, :], z_ref[rows, :], v_ref[rows, :], lb_ref[...], st, reverse)
            o_ref[rows, :] = o
            st_scr[...] = st_new
            return carry

        lax.fori_loop(0, REC_CHUNKS_PER_STEP, step, 0)

    return pl.pallas_call(
        body, name="hgrn_rev_fwd" if reverse else "hgrn_fwd_fwd", grid=(nb,),
        in_specs=_hgrn_specs(order, (QR, z_blk, IR)) + [pl.BlockSpec((1, REC_W), lambda i: (0, 0))],
        out_specs=[pl.BlockSpec((REC_ROWS, REC_W), lambda i: (order(i), 0)),
                   pl.BlockSpec((REC_CHUNKS_PER_STEP, HEAD, REC_W), lambda i: (order(i), 0, 0))],
        out_shape=[jax.ShapeDtypeStruct((S, REC_W), F32), jax.ShapeDtypeStruct((S // REC_CHUNK, HEAD, REC_W), F32)],
        scratch_shapes=[pltpu.VMEM((HEAD, REC_W), F32)],
        compiler_params=_cparams(("arbitrary",)),
    )(proj, proj, proj, lb)


def _hgrn_bwd(proj, lb, states, go, z_blk, reverse):
    S = proj.shape[0]
    nb = S // REC_ROWS
    order = (lambda i: i) if reverse else (lambda i: nb - 1 - i)

    def body(q_ref, z_ref, v_ref, lb_ref, st_ref, go_ref, gq_ref, gz_ref, gv_ref, glb_ref, gst_scr):
        @pl.when(pl.program_id(0) == 0)
        def _():
            gst_scr[...] = jnp.zeros_like(gst_scr)
            glb_ref[...] = jnp.zeros_like(glb_ref)

        chunk = functools.partial(_hgrn_chunk, reverse=reverse)

        def step(t, carry):
            j = t if reverse else REC_CHUNKS_PER_STEP - 1 - t
            rows = _chunk_rows(j)
            _, vjp = jax.vjp(chunk, q_ref[rows, :], z_ref[rows, :], v_ref[rows, :], lb_ref[...], st_ref[j])
            gq, gz, gv, glb, gst = vjp((go_ref[rows, :], gst_scr[...]))
            gq_ref[rows, :] = gq
            gz_ref[rows, :] = gz
            gv_ref[rows, :] = gv
            glb_ref[...] += glb
            gst_scr[...] = gst
            return carry

        lax.fori_loop(0, REC_CHUNKS_PER_STEP, step, 0)

    row_spec = pl.BlockSpec((REC_ROWS, REC_W), lambda i: (order(i), 0))
    return pl.pallas_call(
        body, name="hgrn_rev_bwd" if reverse else "hgrn_fwd_bwd", grid=(nb,),
        in_specs=(_hgrn_specs(order, (QR, z_blk, IR)) + [pl.BlockSpec((1, REC_W), lambda i: (0, 0))]
                  + [pl.BlockSpec((REC_CHUNKS_PER_STEP, HEAD, REC_W), lambda i: (order(i), 0, 0)), row_spec]),
        out_specs=[row_spec] * 3 + [pl.BlockSpec((1, REC_W), lambda i: (0, 0))],
        out_shape=[jax.ShapeDtypeStruct((S, REC_W), F32)] * 3 + [jax.ShapeDtypeStruct((1, REC_W), F32)],
        scratch_shapes=[pltpu.VMEM((HEAD, REC_W), F32)],
        compiler_params=_cparams(("arbitrary",)),
    )(proj, proj, proj, lb, states, go)


def _hgrn_post_f(of, ob, gr, rnw):
    o = of + ob
    ms = _dot_nn(o * o, _block_diag_mask() * (1.0 / HEAD), HP)
    return o * lax.rsqrt(ms + EPS) * rnw * (gr * jax.nn.sigmoid(gr))


def _hgrn_post_fwd(of, ob, proj, rnw):
    S = of.shape[0]

    def fn(i, vals, p, o, a):
        o[0][...] = _hgrn_post_f(vals[0], vals[1], vals[2], p[0][...]).astype(BF16)

    tiles = [(of, REC_W, _c0, False), (ob, REC_W, _c0, False), (proj, REC_W, lambda j: GR, False)]
    return _rowwise("hgrn_post_fwd", fn, S, 512, tiles, _row_params(rnw), [(REC_W, BF16, _c0, REC_W)])[0]


def _hgrn_post_bwd(of, ob, proj, gmixed, rnw):
    S = of.shape[0]

    def fn(i, vals, p, o, a):
        _, vjp = jax.vjp(_hgrn_post_f, vals[0], vals[1], vals[2], p[0][...])
        go, _, ggr, grnw = vjp(vals[3])
        o[0][...] = go
        o[1][...] = ggr
        a[0][...] += grnw

    tiles = [(of, REC_W, _c0, False), (ob, REC_W, _c0, False), (proj, REC_W, lambda j: GR, False),
             (gmixed, REC_W, lambda j: 1, False)]
    return _rowwise("hgrn_post_bwd", fn, S, 256, tiles, _row_params(rnw),
                    [(REC_W, F32, _c0, REC_W), (REC_W, F32, _c0, REC_W)], [(1, REC_W, _c0, REC_W)])


def _lower_bounds_f(g0, g1):
    m = jnp.maximum(g0, g1)
    e0, e1 = jnp.exp(g0 - m), jnp.exp(g1 - m)
    return e1 / (e0 + e1)


def _adamw(name, w, m, v, gparts):
    R, C = w.shape
    P = gparts.shape[0]
    tr = R if R * C * 4 * (P + 7) * 2 <= VMEM_LIMIT_BYTES // 2 else _pick(R, (256, 128, 64, 32, 16, 8))

    def body(w_ref, m_ref, v_ref, gp_ref, g_ref, d_ref, nm_ref, nv_ref):
        g = gp_ref[0].astype(F32)
        for p in range(1, P):
            g = g + gp_ref[p].astype(F32)
        w_ = w_ref[...]
        nm = ADAM_B1 * m_ref[...] + (1.0 - ADAM_B1) * g
        nv = ADAM_B2 * v_ref[...] + (1.0 - ADAM_B2) * jnp.square(g)
        m_hat = nm / (1.0 - ADAM_B1 ** ADAM_STEP)
        v_hat = nv / (1.0 - ADAM_B2 ** ADAM_STEP)
        g_ref[...] = g
        d_ref[...] = -ADAM_LR * (m_hat / (jnp.sqrt(v_hat) + ADAM_EPS) + ADAM_WD * w_)
        nm_ref[...] = nm
        nv_ref[...] = nv

    spec = pl.BlockSpec((tr, C), lambda i: (i, 0))
    return pl.pallas_call(
        body, name=name, grid=(R // tr,),
        in_specs=[spec, spec, spec, pl.BlockSpec((P, tr, C), lambda i: (0, i, 0))],
        out_specs=[spec] * 4, out_shape=[jax.ShapeDtypeStruct((R, C), F32)] * 4,
        compiler_params=_cparams(("parallel",)),
    )(w, m, v, gparts)


def _place():
    return lax.axis_index("x"), lax.axis_index("y"), lax.axis_index("c")


def _index_of(p):
    return 4 * p[0] + 2 * p[1] + p[2]


def _allgather_small(name, rows):
    m_per, n = rows.shape

    def body(x_ref, out_ref, send_sems, recv_sems, local_sem):
        x, y, c = _place()
        me, sibling = (x, y, c), (x, y, 1 - c)
        chips = [(1 - x, y), (x, 1 - y), (1 - x, 1 - y)]

        def blk(p):
            return out_ref.at[pl.ds(_index_of(p) * m_per, m_per), :]

        def copy(k, block, to, src=None):
            return pltpu.make_async_remote_copy(
                src_ref=blk(block) if src is None else src, dst_ref=blk(block),
                send_sem=send_sems.at[k], recv_sem=recv_sems.at[k], device_id=to, device_id_type=MESH)

        mine = pltpu.make_async_copy(x_ref, blk(me), local_sem)
        mine.start()
        first = [copy(0, me, sibling, src=x_ref)]
        first += [copy(1 + j, me, (*chip, c), src=x_ref) for j, chip in enumerate(chips)]
        for cp in first:
            cp.start()
        passed = [copy(4 + j, (*chip, c), sibling) for j, chip in enumerate(chips)]
        for j, chip in enumerate(chips):
            copy(1 + j, (*chip, c), me).wait_recv()
            passed[j].start()
        copy(0, sibling, me).wait_recv()
        for j, chip in enumerate(chips):
            copy(4 + j, (*chip, 1 - c), me).wait_recv()
        for cp in first + passed:
            cp.wait_send()
        mine.wait()

    return pl.pallas_call(
        body, name=name,
        out_shape=jax.ShapeDtypeStruct((N_DEV * m_per, n), rows.dtype),
        in_specs=[pl.BlockSpec(memory_space=pltpu.VMEM)],
        out_specs=pl.BlockSpec(memory_space=pltpu.VMEM),
        scratch_shapes=[pltpu.SemaphoreType.DMA((7,)), pltpu.SemaphoreType.DMA((7,)), pltpu.SemaphoreType.DMA],
        compiler_params=_cparams(),
    )(rows)


def _allgather_big(name, arrs):
    na = len(arrs)

    def body(*refs):
        ins, outs = refs[:na], refs[na:2 * na]
        send_sems, recv_sems, local_sems = refs[2 * na:]
        x, y, c = _place()
        me, sibling = (x, y, c), (x, y, 1 - c)
        chips = [(1 - x, y), (x, 1 - y), (1 - x, 1 - y)]

        def copy(a, k, block, to, src=None):
            dst = outs[a].at[_index_of(block)]
            return pltpu.make_async_remote_copy(
                src_ref=dst if src is None else src, dst_ref=dst,
                send_sem=send_sems.at[a, k], recv_sem=recv_sems.at[a, k], device_id=to, device_id_type=MESH)

        mine = [pltpu.make_async_copy(ins[a], outs[a].at[_index_of(me)], local_sems.at[a]) for a in range(na)]
        for cp in mine:
            cp.start()
        sent = []
        for a in range(na):
            sent.append(copy(a, 0, me, sibling, src=ins[a]))
            sent += [copy(a, 1 + j, me, (*chip, c), src=ins[a]) for j, chip in enumerate(chips)]
        for cp in sent:
            cp.start()
        for j, chip in enumerate(chips):
            for a in range(na):
                copy(a, 1 + j, (*chip, c), me).wait_recv()
                fwd = copy(a, 4 + j, (*chip, c), sibling)
                fwd.start()
                sent.append(fwd)
        for a in range(na):
            copy(a, 0, sibling, me).wait_recv()
            for j, chip in enumerate(chips):
                copy(a, 4 + j, (*chip, 1 - c), me).wait_recv()
        for cp in sent:
            cp.wait_send()
        for cp in mine:
            cp.wait()

    any_spec = pl.BlockSpec(memory_space=pl.ANY)
    return pl.pallas_call(
        body, name=name,
        out_shape=[jax.ShapeDtypeStruct((N_DEV,) + a.shape, a.dtype) for a in arrs],
        in_specs=[any_spec] * na, out_specs=[any_spec] * na,
        scratch_shapes=[pltpu.SemaphoreType.DMA((na, 7)), pltpu.SemaphoreType.DMA((na, 7)), pltpu.SemaphoreType.DMA((na,))],
        compiler_params=_cparams(),
    )(*arrs)


def _scatter_parts(name, parts):
    na = len(parts)

    def body(*refs):
        ins, outs = refs[:na], refs[na:2 * na]
        send_sems, recv_sems, local_sems = refs[2 * na:]
        x, y, c = _place()
        me = _index_of((x, y, c))
        flips = [(k >> 2 & 1, k >> 1 & 1, k & 1) for k in range(1, N_DEV)]
        peers = [(1 - x if fx else x, 1 - y if fy else y, 1 - c if fc else c) for fx, fy, fc in flips]
        mine = [pltpu.make_async_copy(ins[a].at[me], outs[a].at[me], local_sems.at[a]) for a in range(na)]
        for cp in mine:
            cp.start()
        sent = []
        for a in range(na):
            for k, peer in enumerate(peers):
                sent.append(pltpu.make_async_remote_copy(
                    src_ref=ins[a].at[_index_of(peer)], dst_ref=outs[a].at[me],
                    send_sem=send_sems.at[a, k], recv_sem=recv_sems.at[a, k], device_id=peer, device_id_type=MESH))
        for cp in sent:
            cp.start()
        for a in range(na):
            for k, peer in enumerate(peers):
                slot = outs[a].at[_index_of(peer)]
                pltpu.make_async_remote_copy(
                    src_ref=slot, dst_ref=slot, send_sem=send_sems.at[a, k], recv_sem=recv_sems.at[a, k],
                    device_id=peer, device_id_type=MESH).wait_recv()
        for cp in sent:
            cp.wait_send()
        for cp in mine:
            cp.wait()

    any_spec = pl.BlockSpec(memory_space=pl.ANY)
    return pl.pallas_call(
        body, name=name,
        out_shape=[jax.ShapeDtypeStruct(p.shape, p.dtype) for p in parts],
        in_specs=[any_spec] * na, out_specs=[any_spec] * na,
        scratch_shapes=[pltpu.SemaphoreType.DMA((na, 7)), pltpu.SemaphoreType.DMA((na, 7)), pltpu.SemaphoreType.DMA((na,))],
        compiler_params=_cparams(),
    )(*parts)


def _gather_row(name, vec, width):
    n = vec.shape[0]
    rows = jnp.pad(vec, (0, width - n)).reshape(SUBLANES_F32, width // SUBLANES_F32)
    return _allgather_small(name, rows).reshape(N_DEV, width)[:, :n]


def _layer_fwd(x, mod, w):
    sh1, sc1, g1, sh2, sc2, g2 = [mod[i:i + 1] for i in range(N_MOD)]
    S = x.shape[0]
    h1 = _normmod_fwd(x, w["norm1_w"], sc1, sh1)
    proj = _matmul("proj_in", h1, w["w_in"], "nn")
    a1, a_out = _conv_a_fwd(proj, w["conv_a_w"], w["conv_a_b"], w["ln_a_w"], w["ln_a_b"])
    os, ls = zip(*[_attn_fwd(proj, dil) for dil in DILATIONS])
    att, lse = _attn_merge(os, ls)
    of, st_f = _hgrn_fwd(proj, w["lb_f"], ZF, False)
    ob, st_b = _hgrn_fwd(proj, w["lb_b"], ZB, True)
    rec = _hgrn_post_fwd(of, ob, proj, w["rec_norm_w"])
    mixed = jnp.concatenate([att.astype(BF16), rec, a_out], axis=1)
    y1 = _matmul("proj_out", mixed, w["w_out"], "nn")
    x2 = _gate_add(x, y1, g1)
    h2 = _normmod_fwd(x2, w["norm2_w"], sc2, sh2)
    u = _matmul("ffn_up", h2, w["w_up"], "nn")
    act = _ffn_mid_fwd(u, w["conv_f_w"])
    y2 = _matmul("ffn_down", act, w["w_down"], "nn")
    x3 = _gate_add(x2, y2, g2)
    saved = dict(x=x, h1=h1, proj=proj, a1=a1, att=att, lse=lse, of=of, ob=ob, st_f=st_f, st_b=st_b,
                 mixed=mixed, y1=y1, x2=x2, h2=h2, u=u, act=act, y2=y2)
    return x3, saved


def _layer_bwd(gx3, mod, w, s):
    sh1, sc1, g1, sh2, sc2, g2 = [mod[i:i + 1] for i in range(N_MOD)]
    S = gx3.shape[0]
    g = {}
    gy2, gg2 = _gate_bwd(gx3, s["y2"], g2)
    gact = _matmul("ffn_down_dx", gy2, w["w_down"], "nt")
    g["w_down"] = _matmul("ffn_down_dw", s["act"], gy2, "tn")
    gu_g, gu_v, gcw_g, gcw_v = _ffn_mid_bwd(s["u"], gact, w["conv_f_w"])
    gu = jnp.concatenate([gu_g, gu_v], axis=1)
    g["conv_f_w"] = jnp.concatenate([gcw_g, gcw_v], axis=1)
    gh2 = _matmul("ffn_up_dx", gu, w["w_up"], "nt")
    g["w_up"] = _matmul("ffn_up_dw", s["h2"], gu, "tn")
    gx2, g["norm2_w"], gsc2, gsh2 = _normmod_bwd(s["x2"], gh2, gx3, w["norm2_w"], sc2, sh2)
    gy1, gg1 = _gate_bwd(gx2, s["y1"], g1)
    gmixed = _matmul("proj_out_dx", gy1, w["w_out"], "nt")
    g["w_out"] = _matmul("proj_out_dw", s["mixed"], gy1, "tn")
    go, ggr, g["rec_norm_w"] = _hgrn_post_bwd(s["of"], s["ob"], s["proj"], gmixed, w["rec_norm_w"])
    gq_f, gz_f, gv_f, g["lb_f"] = _hgrn_bwd(s["proj"], w["lb_f"], s["st_f"], go, ZF, False)
    gq_b, gz_b, gv_b, g["lb_b"] = _hgrn_bwd(s["proj"], w["lb_b"], s["st_b"], go, ZB, True)
    do, ld = _attn_bwd_prep(gmixed, s["att"], s["lse"])
    gqkv = zip(*[_attn_bwd(s["proj"], do, ld, dil) for dil in DILATIONS])
    gq_a, gk_a, gv_a = [_sum3_bf16(lst, S, ATT_W) for lst in gqkv]
    gav, gag, gcw, g["conv_a_b"], g["ln_a_w"], g["ln_a_b"] = _conv_a_bwd(
        s["proj"], s["a1"], gmixed, w["conv_a_w"], w["ln_a_w"], w["ln_a_b"])
    g["conv_a_w"] = gcw[:CONV_W]
    gproj = jnp.concatenate([gq_a, gk_a, gv_a, (gq_f + gq_b).astype(BF16), gz_f.astype(BF16), gz_b.astype(BF16),
                             (gv_f + gv_b).astype(BF16), ggr.astype(BF16), gav, gag,
                             jnp.zeros((S, IN_COLS_PAD - IN_COLS), BF16)], axis=1)
    gh1 = _matmul("proj_in_dx", gproj, w["w_in"], "nt")
    g["w_in"] = _matmul("proj_in_dw", s["h1"], gproj, "tn")
    gx, g["norm1_w"], gsc1, gsh1 = _normmod_bwd(s["x"], gh1, gx2, w["norm1_w"], sc1, sh1)
    gmod = jnp.concatenate([gsh1, gsc1, gg1, gsh2, gsc2, gg2], axis=0)
    return gx, gmod, g


def _permute_in_cols(t):
    pad = jnp.zeros(t.shape[:-1] + (IN_COLS_PAD - IN_COLS,), t.dtype)
    return jnp.concatenate([t[..., CONV_COLS:], t[..., :CONV_COLS], pad], axis=-1)


def _unpermute_in_cols(t):
    return jnp.concatenate([t[..., IN_COLS - CONV_COLS:IN_COLS], t[..., :IN_COLS - CONV_COLS]], axis=-1)


def _cols_from_gathered(t, lead):
    nd = t.ndim
    perm = tuple(range(1, nd - 1)) + (0, nd - 1)
    t = t.transpose(perm)
    return t.reshape(t.shape[:-2] + (t.shape[-2] * t.shape[-1],))


def _cols_to_parts(t):
    L, R, C = t.shape
    return t.reshape(L * R, N_DEV, C // N_DEV).transpose(1, 0, 2)


SMALL_REPL = (("norm1_w", D), ("conv_a_b", CONV_CH), ("ln_a_w", CONV_CH), ("ln_a_b", CONV_CH),
              ("rec_norm_w", REC_W), ("norm2_w", D))


def kernel(x, c, w_ada, b_ada, norm1_w, w_in, conv_a_w, conv_a_b, ln_a_w, ln_a_b, lb_gamma, rec_norm_w, w_out, norm2_w, w_up, conv_f_w, w_down, final_norm_w, loss_target, m_w_ada, m_b_ada, m_norm1_w, m_w_in, m_conv_a_w, m_conv_a_b, m_ln_a_w, m_ln_a_b, m_lb_gamma, m_rec_norm_w, m_w_out, m_norm2_w, m_w_up, m_conv_f_w, m_w_down, m_final_norm_w, v_w_ada, v_b_ada, v_norm1_w, v_w_in, v_conv_a_w, v_conv_a_b, v_ln_a_w, v_ln_a_b, v_lb_gamma, v_rec_norm_w, v_w_out, v_norm2_w, v_w_up, v_conv_f_w, v_w_down, v_final_norm_w):
    px, py, pc = _place()
    me = _index_of((px, py, pc))
    xs, tgt = x[0], loss_target[0]
    S = xs.shape[0]
    ada_cols = w_ada.shape[2]

    big = [w_in.reshape(DEPTH * D, -1), w_up.reshape(DEPTH * D, -1), w_out.reshape(-1, D), w_down.reshape(-1, D)]
    g_in, g_up, g_out, g_down = _allgather_big("gather_weights", [t.astype(BF16) for t in big])
    w_in_f = _permute_in_cols(_cols_from_gathered(g_in.reshape(N_DEV, DEPTH, D, -1), 1))
    w_up_f = _cols_from_gathered(g_up.reshape(N_DEV, DEPTH, D, -1), 1)
    w_out_f = g_out.reshape(N_DEV, DEPTH, D // N_DEV, D).transpose(1, 0, 2, 3).reshape(DEPTH, D, D)
    w_out_f = jnp.concatenate([w_out_f[:, CONV_CH:], w_out_f[:, :CONV_CH]], axis=1)
    w_down_f = g_down.reshape(N_DEV, DEPTH, D_FF // N_DEV, D).transpose(1, 0, 2, 3).reshape(DEPTH, D_FF, D)

    small_in = jnp.concatenate([c.reshape(-1), conv_a_w.reshape(-1), lb_gamma.reshape(-1), conv_f_w.reshape(-1)])
    gs = _gather_row("gather_small", small_in, 8192)
    o1 = D
    o2 = o1 + conv_a_w.size
    o3 = o2 + lb_gamma.size
    c_all = gs[:, :o1]
    conv_a_f = _cols_from_gathered(gs[:, o1:o2].reshape(N_DEV, DEPTH, CONV_W, -1), 1)
    lb_gamma_f = _cols_from_gathered(gs[:, o2:o3].reshape(N_DEV, DEPTH, 2, -1), 1)
    conv_f_f = _cols_from_gathered(gs[:, o3:].reshape(N_DEV, DEPTH, FFN_CONV_W, -1), 1)
    conv_a_pad = jnp.pad(conv_a_f, ((0, 0), (0, CONV_W_PAD - CONV_W), (0, 0)))

    b_loc = lax.dynamic_slice_in_dim(b_ada, me * ada_cols, ada_cols, axis=1)

    def mod_fn(c_all_, w_, b_):
        cond = c_all_ * jax.nn.sigmoid(c_all_)
        return (jnp.concatenate([_dot_nn(cond, w_[l], HP) + b_[l] for l in range(DEPTH)], axis=1),)

    (mod_part,) = _vmem_call("ada_mod", mod_fn, [c_all, w_ada, b_loc[:, None, :]], [((N_DEV, DEPTH * ada_cols), F32)])
    gm = _allgather_small("gather_mod", mod_part).reshape(N_DEV, N_DEV, DEPTH, ada_cols)
    mod = lax.dynamic_index_in_dim(gm, me, axis=1, keepdims=False)
    mod = mod.transpose(1, 0, 2).reshape(DEPTH, N_MOD, D)

    (lb1,) = _vmem_call("lower_bounds", lambda a, b: (_lower_bounds_f(a, b),), [lb_gamma_f[0], lb_gamma_f[1]], [((2, REC_W), F32)])
    lb4 = jnp.concatenate([jnp.zeros_like(lb1), lb1], axis=0)

    def layer_weights(l):
        row = lambda t: t[l].reshape(1, -1)
        return dict(norm1_w=row(norm1_w), w_in=w_in_f[l], conv_a_w=conv_a_pad[l], conv_a_b=row(conv_a_b), ln_a_w=row(ln_a_w),
                    ln_a_b=row(ln_a_b), lb_f=lb4[2 * l:2 * l + 1], lb_b=lb4[2 * l + 1:2 * l + 2], rec_norm_w=row(rec_norm_w),
                    w_out=w_out_f[l], norm2_w=row(norm2_w), w_up=w_up_f[l], conv_f_w=conv_f_f[l], w_down=w_down_f[l])

    ws = [layer_weights(l) for l in range(DEPTH)]
    h, saved = xs, []
    for l in range(DEPTH):
        h, s = _layer_fwd(h, mod[l], ws[l])
        saved.append(s)
    gh, g_final, loss_row = _loss_head(h, tgt, final_norm_w.reshape(1, D))
    loss = lax.psum(loss_row[0, 0], ("x", "y", "c"))
    gmods, gws = [None] * DEPTH, [None] * DEPTH
    for l in reversed(range(DEPTH)):
        gh, gmods[l], gws[l] = _layer_bwd(gh, mod[l], ws[l], saved[l])
    grad_x = gh[None]

    glb1 = jnp.concatenate([gws[1]["lb_f"], gws[1]["lb_b"]], axis=0)

    def lb_bwd_fn(a, b, g1):
        _, vjp = jax.vjp(_lower_bounds_f, a, b)
        return vjp(g1)

    g_lb_gamma = jnp.stack(_vmem_call("lower_bounds_bwd", lb_bwd_fn, [lb_gamma_f[0], lb_gamma_f[1], glb1], [((2, REC_W), F32)] * 2))
    pieces = [jnp.stack(gmods).reshape(-1)]
    for l in range(DEPTH):
        pieces += [gws[l][n].reshape(-1) for n, _ in SMALL_REPL]
    pieces += [g_final.reshape(-1)]
    pieces += [jnp.stack([gws[l]["conv_a_w"] for l in range(DEPTH)]).reshape(-1), g_lb_gamma.reshape(-1),
               jnp.stack([gws[l]["conv_f_w"] for l in range(DEPTH)]).reshape(-1)]
    small_g = jnp.concatenate(pieces)
    n_small = small_g.shape[0]
    gsm = _gather_row("gather_small_grads", small_g, 71680)
    n_mod = DEPTH * N_MOD * D
    gmod_all = gsm[:, :n_mod].reshape(N_DEV, DEPTH, N_MOD * D)
    gmod_loc = lax.dynamic_slice_in_dim(gmod_all, me * ada_cols, ada_cols, axis=2).transpose(1, 0, 2)

    def small_fn(gsm_, c_all_, gm_):
        cond = c_all_ * jax.nn.sigmoid(c_all_)
        gw = jnp.concatenate([_dot_tn(cond, gm_[l], HP) for l in range(DEPTH)], axis=0)
        return jnp.sum(gsm_, axis=0, keepdims=True), gw

    tot, g_w_ada = _vmem_call("small_grads", small_fn, [gsm, c_all, gmod_loc],
                              [((1, n_small), F32), ((DEPTH * D, ada_cols), F32)])
    tot = tot[0]
    grads = {"w_ada": g_w_ada.reshape(DEPTH, D, ada_cols), "b_ada": tot[:n_mod].reshape(DEPTH, N_MOD * D)}
    pos = n_mod
    per_layer = {n: [] for n, _ in SMALL_REPL}
    for l in range(DEPTH):
        for n, width in SMALL_REPL:
            per_layer[n].append(tot[pos:pos + width])
            pos += width
    for n, _ in SMALL_REPL:
        grads[n] = jnp.stack(per_layer[n])
    grads["final_norm_w"] = tot[pos:pos + D]
    pos += D
    n_ca, n_lb, n_cf = DEPTH * CONV_W * CONV_CH, DEPTH * 2 * REC_W, DEPTH * FFN_CONV_W * 2 * D_FF
    g_ca = tot[pos:pos + n_ca].reshape(DEPTH, CONV_W, CONV_CH)
    g_lb = tot[pos + n_ca:pos + n_ca + n_lb].reshape(DEPTH, 2, REC_W)
    g_cf = tot[pos + n_ca + n_lb:pos + n_ca + n_lb + n_cf].reshape(DEPTH, FFN_CONV_W, 2 * D_FF)
    grads["conv_a_w"] = lax.dynamic_slice_in_dim(g_ca, me * conv_a_w.shape[2], conv_a_w.shape[2], axis=2)
    grads["lb_gamma"] = lax.dynamic_slice_in_dim(g_lb, me * lb_gamma.shape[2], lb_gamma.shape[2], axis=2)
    grads["conv_f_w"] = lax.dynamic_slice_in_dim(g_cf, me * conv_f_w.shape[2], conv_f_w.shape[2], axis=2)

    gw_in = _unpermute_in_cols(jnp.stack([gws[l]["w_in"] for l in range(DEPTH)]))
    gw_up = jnp.stack([gws[l]["w_up"] for l in range(DEPTH)])
    gw_out = jnp.stack([gws[l]["w_out"] for l in range(DEPTH)])
    gw_out = jnp.concatenate([gw_out[:, D - CONV_CH:], gw_out[:, :D - CONV_CH]], axis=1)
    gw_down = jnp.stack([gws[l]["w_down"] for l in range(DEPTH)])
    rows_to_parts = lambda t: t.reshape(DEPTH, N_DEV, -1, D).transpose(1, 0, 2, 3).reshape(N_DEV, -1, D)
    parts = [_cols_to_parts(gw_in), _cols_to_parts(gw_up), rows_to_parts(gw_out), rows_to_parts(gw_down)]
    r_in, r_up, r_out, r_down = _scatter_parts("scatter_grads", [t.astype(BF16) for t in parts])

    given = dict(w_ada=(w_ada, m_w_ada, v_w_ada), b_ada=(b_ada, m_b_ada, v_b_ada), norm1_w=(norm1_w, m_norm1_w, v_norm1_w),
                 w_in=(w_in, m_w_in, v_w_in), conv_a_w=(conv_a_w, m_conv_a_w, v_conv_a_w), conv_a_b=(conv_a_b, m_conv_a_b, v_conv_a_b),
                 ln_a_w=(ln_a_w, m_ln_a_w, v_ln_a_w), ln_a_b=(ln_a_b, m_ln_a_b, v_ln_a_b), lb_gamma=(lb_gamma, m_lb_gamma, v_lb_gamma),
                 rec_norm_w=(rec_norm_w, m_rec_norm_w, v_rec_norm_w), w_out=(w_out, m_w_out, v_w_out),
                 norm2_w=(norm2_w, m_norm2_w, v_norm2_w), w_up=(w_up, m_w_up, v_w_up), conv_f_w=(conv_f_w, m_conv_f_w, v_conv_f_w),
                 w_down=(w_down, m_w_down, v_w_down), final_norm_w=(final_norm_w, m_final_norm_w, v_final_norm_w))
    big_parts = dict(w_in=r_in, w_up=r_up, w_out=r_out, w_down=r_down)
    names = list(given)
    res = {}
    for n in names:
        w_, m_, v_ = given[n]
        shape = w_.shape
        C = shape[-1]
        two_d = lambda t: t.reshape(-1, C)
        gp = big_parts[n] if n in big_parts else two_d(grads[n])[None]
        res[n] = [t.reshape(shape) for t in _adamw("adamw_" + n, two_d(w_), two_d(m_), two_d(v_), gp)]
    return (loss, grad_x, *[res[n][0] for n in names], *[res[n][1] for n in names],
            *[res[n][2] for n in names], *[res[n][3] for n in names])
```

```python
import functools

import numpy as np
import jax
import jax.numpy as jnp
from jax import lax
from jax.experimental import pallas as pl
from jax.experimental.pallas import tpu as pltpu

F32 = jnp.float32
BF16 = jnp.bfloat16
HP = lax.Precision.HIGHEST
MESH = pl.DeviceIdType.MESH

N_DEV = 8
D = 1024
DEPTH = 2
CONV_CH = 256
CONV_W = 31
CONV_W_PAD = 32
ATT_W = 384
REC_W = 384
N_HEADS = 6
HEAD = 64
HEAD_SHIFT = 6
HALF_BAND = 64
ATT_BLK = 128
DILATIONS = (1, 4, 16)
ALIBI_SLOPES = tuple(float(2.0 ** (-8.0 * (h + 1) / N_HEADS)) for h in range(N_HEADS))
MASK_VALUE = -1e30
REC_CHUNK = 64
EXP_CLAMP = 80.0
F_TINY = 1e-30
IN_COLS = 3584
IN_COLS_PAD = IN_COLS
QKV_BLOCKS = 3
D_FF = 2816
FFN_CONV_W = 3
N_MOD = 6
EPS = 1e-6
ADAM_LR, ADAM_B1, ADAM_B2, ADAM_EPS, ADAM_WD, ADAM_STEP = 0.001, 0.9, 0.999, 1e-08, 0.01, 10

VMEM_LIMIT_BYTES = 56 * 1024 * 1024
SUBLANES_F32 = 8
LANES = 128

QA, KA, VA, QR, ZF, ZB, IR, GR = range(8)
AV_BLK, AG_BLK = 12, 13
CONV_COLS = 2 * CONV_CH


def _cparams(sem=None):
    kw = dict(vmem_limit_bytes=VMEM_LIMIT_BYTES)
    if sem is not None:
        kw["dimension_semantics"] = sem
    return pltpu.CompilerParams(**kw)


def _iota(shape, dim):
    return lax.broadcasted_iota(jnp.int32, shape, dim)


def _dot(a, b, dims, precision=None):
    return lax.dot_general(a, b, (dims, ((), ())), precision=precision, preferred_element_type=F32)


def _dot_nn(a, b, precision=None):
    return _dot(a, b, ((1,), (0,)), precision)


def _dot_nt(a, b, precision=None):
    return _dot(a, b, ((1,), (1,)), precision)


def _dot_tn(a, b, precision=None):
    return _dot(a, b, ((0,), (0,)), precision)


def _c0(j):
    return 0


def _pick(n, cands):
    for c in cands:
        if n % c == 0:
            return c
    return n


MATMUL_OUT_TILE_BYTES = 8 * 1024 * 1024


def _div_lanes(n, cap):
    best = None
    for d in range(LANES, min(n, cap) + 1, LANES):
        if n % d == 0:
            best = d
    return best if best is not None else n


def _matmul_tiles(mode, M, N, K):
    if mode == "nn":
        tm = _pick(M, (1024, 512, 256, 128))
        return tm, _div_lanes(N, MATMUL_OUT_TILE_BYTES // (4 * tm)), K
    if mode == "nt":
        return _pick(M, (512, 256, 128)), N, K
    tm = _div_lanes(M, 1408)
    return tm, _div_lanes(N, MATMUL_OUT_TILE_BYTES // (4 * tm)), _pick(K, (1024, 512, 256))


def _matmul(name, a, b, mode, out_dtype=F32):
    if mode == "nn":
        (M, K), (_, N) = a.shape, b.shape
    elif mode == "nt":
        (M, K), (N, _) = a.shape, b.shape
    else:
        (K, M), (_, N) = a.shape, b.shape
    tm, tn, tk = _matmul_tiles(mode, M, N, K)
    nk = K // tk
    if mode == "nn":
        a_spec = pl.BlockSpec((tm, tk), lambda i, j, k: (i, k))
        b_spec = pl.BlockSpec((tk, tn), lambda i, j, k: (k, j))
        dims = ((1,), (0,))
    elif mode == "nt":
        a_spec = pl.BlockSpec((tm, tk), lambda i, j, k: (i, k))
        b_spec = pl.BlockSpec((tn, tk), lambda i, j, k: (j, k))
        dims = ((1,), (1,))
    else:
        a_spec = pl.BlockSpec((tk, tm), lambda i, j, k: (k, i))
        b_spec = pl.BlockSpec((tk, tn), lambda i, j, k: (k, j))
        dims = ((0,), (0,))

    def body_whole(a_ref, b_ref, o_ref):
        o_ref[...] = _dot(a_ref[...].astype(BF16), b_ref[...].astype(BF16), dims).astype(o_ref.dtype)

    def body(a_ref, b_ref, o_ref, acc_ref):
        k = pl.program_id(2)
        part = _dot(a_ref[...].astype(BF16), b_ref[...].astype(BF16), dims)

        @pl.when(k == 0)
        def _():
            acc_ref[...] = part

        @pl.when(k > 0)
        def _():
            acc_ref[...] += part

        @pl.when(k == nk - 1)
        def _():
            o_ref[...] = acc_ref[...].astype(o_ref.dtype)

    return pl.pallas_call(
        body_whole if nk == 1 else body, name=name, grid=(M // tm, N // tn, nk),
        in_specs=[a_spec, b_spec],
        out_specs=pl.BlockSpec((tm, tn), lambda i, j, k: (i, j)),
        out_shape=jax.ShapeDtypeStruct((M, N), out_dtype),
        scratch_shapes=[] if nk == 1 else [pltpu.VMEM((tm, tn), F32)],
        compiler_params=_cparams(("parallel", "parallel", "arbitrary")),
    )(a, b)


def _rowwise(name, fn, S, ts, tiles, params=(), outs=(), accs=(), halo=0, ncb=1):
    in_specs, args = [], []
    for arr, w, jm, with_halo in tiles:
        if with_halo:
            hb, nhb = ts // halo, S // halo
            in_specs += [
                pl.BlockSpec((halo, w), lambda j, i, jm=jm, hb=hb: (jnp.maximum(i * hb - 1, 0), jm(j))),
                pl.BlockSpec((ts, w), lambda j, i, jm=jm: (i, jm(j))),
                pl.BlockSpec((halo, w), lambda j, i, jm=jm, hb=hb, nhb=nhb: (jnp.minimum((i + 1) * hb, nhb - 1), jm(j))),
            ]
            args += [arr, arr, arr]
        else:
            in_specs.append(pl.BlockSpec((ts, w), lambda j, i, jm=jm: (i, jm(j))))
            args.append(arr)
    for arr, r, w, jm in params:
        in_specs.append(pl.BlockSpec((r, w), lambda j, i, jm=jm: (0, jm(j))))
        args.append(arr)
    out_specs, out_shape = [], []
    for w, dt, jm, tw in outs:
        out_specs.append(pl.BlockSpec((ts, w), lambda j, i, jm=jm: (i, jm(j))))
        out_shape.append(jax.ShapeDtypeStruct((S, tw), dt))
    for r, w, jm, tw in accs:
        out_specs.append(pl.BlockSpec((r, w), lambda j, i, jm=jm: (0, jm(j))))
        out_shape.append(jax.ShapeDtypeStruct((r, tw), F32))
    n_tiles, n_params, n_outs = len(tiles), len(params), len(outs)

    def body(*refs):
        i = pl.program_id(1)
        pos, vals = 0, []
        for _, w, _, with_halo in tiles:
            if with_halo:
                before, after = refs[pos][...], refs[pos + 2][...]
                before = jnp.where(i > 0, before, jnp.zeros_like(before))
                after = jnp.where(i < S // ts - 1, after, jnp.zeros_like(after))
                vals.append(jnp.concatenate([before, refs[pos + 1][...], after], axis=0))
                pos += 3
            else:
                vals.append(refs[pos][...])
                pos += 1
        prefs = refs[pos:pos + n_params]
        orefs = refs[pos + n_params:pos + n_params + n_outs]
        arefs = refs[pos + n_params + n_outs:]

        @pl.when(i == 0)
        def _():
            for r in arefs:
                r[...] = jnp.zeros_like(r)

        fn(i, vals, prefs, orefs, arefs)

    res = pl.pallas_call(
        body, name=name, grid=(ncb, S // ts),
        in_specs=in_specs, out_specs=out_specs, out_shape=out_shape,
        compiler_params=_cparams(("arbitrary", "arbitrary")),
    )(*args)
    return res


def _vmem_call(name, fn, ins, out_shapes):
    n_in = len(ins)

    def body(*refs):
        vals = fn(*[r[...] for r in refs[:n_in]])
        for r, v in zip(refs[n_in:], vals):
            r[...] = v.astype(r.dtype)

    return pl.pallas_call(
        body, name=name,
        out_shape=[jax.ShapeDtypeStruct(s, dt) for s, dt in out_shapes],
        compiler_params=_cparams(),
    )(*ins)


def _rms(x, w):
    return x * lax.rsqrt(jnp.mean(x * x, axis=-1, keepdims=True) + EPS) * w


def _normmod_f(x, nw, sc, sh):
    return _rms(x, nw) * (1.0 + sc) + sh


def _row_params(*vecs):
    return [(v, 1, v.shape[1], _c0) for v in vecs]


def _normmod_fwd(x, nw, sc, sh):
    S = x.shape[0]

    def fn(i, vals, p, o, a):
        o[0][...] = _normmod_f(vals[0], p[0][...], p[1][...], p[2][...]).astype(BF16)

    return _rowwise("normmod_fwd", fn, S, 512, [(x, D, _c0, False)], _row_params(nw, sc, sh), [(D, BF16, _c0, D)])[0]


def _normmod_bwd(x, gh, gres, nw, sc, sh):
    S = x.shape[0]

    def fn(i, vals, p, o, a):
        _, vjp = jax.vjp(_normmod_f, vals[0], p[0][...], p[1][...], p[2][...])
        gx, gnw, gsc, gsh = vjp(vals[1])
        o[0][...] = gx + vals[2]
        a[0][...] += gnw
        a[1][...] += gsc
        a[2][...] += gsh

    return _rowwise("normmod_bwd", fn, S, 256, [(x, D, _c0, False), (gh, D, _c0, False), (gres, D, _c0, False)],
                    _row_params(nw, sc, sh), [(D, F32, _c0, D)], [(1, D, _c0, D)] * 3)


def _gate_add(x, y, g):
    S = x.shape[0]

    def fn(i, vals, p, o, a):
        o[0][...] = vals[0] + p[0][...] * vals[1]

    return _rowwise("gate_add", fn, S, 512, [(x, D, _c0, False), (y, D, _c0, False)], _row_params(g), [(D, F32, _c0, D)])[0]


def _gate_bwd(gx, y, g):
    S = gx.shape[0]

    def fn(i, vals, p, o, a):
        o[0][...] = (vals[0] * p[0][...]).astype(BF16)
        a[0][...] += jnp.sum(vals[0] * vals[1], axis=0, keepdims=True)

    return _rowwise("gate_bwd", fn, S, 512, [(gx, D, _c0, False), (y, D, _c0, False)], _row_params(g),
                    [(D, BF16, _c0, D)], [(1, D, _c0, D)])


def _loss_head(x, tgt, fw):
    S = x.shape[0]

    def fn(i, vals, p, o, a):
        y, vjp = jax.vjp(_rms, vals[0], p[0][...])
        err = y - vals[1]
        gx, gfw = vjp(err * (1.0 / D))
        o[0][...] = gx
        a[0][...] += gfw
        part = 0.5 * jnp.sum(jnp.mean(err * err, axis=-1, keepdims=True), axis=0, keepdims=True)
        a[1][...] += jnp.broadcast_to(part, (1, LANES))

    return _rowwise("loss_head", fn, S, 256, [(x, D, _c0, False), (tgt, D, _c0, False)], _row_params(fw),
                    [(D, F32, _c0, D)], [(1, D, _c0, D), (1, LANES, _c0, LANES)])


CONV_HALO = 16
CONV_TS = 512


def _shifted(ext, shift, ts, halo):
    n = ext.shape[0]
    s = shift % n
    r = ext if s == 0 else pltpu.roll(ext, s, 0)
    return r[halo:halo + ts]


def _ln_silu(a, w, b):
    mu = jnp.mean(a, axis=-1, keepdims=True)
    var = jnp.mean(jnp.square(a - mu), axis=-1, keepdims=True)
    y = (a - mu) * lax.rsqrt(var + EPS) * w + b
    return y * jax.nn.sigmoid(y)


def _conv_a_fwd(proj, w_pad, b, lnw, lnb):
    S = proj.shape[0]
    ts, H = min(CONV_TS, S), CONV_HALO

    def fn(i, vals, p, o, a):
        a0 = vals[0] * jax.nn.sigmoid(vals[1])
        acc = jnp.zeros((ts, CONV_CH), F32) + p[1][...]
        for k in range(CONV_W):
            acc = acc + _shifted(a0, CONV_W // 2 - k, ts, H) * p[0][pl.ds(k, 1), :]
        o[0][...] = acc
        o[1][...] = _ln_silu(acc, p[2][...], p[3][...]).astype(BF16)

    tiles = [(proj, CONV_CH, lambda j: AV_BLK, True), (proj, CONV_CH, lambda j: AG_BLK, True)]
    params = [(w_pad, CONV_W_PAD, CONV_CH, _c0)] + _row_params(b, lnw, lnb)
    return _rowwise("conv_a_fwd", fn, S, ts, tiles, params, [(CONV_CH, F32, _c0, CONV_CH), (CONV_CH, BF16, _c0, CONV_CH)], halo=H)


def _conv_a_bwd(proj, a1, gmixed, w_pad, lnw, lnb):
    S = proj.shape[0]
    ts, H = min(CONV_TS, S), CONV_HALO

    def fn(i, vals, p, o, a):
        av, ag, a1e, ge = vals
        lw, lb = p[1][...], p[2][...]
        _, vjp_e = jax.vjp(lambda t: _ln_silu(t, lw, lb), a1e)
        (ga1e,) = vjp_e(ge)
        c = slice(H, H + ts)
        _, vjp_c = jax.vjp(_ln_silu, a1e[c], lw, lb)
        ga1, glw, glb = vjp_c(ge[c])
        a[1][...] += jnp.sum(ga1, axis=0, keepdims=True)
        a[2][...] += glw
        a[3][...] += glb
        sg = jax.nn.sigmoid(ag)
        a0 = av * sg
        ga0 = jnp.zeros((ts, CONV_CH), F32)
        for k in range(CONV_W):
            a[0][pl.ds(k, 1), :] += jnp.sum(ga1 * _shifted(a0, CONV_W // 2 - k, ts, H), axis=0, keepdims=True)
            ga0 = ga0 + _shifted(ga1e, k - CONV_W // 2, ts, H) * p[0][pl.ds(k, 1), :]
        sgc, avc = sg[c], av[c]
        o[0][...] = (ga0 * sgc).astype(BF16)
        o[1][...] = (ga0 * avc * sgc * (1.0 - sgc)).astype(BF16)

    tiles = [(proj, CONV_CH, lambda j: AV_BLK, True), (proj, CONV_CH, lambda j: AG_BLK, True),
             (a1, CONV_CH, _c0, True), (gmixed, CONV_CH, lambda j: 3, True)]
    params = [(w_pad, CONV_W_PAD, CONV_CH, _c0)] + _row_params(lnw, lnb)
    outs = [(CONV_CH, BF16, _c0, CONV_CH), (CONV_CH, BF16, _c0, CONV_CH)]
    accs = [(CONV_W_PAD, CONV_CH, _c0, CONV_CH)] + [(1, CONV_CH, _c0, CONV_CH)] * 3
    return _rowwise("conv_a_bwd", fn, S, ts, tiles, params, outs, accs, halo=H)


FFN_HALO = 8
FFN_TS = 512
FFN_CB = 256
FFN_NCB = D_FF // FFN_CB


def _gelu_mul(g, v):
    return 0.5 * g * (1.0 + lax.erf(g * (2.0 ** -0.5))) * v


def _ffn_mid_fwd(u, cw):
    S = u.shape[0]
    ts, H = min(FFN_TS, S), FFN_HALO

    def conv(ext, w_ref):
        acc = jnp.zeros((ts, FFN_CB), F32)
        for k in range(FFN_CONV_W):
            acc = acc + _shifted(ext, 1 - k, ts, H) * w_ref[pl.ds(k, 1), :]
        return acc

    def fn(i, vals, p, o, a):
        o[0][...] = _gelu_mul(conv(vals[0], p[0]), conv(vals[1], p[1])).astype(BF16)

    tiles = [(u, FFN_CB, lambda j: j, True), (u, FFN_CB, lambda j: j + FFN_NCB, True)]
    params = [(cw, FFN_CONV_W, FFN_CB, lambda j: j), (cw, FFN_CONV_W, FFN_CB, lambda j: j + FFN_NCB)]
    return _rowwise("ffn_mid_fwd", fn, S, ts, tiles, params, [(FFN_CB, BF16, lambda j: j, D_FF)], halo=H, ncb=FFN_NCB)[0]


def _ffn_mid_bwd(u, gact, cw):
    S = u.shape[0]
    ts, H = min(FFN_TS, S), FFN_HALO
    n = ts + 2 * H

    def conv_all(ext, w_ref):
        acc = jnp.zeros((n, FFN_CB), F32)
        for k in range(FFN_CONV_W):
            s = (1 - k) % n
            acc = acc + (ext if s == 0 else pltpu.roll(ext, s, 0)) * w_ref[pl.ds(k, 1), :]
        return acc

    def fn(i, vals, p, o, a):
        ug, uv, ga = vals
        _, vjp = jax.vjp(_gelu_mul, conv_all(ug, p[0]), conv_all(uv, p[1]))
        gcg, gcv = vjp(ga)
        c = slice(H, H + ts)
        for half, (gc, ue) in enumerate(((gcg, ug), (gcv, uv))):
            gu = jnp.zeros((ts, FFN_CB), F32)
            for k in range(FFN_CONV_W):
                gu = gu + _shifted(gc, k - 1, ts, H) * p[half][pl.ds(k, 1), :]
                a[half][pl.ds(k, 1), :] += jnp.sum(gc[c] * _shifted(ue, 1 - k, ts, H), axis=0, keepdims=True)
            o[half][...] = gu.astype(BF16)

    tiles = [(u, FFN_CB, lambda j: j, True), (u, FFN_CB, lambda j: j + FFN_NCB, True), (gact, FFN_CB, lambda j: j, True)]
    params = [(cw, FFN_CONV_W, FFN_CB, lambda j: j), (cw, FFN_CONV_W, FFN_CB, lambda j: j + FFN_NCB)]
    outs = [(FFN_CB, BF16, lambda j: j, D_FF)] * 2
    accs = [(FFN_CONV_W, FFN_CB, lambda j: j, D_FF)] * 2
    return _rowwise("ffn_mid_bwd", fn, S, ts, tiles, params, outs, accs, halo=H, ncb=FFN_NCB)


LD_W = LANES
PAIR_W = 2 * HEAD
N_PAIRS = N_HEADS // 2


def _sub_view(t, d):
    S, C = t.shape
    return t.reshape(S // d, d * C)


def _sub_halo_specs(width, col, blk, hb, nhb):
    per = blk // hb
    return [
        pl.BlockSpec((hb, width), lambda r, i: (jnp.maximum(i * per - 1, 0), col(r))),
        pl.BlockSpec((blk, width), lambda r, i: (i, col(r))),
        pl.BlockSpec((hb, width), lambda r, i: (jnp.minimum((i + 1) * per, nhb - 1), col(r))),
    ]


def _pick_lane(t, lane):
    return jnp.sum(jnp.where(_iota((1, t.shape[1]), 1) == lane, t, 0.0), axis=1, keepdims=True)


def _pair_mask(h2):
    return (_iota((1, PAIR_W), 1) >> HEAD_SHIFT) == h2


def _cat_bf16(a, b, c):
    return jnp.concatenate([a[...], b[...], c[...]], axis=0).astype(BF16)


def _attn_fwd(proj, dil):
    S = proj.shape[0]
    L = S // dil
    blk, hb = min(ATT_BLK, L), HALF_BAND
    span = blk + 2 * hb

    def body(q_ref, kp, kc, kn, vp, vc, vn, o_ref, l_ref):
        i = pl.program_id(1)
        rel = _iota((blk, span), 1) - hb - _iota((blk, span), 0)
        kpos = i * blk - hb + _iota((blk, span), 1)
        valid = (jnp.abs(rel) <= hb) & (kpos >= 0) & (kpos < L)
        dist = jnp.abs(rel).astype(F32) * float(dil)
        q, k, v = q_ref[...].astype(BF16), _cat_bf16(kp, kc, kn), _cat_bf16(vp, vc, vn)
        lse = jnp.zeros((blk, LD_W), F32)
        for pr in range(N_PAIRS):
            sl = slice(pr * PAIR_W, (pr + 1) * PAIR_W)
            qp, kpair, vpair = q[:, sl], k[:, sl], v[:, sl]
            o = jnp.zeros((blk, PAIR_W), F32)
            for h2 in range(2):
                h, mask = 2 * pr + h2, _pair_mask(h2)
                s = _dot_nt(jnp.where(mask, qp, jnp.zeros_like(qp)), kpair) * (HEAD ** -0.5) - ALIBI_SLOPES[h] * dist
                s = jnp.where(valid, s, MASK_VALUE)
                m = jnp.max(s, axis=1, keepdims=True)
                p = jnp.exp(s - m)
                l = jnp.sum(p, axis=1, keepdims=True)
                o = jnp.where(mask, _dot_nn(p.astype(BF16), vpair) / l, o)
                lse = lse + jnp.where(_iota((1, LD_W), 1) == h, m + jnp.log(l), 0.0)
            o_ref[:, sl] = o
        l_ref[...] = lse

    nhb = L // hb
    view = _sub_view(proj, dil)
    in_specs = ([pl.BlockSpec((blk, ATT_W), lambda r, i: (i, r * QKV_BLOCKS +QA))]
                + _sub_halo_specs(ATT_W, lambda r: r * QKV_BLOCKS +KA, blk, hb, nhb)
                + _sub_halo_specs(ATT_W, lambda r: r * QKV_BLOCKS +VA, blk, hb, nhb))
    o, lse = pl.pallas_call(
        body, name=f"attn_fwd_d{dil}", grid=(dil, L // blk), in_specs=in_specs,
        out_specs=[pl.BlockSpec((blk, ATT_W), lambda r, i: (i, r)), pl.BlockSpec((blk, LD_W), lambda r, i: (i, r))],
        out_shape=[jax.ShapeDtypeStruct((L, dil * ATT_W), F32), jax.ShapeDtypeStruct((L, dil * LD_W), F32)],
        compiler_params=_cparams(("parallel", "parallel")),
    )(*([view] * 7))
    return o.reshape(S, ATT_W), lse.reshape(S, LD_W)


def _attn_bwd(proj, do, ld, dil):
    S = proj.shape[0]
    L = S // dil
    blk, hb = min(ATT_BLK, L), HALF_BAND
    span = blk + 2 * hb
    scale = HEAD ** -0.5

    def body(qp, qc, qn, kp, kc, kn, vp, vc, vn, gp, gc, gn, lp, lc, ln, dq_ref, dk_ref, dv_ref):
        i = pl.program_id(1)
        l = lc[...]
        le = jnp.concatenate([lp[...], l, ln[...]], axis=0)
        rel_q = _iota((blk, span), 1) - hb - _iota((blk, span), 0)
        kpos = i * blk - hb + _iota((blk, span), 1)
        valid_q = (jnp.abs(rel_q) <= hb) & (kpos >= 0) & (kpos < L)
        dist_q = jnp.abs(rel_q).astype(F32) * float(dil)
        rel_k = _iota((span, blk), 1) + hb - _iota((span, blk), 0)
        qpos = i * blk - hb + _iota((span, blk), 0)
        valid_k = (jnp.abs(rel_k) <= hb) & (qpos >= 0) & (qpos < L)
        dist_k = jnp.abs(rel_k).astype(F32) * float(dil)
        q_all, k_all, v_all, g_all = qc[...].astype(BF16), kc[...].astype(BF16), vc[...].astype(BF16), gc[...]
        qe_all, ke_all, ve_all = _cat_bf16(qp, qc, qn), _cat_bf16(kp, kc, kn), _cat_bf16(vp, vc, vn)
        ge_all = _cat_bf16(gp, gc, gn)
        for pr in range(N_PAIRS):
            sl = slice(pr * PAIR_W, (pr + 1) * PAIR_W)
            q, k, v, g = q_all[:, sl], k_all[:, sl], v_all[:, sl], g_all[:, sl]
            qe, ke, ve, ge = qe_all[:, sl], ke_all[:, sl], ve_all[:, sl], ge_all[:, sl]
            dq = jnp.zeros((blk, PAIR_W), F32)
            dk = jnp.zeros((blk, PAIR_W), F32)
            dv = jnp.zeros((blk, PAIR_W), F32)
            for h2 in range(2):
                h, mask = 2 * pr + h2, _pair_mask(h2)
                only = lambda t: jnp.where(mask, t, jnp.zeros_like(t))
                s = _dot_nt(only(q), ke) * scale - ALIBI_SLOPES[h] * dist_q
                p = jnp.where(valid_q, jnp.exp(s - _pick_lane(l, h)), 0.0)
                ds = p * (_dot_nt(only(g), ve) - _pick_lane(l, 8 + h))
                dq = jnp.where(mask, _dot_nn(ds.astype(BF16), ke), dq)
                s = _dot_nt(only(qe), k) * scale - ALIBI_SLOPES[h] * dist_k
                p = jnp.where(valid_k, jnp.exp(s - _pick_lane(le, h)), 0.0)
                dv = jnp.where(mask, _dot_tn(p.astype(BF16), ge), dv)
                ds = p * (_dot_nt(only(ge), v) - _pick_lane(le, 8 + h))
                dk = jnp.where(mask, _dot_tn(ds.astype(BF16), qe), dk)
            dq_ref[:, sl] = (dq * scale).astype(BF16)
            dk_ref[:, sl] = (dk * scale).astype(BF16)
            dv_ref[:, sl] = dv.astype(BF16)

    nhb = L // hb
    pview, gview, lview = _sub_view(proj, dil), _sub_view(do, dil), _sub_view(ld, dil)
    in_specs = (_sub_halo_specs(ATT_W, lambda r: r * QKV_BLOCKS +QA, blk, hb, nhb)
                + _sub_halo_specs(ATT_W, lambda r: r * QKV_BLOCKS +KA, blk, hb, nhb)
                + _sub_halo_specs(ATT_W, lambda r: r * QKV_BLOCKS +VA, blk, hb, nhb)
                + _sub_halo_specs(ATT_W, lambda r: r, blk, hb, nhb) + _sub_halo_specs(LD_W, lambda r: r, blk, hb, nhb))
    o_spec = pl.BlockSpec((blk, ATT_W), lambda r, i: (i, r))
    res = pl.pallas_call(
        body, name=f"attn_bwd_d{dil}", grid=(dil, L // blk), in_specs=in_specs,
        out_specs=[o_spec] * 3, out_shape=[jax.ShapeDtypeStruct((L, dil * ATT_W), BF16)] * 3,
        compiler_params=_cparams(("parallel", "parallel")),
    )(*([pview] * 9 + [gview] * 3 + [lview] * 3))
    return [t.reshape(S, ATT_W) for t in res]


def _head_expand(t):
    e = ((_iota((LD_W, ATT_W), 1) >> HEAD_SHIFT) == _iota((LD_W, ATT_W), 0)).astype(F32)
    return _dot_nn(t, e, HP)


def _attn_merge(os, ls):
    S = os[0].shape[0]

    def fn(i, vals, p, o, a):
        o3, l3 = vals[:3], vals[3:]
        m = jnp.maximum(jnp.maximum(l3[0], l3[1]), l3[2])
        e3 = [jnp.exp(l - m) for l in l3]
        den = e3[0] + e3[1] + e3[2]
        out = jnp.zeros((o3[0].shape[0], ATT_W), F32)
        for ob, e in zip(o3, e3):
            out = out + _head_expand(e / den) * ob
        o[0][...] = out
        o[1][...] = m + jnp.log(den)

    tiles = [(t, ATT_W, _c0, False) for t in os] + [(t, LD_W, _c0, False) for t in ls]
    return _rowwise("attn_merge", fn, S, 512, tiles, (), [(ATT_W, F32, _c0, ATT_W), (LD_W, F32, _c0, LD_W)])


def _attn_bwd_prep(gmixed, att, lse):
    S = att.shape[0]

    def fn(i, vals, p, o, a):
        g, out, lse_row = vals
        place_d = ((_iota((ATT_W, LD_W), 0) >> HEAD_SHIFT) + 8 == _iota((ATT_W, LD_W), 1)).astype(F32)
        o[0][...] = g.astype(BF16)
        o[1][...] = jnp.where(_iota((1, LD_W), 1) < 8, lse_row, 0.0) + _dot_nn(g * out, place_d, HP)

    tiles = [(gmixed, ATT_W, _c0, False), (att, ATT_W, _c0, False), (lse, LD_W, _c0, False)]
    return _rowwise("attn_bwd_prep", fn, S, 512, tiles, (), [(ATT_W, BF16, _c0, ATT_W), (LD_W, F32, _c0, LD_W)])


def _sum3_bf16(ts_, S, width):
    def fn(i, vals, p, o, a):
        o[0][...] = (vals[0].astype(F32) + vals[1].astype(F32) + vals[2].astype(F32)).astype(BF16)

    return _rowwise("sum3", fn, S, 512, [(t, width, _c0, False) for t in ts_], (), [(width, BF16, _c0, width)])[0]


def _block_diag_mask():
    return ((_iota((REC_W, REC_W), 0) >> HEAD_SHIFT) == (_iota((REC_W, REC_W), 1) >> HEAD_SHIFT)).astype(F32)


def _rep_heads(t):
    return jnp.concatenate([t] * N_HEADS, axis=0)


def _hgrn_chunk(qr, z, iv, lb, st, reverse):
    C = REC_CHUNK
    r, c = _iota((C, C), 0), _iota((C, C), 1)
    t_cum = (c >= r) if reverse else (c <= r)
    mid_row, last_row = (C // 2, 0) if reverse else (C // 2 - 1, C - 1)
    f = lb + (1.0 - lb) * jax.nn.sigmoid(z)
    logf = jnp.log(jnp.maximum(f, F_TINY))
    k = (1.0 - lb) * jax.nn.sigmoid(-z)
    q = qr * jax.nn.sigmoid(qr)
    b = _dot_nn(t_cum.astype(F32), logf, HP)
    row = _iota((C, 1), 0)
    bm = jnp.sum(jnp.where(row == mid_row, b, 0.0), axis=0, keepdims=True)
    bl = jnp.sum(jnp.where(row == last_row, b, 0.0), axis=0, keepdims=True)
    qt = q * jnp.exp(jnp.minimum(b - bm, EXP_CLAMP))
    kt = k * jnp.exp(jnp.minimum(bm - b, EXP_CLAMP))
    qh = q * jnp.exp(b)
    kh = k * jnp.exp(bl - b)
    lam = jnp.exp(bl)
    bd = ((_iota((PAIR_W, PAIR_W), 0) >> HEAD_SHIFT) == (_iota((PAIR_W, PAIR_W), 1) >> HEAD_SHIFT)).astype(F32)
    s_in = _iota((C, PAIR_W), 1) & (HEAD - 1)
    t_in = _iota((C, PAIR_W), 0)
    tri = (s_in >= t_in) if reverse else (s_in <= t_in)
    twice = lambda t: jnp.concatenate([t, t], axis=0)
    outs, states = [], []
    for pr in range(N_PAIRS):
        sl = slice(pr * PAIR_W, (pr + 1) * PAIR_W)
        k_bd = twice(kt[:, sl]) * bd
        v_bd = twice(iv[:, sl]) * bd
        st_bd = twice(st[:, sl]) * bd
        a = jnp.where(tri, _dot_nt(qt[:, sl], k_bd, lax.Precision.HIGH), 0.0)
        outs.append(_dot_nn(a, v_bd, lax.Precision.HIGH) + _dot_nt(qh[:, sl].astype(BF16), st_bd.astype(BF16)))
        kv = _dot_tn(iv[:, sl].astype(BF16), kh[:, sl].astype(BF16))
        st_bd = st_bd * lam[:, sl] + kv * bd
        states.append(st_bd[0:HEAD] + st_bd[HEAD:PAIR_W])
    return jnp.concatenate(outs, axis=1), jnp.concatenate(states, axis=1)


REC_CHUNKS_PER_STEP = 8
REC_ROWS = REC_CHUNKS_PER_STEP * REC_CHUNK


def _hgrn_specs(order, blocks):
    return [pl.BlockSpec((REC_ROWS, REC_W), lambda i, b=b: (order(i), b)) for b in blocks]


def _chunk_rows(j):
    return pl.ds(pl.multiple_of(j * REC_CHUNK, REC_CHUNK), REC_CHUNK)


def _hgrn_fwd(proj, lb, z_blk, reverse):
    S = proj.shape[0]
    nb = S // REC_ROWS
    order = (lambda i: nb - 1 - i) if reverse else (lambda i: i)

    def body(q_ref, z_ref, v_ref, lb_ref, o_ref, st_ref, st_scr):
        @pl.when(pl.program_id(0) == 0)
        def _():
            st_scr[...] = jnp.zeros_like(st_scr)

        def step(t, carry):
            j = REC_CHUNKS_PER_STEP - 1 - t if reverse else t
            rows = _chunk_rows(j)
            st = st_scr[...]
            st_ref[j] = st
            o, st_new = _hgrn_chunk(q_ref[rows, :], z_ref[rows, :], v_ref[rows, :], lb_ref[...], st, reverse)
            o_ref[rows, :] = o
            st_scr[...] = st_new
            return carry

        lax.fori_loop(0, REC_CHUNKS_PER_STEP, step, 0)

    return pl.pallas_call(
        body, name="hgrn_rev_fwd" if reverse else "hgrn_fwd_fwd", grid=(nb,),
        in_specs=_hgrn_specs(order, (QR, z_blk, IR)) + [pl.BlockSpec((1, REC_W), lambda i: (0, 0))],
        out_specs=[pl.BlockSpec((REC_ROWS, REC_W), lambda i: (order(i), 0)),
                   pl.BlockSpec((REC_CHUNKS_PER_STEP, HEAD, REC_W), lambda i: (order(i), 0, 0))],
        out_shape=[jax.ShapeDtypeStruct((S, REC_W), F32), jax.ShapeDtypeStruct((S // REC_CHUNK, HEAD, REC_W), F32)],
        scratch_shapes=[pltpu.VMEM((HEAD, REC_W), F32)],
        compiler_params=_cparams(("arbitrary",)),
    )(proj, proj, proj, lb)


def _hgrn_bwd(proj, lb, states, go, z_blk, reverse):
    S = proj.shape[0]
    nb = S // REC_ROWS
    order = (lambda i: i) if reverse else (lambda i: nb - 1 - i)

    def body(q_ref, z_ref, v_ref, lb_ref, st_ref, go_ref, gq_ref, gz_ref, gv_ref, glb_ref, gst_scr):
        @pl.when(pl.program_id(0) == 0)
        def _():
            gst_scr[...] = jnp.zeros_like(gst_scr)
            glb_ref[...] = jnp.zeros_like(glb_ref)

        chunk = functools.partial(_hgrn_chunk, reverse=reverse)

        def step(t, carry):
            j = t if reverse else REC_CHUNKS_PER_STEP - 1 - t
            rows = _chunk_rows(j)
            _, vjp = jax.vjp(chunk, q_ref[rows, :], z_ref[rows, :], v_ref[rows, :], lb_ref[...], st_ref[j])
            gq, gz, gv, glb, gst = vjp((go_ref[rows, :], gst_scr[...]))
            gq_ref[rows, :] = gq
            gz_ref[rows, :] = gz
            gv_ref[rows, :] = gv
            glb_ref[...] += glb
            gst_scr[...] = gst
            return carry

        lax.fori_loop(0, REC_CHUNKS_PER_STEP, step, 0)

    row_spec = pl.BlockSpec((REC_ROWS, REC_W), lambda i: (order(i), 0))
    return pl.pallas_call(
        body, name="hgrn_rev_bwd" if reverse else "hgrn_fwd_bwd", grid=(nb,),
        in_specs=(_hgrn_specs(order, (QR, z_blk, IR)) + [pl.BlockSpec((1, REC_W), lambda i: (0, 0))]
                  + [pl.BlockSpec((REC_CHUNKS_PER_STEP, HEAD, REC_W), lambda i: (order(i), 0, 0)), row_spec]),
        out_specs=[row_spec] * 3 + [pl.BlockSpec((1, REC_W), lambda i: (0, 0))],
        out_shape=[jax.ShapeDtypeStruct((S, REC_W), F32)] * 3 + [jax.ShapeDtypeStruct((1, REC_W), F32)],
        scratch_shapes=[pltpu.VMEM((HEAD, REC_W), F32)],
        compiler_params=_cparams(("arbitrary",)),
    )(proj, proj, proj, lb, states, go)


def _hgrn_post_f(of, ob, gr, rnw):
    o = of + ob
    ms = _dot_nn(o * o, _block_diag_mask() * (1.0 / HEAD), HP)
    return o * lax.rsqrt(ms + EPS) * rnw * (gr * jax.nn.sigmoid(gr))


def _hgrn_post_fwd(of, ob, proj, rnw):
    S = of.shape[0]

    def fn(i, vals, p, o, a):
        o[0][...] = _hgrn_post_f(vals[0], vals[1], vals[2], p[0][...]).astype(BF16)

    tiles = [(of, REC_W, _c0, False), (ob, REC_W, _c0, False), (proj, REC_W, lambda j: GR, False)]
    return _rowwise("hgrn_post_fwd", fn, S, 512, tiles, _row_params(rnw), [(REC_W, BF16, _c0, REC_W)])[0]


def _hgrn_post_bwd(of, ob, proj, gmixed, rnw):
    S = of.shape[0]

    def fn(i, vals, p, o, a):
        _, vjp = jax.vjp(_hgrn_post_f, vals[0], vals[1], vals[2], p[0][...])
        go, _, ggr, grnw = vjp(vals[3])
        o[0][...] = go
        o[1][...] = ggr
        a[0][...] += grnw

    tiles = [(of, REC_W, _c0, False), (ob, REC_W, _c0, False), (proj, REC_W, lambda j: GR, False),
             (gmixed, REC_W, lambda j: 1, False)]
    return _rowwise("hgrn_post_bwd", fn, S, 256, tiles, _row_params(rnw),
                    [(REC_W, F32, _c0, REC_W), (REC_W, F32, _c0, REC_W)], [(1, REC_W, _c0, REC_W)])


def _lower_bounds_f(g0, g1):
    m = jnp.maximum(g0, g1)
    e0, e1 = jnp.exp(g0 - m), jnp.exp(g1 - m)
    return e1 / (e0 + e1)


def _adamw(name, w, m, v, gparts):
    R, C = w.shape
    P = gparts.shape[0]
    tr = R if R * C * 4 * (P + 7) * 2 <= VMEM_LIMIT_BYTES // 2 else _pick(R, (256, 128, 64, 32, 16, 8))

    def body(w_ref, m_ref, v_ref, gp_ref, g_ref, d_ref, nm_ref, nv_ref):
        g = gp_ref[0].astype(F32)
        for p in range(1, P):
            g = g + gp_ref[p].astype(F32)
        w_ = w_ref[...]
        nm = ADAM_B1 * m_ref[...] + (1.0 - ADAM_B1) * g
        nv = ADAM_B2 * v_ref[...] + (1.0 - ADAM_B2) * jnp.square(g)
        m_hat = nm / (1.0 - ADAM_B1 ** ADAM_STEP)
        v_hat = nv / (1.0 - ADAM_B2 ** ADAM_STEP)
        g_ref[...] = g
        d_ref[...] = -ADAM_LR * (m_hat / (jnp.sqrt(v_hat) + ADAM_EPS) + ADAM_WD * w_)
        nm_ref[...] = nm
        nv_ref[...] = nv

    spec = pl.BlockSpec((tr, C), lambda i: (i, 0))
    return pl.pallas_call(
        body, name=name, grid=(R // tr,),
        in_specs=[spec, spec, spec, pl.BlockSpec((P, tr, C), lambda i: (0, i, 0))],
        out_specs=[spec] * 4, out_shape=[jax.ShapeDtypeStruct((R, C), F32)] * 4,
        compiler_params=_cparams(("parallel",)),
    )(w, m, v, gparts)


def _place():
    return lax.axis_index("x"), lax.axis_index("y"), lax.axis_index("c")


def _index_of(p):
    return 4 * p[0] + 2 * p[1] + p[2]


def _allgather_small(name, rows):
    m_per, n = rows.shape

    def body(x_ref, out_ref, send_sems, recv_sems, local_sem):
        x, y, c = _place()
        me, sibling = (x, y, c), (x, y, 1 - c)
        chips = [(1 - x, y), (x, 1 - y), (1 - x, 1 - y)]

        def blk(p):
            return out_ref.at[pl.ds(_index_of(p) * m_per, m_per), :]

        def copy(k, block, to, src=None):
            return pltpu.make_async_remote_copy(
                src_ref=blk(block) if src is None else src, dst_ref=blk(block),
                send_sem=send_sems.at[k], recv_sem=recv_sems.at[k], device_id=to, device_id_type=MESH)

        mine = pltpu.make_async_copy(x_ref, blk(me), local_sem)
        mine.start()
        first = [copy(0, me, sibling, src=x_ref)]
        first += [copy(1 + j, me, (*chip, c), src=x_ref) for j, chip in enumerate(chips)]
        for cp in first:
            cp.start()
        passed = [copy(4 + j, (*chip, c), sibling) for j, chip in enumerate(chips)]
        for j, chip in enumerate(chips):
            copy(1 + j, (*chip, c), me).wait_recv()
            passed[j].start()
        copy(0, sibling, me).wait_recv()
        for j, chip in enumerate(chips):
            copy(4 + j, (*chip, 1 - c), me).wait_recv()
        for cp in first + passed:
            cp.wait_send()
        mine.wait()

    return pl.pallas_call(
        body, name=name,
        out_shape=jax.ShapeDtypeStruct((N_DEV * m_per, n), rows.dtype),
        in_specs=[pl.BlockSpec(memory_space=pltpu.VMEM)],
        out_specs=pl.BlockSpec(memory_space=pltpu.VMEM),
        scratch_shapes=[pltpu.SemaphoreType.DMA((7,)), pltpu.SemaphoreType.DMA((7,)), pltpu.SemaphoreType.DMA],
        compiler_params=_cparams(),
    )(rows)


def _allgather_big(name, arrs):
    na = len(arrs)

    def body(*refs):
        ins, outs = refs[:na], refs[na:2 * na]
        send_sems, recv_sems, local_sems = refs[2 * na:]
        x, y, c = _place()
        me, sibling = (x, y, c), (x, y, 1 - c)
        chips = [(1 - x, y), (x, 1 - y), (1 - x, 1 - y)]

        def copy(a, k, block, to, src=None):
            dst = outs[a].at[_index_of(block)]
            return pltpu.make_async_remote_copy(
                src_ref=dst if src is None else src, dst_ref=dst,
                send_sem=send_sems.at[a, k], recv_sem=recv_sems.at[a, k], device_id=to, device_id_type=MESH)

        mine = [pltpu.make_async_copy(ins[a], outs[a].at[_index_of(me)], local_sems.at[a]) for a in range(na)]
        for cp in mine:
            cp.start()
        sent = []
        for a in range(na):
            sent.append(copy(a, 0, me, sibling, src=ins[a]))
            sent += [copy(a, 1 + j, me, (*chip, c), src=ins[a]) for j, chip in enumerate(chips)]
        for cp in sent:
            cp.start()
        for j, chip in enumerate(chips):
            for a in range(na):
                copy(a, 1 + j, (*chip, c), me).wait_recv()
                fwd = copy(a, 4 + j, (*chip, c), sibling)
                fwd.start()
                sent.append(fwd)
        for a in range(na):
            copy(a, 0, sibling, me).wait_recv()
            for j, chip in enumerate(chips):
                copy(a, 4 + j, (*chip, 1 - c), me).wait_recv()
        for cp in sent:
            cp.wait_send()
        for cp in mine:
            cp.wait()

    any_spec = pl.BlockSpec(memory_space=pl.ANY)
    return pl.pallas_call(
        body, name=name,
        out_shape=[jax.ShapeDtypeStruct((N_DEV,) + a.shape, a.dtype) for a in arrs],
        in_specs=[any_spec] * na, out_specs=[any_spec] * na,
        scratch_shapes=[pltpu.SemaphoreType.DMA((na, 7)), pltpu.SemaphoreType.DMA((na, 7)), pltpu.SemaphoreType.DMA((na,))],
        compiler_params=_cparams(),
    )(*arrs)


def _scatter_parts(name, parts):
    na = len(parts)

    def body(*refs):
        ins, outs = refs[:na], refs[na:2 * na]
        send_sems, recv_sems, local_sems = refs[2 * na:]
        x, y, c = _place()
        me = _index_of((x, y, c))
        flips = [(k >> 2 & 1, k >> 1 & 1, k & 1) for k in range(1, N_DEV)]
        peers = [(1 - x if fx else x, 1 - y if fy else y, 1 - c if fc else c) for fx, fy, fc in flips]
        mine = [pltpu.make_async_copy(ins[a].at[me], outs[a].at[me], local_sems.at[a]) for a in range(na)]
        for cp in mine:
            cp.start()
        sent = []
        for a in range(na):
            for k, peer in enumerate(peers):
                sent.append(pltpu.make_async_remote_copy(
                    src_ref=ins[a].at[_index_of(peer)], dst_ref=outs[a].at[me],
                    send_sem=send_sems.at[a, k], recv_sem=recv_sems.at[a, k], device_id=peer, device_id_type=MESH))
        for cp in sent:
            cp.start()
        for a in range(na):
            for k, peer in enumerate(peers):
                slot = outs[a].at[_index_of(peer)]
                pltpu.make_async_remote_copy(
                    src_ref=slot, dst_ref=slot, send_sem=send_sems.at[a, k], recv_sem=recv_sems.at[a, k],
                    device_id=peer, device_id_type=MESH).wait_recv()
        for cp in sent:
            cp.wait_send()
        for cp in mine:
            cp.wait()

    any_spec = pl.BlockSpec(memory_space=pl.ANY)
    return pl.pallas_call(
        body, name=name,
        out_shape=[jax.ShapeDtypeStruct(p.shape, p.dtype) for p in parts],
        in_specs=[any_spec] * na, out_specs=[any_spec] * na,
        scratch_shapes=[pltpu.SemaphoreType.DMA((na, 7)), pltpu.SemaphoreType.DMA((na, 7)), pltpu.SemaphoreType.DMA((na,))],
        compiler_params=_cparams(),
    )(*parts)


N_CHIPS = 4


def _scatter_to_sibling(name, parts):
    na = len(parts)

    def body(*refs):
        ins, outs = refs[:na], refs[na:2 * na]
        send_sems, recv_sems = refs[2 * na:]
        x, y, c = _place()
        sibling = (x, y, 1 - c)
        sent = []
        for a in range(na):
            for q in range(N_CHIPS):
                sent.append(pltpu.make_async_remote_copy(
                    src_ref=ins[a].at[2 * q + (1 - c)], dst_ref=outs[a].at[q],
                    send_sem=send_sems.at[a, q], recv_sem=recv_sems.at[a, q], device_id=sibling, device_id_type=MESH))
        for cp in sent:
            cp.start()
        for cp in sent:
            cp.wait_recv()
        for cp in sent:
            cp.wait_send()

    any_spec = pl.BlockSpec(memory_space=pl.ANY)
    return pl.pallas_call(
        body, name=name,
        out_shape=[jax.ShapeDtypeStruct((N_CHIPS,) + p.shape[1:], p.dtype) for p in parts],
        in_specs=[any_spec] * na, out_specs=[any_spec] * na,
        scratch_shapes=[pltpu.SemaphoreType.DMA((na, N_CHIPS)), pltpu.SemaphoreType.DMA((na, N_CHIPS))],
        compiler_params=_cparams(),
    )(*parts)


def _pair_sum(name, parts, recv):
    _, R, C = parts.shape
    tr = _pick(R, (256, 128, 64, 32, 16))

    def body(p_ref, r_ref, o_ref):
        c = lax.axis_index("c")
        o_ref[...] = (p_ref[c].astype(F32) + r_ref[...].astype(F32)).astype(BF16)

    return pl.pallas_call(
        body, name=name, grid=(N_CHIPS, R // tr),
        in_specs=[pl.BlockSpec((None, 2, tr, C), lambda q, i: (q, 0, i, 0)), pl.BlockSpec((None, tr, C), lambda q, i: (q, i, 0))],
        out_specs=pl.BlockSpec((None, tr, C), lambda q, i: (q, i, 0)),
        out_shape=jax.ShapeDtypeStruct((N_CHIPS, R, C), BF16),
        compiler_params=_cparams(("parallel", "parallel")),
    )(parts.reshape(N_CHIPS, 2, R, C), recv)


def _scatter_to_chips(name, sums):
    na = len(sums)

    def body(*refs):
        ins, outs = refs[:na], refs[na:2 * na]
        send_sems, recv_sems, local_sems = refs[2 * na:]
        x, y, c = _place()
        me = 2 * x + y
        chips = [(1 - x, y), (x, 1 - y), (1 - x, 1 - y)]
        mine = [pltpu.make_async_copy(ins[a].at[me], outs[a].at[me], local_sems.at[a]) for a in range(na)]
        for cp in mine:
            cp.start()
        sent = []
        for a in range(na):
            for k, (qx, qy) in enumerate(chips):
                sent.append(pltpu.make_async_remote_copy(
                    src_ref=ins[a].at[2 * qx + qy], dst_ref=outs[a].at[me],
                    send_sem=send_sems.at[a, k], recv_sem=recv_sems.at[a, k], device_id=(qx, qy, c), device_id_type=MESH))
        for cp in sent:
            cp.start()
        for a in range(na):
            for k, (qx, qy) in enumerate(chips):
                slot = outs[a].at[2 * qx + qy]
                pltpu.make_async_remote_copy(
                    src_ref=slot, dst_ref=slot, send_sem=send_sems.at[a, k], recv_sem=recv_sems.at[a, k],
                    device_id=(qx, qy, c), device_id_type=MESH).wait_recv()
        for cp in sent:
            cp.wait_send()
        for cp in mine:
            cp.wait()

    any_spec = pl.BlockSpec(memory_space=pl.ANY)
    return pl.pallas_call(
        body, name=name,
        out_shape=[jax.ShapeDtypeStruct(p.shape, p.dtype) for p in sums],
        in_specs=[any_spec] * na, out_specs=[any_spec] * na,
        scratch_shapes=[pltpu.SemaphoreType.DMA((na, 3)), pltpu.SemaphoreType.DMA((na, 3)), pltpu.SemaphoreType.DMA((na,))],
        compiler_params=_cparams(),
    )(*sums)


def _gather_row(name, vec, width):
    n = vec.shape[0]
    rows = jnp.pad(vec, (0, width - n)).reshape(SUBLANES_F32, width // SUBLANES_F32)
    return _allgather_small(name, rows).reshape(N_DEV, width)[:, :n]


def _layer_fwd(x, mod, w):
    sh1, sc1, g1, sh2, sc2, g2 = [mod[i:i + 1] for i in range(N_MOD)]
    S = x.shape[0]
    h1 = _normmod_fwd(x, w["norm1_w"], sc1, sh1)
    proj = _matmul("proj_in", h1, w["w_in"], "nn")
    a1, a_out = _conv_a_fwd(proj, w["conv_a_w"], w["conv_a_b"], w["ln_a_w"], w["ln_a_b"])
    qkv = proj[:, :QKV_BLOCKS * ATT_W].astype(BF16)
    os, ls = zip(*[_attn_fwd(qkv, dil) for dil in DILATIONS])
    att, lse = _attn_merge(os, ls)
    of, st_f = _hgrn_fwd(proj, w["lb_f"], ZF, False)
    ob, st_b = _hgrn_fwd(proj, w["lb_b"], ZB, True)
    rec = _hgrn_post_fwd(of, ob, proj, w["rec_norm_w"])
    mixed = jnp.concatenate([att.astype(BF16), rec, a_out], axis=1)
    y1 = _matmul("proj_out", mixed, w["w_out"], "nn")
    x2 = _gate_add(x, y1, g1)
    h2 = _normmod_fwd(x2, w["norm2_w"], sc2, sh2)
    u = _matmul("ffn_up", h2, w["w_up"], "nn")
    act = _ffn_mid_fwd(u, w["conv_f_w"])
    y2 = _matmul("ffn_down", act, w["w_down"], "nn")
    x3 = _gate_add(x2, y2, g2)
    saved = dict(x=x, h1=h1, proj=proj, a1=a1, qkv=qkv, att=att, lse=lse, of=of, ob=ob, st_f=st_f, st_b=st_b,
                 mixed=mixed, y1=y1, x2=x2, h2=h2, u=u, act=act, y2=y2)
    return x3, saved


def _layer_bwd(gx3, mod, w, s):
    sh1, sc1, g1, sh2, sc2, g2 = [mod[i:i + 1] for i in range(N_MOD)]
    S = gx3.shape[0]
    g = {}
    gy2, gg2 = _gate_bwd(gx3, s["y2"], g2)
    gact = _matmul("ffn_down_dx", gy2, w["w_down"], "nt")
    g["w_down"] = _matmul("ffn_down_dw", s["act"], gy2, "tn")
    gu_g, gu_v, gcw_g, gcw_v = _ffn_mid_bwd(s["u"], gact, w["conv_f_w"])
    gu = jnp.concatenate([gu_g, gu_v], axis=1)
    g["conv_f_w"] = jnp.concatenate([gcw_g, gcw_v], axis=1)
    gh2 = _matmul("ffn_up_dx", gu, w["w_up"], "nt")
    g["w_up"] = _matmul("ffn_up_dw", s["h2"], gu, "tn")
    gx2, g["norm2_w"], gsc2, gsh2 = _normmod_bwd(s["x2"], gh2, gx3, w["norm2_w"], sc2, sh2)
    gy1, gg1 = _gate_bwd(gx2, s["y1"], g1)
    gmixed = _matmul("proj_out_dx", gy1, w["w_out"], "nt")
    g["w_out"] = _matmul("proj_out_dw", s["mixed"], gy1, "tn")
    go, ggr, g["rec_norm_w"] = _hgrn_post_bwd(s["of"], s["ob"], s["proj"], gmixed, w["rec_norm_w"])
    gq_f, gz_f, gv_f, g["lb_f"] = _hgrn_bwd(s["proj"], w["lb_f"], s["st_f"], go, ZF, False)
    gq_b, gz_b, gv_b, g["lb_b"] = _hgrn_bwd(s["proj"], w["lb_b"], s["st_b"], go, ZB, True)
    do, ld = _attn_bwd_prep(gmixed, s["att"], s["lse"])
    gqkv = zip(*[_attn_bwd(s["qkv"], do, ld, dil) for dil in DILATIONS])
    gq_a, gk_a, gv_a = [_sum3_bf16(lst, S, ATT_W) for lst in gqkv]
    gav, gag, gcw, g["conv_a_b"], g["ln_a_w"], g["ln_a_b"] = _conv_a_bwd(
        s["proj"], s["a1"], gmixed, w["conv_a_w"], w["ln_a_w"], w["ln_a_b"])
    g["conv_a_w"] = gcw[:CONV_W]
    gproj = jnp.concatenate([gq_a, gk_a, gv_a, (gq_f + gq_b).astype(BF16), gz_f.astype(BF16), gz_b.astype(BF16),
                             (gv_f + gv_b).astype(BF16), ggr.astype(BF16), gav, gag,
                             jnp.zeros((S, IN_COLS_PAD - IN_COLS), BF16)], axis=1)
    gh1 = _matmul("proj_in_dx", gproj, w["w_in"], "nt")
    g["w_in"] = _matmul("proj_in_dw", s["h1"], gproj, "tn")
    gx, g["norm1_w"], gsc1, gsh1 = _normmod_bwd(s["x"], gh1, gx2, w["norm1_w"], sc1, sh1)
    gmod = jnp.concatenate([gsh1, gsc1, gg1, gsh2, gsc2, gg2], axis=0)
    return gx, gmod, g


def _permute_in_cols(t):
    pad = jnp.zeros(t.shape[:-1] + (IN_COLS_PAD - IN_COLS,), t.dtype)
    return jnp.concatenate([t[..., CONV_COLS:], t[..., :CONV_COLS], pad], axis=-1)


def _unpermute_in_cols(t):
    return jnp.concatenate([t[..., IN_COLS - CONV_COLS:IN_COLS], t[..., :IN_COLS - CONV_COLS]], axis=-1)


def _cols_from_gathered(t, lead):
    nd = t.ndim
    perm = tuple(range(1, nd - 1)) + (0, nd - 1)
    t = t.transpose(perm)
    return t.reshape(t.shape[:-2] + (t.shape[-2] * t.shape[-1],))


def _cols_to_parts(t):
    L, R, C = t.shape
    return t.reshape(L * R, N_DEV, C // N_DEV).transpose(1, 0, 2)


SMALL_REPL = (("norm1_w", D), ("conv_a_b", CONV_CH), ("ln_a_w", CONV_CH), ("ln_a_b", CONV_CH),
              ("rec_norm_w", REC_W), ("norm2_w", D))


def kernel(x, c, w_ada, b_ada, norm1_w, w_in, conv_a_w, conv_a_b, ln_a_w, ln_a_b, lb_gamma, rec_norm_w, w_out, norm2_w, w_up, conv_f_w, w_down, final_norm_w, loss_target, m_w_ada, m_b_ada, m_norm1_w, m_w_in, m_conv_a_w, m_conv_a_b, m_ln_a_w, m_ln_a_b, m_lb_gamma, m_rec_norm_w, m_w_out, m_norm2_w, m_w_up, m_conv_f_w, m_w_down, m_final_norm_w, v_w_ada, v_b_ada, v_norm1_w, v_w_in, v_conv_a_w, v_conv_a_b, v_ln_a_w, v_ln_a_b, v_lb_gamma, v_rec_norm_w, v_w_out, v_norm2_w, v_w_up, v_conv_f_w, v_w_down, v_final_norm_w):
    px, py, pc = _place()
    me = _index_of((px, py, pc))
    xs, tgt = x[0], loss_target[0]
    S = xs.shape[0]
    ada_cols = w_ada.shape[2]

    big = [w_in.reshape(DEPTH * D, -1), w_up.reshape(DEPTH * D, -1), w_out.reshape(-1, D), w_down.reshape(-1, D)]
    g_in, g_up, g_out, g_down = _allgather_big("gather_weights", [t.astype(BF16) for t in big])
    w_in_f = _permute_in_cols(_cols_from_gathered(g_in.reshape(N_DEV, DEPTH, D, -1), 1))
    w_up_f = _cols_from_gathered(g_up.reshape(N_DEV, DEPTH, D, -1), 1)
    w_out_f = g_out.reshape(N_DEV, DEPTH, D // N_DEV, D).transpose(1, 0, 2, 3).reshape(DEPTH, D, D)
    w_out_f = jnp.concatenate([w_out_f[:, CONV_CH:], w_out_f[:, :CONV_CH]], axis=1)
    w_down_f = g_down.reshape(N_DEV, DEPTH, D_FF // N_DEV, D).transpose(1, 0, 2, 3).reshape(DEPTH, D_FF, D)

    small_in = jnp.concatenate([c.reshape(-1), conv_a_w.reshape(-1), lb_gamma.reshape(-1), conv_f_w.reshape(-1)])
    gs = _gather_row("gather_small", small_in, 8192)
    o1 = D
    o2 = o1 + conv_a_w.size
    o3 = o2 + lb_gamma.size
    c_all = gs[:, :o1]
    conv_a_f = _cols_from_gathered(gs[:, o1:o2].reshape(N_DEV, DEPTH, CONV_W, -1), 1)
    lb_gamma_f = _cols_from_gathered(gs[:, o2:o3].reshape(N_DEV, DEPTH, 2, -1), 1)
    conv_f_f = _cols_from_gathered(gs[:, o3:].reshape(N_DEV, DEPTH, FFN_CONV_W, -1), 1)
    conv_a_pad = jnp.pad(conv_a_f, ((0, 0), (0, CONV_W_PAD - CONV_W), (0, 0)))

    b_loc = lax.dynamic_slice_in_dim(b_ada, me * ada_cols, ada_cols, axis=1)

    def mod_fn(c_all_, w_, b_):
        cond = c_all_ * jax.nn.sigmoid(c_all_)
        return (jnp.concatenate([_dot_nn(cond, w_[l], HP) + b_[l] for l in range(DEPTH)], axis=1),)

    (mod_part,) = _vmem_call("ada_mod", mod_fn, [c_all, w_ada, b_loc[:, None, :]], [((N_DEV, DEPTH * ada_cols), F32)])
    gm = _allgather_small("gather_mod", mod_part).reshape(N_DEV, N_DEV, DEPTH, ada_cols)
    mod = lax.dynamic_index_in_dim(gm, me, axis=1, keepdims=False)
    mod = mod.transpose(1, 0, 2).reshape(DEPTH, N_MOD, D)

    (lb1,) = _vmem_call("lower_bounds", lambda a, b: (_lower_bounds_f(a, b),), [lb_gamma_f[0], lb_gamma_f[1]], [((2, REC_W), F32)])
    lb4 = jnp.concatenate([jnp.zeros_like(lb1), lb1], axis=0)

    def layer_weights(l):
        row = lambda t: t[l].reshape(1, -1)
        return dict(norm1_w=row(norm1_w), w_in=w_in_f[l], conv_a_w=conv_a_pad[l], conv_a_b=row(conv_a_b), ln_a_w=row(ln_a_w),
                    ln_a_b=row(ln_a_b), lb_f=lb4[2 * l:2 * l + 1], lb_b=lb4[2 * l + 1:2 * l + 2], rec_norm_w=row(rec_norm_w),
                    w_out=w_out_f[l], norm2_w=row(norm2_w), w_up=w_up_f[l], conv_f_w=conv_f_f[l], w_down=w_down_f[l])

    ws = [layer_weights(l) for l in range(DEPTH)]
    h, saved = xs, []
    for l in range(DEPTH):
        h, s = _layer_fwd(h, mod[l], ws[l])
        saved.append(s)
    gh, g_final, loss_row = _loss_head(h, tgt, final_norm_w.reshape(1, D))
    loss = lax.psum(loss_row[0, 0], ("x", "y", "c"))
    gmods, gws = [None] * DEPTH, [None] * DEPTH
    for l in reversed(range(DEPTH)):
        gh, gmods[l], gws[l] = _layer_bwd(gh, mod[l], ws[l], saved[l])
    grad_x = gh[None]

    glb1 = jnp.concatenate([gws[1]["lb_f"], gws[1]["lb_b"]], axis=0)

    def lb_bwd_fn(a, b, g1):
        _, vjp = jax.vjp(_lower_bounds_f, a, b)
        return vjp(g1)

    g_lb_gamma = jnp.stack(_vmem_call("lower_bounds_bwd", lb_bwd_fn, [lb_gamma_f[0], lb_gamma_f[1], glb1], [((2, REC_W), F32)] * 2))
    pieces = [jnp.stack(gmods).reshape(-1)]
    for l in range(DEPTH):
        pieces += [gws[l][n].reshape(-1) for n, _ in SMALL_REPL]
    pieces += [g_final.reshape(-1)]
    pieces += [jnp.stack([gws[l]["conv_a_w"] for l in range(DEPTH)]).reshape(-1), g_lb_gamma.reshape(-1),
               jnp.stack([gws[l]["conv_f_w"] for l in range(DEPTH)]).reshape(-1)]
    small_g = jnp.concatenate(pieces)
    n_small = small_g.shape[0]
    gsm = _gather_row("gather_small_grads", small_g, 71680)
    n_mod = DEPTH * N_MOD * D
    gmod_all = gsm[:, :n_mod].reshape(N_DEV, DEPTH, N_MOD * D)
    gmod_loc = lax.dynamic_slice_in_dim(gmod_all, me * ada_cols, ada_cols, axis=2).transpose(1, 0, 2)

    def small_fn(gsm_, c_all_, gm_):
        cond = c_all_ * jax.nn.sigmoid(c_all_)
        gw = jnp.concatenate([_dot_tn(cond, gm_[l], HP) for l in range(DEPTH)], axis=0)
        return jnp.sum(gsm_, axis=0, keepdims=True), gw

    tot, g_w_ada = _vmem_call("small_grads", small_fn, [gsm, c_all, gmod_loc],
                              [((1, n_small), F32), ((DEPTH * D, ada_cols), F32)])
    tot = tot[0]
    grads = {"w_ada": g_w_ada.reshape(DEPTH, D, ada_cols), "b_ada": tot[:n_mod].reshape(DEPTH, N_MOD * D)}
    pos = n_mod
    per_layer = {n: [] for n, _ in SMALL_REPL}
    for l in range(DEPTH):
        for n, width in SMALL_REPL:
            per_layer[n].append(tot[pos:pos + width])
            pos += width
    for n, _ in SMALL_REPL:
        grads[n] = jnp.stack(per_layer[n])
    grads["final_norm_w"] = tot[pos:pos + D]
    pos += D
    n_ca, n_lb, n_cf = DEPTH * CONV_W * CONV_CH, DEPTH * 2 * REC_W, DEPTH * FFN_CONV_W * 2 * D_FF
    g_ca = tot[pos:pos + n_ca].reshape(DEPTH, CONV_W, CONV_CH)
    g_lb = tot[pos + n_ca:pos + n_ca + n_lb].reshape(DEPTH, 2, REC_W)
    g_cf = tot[pos + n_ca + n_lb:pos + n_ca + n_lb + n_cf].reshape(DEPTH, FFN_CONV_W, 2 * D_FF)
    grads["conv_a_w"] = lax.dynamic_slice_in_dim(g_ca, me * conv_a_w.shape[2], conv_a_w.shape[2], axis=2)
    grads["lb_gamma"] = lax.dynamic_slice_in_dim(g_lb, me * lb_gamma.shape[2], lb_gamma.shape[2], axis=2)
    grads["conv_f_w"] = lax.dynamic_slice_in_dim(g_cf, me * conv_f_w.shape[2], conv_f_w.shape[2], axis=2)

    gw_in = _unpermute_in_cols(jnp.stack([gws[l]["w_in"] for l in range(DEPTH)]))
    gw_up = jnp.stack([gws[l]["w_up"] for l in range(DEPTH)])
    gw_out = jnp.stack([gws[l]["w_out"] for l in range(DEPTH)])
    gw_out = jnp.concatenate([gw_out[:, D - CONV_CH:], gw_out[:, :D - CONV_CH]], axis=1)
    gw_down = jnp.stack([gws[l]["w_down"] for l in range(DEPTH)])
    rows_to_parts = lambda t: t.reshape(DEPTH, N_DEV, -1, D).transpose(1, 0, 2, 3).reshape(N_DEV, -1, D)
    parts = [_cols_to_parts(gw_in), _cols_to_parts(gw_up), rows_to_parts(gw_out), rows_to_parts(gw_down)]
    parts = [t.astype(BF16) for t in parts]
    from_sibling = _scatter_to_sibling("scatter_sibling", parts)
    sums = [_pair_sum("pair_sum", p, r) for p, r in zip(parts, from_sibling)]
    r_in, r_up, r_out, r_down = _scatter_to_chips("scatter_chips", sums)

    given = dict(w_ada=(w_ada, m_w_ada, v_w_ada), b_ada=(b_ada, m_b_ada, v_b_ada), norm1_w=(norm1_w, m_norm1_w, v_norm1_w),
                 w_in=(w_in, m_w_in, v_w_in), conv_a_w=(conv_a_w, m_conv_a_w, v_conv_a_w), conv_a_b=(conv_a_b, m_conv_a_b, v_conv_a_b),
                 ln_a_w=(ln_a_w, m_ln_a_w, v_ln_a_w), ln_a_b=(ln_a_b, m_ln_a_b, v_ln_a_b), lb_gamma=(lb_gamma, m_lb_gamma, v_lb_gamma),
                 rec_norm_w=(rec_norm_w, m_rec_norm_w, v_rec_norm_w), w_out=(w_out, m_w_out, v_w_out),
                 norm2_w=(norm2_w, m_norm2_w, v_norm2_w), w_up=(w_up, m_w_up, v_w_up), conv_f_w=(conv_f_w, m_conv_f_w, v_conv_f_w),
                 w_down=(w_down, m_w_down, v_w_down), final_norm_w=(final_norm_w, m_final_norm_w, v_final_norm_w))
    big_parts = dict(w_in=r_in, w_up=r_up, w_out=r_out, w_down=r_down)
    names = list(given)
    res = {}
    for n in names:
        w_, m_, v_ = given[n]
        shape = w_.shape
        C = shape[-1]
        two_d = lambda t: t.reshape(-1, C)
        gp = big_parts[n] if n in big_parts else two_d(grads[n])[None]
        res[n] = [t.reshape(shape) for t in _adamw("adamw_" + n, two_d(w_), two_d(m_), two_d(v_), gp)]
    return (loss, grad_x, *[res[n][0] for n in names], *[res[n][1] for n in names],
            *[res[n][2] for n in names], *[res[n][3] for n in names])
```

```python
import functools

import numpy as np
import jax
import jax.numpy as jnp
from jax import lax
from jax.experimental import pallas as pl
from jax.experimental.pallas import tpu as pltpu

F32 = jnp.float32
BF16 = jnp.bfloat16
HP = lax.Precision.HIGHEST
MESH = pl.DeviceIdType.MESH

N_DEV = 8
D = 1024
DEPTH = 2
CONV_CH = 256
CONV_W = 31
CONV_W_PAD = 32
ATT_W = 384
REC_W = 384
N_HEADS = 6
HEAD = 64
HEAD_SHIFT = 6
HALF_BAND = 64
ATT_BLK = 128
DILATIONS = (1, 4, 16)
ALIBI_SLOPES = tuple(float(2.0 ** (-8.0 * (h + 1) / N_HEADS)) for h in range(N_HEADS))
MASK_VALUE = -1e30
REC_CHUNK = 64
EXP_CLAMP = 80.0
F_TINY = 1e-30
IN_COLS = 3584
IN_COLS_PAD = IN_COLS
QKV_BLOCKS = 3
D_FF = 2816
FFN_CONV_W = 3
N_MOD = 6
EPS = 1e-6
ADAM_LR, ADAM_B1, ADAM_B2, ADAM_EPS, ADAM_WD, ADAM_STEP = 0.001, 0.9, 0.999, 1e-08, 0.01, 10

VMEM_LIMIT_BYTES = 56 * 1024 * 1024
SUBLANES_F32 = 8
LANES = 128

QA, KA, VA, QR, ZF, ZB, IR, GR = range(8)
AV_BLK, AG_BLK = 12, 13
CONV_COLS = 2 * CONV_CH


def _cparams(sem=None):
    kw = dict(vmem_limit_bytes=VMEM_LIMIT_BYTES)
    if sem is not None:
        kw["dimension_semantics"] = sem
    return pltpu.CompilerParams(**kw)


def _iota(shape, dim):
    return lax.broadcasted_iota(jnp.int32, shape, dim)


def _dot(a, b, dims, precision=None):
    return lax.dot_general(a, b, (dims, ((), ())), precision=precision, preferred_element_type=F32)


def _dot_nn(a, b, precision=None):
    return _dot(a, b, ((1,), (0,)), precision)


def _dot_nt(a, b, precision=None):
    return _dot(a, b, ((1,), (1,)), precision)


def _dot_tn(a, b, precision=None):
    return _dot(a, b, ((0,), (0,)), precision)


def _c0(j):
    return 0


def _pick(n, cands):
    for c in cands:
        if n % c == 0:
            return c
    return n


MATMUL_OUT_TILE_BYTES = 8 * 1024 * 1024


def _div_lanes(n, cap):
    best = None
    for d in range(LANES, min(n, cap) + 1, LANES):
        if n % d == 0:
            best = d
    return best if best is not None else n


def _matmul_tiles(mode, M, N, K):
    if mode == "nn":
        tm = _pick(M, (1024, 512, 256, 128))
        return tm, _div_lanes(N, MATMUL_OUT_TILE_BYTES // (4 * tm)), K
    if mode == "nt":
        return _pick(M, (512, 256, 128)), N, K
    tm = _div_lanes(M, 1408)
    return tm, _div_lanes(N, MATMUL_OUT_TILE_BYTES // (4 * tm)), _pick(K, (1024, 512, 256))


def _matmul(name, a, b, mode, out_dtype=F32):
    if mode == "nn":
        (M, K), (_, N) = a.shape, b.shape
    elif mode == "nt":
        (M, K), (N, _) = a.shape, b.shape
    else:
        (K, M), (_, N) = a.shape, b.shape
    tm, tn, tk = _matmul_tiles(mode, M, N, K)
    nk = K // tk
    if mode == "nn":
        a_spec = pl.BlockSpec((tm, tk), lambda i, j, k: (i, k))
        b_spec = pl.BlockSpec((tk, tn), lambda i, j, k: (k, j))
        dims = ((1,), (0,))
    elif mode == "nt":
        a_spec = pl.BlockSpec((tm, tk), lambda i, j, k: (i, k))
        b_spec = pl.BlockSpec((tn, tk), lambda i, j, k: (j, k))
        dims = ((1,), (1,))
    else:
        a_spec = pl.BlockSpec((tk, tm), lambda i, j, k: (k, i))
        b_spec = pl.BlockSpec((tk, tn), lambda i, j, k: (k, j))
        dims = ((0,), (0,))

    def body_whole(a_ref, b_ref, o_ref):
        o_ref[...] = _dot(a_ref[...].astype(BF16), b_ref[...].astype(BF16), dims).astype(o_ref.dtype)

    def body(a_ref, b_ref, o_ref, acc_ref):
        k = pl.program_id(2)
        part = _dot(a_ref[...].astype(BF16), b_ref[...].astype(BF16), dims)

        @pl.when(k == 0)
        def _():
            acc_ref[...] = part

        @pl.when(k > 0)
        def _():
            acc_ref[...] += part

        @pl.when(k == nk - 1)
        def _():
            o_ref[...] = acc_ref[...].astype(o_ref.dtype)

    return pl.pallas_call(
        body_whole if nk == 1 else body, name=name, grid=(M // tm, N // tn, nk),
        in_specs=[a_spec, b_spec],
        out_specs=pl.BlockSpec((tm, tn), lambda i, j, k: (i, j)),
        out_shape=jax.ShapeDtypeStruct((M, N), out_dtype),
        scratch_shapes=[] if nk == 1 else [pltpu.VMEM((tm, tn), F32)],
        compiler_params=_cparams(("parallel", "parallel", "arbitrary")),
    )(a, b)


def _rowwise(name, fn, S, ts, tiles, params=(), outs=(), accs=(), halo=0, ncb=1):
    in_specs, args, scratch = [], [], []
    for arr, w, jm, with_halo in tiles:
        if isinstance(with_halo, int) and with_halo > 1:
            d = with_halo
            in_specs.append(pl.BlockSpec((ts // d, d * w), lambda j, i: (i, 0)))
            args.append(arr)
            scratch.append(pltpu.VMEM((w // LANES, ts, LANES), F32))
        elif with_halo:
            hb, nhb = ts // halo, S // halo
            in_specs += [
                pl.BlockSpec((halo, w), lambda j, i, jm=jm, hb=hb: (jnp.maximum(i * hb - 1, 0), jm(j))),
                pl.BlockSpec((ts, w), lambda j, i, jm=jm: (i, jm(j))),
                pl.BlockSpec((halo, w), lambda j, i, jm=jm, hb=hb, nhb=nhb: (jnp.minimum((i + 1) * hb, nhb - 1), jm(j))),
            ]
            args += [arr, arr, arr]
        else:
            in_specs.append(pl.BlockSpec((ts, w), lambda j, i, jm=jm: (i, jm(j))))
            args.append(arr)
    for arr, r, w, jm in params:
        in_specs.append(pl.BlockSpec((r, w), lambda j, i, jm=jm: (0, jm(j))))
        args.append(arr)
    out_specs, out_shape = [], []
    for w, dt, jm, tw, *dil in outs:
        if dil:
            out_specs.append(pl.BlockSpec((ts // dil[0], dil[0] * w), lambda j, i: (i, 0)))
            out_shape.append(jax.ShapeDtypeStruct((S // dil[0], dil[0] * w), dt))
            scratch += [pltpu.VMEM((ts, w), F32), pltpu.VMEM((w // LANES, ts, LANES), F32)]
        else:
            out_specs.append(pl.BlockSpec((ts, w), lambda j, i, jm=jm: (i, jm(j))))
            out_shape.append(jax.ShapeDtypeStruct((S, tw), dt))
    for r, w, jm, tw in accs:
        out_specs.append(pl.BlockSpec((r, w), lambda j, i, jm=jm: (0, jm(j))))
        out_shape.append(jax.ShapeDtypeStruct((r, tw), F32))
    n_tiles, n_params, n_outs, n_accs = len(tiles), len(params), len(outs), len(accs)

    def residue_rows(r, d):
        return pl.ds(r, ts // d, stride=d)

    def body(*refs):
        i = pl.program_id(1)
        n_io = len(in_specs) + n_outs + n_accs
        scr = list(refs[n_io:])
        refs = refs[:n_io]
        pos, vals = 0, []
        for _, w, _, with_halo in tiles:
            if isinstance(with_halo, int) and with_halo > 1:
                d, buf = with_halo, scr.pop(0)
                for r in range(d):
                    for c in range(w // LANES):
                        buf[c, residue_rows(r, d), :] = refs[pos][:, r * w + c * LANES:r * w + (c + 1) * LANES].astype(F32)
                vals.append(jnp.concatenate([buf[c] for c in range(w // LANES)], axis=1))
                pos += 1
            elif with_halo:
                before, after = refs[pos][...], refs[pos + 2][...]
                before = jnp.where(i > 0, before, jnp.zeros_like(before))
                after = jnp.where(i < S // ts - 1, after, jnp.zeros_like(after))
                vals.append(jnp.concatenate([before, refs[pos + 1][...], after], axis=0))
                pos += 3
            else:
                vals.append(refs[pos][...])
                pos += 1
        prefs = refs[pos:pos + n_params]
        orefs = list(refs[pos + n_params:pos + n_params + n_outs])
        arefs = refs[pos + n_params + n_outs:]
        staged = []
        for k, (w, _, _, _, *dil) in enumerate(outs):
            if dil:
                staged.append((orefs[k], scr.pop(0), scr.pop(0), w, dil[0]))
                orefs[k] = staged[-1][1]

        @pl.when(i == 0)
        def _():
            for r in arefs:
                r[...] = jnp.zeros_like(r)

        fn(i, vals, prefs, orefs, arefs)
        for out_ref, flat, buf, w, d in staged:
            for c in range(w // LANES):
                buf[c] = flat[:, c * LANES:(c + 1) * LANES]
                for r in range(d):
                    out_ref[:, r * w + c * LANES:r * w + (c + 1) * LANES] = buf[c, residue_rows(r, d), :].astype(out_ref.dtype)

    res = pl.pallas_call(
        body, name=name, grid=(ncb, S // ts),
        in_specs=in_specs, out_specs=out_specs, out_shape=out_shape, scratch_shapes=scratch,
        compiler_params=_cparams(("arbitrary", "arbitrary")),
    )(*args)
    return res


def _vmem_call(name, fn, ins, out_shapes):
    n_in = len(ins)

    def body(*refs):
        vals = fn(*[r[...] for r in refs[:n_in]])
        for r, v in zip(refs[n_in:], vals):
            r[...] = v.astype(r.dtype)

    return pl.pallas_call(
        body, name=name,
        out_shape=[jax.ShapeDtypeStruct(s, dt) for s, dt in out_shapes],
        compiler_params=_cparams(),
    )(*ins)


def _rms(x, w):
    return x * lax.rsqrt(jnp.mean(x * x, axis=-1, keepdims=True) + EPS) * w


def _normmod_f(x, nw, sc, sh):
    return _rms(x, nw) * (1.0 + sc) + sh


def _row_params(*vecs):
    return [(v, 1, v.shape[1], _c0) for v in vecs]


def _normmod_fwd(x, nw, sc, sh):
    S = x.shape[0]

    def fn(i, vals, p, o, a):
        o[0][...] = _normmod_f(vals[0], p[0][...], p[1][...], p[2][...]).astype(BF16)

    return _rowwise("normmod_fwd", fn, S, 512, [(x, D, _c0, False)], _row_params(nw, sc, sh), [(D, BF16, _c0, D)])[0]


def _normmod_bwd(x, gh, gres, nw, sc, sh):
    S = x.shape[0]

    def fn(i, vals, p, o, a):
        _, vjp = jax.vjp(_normmod_f, vals[0], p[0][...], p[1][...], p[2][...])
        gx, gnw, gsc, gsh = vjp(vals[1])
        o[0][...] = gx + vals[2]
        a[0][...] += gnw
        a[1][...] += gsc
        a[2][...] += gsh

    return _rowwise("normmod_bwd", fn, S, 256, [(x, D, _c0, False), (gh, D, _c0, False), (gres, D, _c0, False)],
                    _row_params(nw, sc, sh), [(D, F32, _c0, D)], [(1, D, _c0, D)] * 3)


def _gate_add(x, y, g):
    S = x.shape[0]

    def fn(i, vals, p, o, a):
        o[0][...] = vals[0] + p[0][...] * vals[1]

    return _rowwise("gate_add", fn, S, 512, [(x, D, _c0, False), (y, D, _c0, False)], _row_params(g), [(D, F32, _c0, D)])[0]


def _gate_bwd(gx, y, g):
    S = gx.shape[0]

    def fn(i, vals, p, o, a):
        o[0][...] = (vals[0] * p[0][...]).astype(BF16)
        a[0][...] += jnp.sum(vals[0] * vals[1], axis=0, keepdims=True)

    return _rowwise("gate_bwd", fn, S, 512, [(gx, D, _c0, False), (y, D, _c0, False)], _row_params(g),
                    [(D, BF16, _c0, D)], [(1, D, _c0, D)])


def _loss_head(x, tgt, fw):
    S = x.shape[0]

    def fn(i, vals, p, o, a):
        y, vjp = jax.vjp(_rms, vals[0], p[0][...])
        err = y - vals[1]
        gx, gfw = vjp(err * (1.0 / D))
        o[0][...] = gx
        a[0][...] += gfw
        part = 0.5 * jnp.sum(jnp.mean(err * err, axis=-1, keepdims=True), axis=0, keepdims=True)
        a[1][...] += jnp.broadcast_to(part, (1, LANES))

    return _rowwise("loss_head", fn, S, 256, [(x, D, _c0, False), (tgt, D, _c0, False)], _row_params(fw),
                    [(D, F32, _c0, D)], [(1, D, _c0, D), (1, LANES, _c0, LANES)])


CONV_HALO = 16
CONV_TS = 512


def _shifted(ext, shift, ts, halo):
    n = ext.shape[0]
    s = shift % n
    r = ext if s == 0 else pltpu.roll(ext, s, 0)
    return r[halo:halo + ts]


def _ln_silu(a, w, b):
    mu = jnp.mean(a, axis=-1, keepdims=True)
    var = jnp.mean(jnp.square(a - mu), axis=-1, keepdims=True)
    y = (a - mu) * lax.rsqrt(var + EPS) * w + b
    return y * jax.nn.sigmoid(y)


def _conv_a_fwd(proj, w_pad, b, lnw, lnb):
    S = proj.shape[0]
    ts, H = min(CONV_TS, S), CONV_HALO

    def fn(i, vals, p, o, a):
        a0 = vals[0] * jax.nn.sigmoid(vals[1])
        acc = jnp.zeros((ts, CONV_CH), F32) + p[1][...]
        for k in range(CONV_W):
            acc = acc + _shifted(a0, CONV_W // 2 - k, ts, H) * p[0][pl.ds(k, 1), :]
        o[0][...] = acc
        o[1][...] = _ln_silu(acc, p[2][...], p[3][...]).astype(BF16)

    tiles = [(proj, CONV_CH, lambda j: AV_BLK, True), (proj, CONV_CH, lambda j: AG_BLK, True)]
    params = [(w_pad, CONV_W_PAD, CONV_CH, _c0)] + _row_params(b, lnw, lnb)
    return _rowwise("conv_a_fwd", fn, S, ts, tiles, params, [(CONV_CH, F32, _c0, CONV_CH), (CONV_CH, BF16, _c0, CONV_CH)], halo=H)


def _conv_a_bwd(proj, a1, gmixed, w_pad, lnw, lnb):
    S = proj.shape[0]
    ts, H = min(CONV_TS, S), CONV_HALO

    def fn(i, vals, p, o, a):
        av, ag, a1e, ge = vals
        lw, lb = p[1][...], p[2][...]
        _, vjp_e = jax.vjp(lambda t: _ln_silu(t, lw, lb), a1e)
        (ga1e,) = vjp_e(ge)
        c = slice(H, H + ts)
        _, vjp_c = jax.vjp(_ln_silu, a1e[c], lw, lb)
        ga1, glw, glb = vjp_c(ge[c])
        a[1][...] += jnp.sum(ga1, axis=0, keepdims=True)
        a[2][...] += glw
        a[3][...] += glb
        sg = jax.nn.sigmoid(ag)
        a0 = av * sg
        ga0 = jnp.zeros((ts, CONV_CH), F32)
        for k in range(CONV_W):
            a[0][pl.ds(k, 1), :] += jnp.sum(ga1 * _shifted(a0, CONV_W // 2 - k, ts, H), axis=0, keepdims=True)
            ga0 = ga0 + _shifted(ga1e, k - CONV_W // 2, ts, H) * p[0][pl.ds(k, 1), :]
        sgc, avc = sg[c], av[c]
        o[0][...] = (ga0 * sgc).astype(BF16)
        o[1][...] = (ga0 * avc * sgc * (1.0 - sgc)).astype(BF16)

    tiles = [(proj, CONV_CH, lambda j: AV_BLK, True), (proj, CONV_CH, lambda j: AG_BLK, True),
             (a1, CONV_CH, _c0, True), (gmixed, CONV_CH, lambda j: 3, True)]
    params = [(w_pad, CONV_W_PAD, CONV_CH, _c0)] + _row_params(lnw, lnb)
    outs = [(CONV_CH, BF16, _c0, CONV_CH), (CONV_CH, BF16, _c0, CONV_CH)]
    accs = [(CONV_W_PAD, CONV_CH, _c0, CONV_CH)] + [(1, CONV_CH, _c0, CONV_CH)] * 3
    return _rowwise("conv_a_bwd", fn, S, ts, tiles, params, outs, accs, halo=H)


FFN_HALO = 8
FFN_TS = 512
FFN_CB = 256
FFN_UB = 2 * FFN_CB
FFN_NCB = D_FF // FFN_CB


def _interleave_ffn_cols(t):
    blocks = []
    for j in range(FFN_NCB):
        blocks += [t[..., j * FFN_CB:(j + 1) * FFN_CB], t[..., D_FF + j * FFN_CB:D_FF + (j + 1) * FFN_CB]]
    return jnp.concatenate(blocks, axis=-1)


def _deinterleave_ffn_cols(t):
    gate = [t[..., j * FFN_UB:j * FFN_UB + FFN_CB] for j in range(FFN_NCB)]
    val = [t[..., j * FFN_UB + FFN_CB:(j + 1) * FFN_UB] for j in range(FFN_NCB)]
    return jnp.concatenate(gate + val, axis=-1)


def _gelu_mul(g, v):
    return 0.5 * g * (1.0 + lax.erf(g * (2.0 ** -0.5))) * v


def _ffn_mid_fwd(u, cw):
    S = u.shape[0]
    ts, H = min(FFN_TS, S), FFN_HALO

    def fn(i, vals, p, o, a):
        c = jnp.zeros((ts, FFN_UB), F32)
        for k in range(FFN_CONV_W):
            c = c + _shifted(vals[0], 1 - k, ts, H) * p[0][pl.ds(k, 1), :]
        o[0][...] = _gelu_mul(c[:, :FFN_CB], c[:, FFN_CB:]).astype(BF16)

    ident = lambda j: j
    return _rowwise("ffn_mid_fwd", fn, S, ts, [(u, FFN_UB, ident, True)], [(cw, FFN_CONV_W, FFN_UB, ident)],
                    [(FFN_CB, BF16, ident, D_FF)], halo=H, ncb=FFN_NCB)[0]


def _ffn_mid_bwd(u, gact, cw):
    S = u.shape[0]
    ts, H = min(FFN_TS, S), FFN_HALO
    n = ts + 2 * H

    def fn(i, vals, p, o, a):
        ue, ga = vals
        conv = jnp.zeros((n, FFN_UB), F32)
        for k in range(FFN_CONV_W):
            s = (1 - k) % n
            conv = conv + (ue if s == 0 else pltpu.roll(ue, s, 0)) * p[0][pl.ds(k, 1), :]
        _, vjp = jax.vjp(_gelu_mul, conv[:, :FFN_CB], conv[:, FFN_CB:])
        gc = jnp.concatenate(vjp(ga), axis=1)
        gu = jnp.zeros((ts, FFN_UB), F32)
        for k in range(FFN_CONV_W):
            gu = gu + _shifted(gc, k - 1, ts, H) * p[0][pl.ds(k, 1), :]
            a[0][pl.ds(k, 1), :] += jnp.sum(gc[H:H + ts] * _shifted(ue, 1 - k, ts, H), axis=0, keepdims=True)
        o[0][...] = gu.astype(BF16)

    ident = lambda j: j
    return _rowwise("ffn_mid_bwd", fn, S, ts, [(u, FFN_UB, ident, True), (gact, FFN_CB, ident, True)],
                    [(cw, FFN_CONV_W, FFN_UB, ident)], [(FFN_UB, BF16, ident, 2 * D_FF)],
                    [(FFN_CONV_W, FFN_UB, ident, 2 * D_FF)], halo=H, ncb=FFN_NCB)


LD_W = LANES
PAIR_W = 2 * HEAD
N_PAIRS = N_HEADS // 2


def _sub_view(t, d):
    S, C = t.shape
    return t.reshape(S // d, d * C)


def _sub_halo_specs(width, col, blk, hb, nhb):
    per = blk // hb
    return [
        pl.BlockSpec((hb, width), lambda r, i: (jnp.maximum(i * per - 1, 0), col(r))),
        pl.BlockSpec((blk, width), lambda r, i: (i, col(r))),
        pl.BlockSpec((hb, width), lambda r, i: (jnp.minimum((i + 1) * per, nhb - 1), col(r))),
    ]


def _pick_lane(t, lane):
    return jnp.sum(jnp.where(_iota((1, t.shape[1]), 1) == lane, t, 0.0), axis=1, keepdims=True)


def _pair_mask(h2):
    return (_iota((1, PAIR_W), 1) >> HEAD_SHIFT) == h2


def _cat_bf16(a, b, c):
    return jnp.concatenate([a[...], b[...], c[...]], axis=0).astype(BF16)


def _attn_fwd(view, dil):
    L = view.shape[0]
    blk, hb = min(ATT_BLK, L), HALF_BAND
    span = blk + 2 * hb

    def body(q_ref, kp, kc, kn, vp, vc, vn, o_ref, l_ref):
        i = pl.program_id(1)
        rel = _iota((blk, span), 1) - hb - _iota((blk, span), 0)
        kpos = i * blk - hb + _iota((blk, span), 1)
        valid = (jnp.abs(rel) <= hb) & (kpos >= 0) & (kpos < L)
        dist = jnp.abs(rel).astype(F32) * float(dil)
        q, k, v = q_ref[...].astype(BF16), _cat_bf16(kp, kc, kn), _cat_bf16(vp, vc, vn)
        lse = jnp.zeros((blk, LD_W), F32)
        for pr in range(N_PAIRS):
            sl = slice(pr * PAIR_W, (pr + 1) * PAIR_W)
            qp, kpair, vpair = q[:, sl], k[:, sl], v[:, sl]
            o = jnp.zeros((blk, PAIR_W), F32)
            for h2 in range(2):
                h, mask = 2 * pr + h2, _pair_mask(h2)
                s = _dot_nt(jnp.where(mask, qp, jnp.zeros_like(qp)), kpair) * (HEAD ** -0.5) - ALIBI_SLOPES[h] * dist
                s = jnp.where(valid, s, MASK_VALUE)
                m = jnp.max(s, axis=1, keepdims=True)
                p = jnp.exp(s - m)
                l = jnp.sum(p, axis=1, keepdims=True)
                o = jnp.where(mask, _dot_nn(p.astype(BF16), vpair) / l, o)
                lse = lse + jnp.where(_iota((1, LD_W), 1) == h, m + jnp.log(l), 0.0)
            o_ref[:, sl] = o
        l_ref[...] = lse

    nhb = L // hb
    in_specs = ([pl.BlockSpec((blk, ATT_W), lambda r, i: (i, r * QKV_BLOCKS +QA))]
                + _sub_halo_specs(ATT_W, lambda r: r * QKV_BLOCKS +KA, blk, hb, nhb)
                + _sub_halo_specs(ATT_W, lambda r: r * QKV_BLOCKS +VA, blk, hb, nhb))
    return pl.pallas_call(
        body, name=f"attn_fwd_d{dil}", grid=(dil, L // blk), in_specs=in_specs,
        out_specs=[pl.BlockSpec((blk, ATT_W), lambda r, i: (i, r)), pl.BlockSpec((blk, LD_W), lambda r, i: (i, r))],
        out_shape=[jax.ShapeDtypeStruct((L, dil * ATT_W), F32), jax.ShapeDtypeStruct((L, dil * LD_W), F32)],
        compiler_params=_cparams(("parallel", "parallel")),
    )(*([view] * 7))


def _attn_bwd(pview, gview, lview, dil):
    L = pview.shape[0]
    blk, hb = min(ATT_BLK, L), HALF_BAND
    span = blk + 2 * hb
    scale = HEAD ** -0.5

    def body(qp, qc, qn, kp, kc, kn, vp, vc, vn, gp, gc, gn, lp, lc, ln, dq_ref, dk_ref, dv_ref):
        i = pl.program_id(1)
        l = lc[...]
        le = jnp.concatenate([lp[...], l, ln[...]], axis=0)
        rel_q = _iota((blk, span), 1) - hb - _iota((blk, span), 0)
        kpos = i * blk - hb + _iota((blk, span), 1)
        valid_q = (jnp.abs(rel_q) <= hb) & (kpos >= 0) & (kpos < L)
        dist_q = jnp.abs(rel_q).astype(F32) * float(dil)
        rel_k = _iota((span, blk), 1) + hb - _iota((span, blk), 0)
        qpos = i * blk - hb + _iota((span, blk), 0)
        valid_k = (jnp.abs(rel_k) <= hb) & (qpos >= 0) & (qpos < L)
        dist_k = jnp.abs(rel_k).astype(F32) * float(dil)
        q_all, k_all, v_all, g_all = qc[...].astype(BF16), kc[...].astype(BF16), vc[...].astype(BF16), gc[...]
        qe_all, ke_all, ve_all = _cat_bf16(qp, qc, qn), _cat_bf16(kp, kc, kn), _cat_bf16(vp, vc, vn)
        ge_all = _cat_bf16(gp, gc, gn)
        for pr in range(N_PAIRS):
            sl = slice(pr * PAIR_W, (pr + 1) * PAIR_W)
            q, k, v, g = q_all[:, sl], k_all[:, sl], v_all[:, sl], g_all[:, sl]
            qe, ke, ve, ge = qe_all[:, sl], ke_all[:, sl], ve_all[:, sl], ge_all[:, sl]
            dq = jnp.zeros((blk, PAIR_W), F32)
            dk = jnp.zeros((blk, PAIR_W), F32)
            dv = jnp.zeros((blk, PAIR_W), F32)
            for h2 in range(2):
                h, mask = 2 * pr + h2, _pair_mask(h2)
                only = lambda t: jnp.where(mask, t, jnp.zeros_like(t))
                s = _dot_nt(only(q), ke) * scale - ALIBI_SLOPES[h] * dist_q
                p = jnp.where(valid_q, jnp.exp(s - _pick_lane(l, h)), 0.0)
                ds = p * (_dot_nt(only(g), ve) - _pick_lane(l, 8 + h))
                dq = jnp.where(mask, _dot_nn(ds.astype(BF16), ke), dq)
                s = _dot_nt(only(qe), k) * scale - ALIBI_SLOPES[h] * dist_k
                p = jnp.where(valid_k, jnp.exp(s - _pick_lane(le, h)), 0.0)
                dv = jnp.where(mask, _dot_tn(p.astype(BF16), ge), dv)
                ds = p * (_dot_nt(only(ge), v) - _pick_lane(le, 8 + h))
                dk = jnp.where(mask, _dot_tn(ds.astype(BF16), qe), dk)
            dq_ref[:, sl] = (dq * scale).astype(BF16)
            dk_ref[:, sl] = (dk * scale).astype(BF16)
            dv_ref[:, sl] = dv.astype(BF16)

    nhb = L // hb
    in_specs = (_sub_halo_specs(ATT_W, lambda r: r * QKV_BLOCKS +QA, blk, hb, nhb)
                + _sub_halo_specs(ATT_W, lambda r: r * QKV_BLOCKS +KA, blk, hb, nhb)
                + _sub_halo_specs(ATT_W, lambda r: r * QKV_BLOCKS +VA, blk, hb, nhb)
                + _sub_halo_specs(ATT_W, lambda r: r, blk, hb, nhb) + _sub_halo_specs(LD_W, lambda r: r, blk, hb, nhb))
    o_spec = pl.BlockSpec((blk, ATT_W), lambda r, i: (i, r))
    return pl.pallas_call(
        body, name=f"attn_bwd_d{dil}", grid=(dil, L // blk), in_specs=in_specs,
        out_specs=[o_spec] * 3, out_shape=[jax.ShapeDtypeStruct((L, dil * ATT_W), BF16)] * 3,
        compiler_params=_cparams(("parallel", "parallel")),
    )(*([pview] * 9 + [gview] * 3 + [lview] * 3))


def _head_expand(t):
    e = ((_iota((LD_W, ATT_W), 1) >> HEAD_SHIFT) == _iota((LD_W, ATT_W), 0)).astype(F32)
    return _dot_nn(t, e, HP)


def _dil(d):
    return d if d > 1 else False


def _qkv_views(proj):
    S, w = proj.shape[0], QKV_BLOCKS * ATT_W

    def fn(i, vals, p, o, a):
        for k, d in enumerate(DILATIONS):
            o[k][...] = vals[0].astype(o[k].dtype)

    outs = [(w, BF16, _c0, w) + ((d,) if d > 1 else ()) for d in DILATIONS]
    return _rowwise("qkv_views", fn, S, 512, [(proj, w, _c0, False)], (), outs)


def _attn_merge(os, ls):
    S = os[0].shape[0]

    def fn(i, vals, p, o, a):
        o3, l3 = vals[:3], vals[3:]
        m = jnp.maximum(jnp.maximum(l3[0], l3[1]), l3[2])
        e3 = [jnp.exp(l - m) for l in l3]
        den = e3[0] + e3[1] + e3[2]
        out = jnp.zeros((o3[0].shape[0], ATT_W), F32)
        for ob, e in zip(o3, e3):
            out = out + _head_expand(e / den) * ob
        o[0][...] = out
        o[1][...] = m + jnp.log(den)

    tiles = ([(t, ATT_W, _c0, _dil(d)) for t, d in zip(os, DILATIONS)]
             + [(t, LD_W, _c0, _dil(d)) for t, d in zip(ls, DILATIONS)])
    return _rowwise("attn_merge", fn, S, 512, tiles, (), [(ATT_W, F32, _c0, ATT_W), (LD_W, F32, _c0, LD_W)])


def _attn_bwd_prep(gmixed, att, lse):
    S = att.shape[0]
    n = len(DILATIONS)

    def fn(i, vals, p, o, a):
        g, out, lse_row = vals
        place_d = ((_iota((ATT_W, LD_W), 0) >> HEAD_SHIFT) + 8 == _iota((ATT_W, LD_W), 1)).astype(F32)
        ld = jnp.where(_iota((1, LD_W), 1) < 8, lse_row, 0.0) + _dot_nn(g * out, place_d, HP)
        for k in range(n):
            o[k][...] = g.astype(o[k].dtype)
            o[n + k][...] = ld

    tiles = [(gmixed, ATT_W, _c0, False), (att, ATT_W, _c0, False), (lse, LD_W, _c0, False)]
    outs = ([(ATT_W, BF16, _c0, ATT_W) + ((d,) if d > 1 else ()) for d in DILATIONS]
            + [(LD_W, F32, _c0, LD_W) + ((d,) if d > 1 else ()) for d in DILATIONS])
    res = _rowwise("attn_bwd_prep", fn, S, 512, tiles, (), outs)
    return res[:n], res[n:]


def _sum3_bf16(views, S, width):
    def fn(i, vals, p, o, a):
        o[0][...] = (vals[0].astype(F32) + vals[1].astype(F32) + vals[2].astype(F32)).astype(BF16)

    tiles = [(t, width, _c0, _dil(d)) for t, d in zip(views, DILATIONS)]
    return _rowwise("sum3", fn, S, 512, tiles, (), [(width, BF16, _c0, width)])[0]


def _block_diag_mask():
    return ((_iota((REC_W, REC_W), 0) >> HEAD_SHIFT) == (_iota((REC_W, REC_W), 1) >> HEAD_SHIFT)).astype(F32)


def _rep_heads(t):
    return jnp.concatenate([t] * N_HEADS, axis=0)


def _hgrn_chunk(qr, z, iv, lb, st, reverse):
    C = REC_CHUNK
    r, c = _iota((C, C), 0), _iota((C, C), 1)
    t_cum = (c >= r) if reverse else (c <= r)
    mid_row, last_row = (C // 2, 0) if reverse else (C // 2 - 1, C - 1)
    f = lb + (1.0 - lb) * jax.nn.sigmoid(z)
    logf = jnp.log(jnp.maximum(f, F_TINY))
    k = (1.0 - lb) * jax.nn.sigmoid(-z)
    q = qr * jax.nn.sigmoid(qr)
    b = _dot_nn(t_cum.astype(F32), logf, HP)
    row = _iota((C, 1), 0)
    bm = jnp.sum(jnp.where(row == mid_row, b, 0.0), axis=0, keepdims=True)
    bl = jnp.sum(jnp.where(row == last_row, b, 0.0), axis=0, keepdims=True)
    qt = q * jnp.exp(jnp.minimum(b - bm, EXP_CLAMP))
    kt = k * jnp.exp(jnp.minimum(bm - b, EXP_CLAMP))
    qh = q * jnp.exp(b)
    kh = k * jnp.exp(bl - b)
    lam = jnp.exp(bl)
    bd = ((_iota((PAIR_W, PAIR_W), 0) >> HEAD_SHIFT) == (_iota((PAIR_W, PAIR_W), 1) >> HEAD_SHIFT)).astype(F32)
    s_in = _iota((C, PAIR_W), 1) & (HEAD - 1)
    t_in = _iota((C, PAIR_W), 0)
    tri = (s_in >= t_in) if reverse else (s_in <= t_in)
    twice = lambda t: jnp.concatenate([t, t], axis=0)
    outs, states = [], []
    for pr in range(N_PAIRS):
        sl = slice(pr * PAIR_W, (pr + 1) * PAIR_W)
        k_bd = twice(kt[:, sl]) * bd
        v_bd = (twice(iv[:, sl]) * bd).astype(BF16)
        st_bd = twice(st[:, sl]) * bd
        a = jnp.where(tri, _dot_nt(qt[:, sl], k_bd, lax.Precision.HIGH), 0.0)
        outs.append(_dot_nn(a.astype(BF16), v_bd) + _dot_nt(qh[:, sl].astype(BF16), st_bd.astype(BF16)))
        kv = _dot_tn(iv[:, sl].astype(BF16), kh[:, sl].astype(BF16))
        st_bd = st_bd * lam[:, sl] + kv * bd
        states.append(st_bd[0:HEAD] + st_bd[HEAD:PAIR_W])
    return jnp.concatenate(outs, axis=1), jnp.concatenate(states, axis=1)


REC_CHUNKS_PER_STEP = 8
REC_ROWS = REC_CHUNKS_PER_STEP * REC_CHUNK


def _hgrn_specs(order, blocks):
    return [pl.BlockSpec((REC_ROWS, REC_W), lambda i, b=b: (order(i), b)) for b in blocks]


def _chunk_rows(j):
    return pl.ds(pl.multiple_of(j * REC_CHUNK, REC_CHUNK), REC_CHUNK)


def _hgrn_fwd(proj, lb, z_blk, reverse):
    S = proj.shape[0]
    nb = S // REC_ROWS
    order = (lambda i: nb - 1 - i) if reverse else (lambda i: i)

    def body(q_ref, z_ref, v_ref, lb_ref, o_ref, st_ref, st_scr):
        @pl.when(pl.program_id(0) == 0)
        def _():
            st_scr[...] = jnp.zeros_like(st_scr)

        def step(t, carry):
            j = REC_CHUNKS_PER_STEP - 1 - t if reverse else t
            rows = _chunk_rows(j)
            st = st_scr[...]
            st_ref[j] = st
            o, st_new = _hgrn_chunk(q_ref[rows, :], z_ref[rows, :], v_ref[rows, :], lb_ref[...], st, reverse)
            o_ref[rows, :] = o
            st_scr[...] = st_new
            return carry

        lax.fori_loop(0, REC_CHUNKS_PER_STEP, step, 0)

    return pl.pallas_call(
        body, name="hgrn_rev_fwd" if reverse else "hgrn_fwd_fwd", grid=(nb,),
        in_specs=_hgrn_specs(order, (QR, z_blk, IR)) + [pl.BlockSpec((1, REC_W), lambda i: (0, 0))],
        out_specs=[pl.BlockSpec((REC_ROWS, REC_W), lambda i: (order(i), 0)),
                   pl.BlockSpec((REC_CHUNKS_PER_STEP, HEAD, REC_W), lambda i: (order(i), 0, 0))],
        out_shape=[jax.ShapeDtypeStruct((S, REC_W), F32), jax.ShapeDtypeStruct((S // REC_CHUNK, HEAD, REC_W), F32)],
        scratch_shapes=[pltpu.VMEM((HEAD, REC_W), F32)],
        compiler_params=_cparams(("arbitrary",)),
    )(proj, proj, proj, lb)


def _hgrn_bwd(proj, lb, states, go, z_blk, reverse):
    S = proj.shape[0]
    nb = S // REC_ROWS
    order = (lambda i: i) if reverse else (lambda i: nb - 1 - i)

    def body(q_ref, z_ref, v_ref, lb_ref, st_ref, go_ref, gq_ref, gz_ref, gv_ref, glb_ref, gst_scr):
        @pl.when(pl.program_id(0) == 0)
        def _():
            gst_scr[...] = jnp.zeros_like(gst_scr)
            glb_ref[...] = jnp.zeros_like(glb_ref)

        chunk = functools.partial(_hgrn_chunk, reverse=reverse)

        def step(t, carry):
            j = t if reverse else REC_CHUNKS_PER_STEP - 1 - t
            rows = _chunk_rows(j)
            _, vjp = jax.vjp(chunk, q_ref[rows, :], z_ref[rows, :], v_ref[rows, :], lb_ref[...], st_ref[j])
            gq, gz, gv, glb, gst = vjp((go_ref[rows, :], gst_scr[...]))
            gq_ref[rows, :] = gq
            gz_ref[rows, :] = gz
            gv_ref[rows, :] = gv
            glb_ref[...] += glb
            gst_scr[...] = gst
            return carry

        lax.fori_loop(0, REC_CHUNKS_PER_STEP, step, 0)

    row_spec = pl.BlockSpec((REC_ROWS, REC_W), lambda i: (order(i), 0))
    return pl.pallas_call(
        body, name="hgrn_rev_bwd" if reverse else "hgrn_fwd_bwd", grid=(nb,),
        in_specs=(_hgrn_specs(order, (QR, z_blk, IR)) + [pl.BlockSpec((1, REC_W), lambda i: (0, 0))]
                  + [pl.BlockSpec((REC_CHUNKS_PER_STEP, HEAD, REC_W), lambda i: (order(i), 0, 0)), row_spec]),
        out_specs=[row_spec] * 3 + [pl.BlockSpec((1, REC_W), lambda i: (0, 0))],
        out_shape=[jax.ShapeDtypeStruct((S, REC_W), F32)] * 3 + [jax.ShapeDtypeStruct((1, REC_W), F32)],
        scratch_shapes=[pltpu.VMEM((HEAD, REC_W), F32)],
        compiler_params=_cparams(("arbitrary",)),
    )(proj, proj, proj, lb, states, go)


def _hgrn_post_f(of, ob, gr, rnw):
    o = of + ob
    ms = _dot_nn(o * o, _block_diag_mask() * (1.0 / HEAD), HP)
    return o * lax.rsqrt(ms + EPS) * rnw * (gr * jax.nn.sigmoid(gr))


def _hgrn_post_fwd(of, ob, proj, rnw):
    S = of.shape[0]

    def fn(i, vals, p, o, a):
        o[0][...] = _hgrn_post_f(vals[0], vals[1], vals[2], p[0][...]).astype(BF16)

    tiles = [(of, REC_W, _c0, False), (ob, REC_W, _c0, False), (proj, REC_W, lambda j: GR, False)]
    return _rowwise("hgrn_post_fwd", fn, S, 512, tiles, _row_params(rnw), [(REC_W, BF16, _c0, REC_W)])[0]


def _hgrn_post_bwd(of, ob, proj, gmixed, rnw):
    S = of.shape[0]

    def fn(i, vals, p, o, a):
        _, vjp = jax.vjp(_hgrn_post_f, vals[0], vals[1], vals[2], p[0][...])
        go, _, ggr, grnw = vjp(vals[3])
        o[0][...] = go
        o[1][...] = ggr
        a[0][...] += grnw

    tiles = [(of, REC_W, _c0, False), (ob, REC_W, _c0, False), (proj, REC_W, lambda j: GR, False),
             (gmixed, REC_W, lambda j: 1, False)]
    return _rowwise("hgrn_post_bwd", fn, S, 256, tiles, _row_params(rnw),
                    [(REC_W, F32, _c0, REC_W), (REC_W, F32, _c0, REC_W)], [(1, REC_W, _c0, REC_W)])


def _lower_bounds_f(g0, g1):
    m = jnp.maximum(g0, g1)
    e0, e1 = jnp.exp(g0 - m), jnp.exp(g1 - m)
    return e1 / (e0 + e1)


def _adamw(name, w, m, v, gparts):
    R, C = w.shape
    P = gparts.shape[0]
    tr = R if R * C * 4 * (P + 7) * 2 <= VMEM_LIMIT_BYTES // 2 else _pick(R, (256, 128, 64, 32, 16, 8))

    def body(w_ref, m_ref, v_ref, gp_ref, g_ref, d_ref, nm_ref, nv_ref):
        g = gp_ref[0].astype(F32)
        for p in range(1, P):
            g = g + gp_ref[p].astype(F32)
        w_ = w_ref[...]
        nm = ADAM_B1 * m_ref[...] + (1.0 - ADAM_B1) * g
        nv = ADAM_B2 * v_ref[...] + (1.0 - ADAM_B2) * jnp.square(g)
        m_hat = nm / (1.0 - ADAM_B1 ** ADAM_STEP)
        v_hat = nv / (1.0 - ADAM_B2 ** ADAM_STEP)
        g_ref[...] = g
        d_ref[...] = -ADAM_LR * (m_hat / (jnp.sqrt(v_hat) + ADAM_EPS) + ADAM_WD * w_)
        nm_ref[...] = nm
        nv_ref[...] = nv

    spec = pl.BlockSpec((tr, C), lambda i: (i, 0))
    return pl.pallas_call(
        body, name=name, grid=(R // tr,),
        in_specs=[spec, spec, spec, pl.BlockSpec((P, tr, C), lambda i: (0, i, 0))],
        out_specs=[spec] * 4, out_shape=[jax.ShapeDtypeStruct((R, C), F32)] * 4,
        compiler_params=_cparams(("parallel",)),
    )(w, m, v, gparts)


def _place():
    return lax.axis_index("x"), lax.axis_index("y"), lax.axis_index("c")


def _index_of(p):
    return 4 * p[0] + 2 * p[1] + p[2]


def _allgather_small(name, rows):
    m_per, n = rows.shape

    def body(x_ref, out_ref, send_sems, recv_sems, local_sem):
        x, y, c = _place()
        me, sibling = (x, y, c), (x, y, 1 - c)
        chips = [(1 - x, y), (x, 1 - y), (1 - x, 1 - y)]

        def blk(p):
            return out_ref.at[pl.ds(_index_of(p) * m_per, m_per), :]

        def copy(k, block, to, src=None):
            return pltpu.make_async_remote_copy(
                src_ref=blk(block) if src is None else src, dst_ref=blk(block),
                send_sem=send_sems.at[k], recv_sem=recv_sems.at[k], device_id=to, device_id_type=MESH)

        mine = pltpu.make_async_copy(x_ref, blk(me), local_sem)
        mine.start()
        first = [copy(0, me, sibling, src=x_ref)]
        first += [copy(1 + j, me, (*chip, c), src=x_ref) for j, chip in enumerate(chips)]
        for cp in first:
            cp.start()
        passed = [copy(4 + j, (*chip, c), sibling) for j, chip in enumerate(chips)]
        for j, chip in enumerate(chips):
            copy(1 + j, (*chip, c), me).wait_recv()
            passed[j].start()
        copy(0, sibling, me).wait_recv()
        for j, chip in enumerate(chips):
            copy(4 + j, (*chip, 1 - c), me).wait_recv()
        for cp in first + passed:
            cp.wait_send()
        mine.wait()

    return pl.pallas_call(
        body, name=name,
        out_shape=jax.ShapeDtypeStruct((N_DEV * m_per, n), rows.dtype),
        in_specs=[pl.BlockSpec(memory_space=pltpu.VMEM)],
        out_specs=pl.BlockSpec(memory_space=pltpu.VMEM),
        scratch_shapes=[pltpu.SemaphoreType.DMA((7,)), pltpu.SemaphoreType.DMA((7,)), pltpu.SemaphoreType.DMA],
        compiler_params=_cparams(),
    )(rows)


def _allgather_big(name, arrs):
    na = len(arrs)

    def body(*refs):
        ins, outs = refs[:na], refs[na:2 * na]
        send_sems, recv_sems, local_sems = refs[2 * na:]
        x, y, c = _place()
        me, sibling = (x, y, c), (x, y, 1 - c)
        chips = [(1 - x, y), (x, 1 - y), (1 - x, 1 - y)]

        def copy(a, k, block, to, src=None):
            dst = outs[a].at[_index_of(block)]
            return pltpu.make_async_remote_copy(
                src_ref=dst if src is None else src, dst_ref=dst,
                send_sem=send_sems.at[a, k], recv_sem=recv_sems.at[a, k], device_id=to, device_id_type=MESH)

        mine = [pltpu.make_async_copy(ins[a], outs[a].at[_index_of(me)], local_sems.at[a]) for a in range(na)]
        for cp in mine:
            cp.start()
        sent = []
        for a in range(na):
            sent.append(copy(a, 0, me, sibling, src=ins[a]))
            sent += [copy(a, 1 + j, me, (*chip, c), src=ins[a]) for j, chip in enumerate(chips)]
        for cp in sent:
            cp.start()
        for j, chip in enumerate(chips):
            for a in range(na):
                copy(a, 1 + j, (*chip, c), me).wait_recv()
                fwd = copy(a, 4 + j, (*chip, c), sibling)
                fwd.start()
                sent.append(fwd)
        for a in range(na):
            copy(a, 0, sibling, me).wait_recv()
            for j, chip in enumerate(chips):
                copy(a, 4 + j, (*chip, 1 - c), me).wait_recv()
        for cp in sent:
            cp.wait_send()
        for cp in mine:
            cp.wait()

    any_spec = pl.BlockSpec(memory_space=pl.ANY)
    return pl.pallas_call(
        body, name=name,
        out_shape=[jax.ShapeDtypeStruct((N_DEV,) + a.shape, a.dtype) for a in arrs],
        in_specs=[any_spec] * na, out_specs=[any_spec] * na,
        scratch_shapes=[pltpu.SemaphoreType.DMA((na, 7)), pltpu.SemaphoreType.DMA((na, 7)), pltpu.SemaphoreType.DMA((na,))],
        compiler_params=_cparams(),
    )(*arrs)


def _scatter_parts(name, parts):
    na = len(parts)

    def body(*refs):
        ins, outs = refs[:na], refs[na:2 * na]
        send_sems, recv_sems, local_sems = refs[2 * na:]
        x, y, c = _place()
        me = _index_of((x, y, c))
        flips = [(k >> 2 & 1, k >> 1 & 1, k & 1) for k in range(1, N_DEV)]
        peers = [(1 - x if fx else x, 1 - y if fy else y, 1 - c if fc else c) for fx, fy, fc in flips]
        mine = [pltpu.make_async_copy(ins[a].at[me], outs[a].at[me], local_sems.at[a]) for a in range(na)]
        for cp in mine:
            cp.start()
        sent = []
        for a in range(na):
            for k, peer in enumerate(peers):
                sent.append(pltpu.make_async_remote_copy(
                    src_ref=ins[a].at[_index_of(peer)], dst_ref=outs[a].at[me],
                    send_sem=send_sems.at[a, k], recv_sem=recv_sems.at[a, k], device_id=peer, device_id_type=MESH))
        for cp in sent:
            cp.start()
        for a in range(na):
            for k, peer in enumerate(peers):
                slot = outs[a].at[_index_of(peer)]
                pltpu.make_async_remote_copy(
                    src_ref=slot, dst_ref=slot, send_sem=send_sems.at[a, k], recv_sem=recv_sems.at[a, k],
                    device_id=peer, device_id_type=MESH).wait_recv()
        for cp in sent:
            cp.wait_send()
        for cp in mine:
            cp.wait()

    any_spec = pl.BlockSpec(memory_space=pl.ANY)
    return pl.pallas_call(
        body, name=name,
        out_shape=[jax.ShapeDtypeStruct(p.shape, p.dtype) for p in parts],
        in_specs=[any_spec] * na, out_specs=[any_spec] * na,
        scratch_shapes=[pltpu.SemaphoreType.DMA((na, 7)), pltpu.SemaphoreType.DMA((na, 7)), pltpu.SemaphoreType.DMA((na,))],
        compiler_params=_cparams(),
    )(*parts)


N_CHIPS = 4


def _scatter_to_sibling(name, parts):
    na = len(parts)

    def body(*refs):
        ins, outs = refs[:na], refs[na:2 * na]
        send_sems, recv_sems = refs[2 * na:]
        x, y, c = _place()
        sibling = (x, y, 1 - c)
        sent = []
        for a in range(na):
            for q in range(N_CHIPS):
                sent.append(pltpu.make_async_remote_copy(
                    src_ref=ins[a].at[2 * q + (1 - c)], dst_ref=outs[a].at[q],
                    send_sem=send_sems.at[a, q], recv_sem=recv_sems.at[a, q], device_id=sibling, device_id_type=MESH))
        for cp in sent:
            cp.start()
        for cp in sent:
            cp.wait_recv()
        for cp in sent:
            cp.wait_send()

    any_spec = pl.BlockSpec(memory_space=pl.ANY)
    return pl.pallas_call(
        body, name=name,
        out_shape=[jax.ShapeDtypeStruct((N_CHIPS,) + p.shape[1:], p.dtype) for p in parts],
        in_specs=[any_spec] * na, out_specs=[any_spec] * na,
        scratch_shapes=[pltpu.SemaphoreType.DMA((na, N_CHIPS)), pltpu.SemaphoreType.DMA((na, N_CHIPS))],
        compiler_params=_cparams(),
    )(*parts)


def _pair_sum(name, parts, recv):
    _, R, C = parts.shape
    tr = _pick(R, (256, 128, 64, 32, 16))

    def body(p_ref, r_ref, o_ref):
        c = lax.axis_index("c")
        o_ref[...] = (p_ref[c].astype(F32) + r_ref[...].astype(F32)).astype(BF16)

    return pl.pallas_call(
        body, name=name, grid=(N_CHIPS, R // tr),
        in_specs=[pl.BlockSpec((None, 2, tr, C), lambda q, i: (q, 0, i, 0)), pl.BlockSpec((None, tr, C), lambda q, i: (q, i, 0))],
        out_specs=pl.BlockSpec((None, tr, C), lambda q, i: (q, i, 0)),
        out_shape=jax.ShapeDtypeStruct((N_CHIPS, R, C), BF16),
        compiler_params=_cparams(("parallel", "parallel")),
    )(parts.reshape(N_CHIPS, 2, R, C), recv)


def _scatter_to_chips(name, sums):
    na = len(sums)

    def body(*refs):
        ins, outs = refs[:na], refs[na:2 * na]
        send_sems, recv_sems, local_sems = refs[2 * na:]
        x, y, c = _place()
        me = 2 * x + y
        chips = [(1 - x, y), (x, 1 - y), (1 - x, 1 - y)]
        mine = [pltpu.make_async_copy(ins[a].at[me], outs[a].at[me], local_sems.at[a]) for a in range(na)]
        for cp in mine:
            cp.start()
        sent = []
        for a in range(na):
            for k, (qx, qy) in enumerate(chips):
                sent.append(pltpu.make_async_remote_copy(
                    src_ref=ins[a].at[2 * qx + qy], dst_ref=outs[a].at[me],
                    send_sem=send_sems.at[a, k], recv_sem=recv_sems.at[a, k], device_id=(qx, qy, c), device_id_type=MESH))
        for cp in sent:
            cp.start()
        for a in range(na):
            for k, (qx, qy) in enumerate(chips):
                slot = outs[a].at[2 * qx + qy]
                pltpu.make_async_remote_copy(
                    src_ref=slot, dst_ref=slot, send_sem=send_sems.at[a, k], recv_sem=recv_sems.at[a, k],
                    device_id=(qx, qy, c), device_id_type=MESH).wait_recv()
        for cp in sent:
            cp.wait_send()
        for cp in mine:
            cp.wait()

    any_spec = pl.BlockSpec(memory_space=pl.ANY)
    return pl.pallas_call(
        body, name=name,
        out_shape=[jax.ShapeDtypeStruct(p.shape, p.dtype) for p in sums],
        in_specs=[any_spec] * na, out_specs=[any_spec] * na,
        scratch_shapes=[pltpu.SemaphoreType.DMA((na, 3)), pltpu.SemaphoreType.DMA((na, 3)), pltpu.SemaphoreType.DMA((na,))],
        compiler_params=_cparams(),
    )(*sums)


def _gather_row(name, vec, width):
    n = vec.shape[0]
    rows = jnp.pad(vec, (0, width - n)).reshape(SUBLANES_F32, width // SUBLANES_F32)
    return _allgather_small(name, rows).reshape(N_DEV, width)[:, :n]


def _layer_fwd(x, mod, w):
    sh1, sc1, g1, sh2, sc2, g2 = [mod[i:i + 1] for i in range(N_MOD)]
    S = x.shape[0]
    h1 = _normmod_fwd(x, w["norm1_w"], sc1, sh1)
    proj = _matmul("proj_in", h1, w["w_in"], "nn")
    a1, a_out = _conv_a_fwd(proj, w["conv_a_w"], w["conv_a_b"], w["ln_a_w"], w["ln_a_b"])
    qkv = _qkv_views(proj)
    os, ls = zip(*[_attn_fwd(v, dil) for v, dil in zip(qkv, DILATIONS)])
    att, lse = _attn_merge(os, ls)
    of, st_f = _hgrn_fwd(proj, w["lb_f"], ZF, False)
    ob, st_b = _hgrn_fwd(proj, w["lb_b"], ZB, True)
    rec = _hgrn_post_fwd(of, ob, proj, w["rec_norm_w"])
    mixed = jnp.concatenate([att.astype(BF16), rec, a_out], axis=1)
    y1 = _matmul("proj_out", mixed, w["w_out"], "nn")
    x2 = _gate_add(x, y1, g1)
    h2 = _normmod_fwd(x2, w["norm2_w"], sc2, sh2)
    u = _matmul("ffn_up", h2, w["w_up"], "nn")
    act = _ffn_mid_fwd(u, w["conv_f_w"])
    y2 = _matmul("ffn_down", act, w["w_down"], "nn")
    x3 = _gate_add(x2, y2, g2)
    saved = dict(x=x, h1=h1, proj=proj, a1=a1, qkv=qkv, att=att, lse=lse, of=of, ob=ob, st_f=st_f, st_b=st_b,
                 mixed=mixed, y1=y1, x2=x2, h2=h2, u=u, act=act, y2=y2)
    return x3, saved


def _layer_bwd(gx3, mod, w, s):
    sh1, sc1, g1, sh2, sc2, g2 = [mod[i:i + 1] for i in range(N_MOD)]
    S = gx3.shape[0]
    g = {}
    gy2, gg2 = _gate_bwd(gx3, s["y2"], g2)
    gact = _matmul("ffn_down_dx", gy2, w["w_down"], "nt")
    g["w_down"] = _matmul("ffn_down_dw", s["act"], gy2, "tn")
    gu, g["conv_f_w"] = _ffn_mid_bwd(s["u"], gact, w["conv_f_w"])
    gh2 = _matmul("ffn_up_dx", gu, w["w_up"], "nt")
    g["w_up"] = _matmul("ffn_up_dw", s["h2"], gu, "tn")
    gx2, g["norm2_w"], gsc2, gsh2 = _normmod_bwd(s["x2"], gh2, gx3, w["norm2_w"], sc2, sh2)
    gy1, gg1 = _gate_bwd(gx2, s["y1"], g1)
    gmixed = _matmul("proj_out_dx", gy1, w["w_out"], "nt")
    g["w_out"] = _matmul("proj_out_dw", s["mixed"], gy1, "tn")
    go, ggr, g["rec_norm_w"] = _hgrn_post_bwd(s["of"], s["ob"], s["proj"], gmixed, w["rec_norm_w"])
    gq_f, gz_f, gv_f, g["lb_f"] = _hgrn_bwd(s["proj"], w["lb_f"], s["st_f"], go, ZF, False)
    gq_b, gz_b, gv_b, g["lb_b"] = _hgrn_bwd(s["proj"], w["lb_b"], s["st_b"], go, ZB, True)
    dos, lds = _attn_bwd_prep(gmixed, s["att"], s["lse"])
    gqkv = zip(*[_attn_bwd(v, do, ld, dil) for v, do, ld, dil in zip(s["qkv"], dos, lds, DILATIONS)])
    gq_a, gk_a, gv_a = [_sum3_bf16(lst, S, ATT_W) for lst in gqkv]
    gav, gag, gcw, g["conv_a_b"], g["ln_a_w"], g["ln_a_b"] = _conv_a_bwd(
        s["proj"], s["a1"], gmixed, w["conv_a_w"], w["ln_a_w"], w["ln_a_b"])
    g["conv_a_w"] = gcw[:CONV_W]
    gproj = jnp.concatenate([gq_a, gk_a, gv_a, (gq_f + gq_b).astype(BF16), gz_f.astype(BF16), gz_b.astype(BF16),
                             (gv_f + gv_b).astype(BF16), ggr.astype(BF16), gav, gag,
                             jnp.zeros((S, IN_COLS_PAD - IN_COLS), BF16)], axis=1)
    gh1 = _matmul("proj_in_dx", gproj, w["w_in"], "nt")
    g["w_in"] = _matmul("proj_in_dw", s["h1"], gproj, "tn")
    gx, g["norm1_w"], gsc1, gsh1 = _normmod_bwd(s["x"], gh1, gx2, w["norm1_w"], sc1, sh1)
    gmod = jnp.concatenate([gsh1, gsc1, gg1, gsh2, gsc2, gg2], axis=0)
    return gx, gmod, g


def _permute_in_cols(t):
    pad = jnp.zeros(t.shape[:-1] + (IN_COLS_PAD - IN_COLS,), t.dtype)
    return jnp.concatenate([t[..., CONV_COLS:], t[..., :CONV_COLS], pad], axis=-1)


def _unpermute_in_cols(t):
    return jnp.concatenate([t[..., IN_COLS - CONV_COLS:IN_COLS], t[..., :IN_COLS - CONV_COLS]], axis=-1)


def _cols_from_gathered(t, lead):
    nd = t.ndim
    perm = tuple(range(1, nd - 1)) + (0, nd - 1)
    t = t.transpose(perm)
    return t.reshape(t.shape[:-2] + (t.shape[-2] * t.shape[-1],))


def _cols_to_parts(t):
    L, R, C = t.shape
    return t.reshape(L * R, N_DEV, C // N_DEV).transpose(1, 0, 2)


SMALL_REPL = (("norm1_w", D), ("conv_a_b", CONV_CH), ("ln_a_w", CONV_CH), ("ln_a_b", CONV_CH),
              ("rec_norm_w", REC_W), ("norm2_w", D))


def kernel(x, c, w_ada, b_ada, norm1_w, w_in, conv_a_w, conv_a_b, ln_a_w, ln_a_b, lb_gamma, rec_norm_w, w_out, norm2_w, w_up, conv_f_w, w_down, final_norm_w, loss_target, m_w_ada, m_b_ada, m_norm1_w, m_w_in, m_conv_a_w, m_conv_a_b, m_ln_a_w, m_ln_a_b, m_lb_gamma, m_rec_norm_w, m_w_out, m_norm2_w, m_w_up, m_conv_f_w, m_w_down, m_final_norm_w, v_w_ada, v_b_ada, v_norm1_w, v_w_in, v_conv_a_w, v_conv_a_b, v_ln_a_w, v_ln_a_b, v_lb_gamma, v_rec_norm_w, v_w_out, v_norm2_w, v_w_up, v_conv_f_w, v_w_down, v_final_norm_w):
    px, py, pc = _place()
    me = _index_of((px, py, pc))
    xs, tgt = x[0], loss_target[0]
    S = xs.shape[0]
    ada_cols = w_ada.shape[2]

    big = [w_in.reshape(DEPTH * D, -1), w_up.reshape(DEPTH * D, -1), w_out.reshape(-1, D), w_down.reshape(-1, D)]
    g_in, g_up, g_out, g_down = _allgather_big("gather_weights", [t.astype(BF16) for t in big])
    w_in_f = _permute_in_cols(_cols_from_gathered(g_in.reshape(N_DEV, DEPTH, D, -1), 1))
    w_up_f = _interleave_ffn_cols(_cols_from_gathered(g_up.reshape(N_DEV, DEPTH, D, -1), 1))
    w_out_f = g_out.reshape(N_DEV, DEPTH, D // N_DEV, D).transpose(1, 0, 2, 3).reshape(DEPTH, D, D)
    w_out_f = jnp.concatenate([w_out_f[:, CONV_CH:], w_out_f[:, :CONV_CH]], axis=1)
    w_down_f = g_down.reshape(N_DEV, DEPTH, D_FF // N_DEV, D).transpose(1, 0, 2, 3).reshape(DEPTH, D_FF, D)

    small_in = jnp.concatenate([c.reshape(-1), conv_a_w.reshape(-1), lb_gamma.reshape(-1), conv_f_w.reshape(-1)])
    gs = _gather_row("gather_small", small_in, 8192)
    o1 = D
    o2 = o1 + conv_a_w.size
    o3 = o2 + lb_gamma.size
    c_all = gs[:, :o1]
    conv_a_f = _cols_from_gathered(gs[:, o1:o2].reshape(N_DEV, DEPTH, CONV_W, -1), 1)
    lb_gamma_f = _cols_from_gathered(gs[:, o2:o3].reshape(N_DEV, DEPTH, 2, -1), 1)
    conv_f_f = _interleave_ffn_cols(_cols_from_gathered(gs[:, o3:].reshape(N_DEV, DEPTH, FFN_CONV_W, -1), 1))
    conv_a_pad = jnp.pad(conv_a_f, ((0, 0), (0, CONV_W_PAD - CONV_W), (0, 0)))

    b_loc = lax.dynamic_slice_in_dim(b_ada, me * ada_cols, ada_cols, axis=1)

    def mod_fn(c_all_, w_, b_):
        cond = c_all_ * jax.nn.sigmoid(c_all_)
        return (jnp.concatenate([_dot_nn(cond, w_[l], HP) + b_[l] for l in range(DEPTH)], axis=1),)

    (mod_part,) = _vmem_call("ada_mod", mod_fn, [c_all, w_ada, b_loc[:, None, :]], [((N_DEV, DEPTH * ada_cols), F32)])
    gm = _allgather_small("gather_mod", mod_part).reshape(N_DEV, N_DEV, DEPTH, ada_cols)
    mod = lax.dynamic_index_in_dim(gm, me, axis=1, keepdims=False)
    mod = mod.transpose(1, 0, 2).reshape(DEPTH, N_MOD, D)

    (lb1,) = _vmem_call("lower_bounds", lambda a, b: (_lower_bounds_f(a, b),), [lb_gamma_f[0], lb_gamma_f[1]], [((2, REC_W), F32)])
    lb4 = jnp.concatenate([jnp.zeros_like(lb1), lb1], axis=0)

    def layer_weights(l):
        row = lambda t: t[l].reshape(1, -1)
        return dict(norm1_w=row(norm1_w), w_in=w_in_f[l], conv_a_w=conv_a_pad[l], conv_a_b=row(conv_a_b), ln_a_w=row(ln_a_w),
                    ln_a_b=row(ln_a_b), lb_f=lb4[2 * l:2 * l + 1], lb_b=lb4[2 * l + 1:2 * l + 2], rec_norm_w=row(rec_norm_w),
                    w_out=w_out_f[l], norm2_w=row(norm2_w), w_up=w_up_f[l], conv_f_w=conv_f_f[l], w_down=w_down_f[l])

    ws = [layer_weights(l) for l in range(DEPTH)]
    h, saved = xs, []
    for l in range(DEPTH):
        h, s = _layer_fwd(h, mod[l], ws[l])
        saved.append(s)
    gh, g_final, loss_row = _loss_head(h, tgt, final_norm_w.reshape(1, D))
    loss = lax.psum(loss_row[0, 0], ("x", "y", "c"))
    gmods, gws = [None] * DEPTH, [None] * DEPTH
    for l in reversed(range(DEPTH)):
        gh, gmods[l], gws[l] = _layer_bwd(gh, mod[l], ws[l], saved[l])
    grad_x = gh[None]

    glb1 = jnp.concatenate([gws[1]["lb_f"], gws[1]["lb_b"]], axis=0)

    def lb_bwd_fn(a, b, g1):
        _, vjp = jax.vjp(_lower_bounds_f, a, b)
        return vjp(g1)

    g_lb_gamma = jnp.stack(_vmem_call("lower_bounds_bwd", lb_bwd_fn, [lb_gamma_f[0], lb_gamma_f[1], glb1], [((2, REC_W), F32)] * 2))
    pieces = [jnp.stack(gmods).reshape(-1)]
    for l in range(DEPTH):
        pieces += [gws[l][n].reshape(-1) for n, _ in SMALL_REPL]
    pieces += [g_final.reshape(-1)]
    pieces += [jnp.stack([gws[l]["conv_a_w"] for l in range(DEPTH)]).reshape(-1), g_lb_gamma.reshape(-1),
               _deinterleave_ffn_cols(jnp.stack([gws[l]["conv_f_w"] for l in range(DEPTH)])).reshape(-1)]
    small_g = jnp.concatenate(pieces)
    n_small = small_g.shape[0]
    gsm = _gather_row("gather_small_grads", small_g, 71680)
    n_mod = DEPTH * N_MOD * D
    gmod_all = gsm[:, :n_mod].reshape(N_DEV, DEPTH, N_MOD * D)
    gmod_loc = lax.dynamic_slice_in_dim(gmod_all, me * ada_cols, ada_cols, axis=2).transpose(1, 0, 2)

    def small_fn(gsm_, c_all_, gm_):
        cond = c_all_ * jax.nn.sigmoid(c_all_)
        gw = jnp.concatenate([_dot_tn(cond, gm_[l], HP) for l in range(DEPTH)], axis=0)
        return jnp.sum(gsm_, axis=0, keepdims=True), gw

    tot, g_w_ada = _vmem_call("small_grads", small_fn, [gsm, c_all, gmod_loc],
                              [((1, n_small), F32), ((DEPTH * D, ada_cols), F32)])
    tot = tot[0]
    grads = {"w_ada": g_w_ada.reshape(DEPTH, D, ada_cols), "b_ada": tot[:n_mod].reshape(DEPTH, N_MOD * D)}
    pos = n_mod
    per_layer = {n: [] for n, _ in SMALL_REPL}
    for l in range(DEPTH):
        for n, width in SMALL_REPL:
            per_layer[n].append(tot[pos:pos + width])
            pos += width
    for n, _ in SMALL_REPL:
        grads[n] = jnp.stack(per_layer[n])
    grads["final_norm_w"] = tot[pos:pos + D]
    pos += D
    n_ca, n_lb, n_cf = DEPTH * CONV_W * CONV_CH, DEPTH * 2 * REC_W, DEPTH * FFN_CONV_W * 2 * D_FF
    g_ca = tot[pos:pos + n_ca].reshape(DEPTH, CONV_W, CONV_CH)
    g_lb = tot[pos + n_ca:pos + n_ca + n_lb].reshape(DEPTH, 2, REC_W)
    g_cf = tot[pos + n_ca + n_lb:pos + n_ca + n_lb + n_cf].reshape(DEPTH, FFN_CONV_W, 2 * D_FF)
    grads["conv_a_w"] = lax.dynamic_slice_in_dim(g_ca, me * conv_a_w.shape[2], conv_a_w.shape[2], axis=2)
    grads["lb_gamma"] = lax.dynamic_slice_in_dim(g_lb, me * lb_gamma.shape[2], lb_gamma.shape[2], axis=2)
    grads["conv_f_w"] = lax.dynamic_slice_in_dim(g_cf, me * conv_f_w.shape[2], conv_f_w.shape[2], axis=2)

    gw_in = _unpermute_in_cols(jnp.stack([gws[l]["w_in"] for l in range(DEPTH)]))
    gw_up = _deinterleave_ffn_cols(jnp.stack([gws[l]["w_up"] for l in range(DEPTH)]))
    gw_out = jnp.stack([gws[l]["w_out"] for l in range(DEPTH)])
    gw_out = jnp.concatenate([gw_out[:, D - CONV_CH:], gw_out[:, :D - CONV_CH]], axis=1)
    gw_down = jnp.stack([gws[l]["w_down"] for l in range(DEPTH)])
    rows_to_parts = lambda t: t.reshape(DEPTH, N_DEV, -1, D).transpose(1, 0, 2, 3).reshape(N_DEV, -1, D)
    parts = [_cols_to_parts(gw_in), _cols_to_parts(gw_up), rows_to_parts(gw_out), rows_to_parts(gw_down)]
    parts = [t.astype(BF16) for t in parts]
    from_sibling = _scatter_to_sibling("scatter_sibling", parts)
    sums = [_pair_sum("pair_sum", p, r) for p, r in zip(parts, from_sibling)]
    r_in, r_up, r_out, r_down = _scatter_to_chips("scatter_chips", sums)

    given = dict(w_ada=(w_ada, m_w_ada, v_w_ada), b_ada=(b_ada, m_b_ada, v_b_ada), norm1_w=(norm1_w, m_norm1_w, v_norm1_w),
                 w_in=(w_in, m_w_in, v_w_in), conv_a_w=(conv_a_w, m_conv_a_w, v_conv_a_w), conv_a_b=(conv_a_b, m_conv_a_b, v_conv_a_b),
                 ln_a_w=(ln_a_w, m_ln_a_w, v_ln_a_w), ln_a_b=(ln_a_b, m_ln_a_b, v_ln_a_b), lb_gamma=(lb_gamma, m_lb_gamma, v_lb_gamma),
                 rec_norm_w=(rec_norm_w, m_rec_norm_w, v_rec_norm_w), w_out=(w_out, m_w_out, v_w_out),
                 norm2_w=(norm2_w, m_norm2_w, v_norm2_w), w_up=(w_up, m_w_up, v_w_up), conv_f_w=(conv_f_w, m_conv_f_w, v_conv_f_w),
                 w_down=(w_down, m_w_down, v_w_down), final_norm_w=(final_norm_w, m_final_norm_w, v_final_norm_w))
    big_parts = dict(w_in=r_in, w_up=r_up, w_out=r_out, w_down=r_down)
    names = list(given)
    res = {}
    for n in names:
        w_, m_, v_ = given[n]
        shape = w_.shape
        C = shape[-1]
        two_d = lambda t: t.reshape(-1, C)
        gp = big_parts[n] if n in big_parts else two_d(grads[n])[None]
        res[n] = [t.reshape(shape) for t in _adamw("adamw_" + n, two_d(w_), two_d(m_), two_d(v_), gp)]
    return (loss, grad_x, *[res[n][0] for n in names], *[res[n][1] for n in names],
            *[res[n][2] for n in names], *[res[n][3] for n in names])
```

```python
import functools

import numpy as np
import jax
import jax.numpy as jnp
from jax import lax
from jax.experimental import pallas as pl
from jax.experimental.pallas import tpu as pltpu

F32 = jnp.float32
BF16 = jnp.bfloat16
HP = lax.Precision.HIGHEST
MESH = pl.DeviceIdType.MESH

N_DEV = 8
D = 1024
DEPTH = 2
CONV_CH = 256
CONV_W = 31
CONV_W_PAD = 32
ATT_W = 384
REC_W = 384
N_HEADS = 6
HEAD = 64
HEAD_SHIFT = 6
HALF_BAND = 64
ATT_BLK = 128
DILATIONS = (1, 4, 16)
ALIBI_SLOPES = tuple(float(2.0 ** (-8.0 * (h + 1) / N_HEADS)) for h in range(N_HEADS))
MASK_VALUE = -1e30
REC_CHUNK = 64
EXP_CLAMP = 80.0
F_TINY = 1e-30
IN_COLS = 3584
IN_COLS_PAD = IN_COLS
QKV_BLOCKS = 3
D_FF = 2816
FFN_CONV_W = 3
N_MOD = 6
EPS = 1e-6
ADAM_LR, ADAM_B1, ADAM_B2, ADAM_EPS, ADAM_WD, ADAM_STEP = 0.001, 0.9, 0.999, 1e-08, 0.01, 10

VMEM_LIMIT_BYTES = 56 * 1024 * 1024
SUBLANES_F32 = 8
LANES = 128

QA, KA, VA, QR, ZF, ZB, IR, GR = range(8)
AV_BLK, AG_BLK = 12, 13
CONV_COLS = 2 * CONV_CH


def _cparams(sem=None):
    kw = dict(vmem_limit_bytes=VMEM_LIMIT_BYTES)
    if sem is not None:
        kw["dimension_semantics"] = sem
    return pltpu.CompilerParams(**kw)


def _iota(shape, dim):
    return lax.broadcasted_iota(jnp.int32, shape, dim)


def _dot(a, b, dims, precision=None):
    return lax.dot_general(a, b, (dims, ((), ())), precision=precision, preferred_element_type=F32)


def _dot_nn(a, b, precision=None):
    return _dot(a, b, ((1,), (0,)), precision)


def _dot_nt(a, b, precision=None):
    return _dot(a, b, ((1,), (1,)), precision)


def _dot_tn(a, b, precision=None):
    return _dot(a, b, ((0,), (0,)), precision)


def _c0(j):
    return 0


def _pick(n, cands):
    for c in cands:
        if n % c == 0:
            return c
    return n


MATMUL_OUT_TILE_BYTES = 8 * 1024 * 1024


def _div_lanes(n, cap):
    best = None
    for d in range(LANES, min(n, cap) + 1, LANES):
        if n % d == 0:
            best = d
    return best if best is not None else n


def _matmul_tiles(mode, M, N, K):
    if mode == "nn":
        tm = _pick(M, (1024, 512, 256, 128))
        return tm, _div_lanes(N, MATMUL_OUT_TILE_BYTES // (4 * tm)), K
    if mode == "nt":
        return _pick(M, (512, 256, 128)), N, K
    tm = _div_lanes(M, 1408)
    return tm, _div_lanes(N, MATMUL_OUT_TILE_BYTES // (4 * tm)), _pick(K, (1024, 512, 256))


def _matmul(name, a, b, mode, out_dtype=F32):
    if mode == "nn":
        (M, K), (_, N) = a.shape, b.shape
    elif mode == "nt":
        (M, K), (N, _) = a.shape, b.shape
    else:
        (K, M), (_, N) = a.shape, b.shape
    tm, tn, tk = _matmul_tiles(mode, M, N, K)
    nk = K // tk
    if mode == "nn":
        a_spec = pl.BlockSpec((tm, tk), lambda i, j, k: (i, k))
        b_spec = pl.BlockSpec((tk, tn), lambda i, j, k: (k, j))
        dims = ((1,), (0,))
    elif mode == "nt":
        a_spec = pl.BlockSpec((tm, tk), lambda i, j, k: (i, k))
        b_spec = pl.BlockSpec((tn, tk), lambda i, j, k: (j, k))
        dims = ((1,), (1,))
    else:
        a_spec = pl.BlockSpec((tk, tm), lambda i, j, k: (k, i))
        b_spec = pl.BlockSpec((tk, tn), lambda i, j, k: (k, j))
        dims = ((0,), (0,))

    def body_whole(a_ref, b_ref, o_ref):
        o_ref[...] = _dot(a_ref[...].astype(BF16), b_ref[...].astype(BF16), dims).astype(o_ref.dtype)

    def body(a_ref, b_ref, o_ref, acc_ref):
        k = pl.program_id(2)
        part = _dot(a_ref[...].astype(BF16), b_ref[...].astype(BF16), dims)

        @pl.when(k == 0)
        def _():
            acc_ref[...] = part

        @pl.when(k > 0)
        def _():
            acc_ref[...] += part

        @pl.when(k == nk - 1)
        def _():
            o_ref[...] = acc_ref[...].astype(o_ref.dtype)

    return pl.pallas_call(
        body_whole if nk == 1 else body, name=name, grid=(M // tm, N // tn, nk),
        in_specs=[a_spec, b_spec],
        out_specs=pl.BlockSpec((tm, tn), lambda i, j, k: (i, j)),
        out_shape=jax.ShapeDtypeStruct((M, N), out_dtype),
        scratch_shapes=[] if nk == 1 else [pltpu.VMEM((tm, tn), F32)],
        compiler_params=_cparams(("parallel", "parallel", "arbitrary")),
    )(a, b)


def _rowwise(name, fn, S, ts, tiles, params=(), outs=(), accs=(), halo=0, ncb=1):
    in_specs, args, scratch = [], [], []
    for arr, w, jm, with_halo in tiles:
        if isinstance(with_halo, int) and with_halo > 1:
            d = with_halo
            in_specs.append(pl.BlockSpec((ts // d, d * w), lambda j, i: (i, 0)))
            args.append(arr)
            scratch.append(pltpu.VMEM((w // LANES, ts, LANES), F32))
        elif with_halo:
            hb, nhb = ts // halo, S // halo
            in_specs += [
                pl.BlockSpec((halo, w), lambda j, i, jm=jm, hb=hb: (jnp.maximum(i * hb - 1, 0), jm(j))),
                pl.BlockSpec((ts, w), lambda j, i, jm=jm: (i, jm(j))),
                pl.BlockSpec((halo, w), lambda j, i, jm=jm, hb=hb, nhb=nhb: (jnp.minimum((i + 1) * hb, nhb - 1), jm(j))),
            ]
            args += [arr, arr, arr]
        else:
            in_specs.append(pl.BlockSpec((ts, w), lambda j, i, jm=jm: (i, jm(j))))
            args.append(arr)
    for arr, r, w, jm in params:
        in_specs.append(pl.BlockSpec((r, w), lambda j, i, jm=jm: (0, jm(j))))
        args.append(arr)
    out_specs, out_shape = [], []
    for w, dt, jm, tw, *dil in outs:
        if dil:
            out_specs.append(pl.BlockSpec((ts // dil[0], dil[0] * w), lambda j, i: (i, 0)))
            out_shape.append(jax.ShapeDtypeStruct((S // dil[0], dil[0] * w), dt))
            scratch += [pltpu.VMEM((ts, w), F32), pltpu.VMEM((w // LANES, ts, LANES), F32)]
        else:
            out_specs.append(pl.BlockSpec((ts, w), lambda j, i, jm=jm: (i, jm(j))))
            out_shape.append(jax.ShapeDtypeStruct((S, tw), dt))
    for r, w, jm, tw in accs:
        out_specs.append(pl.BlockSpec((r, w), lambda j, i, jm=jm: (0, jm(j))))
        out_shape.append(jax.ShapeDtypeStruct((r, tw), F32))
    n_tiles, n_params, n_outs, n_accs = len(tiles), len(params), len(outs), len(accs)

    def residue_rows(r, d):
        return pl.ds(r, ts // d, stride=d)

    def body(*refs):
        i = pl.program_id(1)
        n_io = len(in_specs) + n_outs + n_accs
        scr = list(refs[n_io:])
        refs = refs[:n_io]
        pos, vals = 0, []
        for _, w, _, with_halo in tiles:
            if isinstance(with_halo, int) and with_halo > 1:
                d, buf = with_halo, scr.pop(0)
                for r in range(d):
                    for c in range(w // LANES):
                        buf[c, residue_rows(r, d), :] = refs[pos][:, r * w + c * LANES:r * w + (c + 1) * LANES].astype(F32)
                vals.append(jnp.concatenate([buf[c] for c in range(w // LANES)], axis=1))
                pos += 1
            elif with_halo:
                before, after = refs[pos][...], refs[pos + 2][...]
                before = jnp.where(i > 0, before, jnp.zeros_like(before))
                after = jnp.where(i < S // ts - 1, after, jnp.zeros_like(after))
                vals.append(jnp.concatenate([before, refs[pos + 1][...], after], axis=0))
                pos += 3
            else:
                vals.append(refs[pos][...])
                pos += 1
        prefs = refs[pos:pos + n_params]
        orefs = list(refs[pos + n_params:pos + n_params + n_outs])
        arefs = refs[pos + n_params + n_outs:]
        staged = []
        for k, (w, _, _, _, *dil) in enumerate(outs):
            if dil:
                staged.append((orefs[k], scr.pop(0), scr.pop(0), w, dil[0]))
                orefs[k] = staged[-1][1]

        @pl.when(i == 0)
        def _():
            for r in arefs:
                r[...] = jnp.zeros_like(r)

        fn(i, vals, prefs, orefs, arefs)
        for out_ref, flat, buf, w, d in staged:
            for c in range(w // LANES):
                buf[c] = flat[:, c * LANES:(c + 1) * LANES]
                for r in range(d):
                    out_ref[:, r * w + c * LANES:r * w + (c + 1) * LANES] = buf[c, residue_rows(r, d), :].astype(out_ref.dtype)

    res = pl.pallas_call(
        body, name=name, grid=(ncb, S // ts),
        in_specs=in_specs, out_specs=out_specs, out_shape=out_shape, scratch_shapes=scratch,
        compiler_params=_cparams(("arbitrary", "arbitrary")),
    )(*args)
    return res


def _vmem_call(name, fn, ins, out_shapes):
    n_in = len(ins)

    def body(*refs):
        vals = fn(*[r[...] for r in refs[:n_in]])
        for r, v in zip(refs[n_in:], vals):
            r[...] = v.astype(r.dtype)

    return pl.pallas_call(
        body, name=name,
        out_shape=[jax.ShapeDtypeStruct(s, dt) for s, dt in out_shapes],
        compiler_params=_cparams(),
    )(*ins)


def _rms(x, w):
    return x * lax.rsqrt(jnp.mean(x * x, axis=-1, keepdims=True) + EPS) * w


def _normmod_f(x, nw, sc, sh):
    return _rms(x, nw) * (1.0 + sc) + sh


def _row_params(*vecs):
    return [(v, 1, v.shape[1], _c0) for v in vecs]


def _normmod_fwd(x, nw, sc, sh):
    S = x.shape[0]

    def fn(i, vals, p, o, a):
        o[0][...] = _normmod_f(vals[0], p[0][...], p[1][...], p[2][...]).astype(BF16)

    return _rowwise("normmod_fwd", fn, S, 512, [(x, D, _c0, False)], _row_params(nw, sc, sh), [(D, BF16, _c0, D)])[0]


def _normmod_bwd(x, gh, gres, nw, sc, sh):
    S = x.shape[0]

    def fn(i, vals, p, o, a):
        _, vjp = jax.vjp(_normmod_f, vals[0], p[0][...], p[1][...], p[2][...])
        gx, gnw, gsc, gsh = vjp(vals[1])
        o[0][...] = gx + vals[2]
        a[0][...] += gnw
        a[1][...] += gsc
        a[2][...] += gsh

    return _rowwise("normmod_bwd", fn, S, 256, [(x, D, _c0, False), (gh, D, _c0, False), (gres, D, _c0, False)],
                    _row_params(nw, sc, sh), [(D, F32, _c0, D)], [(1, D, _c0, D)] * 3)


def _gate_add(x, y, g):
    S = x.shape[0]

    def fn(i, vals, p, o, a):
        o[0][...] = vals[0] + p[0][...] * vals[1]

    return _rowwise("gate_add", fn, S, 512, [(x, D, _c0, False), (y, D, _c0, False)], _row_params(g), [(D, F32, _c0, D)])[0]


def _gate_bwd(gx, y, g):
    S = gx.shape[0]

    def fn(i, vals, p, o, a):
        o[0][...] = (vals[0] * p[0][...]).astype(BF16)
        a[0][...] += jnp.sum(vals[0] * vals[1], axis=0, keepdims=True)

    return _rowwise("gate_bwd", fn, S, 512, [(gx, D, _c0, False), (y, D, _c0, False)], _row_params(g),
                    [(D, BF16, _c0, D)], [(1, D, _c0, D)])


def _loss_head(x, tgt, fw):
    S = x.shape[0]

    def fn(i, vals, p, o, a):
        y, vjp = jax.vjp(_rms, vals[0], p[0][...])
        err = y - vals[1]
        gx, gfw = vjp(err * (1.0 / D))
        o[0][...] = gx
        a[0][...] += gfw
        part = 0.5 * jnp.sum(jnp.mean(err * err, axis=-1, keepdims=True), axis=0, keepdims=True)
        a[1][...] += jnp.broadcast_to(part, (1, LANES))

    return _rowwise("loss_head", fn, S, 256, [(x, D, _c0, False), (tgt, D, _c0, False)], _row_params(fw),
                    [(D, F32, _c0, D)], [(1, D, _c0, D), (1, LANES, _c0, LANES)])


CONV_HALO = 16
CONV_TS = 512


def _shifted(ext, shift, ts, halo):
    n = ext.shape[0]
    s = shift % n
    r = ext if s == 0 else pltpu.roll(ext, s, 0)
    return r[halo:halo + ts]


def _ln_silu(a, w, b):
    mu = jnp.mean(a, axis=-1, keepdims=True)
    var = jnp.mean(jnp.square(a - mu), axis=-1, keepdims=True)
    y = (a - mu) * lax.rsqrt(var + EPS) * w + b
    return y * jax.nn.sigmoid(y)


def _conv_a_fwd(proj, w_pad, b, lnw, lnb):
    S = proj.shape[0]
    ts, H = min(CONV_TS, S), CONV_HALO

    def fn(i, vals, p, o, a):
        a0 = vals[0] * jax.nn.sigmoid(vals[1])
        acc = jnp.zeros((ts, CONV_CH), F32) + p[1][...]
        for k in range(CONV_W):
            acc = acc + _shifted(a0, CONV_W // 2 - k, ts, H) * p[0][pl.ds(k, 1), :]
        o[0][...] = acc
        o[1][...] = _ln_silu(acc, p[2][...], p[3][...]).astype(BF16)

    tiles = [(proj, CONV_CH, lambda j: AV_BLK, True), (proj, CONV_CH, lambda j: AG_BLK, True)]
    params = [(w_pad, CONV_W_PAD, CONV_CH, _c0)] + _row_params(b, lnw, lnb)
    return _rowwise("conv_a_fwd", fn, S, ts, tiles, params, [(CONV_CH, F32, _c0, CONV_CH), (CONV_CH, BF16, _c0, CONV_CH)], halo=H)


def _conv_a_bwd(proj, a1, gmixed, w_pad, lnw, lnb):
    S = proj.shape[0]
    ts, H = min(CONV_TS, S), CONV_HALO

    def fn(i, vals, p, o, a):
        av, ag, a1e, ge = vals
        lw, lb = p[1][...], p[2][...]
        _, vjp_e = jax.vjp(lambda t: _ln_silu(t, lw, lb), a1e)
        (ga1e,) = vjp_e(ge)
        c = slice(H, H + ts)
        _, vjp_c = jax.vjp(_ln_silu, a1e[c], lw, lb)
        ga1, glw, glb = vjp_c(ge[c])
        a[1][...] += jnp.sum(ga1, axis=0, keepdims=True)
        a[2][...] += glw
        a[3][...] += glb
        sg = jax.nn.sigmoid(ag)
        a0 = av * sg
        ga0 = jnp.zeros((ts, CONV_CH), F32)
        for k in range(CONV_W):
            a[0][pl.ds(k, 1), :] += jnp.sum(ga1 * _shifted(a0, CONV_W // 2 - k, ts, H), axis=0, keepdims=True)
            ga0 = ga0 + _shifted(ga1e, k - CONV_W // 2, ts, H) * p[0][pl.ds(k, 1), :]
        sgc, avc = sg[c], av[c]
        o[0][...] = (ga0 * sgc).astype(BF16)
        o[1][...] = (ga0 * avc * sgc * (1.0 - sgc)).astype(BF16)

    tiles = [(proj, CONV_CH, lambda j: AV_BLK, True), (proj, CONV_CH, lambda j: AG_BLK, True),
             (a1, CONV_CH, _c0, True), (gmixed, CONV_CH, lambda j: 3, True)]
    params = [(w_pad, CONV_W_PAD, CONV_CH, _c0)] + _row_params(lnw, lnb)
    outs = [(CONV_CH, BF16, _c0, CONV_CH), (CONV_CH, BF16, _c0, CONV_CH)]
    accs = [(CONV_W_PAD, CONV_CH, _c0, CONV_CH)] + [(1, CONV_CH, _c0, CONV_CH)] * 3
    return _rowwise("conv_a_bwd", fn, S, ts, tiles, params, outs, accs, halo=H)


FFN_HALO = 8
FFN_TS = 512
FFN_CB = 256
FFN_UB = 2 * FFN_CB
FFN_NCB = D_FF // FFN_CB


def _interleave_ffn_cols(t):
    blocks = []
    for j in range(FFN_NCB):
        blocks += [t[..., j * FFN_CB:(j + 1) * FFN_CB], t[..., D_FF + j * FFN_CB:D_FF + (j + 1) * FFN_CB]]
    return jnp.concatenate(blocks, axis=-1)


def _deinterleave_ffn_cols(t):
    gate = [t[..., j * FFN_UB:j * FFN_UB + FFN_CB] for j in range(FFN_NCB)]
    val = [t[..., j * FFN_UB + FFN_CB:(j + 1) * FFN_UB] for j in range(FFN_NCB)]
    return jnp.concatenate(gate + val, axis=-1)


def _gelu_mul(g, v):
    return 0.5 * g * (1.0 + lax.erf(g * (2.0 ** -0.5))) * v


def _ffn_mid_fwd(u, cw):
    S = u.shape[0]
    ts, H = min(FFN_TS, S), FFN_HALO

    def fn(i, vals, p, o, a):
        c = jnp.zeros((ts, FFN_UB), F32)
        for k in range(FFN_CONV_W):
            c = c + _shifted(vals[0], 1 - k, ts, H) * p[0][pl.ds(k, 1), :]
        o[0][...] = _gelu_mul(c[:, :FFN_CB], c[:, FFN_CB:]).astype(BF16)

    ident = lambda j: j
    return _rowwise("ffn_mid_fwd", fn, S, ts, [(u, FFN_UB, ident, True)], [(cw, FFN_CONV_W, FFN_UB, ident)],
                    [(FFN_CB, BF16, ident, D_FF)], halo=H, ncb=FFN_NCB)[0]


def _ffn_mid_bwd(u, gact, cw):
    S = u.shape[0]
    ts, H = min(FFN_TS, S), FFN_HALO
    n = ts + 2 * H

    def fn(i, vals, p, o, a):
        ug, uv, ga = vals

        def conv_all(ue, w_ref):
            acc = jnp.zeros((n, FFN_CB), F32)
            for k in range(FFN_CONV_W):
                s = (1 - k) % n
                acc = acc + (ue if s == 0 else pltpu.roll(ue, s, 0)) * w_ref[pl.ds(k, 1), :]
            return acc

        _, vjp = jax.vjp(_gelu_mul, conv_all(ug, p[0]), conv_all(uv, p[1]))
        for half, (gc, ue) in enumerate(zip(vjp(ga), (ug, uv))):
            gu = jnp.zeros((ts, FFN_CB), F32)
            for k in range(FFN_CONV_W):
                gu = gu + _shifted(gc, k - 1, ts, H) * p[half][pl.ds(k, 1), :]
                a[half][pl.ds(k, 1), :] += jnp.sum(gc[H:H + ts] * _shifted(ue, 1 - k, ts, H), axis=0, keepdims=True)
            o[half][...] = gu.astype(BF16)

    ident, even, odd = (lambda j: j), (lambda j: 2 * j), (lambda j: 2 * j + 1)
    tiles = [(u, FFN_CB, even, True), (u, FFN_CB, odd, True), (gact, FFN_CB, ident, True)]
    params = [(cw, FFN_CONV_W, FFN_CB, even), (cw, FFN_CONV_W, FFN_CB, odd)]
    gu_gate, gu_val, gw_gate, gw_val = _rowwise("ffn_mid_bwd", fn, S, ts, tiles, params, [(FFN_CB, BF16, ident, D_FF)] * 2,
                                                [(FFN_CONV_W, FFN_CB, ident, D_FF)] * 2, halo=H, ncb=FFN_NCB)
    return (_interleave_ffn_cols(jnp.concatenate([gu_gate, gu_val], axis=1)),
            _interleave_ffn_cols(jnp.concatenate([gw_gate, gw_val], axis=1)))


LD_W = LANES
PAIR_W = 2 * HEAD
N_PAIRS = N_HEADS // 2


def _sub_view(t, d):
    S, C = t.shape
    return t.reshape(S // d, d * C)


def _sub_halo_specs(width, col, blk, hb, nhb):
    per = blk // hb
    return [
        pl.BlockSpec((hb, width), lambda r, i: (jnp.maximum(i * per - 1, 0), col(r))),
        pl.BlockSpec((blk, width), lambda r, i: (i, col(r))),
        pl.BlockSpec((hb, width), lambda r, i: (jnp.minimum((i + 1) * per, nhb - 1), col(r))),
    ]


def _pick_lane(t, lane):
    return jnp.sum(jnp.where(_iota((1, t.shape[1]), 1) == lane, t, 0.0), axis=1, keepdims=True)


def _pair_mask(h2):
    return (_iota((1, PAIR_W), 1) >> HEAD_SHIFT) == h2


def _cat_bf16(a, b, c):
    return jnp.concatenate([a[...], b[...], c[...]], axis=0).astype(BF16)


def _attn_fwd(view, dil):
    L = view.shape[0]
    blk, hb = min(ATT_BLK, L), HALF_BAND
    span = blk + 2 * hb

    def body(q_ref, kp, kc, kn, vp, vc, vn, o_ref, l_ref):
        i = pl.program_id(1)
        rel = _iota((blk, span), 1) - hb - _iota((blk, span), 0)
        kpos = i * blk - hb + _iota((blk, span), 1)
        valid = (jnp.abs(rel) <= hb) & (kpos >= 0) & (kpos < L)
        dist = jnp.abs(rel).astype(F32) * float(dil)
        q, k, v = q_ref[...].astype(BF16), _cat_bf16(kp, kc, kn), _cat_bf16(vp, vc, vn)
        lse = jnp.zeros((blk, LD_W), F32)
        for pr in range(N_PAIRS):
            sl = slice(pr * PAIR_W, (pr + 1) * PAIR_W)
            qp, kpair, vpair = q[:, sl], k[:, sl], v[:, sl]
            o = jnp.zeros((blk, PAIR_W), F32)
            for h2 in range(2):
                h, mask = 2 * pr + h2, _pair_mask(h2)
                s = _dot_nt(jnp.where(mask, qp, jnp.zeros_like(qp)), kpair) * (HEAD ** -0.5) - ALIBI_SLOPES[h] * dist
                s = jnp.where(valid, s, MASK_VALUE)
                m = jnp.max(s, axis=1, keepdims=True)
                p = jnp.exp(s - m)
                l = jnp.sum(p, axis=1, keepdims=True)
                o = jnp.where(mask, _dot_nn(p.astype(BF16), vpair) / l, o)
                lse = lse + jnp.where(_iota((1, LD_W), 1) == h, m + jnp.log(l), 0.0)
            o_ref[:, sl] = o
        l_ref[...] = lse

    nhb = L // hb
    in_specs = ([pl.BlockSpec((blk, ATT_W), lambda r, i: (i, r * QKV_BLOCKS +QA))]
                + _sub_halo_specs(ATT_W, lambda r: r * QKV_BLOCKS +KA, blk, hb, nhb)
                + _sub_halo_specs(ATT_W, lambda r: r * QKV_BLOCKS +VA, blk, hb, nhb))
    return pl.pallas_call(
        body, name=f"attn_fwd_d{dil}", grid=(dil, L // blk), in_specs=in_specs,
        out_specs=[pl.BlockSpec((blk, ATT_W), lambda r, i: (i, r)), pl.BlockSpec((blk, LD_W), lambda r, i: (i, r))],
        out_shape=[jax.ShapeDtypeStruct((L, dil * ATT_W), F32), jax.ShapeDtypeStruct((L, dil * LD_W), F32)],
        compiler_params=_cparams(("parallel", "parallel")),
    )(*([view] * 7))


def _attn_bwd(pview, gview, lview, dil):
    L = pview.shape[0]
    blk, hb = min(ATT_BLK, L), HALF_BAND
    span = blk + 2 * hb
    scale = HEAD ** -0.5

    def body(qp, qc, qn, kp, kc, kn, vp, vc, vn, gp, gc, gn, lp, lc, ln, dq_ref, dk_ref, dv_ref):
        i = pl.program_id(1)
        l = lc[...]
        le = jnp.concatenate([lp[...], l, ln[...]], axis=0)
        rel_q = _iota((blk, span), 1) - hb - _iota((blk, span), 0)
        kpos = i * blk - hb + _iota((blk, span), 1)
        valid_q = (jnp.abs(rel_q) <= hb) & (kpos >= 0) & (kpos < L)
        dist_q = jnp.abs(rel_q).astype(F32) * float(dil)
        rel_k = _iota((span, blk), 1) + hb - _iota((span, blk), 0)
        qpos = i * blk - hb + _iota((span, blk), 0)
        valid_k = (jnp.abs(rel_k) <= hb) & (qpos >= 0) & (qpos < L)
        dist_k = jnp.abs(rel_k).astype(F32) * float(dil)
        q_all, k_all, v_all, g_all = qc[...].astype(BF16), kc[...].astype(BF16), vc[...].astype(BF16), gc[...]
        qe_all, ke_all, ve_all = _cat_bf16(qp, qc, qn), _cat_bf16(kp, kc, kn), _cat_bf16(vp, vc, vn)
        ge_all = _cat_bf16(gp, gc, gn)
        for pr in range(N_PAIRS):
            sl = slice(pr * PAIR_W, (pr + 1) * PAIR_W)
            q, k, v, g = q_all[:, sl], k_all[:, sl], v_all[:, sl], g_all[:, sl]
            qe, ke, ve, ge = qe_all[:, sl], ke_all[:, sl], ve_all[:, sl], ge_all[:, sl]
            dq = jnp.zeros((blk, PAIR_W), F32)
            dk = jnp.zeros((blk, PAIR_W), F32)
            dv = jnp.zeros((blk, PAIR_W), F32)
            for h2 in range(2):
                h, mask = 2 * pr + h2, _pair_mask(h2)
                only = lambda t: jnp.where(mask, t, jnp.zeros_like(t))
                s = _dot_nt(only(q), ke) * scale - ALIBI_SLOPES[h] * dist_q
                p = jnp.where(valid_q, jnp.exp(s - _pick_lane(l, h)), 0.0)
                ds = p * (_dot_nt(only(g), ve) - _pick_lane(l, 8 + h))
                dq = jnp.where(mask, _dot_nn(ds.astype(BF16), ke), dq)
                s = _dot_nt(only(qe), k) * scale - ALIBI_SLOPES[h] * dist_k
                p = jnp.where(valid_k, jnp.exp(s - _pick_lane(le, h)), 0.0)
                dv = jnp.where(mask, _dot_tn(p.astype(BF16), ge), dv)
                ds = p * (_dot_nt(only(ge), v) - _pick_lane(le, 8 + h))
                dk = jnp.where(mask, _dot_tn(ds.astype(BF16), qe), dk)
            dq_ref[:, sl] = (dq * scale).astype(BF16)
            dk_ref[:, sl] = (dk * scale).astype(BF16)
            dv_ref[:, sl] = dv.astype(BF16)

    nhb = L // hb
    in_specs = (_sub_halo_specs(ATT_W, lambda r: r * QKV_BLOCKS +QA, blk, hb, nhb)
                + _sub_halo_specs(ATT_W, lambda r: r * QKV_BLOCKS +KA, blk, hb, nhb)
                + _sub_halo_specs(ATT_W, lambda r: r * QKV_BLOCKS +VA, blk, hb, nhb)
                + _sub_halo_specs(ATT_W, lambda r: r, blk, hb, nhb) + _sub_halo_specs(LD_W, lambda r: r, blk, hb, nhb))
    o_spec = pl.BlockSpec((blk, ATT_W), lambda r, i: (i, r))
    return pl.pallas_call(
        body, name=f"attn_bwd_d{dil}", grid=(dil, L // blk), in_specs=in_specs,
        out_specs=[o_spec] * 3, out_shape=[jax.ShapeDtypeStruct((L, dil * ATT_W), BF16)] * 3,
        compiler_params=_cparams(("parallel", "parallel")),
    )(*([pview] * 9 + [gview] * 3 + [lview] * 3))


def _head_expand(t):
    e = ((_iota((LD_W, ATT_W), 1) >> HEAD_SHIFT) == _iota((LD_W, ATT_W), 0)).astype(F32)
    return _dot_nn(t, e, HP)


def _dil(d):
    return d if d > 1 else False


def _qkv_views(proj):
    S, w = proj.shape[0], QKV_BLOCKS * ATT_W

    def fn(i, vals, p, o, a):
        for k, d in enumerate(DILATIONS):
            o[k][...] = vals[0].astype(o[k].dtype)

    outs = [(w, BF16, _c0, w) + ((d,) if d > 1 else ()) for d in DILATIONS]
    return _rowwise("qkv_views", fn, S, 512, [(proj, w, _c0, False)], (), outs)


def _attn_merge(os, ls):
    S = os[0].shape[0]

    def fn(i, vals, p, o, a):
        o3, l3 = vals[:3], vals[3:]
        m = jnp.maximum(jnp.maximum(l3[0], l3[1]), l3[2])
        e3 = [jnp.exp(l - m) for l in l3]
        den = e3[0] + e3[1] + e3[2]
        out = jnp.zeros((o3[0].shape[0], ATT_W), F32)
        for ob, e in zip(o3, e3):
            out = out + _head_expand(e / den) * ob
        o[0][...] = out
        o[1][...] = m + jnp.log(den)

    tiles = ([(t, ATT_W, _c0, _dil(d)) for t, d in zip(os, DILATIONS)]
             + [(t, LD_W, _c0, _dil(d)) for t, d in zip(ls, DILATIONS)])
    return _rowwise("attn_merge", fn, S, 512, tiles, (), [(ATT_W, F32, _c0, ATT_W), (LD_W, F32, _c0, LD_W)])


def _attn_bwd_prep(gmixed, att, lse):
    S = att.shape[0]
    n = len(DILATIONS)

    def fn(i, vals, p, o, a):
        g, out, lse_row = vals
        place_d = ((_iota((ATT_W, LD_W), 0) >> HEAD_SHIFT) + 8 == _iota((ATT_W, LD_W), 1)).astype(F32)
        ld = jnp.where(_iota((1, LD_W), 1) < 8, lse_row, 0.0) + _dot_nn(g * out, place_d, HP)
        for k in range(n):
            o[k][...] = g.astype(o[k].dtype)
            o[n + k][...] = ld

    tiles = [(gmixed, ATT_W, _c0, False), (att, ATT_W, _c0, False), (lse, LD_W, _c0, False)]
    outs = ([(ATT_W, BF16, _c0, ATT_W) + ((d,) if d > 1 else ()) for d in DILATIONS]
            + [(LD_W, F32, _c0, LD_W) + ((d,) if d > 1 else ()) for d in DILATIONS])
    res = _rowwise("attn_bwd_prep", fn, S, 512, tiles, (), outs)
    return res[:n], res[n:]


def _sum3_bf16(views, S, width):
    def fn(i, vals, p, o, a):
        o[0][...] = (vals[0].astype(F32) + vals[1].astype(F32) + vals[2].astype(F32)).astype(BF16)

    tiles = [(t, width, _c0, _dil(d)) for t, d in zip(views, DILATIONS)]
    return _rowwise("sum3", fn, S, 512, tiles, (), [(width, BF16, _c0, width)])[0]


def _block_diag_mask():
    return ((_iota((REC_W, REC_W), 0) >> HEAD_SHIFT) == (_iota((REC_W, REC_W), 1) >> HEAD_SHIFT)).astype(F32)


def _rep_heads(t):
    return jnp.concatenate([t] * N_HEADS, axis=0)


def _hgrn_chunk(qr, z, iv, lb, st, reverse, precise):
    C = REC_CHUNK
    r, c = _iota((C, C), 0), _iota((C, C), 1)
    t_cum = (c >= r) if reverse else (c <= r)
    mid_row, last_row = (C // 2, 0) if reverse else (C // 2 - 1, C - 1)
    f = lb + (1.0 - lb) * jax.nn.sigmoid(z)
    logf = jnp.log(jnp.maximum(f, F_TINY))
    k = (1.0 - lb) * jax.nn.sigmoid(-z)
    q = qr * jax.nn.sigmoid(qr)
    b = _dot_nn(t_cum.astype(F32), logf, HP)
    row = _iota((C, 1), 0)
    bm = jnp.sum(jnp.where(row == mid_row, b, 0.0), axis=0, keepdims=True)
    bl = jnp.sum(jnp.where(row == last_row, b, 0.0), axis=0, keepdims=True)
    qt = q * jnp.exp(jnp.minimum(b - bm, EXP_CLAMP))
    kt = k * jnp.exp(jnp.minimum(bm - b, EXP_CLAMP))
    qh = q * jnp.exp(b)
    kh = k * jnp.exp(bl - b)
    lam = jnp.exp(bl)
    bd = ((_iota((PAIR_W, PAIR_W), 0) >> HEAD_SHIFT) == (_iota((PAIR_W, PAIR_W), 1) >> HEAD_SHIFT)).astype(F32)
    s_in = _iota((C, PAIR_W), 1) & (HEAD - 1)
    t_in = _iota((C, PAIR_W), 0)
    tri = (s_in >= t_in) if reverse else (s_in <= t_in)
    twice = lambda t: jnp.concatenate([t, t], axis=0)
    outs, states = [], []
    for pr in range(N_PAIRS):
        sl = slice(pr * PAIR_W, (pr + 1) * PAIR_W)
        k_bd = twice(kt[:, sl]) * bd
        v_bd = (twice(iv[:, sl]) * bd).astype(BF16)
        st_bd = twice(st[:, sl]) * bd
        if precise:
            scores = _dot_nt(qt[:, sl], k_bd, lax.Precision.HIGH)
        else:
            scores = _dot_nt(qt[:, sl].astype(BF16), k_bd.astype(BF16))
        a = jnp.where(tri, scores, 0.0)
        outs.append(_dot_nn(a.astype(BF16), v_bd) + _dot_nt(qh[:, sl].astype(BF16), st_bd.astype(BF16)))
        kv = _dot_tn(iv[:, sl].astype(BF16), kh[:, sl].astype(BF16))
        st_bd = st_bd * lam[:, sl] + kv * bd
        states.append(st_bd[0:HEAD] + st_bd[HEAD:PAIR_W])
    return jnp.concatenate(outs, axis=1), jnp.concatenate(states, axis=1)


REC_CHUNKS_PER_STEP = 8
REC_ROWS = REC_CHUNKS_PER_STEP * REC_CHUNK


def _hgrn_specs(order, blocks):
    return [pl.BlockSpec((REC_ROWS, REC_W), lambda i, b=b: (order(i), b)) for b in blocks]


def _chunk_rows(j):
    return pl.ds(pl.multiple_of(j * REC_CHUNK, REC_CHUNK), REC_CHUNK)


def _hgrn_fwd(proj, lb, z_blk, reverse):
    S = proj.shape[0]
    nb = S // REC_ROWS
    order = (lambda i: nb - 1 - i) if reverse else (lambda i: i)

    def body(q_ref, z_ref, v_ref, lb_ref, o_ref, st_ref, st_scr):
        @pl.when(pl.program_id(0) == 0)
        def _():
            st_scr[...] = jnp.zeros_like(st_scr)

        def step(t, carry):
            j = REC_CHUNKS_PER_STEP - 1 - t if reverse else t
            rows = _chunk_rows(j)
            st = st_scr[...]
            st_ref[j] = st
            o, st_new = _hgrn_chunk(q_ref[rows, :], z_ref[rows, :], v_ref[rows, :], lb_ref[...], st, reverse, False)
            o_ref[rows, :] = o
            st_scr[...] = st_new
            return carry

        lax.fori_loop(0, REC_CHUNKS_PER_STEP, step, 0, unroll=2)

    return pl.pallas_call(
        body, name="hgrn_rev_fwd" if reverse else "hgrn_fwd_fwd", grid=(nb,),
        in_specs=_hgrn_specs(order, (QR, z_blk, IR)) + [pl.BlockSpec((1, REC_W), lambda i: (0, 0))],
        out_specs=[pl.BlockSpec((REC_ROWS, REC_W), lambda i: (order(i), 0)),
                   pl.BlockSpec((REC_CHUNKS_PER_STEP, HEAD, REC_W), lambda i: (order(i), 0, 0))],
        out_shape=[jax.ShapeDtypeStruct((S, REC_W), F32), jax.ShapeDtypeStruct((S // REC_CHUNK, HEAD, REC_W), F32)],
        scratch_shapes=[pltpu.VMEM((HEAD, REC_W), F32)],
        compiler_params=_cparams(("arbitrary",)),
    )(proj, proj, proj, lb)


def _hgrn_bwd(proj, lb, states, go, z_blk, reverse):
    S = proj.shape[0]
    nb = S // REC_ROWS
    order = (lambda i: i) if reverse else (lambda i: nb - 1 - i)

    def body(q_ref, z_ref, v_ref, lb_ref, st_ref, go_ref, gq_ref, gz_ref, gv_ref, glb_ref, gst_scr):
        @pl.when(pl.program_id(0) == 0)
        def _():
            gst_scr[...] = jnp.zeros_like(gst_scr)
            glb_ref[...] = jnp.zeros_like(glb_ref)

        chunk = functools.partial(_hgrn_chunk, reverse=reverse, precise=True)

        def step(t, carry):
            j = t if reverse else REC_CHUNKS_PER_STEP - 1 - t
            rows = _chunk_rows(j)
            _, vjp = jax.vjp(chunk, q_ref[rows, :], z_ref[rows, :], v_ref[rows, :], lb_ref[...], st_ref[j])
            gq, gz, gv, glb, gst = vjp((go_ref[rows, :], gst_scr[...]))
            gq_ref[rows, :] = gq
            gz_ref[rows, :] = gz
            gv_ref[rows, :] = gv
            glb_ref[...] += glb
            gst_scr[...] = gst
            return carry

        lax.fori_loop(0, REC_CHUNKS_PER_STEP, step, 0, unroll=2)

    row_spec = pl.BlockSpec((REC_ROWS, REC_W), lambda i: (order(i), 0))
    return pl.pallas_call(
        body, name="hgrn_rev_bwd" if reverse else "hgrn_fwd_bwd", grid=(nb,),
        in_specs=(_hgrn_specs(order, (QR, z_blk, IR)) + [pl.BlockSpec((1, REC_W), lambda i: (0, 0))]
                  + [pl.BlockSpec((REC_CHUNKS_PER_STEP, HEAD, REC_W), lambda i: (order(i), 0, 0)), row_spec]),
        out_specs=[row_spec] * 3 + [pl.BlockSpec((1, REC_W), lambda i: (0, 0))],
        out_shape=[jax.ShapeDtypeStruct((S, REC_W), F32)] * 3 + [jax.ShapeDtypeStruct((1, REC_W), F32)],
        scratch_shapes=[pltpu.VMEM((HEAD, REC_W), F32)],
        compiler_params=_cparams(("arbitrary",)),
    )(proj, proj, proj, lb, states, go)


def _hgrn_post_f(of, ob, gr, rnw):
    o = of + ob
    ms = _dot_nn(o * o, _block_diag_mask() * (1.0 / HEAD), HP)
    return o * lax.rsqrt(ms + EPS) * rnw * (gr * jax.nn.sigmoid(gr))


def _hgrn_post_fwd(of, ob, proj, rnw):
    S = of.shape[0]

    def fn(i, vals, p, o, a):
        o[0][...] = _hgrn_post_f(vals[0], vals[1], vals[2], p[0][...]).astype(BF16)

    tiles = [(of, REC_W, _c0, False), (ob, REC_W, _c0, False), (proj, REC_W, lambda j: GR, False)]
    return _rowwise("hgrn_post_fwd", fn, S, 512, tiles, _row_params(rnw), [(REC_W, BF16, _c0, REC_W)])[0]


def _hgrn_post_bwd(of, ob, proj, gmixed, rnw):
    S = of.shape[0]

    def fn(i, vals, p, o, a):
        _, vjp = jax.vjp(_hgrn_post_f, vals[0], vals[1], vals[2], p[0][...])
        go, _, ggr, grnw = vjp(vals[3])
        o[0][...] = go
        o[1][...] = ggr
        a[0][...] += grnw

    tiles = [(of, REC_W, _c0, False), (ob, REC_W, _c0, False), (proj, REC_W, lambda j: GR, False),
             (gmixed, REC_W, lambda j: 1, False)]
    return _rowwise("hgrn_post_bwd", fn, S, 256, tiles, _row_params(rnw),
                    [(REC_W, F32, _c0, REC_W), (REC_W, F32, _c0, REC_W)], [(1, REC_W, _c0, REC_W)])


def _lower_bounds_f(g0, g1):
    m = jnp.maximum(g0, g1)
    e0, e1 = jnp.exp(g0 - m), jnp.exp(g1 - m)
    return e1 / (e0 + e1)


def _adamw(name, w, m, v, gparts):
    R, C = w.shape
    P = gparts.shape[0]
    tr = R if R * C * 4 * (P + 7) * 2 <= VMEM_LIMIT_BYTES // 2 else _pick(R, (256, 128, 64, 32, 16, 8))

    def body(w_ref, m_ref, v_ref, gp_ref, g_ref, d_ref, nm_ref, nv_ref):
        g = gp_ref[0].astype(F32)
        for p in range(1, P):
            g = g + gp_ref[p].astype(F32)
        w_ = w_ref[...]
        nm = ADAM_B1 * m_ref[...] + (1.0 - ADAM_B1) * g
        nv = ADAM_B2 * v_ref[...] + (1.0 - ADAM_B2) * jnp.square(g)
        m_hat = nm / (1.0 - ADAM_B1 ** ADAM_STEP)
        v_hat = nv / (1.0 - ADAM_B2 ** ADAM_STEP)
        g_ref[...] = g
        d_ref[...] = -ADAM_LR * (m_hat / (jnp.sqrt(v_hat) + ADAM_EPS) + ADAM_WD * w_)
        nm_ref[...] = nm
        nv_ref[...] = nv

    spec = pl.BlockSpec((tr, C), lambda i: (i, 0))
    return pl.pallas_call(
        body, name=name, grid=(R // tr,),
        in_specs=[spec, spec, spec, pl.BlockSpec((P, tr, C), lambda i: (0, i, 0))],
        out_specs=[spec] * 4, out_shape=[jax.ShapeDtypeStruct((R, C), F32)] * 4,
        compiler_params=_cparams(("parallel",)),
    )(w, m, v, gparts)


def _place():
    return lax.axis_index("x"), lax.axis_index("y"), lax.axis_index("c")


def _index_of(p):
    return 4 * p[0] + 2 * p[1] + p[2]


def _allgather_small(name, rows):
    m_per, n = rows.shape

    def body(x_ref, out_ref, send_sems, recv_sems, local_sem):
        x, y, c = _place()
        me, sibling = (x, y, c), (x, y, 1 - c)
        chips = [(1 - x, y), (x, 1 - y), (1 - x, 1 - y)]

        def blk(p):
            return out_ref.at[pl.ds(_index_of(p) * m_per, m_per), :]

        def copy(k, block, to, src=None):
            return pltpu.make_async_remote_copy(
                src_ref=blk(block) if src is None else src, dst_ref=blk(block),
                send_sem=send_sems.at[k], recv_sem=recv_sems.at[k], device_id=to, device_id_type=MESH)

        mine = pltpu.make_async_copy(x_ref, blk(me), local_sem)
        mine.start()
        first = [copy(0, me, sibling, src=x_ref)]
        first += [copy(1 + j, me, (*chip, c), src=x_ref) for j, chip in enumerate(chips)]
        for cp in first:
            cp.start()
        passed = [copy(4 + j, (*chip, c), sibling) for j, chip in enumerate(chips)]
        for j, chip in enumerate(chips):
            copy(1 + j, (*chip, c), me).wait_recv()
            passed[j].start()
        copy(0, sibling, me).wait_recv()
        for j, chip in enumerate(chips):
            copy(4 + j, (*chip, 1 - c), me).wait_recv()
        for cp in first + passed:
            cp.wait_send()
        mine.wait()

    return pl.pallas_call(
        body, name=name,
        out_shape=jax.ShapeDtypeStruct((N_DEV * m_per, n), rows.dtype),
        in_specs=[pl.BlockSpec(memory_space=pltpu.VMEM)],
        out_specs=pl.BlockSpec(memory_space=pltpu.VMEM),
        scratch_shapes=[pltpu.SemaphoreType.DMA((7,)), pltpu.SemaphoreType.DMA((7,)), pltpu.SemaphoreType.DMA],
        compiler_params=_cparams(),
    )(rows)


def _allgather_big(name, arrs):
    na = len(arrs)

    def body(*refs):
        ins, outs = refs[:na], refs[na:2 * na]
        send_sems, recv_sems, local_sems = refs[2 * na:]
        x, y, c = _place()
        me, sibling = (x, y, c), (x, y, 1 - c)
        chips = [(1 - x, y), (x, 1 - y), (1 - x, 1 - y)]

        def copy(a, k, block, to, src=None):
            dst = outs[a].at[_index_of(block)]
            return pltpu.make_async_remote_copy(
                src_ref=dst if src is None else src, dst_ref=dst,
                send_sem=send_sems.at[a, k], recv_sem=recv_sems.at[a, k], device_id=to, device_id_type=MESH)

        mine = [pltpu.make_async_copy(ins[a], outs[a].at[_index_of(me)], local_sems.at[a]) for a in range(na)]
        for cp in mine:
            cp.start()
        sent = []
        for a in range(na):
            sent.append(copy(a, 0, me, sibling, src=ins[a]))
            sent += [copy(a, 1 + j, me, (*chip, c), src=ins[a]) for j, chip in enumerate(chips)]
        for cp in sent:
            cp.start()
        for j, chip in enumerate(chips):
            for a in range(na):
                copy(a, 1 + j, (*chip, c), me).wait_recv()
                fwd = copy(a, 4 + j, (*chip, c), sibling)
                fwd.start()
                sent.append(fwd)
        for a in range(na):
            copy(a, 0, sibling, me).wait_recv()
            for j, chip in enumerate(chips):
                copy(a, 4 + j, (*chip, 1 - c), me).wait_recv()
        for cp in sent:
            cp.wait_send()
        for cp in mine:
            cp.wait()

    any_spec = pl.BlockSpec(memory_space=pl.ANY)
    return pl.pallas_call(
        body, name=name,
        out_shape=[jax.ShapeDtypeStruct((N_DEV,) + a.shape, a.dtype) for a in arrs],
        in_specs=[any_spec] * na, out_specs=[any_spec] * na,
        scratch_shapes=[pltpu.SemaphoreType.DMA((na, 7)), pltpu.SemaphoreType.DMA((na, 7)), pltpu.SemaphoreType.DMA((na,))],
        compiler_params=_cparams(),
    )(*arrs)


def _scatter_parts(name, parts):
    na = len(parts)

    def body(*refs):
        ins, outs = refs[:na], refs[na:2 * na]
        send_sems, recv_sems, local_sems = refs[2 * na:]
        x, y, c = _place()
        me = _index_of((x, y, c))
        flips = [(k >> 2 & 1, k >> 1 & 1, k & 1) for k in range(1, N_DEV)]
        peers = [(1 - x if fx else x, 1 - y if fy else y, 1 - c if fc else c) for fx, fy, fc in flips]
        mine = [pltpu.make_async_copy(ins[a].at[me], outs[a].at[me], local_sems.at[a]) for a in range(na)]
        for cp in mine:
            cp.start()
        sent = []
        for a in range(na):
            for k, peer in enumerate(peers):
                sent.append(pltpu.make_async_remote_copy(
                    src_ref=ins[a].at[_index_of(peer)], dst_ref=outs[a].at[me],
                    send_sem=send_sems.at[a, k], recv_sem=recv_sems.at[a, k], device_id=peer, device_id_type=MESH))
        for cp in sent:
            cp.start()
        for a in range(na):
            for k, peer in enumerate(peers):
                slot = outs[a].at[_index_of(peer)]
                pltpu.make_async_remote_copy(
                    src_ref=slot, dst_ref=slot, send_sem=send_sems.at[a, k], recv_sem=recv_sems.at[a, k],
                    device_id=peer, device_id_type=MESH).wait_recv()
        for cp in sent:
            cp.wait_send()
        for cp in mine:
            cp.wait()

    any_spec = pl.BlockSpec(memory_space=pl.ANY)
    return pl.pallas_call(
        body, name=name,
        out_shape=[jax.ShapeDtypeStruct(p.shape, p.dtype) for p in parts],
        in_specs=[any_spec] * na, out_specs=[any_spec] * na,
        scratch_shapes=[pltpu.SemaphoreType.DMA((na, 7)), pltpu.SemaphoreType.DMA((na, 7)), pltpu.SemaphoreType.DMA((na,))],
        compiler_params=_cparams(),
    )(*parts)


N_CHIPS = 4


def _scatter_to_sibling(name, parts):
    na = len(parts)

    def body(*refs):
        ins, outs = refs[:na], refs[na:2 * na]
        send_sems, recv_sems = refs[2 * na:]
        x, y, c = _place()
        sibling = (x, y, 1 - c)
        sent = []
        for a in range(na):
            for q in range(N_CHIPS):
                sent.append(pltpu.make_async_remote_copy(
                    src_ref=ins[a].at[2 * q + (1 - c)], dst_ref=outs[a].at[q],
                    send_sem=send_sems.at[a, q], recv_sem=recv_sems.at[a, q], device_id=sibling, device_id_type=MESH))
        for cp in sent:
            cp.start()
        for cp in sent:
            cp.wait_recv()
        for cp in sent:
            cp.wait_send()

    any_spec = pl.BlockSpec(memory_space=pl.ANY)
    return pl.pallas_call(
        body, name=name,
        out_shape=[jax.ShapeDtypeStruct((N_CHIPS,) + p.shape[1:], p.dtype) for p in parts],
        in_specs=[any_spec] * na, out_specs=[any_spec] * na,
        scratch_shapes=[pltpu.SemaphoreType.DMA((na, N_CHIPS)), pltpu.SemaphoreType.DMA((na, N_CHIPS))],
        compiler_params=_cparams(),
    )(*parts)


def _pair_sum(name, parts, recv):
    _, R, C = parts.shape
    tr = _pick(R, (256, 128, 64, 32, 16))

    def body(p_ref, r_ref, o_ref):
        c = lax.axis_index("c")
        o_ref[...] = (p_ref[c].astype(F32) + r_ref[...].astype(F32)).astype(BF16)

    return pl.pallas_call(
        body, name=name, grid=(N_CHIPS, R // tr),
        in_specs=[pl.BlockSpec((None, 2, tr, C), lambda q, i: (q, 0, i, 0)), pl.BlockSpec((None, tr, C), lambda q, i: (q, i, 0))],
        out_specs=pl.BlockSpec((None, tr, C), lambda q, i: (q, i, 0)),
        out_shape=jax.ShapeDtypeStruct((N_CHIPS, R, C), BF16),
        compiler_params=_cparams(("parallel", "parallel")),
    )(parts.reshape(N_CHIPS, 2, R, C), recv)


def _scatter_to_chips(name, sums):
    na = len(sums)

    def body(*refs):
        ins, outs = refs[:na], refs[na:2 * na]
        send_sems, recv_sems, local_sems = refs[2 * na:]
        x, y, c = _place()
        me = 2 * x + y
        chips = [(1 - x, y), (x, 1 - y), (1 - x, 1 - y)]
        mine = [pltpu.make_async_copy(ins[a].at[me], outs[a].at[me], local_sems.at[a]) for a in range(na)]
        for cp in mine:
            cp.start()
        sent = []
        for a in range(na):
            for k, (qx, qy) in enumerate(chips):
                sent.append(pltpu.make_async_remote_copy(
                    src_ref=ins[a].at[2 * qx + qy], dst_ref=outs[a].at[me],
                    send_sem=send_sems.at[a, k], recv_sem=recv_sems.at[a, k], device_id=(qx, qy, c), device_id_type=MESH))
        for cp in sent:
            cp.start()
        for a in range(na):
            for k, (qx, qy) in enumerate(chips):
                slot = outs[a].at[2 * qx + qy]
                pltpu.make_async_remote_copy(
                    src_ref=slot, dst_ref=slot, send_sem=send_sems.at[a, k], recv_sem=recv_sems.at[a, k],
                    device_id=(qx, qy, c), device_id_type=MESH).wait_recv()
        for cp in sent:
            cp.wait_send()
        for cp in mine:
            cp.wait()

    any_spec = pl.BlockSpec(memory_space=pl.ANY)
    return pl.pallas_call(
        body, name=name,
        out_shape=[jax.ShapeDtypeStruct(p.shape, p.dtype) for p in sums],
        in_specs=[any_spec] * na, out_specs=[any_spec] * na,
        scratch_shapes=[pltpu.SemaphoreType.DMA((na, 3)), pltpu.SemaphoreType.DMA((na, 3)), pltpu.SemaphoreType.DMA((na,))],
        compiler_params=_cparams(),
    )(*sums)


def _gather_row(name, vec, width):
    n = vec.shape[0]
    rows = jnp.pad(vec, (0, width - n)).reshape(SUBLANES_F32, width // SUBLANES_F32)
    return _allgather_small(name, rows).reshape(N_DEV, width)[:, :n]


def _layer_fwd(x, mod, w):
    sh1, sc1, g1, sh2, sc2, g2 = [mod[i:i + 1] for i in range(N_MOD)]
    S = x.shape[0]
    h1 = _normmod_fwd(x, w["norm1_w"], sc1, sh1)
    proj = _matmul("proj_in", h1, w["w_in"], "nn")
    a1, a_out = _conv_a_fwd(proj, w["conv_a_w"], w["conv_a_b"], w["ln_a_w"], w["ln_a_b"])
    qkv = _qkv_views(proj)
    os, ls = zip(*[_attn_fwd(v, dil) for v, dil in zip(qkv, DILATIONS)])
    att, lse = _attn_merge(os, ls)
    of, st_f = _hgrn_fwd(proj, w["lb_f"], ZF, False)
    ob, st_b = _hgrn_fwd(proj, w["lb_b"], ZB, True)
    rec = _hgrn_post_fwd(of, ob, proj, w["rec_norm_w"])
    mixed = jnp.concatenate([att.astype(BF16), rec, a_out], axis=1)
    y1 = _matmul("proj_out", mixed, w["w_out"], "nn")
    x2 = _gate_add(x, y1, g1)
    h2 = _normmod_fwd(x2, w["norm2_w"], sc2, sh2)
    u = _matmul("ffn_up", h2, w["w_up"], "nn")
    act = _ffn_mid_fwd(u, w["conv_f_w"])
    y2 = _matmul("ffn_down", act, w["w_down"], "nn")
    x3 = _gate_add(x2, y2, g2)
    saved = dict(x=x, h1=h1, proj=proj, a1=a1, qkv=qkv, att=att, lse=lse, of=of, ob=ob, st_f=st_f, st_b=st_b,
                 mixed=mixed, y1=y1, x2=x2, h2=h2, u=u, act=act, y2=y2)
    return x3, saved


def _layer_bwd(gx3, mod, w, s):
    sh1, sc1, g1, sh2, sc2, g2 = [mod[i:i + 1] for i in range(N_MOD)]
    S = gx3.shape[0]
    g = {}
    gy2, gg2 = _gate_bwd(gx3, s["y2"], g2)
    gact = _matmul("ffn_down_dx", gy2, w["w_down"], "nt")
    g["w_down"] = _matmul("ffn_down_dw", s["act"], gy2, "tn")
    gu, g["conv_f_w"] = _ffn_mid_bwd(s["u"], gact, w["conv_f_w"])
    gh2 = _matmul("ffn_up_dx", gu, w["w_up"], "nt")
    g["w_up"] = _matmul("ffn_up_dw", s["h2"], gu, "tn")
    gx2, g["norm2_w"], gsc2, gsh2 = _normmod_bwd(s["x2"], gh2, gx3, w["norm2_w"], sc2, sh2)
    gy1, gg1 = _gate_bwd(gx2, s["y1"], g1)
    gmixed = _matmul("proj_out_dx", gy1, w["w_out"], "nt")
    g["w_out"] = _matmul("proj_out_dw", s["mixed"], gy1, "tn")
    go, ggr, g["rec_norm_w"] = _hgrn_post_bwd(s["of"], s["ob"], s["proj"], gmixed, w["rec_norm_w"])
    gq_f, gz_f, gv_f, g["lb_f"] = _hgrn_bwd(s["proj"], w["lb_f"], s["st_f"], go, ZF, False)
    gq_b, gz_b, gv_b, g["lb_b"] = _hgrn_bwd(s["proj"], w["lb_b"], s["st_b"], go, ZB, True)
    dos, lds = _attn_bwd_prep(gmixed, s["att"], s["lse"])
    gqkv = zip(*[_attn_bwd(v, do, ld, dil) for v, do, ld, dil in zip(s["qkv"], dos, lds, DILATIONS)])
    gq_a, gk_a, gv_a = [_sum3_bf16(lst, S, ATT_W) for lst in gqkv]
    gav, gag, gcw, g["conv_a_b"], g["ln_a_w"], g["ln_a_b"] = _conv_a_bwd(
        s["proj"], s["a1"], gmixed, w["conv_a_w"], w["ln_a_w"], w["ln_a_b"])
    g["conv_a_w"] = gcw[:CONV_W]
    gproj = jnp.concatenate([gq_a, gk_a, gv_a, (gq_f + gq_b).astype(BF16), gz_f.astype(BF16), gz_b.astype(BF16),
                             (gv_f + gv_b).astype(BF16), ggr.astype(BF16), gav, gag,
                             jnp.zeros((S, IN_COLS_PAD - IN_COLS), BF16)], axis=1)
    gh1 = _matmul("proj_in_dx", gproj, w["w_in"], "nt")
    g["w_in"] = _matmul("proj_in_dw", s["h1"], gproj, "tn")
    gx, g["norm1_w"], gsc1, gsh1 = _normmod_bwd(s["x"], gh1, gx2, w["norm1_w"], sc1, sh1)
    gmod = jnp.concatenate([gsh1, gsc1, gg1, gsh2, gsc2, gg2], axis=0)
    return gx, gmod, g


def _permute_in_cols(t):
    pad = jnp.zeros(t.shape[:-1] + (IN_COLS_PAD - IN_COLS,), t.dtype)
    return jnp.concatenate([t[..., CONV_COLS:], t[..., :CONV_COLS], pad], axis=-1)


def _unpermute_in_cols(t):
    return jnp.concatenate([t[..., IN_COLS - CONV_COLS:IN_COLS], t[..., :IN_COLS - CONV_COLS]], axis=-1)


def _cols_from_gathered(t, lead):
    nd = t.ndim
    perm = tuple(range(1, nd - 1)) + (0, nd - 1)
    t = t.transpose(perm)
    return t.reshape(t.shape[:-2] + (t.shape[-2] * t.shape[-1],))


def _cols_to_parts(t):
    L, R, C = t.shape
    return t.reshape(L * R, N_DEV, C // N_DEV).transpose(1, 0, 2)


SMALL_REPL = (("norm1_w", D), ("conv_a_b", CONV_CH), ("ln_a_w", CONV_CH), ("ln_a_b", CONV_CH),
              ("rec_norm_w", REC_W), ("norm2_w", D))


def kernel(x, c, w_ada, b_ada, norm1_w, w_in, conv_a_w, conv_a_b, ln_a_w, ln_a_b, lb_gamma, rec_norm_w, w_out, norm2_w, w_up, conv_f_w, w_down, final_norm_w, loss_target, m_w_ada, m_b_ada, m_norm1_w, m_w_in, m_conv_a_w, m_conv_a_b, m_ln_a_w, m_ln_a_b, m_lb_gamma, m_rec_norm_w, m_w_out, m_norm2_w, m_w_up, m_conv_f_w, m_w_down, m_final_norm_w, v_w_ada, v_b_ada, v_norm1_w, v_w_in, v_conv_a_w, v_conv_a_b, v_ln_a_w, v_ln_a_b, v_lb_gamma, v_rec_norm_w, v_w_out, v_norm2_w, v_w_up, v_conv_f_w, v_w_down, v_final_norm_w):
    px, py, pc = _place()
    me = _index_of((px, py, pc))
    xs, tgt = x[0], loss_target[0]
    S = xs.shape[0]
    ada_cols = w_ada.shape[2]

    big = [w_in.reshape(DEPTH * D, -1), w_up.reshape(DEPTH * D, -1), w_out.reshape(-1, D), w_down.reshape(-1, D)]
    g_in, g_up, g_out, g_down = _allgather_big("gather_weights", [t.astype(BF16) for t in big])
    w_in_f = _permute_in_cols(_cols_from_gathered(g_in.reshape(N_DEV, DEPTH, D, -1), 1))
    w_up_f = _interleave_ffn_cols(_cols_from_gathered(g_up.reshape(N_DEV, DEPTH, D, -1), 1))
    w_out_f = g_out.reshape(N_DEV, DEPTH, D // N_DEV, D).transpose(1, 0, 2, 3).reshape(DEPTH, D, D)
    w_out_f = jnp.concatenate([w_out_f[:, CONV_CH:], w_out_f[:, :CONV_CH]], axis=1)
    w_down_f = g_down.reshape(N_DEV, DEPTH, D_FF // N_DEV, D).transpose(1, 0, 2, 3).reshape(DEPTH, D_FF, D)

    small_in = jnp.concatenate([c.reshape(-1), conv_a_w.reshape(-1), lb_gamma.reshape(-1), conv_f_w.reshape(-1)])
    gs = _gather_row("gather_small", small_in, 8192)
    o1 = D
    o2 = o1 + conv_a_w.size
    o3 = o2 + lb_gamma.size
    c_all = gs[:, :o1]
    conv_a_f = _cols_from_gathered(gs[:, o1:o2].reshape(N_DEV, DEPTH, CONV_W, -1), 1)
    lb_gamma_f = _cols_from_gathered(gs[:, o2:o3].reshape(N_DEV, DEPTH, 2, -1), 1)
    conv_f_f = _interleave_ffn_cols(_cols_from_gathered(gs[:, o3:].reshape(N_DEV, DEPTH, FFN_CONV_W, -1), 1))
    conv_a_pad = jnp.pad(conv_a_f, ((0, 0), (0, CONV_W_PAD - CONV_W), (0, 0)))

    b_loc = lax.dynamic_slice_in_dim(b_ada, me * ada_cols, ada_cols, axis=1)

    def mod_fn(c_all_, w_, b_):
        cond = c_all_ * jax.nn.sigmoid(c_all_)
        return (jnp.concatenate([_dot_nn(cond, w_[l], HP) + b_[l] for l in range(DEPTH)], axis=1),)

    (mod_part,) = _vmem_call("ada_mod", mod_fn, [c_all, w_ada, b_loc[:, None, :]], [((N_DEV, DEPTH * ada_cols), F32)])
    gm = _allgather_small("gather_mod", mod_part).reshape(N_DEV, N_DEV, DEPTH, ada_cols)
    mod = lax.dynamic_index_in_dim(gm, me, axis=1, keepdims=False)
    mod = mod.transpose(1, 0, 2).reshape(DEPTH, N_MOD, D)

    (lb1,) = _vmem_call("lower_bounds", lambda a, b: (_lower_bounds_f(a, b),), [lb_gamma_f[0], lb_gamma_f[1]], [((2, REC_W), F32)])
    lb4 = jnp.concatenate([jnp.zeros_like(lb1), lb1], axis=0)

    def layer_weights(l):
        row = lambda t: t[l].reshape(1, -1)
        return dict(norm1_w=row(norm1_w), w_in=w_in_f[l], conv_a_w=conv_a_pad[l], conv_a_b=row(conv_a_b), ln_a_w=row(ln_a_w),
                    ln_a_b=row(ln_a_b), lb_f=lb4[2 * l:2 * l + 1], lb_b=lb4[2 * l + 1:2 * l + 2], rec_norm_w=row(rec_norm_w),
                    w_out=w_out_f[l], norm2_w=row(norm2_w), w_up=w_up_f[l], conv_f_w=conv_f_f[l], w_down=w_down_f[l])

    ws = [layer_weights(l) for l in range(DEPTH)]
    h, saved = xs, []
    for l in range(DEPTH):
        h, s = _layer_fwd(h, mod[l], ws[l])
        saved.append(s)
    gh, g_final, loss_row = _loss_head(h, tgt, final_norm_w.reshape(1, D))
    loss = lax.psum(loss_row[0, 0], ("x", "y", "c"))
    gmods, gws = [None] * DEPTH, [None] * DEPTH
    for l in reversed(range(DEPTH)):
        gh, gmods[l], gws[l] = _layer_bwd(gh, mod[l], ws[l], saved[l])
    grad_x = gh[None]

    glb1 = jnp.concatenate([gws[1]["lb_f"], gws[1]["lb_b"]], axis=0)

    def lb_bwd_fn(a, b, g1):
        _, vjp = jax.vjp(_lower_bounds_f, a, b)
        return vjp(g1)

    g_lb_gamma = jnp.stack(_vmem_call("lower_bounds_bwd", lb_bwd_fn, [lb_gamma_f[0], lb_gamma_f[1], glb1], [((2, REC_W), F32)] * 2))
    pieces = [jnp.stack(gmods).reshape(-1)]
    for l in range(DEPTH):
        pieces += [gws[l][n].reshape(-1) for n, _ in SMALL_REPL]
    pieces += [g_final.reshape(-1)]
    pieces += [jnp.stack([gws[l]["conv_a_w"] for l in range(DEPTH)]).reshape(-1), g_lb_gamma.reshape(-1),
               _deinterleave_ffn_cols(jnp.stack([gws[l]["conv_f_w"] for l in range(DEPTH)])).reshape(-1)]
    small_g = jnp.concatenate(pieces)
    n_small = small_g.shape[0]
    gsm = _gather_row("gather_small_grads", small_g, 71680)
    n_mod = DEPTH * N_MOD * D
    gmod_all = gsm[:, :n_mod].reshape(N_DEV, DEPTH, N_MOD * D)
    gmod_loc = lax.dynamic_slice_in_dim(gmod_all, me * ada_cols, ada_cols, axis=2).transpose(1, 0, 2)

    def small_fn(gsm_, c_all_, gm_):
        cond = c_all_ * jax.nn.sigmoid(c_all_)
        gw = jnp.concatenate([_dot_tn(cond, gm_[l], HP) for l in range(DEPTH)], axis=0)
        return jnp.sum(gsm_, axis=0, keepdims=True), gw

    tot, g_w_ada = _vmem_call("small_grads", small_fn, [gsm, c_all, gmod_loc],
                              [((1, n_small), F32), ((DEPTH * D, ada_cols), F32)])
    tot = tot[0]
    grads = {"w_ada": g_w_ada.reshape(DEPTH, D, ada_cols), "b_ada": tot[:n_mod].reshape(DEPTH, N_MOD * D)}
    pos = n_mod
    per_layer = {n: [] for n, _ in SMALL_REPL}
    for l in range(DEPTH):
        for n, width in SMALL_REPL:
            per_layer[n].append(tot[pos:pos + width])
            pos += width
    for n, _ in SMALL_REPL:
        grads[n] = jnp.stack(per_layer[n])
    grads["final_norm_w"] = tot[pos:pos + D]
    pos += D
    n_ca, n_lb, n_cf = DEPTH * CONV_W * CONV_CH, DEPTH * 2 * REC_W, DEPTH * FFN_CONV_W * 2 * D_FF
    g_ca = tot[pos:pos + n_ca].reshape(DEPTH, CONV_W, CONV_CH)
    g_lb = tot[pos + n_ca:pos + n_ca + n_lb].reshape(DEPTH, 2, REC_W)
    g_cf = tot[pos + n_ca + n_lb:pos + n_ca + n_lb + n_cf].reshape(DEPTH, FFN_CONV_W, 2 * D_FF)
    grads["conv_a_w"] = lax.dynamic_slice_in_dim(g_ca, me * conv_a_w.shape[2], conv_a_w.shape[2], axis=2)
    grads["lb_gamma"] = lax.dynamic_slice_in_dim(g_lb, me * lb_gamma.shape[2], lb_gamma.shape[2], axis=2)
    grads["conv_f_w"] = lax.dynamic_slice_in_dim(g_cf, me * conv_f_w.shape[2], conv_f_w.shape[2], axis=2)

    gw_in = _unpermute_in_cols(jnp.stack([gws[l]["w_in"] for l in range(DEPTH)]))
    gw_up = _deinterleave_ffn_cols(jnp.stack([gws[l]["w_up"] for l in range(DEPTH)]))
    gw_out = jnp.stack([gws[l]["w_out"] for l in range(DEPTH)])
    gw_out = jnp.concatenate([gw_out[:, D - CONV_CH:], gw_out[:, :D - CONV_CH]], axis=1)
    gw_down = jnp.stack([gws[l]["w_down"] for l in range(DEPTH)])
    rows_to_parts = lambda t: t.reshape(DEPTH, N_DEV, -1, D).transpose(1, 0, 2, 3).reshape(N_DEV, -1, D)
    parts = [_cols_to_parts(gw_in), _cols_to_parts(gw_up), rows_to_parts(gw_out), rows_to_parts(gw_down)]
    parts = [t.astype(BF16) for t in parts]
    from_sibling = _scatter_to_sibling("scatter_sibling", parts)
    sums = [_pair_sum("pair_sum", p, r) for p, r in zip(parts, from_sibling)]
    r_in, r_up, r_out, r_down = _scatter_to_chips("scatter_chips", sums)

    given = dict(w_ada=(w_ada, m_w_ada, v_w_ada), b_ada=(b_ada, m_b_ada, v_b_ada), norm1_w=(norm1_w, m_norm1_w, v_norm1_w),
                 w_in=(w_in, m_w_in, v_w_in), conv_a_w=(conv_a_w, m_conv_a_w, v_conv_a_w), conv_a_b=(conv_a_b, m_conv_a_b, v_conv_a_b),
                 ln_a_w=(ln_a_w, m_ln_a_w, v_ln_a_w), ln_a_b=(ln_a_b, m_ln_a_b, v_ln_a_b), lb_gamma=(lb_gamma, m_lb_gamma, v_lb_gamma),
                 rec_norm_w=(rec_norm_w, m_rec_norm_w, v_rec_norm_w), w_out=(w_out, m_w_out, v_w_out),
                 norm2_w=(norm2_w, m_norm2_w, v_norm2_w), w_up=(w_up, m_w_up, v_w_up), conv_f_w=(conv_f_w, m_conv_f_w, v_conv_f_w),
                 w_down=(w_down, m_w_down, v_w_down), final_norm_w=(final_norm_w, m_final_norm_w, v_final_norm_w))
    big_parts = dict(w_in=r_in, w_up=r_up, w_out=r_out, w_down=r_down)
    names = list(given)
    res = {}
    for n in names:
        w_, m_, v_ = given[n]
        shape = w_.shape
        C = shape[-1]
        two_d = lambda t: t.reshape(-1, C)
        gp = big_parts[n] if n in big_parts else two_d(grads[n])[None]
        res[n] = [t.reshape(shape) for t in _adamw("adamw_" + n, two_d(w_), two_d(m_), two_d(v_), gp)]
    return (loss, grad_x, *[res[n][0] for n in names], *[res[n][1] for n in names],
            *[res[n][2] for n in names], *[res[n][3] for n in names])
```

```python
import functools

import jax
import jax.numpy as jnp
from jax import lax
from jax.experimental import pallas as pl
from jax.experimental.pallas import tpu as pltpu

F32 = jnp.float32
BF16 = jnp.bfloat16
HP = lax.Precision.HIGHEST
MESH = pl.DeviceIdType.MESH

N_DEV = 8
D = 1024
DEPTH = 2
CONV_CH = 256
CONV_W = 31
CONV_W_PAD = 32
ATT_W = 384
REC_W = 384
N_HEADS = 6
HEAD = 64
HEAD_SHIFT = 6
HALF_BAND = 64
ATT_BLK = 128
DILATIONS = (1, 4, 16)
ALIBI_SLOPES = tuple(float(2.0 ** (-8.0 * (h + 1) / N_HEADS)) for h in range(N_HEADS))
MASK_VALUE = -1e30
REC_CHUNK = 64
EXP_CLAMP = 80.0
F_TINY = 1e-30
IN_COLS = 3584
IN_COLS_PAD = IN_COLS
QKV_BLOCKS = 3
D_FF = 2816
FFN_CONV_W = 3
N_MOD = 6
EPS = 1e-6
ADAM_LR, ADAM_B1, ADAM_B2, ADAM_EPS, ADAM_WD, ADAM_STEP = 0.001, 0.9, 0.999, 1e-08, 0.01, 10

VMEM_LIMIT_BYTES = 56 * 1024 * 1024
SUBLANES_F32 = 8
LANES = 128

QA, KA, VA, QR, ZF, ZB, IR, GR = range(8)
AV_BLK, AG_BLK = 12, 13
CONV_COLS = 2 * CONV_CH


def _cparams(sem=None):
    kw = dict(vmem_limit_bytes=VMEM_LIMIT_BYTES)
    if sem is not None:
        kw["dimension_semantics"] = sem
    return pltpu.CompilerParams(**kw)


def _iota(shape, dim):
    return lax.broadcasted_iota(jnp.int32, shape, dim)


def _dot(a, b, dims, precision=None):
    return lax.dot_general(a, b, (dims, ((), ())), precision=precision, preferred_element_type=F32)


def _dot_nn(a, b, precision=None):
    return _dot(a, b, ((1,), (0,)), precision)


def _dot_nt(a, b, precision=None):
    return _dot(a, b, ((1,), (1,)), precision)


def _dot_tn(a, b, precision=None):
    return _dot(a, b, ((0,), (0,)), precision)


def _c0(j):
    return 0


def _pick(n, cands):
    for c in cands:
        if n % c == 0:
            return c
    return n


MATMUL_OUT_TILE_BYTES = 8 * 1024 * 1024


def _div_lanes(n, cap):
    best = None
    for d in range(LANES, min(n, cap) + 1, LANES):
        if n % d == 0:
            best = d
    return best if best is not None else n


def _matmul_tiles(mode, M, N, K):
    if mode == "nn":
        tm = _pick(M, (1024, 512, 256, 128))
        return tm, _div_lanes(N, MATMUL_OUT_TILE_BYTES // (4 * tm)), K
    if mode == "nt":
        return _pick(M, (512, 256, 128)), N, K
    tm = _div_lanes(M, 1408)
    return tm, _div_lanes(N, MATMUL_OUT_TILE_BYTES // (4 * tm)), _pick(K, (1024, 512, 256))


def _matmul(name, a, b, mode, out_dtype=F32):
    if mode == "nn":
        (M, K), (_, N) = a.shape, b.shape
    elif mode == "nt":
        (M, K), (N, _) = a.shape, b.shape
    else:
        (K, M), (_, N) = a.shape, b.shape
    tm, tn, tk = _matmul_tiles(mode, M, N, K)
    nk = K // tk
    if mode == "nn":
        a_spec = pl.BlockSpec((tm, tk), lambda i, j, k: (i, k))
        b_spec = pl.BlockSpec((tk, tn), lambda i, j, k: (k, j))
        dims = ((1,), (0,))
    elif mode == "nt":
        a_spec = pl.BlockSpec((tm, tk), lambda i, j, k: (i, k))
        b_spec = pl.BlockSpec((tn, tk), lambda i, j, k: (j, k))
        dims = ((1,), (1,))
    else:
        a_spec = pl.BlockSpec((tk, tm), lambda i, j, k: (k, i))
        b_spec = pl.BlockSpec((tk, tn), lambda i, j, k: (k, j))
        dims = ((0,), (0,))

    def body_whole(a_ref, b_ref, o_ref):
        o_ref[...] = _dot(a_ref[...].astype(BF16), b_ref[...].astype(BF16), dims).astype(o_ref.dtype)

    def body(a_ref, b_ref, o_ref, acc_ref):
        k = pl.program_id(2)
        part = _dot(a_ref[...].astype(BF16), b_ref[...].astype(BF16), dims)

        @pl.when(k == 0)
        def _():
            acc_ref[...] = part

        @pl.when(k > 0)
        def _():
            acc_ref[...] += part

        @pl.when(k == nk - 1)
        def _():
            o_ref[...] = acc_ref[...].astype(o_ref.dtype)

    return pl.pallas_call(
        body_whole if nk == 1 else body, name=name, grid=(M // tm, N // tn, nk),
        in_specs=[a_spec, b_spec],
        out_specs=pl.BlockSpec((tm, tn), lambda i, j, k: (i, j)),
        out_shape=jax.ShapeDtypeStruct((M, N), out_dtype),
        scratch_shapes=[] if nk == 1 else [pltpu.VMEM((tm, tn), F32)],
        compiler_params=_cparams(("parallel", "parallel", "arbitrary")),
    )(a, b)


def _rowwise(name, fn, S, ts, tiles, params=(), outs=(), accs=(), halo=0, ncb=1):
    in_specs, args, scratch = [], [], []
    for arr, w, jm, with_halo in tiles:
        if isinstance(with_halo, int) and with_halo > 1:
            d = with_halo
            in_specs.append(pl.BlockSpec((ts // d, d * w), lambda j, i: (i, 0)))
            args.append(arr)
            scratch.append(pltpu.VMEM((w // LANES, ts, LANES), F32))
        elif with_halo:
            hb, nhb = ts // halo, S // halo
            in_specs += [
                pl.BlockSpec((halo, w), lambda j, i, jm=jm, hb=hb: (jnp.maximum(i * hb - 1, 0), jm(j))),
                pl.BlockSpec((ts, w), lambda j, i, jm=jm: (i, jm(j))),
                pl.BlockSpec((halo, w), lambda j, i, jm=jm, hb=hb, nhb=nhb: (jnp.minimum((i + 1) * hb, nhb - 1), jm(j))),
            ]
            args += [arr, arr, arr]
        else:
            in_specs.append(pl.BlockSpec((ts, w), lambda j, i, jm=jm: (i, jm(j))))
            args.append(arr)
    for arr, r, w, jm in params:
        in_specs.append(pl.BlockSpec((r, w), lambda j, i, jm=jm: (0, jm(j))))
        args.append(arr)
    out_specs, out_shape = [], []
    for w, dt, jm, tw, *dil in outs:
        if dil:
            out_specs.append(pl.BlockSpec((ts // dil[0], dil[0] * w), lambda j, i: (i, 0)))
            out_shape.append(jax.ShapeDtypeStruct((S // dil[0], dil[0] * w), dt))
            scratch += [pltpu.VMEM((ts, w), F32), pltpu.VMEM((w // LANES, ts, LANES), F32)]
        else:
            out_specs.append(pl.BlockSpec((ts, w), lambda j, i, jm=jm: (i, jm(j))))
            out_shape.append(jax.ShapeDtypeStruct((S, tw), dt))
    for r, w, jm, tw in accs:
        out_specs.append(pl.BlockSpec((r, w), lambda j, i, jm=jm: (0, jm(j))))
        out_shape.append(jax.ShapeDtypeStruct((r, tw), F32))
    n_tiles, n_params, n_outs, n_accs = len(tiles), len(params), len(outs), len(accs)

    def residue_rows(r, d):
        return pl.ds(r, ts // d, stride=d)

    def body(*refs):
        i = pl.program_id(1)
        n_io = len(in_specs) + n_outs + n_accs
        scr = list(refs[n_io:])
        refs = refs[:n_io]
        pos, vals = 0, []
        for _, w, _, with_halo in tiles:
            if isinstance(with_halo, int) and with_halo > 1:
                d, buf = with_halo, scr.pop(0)
                for r in range(d):
                    for c in range(w // LANES):
                        buf[c, residue_rows(r, d), :] = refs[pos][:, r * w + c * LANES:r * w + (c + 1) * LANES].astype(F32)
                vals.append(jnp.concatenate([buf[c] for c in range(w // LANES)], axis=1))
                pos += 1
            elif with_halo:
                before, after = refs[pos][...], refs[pos + 2][...]
                before = jnp.where(i > 0, before, jnp.zeros_like(before))
                after = jnp.where(i < S // ts - 1, after, jnp.zeros_like(after))
                vals.append(jnp.concatenate([before, refs[pos + 1][...], after], axis=0))
                pos += 3
            else:
                vals.append(refs[pos][...])
                pos += 1
        prefs = refs[pos:pos + n_params]
        orefs = list(refs[pos + n_params:pos + n_params + n_outs])
        arefs = refs[pos + n_params + n_outs:]
        staged = []
        for k, (w, _, _, _, *dil) in enumerate(outs):
            if dil:
                staged.append((orefs[k], scr.pop(0), scr.pop(0), w, dil[0]))
                orefs[k] = staged[-1][1]

        @pl.when(i == 0)
        def _():
            for r in arefs:
                r[...] = jnp.zeros_like(r)

        fn(i, vals, prefs, orefs, arefs)
        for out_ref, flat, buf, w, d in staged:
            for c in range(w // LANES):
                buf[c] = flat[:, c * LANES:(c + 1) * LANES]
                for r in range(d):
                    out_ref[:, r * w + c * LANES:r * w + (c + 1) * LANES] = buf[c, residue_rows(r, d), :].astype(out_ref.dtype)

    res = pl.pallas_call(
        body, name=name, grid=(ncb, S // ts),
        in_specs=in_specs, out_specs=out_specs, out_shape=out_shape, scratch_shapes=scratch,
        compiler_params=_cparams(("arbitrary", "arbitrary")),
    )(*args)
    return res


def _vmem_call(name, fn, ins, out_shapes):
    n_in = len(ins)

    def body(*refs):
        vals = fn(*[r[...] for r in refs[:n_in]])
        for r, v in zip(refs[n_in:], vals):
            r[...] = v.astype(r.dtype)

    return pl.pallas_call(
        body, name=name,
        out_shape=[jax.ShapeDtypeStruct(s, dt) for s, dt in out_shapes],
        compiler_params=_cparams(),
    )(*ins)


def _rms(x, w):
    return x * lax.rsqrt(jnp.mean(x * x, axis=-1, keepdims=True) + EPS) * w


def _normmod_f(x, nw, sc, sh):
    return _rms(x, nw) * (1.0 + sc) + sh


def _row_params(*vecs):
    return [(v, 1, v.shape[1], _c0) for v in vecs]


def _normmod_fwd(x, nw, sc, sh):
    S = x.shape[0]

    def fn(i, vals, p, o, a):
        o[0][...] = _normmod_f(vals[0], p[0][...], p[1][...], p[2][...]).astype(BF16)

    return _rowwise("normmod_fwd", fn, S, 512, [(x, D, _c0, False)], _row_params(nw, sc, sh), [(D, BF16, _c0, D)])[0]


def _normmod_bwd(x, gh, gres, nw, sc, sh):
    S = x.shape[0]

    def fn(i, vals, p, o, a):
        _, vjp = jax.vjp(_normmod_f, vals[0], p[0][...], p[1][...], p[2][...])
        gx, gnw, gsc, gsh = vjp(vals[1])
        o[0][...] = gx + vals[2]
        a[0][...] += gnw
        a[1][...] += gsc
        a[2][...] += gsh

    return _rowwise("normmod_bwd", fn, S, 512, [(x, D, _c0, False), (gh, D, _c0, False), (gres, D, _c0, False)],
                    _row_params(nw, sc, sh), [(D, F32, _c0, D)], [(1, D, _c0, D)] * 3)


def _gate_add(x, y, g):
    S = x.shape[0]

    def fn(i, vals, p, o, a):
        o[0][...] = vals[0] + p[0][...] * vals[1]

    return _rowwise("gate_add", fn, S, 512, [(x, D, _c0, False), (y, D, _c0, False)], _row_params(g), [(D, F32, _c0, D)])[0]


def _gate_bwd(gx, y, g):
    S = gx.shape[0]

    def fn(i, vals, p, o, a):
        o[0][...] = (vals[0] * p[0][...]).astype(BF16)
        a[0][...] += jnp.sum(vals[0] * vals[1], axis=0, keepdims=True)

    return _rowwise("gate_bwd", fn, S, 512, [(gx, D, _c0, False), (y, D, _c0, False)], _row_params(g),
                    [(D, BF16, _c0, D)], [(1, D, _c0, D)])


def _loss_head(x, tgt, fw):
    S = x.shape[0]

    def fn(i, vals, p, o, a):
        y, vjp = jax.vjp(_rms, vals[0], p[0][...])
        err = y - vals[1]
        gx, gfw = vjp(err * (1.0 / D))
        o[0][...] = gx
        a[0][...] += gfw
        part = 0.5 * jnp.sum(jnp.mean(err * err, axis=-1, keepdims=True), axis=0, keepdims=True)
        a[1][...] += jnp.broadcast_to(part, (1, LANES))

    return _rowwise("loss_head", fn, S, 256, [(x, D, _c0, False), (tgt, D, _c0, False)], _row_params(fw),
                    [(D, F32, _c0, D)], [(1, D, _c0, D), (1, LANES, _c0, LANES)])


CONV_HALO = 16
CONV_TS = 512


def _shifted(ext, shift, ts, halo):
    n = ext.shape[0]
    s = shift % n
    r = ext if s == 0 else pltpu.roll(ext, s, 0)
    return r[halo:halo + ts]


def _ln_silu(a, w, b):
    mu = jnp.mean(a, axis=-1, keepdims=True)
    var = jnp.mean(jnp.square(a - mu), axis=-1, keepdims=True)
    y = (a - mu) * lax.rsqrt(var + EPS) * w + b
    return y * jax.nn.sigmoid(y)


def _conv_a_fwd(proj, w_pad, b, lnw, lnb):
    S = proj.shape[0]
    ts, H = min(CONV_TS, S), CONV_HALO

    def fn(i, vals, p, o, a):
        a0 = vals[0] * jax.nn.sigmoid(vals[1])
        acc = jnp.zeros((ts, CONV_CH), F32) + p[1][...]
        for k in range(CONV_W):
            acc = acc + _shifted(a0, CONV_W // 2 - k, ts, H) * p[0][pl.ds(k, 1), :]
        o[0][...] = acc
        o[1][...] = _ln_silu(acc, p[2][...], p[3][...]).astype(BF16)

    tiles = [(proj, CONV_CH, lambda j: AV_BLK, True), (proj, CONV_CH, lambda j: AG_BLK, True)]
    params = [(w_pad, CONV_W_PAD, CONV_CH, _c0)] + _row_params(b, lnw, lnb)
    return _rowwise("conv_a_fwd", fn, S, ts, tiles, params, [(CONV_CH, F32, _c0, CONV_CH), (CONV_CH, BF16, _c0, CONV_CH)], halo=H)


def _conv_a_bwd(proj, a1, gmixed, w_pad, lnw, lnb):
    S = proj.shape[0]
    ts, H = min(CONV_TS, S), CONV_HALO

    def fn(i, vals, p, o, a):
        av, ag, a1e, ge = vals
        lw, lb = p[1][...], p[2][...]
        _, vjp_e = jax.vjp(lambda t: _ln_silu(t, lw, lb), a1e)
        (ga1e,) = vjp_e(ge)
        c = slice(H, H + ts)
        _, vjp_c = jax.vjp(_ln_silu, a1e[c], lw, lb)
        ga1, glw, glb = vjp_c(ge[c])
        a[1][...] += jnp.sum(ga1, axis=0, keepdims=True)
        a[2][...] += glw
        a[3][...] += glb
        sg = jax.nn.sigmoid(ag)
        a0 = av * sg
        ga0 = jnp.zeros((ts, CONV_CH), F32)
        for k in range(CONV_W):
            a[0][pl.ds(k, 1), :] += jnp.sum(ga1 * _shifted(a0, CONV_W // 2 - k, ts, H), axis=0, keepdims=True)
            ga0 = ga0 + _shifted(ga1e, k - CONV_W // 2, ts, H) * p[0][pl.ds(k, 1), :]
        sgc, avc = sg[c], av[c]
        o[0][...] = (ga0 * sgc).astype(BF16)
        o[1][...] = (ga0 * avc * sgc * (1.0 - sgc)).astype(BF16)

    tiles = [(proj, CONV_CH, lambda j: AV_BLK, True), (proj, CONV_CH, lambda j: AG_BLK, True),
             (a1, CONV_CH, _c0, True), (gmixed, CONV_CH, lambda j: 3, True)]
    params = [(w_pad, CONV_W_PAD, CONV_CH, _c0)] + _row_params(lnw, lnb)
    outs = [(CONV_CH, BF16, _c0, CONV_CH), (CONV_CH, BF16, _c0, CONV_CH)]
    accs = [(CONV_W_PAD, CONV_CH, _c0, CONV_CH)] + [(1, CONV_CH, _c0, CONV_CH)] * 3
    return _rowwise("conv_a_bwd", fn, S, ts, tiles, params, outs, accs, halo=H)


FFN_HALO = 8
FFN_TS = 512
FFN_TS_FWD = 1024
FFN_CB = 256
FFN_NCB = D_FF // FFN_CB


def _gelu_mul(g, v):
    return 0.5 * g * (1.0 + lax.erf(g * (2.0 ** -0.5))) * v


def _ffn_mid_fwd(u, cw):
    S = u.shape[0]
    ts, H = min(FFN_TS_FWD, S), FFN_HALO

    def conv(ext, w_ref):
        acc = jnp.zeros((ts, FFN_CB), F32)
        for k in range(FFN_CONV_W):
            acc = acc + _shifted(ext, 1 - k, ts, H) * w_ref[pl.ds(k, 1), :]
        return acc

    def fn(i, vals, p, o, a):
        o[0][...] = _gelu_mul(conv(vals[0], p[0]), conv(vals[1], p[1])).astype(BF16)

    gate, val = (lambda j: j), (lambda j: j + FFN_NCB)
    return _rowwise("ffn_mid_fwd", fn, S, ts, [(u, FFN_CB, gate, True), (u, FFN_CB, val, True)],
                    [(cw, FFN_CONV_W, FFN_CB, gate), (cw, FFN_CONV_W, FFN_CB, val)],
                    [(FFN_CB, BF16, gate, D_FF)], halo=H, ncb=FFN_NCB)[0]


def _ffn_mid_bwd(u, gact, cw):
    S = u.shape[0]
    ts, H = min(FFN_TS, S), FFN_HALO
    n = ts + 2 * H

    def fn(i, vals, p, o, a):
        ug, uv, ga = vals

        def conv_all(ue, w_ref):
            acc = jnp.zeros((n, FFN_CB), F32)
            for k in range(FFN_CONV_W):
                s = (1 - k) % n
                acc = acc + (ue if s == 0 else pltpu.roll(ue, s, 0)) * w_ref[pl.ds(k, 1), :]
            return acc

        _, vjp = jax.vjp(_gelu_mul, conv_all(ug, p[0]), conv_all(uv, p[1]))
        for half, (gc, ue) in enumerate(zip(vjp(ga), (ug, uv))):
            gu = jnp.zeros((ts, FFN_CB), F32)
            for k in range(FFN_CONV_W):
                gu = gu + _shifted(gc, k - 1, ts, H) * p[half][pl.ds(k, 1), :]
                a[half][pl.ds(k, 1), :] += jnp.sum(gc[H:H + ts] * _shifted(ue, 1 - k, ts, H), axis=0, keepdims=True)
            o[half][...] = gu.astype(BF16)

    gate, val = (lambda j: j), (lambda j: j + FFN_NCB)
    tiles = [(u, FFN_CB, gate, True), (u, FFN_CB, val, True), (gact, FFN_CB, gate, True)]
    params = [(cw, FFN_CONV_W, FFN_CB, gate), (cw, FFN_CONV_W, FFN_CB, val)]
    gu_gate, gu_val, gw_gate, gw_val = _rowwise("ffn_mid_bwd", fn, S, ts, tiles, params, [(FFN_CB, BF16, gate, D_FF)] * 2,
                                                [(FFN_CONV_W, FFN_CB, gate, D_FF)] * 2, halo=H, ncb=FFN_NCB)
    return jnp.concatenate([gu_gate, gu_val], axis=1), jnp.concatenate([gw_gate, gw_val], axis=1)


LD_W = LANES
PAIR_W = 2 * HEAD
N_PAIRS = N_HEADS // 2


def _sub_view(t, d):
    S, C = t.shape
    return t.reshape(S // d, d * C)


def _sub_halo_specs(width, col, blk, hb, nhb):
    per = blk // hb
    return [
        pl.BlockSpec((hb, width), lambda r, i: (jnp.maximum(i * per - 1, 0), col(r))),
        pl.BlockSpec((blk, width), lambda r, i: (i, col(r))),
        pl.BlockSpec((hb, width), lambda r, i: (jnp.minimum((i + 1) * per, nhb - 1), col(r))),
    ]


def _pick_lane(t, lane):
    return jnp.sum(jnp.where(_iota((1, t.shape[1]), 1) == lane, t, 0.0), axis=1, keepdims=True)


def _pair_mask(h2):
    return (_iota((1, PAIR_W), 1) >> HEAD_SHIFT) == h2


def _cat_bf16(a, b, c):
    return jnp.concatenate([a[...], b[...], c[...]], axis=0).astype(BF16)


def _attn_fwd(view, dil):
    L = view.shape[0]
    blk, hb = min(ATT_BLK, L), HALF_BAND
    span = blk + 2 * hb

    def body(q_ref, kp, kc, kn, vp, vc, vn, o_ref, l_ref):
        i = pl.program_id(1)
        rel = _iota((blk, span), 1) - hb - _iota((blk, span), 0)
        kpos = i * blk - hb + _iota((blk, span), 1)
        valid = (jnp.abs(rel) <= hb) & (kpos >= 0) & (kpos < L)
        dist = jnp.abs(rel).astype(F32) * float(dil)
        q, k, v = q_ref[...].astype(BF16), _cat_bf16(kp, kc, kn), _cat_bf16(vp, vc, vn)
        lse = jnp.zeros((blk, LD_W), F32)
        for pr in range(N_PAIRS):
            sl = slice(pr * PAIR_W, (pr + 1) * PAIR_W)
            qp, kpair, vpair = q[:, sl], k[:, sl], v[:, sl]
            o = jnp.zeros((blk, PAIR_W), F32)
            for h2 in range(2):
                h, mask = 2 * pr + h2, _pair_mask(h2)
                s = _dot_nt(jnp.where(mask, qp, jnp.zeros_like(qp)), kpair) * (HEAD ** -0.5) - ALIBI_SLOPES[h] * dist
                s = jnp.where(valid, s, MASK_VALUE)
                m = jnp.max(s, axis=1, keepdims=True)
                p = jnp.exp(s - m)
                l = jnp.sum(p, axis=1, keepdims=True)
                o = jnp.where(mask, _dot_nn(p.astype(BF16), vpair) / l, o)
                lse = lse + jnp.where(_iota((1, LD_W), 1) == h, m + jnp.log(l), 0.0)
            o_ref[:, sl] = o
        l_ref[...] = lse

    nhb = L // hb
    in_specs = ([pl.BlockSpec((blk, ATT_W), lambda r, i: (i, r * QKV_BLOCKS +QA))]
                + _sub_halo_specs(ATT_W, lambda r: r * QKV_BLOCKS +KA, blk, hb, nhb)
                + _sub_halo_specs(ATT_W, lambda r: r * QKV_BLOCKS +VA, blk, hb, nhb))
    return pl.pallas_call(
        body, name=f"attn_fwd_d{dil}", grid=(dil, L // blk), in_specs=in_specs,
        out_specs=[pl.BlockSpec((blk, ATT_W), lambda r, i: (i, r)), pl.BlockSpec((blk, LD_W), lambda r, i: (i, r))],
        out_shape=[jax.ShapeDtypeStruct((L, dil * ATT_W), F32), jax.ShapeDtypeStruct((L, dil * LD_W), F32)],
        compiler_params=_cparams(("parallel", "parallel")),
    )(*([view] * 7))


def _attn_bwd(pview, gview, lview, dil):
    L = pview.shape[0]
    blk, hb = min(ATT_BLK, L), HALF_BAND
    span = blk + 2 * hb
    scale = HEAD ** -0.5

    def body(qp, qc, qn, kp, kc, kn, vp, vc, vn, gp, gc, gn, lp, lc, ln, dq_ref, dk_ref, dv_ref):
        i = pl.program_id(1)
        l = lc[...]
        le = jnp.concatenate([lp[...], l, ln[...]], axis=0)
        rel_q = _iota((blk, span), 1) - hb - _iota((blk, span), 0)
        kpos = i * blk - hb + _iota((blk, span), 1)
        valid_q = (jnp.abs(rel_q) <= hb) & (kpos >= 0) & (kpos < L)
        dist_q = jnp.abs(rel_q).astype(F32) * float(dil)
        rel_k = _iota((span, blk), 1) + hb - _iota((span, blk), 0)
        qpos = i * blk - hb + _iota((span, blk), 0)
        valid_k = (jnp.abs(rel_k) <= hb) & (qpos >= 0) & (qpos < L)
        dist_k = jnp.abs(rel_k).astype(F32) * float(dil)
        q_all, k_all, v_all, g_all = qc[...].astype(BF16), kc[...].astype(BF16), vc[...].astype(BF16), gc[...]
        qe_all, ke_all, ve_all = _cat_bf16(qp, qc, qn), _cat_bf16(kp, kc, kn), _cat_bf16(vp, vc, vn)
        ge_all = _cat_bf16(gp, gc, gn)
        for pr in range(N_PAIRS):
            sl = slice(pr * PAIR_W, (pr + 1) * PAIR_W)
            q, k, v, g = q_all[:, sl], k_all[:, sl], v_all[:, sl], g_all[:, sl]
            qe, ke, ve, ge = qe_all[:, sl], ke_all[:, sl], ve_all[:, sl], ge_all[:, sl]
            dq = jnp.zeros((blk, PAIR_W), F32)
            dk = jnp.zeros((blk, PAIR_W), F32)
            dv = jnp.zeros((blk, PAIR_W), F32)
            for h2 in range(2):
                h, mask = 2 * pr + h2, _pair_mask(h2)
                only = lambda t: jnp.where(mask, t, jnp.zeros_like(t))
                s = _dot_nt(only(q), ke) * scale - ALIBI_SLOPES[h] * dist_q
                p = jnp.where(valid_q, jnp.exp(s - _pick_lane(l, h)), 0.0)
                ds = p * (_dot_nt(only(g), ve) - _pick_lane(l, 8 + h))
                dq = jnp.where(mask, _dot_nn(ds.astype(BF16), ke), dq)
                s = _dot_nt(only(qe), k) * scale - ALIBI_SLOPES[h] * dist_k
                p = jnp.where(valid_k, jnp.exp(s - _pick_lane(le, h)), 0.0)
                dv = jnp.where(mask, _dot_tn(p.astype(BF16), ge), dv)
                ds = p * (_dot_nt(only(ge), v) - _pick_lane(le, 8 + h))
                dk = jnp.where(mask, _dot_tn(ds.astype(BF16), qe), dk)
            dq_ref[:, sl] = (dq * scale).astype(BF16)
            dk_ref[:, sl] = (dk * scale).astype(BF16)
            dv_ref[:, sl] = dv.astype(BF16)

    nhb = L // hb
    in_specs = (_sub_halo_specs(ATT_W, lambda r: r * QKV_BLOCKS +QA, blk, hb, nhb)
                + _sub_halo_specs(ATT_W, lambda r: r * QKV_BLOCKS +KA, blk, hb, nhb)
                + _sub_halo_specs(ATT_W, lambda r: r * QKV_BLOCKS +VA, blk, hb, nhb)
                + _sub_halo_specs(ATT_W, lambda r: r, blk, hb, nhb) + _sub_halo_specs(LD_W, lambda r: r, blk, hb, nhb))
    o_spec = pl.BlockSpec((blk, ATT_W), lambda r, i: (i, r))
    return pl.pallas_call(
        body, name=f"attn_bwd_d{dil}", grid=(dil, L // blk), in_specs=in_specs,
        out_specs=[o_spec] * 3, out_shape=[jax.ShapeDtypeStruct((L, dil * ATT_W), BF16)] * 3,
        compiler_params=_cparams(("parallel", "parallel")),
    )(*([pview] * 9 + [gview] * 3 + [lview] * 3))


def _head_expand(t):
    e = ((_iota((LD_W, ATT_W), 1) >> HEAD_SHIFT) == _iota((LD_W, ATT_W), 0)).astype(F32)
    return _dot_nn(t, e, HP)


def _dil(d):
    return d if d > 1 else False


def _qkv_views(proj):
    S, w = proj.shape[0], QKV_BLOCKS * ATT_W

    def fn(i, vals, p, o, a):
        for k, d in enumerate(DILATIONS):
            o[k][...] = vals[0].astype(o[k].dtype)

    outs = [(w, BF16, _c0, w) + ((d,) if d > 1 else ()) for d in DILATIONS]
    return _rowwise("qkv_views", fn, S, 512, [(proj, w, _c0, False)], (), outs)


def _attn_merge(os, ls):
    S = os[0].shape[0]

    def fn(i, vals, p, o, a):
        o3, l3 = vals[:3], vals[3:]
        m = jnp.maximum(jnp.maximum(l3[0], l3[1]), l3[2])
        e3 = [jnp.exp(l - m) for l in l3]
        den = e3[0] + e3[1] + e3[2]
        out = jnp.zeros((o3[0].shape[0], ATT_W), F32)
        for ob, e in zip(o3, e3):
            out = out + _head_expand(e / den) * ob
        o[0][...] = out
        o[1][...] = m + jnp.log(den)

    tiles = ([(t, ATT_W, _c0, _dil(d)) for t, d in zip(os, DILATIONS)]
             + [(t, LD_W, _c0, _dil(d)) for t, d in zip(ls, DILATIONS)])
    return _rowwise("attn_merge", fn, S, 512, tiles, (), [(ATT_W, F32, _c0, ATT_W), (LD_W, F32, _c0, LD_W)])


def _attn_bwd_prep(gmixed, att, lse):
    S = att.shape[0]
    n = len(DILATIONS)

    def fn(i, vals, p, o, a):
        g, out, lse_row = vals
        place_d = ((_iota((ATT_W, LD_W), 0) >> HEAD_SHIFT) + 8 == _iota((ATT_W, LD_W), 1)).astype(F32)
        ld = jnp.where(_iota((1, LD_W), 1) < 8, lse_row, 0.0) + _dot_nn(g * out, place_d, HP)
        for k in range(n):
            o[k][...] = g.astype(o[k].dtype)
            o[n + k][...] = ld

    tiles = [(gmixed, ATT_W, _c0, False), (att, ATT_W, _c0, False), (lse, LD_W, _c0, False)]
    outs = ([(ATT_W, BF16, _c0, ATT_W) + ((d,) if d > 1 else ()) for d in DILATIONS]
            + [(LD_W, F32, _c0, LD_W) + ((d,) if d > 1 else ()) for d in DILATIONS])
    res = _rowwise("attn_bwd_prep", fn, S, 512, tiles, (), outs)
    return res[:n], res[n:]


def _sum3_bf16(views, S, width):
    def fn(i, vals, p, o, a):
        o[0][...] = (vals[0].astype(F32) + vals[1].astype(F32) + vals[2].astype(F32)).astype(BF16)

    tiles = [(t, width, _c0, _dil(d)) for t, d in zip(views, DILATIONS)]
    return _rowwise("sum3", fn, S, 512, tiles, (), [(width, BF16, _c0, width)])[0]


def _block_diag_mask():
    return ((_iota((REC_W, REC_W), 0) >> HEAD_SHIFT) == (_iota((REC_W, REC_W), 1) >> HEAD_SHIFT)).astype(F32)


def _hgrn_chunk(qr, z, iv, lb, st, reverse, precise):
    C = REC_CHUNK
    r, c = _iota((C, C), 0), _iota((C, C), 1)
    t_cum = (c >= r) if reverse else (c <= r)
    mid_row, last_row = (C // 2, 0) if reverse else (C // 2 - 1, C - 1)
    f = lb + (1.0 - lb) * jax.nn.sigmoid(z)
    logf = jnp.log(jnp.maximum(f, F_TINY))
    k = (1.0 - lb) * jax.nn.sigmoid(-z)
    q = qr * jax.nn.sigmoid(qr)
    b = _dot_nn(t_cum.astype(F32), logf, HP)
    row = _iota((C, 1), 0)
    bm = jnp.sum(jnp.where(row == mid_row, b, 0.0), axis=0, keepdims=True)
    bl = jnp.sum(jnp.where(row == last_row, b, 0.0), axis=0, keepdims=True)
    qt = q * jnp.exp(jnp.minimum(b - bm, EXP_CLAMP))
    kt = k * jnp.exp(jnp.minimum(bm - b, EXP_CLAMP))
    qh = q * jnp.exp(b)
    kh = k * jnp.exp(bl - b)
    lam = jnp.exp(bl)
    bd = ((_iota((PAIR_W, PAIR_W), 0) >> HEAD_SHIFT) == (_iota((PAIR_W, PAIR_W), 1) >> HEAD_SHIFT)).astype(F32)
    s_in = _iota((C, PAIR_W), 1) & (HEAD - 1)
    t_in = _iota((C, PAIR_W), 0)
    tri = (s_in >= t_in) if reverse else (s_in <= t_in)
    twice = lambda t: jnp.concatenate([t, t], axis=0)
    outs, states = [], []
    for pr in range(N_PAIRS):
        sl = slice(pr * PAIR_W, (pr + 1) * PAIR_W)
        k_bd = twice(kt[:, sl]) * bd
        v_bd = (twice(iv[:, sl]) * bd).astype(BF16)
        st_bd = twice(st[:, sl]) * bd
        if precise:
            scores = _dot_nt(qt[:, sl], k_bd, lax.Precision.HIGH)
        else:
            scores = _dot_nt(qt[:, sl].astype(BF16), k_bd.astype(BF16))
        a = jnp.where(tri, scores, 0.0)
        outs.append(_dot_nn(a.astype(BF16), v_bd) + _dot_nt(qh[:, sl].astype(BF16), st_bd.astype(BF16)))
        kv = _dot_tn(iv[:, sl].astype(BF16), kh[:, sl].astype(BF16))
        st_bd = st_bd * lam[:, sl] + kv * bd
        states.append(st_bd[0:HEAD] + st_bd[HEAD:PAIR_W])
    return jnp.concatenate(outs, axis=1), jnp.concatenate(states, axis=1)


REC_CHUNKS_PER_STEP = 8
REC_ROWS = REC_CHUNKS_PER_STEP * REC_CHUNK


def _hgrn_specs(order, blocks):
    return [pl.BlockSpec((REC_ROWS, REC_W), lambda i, b=b: (order(i), b)) for b in blocks]


def _chunk_rows(j):
    return pl.ds(pl.multiple_of(j * REC_CHUNK, REC_CHUNK), REC_CHUNK)


def _hgrn_fwd(proj, lb, z_blk, reverse):
    S = proj.shape[0]
    nb = S // REC_ROWS
    order = (lambda i: nb - 1 - i) if reverse else (lambda i: i)

    def body(q_ref, z_ref, v_ref, lb_ref, o_ref, st_ref, st_scr):
        @pl.when(pl.program_id(0) == 0)
        def _():
            st_scr[...] = jnp.zeros_like(st_scr)

        def step(t, carry):
            j = REC_CHUNKS_PER_STEP - 1 - t if reverse else t
            rows = _chunk_rows(j)
            st = st_scr[...]
            st_ref[j] = st
            o, st_new = _hgrn_chunk(q_ref[rows, :], z_ref[rows, :], v_ref[rows, :], lb_ref[...], st, reverse, False)
            o_ref[rows, :] = o
            st_scr[...] = st_new
            return carry

        lax.fori_loop(0, REC_CHUNKS_PER_STEP, step, 0, unroll=2)

    return pl.pallas_call(
        body, name="hgrn_rev_fwd" if reverse else "hgrn_fwd_fwd", grid=(nb,),
        in_specs=_hgrn_specs(order, (QR, z_blk, IR)) + [pl.BlockSpec((1, REC_W), lambda i: (0, 0))],
        out_specs=[pl.BlockSpec((REC_ROWS, REC_W), lambda i: (order(i), 0)),
                   pl.BlockSpec((REC_CHUNKS_PER_STEP, HEAD, REC_W), lambda i: (order(i), 0, 0))],
        out_shape=[jax.ShapeDtypeStruct((S, REC_W), F32), jax.ShapeDtypeStruct((S // REC_CHUNK, HEAD, REC_W), F32)],
        scratch_shapes=[pltpu.VMEM((HEAD, REC_W), F32)],
        compiler_params=_cparams(("arbitrary",)),
    )(proj, proj, proj, lb)


def _hgrn_bwd(proj, lb, states, go, z_blk, reverse):
    S = proj.shape[0]
    nb = S // REC_ROWS
    order = (lambda i: i) if reverse else (lambda i: nb - 1 - i)

    def body(q_ref, z_ref, v_ref, lb_ref, st_ref, go_ref, gq_ref, gz_ref, gv_ref, glb_ref, gst_scr):
        @pl.when(pl.program_id(0) == 0)
        def _():
            gst_scr[...] = jnp.zeros_like(gst_scr)
            glb_ref[...] = jnp.zeros_like(glb_ref)

        chunk = functools.partial(_hgrn_chunk, reverse=reverse, precise=True)

        def step(t, carry):
            j = t if reverse else REC_CHUNKS_PER_STEP - 1 - t
            rows = _chunk_rows(j)
            _, vjp = jax.vjp(chunk, q_ref[rows, :], z_ref[rows, :], v_ref[rows, :], lb_ref[...], st_ref[j])
            gq, gz, gv, glb, gst = vjp((go_ref[rows, :], gst_scr[...]))
            gq_ref[rows, :] = gq
            gz_ref[rows, :] = gz
            gv_ref[rows, :] = gv
            glb_ref[...] += glb
            gst_scr[...] = gst
            return carry

        lax.fori_loop(0, REC_CHUNKS_PER_STEP, step, 0, unroll=2)

    row_spec = pl.BlockSpec((REC_ROWS, REC_W), lambda i: (order(i), 0))
    return pl.pallas_call(
        body, name="hgrn_rev_bwd" if reverse else "hgrn_fwd_bwd", grid=(nb,),
        in_specs=(_hgrn_specs(order, (QR, z_blk, IR)) + [pl.BlockSpec((1, REC_W), lambda i: (0, 0))]
                  + [pl.BlockSpec((REC_CHUNKS_PER_STEP, HEAD, REC_W), lambda i: (order(i), 0, 0)), row_spec]),
        out_specs=[row_spec] * 3 + [pl.BlockSpec((1, REC_W), lambda i: (0, 0))],
        out_shape=[jax.ShapeDtypeStruct((S, REC_W), F32)] * 3 + [jax.ShapeDtypeStruct((1, REC_W), F32)],
        scratch_shapes=[pltpu.VMEM((HEAD, REC_W), F32)],
        compiler_params=_cparams(("arbitrary",)),
    )(proj, proj, proj, lb, states, go)


def _hgrn_post_f(of, ob, gr, rnw):
    o = of + ob
    ms = _dot_nn(o * o, _block_diag_mask() * (1.0 / HEAD), HP)
    return o * lax.rsqrt(ms + EPS) * rnw * (gr * jax.nn.sigmoid(gr))


def _hgrn_post_fwd(of, ob, proj, rnw):
    S = of.shape[0]

    def fn(i, vals, p, o, a):
        o[0][...] = _hgrn_post_f(vals[0], vals[1], vals[2], p[0][...]).astype(BF16)

    tiles = [(of, REC_W, _c0, False), (ob, REC_W, _c0, False), (proj, REC_W, lambda j: GR, False)]
    return _rowwise("hgrn_post_fwd", fn, S, 512, tiles, _row_params(rnw), [(REC_W, BF16, _c0, REC_W)])[0]


def _hgrn_post_bwd(of, ob, proj, gmixed, rnw):
    S = of.shape[0]

    def fn(i, vals, p, o, a):
        _, vjp = jax.vjp(_hgrn_post_f, vals[0], vals[1], vals[2], p[0][...])
        go, _, ggr, grnw = vjp(vals[3])
        o[0][...] = go
        o[1][...] = ggr
        a[0][...] += grnw

    tiles = [(of, REC_W, _c0, False), (ob, REC_W, _c0, False), (proj, REC_W, lambda j: GR, False),
             (gmixed, REC_W, lambda j: 1, False)]
    return _rowwise("hgrn_post_bwd", fn, S, 512, tiles, _row_params(rnw),
                    [(REC_W, F32, _c0, REC_W), (REC_W, F32, _c0, REC_W)], [(1, REC_W, _c0, REC_W)])


def _lower_bounds_f(g0, g1):
    m = jnp.maximum(g0, g1)
    e0, e1 = jnp.exp(g0 - m), jnp.exp(g1 - m)
    return e1 / (e0 + e1)


def _adamw(name, w, m, v, gparts):
    R, C = w.shape
    P = gparts.shape[0]
    tr = R if R * C * 4 * (P + 7) * 2 <= VMEM_LIMIT_BYTES // 2 else _pick(R, (256, 128, 64, 32, 16, 8))

    def body(w_ref, m_ref, v_ref, gp_ref, g_ref, d_ref, nm_ref, nv_ref):
        g = gp_ref[0].astype(F32)
        for p in range(1, P):
            g = g + gp_ref[p].astype(F32)
        w_ = w_ref[...]
        nm = ADAM_B1 * m_ref[...] + (1.0 - ADAM_B1) * g
        nv = ADAM_B2 * v_ref[...] + (1.0 - ADAM_B2) * jnp.square(g)
        m_hat = nm / (1.0 - ADAM_B1 ** ADAM_STEP)
        v_hat = nv / (1.0 - ADAM_B2 ** ADAM_STEP)
        g_ref[...] = g
        d_ref[...] = -ADAM_LR * (m_hat / (jnp.sqrt(v_hat) + ADAM_EPS) + ADAM_WD * w_)
        nm_ref[...] = nm
        nv_ref[...] = nv

    spec = pl.BlockSpec((tr, C), lambda i: (i, 0))
    return pl.pallas_call(
        body, name=name, grid=(R // tr,),
        in_specs=[spec, spec, spec, pl.BlockSpec((P, tr, C), lambda i: (0, i, 0))],
        out_specs=[spec] * 4, out_shape=[jax.ShapeDtypeStruct((R, C), F32)] * 4,
        compiler_params=_cparams(("parallel",)),
    )(w, m, v, gparts)


def _place():
    return lax.axis_index("x"), lax.axis_index("y"), lax.axis_index("c")


def _index_of(p):
    return 4 * p[0] + 2 * p[1] + p[2]


def _allgather_small(name, rows):
    m_per, n = rows.shape

    def body(x_ref, out_ref, send_sems, recv_sems, local_sem):
        x, y, c = _place()
        me, sibling = (x, y, c), (x, y, 1 - c)
        chips = [(1 - x, y), (x, 1 - y), (1 - x, 1 - y)]

        def blk(p):
            return out_ref.at[pl.ds(_index_of(p) * m_per, m_per), :]

        def copy(k, block, to, src=None):
            return pltpu.make_async_remote_copy(
                src_ref=blk(block) if src is None else src, dst_ref=blk(block),
                send_sem=send_sems.at[k], recv_sem=recv_sems.at[k], device_id=to, device_id_type=MESH)

        mine = pltpu.make_async_copy(x_ref, blk(me), local_sem)
        mine.start()
        first = [copy(0, me, sibling, src=x_ref)]
        first += [copy(1 + j, me, (*chip, c), src=x_ref) for j, chip in enumerate(chips)]
        for cp in first:
            cp.start()
        passed = [copy(4 + j, (*chip, c), sibling) for j, chip in enumerate(chips)]
        for j, chip in enumerate(chips):
            copy(1 + j, (*chip, c), me).wait_recv()
            passed[j].start()
        copy(0, sibling, me).wait_recv()
        for j, chip in enumerate(chips):
            copy(4 + j, (*chip, 1 - c), me).wait_recv()
        for cp in first + passed:
            cp.wait_send()
        mine.wait()

    return pl.pallas_call(
        body, name=name,
        out_shape=jax.ShapeDtypeStruct((N_DEV * m_per, n), rows.dtype),
        in_specs=[pl.BlockSpec(memory_space=pltpu.VMEM)],
        out_specs=pl.BlockSpec(memory_space=pltpu.VMEM),
        scratch_shapes=[pltpu.SemaphoreType.DMA((7,)), pltpu.SemaphoreType.DMA((7,)), pltpu.SemaphoreType.DMA],
        compiler_params=_cparams(),
    )(rows)


def _allgather_big(name, arrs):
    na = len(arrs)

    def body(*refs):
        ins, outs = refs[:na], refs[na:2 * na]
        send_sems, recv_sems, local_sems = refs[2 * na:]
        x, y, c = _place()
        me, sibling = (x, y, c), (x, y, 1 - c)
        chips = [(1 - x, y), (x, 1 - y), (1 - x, 1 - y)]

        def copy(a, k, block, to, src=None):
            dst = outs[a].at[_index_of(block)]
            return pltpu.make_async_remote_copy(
                src_ref=dst if src is None else src, dst_ref=dst,
                send_sem=send_sems.at[a, k], recv_sem=recv_sems.at[a, k], device_id=to, device_id_type=MESH)

        mine = [pltpu.make_async_copy(ins[a], outs[a].at[_index_of(me)], local_sems.at[a]) for a in range(na)]
        for cp in mine:
            cp.start()
        sent = []
        for a in range(na):
            sent.append(copy(a, 0, me, sibling, src=ins[a]))
            sent += [copy(a, 1 + j, me, (*chip, c), src=ins[a]) for j, chip in enumerate(chips)]
        for cp in sent:
            cp.start()
        for j, chip in enumerate(chips):
            for a in range(na):
                copy(a, 1 + j, (*chip, c), me).wait_recv()
                fwd = copy(a, 4 + j, (*chip, c), sibling)
                fwd.start()
                sent.append(fwd)
        for a in range(na):
            copy(a, 0, sibling, me).wait_recv()
            for j, chip in enumerate(chips):
                copy(a, 4 + j, (*chip, 1 - c), me).wait_recv()
        for cp in sent:
            cp.wait_send()
        for cp in mine:
            cp.wait()

    any_spec = pl.BlockSpec(memory_space=pl.ANY)
    return pl.pallas_call(
        body, name=name,
        out_shape=[jax.ShapeDtypeStruct((N_DEV,) + a.shape, a.dtype) for a in arrs],
        in_specs=[any_spec] * na, out_specs=[any_spec] * na,
        scratch_shapes=[pltpu.SemaphoreType.DMA((na, 7)), pltpu.SemaphoreType.DMA((na, 7)), pltpu.SemaphoreType.DMA((na,))],
        compiler_params=_cparams(),
    )(*arrs)


N_CHIPS = 4


def _scatter_to_sibling(name, parts):
    na = len(parts)

    def body(*refs):
        ins, outs = refs[:na], refs[na:2 * na]
        send_sems, recv_sems = refs[2 * na:]
        x, y, c = _place()
        sibling = (x, y, 1 - c)
        sent = []
        for a in range(na):
            for q in range(N_CHIPS):
                sent.append(pltpu.make_async_remote_copy(
                    src_ref=ins[a].at[2 * q + (1 - c)], dst_ref=outs[a].at[q],
                    send_sem=send_sems.at[a, q], recv_sem=recv_sems.at[a, q], device_id=sibling, device_id_type=MESH))
        for cp in sent:
            cp.start()
        for cp in sent:
            cp.wait_recv()
        for cp in sent:
            cp.wait_send()

    any_spec = pl.BlockSpec(memory_space=pl.ANY)
    return pl.pallas_call(
        body, name=name,
        out_shape=[jax.ShapeDtypeStruct((N_CHIPS,) + p.shape[1:], p.dtype) for p in parts],
        in_specs=[any_spec] * na, out_specs=[any_spec] * na,
        scratch_shapes=[pltpu.SemaphoreType.DMA((na, N_CHIPS)), pltpu.SemaphoreType.DMA((na, N_CHIPS))],
        compiler_params=_cparams(),
    )(*parts)


def _pair_sum(name, parts, recv):
    _, R, C = parts.shape
    tr = _pick(R, (256, 128, 64, 32, 16))

    def body(p_ref, r_ref, o_ref):
        c = lax.axis_index("c")
        o_ref[...] = (p_ref[c].astype(F32) + r_ref[...].astype(F32)).astype(BF16)

    return pl.pallas_call(
        body, name=name, grid=(N_CHIPS, R // tr),
        in_specs=[pl.BlockSpec((None, 2, tr, C), lambda q, i: (q, 0, i, 0)), pl.BlockSpec((None, tr, C), lambda q, i: (q, i, 0))],
        out_specs=pl.BlockSpec((None, tr, C), lambda q, i: (q, i, 0)),
        out_shape=jax.ShapeDtypeStruct((N_CHIPS, R, C), BF16),
        compiler_params=_cparams(("parallel", "parallel")),
    )(parts.reshape(N_CHIPS, 2, R, C), recv)


def _scatter_to_chips(name, sums):
    na = len(sums)

    def body(*refs):
        ins, outs = refs[:na], refs[na:2 * na]
        send_sems, recv_sems, local_sems = refs[2 * na:]
        x, y, c = _place()
        me = 2 * x + y
        chips = [(1 - x, y), (x, 1 - y), (1 - x, 1 - y)]
        mine = [pltpu.make_async_copy(ins[a].at[me], outs[a].at[me], local_sems.at[a]) for a in range(na)]
        for cp in mine:
            cp.start()
        sent = []
        for a in range(na):
            for k, (qx, qy) in enumerate(chips):
                sent.append(pltpu.make_async_remote_copy(
                    src_ref=ins[a].at[2 * qx + qy], dst_ref=outs[a].at[me],
                    send_sem=send_sems.at[a, k], recv_sem=recv_sems.at[a, k], device_id=(qx, qy, c), device_id_type=MESH))
        for cp in sent:
            cp.start()
        for a in range(na):
            for k, (qx, qy) in enumerate(chips):
                slot = outs[a].at[2 * qx + qy]
                pltpu.make_async_remote_copy(
                    src_ref=slot, dst_ref=slot, send_sem=send_sems.at[a, k], recv_sem=recv_sems.at[a, k],
                    device_id=(qx, qy, c), device_id_type=MESH).wait_recv()
        for cp in sent:
            cp.wait_send()
        for cp in mine:
            cp.wait()

    any_spec = pl.BlockSpec(memory_space=pl.ANY)
    return pl.pallas_call(
        body, name=name,
        out_shape=[jax.ShapeDtypeStruct(p.shape, p.dtype) for p in sums],
        in_specs=[any_spec] * na, out_specs=[any_spec] * na,
        scratch_shapes=[pltpu.SemaphoreType.DMA((na, 3)), pltpu.SemaphoreType.DMA((na, 3)), pltpu.SemaphoreType.DMA((na,))],
        compiler_params=_cparams(),
    )(*sums)


def _gather_row(name, vec, width):
    n = vec.shape[0]
    rows = jnp.pad(vec, (0, width - n)).reshape(SUBLANES_F32, width // SUBLANES_F32)
    return _allgather_small(name, rows).reshape(N_DEV, width)[:, :n]


def _layer_fwd(x, mod, w):
    sh1, sc1, g1, sh2, sc2, g2 = [mod[i:i + 1] for i in range(N_MOD)]
    S = x.shape[0]
    h1 = _normmod_fwd(x, w["norm1_w"], sc1, sh1)
    proj = _matmul("proj_in", h1, w["w_in"], "nn")
    a1, a_out = _conv_a_fwd(proj, w["conv_a_w"], w["conv_a_b"], w["ln_a_w"], w["ln_a_b"])
    qkv = _qkv_views(proj)
    os, ls = zip(*[_attn_fwd(v, dil) for v, dil in zip(qkv, DILATIONS)])
    att, lse = _attn_merge(os, ls)
    of, st_f = _hgrn_fwd(proj, w["lb_f"], ZF, False)
    ob, st_b = _hgrn_fwd(proj, w["lb_b"], ZB, True)
    rec = _hgrn_post_fwd(of, ob, proj, w["rec_norm_w"])
    mixed = jnp.concatenate([att.astype(BF16), rec, a_out], axis=1)
    y1 = _matmul("proj_out", mixed, w["w_out"], "nn")
    x2 = _gate_add(x, y1, g1)
    h2 = _normmod_fwd(x2, w["norm2_w"], sc2, sh2)
    u = _matmul("ffn_up", h2, w["w_up"], "nn")
    act = _ffn_mid_fwd(u, w["conv_f_w"])
    y2 = _matmul("ffn_down", act, w["w_down"], "nn")
    x3 = _gate_add(x2, y2, g2)
    saved = dict(x=x, h1=h1, proj=proj, a1=a1, qkv=qkv, att=att, lse=lse, of=of, ob=ob, st_f=st_f, st_b=st_b,
                 mixed=mixed, y1=y1, x2=x2, h2=h2, u=u, act=act, y2=y2)
    return x3, saved


def _layer_bwd(gx3, mod, w, s):
    sh1, sc1, g1, sh2, sc2, g2 = [mod[i:i + 1] for i in range(N_MOD)]
    S = gx3.shape[0]
    g = {}
    gy2, gg2 = _gate_bwd(gx3, s["y2"], g2)
    gact = _matmul("ffn_down_dx", gy2, w["w_down"], "nt")
    g["w_down"] = _matmul("ffn_down_dw", s["act"], gy2, "tn")
    gu, g["conv_f_w"] = _ffn_mid_bwd(s["u"], gact, w["conv_f_w"])
    gh2 = _matmul("ffn_up_dx", gu, w["w_up"], "nt")
    g["w_up"] = _matmul("ffn_up_dw", s["h2"], gu, "tn")
    gx2, g["norm2_w"], gsc2, gsh2 = _normmod_bwd(s["x2"], gh2, gx3, w["norm2_w"], sc2, sh2)
    gy1, gg1 = _gate_bwd(gx2, s["y1"], g1)
    gmixed = _matmul("proj_out_dx", gy1, w["w_out"], "nt")
    g["w_out"] = _matmul("proj_out_dw", s["mixed"], gy1, "tn")
    go, ggr, g["rec_norm_w"] = _hgrn_post_bwd(s["of"], s["ob"], s["proj"], gmixed, w["rec_norm_w"])
    gq_f, gz_f, gv_f, g["lb_f"] = _hgrn_bwd(s["proj"], w["lb_f"], s["st_f"], go, ZF, False)
    gq_b, gz_b, gv_b, g["lb_b"] = _hgrn_bwd(s["proj"], w["lb_b"], s["st_b"], go, ZB, True)
    dos, lds = _attn_bwd_prep(gmixed, s["att"], s["lse"])
    gqkv = zip(*[_attn_bwd(v, do, ld, dil) for v, do, ld, dil in zip(s["qkv"], dos, lds, DILATIONS)])
    gq_a, gk_a, gv_a = [_sum3_bf16(lst, S, ATT_W) for lst in gqkv]
    gav, gag, gcw, g["conv_a_b"], g["ln_a_w"], g["ln_a_b"] = _conv_a_bwd(
        s["proj"], s["a1"], gmixed, w["conv_a_w"], w["ln_a_w"], w["ln_a_b"])
    g["conv_a_w"] = gcw[:CONV_W]
    gproj = jnp.concatenate([gq_a, gk_a, gv_a, (gq_f + gq_b).astype(BF16), gz_f.astype(BF16), gz_b.astype(BF16),
                             (gv_f + gv_b).astype(BF16), ggr.astype(BF16), gav, gag,
                             jnp.zeros((S, IN_COLS_PAD - IN_COLS), BF16)], axis=1)
    gh1 = _matmul("proj_in_dx", gproj, w["w_in"], "nt")
    g["w_in"] = _matmul("proj_in_dw", s["h1"], gproj, "tn")
    gx, g["norm1_w"], gsc1, gsh1 = _normmod_bwd(s["x"], gh1, gx2, w["norm1_w"], sc1, sh1)
    gmod = jnp.concatenate([gsh1, gsc1, gg1, gsh2, gsc2, gg2], axis=0)
    return gx, gmod, g


def _permute_in_cols(t):
    pad = jnp.zeros(t.shape[:-1] + (IN_COLS_PAD - IN_COLS,), t.dtype)
    return jnp.concatenate([t[..., CONV_COLS:], t[..., :CONV_COLS], pad], axis=-1)


def _unpermute_in_cols(t):
    return jnp.concatenate([t[..., IN_COLS - CONV_COLS:IN_COLS], t[..., :IN_COLS - CONV_COLS]], axis=-1)


def _cols_from_gathered(t, lead):
    nd = t.ndim
    perm = tuple(range(1, nd - 1)) + (0, nd - 1)
    t = t.transpose(perm)
    return t.reshape(t.shape[:-2] + (t.shape[-2] * t.shape[-1],))


def _cols_to_parts(t):
    L, R, C = t.shape
    return t.reshape(L * R, N_DEV, C // N_DEV).transpose(1, 0, 2)


SMALL_REPL = (("norm1_w", D), ("conv_a_b", CONV_CH), ("ln_a_w", CONV_CH), ("ln_a_b", CONV_CH),
              ("rec_norm_w", REC_W), ("norm2_w", D))


def kernel(x, c, w_ada, b_ada, norm1_w, w_in, conv_a_w, conv_a_b, ln_a_w, ln_a_b, lb_gamma, rec_norm_w, w_out, norm2_w, w_up, conv_f_w, w_down, final_norm_w, loss_target, m_w_ada, m_b_ada, m_norm1_w, m_w_in, m_conv_a_w, m_conv_a_b, m_ln_a_w, m_ln_a_b, m_lb_gamma, m_rec_norm_w, m_w_out, m_norm2_w, m_w_up, m_conv_f_w, m_w_down, m_final_norm_w, v_w_ada, v_b_ada, v_norm1_w, v_w_in, v_conv_a_w, v_conv_a_b, v_ln_a_w, v_ln_a_b, v_lb_gamma, v_rec_norm_w, v_w_out, v_norm2_w, v_w_up, v_conv_f_w, v_w_down, v_final_norm_w):
    px, py, pc = _place()
    me = _index_of((px, py, pc))
    xs, tgt = x[0], loss_target[0]
    S = xs.shape[0]
    ada_cols = w_ada.shape[2]

    big = [w_in.reshape(DEPTH * D, -1), w_up.reshape(DEPTH * D, -1), w_out.reshape(-1, D), w_down.reshape(-1, D)]
    g_in, g_up, g_out, g_down = _allgather_big("gather_weights", [t.astype(BF16) for t in big])
    w_in_f = _permute_in_cols(_cols_from_gathered(g_in.reshape(N_DEV, DEPTH, D, -1), 1))
    w_up_f = _cols_from_gathered(g_up.reshape(N_DEV, DEPTH, D, -1), 1)
    w_out_f = g_out.reshape(N_DEV, DEPTH, D // N_DEV, D).transpose(1, 0, 2, 3).reshape(DEPTH, D, D)
    w_out_f = jnp.concatenate([w_out_f[:, CONV_CH:], w_out_f[:, :CONV_CH]], axis=1)
    w_down_f = g_down.reshape(N_DEV, DEPTH, D_FF // N_DEV, D).transpose(1, 0, 2, 3).reshape(DEPTH, D_FF, D)

    small_in = jnp.concatenate([c.reshape(-1), conv_a_w.reshape(-1), lb_gamma.reshape(-1), conv_f_w.reshape(-1)])
    gs = _gather_row("gather_small", small_in, 8192)
    o1 = D
    o2 = o1 + conv_a_w.size
    o3 = o2 + lb_gamma.size
    c_all = gs[:, :o1]
    conv_a_f = _cols_from_gathered(gs[:, o1:o2].reshape(N_DEV, DEPTH, CONV_W, -1), 1)
    lb_gamma_f = _cols_from_gathered(gs[:, o2:o3].reshape(N_DEV, DEPTH, 2, -1), 1)
    conv_f_f = _cols_from_gathered(gs[:, o3:].reshape(N_DEV, DEPTH, FFN_CONV_W, -1), 1)
    conv_a_pad = jnp.pad(conv_a_f, ((0, 0), (0, CONV_W_PAD - CONV_W), (0, 0)))

    b_loc = lax.dynamic_slice_in_dim(b_ada, me * ada_cols, ada_cols, axis=1)

    def mod_fn(c_all_, w_, b_):
        cond = c_all_ * jax.nn.sigmoid(c_all_)
        return (jnp.concatenate([_dot_nn(cond, w_[l], HP) + b_[l] for l in range(DEPTH)], axis=1),)

    (mod_part,) = _vmem_call("ada_mod", mod_fn, [c_all, w_ada, b_loc[:, None, :]], [((N_DEV, DEPTH * ada_cols), F32)])
    gm = _allgather_small("gather_mod", mod_part).reshape(N_DEV, N_DEV, DEPTH, ada_cols)
    mod = lax.dynamic_index_in_dim(gm, me, axis=1, keepdims=False)
    mod = mod.transpose(1, 0, 2).reshape(DEPTH, N_MOD, D)

    (lb1,) = _vmem_call("lower_bounds", lambda a, b: (_lower_bounds_f(a, b),), [lb_gamma_f[0], lb_gamma_f[1]], [((2, REC_W), F32)])
    lb4 = jnp.concatenate([jnp.zeros_like(lb1), lb1], axis=0)

    def layer_weights(l):
        row = lambda t: t[l].reshape(1, -1)
        return dict(norm1_w=row(norm1_w), w_in=w_in_f[l], conv_a_w=conv_a_pad[l], conv_a_b=row(conv_a_b), ln_a_w=row(ln_a_w),
                    ln_a_b=row(ln_a_b), lb_f=lb4[2 * l:2 * l + 1], lb_b=lb4[2 * l + 1:2 * l + 2], rec_norm_w=row(rec_norm_w),
                    w_out=w_out_f[l], norm2_w=row(norm2_w), w_up=w_up_f[l], conv_f_w=conv_f_f[l], w_down=w_down_f[l])

    ws = [layer_weights(l) for l in range(DEPTH)]
    h, saved = xs, []
    for l in range(DEPTH):
        h, s = _layer_fwd(h, mod[l], ws[l])
        saved.append(s)
    gh, g_final, loss_row = _loss_head(h, tgt, final_norm_w.reshape(1, D))
    loss = lax.psum(loss_row[0, 0], ("x", "y", "c"))
    gmods, gws = [None] * DEPTH, [None] * DEPTH
    for l in reversed(range(DEPTH)):
        gh, gmods[l], gws[l] = _layer_bwd(gh, mod[l], ws[l], saved[l])
    grad_x = gh[None]

    glb1 = jnp.concatenate([gws[1]["lb_f"], gws[1]["lb_b"]], axis=0)

    def lb_bwd_fn(a, b, g1):
        _, vjp = jax.vjp(_lower_bounds_f, a, b)
        return vjp(g1)

    g_lb_gamma = jnp.stack(_vmem_call("lower_bounds_bwd", lb_bwd_fn, [lb_gamma_f[0], lb_gamma_f[1], glb1], [((2, REC_W), F32)] * 2))
    pieces = [jnp.stack(gmods).reshape(-1)]
    for l in range(DEPTH):
        pieces += [gws[l][n].reshape(-1) for n, _ in SMALL_REPL]
    pieces += [g_final.reshape(-1)]
    pieces += [jnp.stack([gws[l]["conv_a_w"] for l in range(DEPTH)]).reshape(-1), g_lb_gamma.reshape(-1),
               jnp.stack([gws[l]["conv_f_w"] for l in range(DEPTH)]).reshape(-1)]
    small_g = jnp.concatenate(pieces)
    n_small = small_g.shape[0]
    gsm = _gather_row("gather_small_grads", small_g, 71680)
    n_mod = DEPTH * N_MOD * D
    gmod_all = gsm[:, :n_mod].reshape(N_DEV, DEPTH, N_MOD * D)
    gmod_loc = lax.dynamic_slice_in_dim(gmod_all, me * ada_cols, ada_cols, axis=2).transpose(1, 0, 2)

    def small_fn(gsm_, c_all_, gm_):
        cond = c_all_ * jax.nn.sigmoid(c_all_)
        gw = jnp.concatenate([_dot_tn(cond, gm_[l], HP) for l in range(DEPTH)], axis=0)
        return jnp.sum(gsm_, axis=0, keepdims=True), gw

    tot, g_w_ada = _vmem_call("small_grads", small_fn, [gsm, c_all, gmod_loc],
                              [((1, n_small), F32), ((DEPTH * D, ada_cols), F32)])
    tot = tot[0]
    grads = {"w_ada": g_w_ada.reshape(DEPTH, D, ada_cols), "b_ada": tot[:n_mod].reshape(DEPTH, N_MOD * D)}
    pos = n_mod
    per_layer = {n: [] for n, _ in SMALL_REPL}
    for l in range(DEPTH):
        for n, width in SMALL_REPL:
            per_layer[n].append(tot[pos:pos + width])
            pos += width
    for n, _ in SMALL_REPL:
        grads[n] = jnp.stack(per_layer[n])
    grads["final_norm_w"] = tot[pos:pos + D]
    pos += D
    n_ca, n_lb, n_cf = DEPTH * CONV_W * CONV_CH, DEPTH * 2 * REC_W, DEPTH * FFN_CONV_W * 2 * D_FF
    g_ca = tot[pos:pos + n_ca].reshape(DEPTH, CONV_W, CONV_CH)
    g_lb = tot[pos + n_ca:pos + n_ca + n_lb].reshape(DEPTH, 2, REC_W)
    g_cf = tot[pos + n_ca + n_lb:pos + n_ca + n_lb + n_cf].reshape(DEPTH, FFN_CONV_W, 2 * D_FF)
    grads["conv_a_w"] = lax.dynamic_slice_in_dim(g_ca, me * conv_a_w.shape[2], conv_a_w.shape[2], axis=2)
    grads["lb_gamma"] = lax.dynamic_slice_in_dim(g_lb, me * lb_gamma.shape[2], lb_gamma.shape[2], axis=2)
    grads["conv_f_w"] = lax.dynamic_slice_in_dim(g_cf, me * conv_f_w.shape[2], conv_f_w.shape[2], axis=2)

    gw_in = _unpermute_in_cols(jnp.stack([gws[l]["w_in"] for l in range(DEPTH)]))
    gw_up = jnp.stack([gws[l]["w_up"] for l in range(DEPTH)])
    gw_out = jnp.stack([gws[l]["w_out"] for l in range(DEPTH)])
    gw_out = jnp.concatenate([gw_out[:, D - CONV_CH:], gw_out[:, :D - CONV_CH]], axis=1)
    gw_down = jnp.stack([gws[l]["w_down"] for l in range(DEPTH)])
    rows_to_parts = lambda t: t.reshape(DEPTH, N_DEV, -1, D).transpose(1, 0, 2, 3).reshape(N_DEV, -1, D)
    parts = [_cols_to_parts(gw_in), _cols_to_parts(gw_up), rows_to_parts(gw_out), rows_to_parts(gw_down)]
    parts = [t.astype(BF16) for t in parts]
    from_sibling = _scatter_to_sibling("scatter_sibling", parts)
    sums = [_pair_sum("pair_sum", p, r) for p, r in zip(parts, from_sibling)]
    r_in, r_up, r_out, r_down = _scatter_to_chips("scatter_chips", sums)

    given = dict(w_ada=(w_ada, m_w_ada, v_w_ada), b_ada=(b_ada, m_b_ada, v_b_ada), norm1_w=(norm1_w, m_norm1_w, v_norm1_w),
                 w_in=(w_in, m_w_in, v_w_in), conv_a_w=(conv_a_w, m_conv_a_w, v_conv_a_w), conv_a_b=(conv_a_b, m_conv_a_b, v_conv_a_b),
                 ln_a_w=(ln_a_w, m_ln_a_w, v_ln_a_w), ln_a_b=(ln_a_b, m_ln_a_b, v_ln_a_b), lb_gamma=(lb_gamma, m_lb_gamma, v_lb_gamma),
                 rec_norm_w=(rec_norm_w, m_rec_norm_w, v_rec_norm_w), w_out=(w_out, m_w_out, v_w_out),
                 norm2_w=(norm2_w, m_norm2_w, v_norm2_w), w_up=(w_up, m_w_up, v_w_up), conv_f_w=(conv_f_w, m_conv_f_w, v_conv_f_w),
                 w_down=(w_down, m_w_down, v_w_down), final_norm_w=(final_norm_w, m_final_norm_w, v_final_norm_w))
    big_parts = dict(w_in=r_in, w_up=r_up, w_out=r_out, w_down=r_down)
    names = list(given)
    res = {}
    for n in names:
        w_, m_, v_ = given[n]
        shape = w_.shape
        C = shape[-1]
        two_d = lambda t: t.reshape(-1, C)
        gp = big_parts[n] if n in big_parts else two_d(grads[n])[None]
        res[n] = [t.reshape(shape) for t in _adamw("adamw_" + n, two_d(w_), two_d(m_), two_d(v_), gp)]
    return (loss, grad_x, *[res[n][0] for n in names], *[res[n][1] for n in names],
            *[res[n][2] for n in names], *[res[n][3] for n in names])
```

```python
import functools

import jax
import jax.numpy as jnp
from jax import lax
from jax.experimental import pallas as pl
from jax.experimental.pallas import tpu as pltpu

F32 = jnp.float32
BF16 = jnp.bfloat16
HP = lax.Precision.HIGHEST
MESH = pl.DeviceIdType.MESH

N_DEV = 8
D = 1024
DEPTH = 2
CONV_CH = 256
CONV_W = 31
CONV_W_PAD = 32
ATT_W = 384
REC_W = 384
N_HEADS = 6
HEAD = 64
HEAD_SHIFT = 6
HALF_BAND = 64
ATT_BLK = 128
DILATIONS = (1, 4, 16)
ALIBI_SLOPES = tuple(float(2.0 ** (-8.0 * (h + 1) / N_HEADS)) for h in range(N_HEADS))
MASK_VALUE = -1e30
REC_CHUNK = 64
EXP_CLAMP = 80.0
F_TINY = 1e-30
IN_COLS = 3584
IN_COLS_PAD = IN_COLS
QKV_BLOCKS = 3
D_FF = 2816
FFN_CONV_W = 3
N_MOD = 6
EPS = 1e-6
ADAM_LR, ADAM_B1, ADAM_B2, ADAM_EPS, ADAM_WD, ADAM_STEP = 0.001, 0.9, 0.999, 1e-08, 0.01, 10

VMEM_LIMIT_BYTES = 56 * 1024 * 1024
SUBLANES_F32 = 8
LANES = 128

QA, KA, VA, QR, ZF, ZB, IR, GR = range(8)
AV_BLK, AG_BLK = 12, 13
CONV_COLS = 2 * CONV_CH


def _cparams(sem=None):
    kw = dict(vmem_limit_bytes=VMEM_LIMIT_BYTES)
    if sem is not None:
        kw["dimension_semantics"] = sem
    return pltpu.CompilerParams(**kw)


def _iota(shape, dim):
    return lax.broadcasted_iota(jnp.int32, shape, dim)


def _dot(a, b, dims, precision=None):
    return lax.dot_general(a, b, (dims, ((), ())), precision=precision, preferred_element_type=F32)


def _dot_nn(a, b, precision=None):
    return _dot(a, b, ((1,), (0,)), precision)


def _dot_nt(a, b, precision=None):
    return _dot(a, b, ((1,), (1,)), precision)


def _dot_tn(a, b, precision=None):
    return _dot(a, b, ((0,), (0,)), precision)


def _c0(j):
    return 0


def _pick(n, cands):
    for c in cands:
        if n % c == 0:
            return c
    return n


MATMUL_OUT_TILE_BYTES = 8 * 1024 * 1024


def _div_lanes(n, cap):
    best = None
    for d in range(LANES, min(n, cap) + 1, LANES):
        if n % d == 0:
            best = d
    return best if best is not None else n


def _matmul_tiles(mode, M, N, K):
    if mode == "nn":
        tm = _pick(M, (1024, 512, 256, 128))
        return tm, _div_lanes(N, MATMUL_OUT_TILE_BYTES // (4 * tm)), K
    if mode == "nt":
        return _pick(M, (512, 256, 128)), N, K
    tm = _div_lanes(M, 1408)
    return tm, _div_lanes(N, MATMUL_OUT_TILE_BYTES // (4 * tm)), _pick(K, (1024, 512, 256))


def _matmul(name, a, b, mode, out_dtype=F32):
    if mode == "nn":
        (M, K), (_, N) = a.shape, b.shape
    elif mode == "nt":
        (M, K), (N, _) = a.shape, b.shape
    else:
        (K, M), (_, N) = a.shape, b.shape
    tm, tn, tk = _matmul_tiles(mode, M, N, K)
    nk = K // tk
    if mode == "nn":
        a_spec = pl.BlockSpec((tm, tk), lambda i, j, k: (i, k))
        b_spec = pl.BlockSpec((tk, tn), lambda i, j, k: (k, j))
        dims = ((1,), (0,))
    elif mode == "nt":
        a_spec = pl.BlockSpec((tm, tk), lambda i, j, k: (i, k))
        b_spec = pl.BlockSpec((tn, tk), lambda i, j, k: (j, k))
        dims = ((1,), (1,))
    else:
        a_spec = pl.BlockSpec((tk, tm), lambda i, j, k: (k, i))
        b_spec = pl.BlockSpec((tk, tn), lambda i, j, k: (k, j))
        dims = ((0,), (0,))

    def body_whole(a_ref, b_ref, o_ref):
        o_ref[...] = _dot(a_ref[...].astype(BF16), b_ref[...].astype(BF16), dims).astype(o_ref.dtype)

    def body(a_ref, b_ref, o_ref, acc_ref):
        k = pl.program_id(2)
        part = _dot(a_ref[...].astype(BF16), b_ref[...].astype(BF16), dims)

        @pl.when(k == 0)
        def _():
            acc_ref[...] = part

        @pl.when(k > 0)
        def _():
            acc_ref[...] += part

        @pl.when(k == nk - 1)
        def _():
            o_ref[...] = acc_ref[...].astype(o_ref.dtype)

    return pl.pallas_call(
        body_whole if nk == 1 else body, name=name, grid=(M // tm, N // tn, nk),
        in_specs=[a_spec, b_spec],
        out_specs=pl.BlockSpec((tm, tn), lambda i, j, k: (i, j)),
        out_shape=jax.ShapeDtypeStruct((M, N), out_dtype),
        scratch_shapes=[] if nk == 1 else [pltpu.VMEM((tm, tn), F32)],
        compiler_params=_cparams(("parallel", "parallel", "arbitrary")),
    )(a, b)


def _matmul_nt_pieces(name, a_pieces, b_pieces):
    M, N = a_pieces[0].shape[0], b_pieces[0].shape[0]
    tm = _pick(M, (512, 256, 128))
    n = len(a_pieces)

    def body(*refs):
        acc = _dot(refs[0][...].astype(BF16), refs[n][...].astype(BF16), ((1,), (1,)))
        for p in range(1, n):
            acc = acc + _dot(refs[p][...].astype(BF16), refs[n + p][...].astype(BF16), ((1,), (1,)))
        refs[2 * n][...] = acc

    in_specs = ([pl.BlockSpec((tm, a.shape[1]), lambda i: (i, 0)) for a in a_pieces]
                + [pl.BlockSpec(b.shape, lambda i: (0, 0)) for b in b_pieces])
    return pl.pallas_call(
        body, name=name, grid=(M // tm,), in_specs=in_specs,
        out_specs=pl.BlockSpec((tm, N), lambda i: (i, 0)), out_shape=jax.ShapeDtypeStruct((M, N), F32),
        compiler_params=_cparams(("parallel",)),
    )(*a_pieces, *b_pieces)


def _rowwise(name, fn, S, ts, tiles, params=(), outs=(), accs=(), halo=0, ncb=1):
    in_specs, args, scratch = [], [], []
    for arr, w, jm, with_halo in tiles:
        if isinstance(with_halo, int) and with_halo > 1:
            d = with_halo
            in_specs.append(pl.BlockSpec((ts // d, d * w), lambda j, i: (i, 0)))
            args.append(arr)
            scratch.append(pltpu.VMEM((w // LANES, ts, LANES), F32))
        elif with_halo:
            hb, nhb = ts // halo, S // halo
            in_specs += [
                pl.BlockSpec((halo, w), lambda j, i, jm=jm, hb=hb: (jnp.maximum(i * hb - 1, 0), jm(j))),
                pl.BlockSpec((ts, w), lambda j, i, jm=jm: (i, jm(j))),
                pl.BlockSpec((halo, w), lambda j, i, jm=jm, hb=hb, nhb=nhb: (jnp.minimum((i + 1) * hb, nhb - 1), jm(j))),
            ]
            args += [arr, arr, arr]
        else:
            in_specs.append(pl.BlockSpec((ts, w), lambda j, i, jm=jm: (i, jm(j))))
            args.append(arr)
    for arr, r, w, jm in params:
        in_specs.append(pl.BlockSpec((r, w), lambda j, i, jm=jm: (0, jm(j))))
        args.append(arr)
    out_specs, out_shape = [], []
    for w, dt, jm, tw, *dil in outs:
        if dil:
            out_specs.append(pl.BlockSpec((ts // dil[0], dil[0] * w), lambda j, i: (i, 0)))
            out_shape.append(jax.ShapeDtypeStruct((S // dil[0], dil[0] * w), dt))
            scratch += [pltpu.VMEM((ts, w), F32), pltpu.VMEM((w // LANES, ts, LANES), F32)]
        else:
            out_specs.append(pl.BlockSpec((ts, w), lambda j, i, jm=jm: (i, jm(j))))
            out_shape.append(jax.ShapeDtypeStruct((S, tw), dt))
    for r, w, jm, tw in accs:
        out_specs.append(pl.BlockSpec((r, w), lambda j, i, jm=jm: (0, jm(j))))
        out_shape.append(jax.ShapeDtypeStruct((r, tw), F32))
    n_tiles, n_params, n_outs, n_accs = len(tiles), len(params), len(outs), len(accs)

    def residue_rows(r, d):
        return pl.ds(r, ts // d, stride=d)

    def body(*refs):
        i = pl.program_id(1)
        n_io = len(in_specs) + n_outs + n_accs
        scr = list(refs[n_io:])
        refs = refs[:n_io]
        pos, vals = 0, []
        for _, w, _, with_halo in tiles:
            if isinstance(with_halo, int) and with_halo > 1:
                d, buf = with_halo, scr.pop(0)
                for r in range(d):
                    for c in range(w // LANES):
                        buf[c, residue_rows(r, d), :] = refs[pos][:, r * w + c * LANES:r * w + (c + 1) * LANES].astype(F32)
                vals.append(jnp.concatenate([buf[c] for c in range(w // LANES)], axis=1))
                pos += 1
            elif with_halo:
                before, after = refs[pos][...], refs[pos + 2][...]
                before = jnp.where(i > 0, before, jnp.zeros_like(before))
                after = jnp.where(i < S // ts - 1, after, jnp.zeros_like(after))
                vals.append(jnp.concatenate([before, refs[pos + 1][...], after], axis=0))
                pos += 3
            else:
                vals.append(refs[pos][...])
                pos += 1
        prefs = refs[pos:pos + n_params]
        orefs = list(refs[pos + n_params:pos + n_params + n_outs])
        arefs = refs[pos + n_params + n_outs:]
        staged = []
        for k, (w, _, _, _, *dil) in enumerate(outs):
            if dil:
                staged.append((orefs[k], scr.pop(0), scr.pop(0), w, dil[0]))
                orefs[k] = staged[-1][1]

        @pl.when(i == 0)
        def _():
            for r in arefs:
                r[...] = jnp.zeros_like(r)

        fn(i, vals, prefs, orefs, arefs)
        for out_ref, flat, buf, w, d in staged:
            for c in range(w // LANES):
                buf[c] = flat[:, c * LANES:(c + 1) * LANES]
                for r in range(d):
                    out_ref[:, r * w + c * LANES:r * w + (c + 1) * LANES] = buf[c, residue_rows(r, d), :].astype(out_ref.dtype)

    res = pl.pallas_call(
        body, name=name, grid=(ncb, S // ts),
        in_specs=in_specs, out_specs=out_specs, out_shape=out_shape, scratch_shapes=scratch,
        compiler_params=_cparams(("arbitrary", "arbitrary")),
    )(*args)
    return res


def _vmem_call(name, fn, ins, out_shapes):
    n_in = len(ins)

    def body(*refs):
        vals = fn(*[r[...] for r in refs[:n_in]])
        for r, v in zip(refs[n_in:], vals):
            r[...] = v.astype(r.dtype)

    return pl.pallas_call(
        body, name=name,
        out_shape=[jax.ShapeDtypeStruct(s, dt) for s, dt in out_shapes],
        compiler_params=_cparams(),
    )(*ins)


def _rms(x, w):
    return x * lax.rsqrt(jnp.mean(x * x, axis=-1, keepdims=True) + EPS) * w


def _normmod_f(x, nw, sc, sh):
    return _rms(x, nw) * (1.0 + sc) + sh


def _row_params(*vecs):
    return [(v, 1, v.shape[1], _c0) for v in vecs]


def _normmod_fwd(x, nw, sc, sh):
    S = x.shape[0]

    def fn(i, vals, p, o, a):
        o[0][...] = _normmod_f(vals[0], p[0][...], p[1][...], p[2][...]).astype(BF16)

    return _rowwise("normmod_fwd", fn, S, 512, [(x, D, _c0, False)], _row_params(nw, sc, sh), [(D, BF16, _c0, D)])[0]


def _normmod_bwd(x, gh, gres, nw, sc, sh):
    S = x.shape[0]

    def fn(i, vals, p, o, a):
        _, vjp = jax.vjp(_normmod_f, vals[0], p[0][...], p[1][...], p[2][...])
        gx, gnw, gsc, gsh = vjp(vals[1])
        o[0][...] = gx + vals[2]
        a[0][...] += gnw
        a[1][...] += gsc
        a[2][...] += gsh

    return _rowwise("normmod_bwd", fn, S, 512, [(x, D, _c0, False), (gh, D, _c0, False), (gres, D, _c0, False)],
                    _row_params(nw, sc, sh), [(D, F32, _c0, D)], [(1, D, _c0, D)] * 3)


def _gate_add(x, y, g):
    S = x.shape[0]

    def fn(i, vals, p, o, a):
        o[0][...] = vals[0] + p[0][...] * vals[1]

    return _rowwise("gate_add", fn, S, 512, [(x, D, _c0, False), (y, D, _c0, False)], _row_params(g), [(D, F32, _c0, D)])[0]


def _gate_bwd(gx, y, g):
    S = gx.shape[0]

    def fn(i, vals, p, o, a):
        o[0][...] = (vals[0] * p[0][...]).astype(BF16)
        a[0][...] += jnp.sum(vals[0] * vals[1], axis=0, keepdims=True)

    return _rowwise("gate_bwd", fn, S, 512, [(gx, D, _c0, False), (y, D, _c0, False)], _row_params(g),
                    [(D, BF16, _c0, D)], [(1, D, _c0, D)])


def _gate_add_normmod(x, y, g, nw, sc, sh):
    S = x.shape[0]

    def fn(i, vals, p, o, a):
        x2 = vals[0] + p[0][...] * vals[1]
        o[0][...] = x2
        o[1][...] = _normmod_f(x2, p[1][...], p[2][...], p[3][...]).astype(BF16)

    return _rowwise("gate_add_normmod", fn, S, 512, [(x, D, _c0, False), (y, D, _c0, False)], _row_params(g, nw, sc, sh),
                    [(D, F32, _c0, D), (D, BF16, _c0, D)])


def _normmod_gate_bwd(x, gh, gres, nw, sc, sh, y, g):
    S = x.shape[0]

    def fn(i, vals, p, o, a):
        _, vjp = jax.vjp(_normmod_f, vals[0], p[0][...], p[1][...], p[2][...])
        gx, gnw, gsc, gsh = vjp(vals[1])
        gx = gx + vals[2]
        o[0][...] = gx
        o[1][...] = (gx * p[3][...]).astype(BF16)
        a[0][...] += gnw
        a[1][...] += gsc
        a[2][...] += gsh
        a[3][...] += jnp.sum(gx * vals[3], axis=0, keepdims=True)

    tiles = [(t, D, _c0, False) for t in (x, gh, gres, y)]
    return _rowwise("normmod_gate_bwd", fn, S, 512, tiles, _row_params(nw, sc, sh, g),
                    [(D, F32, _c0, D), (D, BF16, _c0, D)], [(1, D, _c0, D)] * 4)


def _loss_head(x, tgt, fw):
    S = x.shape[0]

    def fn(i, vals, p, o, a):
        y, vjp = jax.vjp(_rms, vals[0], p[0][...])
        err = y - vals[1]
        gx, gfw = vjp(err * (1.0 / D))
        o[0][...] = gx
        a[0][...] += gfw
        part = 0.5 * jnp.sum(jnp.mean(err * err, axis=-1, keepdims=True), axis=0, keepdims=True)
        a[1][...] += jnp.broadcast_to(part, (1, LANES))

    return _rowwise("loss_head", fn, S, 256, [(x, D, _c0, False), (tgt, D, _c0, False)], _row_params(fw),
                    [(D, F32, _c0, D)], [(1, D, _c0, D), (1, LANES, _c0, LANES)])


CONV_HALO = 16
CONV_TS = 512


def _shifted(ext, shift, ts, halo):
    n = ext.shape[0]
    s = shift % n
    r = ext if s == 0 else pltpu.roll(ext, s, 0)
    return r[halo:halo + ts]


def _ln_silu(a, w, b):
    mu = jnp.mean(a, axis=-1, keepdims=True)
    var = jnp.mean(jnp.square(a - mu), axis=-1, keepdims=True)
    y = (a - mu) * lax.rsqrt(var + EPS) * w + b
    return y * jax.nn.sigmoid(y)


def _conv_a_fwd(proj, w_pad, b, lnw, lnb):
    S = proj.shape[0]
    ts, H = min(CONV_TS, S), CONV_HALO

    def fn(i, vals, p, o, a):
        a0 = vals[0] * jax.nn.sigmoid(vals[1])
        acc = jnp.zeros((ts, CONV_CH), F32) + p[1][...]
        for k in range(CONV_W):
            acc = acc + _shifted(a0, CONV_W // 2 - k, ts, H) * p[0][pl.ds(k, 1), :]
        o[0][...] = acc
        o[1][...] = _ln_silu(acc, p[2][...], p[3][...]).astype(BF16)

    tiles = [(proj, CONV_CH, lambda j: AV_BLK, True), (proj, CONV_CH, lambda j: AG_BLK, True)]
    params = [(w_pad, CONV_W_PAD, CONV_CH, _c0)] + _row_params(b, lnw, lnb)
    return _rowwise("conv_a_fwd", fn, S, ts, tiles, params, [(CONV_CH, F32, _c0, CONV_CH), (CONV_CH, BF16, _c0, CONV_CH)], halo=H)


def _conv_a_bwd(proj, a1, gmixed, w_pad, lnw, lnb):
    S = proj.shape[0]
    ts, H = min(CONV_TS, S), CONV_HALO

    def fn(i, vals, p, o, a):
        av, ag, a1e, ge = vals
        lw, lb = p[1][...], p[2][...]
        _, vjp_e = jax.vjp(lambda t: _ln_silu(t, lw, lb), a1e)
        (ga1e,) = vjp_e(ge)
        c = slice(H, H + ts)
        _, vjp_c = jax.vjp(_ln_silu, a1e[c], lw, lb)
        ga1, glw, glb = vjp_c(ge[c])
        a[1][...] += jnp.sum(ga1, axis=0, keepdims=True)
        a[2][...] += glw
        a[3][...] += glb
        sg = jax.nn.sigmoid(ag)
        a0 = av * sg
        ga0 = jnp.zeros((ts, CONV_CH), F32)
        for k in range(CONV_W):
            a[0][pl.ds(k, 1), :] += jnp.sum(ga1 * _shifted(a0, CONV_W // 2 - k, ts, H), axis=0, keepdims=True)
            ga0 = ga0 + _shifted(ga1e, k - CONV_W // 2, ts, H) * p[0][pl.ds(k, 1), :]
        sgc, avc = sg[c], av[c]
        o[0][...] = (ga0 * sgc).astype(BF16)
        o[1][...] = (ga0 * avc * sgc * (1.0 - sgc)).astype(BF16)

    tiles = [(proj, CONV_CH, lambda j: AV_BLK, True), (proj, CONV_CH, lambda j: AG_BLK, True),
             (a1, CONV_CH, _c0, True), (gmixed, CONV_CH, lambda j: 3, True)]
    params = [(w_pad, CONV_W_PAD, CONV_CH, _c0)] + _row_params(lnw, lnb)
    outs = [(CONV_CH, BF16, _c0, CONV_CH), (CONV_CH, BF16, _c0, CONV_CH)]
    accs = [(CONV_W_PAD, CONV_CH, _c0, CONV_CH)] + [(1, CONV_CH, _c0, CONV_CH)] * 3
    return _rowwise("conv_a_bwd", fn, S, ts, tiles, params, outs, accs, halo=H)


FFN_HALO = 8
FFN_TS = 512
FFN_TS_FWD = 1024
FFN_CB = 256
FFN_NCB = D_FF // FFN_CB


def _gelu_mul(g, v):
    return 0.5 * g * (1.0 + lax.erf(g * (2.0 ** -0.5))) * v


def _ffn_mid_fwd(u, cw):
    S = u.shape[0]
    ts, H = min(FFN_TS_FWD, S), FFN_HALO

    def conv(ext, w_ref):
        acc = jnp.zeros((ts, FFN_CB), F32)
        for k in range(FFN_CONV_W):
            acc = acc + _shifted(ext, 1 - k, ts, H) * w_ref[pl.ds(k, 1), :]
        return acc

    def fn(i, vals, p, o, a):
        o[0][...] = _gelu_mul(conv(vals[0], p[0]), conv(vals[1], p[1])).astype(BF16)

    gate, val = (lambda j: j), (lambda j: j + FFN_NCB)
    return _rowwise("ffn_mid_fwd", fn, S, ts, [(u, FFN_CB, gate, True), (u, FFN_CB, val, True)],
                    [(cw, FFN_CONV_W, FFN_CB, gate), (cw, FFN_CONV_W, FFN_CB, val)],
                    [(FFN_CB, BF16, gate, D_FF)], halo=H, ncb=FFN_NCB)[0]


def _ffn_mid_bwd(u, gact, cw):
    S = u.shape[0]
    ts, H = min(FFN_TS, S), FFN_HALO
    n = ts + 2 * H

    def fn(i, vals, p, o, a):
        ug, uv, ga = vals

        def conv_all(ue, w_ref):
            acc = jnp.zeros((n, FFN_CB), F32)
            for k in range(FFN_CONV_W):
                s = (1 - k) % n
                acc = acc + (ue if s == 0 else pltpu.roll(ue, s, 0)) * w_ref[pl.ds(k, 1), :]
            return acc

        _, vjp = jax.vjp(_gelu_mul, conv_all(ug, p[0]), conv_all(uv, p[1]))
        for half, (gc, ue) in enumerate(zip(vjp(ga), (ug, uv))):
            gu = jnp.zeros((ts, FFN_CB), F32)
            for k in range(FFN_CONV_W):
                gu = gu + _shifted(gc, k - 1, ts, H) * p[half][pl.ds(k, 1), :]
                a[half][pl.ds(k, 1), :] += jnp.sum(gc[H:H + ts] * _shifted(ue, 1 - k, ts, H), axis=0, keepdims=True)
            o[half][...] = gu.astype(BF16)

    gate, val = (lambda j: j), (lambda j: j + FFN_NCB)
    tiles = [(u, FFN_CB, gate, True), (u, FFN_CB, val, True), (gact, FFN_CB, gate, True)]
    params = [(cw, FFN_CONV_W, FFN_CB, gate), (cw, FFN_CONV_W, FFN_CB, val)]
    gu_gate, gu_val, gw_gate, gw_val = _rowwise("ffn_mid_bwd", fn, S, ts, tiles, params, [(FFN_CB, BF16, gate, D_FF)] * 2,
                                                [(FFN_CONV_W, FFN_CB, gate, D_FF)] * 2, halo=H, ncb=FFN_NCB)
    return (gu_gate, gu_val), jnp.concatenate([gw_gate, gw_val], axis=1)


LD_W = LANES
PAIR_W = 2 * HEAD
N_PAIRS = N_HEADS // 2


def _sub_view(t, d):
    S, C = t.shape
    return t.reshape(S // d, d * C)


def _sub_halo_specs(width, col, blk, hb, nhb):
    per = blk // hb
    return [
        pl.BlockSpec((hb, width), lambda r, i: (jnp.maximum(i * per - 1, 0), col(r))),
        pl.BlockSpec((blk, width), lambda r, i: (i, col(r))),
        pl.BlockSpec((hb, width), lambda r, i: (jnp.minimum((i + 1) * per, nhb - 1), col(r))),
    ]


def _pick_lane(t, lane):
    return jnp.sum(jnp.where(_iota((1, t.shape[1]), 1) == lane, t, 0.0), axis=1, keepdims=True)


def _pair_mask(h2):
    return (_iota((1, PAIR_W), 1) >> HEAD_SHIFT) == h2


def _cat_bf16(a, b, c):
    return jnp.concatenate([a[...], b[...], c[...]], axis=0).astype(BF16)


def _attn_fwd(view, dil):
    L = view.shape[0]
    blk, hb = min(ATT_BLK, L), HALF_BAND
    span = blk + 2 * hb

    def body(q_ref, kp, kc, kn, vp, vc, vn, o_ref, l_ref):
        i = pl.program_id(1)
        rel = _iota((blk, span), 1) - hb - _iota((blk, span), 0)
        kpos = i * blk - hb + _iota((blk, span), 1)
        valid = (jnp.abs(rel) <= hb) & (kpos >= 0) & (kpos < L)
        dist = jnp.abs(rel).astype(F32) * float(dil)
        q, k, v = q_ref[...].astype(BF16), _cat_bf16(kp, kc, kn), _cat_bf16(vp, vc, vn)
        lse = jnp.zeros((blk, LD_W), F32)
        for pr in range(N_PAIRS):
            sl = slice(pr * PAIR_W, (pr + 1) * PAIR_W)
            qp, kpair, vpair = q[:, sl], k[:, sl], v[:, sl]
            o = jnp.zeros((blk, PAIR_W), F32)
            for h2 in range(2):
                h, mask = 2 * pr + h2, _pair_mask(h2)
                s = _dot_nt(jnp.where(mask, qp, jnp.zeros_like(qp)), kpair) * (HEAD ** -0.5) - ALIBI_SLOPES[h] * dist
                s = jnp.where(valid, s, MASK_VALUE)
                m = jnp.max(s, axis=1, keepdims=True)
                p = jnp.exp(s - m)
                l = jnp.sum(p, axis=1, keepdims=True)
                o = jnp.where(mask, _dot_nn(p.astype(BF16), vpair) / l, o)
                lse = lse + jnp.where(_iota((1, LD_W), 1) == h, m + jnp.log(l), 0.0)
            o_ref[:, sl] = o
        l_ref[...] = lse

    nhb = L // hb
    in_specs = ([pl.BlockSpec((blk, ATT_W), lambda r, i: (i, r * QKV_BLOCKS +QA))]
                + _sub_halo_specs(ATT_W, lambda r: r * QKV_BLOCKS +KA, blk, hb, nhb)
                + _sub_halo_specs(ATT_W, lambda r: r * QKV_BLOCKS +VA, blk, hb, nhb))
    return pl.pallas_call(
        body, name=f"attn_fwd_d{dil}", grid=(dil, L // blk), in_specs=in_specs,
        out_specs=[pl.BlockSpec((blk, ATT_W), lambda r, i: (i, r)), pl.BlockSpec((blk, LD_W), lambda r, i: (i, r))],
        out_shape=[jax.ShapeDtypeStruct((L, dil * ATT_W), F32), jax.ShapeDtypeStruct((L, dil * LD_W), F32)],
        compiler_params=_cparams(("parallel", "parallel")),
    )(*([view] * 7))


def _attn_bwd(pview, gview, lview, dil):
    L = pview.shape[0]
    blk, hb = min(ATT_BLK, L), HALF_BAND
    span = blk + 2 * hb
    scale = HEAD ** -0.5

    def body(qp, qc, qn, kp, kc, kn, vp, vc, vn, gp, gc, gn, lp, lc, ln, dq_ref, dk_ref, dv_ref):
        i = pl.program_id(1)
        l = lc[...]
        le = jnp.concatenate([lp[...], l, ln[...]], axis=0)
        rel_q = _iota((blk, span), 1) - hb - _iota((blk, span), 0)
        kpos = i * blk - hb + _iota((blk, span), 1)
        valid_q = (jnp.abs(rel_q) <= hb) & (kpos >= 0) & (kpos < L)
        dist_q = jnp.abs(rel_q).astype(F32) * float(dil)
        rel_k = _iota((span, blk), 1) + hb - _iota((span, blk), 0)
        qpos = i * blk - hb + _iota((span, blk), 0)
        valid_k = (jnp.abs(rel_k) <= hb) & (qpos >= 0) & (qpos < L)
        dist_k = jnp.abs(rel_k).astype(F32) * float(dil)
        q_all, k_all, v_all, g_all = qc[...].astype(BF16), kc[...].astype(BF16), vc[...].astype(BF16), gc[...]
        qe_all, ke_all, ve_all = _cat_bf16(qp, qc, qn), _cat_bf16(kp, kc, kn), _cat_bf16(vp, vc, vn)
        ge_all = _cat_bf16(gp, gc, gn)
        for pr in range(N_PAIRS):
            sl = slice(pr * PAIR_W, (pr + 1) * PAIR_W)
            q, k, v, g = q_all[:, sl], k_all[:, sl], v_all[:, sl], g_all[:, sl]
            qe, ke, ve, ge = qe_all[:, sl], ke_all[:, sl], ve_all[:, sl], ge_all[:, sl]
            dq = jnp.zeros((blk, PAIR_W), F32)
            dk = jnp.zeros((blk, PAIR_W), F32)
            dv = jnp.zeros((blk, PAIR_W), F32)
            for h2 in range(2):
                h, mask = 2 * pr + h2, _pair_mask(h2)
                only = lambda t: jnp.where(mask, t, jnp.zeros_like(t))
                s = _dot_nt(only(q), ke) * scale - ALIBI_SLOPES[h] * dist_q
                p = jnp.where(valid_q, jnp.exp(s - _pick_lane(l, h)), 0.0)
                ds = p * (_dot_nt(only(g), ve) - _pick_lane(l, 8 + h))
                dq = jnp.where(mask, _dot_nn(ds.astype(BF16), ke), dq)
                s = _dot_nt(only(qe), k) * scale - ALIBI_SLOPES[h] * dist_k
                p = jnp.where(valid_k, jnp.exp(s - _pick_lane(le, h)), 0.0)
                dv = jnp.where(mask, _dot_tn(p.astype(BF16), ge), dv)
                ds = p * (_dot_nt(only(ge), v) - _pick_lane(le, 8 + h))
                dk = jnp.where(mask, _dot_tn(ds.astype(BF16), qe), dk)
            dq_ref[:, sl] = (dq * scale).astype(BF16)
            dk_ref[:, sl] = (dk * scale).astype(BF16)
            dv_ref[:, sl] = dv.astype(BF16)

    nhb = L // hb
    in_specs = (_sub_halo_specs(ATT_W, lambda r: r * QKV_BLOCKS +QA, blk, hb, nhb)
                + _sub_halo_specs(ATT_W, lambda r: r * QKV_BLOCKS +KA, blk, hb, nhb)
                + _sub_halo_specs(ATT_W, lambda r: r * QKV_BLOCKS +VA, blk, hb, nhb)
                + _sub_halo_specs(ATT_W, lambda r: r, blk, hb, nhb) + _sub_halo_specs(LD_W, lambda r: r, blk, hb, nhb))
    o_spec = pl.BlockSpec((blk, ATT_W), lambda r, i: (i, r))
    return pl.pallas_call(
        body, name=f"attn_bwd_d{dil}", grid=(dil, L // blk), in_specs=in_specs,
        out_specs=[o_spec] * 3, out_shape=[jax.ShapeDtypeStruct((L, dil * ATT_W), BF16)] * 3,
        compiler_params=_cparams(("parallel", "parallel")),
    )(*([pview] * 9 + [gview] * 3 + [lview] * 3))


def _head_expand(t):
    e = ((_iota((LD_W, ATT_W), 1) >> HEAD_SHIFT) == _iota((LD_W, ATT_W), 0)).astype(F32)
    return _dot_nn(t, e, HP)


def _dil(d):
    return d if d > 1 else False


def _qkv_views(proj):
    S, w = proj.shape[0], QKV_BLOCKS * ATT_W

    def fn(i, vals, p, o, a):
        for k, d in enumerate(DILATIONS):
            o[k][...] = vals[0].astype(o[k].dtype)

    outs = [(w, BF16, _c0, w) + ((d,) if d > 1 else ()) for d in DILATIONS]
    return _rowwise("qkv_views", fn, S, 512, [(proj, w, _c0, False)], (), outs)


def _attn_merge(os, ls):
    S = os[0].shape[0]

    def fn(i, vals, p, o, a):
        o3, l3 = vals[:3], vals[3:]
        m = jnp.maximum(jnp.maximum(l3[0], l3[1]), l3[2])
        e3 = [jnp.exp(l - m) for l in l3]
        den = e3[0] + e3[1] + e3[2]
        out = jnp.zeros((o3[0].shape[0], ATT_W), F32)
        for ob, e in zip(o3, e3):
            out = out + _head_expand(e / den) * ob
        o[0][...] = out
        o[1][...] = m + jnp.log(den)

    tiles = ([(t, ATT_W, _c0, _dil(d)) for t, d in zip(os, DILATIONS)]
             + [(t, LD_W, _c0, _dil(d)) for t, d in zip(ls, DILATIONS)])
    return _rowwise("attn_merge", fn, S, 512, tiles, (), [(ATT_W, F32, _c0, ATT_W), (LD_W, F32, _c0, LD_W)])


def _attn_bwd_prep(gmixed, att, lse):
    S = att.shape[0]
    n = len(DILATIONS)

    def fn(i, vals, p, o, a):
        g, out, lse_row = vals
        place_d = ((_iota((ATT_W, LD_W), 0) >> HEAD_SHIFT) + 8 == _iota((ATT_W, LD_W), 1)).astype(F32)
        ld = jnp.where(_iota((1, LD_W), 1) < 8, lse_row, 0.0) + _dot_nn(g * out, place_d, HP)
        for k in range(n):
            o[k][...] = g.astype(o[k].dtype)
            o[n + k][...] = ld

    tiles = [(gmixed, ATT_W, _c0, False), (att, ATT_W, _c0, False), (lse, LD_W, _c0, False)]
    outs = ([(ATT_W, BF16, _c0, ATT_W) + ((d,) if d > 1 else ()) for d in DILATIONS]
            + [(LD_W, F32, _c0, LD_W) + ((d,) if d > 1 else ()) for d in DILATIONS])
    res = _rowwise("attn_bwd_prep", fn, S, 512, tiles, (), outs)
    return res[:n], res[n:]


def _sum3_bf16(views, S, width):
    def fn(i, vals, p, o, a):
        o[0][...] = (vals[0].astype(F32) + vals[1].astype(F32) + vals[2].astype(F32)).astype(BF16)

    tiles = [(t, width, _c0, _dil(d)) for t, d in zip(views, DILATIONS)]
    return _rowwise("sum3", fn, S, 512, tiles, (), [(width, BF16, _c0, width)])[0]


def _block_diag_mask():
    return ((_iota((REC_W, REC_W), 0) >> HEAD_SHIFT) == (_iota((REC_W, REC_W), 1) >> HEAD_SHIFT)).astype(F32)


def _hgrn_chunk(qr, z, iv, lb, st, reverse, precise):
    C = REC_CHUNK
    r, c = _iota((C, C), 0), _iota((C, C), 1)
    t_cum = (c >= r) if reverse else (c <= r)
    mid_row, last_row = (C // 2, 0) if reverse else (C // 2 - 1, C - 1)
    f = lb + (1.0 - lb) * jax.nn.sigmoid(z)
    logf = jnp.log(jnp.maximum(f, F_TINY))
    k = (1.0 - lb) * jax.nn.sigmoid(-z)
    q = qr * jax.nn.sigmoid(qr)
    b = _dot_nn(t_cum.astype(F32), logf, HP)
    row = _iota((C, 1), 0)
    bm = jnp.sum(jnp.where(row == mid_row, b, 0.0), axis=0, keepdims=True)
    bl = jnp.sum(jnp.where(row == last_row, b, 0.0), axis=0, keepdims=True)
    qt = q * jnp.exp(jnp.minimum(b - bm, EXP_CLAMP))
    kt = k * jnp.exp(jnp.minimum(bm - b, EXP_CLAMP))
    qh = q * jnp.exp(b)
    kh = k * jnp.exp(bl - b)
    lam = jnp.exp(bl)
    bd = ((_iota((PAIR_W, PAIR_W), 0) >> HEAD_SHIFT) == (_iota((PAIR_W, PAIR_W), 1) >> HEAD_SHIFT)).astype(F32)
    s_in = _iota((C, PAIR_W), 1) & (HEAD - 1)
    t_in = _iota((C, PAIR_W), 0)
    tri = (s_in >= t_in) if reverse else (s_in <= t_in)
    twice = lambda t: jnp.concatenate([t, t], axis=0)
    outs, states = [], []
    for pr in range(N_PAIRS):
        sl = slice(pr * PAIR_W, (pr + 1) * PAIR_W)
        k_bd = twice(kt[:, sl]) * bd
        v_bd = (twice(iv[:, sl]) * bd).astype(BF16)
        st_bd = twice(st[:, sl]) * bd
        if precise:
            scores = _dot_nt(qt[:, sl], k_bd, lax.Precision.HIGH)
        else:
            scores = _dot_nt(qt[:, sl].astype(BF16), k_bd.astype(BF16))
        a = jnp.where(tri, scores, 0.0)
        outs.append(_dot_nn(a.astype(BF16), v_bd) + _dot_nt(qh[:, sl].astype(BF16), st_bd.astype(BF16)))
        kv = _dot_tn(iv[:, sl].astype(BF16), kh[:, sl].astype(BF16))
        st_bd = st_bd * lam[:, sl] + kv * bd
        states.append(st_bd[0:HEAD] + st_bd[HEAD:PAIR_W])
    return jnp.concatenate(outs, axis=1), jnp.concatenate(states, axis=1)


REC_CHUNKS_PER_STEP = 8
REC_ROWS = REC_CHUNKS_PER_STEP * REC_CHUNK


def _hgrn_specs(order, blocks):
    return [pl.BlockSpec((REC_ROWS, REC_W), lambda i, b=b: (order(i), b)) for b in blocks]


def _chunk_rows(j):
    return pl.ds(pl.multiple_of(j * REC_CHUNK, REC_CHUNK), REC_CHUNK)


def _hgrn_fwd(proj, lb, z_blk, reverse):
    S = proj.shape[0]
    nb = S // REC_ROWS
    order = (lambda i: nb - 1 - i) if reverse else (lambda i: i)

    def body(q_ref, z_ref, v_ref, lb_ref, o_ref, st_ref, st_scr):
        @pl.when(pl.program_id(0) == 0)
        def _():
            st_scr[...] = jnp.zeros_like(st_scr)

        def step(t, carry):
            j = REC_CHUNKS_PER_STEP - 1 - t if reverse else t
            rows = _chunk_rows(j)
            st = st_scr[...]
            st_ref[j] = st
            o, st_new = _hgrn_chunk(q_ref[rows, :], z_ref[rows, :], v_ref[rows, :], lb_ref[...], st, reverse, False)
            o_ref[rows, :] = o
            st_scr[...] = st_new
            return carry

        lax.fori_loop(0, REC_CHUNKS_PER_STEP, step, 0, unroll=2)

    return pl.pallas_call(
        body, name="hgrn_rev_fwd" if reverse else "hgrn_fwd_fwd", grid=(nb,),
        in_specs=_hgrn_specs(order, (QR, z_blk, IR)) + [pl.BlockSpec((1, REC_W), lambda i: (0, 0))],
        out_specs=[pl.BlockSpec((REC_ROWS, REC_W), lambda i: (order(i), 0)),
                   pl.BlockSpec((REC_CHUNKS_PER_STEP, HEAD, REC_W), lambda i: (order(i), 0, 0))],
        out_shape=[jax.ShapeDtypeStruct((S, REC_W), F32), jax.ShapeDtypeStruct((S // REC_CHUNK, HEAD, REC_W), F32)],
        scratch_shapes=[pltpu.VMEM((HEAD, REC_W), F32)],
        compiler_params=_cparams(("arbitrary",)),
    )(proj, proj, proj, lb)


def _hgrn_bwd(proj, lb, states, go, z_blk, reverse):
    S = proj.shape[0]
    nb = S // REC_ROWS
    order = (lambda i: i) if reverse else (lambda i: nb - 1 - i)

    def body(q_ref, z_ref, v_ref, lb_ref, st_ref, go_ref, gq_ref, gz_ref, gv_ref, glb_ref, gst_scr):
        @pl.when(pl.program_id(0) == 0)
        def _():
            gst_scr[...] = jnp.zeros_like(gst_scr)
            glb_ref[...] = jnp.zeros_like(glb_ref)

        chunk = functools.partial(_hgrn_chunk, reverse=reverse, precise=True)

        def step(t, carry):
            j = t if reverse else REC_CHUNKS_PER_STEP - 1 - t
            rows = _chunk_rows(j)
            _, vjp = jax.vjp(chunk, q_ref[rows, :], z_ref[rows, :], v_ref[rows, :], lb_ref[...], st_ref[j])
            gq, gz, gv, glb, gst = vjp((go_ref[rows, :], gst_scr[...]))
            gq_ref[rows, :] = gq
            gz_ref[rows, :] = gz
            gv_ref[rows, :] = gv
            glb_ref[...] += glb
            gst_scr[...] = gst
            return carry

        lax.fori_loop(0, REC_CHUNKS_PER_STEP, step, 0, unroll=2)

    row_spec = pl.BlockSpec((REC_ROWS, REC_W), lambda i: (order(i), 0))
    return pl.pallas_call(
        body, name="hgrn_rev_bwd" if reverse else "hgrn_fwd_bwd", grid=(nb,),
        in_specs=(_hgrn_specs(order, (QR, z_blk, IR)) + [pl.BlockSpec((1, REC_W), lambda i: (0, 0))]
                  + [pl.BlockSpec((REC_CHUNKS_PER_STEP, HEAD, REC_W), lambda i: (order(i), 0, 0)), row_spec]),
        out_specs=[row_spec] * 3 + [pl.BlockSpec((1, REC_W), lambda i: (0, 0))],
        out_shape=[jax.ShapeDtypeStruct((S, REC_W), F32)] * 3 + [jax.ShapeDtypeStruct((1, REC_W), F32)],
        scratch_shapes=[pltpu.VMEM((HEAD, REC_W), F32)],
        compiler_params=_cparams(("arbitrary",)),
    )(proj, proj, proj, lb, states, go)


def _hgrn_post_f(of, ob, gr, rnw):
    o = of + ob
    ms = _dot_nn(o * o, _block_diag_mask() * (1.0 / HEAD), HP)
    return o * lax.rsqrt(ms + EPS) * rnw * (gr * jax.nn.sigmoid(gr))


def _hgrn_post_fwd(of, ob, proj, rnw):
    S = of.shape[0]

    def fn(i, vals, p, o, a):
        o[0][...] = _hgrn_post_f(vals[0], vals[1], vals[2], p[0][...]).astype(BF16)

    tiles = [(of, REC_W, _c0, False), (ob, REC_W, _c0, False), (proj, REC_W, lambda j: GR, False)]
    return _rowwise("hgrn_post_fwd", fn, S, 512, tiles, _row_params(rnw), [(REC_W, BF16, _c0, REC_W)])[0]


def _hgrn_post_bwd(of, ob, proj, gmixed, rnw):
    S = of.shape[0]

    def fn(i, vals, p, o, a):
        _, vjp = jax.vjp(_hgrn_post_f, vals[0], vals[1], vals[2], p[0][...])
        go, _, ggr, grnw = vjp(vals[3])
        o[0][...] = go
        o[1][...] = ggr
        a[0][...] += grnw

    tiles = [(of, REC_W, _c0, False), (ob, REC_W, _c0, False), (proj, REC_W, lambda j: GR, False),
             (gmixed, REC_W, lambda j: 1, False)]
    return _rowwise("hgrn_post_bwd", fn, S, 512, tiles, _row_params(rnw),
                    [(REC_W, F32, _c0, REC_W), (REC_W, F32, _c0, REC_W)], [(1, REC_W, _c0, REC_W)])


def _lower_bounds_f(g0, g1):
    m = jnp.maximum(g0, g1)
    e0, e1 = jnp.exp(g0 - m), jnp.exp(g1 - m)
    return e1 / (e0 + e1)


def _adamw(name, w, m, v, gparts):
    R, C = w.shape
    P = gparts.shape[0]
    tr = R if R * C * 4 * (P + 7) * 2 <= VMEM_LIMIT_BYTES // 2 else _pick(R, (256, 128, 64, 32, 16, 8))

    def body(w_ref, m_ref, v_ref, gp_ref, g_ref, d_ref, nm_ref, nv_ref):
        g = gp_ref[0].astype(F32)
        for p in range(1, P):
            g = g + gp_ref[p].astype(F32)
        w_ = w_ref[...]
        nm = ADAM_B1 * m_ref[...] + (1.0 - ADAM_B1) * g
        nv = ADAM_B2 * v_ref[...] + (1.0 - ADAM_B2) * jnp.square(g)
        m_hat = nm / (1.0 - ADAM_B1 ** ADAM_STEP)
        v_hat = nv / (1.0 - ADAM_B2 ** ADAM_STEP)
        g_ref[...] = g
        d_ref[...] = -ADAM_LR * (m_hat / (jnp.sqrt(v_hat) + ADAM_EPS) + ADAM_WD * w_)
        nm_ref[...] = nm
        nv_ref[...] = nv

    spec = pl.BlockSpec((tr, C), lambda i: (i, 0))
    return pl.pallas_call(
        body, name=name, grid=(R // tr,),
        in_specs=[spec, spec, spec, pl.BlockSpec((P, tr, C), lambda i: (0, i, 0))],
        out_specs=[spec] * 4, out_shape=[jax.ShapeDtypeStruct((R, C), F32)] * 4,
        compiler_params=_cparams(("parallel",)),
    )(w, m, v, gparts)


def _place():
    return lax.axis_index("x"), lax.axis_index("y"), lax.axis_index("c")


def _index_of(p):
    return 4 * p[0] + 2 * p[1] + p[2]


def _allgather_small(name, rows):
    m_per, n = rows.shape

    def body(x_ref, out_ref, send_sems, recv_sems, local_sem):
        x, y, c = _place()
        me, sibling = (x, y, c), (x, y, 1 - c)
        chips = [(1 - x, y), (x, 1 - y), (1 - x, 1 - y)]

        def blk(p):
            return out_ref.at[pl.ds(_index_of(p) * m_per, m_per), :]

        def copy(k, block, to, src=None):
            return pltpu.make_async_remote_copy(
                src_ref=blk(block) if src is None else src, dst_ref=blk(block),
                send_sem=send_sems.at[k], recv_sem=recv_sems.at[k], device_id=to, device_id_type=MESH)

        mine = pltpu.make_async_copy(x_ref, blk(me), local_sem)
        mine.start()
        first = [copy(0, me, sibling, src=x_ref)]
        first += [copy(1 + j, me, (*chip, c), src=x_ref) for j, chip in enumerate(chips)]
        for cp in first:
            cp.start()
        passed = [copy(4 + j, (*chip, c), sibling) for j, chip in enumerate(chips)]
        for j, chip in enumerate(chips):
            copy(1 + j, (*chip, c), me).wait_recv()
            passed[j].start()
        copy(0, sibling, me).wait_recv()
        for j, chip in enumerate(chips):
            copy(4 + j, (*chip, 1 - c), me).wait_recv()
        for cp in first + passed:
            cp.wait_send()
        mine.wait()

    return pl.pallas_call(
        body, name=name,
        out_shape=jax.ShapeDtypeStruct((N_DEV * m_per, n), rows.dtype),
        in_specs=[pl.BlockSpec(memory_space=pltpu.VMEM)],
        out_specs=pl.BlockSpec(memory_space=pltpu.VMEM),
        scratch_shapes=[pltpu.SemaphoreType.DMA((7,)), pltpu.SemaphoreType.DMA((7,)), pltpu.SemaphoreType.DMA],
        compiler_params=_cparams(),
    )(rows)


def _allgather_big(name, arrs):
    na = len(arrs)

    def body(*refs):
        ins, outs = refs[:na], refs[na:2 * na]
        send_sems, recv_sems, local_sems = refs[2 * na:]
        x, y, c = _place()
        me, sibling = (x, y, c), (x, y, 1 - c)
        chips = [(1 - x, y), (x, 1 - y), (1 - x, 1 - y)]

        def copy(a, k, block, to, src=None):
            dst = outs[a].at[_index_of(block)]
            return pltpu.make_async_remote_copy(
                src_ref=dst if src is None else src, dst_ref=dst,
                send_sem=send_sems.at[a, k], recv_sem=recv_sems.at[a, k], device_id=to, device_id_type=MESH)

        mine = [pltpu.make_async_copy(ins[a], outs[a].at[_index_of(me)], local_sems.at[a]) for a in range(na)]
        for cp in mine:
            cp.start()
        sent = []
        for a in range(na):
            sent.append(copy(a, 0, me, sibling, src=ins[a]))
            sent += [copy(a, 1 + j, me, (*chip, c), src=ins[a]) for j, chip in enumerate(chips)]
        for cp in sent:
            cp.start()
        for j, chip in enumerate(chips):
            for a in range(na):
                copy(a, 1 + j, (*chip, c), me).wait_recv()
                fwd = copy(a, 4 + j, (*chip, c), sibling)
                fwd.start()
                sent.append(fwd)
        for a in range(na):
            copy(a, 0, sibling, me).wait_recv()
            for j, chip in enumerate(chips):
                copy(a, 4 + j, (*chip, 1 - c), me).wait_recv()
        for cp in sent:
            cp.wait_send()
        for cp in mine:
            cp.wait()

    any_spec = pl.BlockSpec(memory_space=pl.ANY)
    return pl.pallas_call(
        body, name=name,
        out_shape=[jax.ShapeDtypeStruct((N_DEV,) + a.shape, a.dtype) for a in arrs],
        in_specs=[any_spec] * na, out_specs=[any_spec] * na,
        scratch_shapes=[pltpu.SemaphoreType.DMA((na, 7)), pltpu.SemaphoreType.DMA((na, 7)), pltpu.SemaphoreType.DMA((na,))],
        compiler_params=_cparams(),
    )(*arrs)


N_CHIPS = 4


def _scatter_to_sibling(name, parts):
    na = len(parts)

    def body(*refs):
        ins, outs = refs[:na], refs[na:2 * na]
        send_sems, recv_sems = refs[2 * na:]
        x, y, c = _place()
        sibling = (x, y, 1 - c)
        sent = []
        for a in range(na):
            for q in range(N_CHIPS):
                sent.append(pltpu.make_async_remote_copy(
                    src_ref=ins[a].at[2 * q + (1 - c)], dst_ref=outs[a].at[q],
                    send_sem=send_sems.at[a, q], recv_sem=recv_sems.at[a, q], device_id=sibling, device_id_type=MESH))
        for cp in sent:
            cp.start()
        for cp in sent:
            cp.wait_recv()
        for cp in sent:
            cp.wait_send()

    any_spec = pl.BlockSpec(memory_space=pl.ANY)
    return pl.pallas_call(
        body, name=name,
        out_shape=[jax.ShapeDtypeStruct((N_CHIPS,) + p.shape[1:], p.dtype) for p in parts],
        in_specs=[any_spec] * na, out_specs=[any_spec] * na,
        scratch_shapes=[pltpu.SemaphoreType.DMA((na, N_CHIPS)), pltpu.SemaphoreType.DMA((na, N_CHIPS))],
        compiler_params=_cparams(),
    )(*parts)


def _pair_sum(name, parts, recv):
    _, R, C = parts.shape
    tr = _pick(R, (256, 128, 64, 32, 16))

    def body(p_ref, r_ref, o_ref):
        c = lax.axis_index("c")
        o_ref[...] = (p_ref[c].astype(F32) + r_ref[...].astype(F32)).astype(BF16)

    return pl.pallas_call(
        body, name=name, grid=(N_CHIPS, R // tr),
        in_specs=[pl.BlockSpec((None, 2, tr, C), lambda q, i: (q, 0, i, 0)), pl.BlockSpec((None, tr, C), lambda q, i: (q, i, 0))],
        out_specs=pl.BlockSpec((None, tr, C), lambda q, i: (q, i, 0)),
        out_shape=jax.ShapeDtypeStruct((N_CHIPS, R, C), BF16),
        compiler_params=_cparams(("parallel", "parallel")),
    )(parts.reshape(N_CHIPS, 2, R, C), recv)


def _scatter_to_chips(name, sums):
    na = len(sums)

    def body(*refs):
        ins, outs = refs[:na], refs[na:2 * na]
        send_sems, recv_sems, local_sems = refs[2 * na:]
        x, y, c = _place()
        me = 2 * x + y
        chips = [(1 - x, y), (x, 1 - y), (1 - x, 1 - y)]
        mine = [pltpu.make_async_copy(ins[a].at[me], outs[a].at[me], local_sems.at[a]) for a in range(na)]
        for cp in mine:
            cp.start()
        sent = []
        for a in range(na):
            for k, (qx, qy) in enumerate(chips):
                sent.append(pltpu.make_async_remote_copy(
                    src_ref=ins[a].at[2 * qx + qy], dst_ref=outs[a].at[me],
                    send_sem=send_sems.at[a, k], recv_sem=recv_sems.at[a, k], device_id=(qx, qy, c), device_id_type=MESH))
        for cp in sent:
            cp.start()
        for a in range(na):
            for k, (qx, qy) in enumerate(chips):
                slot = outs[a].at[2 * qx + qy]
                pltpu.make_async_remote_copy(
                    src_ref=slot, dst_ref=slot, send_sem=send_sems.at[a, k], recv_sem=recv_sems.at[a, k],
                    device_id=(qx, qy, c), device_id_type=MESH).wait_recv()
        for cp in sent:
            cp.wait_send()
        for cp in mine:
            cp.wait()

    any_spec = pl.BlockSpec(memory_space=pl.ANY)
    return pl.pallas_call(
        body, name=name,
        out_shape=[jax.ShapeDtypeStruct(p.shape, p.dtype) for p in sums],
        in_specs=[any_spec] * na, out_specs=[any_spec] * na,
        scratch_shapes=[pltpu.SemaphoreType.DMA((na, 3)), pltpu.SemaphoreType.DMA((na, 3)), pltpu.SemaphoreType.DMA((na,))],
        compiler_params=_cparams(),
    )(*sums)


def _gather_row(name, vec, width):
    n = vec.shape[0]
    rows = jnp.pad(vec, (0, width - n)).reshape(SUBLANES_F32, width // SUBLANES_F32)
    return _allgather_small(name, rows).reshape(N_DEV, width)[:, :n]


def _layer_fwd(x, mod, w):
    sh1, sc1, g1, sh2, sc2, g2 = [mod[i:i + 1] for i in range(N_MOD)]
    S = x.shape[0]
    h1 = _normmod_fwd(x, w["norm1_w"], sc1, sh1)
    proj = _matmul("proj_in", h1, w["w_in"], "nn")
    a1, a_out = _conv_a_fwd(proj, w["conv_a_w"], w["conv_a_b"], w["ln_a_w"], w["ln_a_b"])
    qkv = _qkv_views(proj)
    os, ls = zip(*[_attn_fwd(v, dil) for v, dil in zip(qkv, DILATIONS)])
    att, lse = _attn_merge(os, ls)
    of, st_f = _hgrn_fwd(proj, w["lb_f"], ZF, False)
    ob, st_b = _hgrn_fwd(proj, w["lb_b"], ZB, True)
    rec = _hgrn_post_fwd(of, ob, proj, w["rec_norm_w"])
    mixed = jnp.concatenate([att.astype(BF16), rec, a_out], axis=1)
    y1 = _matmul("proj_out", mixed, w["w_out"], "nn")
    x2, h2 = _gate_add_normmod(x, y1, g1, w["norm2_w"], sc2, sh2)
    u = _matmul("ffn_up", h2, w["w_up"], "nn")
    act = _ffn_mid_fwd(u, w["conv_f_w"])
    y2 = _matmul("ffn_down", act, w["w_down"], "nn")
    x3 = _gate_add(x2, y2, g2)
    saved = dict(x=x, h1=h1, proj=proj, a1=a1, qkv=qkv, att=att, lse=lse, of=of, ob=ob, st_f=st_f, st_b=st_b,
                 mixed=mixed, y1=y1, x2=x2, h2=h2, u=u, act=act, y2=y2)
    return x3, saved


def _layer_bwd(gx3, mod, w, s):
    sh1, sc1, g1, sh2, sc2, g2 = [mod[i:i + 1] for i in range(N_MOD)]
    S = gx3.shape[0]
    g = {}
    gy2, gg2 = _gate_bwd(gx3, s["y2"], g2)
    gact = _matmul("ffn_down_dx", gy2, w["w_down"], "nt")
    g["w_down"] = _matmul("ffn_down_dw", s["act"], gy2, "tn")
    gu, g["conv_f_w"] = _ffn_mid_bwd(s["u"], gact, w["conv_f_w"])
    gh2 = _matmul_nt_pieces("ffn_up_dx", gu, (w["w_up"][:, :D_FF], w["w_up"][:, D_FF:]))
    g["w_up"] = jnp.concatenate([_matmul("ffn_up_dw", s["h2"], t, "tn") for t in gu], axis=1)
    gx2, gy1, g["norm2_w"], gsc2, gsh2, gg1 = _normmod_gate_bwd(s["x2"], gh2, gx3, w["norm2_w"], sc2, sh2, s["y1"], g1)
    gmixed = _matmul("proj_out_dx", gy1, w["w_out"], "nt")
    g["w_out"] = _matmul("proj_out_dw", s["mixed"], gy1, "tn")
    go, ggr, g["rec_norm_w"] = _hgrn_post_bwd(s["of"], s["ob"], s["proj"], gmixed, w["rec_norm_w"])
    gq_f, gz_f, gv_f, g["lb_f"] = _hgrn_bwd(s["proj"], w["lb_f"], s["st_f"], go, ZF, False)
    gq_b, gz_b, gv_b, g["lb_b"] = _hgrn_bwd(s["proj"], w["lb_b"], s["st_b"], go, ZB, True)
    dos, lds = _attn_bwd_prep(gmixed, s["att"], s["lse"])
    gqkv = zip(*[_attn_bwd(v, do, ld, dil) for v, do, ld, dil in zip(s["qkv"], dos, lds, DILATIONS)])
    gq_a, gk_a, gv_a = [_sum3_bf16(lst, S, ATT_W) for lst in gqkv]
    gav, gag, gcw, g["conv_a_b"], g["ln_a_w"], g["ln_a_b"] = _conv_a_bwd(
        s["proj"], s["a1"], gmixed, w["conv_a_w"], w["ln_a_w"], w["ln_a_b"])
    g["conv_a_w"] = gcw[:CONV_W]
    gproj = jnp.concatenate([gq_a, gk_a, gv_a, (gq_f + gq_b).astype(BF16), gz_f.astype(BF16), gz_b.astype(BF16),
                             (gv_f + gv_b).astype(BF16), ggr.astype(BF16), gav, gag,
                             jnp.zeros((S, IN_COLS_PAD - IN_COLS), BF16)], axis=1)
    gh1 = _matmul("proj_in_dx", gproj, w["w_in"], "nt")
    g["w_in"] = _matmul("proj_in_dw", s["h1"], gproj, "tn")
    gx, g["norm1_w"], gsc1, gsh1 = _normmod_bwd(s["x"], gh1, gx2, w["norm1_w"], sc1, sh1)
    gmod = jnp.concatenate([gsh1, gsc1, gg1, gsh2, gsc2, gg2], axis=0)
    return gx, gmod, g


def _permute_in_cols(t):
    pad = jnp.zeros(t.shape[:-1] + (IN_COLS_PAD - IN_COLS,), t.dtype)
    return jnp.concatenate([t[..., CONV_COLS:], t[..., :CONV_COLS], pad], axis=-1)


def _unpermute_in_cols(t):
    return jnp.concatenate([t[..., IN_COLS - CONV_COLS:IN_COLS], t[..., :IN_COLS - CONV_COLS]], axis=-1)


def _cols_from_gathered(t, lead):
    nd = t.ndim
    perm = tuple(range(1, nd - 1)) + (0, nd - 1)
    t = t.transpose(perm)
    return t.reshape(t.shape[:-2] + (t.shape[-2] * t.shape[-1],))


def _cols_to_parts(t):
    L, R, C = t.shape
    return t.reshape(L * R, N_DEV, C // N_DEV).transpose(1, 0, 2)


SMALL_REPL = (("norm1_w", D), ("conv_a_b", CONV_CH), ("ln_a_w", CONV_CH), ("ln_a_b", CONV_CH),
              ("rec_norm_w", REC_W), ("norm2_w", D))


def kernel(x, c, w_ada, b_ada, norm1_w, w_in, conv_a_w, conv_a_b, ln_a_w, ln_a_b, lb_gamma, rec_norm_w, w_out, norm2_w, w_up, conv_f_w, w_down, final_norm_w, loss_target, m_w_ada, m_b_ada, m_norm1_w, m_w_in, m_conv_a_w, m_conv_a_b, m_ln_a_w, m_ln_a_b, m_lb_gamma, m_rec_norm_w, m_w_out, m_norm2_w, m_w_up, m_conv_f_w, m_w_down, m_final_norm_w, v_w_ada, v_b_ada, v_norm1_w, v_w_in, v_conv_a_w, v_conv_a_b, v_ln_a_w, v_ln_a_b, v_lb_gamma, v_rec_norm_w, v_w_out, v_norm2_w, v_w_up, v_conv_f_w, v_w_down, v_final_norm_w):
    px, py, pc = _place()
    me = _index_of((px, py, pc))
    xs, tgt = x[0], loss_target[0]
    S = xs.shape[0]
    ada_cols = w_ada.shape[2]

    big = [w_in.reshape(DEPTH * D, -1), w_up.reshape(DEPTH * D, -1), w_out.reshape(-1, D), w_down.reshape(-1, D)]
    g_in, g_up, g_out, g_down = _allgather_big("gather_weights", [t.astype(BF16) for t in big])
    w_in_f = _permute_in_cols(_cols_from_gathered(g_in.reshape(N_DEV, DEPTH, D, -1), 1))
    w_up_f = _cols_from_gathered(g_up.reshape(N_DEV, DEPTH, D, -1), 1)
    w_out_f = g_out.reshape(N_DEV, DEPTH, D // N_DEV, D).transpose(1, 0, 2, 3).reshape(DEPTH, D, D)
    w_out_f = jnp.concatenate([w_out_f[:, CONV_CH:], w_out_f[:, :CONV_CH]], axis=1)
    w_down_f = g_down.reshape(N_DEV, DEPTH, D_FF // N_DEV, D).transpose(1, 0, 2, 3).reshape(DEPTH, D_FF, D)

    small_in = jnp.concatenate([c.reshape(-1), conv_a_w.reshape(-1), lb_gamma.reshape(-1), conv_f_w.reshape(-1)])
    gs = _gather_row("gather_small", small_in, 8192)
    o1 = D
    o2 = o1 + conv_a_w.size
    o3 = o2 + lb_gamma.size
    c_all = gs[:, :o1]
    conv_a_f = _cols_from_gathered(gs[:, o1:o2].reshape(N_DEV, DEPTH, CONV_W, -1), 1)
    lb_gamma_f = _cols_from_gathered(gs[:, o2:o3].reshape(N_DEV, DEPTH, 2, -1), 1)
    conv_f_f = _cols_from_gathered(gs[:, o3:].reshape(N_DEV, DEPTH, FFN_CONV_W, -1), 1)
    conv_a_pad = jnp.pad(conv_a_f, ((0, 0), (0, CONV_W_PAD - CONV_W), (0, 0)))

    b_loc = lax.dynamic_slice_in_dim(b_ada, me * ada_cols, ada_cols, axis=1)

    def mod_fn(c_all_, w_, b_):
        cond = c_all_ * jax.nn.sigmoid(c_all_)
        return (jnp.concatenate([_dot_nn(cond, w_[l], HP) + b_[l] for l in range(DEPTH)], axis=1),)

    (mod_part,) = _vmem_call("ada_mod", mod_fn, [c_all, w_ada, b_loc[:, None, :]], [((N_DEV, DEPTH * ada_cols), F32)])
    gm = _allgather_small("gather_mod", mod_part).reshape(N_DEV, N_DEV, DEPTH, ada_cols)
    mod = lax.dynamic_index_in_dim(gm, me, axis=1, keepdims=False)
    mod = mod.transpose(1, 0, 2).reshape(DEPTH, N_MOD, D)

    (lb1,) = _vmem_call("lower_bounds", lambda a, b: (_lower_bounds_f(a, b),), [lb_gamma_f[0], lb_gamma_f[1]], [((2, REC_W), F32)])
    lb4 = jnp.concatenate([jnp.zeros_like(lb1), lb1], axis=0)

    def layer_weights(l):
        row = lambda t: t[l].reshape(1, -1)
        return dict(norm1_w=row(norm1_w), w_in=w_in_f[l], conv_a_w=conv_a_pad[l], conv_a_b=row(conv_a_b), ln_a_w=row(ln_a_w),
                    ln_a_b=row(ln_a_b), lb_f=lb4[2 * l:2 * l + 1], lb_b=lb4[2 * l + 1:2 * l + 2], rec_norm_w=row(rec_norm_w),
                    w_out=w_out_f[l], norm2_w=row(norm2_w), w_up=w_up_f[l], conv_f_w=conv_f_f[l], w_down=w_down_f[l])

    ws = [layer_weights(l) for l in range(DEPTH)]
    h, saved = xs, []
    for l in range(DEPTH):
        h, s = _layer_fwd(h, mod[l], ws[l])
        saved.append(s)
    gh, g_final, loss_row = _loss_head(h, tgt, final_norm_w.reshape(1, D))
    loss = lax.psum(loss_row[0, 0], ("x", "y", "c"))
    gmods, gws = [None] * DEPTH, [None] * DEPTH
    for l in reversed(range(DEPTH)):
        gh, gmods[l], gws[l] = _layer_bwd(gh, mod[l], ws[l], saved[l])
    grad_x = gh[None]

    glb1 = jnp.concatenate([gws[1]["lb_f"], gws[1]["lb_b"]], axis=0)

    def lb_bwd_fn(a, b, g1):
        _, vjp = jax.vjp(_lower_bounds_f, a, b)
        return vjp(g1)

    g_lb_gamma = jnp.stack(_vmem_call("lower_bounds_bwd", lb_bwd_fn, [lb_gamma_f[0], lb_gamma_f[1], glb1], [((2, REC_W), F32)] * 2))
    pieces = [jnp.stack(gmods).reshape(-1)]
    for l in range(DEPTH):
        pieces += [gws[l][n].reshape(-1) for n, _ in SMALL_REPL]
    pieces += [g_final.reshape(-1)]
    pieces += [jnp.stack([gws[l]["conv_a_w"] for l in range(DEPTH)]).reshape(-1), g_lb_gamma.reshape(-1),
               jnp.stack([gws[l]["conv_f_w"] for l in range(DEPTH)]).reshape(-1)]
    small_g = jnp.concatenate(pieces)
    n_small = small_g.shape[0]
    gsm = _gather_row("gather_small_grads", small_g, 71680)
    n_mod = DEPTH * N_MOD * D
    gmod_all = gsm[:, :n_mod].reshape(N_DEV, DEPTH, N_MOD * D)
    gmod_loc = lax.dynamic_slice_in_dim(gmod_all, me * ada_cols, ada_cols, axis=2).transpose(1, 0, 2)

    def small_fn(gsm_, c_all_, gm_):
        cond = c_all_ * jax.nn.sigmoid(c_all_)
        gw = jnp.concatenate([_dot_tn(cond, gm_[l], HP) for l in range(DEPTH)], axis=0)
        return jnp.sum(gsm_, axis=0, keepdims=True), gw

    tot, g_w_ada = _vmem_call("small_grads", small_fn, [gsm, c_all, gmod_loc],
                              [((1, n_small), F32), ((DEPTH * D, ada_cols), F32)])
    tot = tot[0]
    grads = {"w_ada": g_w_ada.reshape(DEPTH, D, ada_cols), "b_ada": tot[:n_mod].reshape(DEPTH, N_MOD * D)}
    pos = n_mod
    per_layer = {n: [] for n, _ in SMALL_REPL}
    for l in range(DEPTH):
        for n, width in SMALL_REPL:
            per_layer[n].append(tot[pos:pos + width])
            pos += width
    for n, _ in SMALL_REPL:
        grads[n] = jnp.stack(per_layer[n])
    grads["final_norm_w"] = tot[pos:pos + D]
    pos += D
    n_ca, n_lb, n_cf = DEPTH * CONV_W * CONV_CH, DEPTH * 2 * REC_W, DEPTH * FFN_CONV_W * 2 * D_FF
    g_ca = tot[pos:pos + n_ca].reshape(DEPTH, CONV_W, CONV_CH)
    g_lb = tot[pos + n_ca:pos + n_ca + n_lb].reshape(DEPTH, 2, REC_W)
    g_cf = tot[pos + n_ca + n_lb:pos + n_ca + n_lb + n_cf].reshape(DEPTH, FFN_CONV_W, 2 * D_FF)
    grads["conv_a_w"] = lax.dynamic_slice_in_dim(g_ca, me * conv_a_w.shape[2], conv_a_w.shape[2], axis=2)
    grads["lb_gamma"] = lax.dynamic_slice_in_dim(g_lb, me * lb_gamma.shape[2], lb_gamma.shape[2], axis=2)
    grads["conv_f_w"] = lax.dynamic_slice_in_dim(g_cf, me * conv_f_w.shape[2], conv_f_w.shape[2], axis=2)

    gw_in = _unpermute_in_cols(jnp.stack([gws[l]["w_in"] for l in range(DEPTH)]))
    gw_up = jnp.stack([gws[l]["w_up"] for l in range(DEPTH)])
    gw_out = jnp.stack([gws[l]["w_out"] for l in range(DEPTH)])
    gw_out = jnp.concatenate([gw_out[:, D - CONV_CH:], gw_out[:, :D - CONV_CH]], axis=1)
    gw_down = jnp.stack([gws[l]["w_down"] for l in range(DEPTH)])
    rows_to_parts = lambda t: t.reshape(DEPTH, N_DEV, -1, D).transpose(1, 0, 2, 3).reshape(N_DEV, -1, D)
    parts = [_cols_to_parts(gw_in), _cols_to_parts(gw_up), rows_to_parts(gw_out), rows_to_parts(gw_down)]
    parts = [t.astype(BF16) for t in parts]
    from_sibling = _scatter_to_sibling("scatter_sibling", parts)
    sums = [_pair_sum("pair_sum", p, r) for p, r in zip(parts, from_sibling)]
    r_in, r_up, r_out, r_down = _scatter_to_chips("scatter_chips", sums)

    given = dict(w_ada=(w_ada, m_w_ada, v_w_ada), b_ada=(b_ada, m_b_ada, v_b_ada), norm1_w=(norm1_w, m_norm1_w, v_norm1_w),
                 w_in=(w_in, m_w_in, v_w_in), conv_a_w=(conv_a_w, m_conv_a_w, v_conv_a_w), conv_a_b=(conv_a_b, m_conv_a_b, v_conv_a_b),
                 ln_a_w=(ln_a_w, m_ln_a_w, v_ln_a_w), ln_a_b=(ln_a_b, m_ln_a_b, v_ln_a_b), lb_gamma=(lb_gamma, m_lb_gamma, v_lb_gamma),
                 rec_norm_w=(rec_norm_w, m_rec_norm_w, v_rec_norm_w), w_out=(w_out, m_w_out, v_w_out),
                 norm2_w=(norm2_w, m_norm2_w, v_norm2_w), w_up=(w_up, m_w_up, v_w_up), conv_f_w=(conv_f_w, m_conv_f_w, v_conv_f_w),
                 w_down=(w_down, m_w_down, v_w_down), final_norm_w=(final_norm_w, m_final_norm_w, v_final_norm_w))
    big_parts = dict(w_in=r_in, w_up=r_up, w_out=r_out, w_down=r_down)
    names = list(given)
    res = {}
    for n in names:
        w_, m_, v_ = given[n]
        shape = w_.shape
        C = shape[-1]
        two_d = lambda t: t.reshape(-1, C)
        gp = big_parts[n] if n in big_parts else two_d(grads[n])[None]
        res[n] = [t.reshape(shape) for t in _adamw("adamw_" + n, two_d(w_), two_d(m_), two_d(v_), gp)]
    return (loss, grad_x, *[res[n][0] for n in names], *[res[n][1] for n in names],
            *[res[n][2] for n in names], *[res[n][3] for n in names])
```

```python
import functools

import jax
import jax.numpy as jnp
from jax import lax
from jax.experimental import pallas as pl
from jax.experimental.pallas import tpu as pltpu

F32 = jnp.float32
BF16 = jnp.bfloat16
HP = lax.Precision.HIGHEST
MESH = pl.DeviceIdType.MESH

N_DEV = 8
D = 1024
DEPTH = 2
CONV_CH = 256
CONV_W = 31
CONV_W_PAD = 32
ATT_W = 384
REC_W = 384
N_HEADS = 6
HEAD = 64
HEAD_SHIFT = 6
HALF_BAND = 64
ATT_BLK = 128
DILATIONS = (1, 4, 16)
ALIBI_SLOPES = tuple(float(2.0 ** (-8.0 * (h + 1) / N_HEADS)) for h in range(N_HEADS))
MASK_VALUE = -1e30
REC_CHUNK = 64
EXP_CLAMP = 80.0
F_TINY = 1e-30
IN_COLS = 3584
IN_COLS_PAD = IN_COLS
QKV_BLOCKS = 3
D_FF = 2816
FFN_CONV_W = 3
N_MOD = 6
EPS = 1e-6
ADAM_LR, ADAM_B1, ADAM_B2, ADAM_EPS, ADAM_WD, ADAM_STEP = 0.001, 0.9, 0.999, 1e-08, 0.01, 10

VMEM_LIMIT_BYTES = 56 * 1024 * 1024
SUBLANES_F32 = 8
LANES = 128

QA, KA, VA, QR, ZF, ZB, IR, GR = range(8)
AV_BLK, AG_BLK = 12, 13
CONV_COLS = 2 * CONV_CH


def _cparams(sem=None):
    kw = dict(vmem_limit_bytes=VMEM_LIMIT_BYTES)
    if sem is not None:
        kw["dimension_semantics"] = sem
    return pltpu.CompilerParams(**kw)


def _iota(shape, dim):
    return lax.broadcasted_iota(jnp.int32, shape, dim)


def _dot(a, b, dims, precision=None):
    return lax.dot_general(a, b, (dims, ((), ())), precision=precision, preferred_element_type=F32)


def _dot_nn(a, b, precision=None):
    return _dot(a, b, ((1,), (0,)), precision)


def _dot_nt(a, b, precision=None):
    return _dot(a, b, ((1,), (1,)), precision)


def _dot_tn(a, b, precision=None):
    return _dot(a, b, ((0,), (0,)), precision)


def _c0(j):
    return 0


def _pick(n, cands):
    for c in cands:
        if n % c == 0:
            return c
    return n


MATMUL_OUT_TILE_BYTES = 8 * 1024 * 1024


def _div_lanes(n, cap):
    best = None
    for d in range(LANES, min(n, cap) + 1, LANES):
        if n % d == 0:
            best = d
    return best if best is not None else n


def _matmul_tiles(mode, M, N, K):
    if mode == "nn":
        tm = _pick(M, (1024, 512, 256, 128))
        return tm, _div_lanes(N, MATMUL_OUT_TILE_BYTES // (4 * tm)), K
    if mode == "nt":
        return _pick(M, (512, 256, 128)), N, K
    tm = _div_lanes(M, 1408)
    return tm, _div_lanes(N, MATMUL_OUT_TILE_BYTES // (4 * tm)), _pick(K, (1024, 512, 256))


def _matmul(name, a, b, mode, out_dtype=F32):
    if mode == "nn":
        (M, K), (_, N) = a.shape, b.shape
    elif mode == "nt":
        (M, K), (N, _) = a.shape, b.shape
    else:
        (K, M), (_, N) = a.shape, b.shape
    tm, tn, tk = _matmul_tiles(mode, M, N, K)
    nk = K // tk
    if mode == "nn":
        a_spec = pl.BlockSpec((tm, tk), lambda i, j, k: (i, k))
        b_spec = pl.BlockSpec((tk, tn), lambda i, j, k: (k, j))
        dims = ((1,), (0,))
    elif mode == "nt":
        a_spec = pl.BlockSpec((tm, tk), lambda i, j, k: (i, k))
        b_spec = pl.BlockSpec((tn, tk), lambda i, j, k: (j, k))
        dims = ((1,), (1,))
    else:
        a_spec = pl.BlockSpec((tk, tm), lambda i, j, k: (k, i))
        b_spec = pl.BlockSpec((tk, tn), lambda i, j, k: (k, j))
        dims = ((0,), (0,))

    def body_whole(a_ref, b_ref, o_ref):
        o_ref[...] = _dot(a_ref[...].astype(BF16), b_ref[...].astype(BF16), dims).astype(o_ref.dtype)

    def body(a_ref, b_ref, o_ref, acc_ref):
        k = pl.program_id(2)
        part = _dot(a_ref[...].astype(BF16), b_ref[...].astype(BF16), dims)

        @pl.when(k == 0)
        def _():
            acc_ref[...] = part

        @pl.when(k > 0)
        def _():
            acc_ref[...] += part

        @pl.when(k == nk - 1)
        def _():
            o_ref[...] = acc_ref[...].astype(o_ref.dtype)

    return pl.pallas_call(
        body_whole if nk == 1 else body, name=name, grid=(M // tm, N // tn, nk),
        in_specs=[a_spec, b_spec],
        out_specs=pl.BlockSpec((tm, tn), lambda i, j, k: (i, j)),
        out_shape=jax.ShapeDtypeStruct((M, N), out_dtype),
        scratch_shapes=[] if nk == 1 else [pltpu.VMEM((tm, tn), F32)],
        compiler_params=_cparams(("parallel", "parallel", "arbitrary")),
    )(a, b)


def _matmul_nt_pieces(name, a_pieces, b_pieces):
    M, N = a_pieces[0].shape[0], b_pieces[0].shape[0]
    tm = _pick(M, (512, 256, 128))
    n = len(a_pieces)

    def body(*refs):
        acc = _dot(refs[0][...].astype(BF16), refs[n][...].astype(BF16), ((1,), (1,)))
        for p in range(1, n):
            acc = acc + _dot(refs[p][...].astype(BF16), refs[n + p][...].astype(BF16), ((1,), (1,)))
        refs[2 * n][...] = acc

    in_specs = ([pl.BlockSpec((tm, a.shape[1]), lambda i: (i, 0)) for a in a_pieces]
                + [pl.BlockSpec(b.shape, lambda i: (0, 0)) for b in b_pieces])
    return pl.pallas_call(
        body, name=name, grid=(M // tm,), in_specs=in_specs,
        out_specs=pl.BlockSpec((tm, N), lambda i: (i, 0)), out_shape=jax.ShapeDtypeStruct((M, N), F32),
        compiler_params=_cparams(("parallel",)),
    )(*a_pieces, *b_pieces)


def _rowwise(name, fn, S, ts, tiles, params=(), outs=(), accs=(), halo=0, ncb=1):
    in_specs, args, scratch = [], [], []
    for arr, w, jm, with_halo in tiles:
        if isinstance(with_halo, int) and with_halo > 1:
            d = with_halo
            in_specs.append(pl.BlockSpec((ts // d, d * w), lambda j, i: (i, 0)))
            args.append(arr)
            scratch.append(pltpu.VMEM((w // LANES, ts, LANES), F32))
        elif with_halo:
            hb, nhb = ts // halo, S // halo
            in_specs += [
                pl.BlockSpec((halo, w), lambda j, i, jm=jm, hb=hb: (jnp.maximum(i * hb - 1, 0), jm(j))),
                pl.BlockSpec((ts, w), lambda j, i, jm=jm: (i, jm(j))),
                pl.BlockSpec((halo, w), lambda j, i, jm=jm, hb=hb, nhb=nhb: (jnp.minimum((i + 1) * hb, nhb - 1), jm(j))),
            ]
            args += [arr, arr, arr]
        else:
            in_specs.append(pl.BlockSpec((ts, w), lambda j, i, jm=jm: (i, jm(j))))
            args.append(arr)
    for arr, r, w, jm in params:
        in_specs.append(pl.BlockSpec((r, w), lambda j, i, jm=jm: (0, jm(j))))
        args.append(arr)
    out_specs, out_shape = [], []
    for w, dt, jm, tw, *dil in outs:
        if dil:
            out_specs.append(pl.BlockSpec((ts // dil[0], dil[0] * w), lambda j, i: (i, 0)))
            out_shape.append(jax.ShapeDtypeStruct((S // dil[0], dil[0] * w), dt))
            scratch += [pltpu.VMEM((ts, w), F32), pltpu.VMEM((w // LANES, ts, LANES), F32)]
        else:
            out_specs.append(pl.BlockSpec((ts, w), lambda j, i, jm=jm: (i, jm(j))))
            out_shape.append(jax.ShapeDtypeStruct((S, tw), dt))
    for r, w, jm, tw in accs:
        out_specs.append(pl.BlockSpec((r, w), lambda j, i, jm=jm: (0, jm(j))))
        out_shape.append(jax.ShapeDtypeStruct((r, tw), F32))
    n_tiles, n_params, n_outs, n_accs = len(tiles), len(params), len(outs), len(accs)

    def residue_rows(r, d):
        return pl.ds(r, ts // d, stride=d)

    def body(*refs):
        i = pl.program_id(1)
        n_io = len(in_specs) + n_outs + n_accs
        scr = list(refs[n_io:])
        refs = refs[:n_io]
        pos, vals = 0, []
        for _, w, _, with_halo in tiles:
            if isinstance(with_halo, int) and with_halo > 1:
                d, buf = with_halo, scr.pop(0)
                for r in range(d):
                    for c in range(w // LANES):
                        buf[c, residue_rows(r, d), :] = refs[pos][:, r * w + c * LANES:r * w + (c + 1) * LANES].astype(F32)
                vals.append(jnp.concatenate([buf[c] for c in range(w // LANES)], axis=1))
                pos += 1
            elif with_halo:
                before, after = refs[pos][...], refs[pos + 2][...]
                before = jnp.where(i > 0, before, jnp.zeros_like(before))
                after = jnp.where(i < S // ts - 1, after, jnp.zeros_like(after))
                vals.append(jnp.concatenate([before, refs[pos + 1][...], after], axis=0))
                pos += 3
            else:
                vals.append(refs[pos][...])
                pos += 1
        prefs = refs[pos:pos + n_params]
        orefs = list(refs[pos + n_params:pos + n_params + n_outs])
        arefs = refs[pos + n_params + n_outs:]
        staged = []
        for k, (w, _, _, _, *dil) in enumerate(outs):
            if dil:
                staged.append((orefs[k], scr.pop(0), scr.pop(0), w, dil[0]))
                orefs[k] = staged[-1][1]

        @pl.when(i == 0)
        def _():
            for r in arefs:
                r[...] = jnp.zeros_like(r)

        fn(i, vals, prefs, orefs, arefs)
        for out_ref, flat, buf, w, d in staged:
            for c in range(w // LANES):
                buf[c] = flat[:, c * LANES:(c + 1) * LANES]
                for r in range(d):
                    out_ref[:, r * w + c * LANES:r * w + (c + 1) * LANES] = buf[c, residue_rows(r, d), :].astype(out_ref.dtype)

    res = pl.pallas_call(
        body, name=name, grid=(ncb, S // ts),
        in_specs=in_specs, out_specs=out_specs, out_shape=out_shape, scratch_shapes=scratch,
        compiler_params=_cparams(("arbitrary", "arbitrary")),
    )(*args)
    return res


def _vmem_call(name, fn, ins, out_shapes):
    n_in = len(ins)

    def body(*refs):
        vals = fn(*[r[...] for r in refs[:n_in]])
        for r, v in zip(refs[n_in:], vals):
            r[...] = v.astype(r.dtype)

    return pl.pallas_call(
        body, name=name,
        out_shape=[jax.ShapeDtypeStruct(s, dt) for s, dt in out_shapes],
        compiler_params=_cparams(),
    )(*ins)


def _rms(x, w):
    return x * lax.rsqrt(jnp.mean(x * x, axis=-1, keepdims=True) + EPS) * w


def _normmod_f(x, nw, sc, sh):
    return _rms(x, nw) * (1.0 + sc) + sh


def _row_params(*vecs):
    return [(v, 1, v.shape[1], _c0) for v in vecs]


def _normmod_fwd(x, nw, sc, sh):
    S = x.shape[0]

    def fn(i, vals, p, o, a):
        o[0][...] = _normmod_f(vals[0], p[0][...], p[1][...], p[2][...]).astype(BF16)

    return _rowwise("normmod_fwd", fn, S, 512, [(x, D, _c0, False)], _row_params(nw, sc, sh), [(D, BF16, _c0, D)])[0]


def _normmod_bwd(x, gh, gres, nw, sc, sh):
    S = x.shape[0]

    def fn(i, vals, p, o, a):
        _, vjp = jax.vjp(_normmod_f, vals[0], p[0][...], p[1][...], p[2][...])
        gx, gnw, gsc, gsh = vjp(vals[1])
        o[0][...] = gx + vals[2]
        a[0][...] += gnw
        a[1][...] += gsc
        a[2][...] += gsh

    return _rowwise("normmod_bwd", fn, S, 512, [(x, D, _c0, False), (gh, D, _c0, False), (gres, D, _c0, False)],
                    _row_params(nw, sc, sh), [(D, F32, _c0, D)], [(1, D, _c0, D)] * 3)


def _gate_add(x, y, g):
    S = x.shape[0]

    def fn(i, vals, p, o, a):
        o[0][...] = vals[0] + p[0][...] * vals[1]

    return _rowwise("gate_add", fn, S, 512, [(x, D, _c0, False), (y, D, _c0, False)], _row_params(g), [(D, F32, _c0, D)])[0]


def _gate_bwd(gx, y, g):
    S = gx.shape[0]

    def fn(i, vals, p, o, a):
        o[0][...] = (vals[0] * p[0][...]).astype(BF16)
        a[0][...] += jnp.sum(vals[0] * vals[1], axis=0, keepdims=True)

    return _rowwise("gate_bwd", fn, S, 512, [(gx, D, _c0, False), (y, D, _c0, False)], _row_params(g),
                    [(D, BF16, _c0, D)], [(1, D, _c0, D)])


def _gate_add_normmod(x, y, g, nw, sc, sh):
    S = x.shape[0]

    def fn(i, vals, p, o, a):
        x2 = vals[0] + p[0][...] * vals[1]
        o[0][...] = x2
        o[1][...] = _normmod_f(x2, p[1][...], p[2][...], p[3][...]).astype(BF16)

    return _rowwise("gate_add_normmod", fn, S, 512, [(x, D, _c0, False), (y, D, _c0, False)], _row_params(g, nw, sc, sh),
                    [(D, F32, _c0, D), (D, BF16, _c0, D)])


def _normmod_gate_bwd(x, gh, gres, nw, sc, sh, y, g):
    S = x.shape[0]

    def fn(i, vals, p, o, a):
        _, vjp = jax.vjp(_normmod_f, vals[0], p[0][...], p[1][...], p[2][...])
        gx, gnw, gsc, gsh = vjp(vals[1])
        gx = gx + vals[2]
        o[0][...] = gx
        o[1][...] = (gx * p[3][...]).astype(BF16)
        a[0][...] += gnw
        a[1][...] += gsc
        a[2][...] += gsh
        a[3][...] += jnp.sum(gx * vals[3], axis=0, keepdims=True)

    tiles = [(t, D, _c0, False) for t in (x, gh, gres, y)]
    return _rowwise("normmod_gate_bwd", fn, S, 512, tiles, _row_params(nw, sc, sh, g),
                    [(D, F32, _c0, D), (D, BF16, _c0, D)], [(1, D, _c0, D)] * 4)


def _loss_head(x, tgt, fw):
    S = x.shape[0]

    def fn(i, vals, p, o, a):
        y, vjp = jax.vjp(_rms, vals[0], p[0][...])
        err = y - vals[1]
        gx, gfw = vjp(err * (1.0 / D))
        o[0][...] = gx
        a[0][...] += gfw
        part = 0.5 * jnp.sum(jnp.mean(err * err, axis=-1, keepdims=True), axis=0, keepdims=True)
        a[1][...] += jnp.broadcast_to(part, (1, LANES))

    return _rowwise("loss_head", fn, S, 256, [(x, D, _c0, False), (tgt, D, _c0, False)], _row_params(fw),
                    [(D, F32, _c0, D)], [(1, D, _c0, D), (1, LANES, _c0, LANES)])


CONV_HALO = 16
CONV_TS = 512


def _shifted(ext, shift, ts, halo):
    n = ext.shape[0]
    s = shift % n
    r = ext if s == 0 else pltpu.roll(ext, s, 0)
    return r[halo:halo + ts]


def _ln_silu(a, w, b):
    mu = jnp.mean(a, axis=-1, keepdims=True)
    var = jnp.mean(jnp.square(a - mu), axis=-1, keepdims=True)
    y = (a - mu) * lax.rsqrt(var + EPS) * w + b
    return y * jax.nn.sigmoid(y)


def _conv_a_fwd(proj, w_pad, b, lnw, lnb):
    S = proj.shape[0]
    ts, H = min(CONV_TS, S), CONV_HALO

    def fn(i, vals, p, o, a):
        a0 = vals[0] * jax.nn.sigmoid(vals[1])
        acc = jnp.zeros((ts, CONV_CH), F32) + p[1][...]
        for k in range(CONV_W):
            acc = acc + _shifted(a0, CONV_W // 2 - k, ts, H) * p[0][pl.ds(k, 1), :]
        o[0][...] = acc
        o[1][...] = _ln_silu(acc, p[2][...], p[3][...]).astype(BF16)

    tiles = [(proj, CONV_CH, lambda j: AV_BLK, True), (proj, CONV_CH, lambda j: AG_BLK, True)]
    params = [(w_pad, CONV_W_PAD, CONV_CH, _c0)] + _row_params(b, lnw, lnb)
    return _rowwise("conv_a_fwd", fn, S, ts, tiles, params, [(CONV_CH, F32, _c0, CONV_CH), (CONV_CH, BF16, _c0, CONV_CH)], halo=H)


def _conv_a_bwd(proj, a1, gmixed, w_pad, lnw, lnb):
    S = proj.shape[0]
    ts, H = min(CONV_TS, S), CONV_HALO

    def fn(i, vals, p, o, a):
        av, ag, a1e, ge = vals
        lw, lb = p[1][...], p[2][...]
        _, vjp_e = jax.vjp(lambda t: _ln_silu(t, lw, lb), a1e)
        (ga1e,) = vjp_e(ge)
        c = slice(H, H + ts)
        _, vjp_c = jax.vjp(_ln_silu, a1e[c], lw, lb)
        ga1, glw, glb = vjp_c(ge[c])
        a[1][...] += jnp.sum(ga1, axis=0, keepdims=True)
        a[2][...] += glw
        a[3][...] += glb
        sg = jax.nn.sigmoid(ag)
        a0 = av * sg
        ga0 = jnp.zeros((ts, CONV_CH), F32)
        for k in range(CONV_W):
            a[0][pl.ds(k, 1), :] += jnp.sum(ga1 * _shifted(a0, CONV_W // 2 - k, ts, H), axis=0, keepdims=True)
            ga0 = ga0 + _shifted(ga1e, k - CONV_W // 2, ts, H) * p[0][pl.ds(k, 1), :]
        sgc, avc = sg[c], av[c]
        o[0][...] = (ga0 * sgc).astype(BF16)
        o[1][...] = (ga0 * avc * sgc * (1.0 - sgc)).astype(BF16)

    tiles = [(proj, CONV_CH, lambda j: AV_BLK, True), (proj, CONV_CH, lambda j: AG_BLK, True),
             (a1, CONV_CH, _c0, True), (gmixed, CONV_CH, lambda j: 3, True)]
    params = [(w_pad, CONV_W_PAD, CONV_CH, _c0)] + _row_params(lnw, lnb)
    outs = [(CONV_CH, BF16, _c0, CONV_CH), (CONV_CH, BF16, _c0, CONV_CH)]
    accs = [(CONV_W_PAD, CONV_CH, _c0, CONV_CH)] + [(1, CONV_CH, _c0, CONV_CH)] * 3
    return _rowwise("conv_a_bwd", fn, S, ts, tiles, params, outs, accs, halo=H)


FFN_HALO = 8
FFN_TS = 512
FFN_TS_FWD = 1024
FFN_CB = 256
FFN_NCB = D_FF // FFN_CB


def _gelu_mul(g, v):
    return 0.5 * g * (1.0 + lax.erf(g * (2.0 ** -0.5))) * v


def _ffn_mid_fwd(u, cw):
    S = u.shape[0]
    ts, H = min(FFN_TS_FWD, S), FFN_HALO

    def conv(ext, w_ref):
        acc = jnp.zeros((ts, FFN_CB), F32)
        for k in range(FFN_CONV_W):
            acc = acc + _shifted(ext, 1 - k, ts, H) * w_ref[pl.ds(k, 1), :]
        return acc

    def fn(i, vals, p, o, a):
        o[0][...] = _gelu_mul(conv(vals[0], p[0]), conv(vals[1], p[1])).astype(BF16)

    gate, val = (lambda j: j), (lambda j: j + FFN_NCB)
    return _rowwise("ffn_mid_fwd", fn, S, ts, [(u, FFN_CB, gate, True), (u, FFN_CB, val, True)],
                    [(cw, FFN_CONV_W, FFN_CB, gate), (cw, FFN_CONV_W, FFN_CB, val)],
                    [(FFN_CB, BF16, gate, D_FF)], halo=H, ncb=FFN_NCB)[0]


def _ffn_mid_bwd(u, gact, cw):
    S = u.shape[0]
    ts, H = min(FFN_TS, S), FFN_HALO
    n = ts + 2 * H

    def fn(i, vals, p, o, a):
        ug, uv, ga = vals

        def conv_all(ue, w_ref):
            acc = jnp.zeros((n, FFN_CB), F32)
            for k in range(FFN_CONV_W):
                s = (1 - k) % n
                acc = acc + (ue if s == 0 else pltpu.roll(ue, s, 0)) * w_ref[pl.ds(k, 1), :]
            return acc

        _, vjp = jax.vjp(_gelu_mul, conv_all(ug, p[0]), conv_all(uv, p[1]))
        for half, (gc, ue) in enumerate(zip(vjp(ga), (ug, uv))):
            gu = jnp.zeros((ts, FFN_CB), F32)
            for k in range(FFN_CONV_W):
                gu = gu + _shifted(gc, k - 1, ts, H) * p[half][pl.ds(k, 1), :]
                a[half][pl.ds(k, 1), :] += jnp.sum(gc[H:H + ts] * _shifted(ue, 1 - k, ts, H), axis=0, keepdims=True)
            o[half][...] = gu.astype(BF16)

    gate, val = (lambda j: j), (lambda j: j + FFN_NCB)
    tiles = [(u, FFN_CB, gate, True), (u, FFN_CB, val, True), (gact, FFN_CB, gate, True)]
    params = [(cw, FFN_CONV_W, FFN_CB, gate), (cw, FFN_CONV_W, FFN_CB, val)]
    gu_gate, gu_val, gw_gate, gw_val = _rowwise("ffn_mid_bwd", fn, S, ts, tiles, params, [(FFN_CB, BF16, gate, D_FF)] * 2,
                                                [(FFN_CONV_W, FFN_CB, gate, D_FF)] * 2, halo=H, ncb=FFN_NCB)
    return (gu_gate, gu_val), jnp.concatenate([gw_gate, gw_val], axis=1)


LD_W = LANES
PAIR_W = 2 * HEAD
N_PAIRS = N_HEADS // 2


def _sub_view(t, d):
    S, C = t.shape
    return t.reshape(S // d, d * C)


def _sub_halo_specs(width, col, blk, hb, nhb):
    per = blk // hb
    return [
        pl.BlockSpec((hb, width), lambda r, i: (jnp.maximum(i * per - 1, 0), col(r))),
        pl.BlockSpec((blk, width), lambda r, i: (i, col(r))),
        pl.BlockSpec((hb, width), lambda r, i: (jnp.minimum((i + 1) * per, nhb - 1), col(r))),
    ]


def _pick_lane(t, lane):
    return jnp.sum(jnp.where(_iota((1, t.shape[1]), 1) == lane, t, 0.0), axis=1, keepdims=True)


def _pair_mask(h2):
    return (_iota((1, PAIR_W), 1) >> HEAD_SHIFT) == h2


def _cat_bf16(a, b, c):
    return jnp.concatenate([a[...], b[...], c[...]], axis=0).astype(BF16)


def _attn_fwd(view, dil):
    L = view.shape[0]
    blk, hb = min(ATT_BLK, L), HALF_BAND
    span = blk + 2 * hb

    def body(q_ref, kp, kc, kn, vp, vc, vn, o_ref, l_ref):
        i = pl.program_id(1)
        rel = _iota((blk, span), 1) - hb - _iota((blk, span), 0)
        kpos = i * blk - hb + _iota((blk, span), 1)
        valid = (jnp.abs(rel) <= hb) & (kpos >= 0) & (kpos < L)
        dist = jnp.abs(rel).astype(F32) * float(dil)
        q, k, v = q_ref[...].astype(BF16), _cat_bf16(kp, kc, kn), _cat_bf16(vp, vc, vn)
        lse = jnp.zeros((blk, LD_W), F32)
        for pr in range(N_PAIRS):
            sl = slice(pr * PAIR_W, (pr + 1) * PAIR_W)
            qp, kpair, vpair = q[:, sl], k[:, sl], v[:, sl]
            o = jnp.zeros((blk, PAIR_W), F32)
            for h2 in range(2):
                h, mask = 2 * pr + h2, _pair_mask(h2)
                s = _dot_nt(jnp.where(mask, qp, jnp.zeros_like(qp)), kpair) * (HEAD ** -0.5) - ALIBI_SLOPES[h] * dist
                s = jnp.where(valid, s, MASK_VALUE)
                m = jnp.max(s, axis=1, keepdims=True)
                p = jnp.exp(s - m)
                l = jnp.sum(p, axis=1, keepdims=True)
                o = jnp.where(mask, _dot_nn(p.astype(BF16), vpair) / l, o)
                lse = lse + jnp.where(_iota((1, LD_W), 1) == h, m + jnp.log(l), 0.0)
            o_ref[:, sl] = o
        l_ref[...] = lse

    nhb = L // hb
    in_specs = ([pl.BlockSpec((blk, ATT_W), lambda r, i: (i, r * QKV_BLOCKS +QA))]
                + _sub_halo_specs(ATT_W, lambda r: r * QKV_BLOCKS +KA, blk, hb, nhb)
                + _sub_halo_specs(ATT_W, lambda r: r * QKV_BLOCKS +VA, blk, hb, nhb))
    return pl.pallas_call(
        body, name=f"attn_fwd_d{dil}", grid=(dil, L // blk), in_specs=in_specs,
        out_specs=[pl.BlockSpec((blk, ATT_W), lambda r, i: (i, r)), pl.BlockSpec((blk, LD_W), lambda r, i: (i, r))],
        out_shape=[jax.ShapeDtypeStruct((L, dil * ATT_W), F32), jax.ShapeDtypeStruct((L, dil * LD_W), F32)],
        compiler_params=_cparams(("parallel", "parallel")),
    )(*([view] * 7))


def _attn_bwd(pview, gview, lview, dil):
    L = pview.shape[0]
    blk, hb = min(ATT_BLK, L), HALF_BAND
    span = blk + 2 * hb
    scale = HEAD ** -0.5

    def body(qp, qc, qn, kp, kc, kn, vp, vc, vn, gp, gc, gn, lp, lc, ln, dq_ref, dk_ref, dv_ref):
        i = pl.program_id(1)
        l = lc[...]
        le = jnp.concatenate([lp[...], l, ln[...]], axis=0)
        rel_q = _iota((blk, span), 1) - hb - _iota((blk, span), 0)
        kpos = i * blk - hb + _iota((blk, span), 1)
        valid_q = (jnp.abs(rel_q) <= hb) & (kpos >= 0) & (kpos < L)
        dist_q = jnp.abs(rel_q).astype(F32) * float(dil)
        rel_k = _iota((span, blk), 1) + hb - _iota((span, blk), 0)
        qpos = i * blk - hb + _iota((span, blk), 0)
        valid_k = (jnp.abs(rel_k) <= hb) & (qpos >= 0) & (qpos < L)
        dist_k = jnp.abs(rel_k).astype(F32) * float(dil)
        q_all, k_all, v_all, g_all = qc[...].astype(BF16), kc[...].astype(BF16), vc[...].astype(BF16), gc[...]
        qe_all, ke_all, ve_all = _cat_bf16(qp, qc, qn), _cat_bf16(kp, kc, kn), _cat_bf16(vp, vc, vn)
        ge_all = _cat_bf16(gp, gc, gn)
        for pr in range(N_PAIRS):
            sl = slice(pr * PAIR_W, (pr + 1) * PAIR_W)
            q, k, v, g = q_all[:, sl], k_all[:, sl], v_all[:, sl], g_all[:, sl]
            qe, ke, ve, ge = qe_all[:, sl], ke_all[:, sl], ve_all[:, sl], ge_all[:, sl]
            dq = jnp.zeros((blk, PAIR_W), F32)
            dk = jnp.zeros((blk, PAIR_W), F32)
            dv = jnp.zeros((blk, PAIR_W), F32)
            for h2 in range(2):
                h, mask = 2 * pr + h2, _pair_mask(h2)
                only = lambda t: jnp.where(mask, t, jnp.zeros_like(t))
                s = _dot_nt(only(q), ke) * scale - ALIBI_SLOPES[h] * dist_q
                p = jnp.where(valid_q, jnp.exp(s - _pick_lane(l, h)), 0.0)
                ds = p * (_dot_nt(only(g), ve) - _pick_lane(l, 8 + h))
                dq = jnp.where(mask, _dot_nn(ds.astype(BF16), ke), dq)
                s = _dot_nt(only(qe), k) * scale - ALIBI_SLOPES[h] * dist_k
                p = jnp.where(valid_k, jnp.exp(s - _pick_lane(le, h)), 0.0)
                dv = jnp.where(mask, _dot_tn(p.astype(BF16), ge), dv)
                ds = p * (_dot_nt(only(ge), v) - _pick_lane(le, 8 + h))
                dk = jnp.where(mask, _dot_tn(ds.astype(BF16), qe), dk)
            dq_ref[:, sl] = (dq * scale).astype(BF16)
            dk_ref[:, sl] = (dk * scale).astype(BF16)
            dv_ref[:, sl] = dv.astype(BF16)

    nhb = L // hb
    in_specs = (_sub_halo_specs(ATT_W, lambda r: r * QKV_BLOCKS +QA, blk, hb, nhb)
                + _sub_halo_specs(ATT_W, lambda r: r * QKV_BLOCKS +KA, blk, hb, nhb)
                + _sub_halo_specs(ATT_W, lambda r: r * QKV_BLOCKS +VA, blk, hb, nhb)
                + _sub_halo_specs(ATT_W, lambda r: r, blk, hb, nhb) + _sub_halo_specs(LD_W, lambda r: r, blk, hb, nhb))
    o_spec = pl.BlockSpec((blk, ATT_W), lambda r, i: (i, r))
    return pl.pallas_call(
        body, name=f"attn_bwd_d{dil}", grid=(dil, L // blk), in_specs=in_specs,
        out_specs=[o_spec] * 3, out_shape=[jax.ShapeDtypeStruct((L, dil * ATT_W), BF16)] * 3,
        compiler_params=_cparams(("parallel", "parallel")),
    )(*([pview] * 9 + [gview] * 3 + [lview] * 3))


def _head_expand(t):
    e = ((_iota((LD_W, ATT_W), 1) >> HEAD_SHIFT) == _iota((LD_W, ATT_W), 0)).astype(F32)
    return _dot_nn(t, e, HP)


def _dil(d):
    return d if d > 1 else False


def _qkv_views(proj):
    S, w = proj.shape[0], QKV_BLOCKS * ATT_W

    def fn(i, vals, p, o, a):
        for k, d in enumerate(DILATIONS):
            o[k][...] = vals[0].astype(o[k].dtype)

    outs = [(w, BF16, _c0, w) + ((d,) if d > 1 else ()) for d in DILATIONS]
    return _rowwise("qkv_views", fn, S, 512, [(proj, w, _c0, False)], (), outs)


def _attn_merge(os, ls):
    S = os[0].shape[0]

    def fn(i, vals, p, o, a):
        o3, l3 = vals[:3], vals[3:]
        m = jnp.maximum(jnp.maximum(l3[0], l3[1]), l3[2])
        e3 = [jnp.exp(l - m) for l in l3]
        den = e3[0] + e3[1] + e3[2]
        out = jnp.zeros((o3[0].shape[0], ATT_W), F32)
        for ob, e in zip(o3, e3):
            out = out + _head_expand(e / den) * ob
        o[0][...] = out
        o[1][...] = m + jnp.log(den)
        o[2][...] = out.astype(BF16)

    tiles = ([(t, ATT_W, _c0, _dil(d)) for t, d in zip(os, DILATIONS)]
             + [(t, LD_W, _c0, _dil(d)) for t, d in zip(ls, DILATIONS)])
    return _rowwise("attn_merge", fn, S, 512, tiles, (),
                    [(ATT_W, F32, _c0, ATT_W), (LD_W, F32, _c0, LD_W), (ATT_W, BF16, _c0, ATT_W)])


def _attn_bwd_prep(gmixed, att, lse):
    S = att.shape[0]
    n = len(DILATIONS)

    def fn(i, vals, p, o, a):
        g, out, lse_row = vals
        place_d = ((_iota((ATT_W, LD_W), 0) >> HEAD_SHIFT) + 8 == _iota((ATT_W, LD_W), 1)).astype(F32)
        ld = jnp.where(_iota((1, LD_W), 1) < 8, lse_row, 0.0) + _dot_nn(g * out, place_d, HP)
        for k in range(n):
            o[k][...] = g.astype(o[k].dtype)
            o[n + k][...] = ld

    tiles = [(gmixed, ATT_W, _c0, False), (att, ATT_W, _c0, False), (lse, LD_W, _c0, False)]
    outs = ([(ATT_W, BF16, _c0, ATT_W) + ((d,) if d > 1 else ()) for d in DILATIONS]
            + [(LD_W, F32, _c0, LD_W) + ((d,) if d > 1 else ()) for d in DILATIONS])
    res = _rowwise("attn_bwd_prep", fn, S, 512, tiles, (), outs)
    return res[:n], res[n:]


def _sum3_bf16(views, S, width):
    def fn(i, vals, p, o, a):
        o[0][...] = (vals[0].astype(F32) + vals[1].astype(F32) + vals[2].astype(F32)).astype(BF16)

    tiles = [(t, width, _c0, _dil(d)) for t, d in zip(views, DILATIONS)]
    return _rowwise("sum3", fn, S, 512, tiles, (), [(width, BF16, _c0, width)])[0]


def _block_diag_mask():
    return ((_iota((REC_W, REC_W), 0) >> HEAD_SHIFT) == (_iota((REC_W, REC_W), 1) >> HEAD_SHIFT)).astype(F32)


def _hgrn_chunk(qr, z, iv, lb, st, reverse, precise):
    C = REC_CHUNK
    r, c = _iota((C, C), 0), _iota((C, C), 1)
    t_cum = (c >= r) if reverse else (c <= r)
    mid_row, last_row = (C // 2, 0) if reverse else (C // 2 - 1, C - 1)
    f = lb + (1.0 - lb) * jax.nn.sigmoid(z)
    logf = jnp.log(jnp.maximum(f, F_TINY))
    k = (1.0 - lb) * jax.nn.sigmoid(-z)
    q = qr * jax.nn.sigmoid(qr)
    b = _dot_nn(t_cum.astype(F32), logf, HP)
    row = _iota((C, 1), 0)
    bm = jnp.sum(jnp.where(row == mid_row, b, 0.0), axis=0, keepdims=True)
    bl = jnp.sum(jnp.where(row == last_row, b, 0.0), axis=0, keepdims=True)
    qt = q * jnp.exp(jnp.minimum(b - bm, EXP_CLAMP))
    kt = k * jnp.exp(jnp.minimum(bm - b, EXP_CLAMP))
    qh = q * jnp.exp(b)
    kh = k * jnp.exp(bl - b)
    lam = jnp.exp(bl)
    bd = ((_iota((PAIR_W, PAIR_W), 0) >> HEAD_SHIFT) == (_iota((PAIR_W, PAIR_W), 1) >> HEAD_SHIFT)).astype(F32)
    s_in = _iota((C, PAIR_W), 1) & (HEAD - 1)
    t_in = _iota((C, PAIR_W), 0)
    tri = (s_in >= t_in) if reverse else (s_in <= t_in)
    twice = lambda t: jnp.concatenate([t, t], axis=0)
    outs, states = [], []
    for pr in range(N_PAIRS):
        sl = slice(pr * PAIR_W, (pr + 1) * PAIR_W)
        k_bd = twice(kt[:, sl]) * bd
        v_bd = (twice(iv[:, sl]) * bd).astype(BF16)
        st_bd = twice(st[:, sl]) * bd
        if precise:
            scores = _dot_nt(qt[:, sl], k_bd, lax.Precision.HIGH)
        else:
            scores = _dot_nt(qt[:, sl].astype(BF16), k_bd.astype(BF16))
        a = jnp.where(tri, scores, 0.0)
        outs.append(_dot_nn(a.astype(BF16), v_bd) + _dot_nt(qh[:, sl].astype(BF16), st_bd.astype(BF16)))
        kv = _dot_tn(iv[:, sl].astype(BF16), kh[:, sl].astype(BF16))
        st_bd = st_bd * lam[:, sl] + kv * bd
        states.append(st_bd[0:HEAD] + st_bd[HEAD:PAIR_W])
    return jnp.concatenate(outs, axis=1), jnp.concatenate(states, axis=1)


REC_CHUNKS_PER_STEP = 8
REC_ROWS = REC_CHUNKS_PER_STEP * REC_CHUNK


def _hgrn_specs(order, blocks):
    return [pl.BlockSpec((REC_ROWS, REC_W), lambda i, b=b: (order(i), b)) for b in blocks]


def _chunk_rows(j):
    return pl.ds(pl.multiple_of(j * REC_CHUNK, REC_CHUNK), REC_CHUNK)


def _hgrn_fwd(proj, lb, z_blk, reverse):
    S = proj.shape[0]
    nb = S // REC_ROWS
    order = (lambda i: nb - 1 - i) if reverse else (lambda i: i)

    def body(q_ref, z_ref, v_ref, lb_ref, o_ref, st_ref, st_scr):
        @pl.when(pl.program_id(0) == 0)
        def _():
            st_scr[...] = jnp.zeros_like(st_scr)

        def step(t, carry):
            j = REC_CHUNKS_PER_STEP - 1 - t if reverse else t
            rows = _chunk_rows(j)
            st = st_scr[...]
            st_ref[j] = st
            o, st_new = _hgrn_chunk(q_ref[rows, :], z_ref[rows, :], v_ref[rows, :], lb_ref[...], st, reverse, False)
            o_ref[rows, :] = o
            st_scr[...] = st_new
            return carry

        lax.fori_loop(0, REC_CHUNKS_PER_STEP, step, 0, unroll=2)

    return pl.pallas_call(
        body, name="hgrn_rev_fwd" if reverse else "hgrn_fwd_fwd", grid=(nb,),
        in_specs=_hgrn_specs(order, (QR, z_blk, IR)) + [pl.BlockSpec((1, REC_W), lambda i: (0, 0))],
        out_specs=[pl.BlockSpec((REC_ROWS, REC_W), lambda i: (order(i), 0)),
                   pl.BlockSpec((REC_CHUNKS_PER_STEP, HEAD, REC_W), lambda i: (order(i), 0, 0))],
        out_shape=[jax.ShapeDtypeStruct((S, REC_W), F32), jax.ShapeDtypeStruct((S // REC_CHUNK, HEAD, REC_W), F32)],
        scratch_shapes=[pltpu.VMEM((HEAD, REC_W), F32)],
        compiler_params=_cparams(("arbitrary",)),
    )(proj, proj, proj, lb)


def _hgrn_bwd(proj, lb, states, go, z_blk, reverse, other=None):
    S = proj.shape[0]
    nb = S // REC_ROWS
    order = (lambda i: i) if reverse else (lambda i: nb - 1 - i)
    n_other = 0 if other is None else 2

    def body(*refs):
        q_ref, z_ref, v_ref, lb_ref, st_ref, go_ref = refs[:6]
        other_refs = refs[6:6 + n_other]
        gq_ref, gz_ref, gv_ref, glb_ref, gst_scr = refs[6 + n_other:]

        @pl.when(pl.program_id(0) == 0)
        def _():
            gst_scr[...] = jnp.zeros_like(gst_scr)
            glb_ref[...] = jnp.zeros_like(glb_ref)

        chunk = functools.partial(_hgrn_chunk, reverse=reverse, precise=True)

        def step(t, carry):
            j = t if reverse else REC_CHUNKS_PER_STEP - 1 - t
            rows = _chunk_rows(j)
            _, vjp = jax.vjp(chunk, q_ref[rows, :], z_ref[rows, :], v_ref[rows, :], lb_ref[...], st_ref[j])
            gq, gz, gv, glb, gst = vjp((go_ref[rows, :], gst_scr[...]))
            if other_refs:
                gq = gq + other_refs[0][rows, :]
                gv = gv + other_refs[1][rows, :]
            gq_ref[rows, :] = gq.astype(gq_ref.dtype)
            gz_ref[rows, :] = gz.astype(gz_ref.dtype)
            gv_ref[rows, :] = gv.astype(gv_ref.dtype)
            glb_ref[...] += glb
            gst_scr[...] = gst
            return carry

        lax.fori_loop(0, REC_CHUNKS_PER_STEP, step, 0, unroll=2)

    row_spec = pl.BlockSpec((REC_ROWS, REC_W), lambda i: (order(i), 0))
    return pl.pallas_call(
        body, name="hgrn_rev_bwd" if reverse else "hgrn_fwd_bwd", grid=(nb,),
        in_specs=(_hgrn_specs(order, (QR, z_blk, IR)) + [pl.BlockSpec((1, REC_W), lambda i: (0, 0))]
                  + [pl.BlockSpec((REC_CHUNKS_PER_STEP, HEAD, REC_W), lambda i: (order(i), 0, 0)), row_spec]
                  + [row_spec] * n_other),
        out_specs=[row_spec] * 3 + [pl.BlockSpec((1, REC_W), lambda i: (0, 0))],
        out_shape=([jax.ShapeDtypeStruct((S, REC_W), dt) for dt in (BF16 if other else F32, BF16, BF16 if other else F32)]
                   + [jax.ShapeDtypeStruct((1, REC_W), F32)]),
        scratch_shapes=[pltpu.VMEM((HEAD, REC_W), F32)],
        compiler_params=_cparams(("arbitrary",)),
    )(proj, proj, proj, lb, states, go, *(other or ()))


def _hgrn_post_f(of, ob, gr, rnw):
    o = of + ob
    ms = _dot_nn(o * o, _block_diag_mask() * (1.0 / HEAD), HP)
    return o * lax.rsqrt(ms + EPS) * rnw * (gr * jax.nn.sigmoid(gr))


def _hgrn_post_fwd(of, ob, proj, rnw):
    S = of.shape[0]

    def fn(i, vals, p, o, a):
        o[0][...] = _hgrn_post_f(vals[0], vals[1], vals[2], p[0][...]).astype(BF16)

    tiles = [(of, REC_W, _c0, False), (ob, REC_W, _c0, False), (proj, REC_W, lambda j: GR, False)]
    return _rowwise("hgrn_post_fwd", fn, S, 512, tiles, _row_params(rnw), [(REC_W, BF16, _c0, REC_W)])[0]


def _hgrn_post_bwd(of, ob, proj, gmixed, rnw):
    S = of.shape[0]

    def fn(i, vals, p, o, a):
        _, vjp = jax.vjp(_hgrn_post_f, vals[0], vals[1], vals[2], p[0][...])
        go, _, ggr, grnw = vjp(vals[3])
        o[0][...] = go
        o[1][...] = ggr.astype(BF16)
        a[0][...] += grnw

    tiles = [(of, REC_W, _c0, False), (ob, REC_W, _c0, False), (proj, REC_W, lambda j: GR, False),
             (gmixed, REC_W, lambda j: 1, False)]
    return _rowwise("hgrn_post_bwd", fn, S, 512, tiles, _row_params(rnw),
                    [(REC_W, F32, _c0, REC_W), (REC_W, BF16, _c0, REC_W)], [(1, REC_W, _c0, REC_W)])


def _lower_bounds_f(g0, g1):
    m = jnp.maximum(g0, g1)
    e0, e1 = jnp.exp(g0 - m), jnp.exp(g1 - m)
    return e1 / (e0 + e1)


def _adamw(name, w, m, v, gparts):
    R, C = w.shape
    P = gparts.shape[0]
    tr = R if R * C * 4 * (P + 7) * 2 <= VMEM_LIMIT_BYTES // 2 else _pick(R, (256, 128, 64, 32, 16, 8))

    def body(w_ref, m_ref, v_ref, gp_ref, g_ref, d_ref, nm_ref, nv_ref):
        g = gp_ref[0].astype(F32)
        for p in range(1, P):
            g = g + gp_ref[p].astype(F32)
        w_ = w_ref[...]
        nm = ADAM_B1 * m_ref[...] + (1.0 - ADAM_B1) * g
        nv = ADAM_B2 * v_ref[...] + (1.0 - ADAM_B2) * jnp.square(g)
        m_hat = nm / (1.0 - ADAM_B1 ** ADAM_STEP)
        v_hat = nv / (1.0 - ADAM_B2 ** ADAM_STEP)
        g_ref[...] = g
        d_ref[...] = -ADAM_LR * (m_hat / (jnp.sqrt(v_hat) + ADAM_EPS) + ADAM_WD * w_)
        nm_ref[...] = nm
        nv_ref[...] = nv

    spec = pl.BlockSpec((tr, C), lambda i: (i, 0))
    return pl.pallas_call(
        body, name=name, grid=(R // tr,),
        in_specs=[spec, spec, spec, pl.BlockSpec((P, tr, C), lambda i: (0, i, 0))],
        out_specs=[spec] * 4, out_shape=[jax.ShapeDtypeStruct((R, C), F32)] * 4,
        compiler_params=_cparams(("parallel",)),
    )(w, m, v, gparts)


def _place():
    return lax.axis_index("x"), lax.axis_index("y"), lax.axis_index("c")


def _index_of(p):
    return 4 * p[0] + 2 * p[1] + p[2]


def _allgather_small(name, rows):
    m_per, n = rows.shape

    def body(x_ref, out_ref, send_sems, recv_sems, local_sem):
        x, y, c = _place()
        me, sibling = (x, y, c), (x, y, 1 - c)
        chips = [(1 - x, y), (x, 1 - y), (1 - x, 1 - y)]

        def blk(p):
            return out_ref.at[pl.ds(_index_of(p) * m_per, m_per), :]

        def copy(k, block, to, src=None):
            return pltpu.make_async_remote_copy(
                src_ref=blk(block) if src is None else src, dst_ref=blk(block),
                send_sem=send_sems.at[k], recv_sem=recv_sems.at[k], device_id=to, device_id_type=MESH)

        mine = pltpu.make_async_copy(x_ref, blk(me), local_sem)
        mine.start()
        first = [copy(0, me, sibling, src=x_ref)]
        first += [copy(1 + j, me, (*chip, c), src=x_ref) for j, chip in enumerate(chips)]
        for cp in first:
            cp.start()
        passed = [copy(4 + j, (*chip, c), sibling) for j, chip in enumerate(chips)]
        for j, chip in enumerate(chips):
            copy(1 + j, (*chip, c), me).wait_recv()
            passed[j].start()
        copy(0, sibling, me).wait_recv()
        for j, chip in enumerate(chips):
            copy(4 + j, (*chip, 1 - c), me).wait_recv()
        for cp in first + passed:
            cp.wait_send()
        mine.wait()

    return pl.pallas_call(
        body, name=name,
        out_shape=jax.ShapeDtypeStruct((N_DEV * m_per, n), rows.dtype),
        in_specs=[pl.BlockSpec(memory_space=pltpu.VMEM)],
        out_specs=pl.BlockSpec(memory_space=pltpu.VMEM),
        scratch_shapes=[pltpu.SemaphoreType.DMA((7,)), pltpu.SemaphoreType.DMA((7,)), pltpu.SemaphoreType.DMA],
        compiler_params=_cparams(),
    )(rows)


def _allgather_big(name, arrs):
    na = len(arrs)

    def body(*refs):
        ins, outs = refs[:na], refs[na:2 * na]
        send_sems, recv_sems, local_sems = refs[2 * na:]
        x, y, c = _place()
        me, sibling = (x, y, c), (x, y, 1 - c)
        chips = [(1 - x, y), (x, 1 - y), (1 - x, 1 - y)]

        def copy(a, k, block, to, src=None):
            dst = outs[a].at[_index_of(block)]
            return pltpu.make_async_remote_copy(
                src_ref=dst if src is None else src, dst_ref=dst,
                send_sem=send_sems.at[a, k], recv_sem=recv_sems.at[a, k], device_id=to, device_id_type=MESH)

        mine = [pltpu.make_async_copy(ins[a], outs[a].at[_index_of(me)], local_sems.at[a]) for a in range(na)]
        for cp in mine:
            cp.start()
        sent = []
        for a in range(na):
            sent.append(copy(a, 0, me, sibling, src=ins[a]))
            sent += [copy(a, 1 + j, me, (*chip, c), src=ins[a]) for j, chip in enumerate(chips)]
        for cp in sent:
            cp.start()
        for j, chip in enumerate(chips):
            for a in range(na):
                copy(a, 1 + j, (*chip, c), me).wait_recv()
                fwd = copy(a, 4 + j, (*chip, c), sibling)
                fwd.start()
                sent.append(fwd)
        for a in range(na):
            copy(a, 0, sibling, me).wait_recv()
            for j, chip in enumerate(chips):
                copy(a, 4 + j, (*chip, 1 - c), me).wait_recv()
        for cp in sent:
            cp.wait_send()
        for cp in mine:
            cp.wait()

    any_spec = pl.BlockSpec(memory_space=pl.ANY)
    return pl.pallas_call(
        body, name=name,
        out_shape=[jax.ShapeDtypeStruct((N_DEV,) + a.shape, a.dtype) for a in arrs],
        in_specs=[any_spec] * na, out_specs=[any_spec] * na,
        scratch_shapes=[pltpu.SemaphoreType.DMA((na, 7)), pltpu.SemaphoreType.DMA((na, 7)), pltpu.SemaphoreType.DMA((na,))],
        compiler_params=_cparams(),
    )(*arrs)


N_CHIPS = 4


def _scatter_to_sibling(name, parts):
    na = len(parts)

    def body(*refs):
        ins, outs = refs[:na], refs[na:2 * na]
        send_sems, recv_sems = refs[2 * na:]
        x, y, c = _place()
        sibling = (x, y, 1 - c)
        sent = []
        for a in range(na):
            for q in range(N_CHIPS):
                sent.append(pltpu.make_async_remote_copy(
                    src_ref=ins[a].at[2 * q + (1 - c)], dst_ref=outs[a].at[q],
                    send_sem=send_sems.at[a, q], recv_sem=recv_sems.at[a, q], device_id=sibling, device_id_type=MESH))
        for cp in sent:
            cp.start()
        for cp in sent:
            cp.wait_recv()
        for cp in sent:
            cp.wait_send()

    any_spec = pl.BlockSpec(memory_space=pl.ANY)
    return pl.pallas_call(
        body, name=name,
        out_shape=[jax.ShapeDtypeStruct((N_CHIPS,) + p.shape[1:], p.dtype) for p in parts],
        in_specs=[any_spec] * na, out_specs=[any_spec] * na,
        scratch_shapes=[pltpu.SemaphoreType.DMA((na, N_CHIPS)), pltpu.SemaphoreType.DMA((na, N_CHIPS))],
        compiler_params=_cparams(),
    )(*parts)


def _pair_sum(name, parts, recv):
    _, R, C = parts.shape
    tr = _pick(R, (256, 128, 64, 32, 16))

    def body(p_ref, r_ref, o_ref):
        c = lax.axis_index("c")
        o_ref[...] = (p_ref[c].astype(F32) + r_ref[...].astype(F32)).astype(BF16)

    return pl.pallas_call(
        body, name=name, grid=(N_CHIPS, R // tr),
        in_specs=[pl.BlockSpec((None, 2, tr, C), lambda q, i: (q, 0, i, 0)), pl.BlockSpec((None, tr, C), lambda q, i: (q, i, 0))],
        out_specs=pl.BlockSpec((None, tr, C), lambda q, i: (q, i, 0)),
        out_shape=jax.ShapeDtypeStruct((N_CHIPS, R, C), BF16),
        compiler_params=_cparams(("parallel", "parallel")),
    )(parts.reshape(N_CHIPS, 2, R, C), recv)


def _scatter_to_chips(name, sums):
    na = len(sums)

    def body(*refs):
        ins, outs = refs[:na], refs[na:2 * na]
        send_sems, recv_sems, local_sems = refs[2 * na:]
        x, y, c = _place()
        me = 2 * x + y
        chips = [(1 - x, y), (x, 1 - y), (1 - x, 1 - y)]
        mine = [pltpu.make_async_copy(ins[a].at[me], outs[a].at[me], local_sems.at[a]) for a in range(na)]
        for cp in mine:
            cp.start()
        sent = []
        for a in range(na):
            for k, (qx, qy) in enumerate(chips):
                sent.append(pltpu.make_async_remote_copy(
                    src_ref=ins[a].at[2 * qx + qy], dst_ref=outs[a].at[me],
                    send_sem=send_sems.at[a, k], recv_sem=recv_sems.at[a, k], device_id=(qx, qy, c), device_id_type=MESH))
        for cp in sent:
            cp.start()
        for a in range(na):
            for k, (qx, qy) in enumerate(chips):
                slot = outs[a].at[2 * qx + qy]
                pltpu.make_async_remote_copy(
                    src_ref=slot, dst_ref=slot, send_sem=send_sems.at[a, k], recv_sem=recv_sems.at[a, k],
                    device_id=(qx, qy, c), device_id_type=MESH).wait_recv()
        for cp in sent:
            cp.wait_send()
        for cp in mine:
            cp.wait()

    any_spec = pl.BlockSpec(memory_space=pl.ANY)
    return pl.pallas_call(
        body, name=name,
        out_shape=[jax.ShapeDtypeStruct(p.shape, p.dtype) for p in sums],
        in_specs=[any_spec] * na, out_specs=[any_spec] * na,
        scratch_shapes=[pltpu.SemaphoreType.DMA((na, 3)), pltpu.SemaphoreType.DMA((na, 3)), pltpu.SemaphoreType.DMA((na,))],
        compiler_params=_cparams(),
    )(*sums)


def _gather_row(name, vec, width):
    n = vec.shape[0]
    rows = jnp.pad(vec, (0, width - n)).reshape(SUBLANES_F32, width // SUBLANES_F32)
    return _allgather_small(name, rows).reshape(N_DEV, width)[:, :n]


def _layer_fwd(x, mod, w):
    sh1, sc1, g1, sh2, sc2, g2 = [mod[i:i + 1] for i in range(N_MOD)]
    S = x.shape[0]
    h1 = _normmod_fwd(x, w["norm1_w"], sc1, sh1)
    proj = _matmul("proj_in", h1, w["w_in"], "nn")
    a1, a_out = _conv_a_fwd(proj, w["conv_a_w"], w["conv_a_b"], w["ln_a_w"], w["ln_a_b"])
    qkv = _qkv_views(proj)
    os, ls = zip(*[_attn_fwd(v, dil) for v, dil in zip(qkv, DILATIONS)])
    att, lse, att_b = _attn_merge(os, ls)
    of, st_f = _hgrn_fwd(proj, w["lb_f"], ZF, False)
    ob, st_b = _hgrn_fwd(proj, w["lb_b"], ZB, True)
    rec = _hgrn_post_fwd(of, ob, proj, w["rec_norm_w"])
    mixed = jnp.concatenate([att_b, rec, a_out], axis=1)
    y1 = _matmul("proj_out", mixed, w["w_out"], "nn")
    x2, h2 = _gate_add_normmod(x, y1, g1, w["norm2_w"], sc2, sh2)
    u = _matmul("ffn_up", h2, w["w_up"], "nn")
    act = _ffn_mid_fwd(u, w["conv_f_w"])
    y2 = _matmul("ffn_down", act, w["w_down"], "nn")
    x3 = _gate_add(x2, y2, g2)
    saved = dict(x=x, h1=h1, proj=proj, a1=a1, qkv=qkv, att=att, lse=lse, of=of, ob=ob, st_f=st_f, st_b=st_b,
                 mixed=mixed, y1=y1, x2=x2, h2=h2, u=u, act=act, y2=y2)
    return x3, saved


def _layer_bwd(gx3, mod, w, s):
    sh1, sc1, g1, sh2, sc2, g2 = [mod[i:i + 1] for i in range(N_MOD)]
    S = gx3.shape[0]
    g = {}
    gy2, gg2 = _gate_bwd(gx3, s["y2"], g2)
    gact = _matmul("ffn_down_dx", gy2, w["w_down"], "nt")
    g["w_down"] = _matmul("ffn_down_dw", s["act"], gy2, "tn")
    gu, g["conv_f_w"] = _ffn_mid_bwd(s["u"], gact, w["conv_f_w"])
    gh2 = _matmul_nt_pieces("ffn_up_dx", gu, (w["w_up"][:, :D_FF], w["w_up"][:, D_FF:]))
    g["w_up"] = jnp.concatenate([_matmul("ffn_up_dw", s["h2"], t, "tn") for t in gu], axis=1)
    gx2, gy1, g["norm2_w"], gsc2, gsh2, gg1 = _normmod_gate_bwd(s["x2"], gh2, gx3, w["norm2_w"], sc2, sh2, s["y1"], g1)
    gmixed = _matmul("proj_out_dx", gy1, w["w_out"], "nt")
    g["w_out"] = _matmul("proj_out_dw", s["mixed"], gy1, "tn")
    go, ggr, g["rec_norm_w"] = _hgrn_post_bwd(s["of"], s["ob"], s["proj"], gmixed, w["rec_norm_w"])
    gq_f, gz_f, gv_f, g["lb_f"] = _hgrn_bwd(s["proj"], w["lb_f"], s["st_f"], go, ZF, False)
    gq_r, gz_b, gv_r, g["lb_b"] = _hgrn_bwd(s["proj"], w["lb_b"], s["st_b"], go, ZB, True, other=(gq_f, gv_f))
    dos, lds = _attn_bwd_prep(gmixed, s["att"], s["lse"])
    gqkv = zip(*[_attn_bwd(v, do, ld, dil) for v, do, ld, dil in zip(s["qkv"], dos, lds, DILATIONS)])
    gq_a, gk_a, gv_a = [_sum3_bf16(lst, S, ATT_W) for lst in gqkv]
    gav, gag, gcw, g["conv_a_b"], g["ln_a_w"], g["ln_a_b"] = _conv_a_bwd(
        s["proj"], s["a1"], gmixed, w["conv_a_w"], w["ln_a_w"], w["ln_a_b"])
    g["conv_a_w"] = gcw[:CONV_W]
    gproj = jnp.concatenate([gq_a, gk_a, gv_a, gq_r, gz_f, gz_b, gv_r, ggr, gav, gag,
                             jnp.zeros((S, IN_COLS_PAD - IN_COLS), BF16)], axis=1)
    gh1 = _matmul("proj_in_dx", gproj, w["w_in"], "nt")
    g["w_in"] = _matmul("proj_in_dw", s["h1"], gproj, "tn")
    gx, g["norm1_w"], gsc1, gsh1 = _normmod_bwd(s["x"], gh1, gx2, w["norm1_w"], sc1, sh1)
    gmod = jnp.concatenate([gsh1, gsc1, gg1, gsh2, gsc2, gg2], axis=0)
    return gx, gmod, g


def _permute_in_cols(t):
    pad = jnp.zeros(t.shape[:-1] + (IN_COLS_PAD - IN_COLS,), t.dtype)
    return jnp.concatenate([t[..., CONV_COLS:], t[..., :CONV_COLS], pad], axis=-1)


def _unpermute_in_cols(t):
    return jnp.concatenate([t[..., IN_COLS - CONV_COLS:IN_COLS], t[..., :IN_COLS - CONV_COLS]], axis=-1)


def _cols_from_gathered(t, lead):
    nd = t.ndim
    perm = tuple(range(1, nd - 1)) + (0, nd - 1)
    t = t.transpose(perm)
    return t.reshape(t.shape[:-2] + (t.shape[-2] * t.shape[-1],))


def _cols_to_parts(t):
    L, R, C = t.shape
    return t.reshape(L * R, N_DEV, C // N_DEV).transpose(1, 0, 2)


SMALL_REPL = (("norm1_w", D), ("conv_a_b", CONV_CH), ("ln_a_w", CONV_CH), ("ln_a_b", CONV_CH),
              ("rec_norm_w", REC_W), ("norm2_w", D))


def kernel(x, c, w_ada, b_ada, norm1_w, w_in, conv_a_w, conv_a_b, ln_a_w, ln_a_b, lb_gamma, rec_norm_w, w_out, norm2_w, w_up, conv_f_w, w_down, final_norm_w, loss_target, m_w_ada, m_b_ada, m_norm1_w, m_w_in, m_conv_a_w, m_conv_a_b, m_ln_a_w, m_ln_a_b, m_lb_gamma, m_rec_norm_w, m_w_out, m_norm2_w, m_w_up, m_conv_f_w, m_w_down, m_final_norm_w, v_w_ada, v_b_ada, v_norm1_w, v_w_in, v_conv_a_w, v_conv_a_b, v_ln_a_w, v_ln_a_b, v_lb_gamma, v_rec_norm_w, v_w_out, v_norm2_w, v_w_up, v_conv_f_w, v_w_down, v_final_norm_w):
    px, py, pc = _place()
    me = _index_of((px, py, pc))
    xs, tgt = x[0], loss_target[0]
    S = xs.shape[0]
    ada_cols = w_ada.shape[2]

    big = [w_in.reshape(DEPTH * D, -1), w_up.reshape(DEPTH * D, -1), w_out.reshape(-1, D), w_down.reshape(-1, D)]
    g_in, g_up, g_out, g_down = _allgather_big("gather_weights", [t.astype(BF16) for t in big])
    w_in_f = _permute_in_cols(_cols_from_gathered(g_in.reshape(N_DEV, DEPTH, D, -1), 1))
    w_up_f = _cols_from_gathered(g_up.reshape(N_DEV, DEPTH, D, -1), 1)
    w_out_f = g_out.reshape(N_DEV, DEPTH, D // N_DEV, D).transpose(1, 0, 2, 3).reshape(DEPTH, D, D)
    w_out_f = jnp.concatenate([w_out_f[:, CONV_CH:], w_out_f[:, :CONV_CH]], axis=1)
    w_down_f = g_down.reshape(N_DEV, DEPTH, D_FF // N_DEV, D).transpose(1, 0, 2, 3).reshape(DEPTH, D_FF, D)

    small_in = jnp.concatenate([c.reshape(-1), conv_a_w.reshape(-1), lb_gamma.reshape(-1), conv_f_w.reshape(-1)])
    gs = _gather_row("gather_small", small_in, 8192)
    o1 = D
    o2 = o1 + conv_a_w.size
    o3 = o2 + lb_gamma.size
    c_all = gs[:, :o1]
    conv_a_f = _cols_from_gathered(gs[:, o1:o2].reshape(N_DEV, DEPTH, CONV_W, -1), 1)
    lb_gamma_f = _cols_from_gathered(gs[:, o2:o3].reshape(N_DEV, DEPTH, 2, -1), 1)
    conv_f_f = _cols_from_gathered(gs[:, o3:].reshape(N_DEV, DEPTH, FFN_CONV_W, -1), 1)
    conv_a_pad = jnp.pad(conv_a_f, ((0, 0), (0, CONV_W_PAD - CONV_W), (0, 0)))

    b_loc = lax.dynamic_slice_in_dim(b_ada, me * ada_cols, ada_cols, axis=1)

    def mod_fn(c_all_, w_, b_):
        cond = c_all_ * jax.nn.sigmoid(c_all_)
        return (jnp.concatenate([_dot_nn(cond, w_[l], HP) + b_[l] for l in range(DEPTH)], axis=1),)

    (mod_part,) = _vmem_call("ada_mod", mod_fn, [c_all, w_ada, b_loc[:, None, :]], [((N_DEV, DEPTH * ada_cols), F32)])
    gm = _allgather_small("gather_mod", mod_part).reshape(N_DEV, N_DEV, DEPTH, ada_cols)
    mod = lax.dynamic_index_in_dim(gm, me, axis=1, keepdims=False)
    mod = mod.transpose(1, 0, 2).reshape(DEPTH, N_MOD, D)

    (lb1,) = _vmem_call("lower_bounds", lambda a, b: (_lower_bounds_f(a, b),), [lb_gamma_f[0], lb_gamma_f[1]], [((2, REC_W), F32)])
    lb4 = jnp.concatenate([jnp.zeros_like(lb1), lb1], axis=0)

    def layer_weights(l):
        row = lambda t: t[l].reshape(1, -1)
        return dict(norm1_w=row(norm1_w), w_in=w_in_f[l], conv_a_w=conv_a_pad[l], conv_a_b=row(conv_a_b), ln_a_w=row(ln_a_w),
                    ln_a_b=row(ln_a_b), lb_f=lb4[2 * l:2 * l + 1], lb_b=lb4[2 * l + 1:2 * l + 2], rec_norm_w=row(rec_norm_w),
                    w_out=w_out_f[l], norm2_w=row(norm2_w), w_up=w_up_f[l], conv_f_w=conv_f_f[l], w_down=w_down_f[l])

    ws = [layer_weights(l) for l in range(DEPTH)]
    h, saved = xs, []
    for l in range(DEPTH):
        h, s = _layer_fwd(h, mod[l], ws[l])
        saved.append(s)
    gh, g_final, loss_row = _loss_head(h, tgt, final_norm_w.reshape(1, D))
    loss = lax.psum(loss_row[0, 0], ("x", "y", "c"))
    gmods, gws = [None] * DEPTH, [None] * DEPTH
    for l in reversed(range(DEPTH)):
        gh, gmods[l], gws[l] = _layer_bwd(gh, mod[l], ws[l], saved[l])
    grad_x = gh[None]

    glb1 = jnp.concatenate([gws[1]["lb_f"], gws[1]["lb_b"]], axis=0)

    def lb_bwd_fn(a, b, g1):
        _, vjp = jax.vjp(_lower_bounds_f, a, b)
        return vjp(g1)

    g_lb_gamma = jnp.stack(_vmem_call("lower_bounds_bwd", lb_bwd_fn, [lb_gamma_f[0], lb_gamma_f[1], glb1], [((2, REC_W), F32)] * 2))
    pieces = [jnp.stack(gmods).reshape(-1)]
    for l in range(DEPTH):
        pieces += [gws[l][n].reshape(-1) for n, _ in SMALL_REPL]
    pieces += [g_final.reshape(-1)]
    pieces += [jnp.stack([gws[l]["conv_a_w"] for l in range(DEPTH)]).reshape(-1), g_lb_gamma.reshape(-1),
               jnp.stack([gws[l]["conv_f_w"] for l in range(DEPTH)]).reshape(-1)]
    small_g = jnp.concatenate(pieces)
    n_small = small_g.shape[0]
    gsm = _gather_row("gather_small_grads", small_g, 71680)
    n_mod = DEPTH * N_MOD * D
    gmod_all = gsm[:, :n_mod].reshape(N_DEV, DEPTH, N_MOD * D)
    gmod_loc = lax.dynamic_slice_in_dim(gmod_all, me * ada_cols, ada_cols, axis=2).transpose(1, 0, 2)

    def small_fn(gsm_, c_all_, gm_):
        cond = c_all_ * jax.nn.sigmoid(c_all_)
        gw = jnp.concatenate([_dot_tn(cond, gm_[l], HP) for l in range(DEPTH)], axis=0)
        return jnp.sum(gsm_, axis=0, keepdims=True), gw

    tot, g_w_ada = _vmem_call("small_grads", small_fn, [gsm, c_all, gmod_loc],
                              [((1, n_small), F32), ((DEPTH * D, ada_cols), F32)])
    tot = tot[0]
    grads = {"w_ada": g_w_ada.reshape(DEPTH, D, ada_cols), "b_ada": tot[:n_mod].reshape(DEPTH, N_MOD * D)}
    pos = n_mod
    per_layer = {n: [] for n, _ in SMALL_REPL}
    for l in range(DEPTH):
        for n, width in SMALL_REPL:
            per_layer[n].append(tot[pos:pos + width])
            pos += width
    for n, _ in SMALL_REPL:
        grads[n] = jnp.stack(per_layer[n])
    grads["final_norm_w"] = tot[pos:pos + D]
    pos += D
    n_ca, n_lb, n_cf = DEPTH * CONV_W * CONV_CH, DEPTH * 2 * REC_W, DEPTH * FFN_CONV_W * 2 * D_FF
    g_ca = tot[pos:pos + n_ca].reshape(DEPTH, CONV_W, CONV_CH)
    g_lb = tot[pos + n_ca:pos + n_ca + n_lb].reshape(DEPTH, 2, REC_W)
    g_cf = tot[pos + n_ca + n_lb:pos + n_ca + n_lb + n_cf].reshape(DEPTH, FFN_CONV_W, 2 * D_FF)
    grads["conv_a_w"] = lax.dynamic_slice_in_dim(g_ca, me * conv_a_w.shape[2], conv_a_w.shape[2], axis=2)
    grads["lb_gamma"] = lax.dynamic_slice_in_dim(g_lb, me * lb_gamma.shape[2], lb_gamma.shape[2], axis=2)
    grads["conv_f_w"] = lax.dynamic_slice_in_dim(g_cf, me * conv_f_w.shape[2], conv_f_w.shape[2], axis=2)

    gw_in = _unpermute_in_cols(jnp.stack([gws[l]["w_in"] for l in range(DEPTH)]))
    gw_up = jnp.stack([gws[l]["w_up"] for l in range(DEPTH)])
    gw_out = jnp.stack([gws[l]["w_out"] for l in range(DEPTH)])
    gw_out = jnp.concatenate([gw_out[:, D - CONV_CH:], gw_out[:, :D - CONV_CH]], axis=1)
    gw_down = jnp.stack([gws[l]["w_down"] for l in range(DEPTH)])
    rows_to_parts = lambda t: t.reshape(DEPTH, N_DEV, -1, D).transpose(1, 0, 2, 3).reshape(N_DEV, -1, D)
    parts = [_cols_to_parts(gw_in), _cols_to_parts(gw_up), rows_to_parts(gw_out), rows_to_parts(gw_down)]
    parts = [t.astype(BF16) for t in parts]
    from_sibling = _scatter_to_sibling("scatter_sibling", parts)
    sums = [_pair_sum("pair_sum", p, r) for p, r in zip(parts, from_sibling)]
    r_in, r_up, r_out, r_down = _scatter_to_chips("scatter_chips", sums)

    given = dict(w_ada=(w_ada, m_w_ada, v_w_ada), b_ada=(b_ada, m_b_ada, v_b_ada), norm1_w=(norm1_w, m_norm1_w, v_norm1_w),
                 w_in=(w_in, m_w_in, v_w_in), conv_a_w=(conv_a_w, m_conv_a_w, v_conv_a_w), conv_a_b=(conv_a_b, m_conv_a_b, v_conv_a_b),
                 ln_a_w=(ln_a_w, m_ln_a_w, v_ln_a_w), ln_a_b=(ln_a_b, m_ln_a_b, v_ln_a_b), lb_gamma=(lb_gamma, m_lb_gamma, v_lb_gamma),
                 rec_norm_w=(rec_norm_w, m_rec_norm_w, v_rec_norm_w), w_out=(w_out, m_w_out, v_w_out),
                 norm2_w=(norm2_w, m_norm2_w, v_norm2_w), w_up=(w_up, m_w_up, v_w_up), conv_f_w=(conv_f_w, m_conv_f_w, v_conv_f_w),
                 w_down=(w_down, m_w_down, v_w_down), final_norm_w=(final_norm_w, m_final_norm_w, v_final_norm_w))
    big_parts = dict(w_in=r_in, w_up=r_up, w_out=r_out, w_down=r_down)
    names = list(given)
    res = {}
    for n in names:
        w_, m_, v_ = given[n]
        shape = w_.shape
        C = shape[-1]
        two_d = lambda t: t.reshape(-1, C)
        gp = big_parts[n] if n in big_parts else two_d(grads[n])[None]
        res[n] = [t.reshape(shape) for t in _adamw("adamw_" + n, two_d(w_), two_d(m_), two_d(v_), gp)]
    return (loss, grad_x, *[res[n][0] for n in names], *[res[n][1] for n in names],
            *[res[n][2] for n in names], *[res[n][3] for n in names])
```

```python
import functools

import jax
import jax.numpy as jnp
from jax import lax
from jax.experimental import pallas as pl
from jax.experimental.pallas import tpu as pltpu

F32 = jnp.float32
BF16 = jnp.bfloat16
HP = lax.Precision.HIGHEST
MESH = pl.DeviceIdType.MESH

N_DEV = 8
D = 1024
DEPTH = 2
CONV_CH = 256
CONV_W = 31
CONV_W_PAD = 32
ATT_W = 384
REC_W = 384
N_HEADS = 6
HEAD = 64
HEAD_SHIFT = 6
HALF_BAND = 64
ATT_BLK = 128
DILATIONS = (1, 4, 16)
ALIBI_SLOPES = tuple(float(2.0 ** (-8.0 * (h + 1) / N_HEADS)) for h in range(N_HEADS))
MASK_VALUE = -1e30
REC_CHUNK = 64
EXP_CLAMP = 80.0
F_TINY = 1e-30
IN_COLS = 3584
IN_COLS_PAD = IN_COLS
QKV_BLOCKS = 3
D_FF = 2816
FFN_CONV_W = 3
N_MOD = 6
EPS = 1e-6
ADAM_LR, ADAM_B1, ADAM_B2, ADAM_EPS, ADAM_WD, ADAM_STEP = 0.001, 0.9, 0.999, 1e-08, 0.01, 10

VMEM_LIMIT_BYTES = 56 * 1024 * 1024
SUBLANES_F32 = 8
LANES = 128

QA, KA, VA, QR, ZF, ZB, IR, GR = range(8)
AV_BLK, AG_BLK = 12, 13
CONV_COLS = 2 * CONV_CH


def _cparams(sem=None):
    kw = dict(vmem_limit_bytes=VMEM_LIMIT_BYTES)
    if sem is not None:
        kw["dimension_semantics"] = sem
    return pltpu.CompilerParams(**kw)


def _iota(shape, dim):
    return lax.broadcasted_iota(jnp.int32, shape, dim)


def _dot(a, b, dims, precision=None):
    return lax.dot_general(a, b, (dims, ((), ())), precision=precision, preferred_element_type=F32)


def _dot_nn(a, b, precision=None):
    return _dot(a, b, ((1,), (0,)), precision)


def _dot_nt(a, b, precision=None):
    return _dot(a, b, ((1,), (1,)), precision)


def _dot_tn(a, b, precision=None):
    return _dot(a, b, ((0,), (0,)), precision)


def _c0(j):
    return 0


def _pick(n, cands):
    for c in cands:
        if n % c == 0:
            return c
    return n


MATMUL_OUT_TILE_BYTES = 8 * 1024 * 1024


def _div_lanes(n, cap):
    best = None
    for d in range(LANES, min(n, cap) + 1, LANES):
        if n % d == 0:
            best = d
    return best if best is not None else n


def _matmul_tiles(mode, M, N, K):
    if mode == "nn":
        tm = _pick(M, (1024, 512, 256, 128))
        return tm, _div_lanes(N, MATMUL_OUT_TILE_BYTES // (4 * tm)), K
    if mode == "nt":
        return _pick(M, (512, 256, 128)), N, K
    tm = _div_lanes(M, 1408)
    return tm, _div_lanes(N, MATMUL_OUT_TILE_BYTES // (4 * tm)), _pick(K, (1024, 512, 256))


def _matmul(name, a, b, mode, out_dtype=F32):
    if mode == "nn":
        (M, K), (_, N) = a.shape, b.shape
    elif mode == "nt":
        (M, K), (N, _) = a.shape, b.shape
    else:
        (K, M), (_, N) = a.shape, b.shape
    tm, tn, tk = _matmul_tiles(mode, M, N, K)
    nk = K // tk
    if mode == "nn":
        a_spec = pl.BlockSpec((tm, tk), lambda i, j, k: (i, k))
        b_spec = pl.BlockSpec((tk, tn), lambda i, j, k: (k, j))
        dims = ((1,), (0,))
    elif mode == "nt":
        a_spec = pl.BlockSpec((tm, tk), lambda i, j, k: (i, k))
        b_spec = pl.BlockSpec((tn, tk), lambda i, j, k: (j, k))
        dims = ((1,), (1,))
    else:
        a_spec = pl.BlockSpec((tk, tm), lambda i, j, k: (k, i))
        b_spec = pl.BlockSpec((tk, tn), lambda i, j, k: (k, j))
        dims = ((0,), (0,))

    def body_whole(a_ref, b_ref, o_ref):
        o_ref[...] = _dot(a_ref[...].astype(BF16), b_ref[...].astype(BF16), dims).astype(o_ref.dtype)

    def body(a_ref, b_ref, o_ref, acc_ref):
        k = pl.program_id(2)
        part = _dot(a_ref[...].astype(BF16), b_ref[...].astype(BF16), dims)

        @pl.when(k == 0)
        def _():
            acc_ref[...] = part

        @pl.when(k > 0)
        def _():
            acc_ref[...] += part

        @pl.when(k == nk - 1)
        def _():
            o_ref[...] = acc_ref[...].astype(o_ref.dtype)

    return pl.pallas_call(
        body_whole if nk == 1 else body, name=name, grid=(M // tm, N // tn, nk),
        in_specs=[a_spec, b_spec],
        out_specs=pl.BlockSpec((tm, tn), lambda i, j, k: (i, j)),
        out_shape=jax.ShapeDtypeStruct((M, N), out_dtype),
        scratch_shapes=[] if nk == 1 else [pltpu.VMEM((tm, tn), F32)],
        compiler_params=_cparams(("parallel", "parallel", "arbitrary")),
    )(a, b)


def _matmul_nt_pieces(name, a_pieces, b_pieces):
    M, N = a_pieces[0].shape[0], b_pieces[0].shape[0]
    tm = _pick(M, (512, 256, 128))
    n = len(a_pieces)

    def body(*refs):
        acc = _dot(refs[0][...].astype(BF16), refs[n][...].astype(BF16), ((1,), (1,)))
        for p in range(1, n):
            acc = acc + _dot(refs[p][...].astype(BF16), refs[n + p][...].astype(BF16), ((1,), (1,)))
        refs[2 * n][...] = acc

    in_specs = ([pl.BlockSpec((tm, a.shape[1]), lambda i: (i, 0)) for a in a_pieces]
                + [pl.BlockSpec(b.shape, lambda i: (0, 0)) for b in b_pieces])
    return pl.pallas_call(
        body, name=name, grid=(M // tm,), in_specs=in_specs,
        out_specs=pl.BlockSpec((tm, N), lambda i: (i, 0)), out_shape=jax.ShapeDtypeStruct((M, N), F32),
        compiler_params=_cparams(("parallel",)),
    )(*a_pieces, *b_pieces)


def _rowwise(name, fn, S, ts, tiles, params=(), outs=(), accs=(), halo=0, ncb=1):
    in_specs, args, scratch = [], [], []
    for arr, w, jm, with_halo in tiles:
        if isinstance(with_halo, int) and with_halo > 1:
            d = with_halo
            in_specs.append(pl.BlockSpec((ts // d, d * w), lambda j, i: (i, 0)))
            args.append(arr)
            scratch.append(pltpu.VMEM((w // LANES, ts, LANES), F32))
        elif with_halo:
            hb, nhb = ts // halo, S // halo
            in_specs += [
                pl.BlockSpec((halo, w), lambda j, i, jm=jm, hb=hb: (jnp.maximum(i * hb - 1, 0), jm(j))),
                pl.BlockSpec((ts, w), lambda j, i, jm=jm: (i, jm(j))),
                pl.BlockSpec((halo, w), lambda j, i, jm=jm, hb=hb, nhb=nhb: (jnp.minimum((i + 1) * hb, nhb - 1), jm(j))),
            ]
            args += [arr, arr, arr]
        else:
            in_specs.append(pl.BlockSpec((ts, w), lambda j, i, jm=jm: (i, jm(j))))
            args.append(arr)
    for arr, r, w, jm in params:
        in_specs.append(pl.BlockSpec((r, w), lambda j, i, jm=jm: (0, jm(j))))
        args.append(arr)
    out_specs, out_shape = [], []
    for w, dt, jm, tw, *dil in outs:
        if dil:
            out_specs.append(pl.BlockSpec((ts // dil[0], dil[0] * w), lambda j, i: (i, 0)))
            out_shape.append(jax.ShapeDtypeStruct((S // dil[0], dil[0] * w), dt))
            scratch += [pltpu.VMEM((ts, w), F32), pltpu.VMEM((w // LANES, ts, LANES), F32)]
        else:
            out_specs.append(pl.BlockSpec((ts, w), lambda j, i, jm=jm: (i, jm(j))))
            out_shape.append(jax.ShapeDtypeStruct((S, tw), dt))
    for r, w, jm, tw in accs:
        out_specs.append(pl.BlockSpec((r, w), lambda j, i, jm=jm: (0, jm(j))))
        out_shape.append(jax.ShapeDtypeStruct((r, tw), F32))
    n_tiles, n_params, n_outs, n_accs = len(tiles), len(params), len(outs), len(accs)

    def residue_rows(r, d):
        return pl.ds(r, ts // d, stride=d)

    def body(*refs):
        i = pl.program_id(1)
        n_io = len(in_specs) + n_outs + n_accs
        scr = list(refs[n_io:])
        refs = refs[:n_io]
        pos, vals = 0, []
        for _, w, _, with_halo in tiles:
            if isinstance(with_halo, int) and with_halo > 1:
                d, buf = with_halo, scr.pop(0)
                for r in range(d):
                    for c in range(w // LANES):
                        buf[c, residue_rows(r, d), :] = refs[pos][:, r * w + c * LANES:r * w + (c + 1) * LANES].astype(F32)
                vals.append(jnp.concatenate([buf[c] for c in range(w // LANES)], axis=1))
                pos += 1
            elif with_halo:
                before, after = refs[pos][...], refs[pos + 2][...]
                before = jnp.where(i > 0, before, jnp.zeros_like(before))
                after = jnp.where(i < S // ts - 1, after, jnp.zeros_like(after))
                vals.append(jnp.concatenate([before, refs[pos + 1][...], after], axis=0))
                pos += 3
            else:
                vals.append(refs[pos][...])
                pos += 1
        prefs = refs[pos:pos + n_params]
        orefs = list(refs[pos + n_params:pos + n_params + n_outs])
        arefs = refs[pos + n_params + n_outs:]
        staged = []
        for k, (w, _, _, _, *dil) in enumerate(outs):
            if dil:
                staged.append((orefs[k], scr.pop(0), scr.pop(0), w, dil[0]))
                orefs[k] = staged[-1][1]

        @pl.when(i == 0)
        def _():
            for r in arefs:
                r[...] = jnp.zeros_like(r)

        fn(i, vals, prefs, orefs, arefs)
        for out_ref, flat, buf, w, d in staged:
            for c in range(w // LANES):
                buf[c] = flat[:, c * LANES:(c + 1) * LANES]
                for r in range(d):
                    out_ref[:, r * w + c * LANES:r * w + (c + 1) * LANES] = buf[c, residue_rows(r, d), :].astype(out_ref.dtype)

    res = pl.pallas_call(
        body, name=name, grid=(ncb, S // ts),
        in_specs=in_specs, out_specs=out_specs, out_shape=out_shape, scratch_shapes=scratch,
        compiler_params=_cparams(("arbitrary", "arbitrary")),
    )(*args)
    return res


def _vmem_call(name, fn, ins, out_shapes):
    n_in = len(ins)

    def body(*refs):
        vals = fn(*[r[...] for r in refs[:n_in]])
        for r, v in zip(refs[n_in:], vals):
            r[...] = v.astype(r.dtype)

    return pl.pallas_call(
        body, name=name,
        out_shape=[jax.ShapeDtypeStruct(s, dt) for s, dt in out_shapes],
        compiler_params=_cparams(),
    )(*ins)


def _rms(x, w):
    return x * lax.rsqrt(jnp.mean(x * x, axis=-1, keepdims=True) + EPS) * w


def _normmod_f(x, nw, sc, sh):
    return _rms(x, nw) * (1.0 + sc) + sh


def _row_params(*vecs):
    return [(v, 1, v.shape[1], _c0) for v in vecs]


def _normmod_fwd(x, nw, sc, sh):
    S = x.shape[0]

    def fn(i, vals, p, o, a):
        o[0][...] = _normmod_f(vals[0], p[0][...], p[1][...], p[2][...]).astype(BF16)

    return _rowwise("normmod_fwd", fn, S, 512, [(x, D, _c0, False)], _row_params(nw, sc, sh), [(D, BF16, _c0, D)])[0]


def _normmod_bwd(x, gh, gres, nw, sc, sh):
    S = x.shape[0]

    def fn(i, vals, p, o, a):
        _, vjp = jax.vjp(_normmod_f, vals[0], p[0][...], p[1][...], p[2][...])
        gx, gnw, gsc, gsh = vjp(vals[1])
        o[0][...] = gx + vals[2]
        a[0][...] += gnw
        a[1][...] += gsc
        a[2][...] += gsh

    return _rowwise("normmod_bwd", fn, S, 512, [(x, D, _c0, False), (gh, D, _c0, False), (gres, D, _c0, False)],
                    _row_params(nw, sc, sh), [(D, F32, _c0, D)], [(1, D, _c0, D)] * 3)


def _gate_add(x, y, g):
    S = x.shape[0]

    def fn(i, vals, p, o, a):
        o[0][...] = vals[0] + p[0][...] * vals[1]

    return _rowwise("gate_add", fn, S, 512, [(x, D, _c0, False), (y, D, _c0, False)], _row_params(g), [(D, F32, _c0, D)])[0]


def _gate_bwd(gx, y, g):
    S = gx.shape[0]

    def fn(i, vals, p, o, a):
        o[0][...] = (vals[0] * p[0][...]).astype(BF16)
        a[0][...] += jnp.sum(vals[0] * vals[1], axis=0, keepdims=True)

    return _rowwise("gate_bwd", fn, S, 512, [(gx, D, _c0, False), (y, D, _c0, False)], _row_params(g),
                    [(D, BF16, _c0, D)], [(1, D, _c0, D)])


def _gate_add_normmod(x, y, g, nw, sc, sh):
    S = x.shape[0]

    def fn(i, vals, p, o, a):
        x2 = vals[0] + p[0][...] * vals[1]
        o[0][...] = x2
        o[1][...] = _normmod_f(x2, p[1][...], p[2][...], p[3][...]).astype(BF16)

    return _rowwise("gate_add_normmod", fn, S, 512, [(x, D, _c0, False), (y, D, _c0, False)], _row_params(g, nw, sc, sh),
                    [(D, F32, _c0, D), (D, BF16, _c0, D)])


def _normmod_gate_bwd(x, gh, gres, nw, sc, sh, y, g):
    S = x.shape[0]

    def fn(i, vals, p, o, a):
        _, vjp = jax.vjp(_normmod_f, vals[0], p[0][...], p[1][...], p[2][...])
        gx, gnw, gsc, gsh = vjp(vals[1])
        gx = gx + vals[2]
        o[0][...] = gx
        o[1][...] = (gx * p[3][...]).astype(BF16)
        a[0][...] += gnw
        a[1][...] += gsc
        a[2][...] += gsh
        a[3][...] += jnp.sum(gx * vals[3], axis=0, keepdims=True)

    tiles = [(t, D, _c0, False) for t in (x, gh, gres, y)]
    return _rowwise("normmod_gate_bwd", fn, S, 512, tiles, _row_params(nw, sc, sh, g),
                    [(D, F32, _c0, D), (D, BF16, _c0, D)], [(1, D, _c0, D)] * 4)


def _loss_head(x, tgt, fw):
    S = x.shape[0]

    def fn(i, vals, p, o, a):
        y, vjp = jax.vjp(_rms, vals[0], p[0][...])
        err = y - vals[1]
        gx, gfw = vjp(err * (1.0 / D))
        o[0][...] = gx
        a[0][...] += gfw
        part = 0.5 * jnp.sum(jnp.mean(err * err, axis=-1, keepdims=True), axis=0, keepdims=True)
        a[1][...] += jnp.broadcast_to(part, (1, LANES))

    return _rowwise("loss_head", fn, S, 256, [(x, D, _c0, False), (tgt, D, _c0, False)], _row_params(fw),
                    [(D, F32, _c0, D)], [(1, D, _c0, D), (1, LANES, _c0, LANES)])


CONV_HALO = 16
CONV_TS = 512


def _shifted(ext, shift, ts, halo):
    n = ext.shape[0]
    s = shift % n
    r = ext if s == 0 else pltpu.roll(ext, s, 0)
    return r[halo:halo + ts]


def _ln_silu(a, w, b):
    mu = jnp.mean(a, axis=-1, keepdims=True)
    var = jnp.mean(jnp.square(a - mu), axis=-1, keepdims=True)
    y = (a - mu) * lax.rsqrt(var + EPS) * w + b
    return y * jax.nn.sigmoid(y)


def _conv_a_fwd(proj, w_pad, b, lnw, lnb):
    S = proj.shape[0]
    ts, H = min(CONV_TS, S), CONV_HALO

    def fn(i, vals, p, o, a):
        a0 = vals[0] * jax.nn.sigmoid(vals[1])
        acc = jnp.zeros((ts, CONV_CH), F32) + p[1][...]
        for k in range(CONV_W):
            acc = acc + _shifted(a0, CONV_W // 2 - k, ts, H) * p[0][pl.ds(k, 1), :]
        o[0][...] = acc
        o[1][...] = _ln_silu(acc, p[2][...], p[3][...]).astype(BF16)

    tiles = [(proj, CONV_CH, lambda j: AV_BLK, True), (proj, CONV_CH, lambda j: AG_BLK, True)]
    params = [(w_pad, CONV_W_PAD, CONV_CH, _c0)] + _row_params(b, lnw, lnb)
    return _rowwise("conv_a_fwd", fn, S, ts, tiles, params, [(CONV_CH, F32, _c0, CONV_CH), (CONV_CH, BF16, _c0, CONV_CH)], halo=H)


def _conv_a_bwd(proj, a1, gmixed, w_pad, lnw, lnb):
    S = proj.shape[0]
    ts, H = min(CONV_TS, S), CONV_HALO

    def fn(i, vals, p, o, a):
        av, ag, a1e, ge = vals
        lw, lb = p[1][...], p[2][...]
        _, vjp_e = jax.vjp(lambda t: _ln_silu(t, lw, lb), a1e)
        (ga1e,) = vjp_e(ge)
        c = slice(H, H + ts)
        _, vjp_c = jax.vjp(_ln_silu, a1e[c], lw, lb)
        ga1, glw, glb = vjp_c(ge[c])
        a[1][...] += jnp.sum(ga1, axis=0, keepdims=True)
        a[2][...] += glw
        a[3][...] += glb
        sg = jax.nn.sigmoid(ag)
        a0 = av * sg
        ga0 = jnp.zeros((ts, CONV_CH), F32)
        for k in range(CONV_W):
            a[0][pl.ds(k, 1), :] += jnp.sum(ga1 * _shifted(a0, CONV_W // 2 - k, ts, H), axis=0, keepdims=True)
            ga0 = ga0 + _shifted(ga1e, k - CONV_W // 2, ts, H) * p[0][pl.ds(k, 1), :]
        sgc, avc = sg[c], av[c]
        o[0][...] = (ga0 * sgc).astype(BF16)
        o[1][...] = (ga0 * avc * sgc * (1.0 - sgc)).astype(BF16)

    tiles = [(proj, CONV_CH, lambda j: AV_BLK, True), (proj, CONV_CH, lambda j: AG_BLK, True),
             (a1, CONV_CH, _c0, True), (gmixed, CONV_CH, lambda j: 3, True)]
    params = [(w_pad, CONV_W_PAD, CONV_CH, _c0)] + _row_params(lnw, lnb)
    outs = [(CONV_CH, BF16, _c0, CONV_CH), (CONV_CH, BF16, _c0, CONV_CH)]
    accs = [(CONV_W_PAD, CONV_CH, _c0, CONV_CH)] + [(1, CONV_CH, _c0, CONV_CH)] * 3
    return _rowwise("conv_a_bwd", fn, S, ts, tiles, params, outs, accs, halo=H)


FFN_HALO = 8
FFN_TS = 512
FFN_TS_FWD = 1024
FFN_CB = 256
FFN_NCB = D_FF // FFN_CB


def _gelu_mul(g, v):
    return 0.5 * g * (1.0 + lax.erf(g * (2.0 ** -0.5))) * v


def _ffn_mid_fwd(u, cw):
    S = u.shape[0]
    ts, H = min(FFN_TS_FWD, S), FFN_HALO

    def conv(ext, w_ref):
        acc = jnp.zeros((ts, FFN_CB), F32)
        for k in range(FFN_CONV_W):
            acc = acc + _shifted(ext, 1 - k, ts, H) * w_ref[pl.ds(k, 1), :]
        return acc

    def fn(i, vals, p, o, a):
        o[0][...] = _gelu_mul(conv(vals[0], p[0]), conv(vals[1], p[1])).astype(BF16)

    gate, val = (lambda j: j), (lambda j: j + FFN_NCB)
    return _rowwise("ffn_mid_fwd", fn, S, ts, [(u, FFN_CB, gate, True), (u, FFN_CB, val, True)],
                    [(cw, FFN_CONV_W, FFN_CB, gate), (cw, FFN_CONV_W, FFN_CB, val)],
                    [(FFN_CB, BF16, gate, D_FF)], halo=H, ncb=FFN_NCB)[0]


def _ffn_mid_bwd(u, gact, cw):
    S = u.shape[0]
    ts, H = min(FFN_TS, S), FFN_HALO
    n = ts + 2 * H

    def fn(i, vals, p, o, a):
        ug, uv, ga = vals

        def conv_all(ue, w_ref):
            acc = jnp.zeros((n, FFN_CB), F32)
            for k in range(FFN_CONV_W):
                s = (1 - k) % n
                acc = acc + (ue if s == 0 else pltpu.roll(ue, s, 0)) * w_ref[pl.ds(k, 1), :]
            return acc

        _, vjp = jax.vjp(_gelu_mul, conv_all(ug, p[0]), conv_all(uv, p[1]))
        for half, (gc, ue) in enumerate(zip(vjp(ga), (ug, uv))):
            gu = jnp.zeros((ts, FFN_CB), F32)
            for k in range(FFN_CONV_W):
                gu = gu + _shifted(gc, k - 1, ts, H) * p[half][pl.ds(k, 1), :]
                a[half][pl.ds(k, 1), :] += jnp.sum(gc[H:H + ts] * _shifted(ue, 1 - k, ts, H), axis=0, keepdims=True)
            o[half][...] = gu.astype(BF16)

    gate, val = (lambda j: j), (lambda j: j + FFN_NCB)
    tiles = [(u, FFN_CB, gate, True), (u, FFN_CB, val, True), (gact, FFN_CB, gate, True)]
    params = [(cw, FFN_CONV_W, FFN_CB, gate), (cw, FFN_CONV_W, FFN_CB, val)]
    gu_gate, gu_val, gw_gate, gw_val = _rowwise("ffn_mid_bwd", fn, S, ts, tiles, params, [(FFN_CB, BF16, gate, D_FF)] * 2,
                                                [(FFN_CONV_W, FFN_CB, gate, D_FF)] * 2, halo=H, ncb=FFN_NCB)
    return (gu_gate, gu_val), jnp.concatenate([gw_gate, gw_val], axis=1)


LD_W = LANES
PAIR_W = 2 * HEAD
N_PAIRS = N_HEADS // 2


def _sub_view(t, d):
    S, C = t.shape
    return t.reshape(S // d, d * C)


def _sub_halo_specs(width, col, blk, hb, nhb):
    per = blk // hb
    return [
        pl.BlockSpec((hb, width), lambda r, i: (jnp.maximum(i * per - 1, 0), col(r))),
        pl.BlockSpec((blk, width), lambda r, i: (i, col(r))),
        pl.BlockSpec((hb, width), lambda r, i: (jnp.minimum((i + 1) * per, nhb - 1), col(r))),
    ]


def _pick_lane(t, lane):
    return jnp.sum(jnp.where(_iota((1, t.shape[1]), 1) == lane, t, 0.0), axis=1, keepdims=True)


def _pair_mask(h2):
    return (_iota((1, PAIR_W), 1) >> HEAD_SHIFT) == h2


def _cat_bf16(a, b, c):
    return jnp.concatenate([a[...], b[...], c[...]], axis=0).astype(BF16)


def _attn_fwd(view, dil):
    L = view.shape[0]
    blk, hb = min(ATT_BLK, L), HALF_BAND
    span = blk + 2 * hb

    def body(q_ref, kp, kc, kn, vp, vc, vn, o_ref, l_ref):
        i = pl.program_id(1)
        rel = _iota((blk, span), 1) - hb - _iota((blk, span), 0)
        kpos = i * blk - hb + _iota((blk, span), 1)
        valid = (jnp.abs(rel) <= hb) & (kpos >= 0) & (kpos < L)
        dist = jnp.abs(rel).astype(F32) * float(dil)
        q, k, v = q_ref[...].astype(BF16), _cat_bf16(kp, kc, kn), _cat_bf16(vp, vc, vn)
        lse = jnp.zeros((blk, LD_W), F32)
        for pr in range(N_PAIRS):
            sl = slice(pr * PAIR_W, (pr + 1) * PAIR_W)
            qp, kpair, vpair = q[:, sl], k[:, sl], v[:, sl]
            o = jnp.zeros((blk, PAIR_W), F32)
            for h2 in range(2):
                h, mask = 2 * pr + h2, _pair_mask(h2)
                s = _dot_nt(jnp.where(mask, qp, jnp.zeros_like(qp)), kpair) * (HEAD ** -0.5) - ALIBI_SLOPES[h] * dist
                s = jnp.where(valid, s, MASK_VALUE)
                m = jnp.max(s, axis=1, keepdims=True)
                p = jnp.exp(s - m)
                l = jnp.sum(p, axis=1, keepdims=True)
                o = jnp.where(mask, _dot_nn(p.astype(BF16), vpair) / l, o)
                lse = lse + jnp.where(_iota((1, LD_W), 1) == h, m + jnp.log(l), 0.0)
            o_ref[:, sl] = o
        l_ref[...] = lse

    nhb = L // hb
    in_specs = ([pl.BlockSpec((blk, ATT_W), lambda r, i: (i, r * QKV_BLOCKS +QA))]
                + _sub_halo_specs(ATT_W, lambda r: r * QKV_BLOCKS +KA, blk, hb, nhb)
                + _sub_halo_specs(ATT_W, lambda r: r * QKV_BLOCKS +VA, blk, hb, nhb))
    return pl.pallas_call(
        body, name=f"attn_fwd_d{dil}", grid=(dil, L // blk), in_specs=in_specs,
        out_specs=[pl.BlockSpec((blk, ATT_W), lambda r, i: (i, r)), pl.BlockSpec((blk, LD_W), lambda r, i: (i, r))],
        out_shape=[jax.ShapeDtypeStruct((L, dil * ATT_W), F32), jax.ShapeDtypeStruct((L, dil * LD_W), F32)],
        compiler_params=_cparams(("parallel", "parallel")),
    )(*([view] * 7))


def _attn_bwd(pview, gview, lview, dil):
    L = pview.shape[0]
    blk, hb = min(ATT_BLK, L), HALF_BAND
    span = blk + 2 * hb
    scale = HEAD ** -0.5

    def body(qp, qc, qn, kp, kc, kn, vp, vc, vn, gp, gc, gn, lp, lc, ln, dq_ref, dk_ref, dv_ref):
        i = pl.program_id(1)
        l = lc[...]
        le = jnp.concatenate([lp[...], l, ln[...]], axis=0)
        rel_q = _iota((blk, span), 1) - hb - _iota((blk, span), 0)
        kpos = i * blk - hb + _iota((blk, span), 1)
        valid_q = (jnp.abs(rel_q) <= hb) & (kpos >= 0) & (kpos < L)
        dist_q = jnp.abs(rel_q).astype(F32) * float(dil)
        rel_k = _iota((span, blk), 1) + hb - _iota((span, blk), 0)
        qpos = i * blk - hb + _iota((span, blk), 0)
        valid_k = (jnp.abs(rel_k) <= hb) & (qpos >= 0) & (qpos < L)
        dist_k = jnp.abs(rel_k).astype(F32) * float(dil)
        q_all, k_all, v_all, g_all = qc[...].astype(BF16), kc[...].astype(BF16), vc[...].astype(BF16), gc[...]
        qe_all, ke_all, ve_all = _cat_bf16(qp, qc, qn), _cat_bf16(kp, kc, kn), _cat_bf16(vp, vc, vn)
        ge_all = _cat_bf16(gp, gc, gn)
        for pr in range(N_PAIRS):
            sl = slice(pr * PAIR_W, (pr + 1) * PAIR_W)
            q, k, v, g = q_all[:, sl], k_all[:, sl], v_all[:, sl], g_all[:, sl]
            qe, ke, ve, ge = qe_all[:, sl], ke_all[:, sl], ve_all[:, sl], ge_all[:, sl]
            dq = jnp.zeros((blk, PAIR_W), F32)
            dk = jnp.zeros((blk, PAIR_W), F32)
            dv = jnp.zeros((blk, PAIR_W), F32)
            for h2 in range(2):
                h, mask = 2 * pr + h2, _pair_mask(h2)
                only = lambda t: jnp.where(mask, t, jnp.zeros_like(t))
                s = _dot_nt(only(q), ke) * scale - ALIBI_SLOPES[h] * dist_q
                p = jnp.where(valid_q, jnp.exp(s - _pick_lane(l, h)), 0.0)
                ds = p * (_dot_nt(only(g), ve) - _pick_lane(l, 8 + h))
                dq = jnp.where(mask, _dot_nn(ds.astype(BF16), ke), dq)
                s = _dot_nt(only(qe), k) * scale - ALIBI_SLOPES[h] * dist_k
                p = jnp.where(valid_k, jnp.exp(s - _pick_lane(le, h)), 0.0)
                dv = jnp.where(mask, _dot_tn(p.astype(BF16), ge), dv)
                ds = p * (_dot_nt(only(ge), v) - _pick_lane(le, 8 + h))
                dk = jnp.where(mask, _dot_tn(ds.astype(BF16), qe), dk)
            dq_ref[:, sl] = (dq * scale).astype(BF16)
            dk_ref[:, sl] = (dk * scale).astype(BF16)
            dv_ref[:, sl] = dv.astype(BF16)

    nhb = L // hb
    in_specs = (_sub_halo_specs(ATT_W, lambda r: r * QKV_BLOCKS +QA, blk, hb, nhb)
                + _sub_halo_specs(ATT_W, lambda r: r * QKV_BLOCKS +KA, blk, hb, nhb)
                + _sub_halo_specs(ATT_W, lambda r: r * QKV_BLOCKS +VA, blk, hb, nhb)
                + _sub_halo_specs(ATT_W, lambda r: r, blk, hb, nhb) + _sub_halo_specs(LD_W, lambda r: r, blk, hb, nhb))
    o_spec = pl.BlockSpec((blk, ATT_W), lambda r, i: (i, r))
    return pl.pallas_call(
        body, name=f"attn_bwd_d{dil}", grid=(dil, L // blk), in_specs=in_specs,
        out_specs=[o_spec] * 3, out_shape=[jax.ShapeDtypeStruct((L, dil * ATT_W), BF16)] * 3,
        compiler_params=_cparams(("parallel", "parallel")),
    )(*([pview] * 9 + [gview] * 3 + [lview] * 3))


def _head_expand(t):
    e = ((_iota((LD_W, ATT_W), 1) >> HEAD_SHIFT) == _iota((LD_W, ATT_W), 0)).astype(F32)
    return _dot_nn(t, e, HP)


def _dil(d):
    return d if d > 1 else False


def _qkv_views(proj):
    S, w = proj.shape[0], QKV_BLOCKS * ATT_W

    def fn(i, vals, p, o, a):
        for k, d in enumerate(DILATIONS):
            o[k][...] = vals[0].astype(o[k].dtype)

    outs = [(w, BF16, _c0, w) + ((d,) if d > 1 else ()) for d in DILATIONS]
    return _rowwise("qkv_views", fn, S, 512, [(proj, w, _c0, False)], (), outs)


def _attn_merge(os, ls):
    S = os[0].shape[0]

    def fn(i, vals, p, o, a):
        o3, l3 = vals[:3], vals[3:]
        m = jnp.maximum(jnp.maximum(l3[0], l3[1]), l3[2])
        e3 = [jnp.exp(l - m) for l in l3]
        den = e3[0] + e3[1] + e3[2]
        out = jnp.zeros((o3[0].shape[0], ATT_W), F32)
        for ob, e in zip(o3, e3):
            out = out + _head_expand(e / den) * ob
        o[0][...] = out
        o[1][...] = m + jnp.log(den)
        o[2][...] = out.astype(BF16)

    tiles = ([(t, ATT_W, _c0, _dil(d)) for t, d in zip(os, DILATIONS)]
             + [(t, LD_W, _c0, _dil(d)) for t, d in zip(ls, DILATIONS)])
    return _rowwise("attn_merge", fn, S, 512, tiles, (),
                    [(ATT_W, F32, _c0, ATT_W), (LD_W, F32, _c0, LD_W), (ATT_W, BF16, _c0, ATT_W)])


def _attn_bwd_prep(gmixed, att, lse):
    S = att.shape[0]
    n = len(DILATIONS)

    def fn(i, vals, p, o, a):
        g, out, lse_row = vals
        place_d = ((_iota((ATT_W, LD_W), 0) >> HEAD_SHIFT) + 8 == _iota((ATT_W, LD_W), 1)).astype(F32)
        ld = jnp.where(_iota((1, LD_W), 1) < 8, lse_row, 0.0) + _dot_nn(g * out, place_d, HP)
        for k in range(n):
            o[k][...] = g.astype(o[k].dtype)
            o[n + k][...] = ld

    tiles = [(gmixed, ATT_W, _c0, False), (att, ATT_W, _c0, False), (lse, LD_W, _c0, False)]
    outs = ([(ATT_W, BF16, _c0, ATT_W) + ((d,) if d > 1 else ()) for d in DILATIONS]
            + [(LD_W, F32, _c0, LD_W) + ((d,) if d > 1 else ()) for d in DILATIONS])
    res = _rowwise("attn_bwd_prep", fn, S, 512, tiles, (), outs)
    return res[:n], res[n:]


def _sum3_bf16(views, S, width):
    def fn(i, vals, p, o, a):
        o[0][...] = (vals[0].astype(F32) + vals[1].astype(F32) + vals[2].astype(F32)).astype(BF16)

    tiles = [(t, width, _c0, _dil(d)) for t, d in zip(views, DILATIONS)]
    return _rowwise("sum3", fn, S, 512, tiles, (), [(width, BF16, _c0, width)])[0]


def _block_diag_mask():
    return ((_iota((REC_W, REC_W), 0) >> HEAD_SHIFT) == (_iota((REC_W, REC_W), 1) >> HEAD_SHIFT)).astype(F32)


def _hgrn_chunk(qr, z, iv, lb, st, reverse, precise):
    C = REC_CHUNK
    r, c = _iota((C, C), 0), _iota((C, C), 1)
    t_cum = (c >= r) if reverse else (c <= r)
    mid_row, last_row = (C // 2, 0) if reverse else (C // 2 - 1, C - 1)
    f = lb + (1.0 - lb) * jax.nn.sigmoid(z)
    logf = jnp.log(jnp.maximum(f, F_TINY))
    k = (1.0 - lb) * jax.nn.sigmoid(-z)
    q = qr * jax.nn.sigmoid(qr)
    b = _dot_nn(t_cum.astype(F32), logf, HP)
    row = _iota((C, 1), 0)
    bm = jnp.sum(jnp.where(row == mid_row, b, 0.0), axis=0, keepdims=True)
    bl = jnp.sum(jnp.where(row == last_row, b, 0.0), axis=0, keepdims=True)
    qt = q * jnp.exp(jnp.minimum(b - bm, EXP_CLAMP))
    kt = k * jnp.exp(jnp.minimum(bm - b, EXP_CLAMP))
    qh = q * jnp.exp(b)
    kh = k * jnp.exp(bl - b)
    lam = jnp.exp(bl)
    bd = ((_iota((PAIR_W, PAIR_W), 0) >> HEAD_SHIFT) == (_iota((PAIR_W, PAIR_W), 1) >> HEAD_SHIFT)).astype(F32)
    s_in = _iota((C, PAIR_W), 1) & (HEAD - 1)
    t_in = _iota((C, PAIR_W), 0)
    tri = (s_in >= t_in) if reverse else (s_in <= t_in)
    twice = lambda t: jnp.concatenate([t, t], axis=0)
    outs, states = [], []
    for pr in range(N_PAIRS):
        sl = slice(pr * PAIR_W, (pr + 1) * PAIR_W)
        k_bd = twice(kt[:, sl]) * bd
        v_bd = (twice(iv[:, sl]) * bd).astype(BF16)
        st_bd = twice(st[:, sl]) * bd
        if precise:
            scores = _dot_nt(qt[:, sl], k_bd, lax.Precision.HIGH)
        else:
            scores = _dot_nt(qt[:, sl].astype(BF16), k_bd.astype(BF16))
        a = jnp.where(tri, scores, 0.0)
        outs.append(_dot_nn(a.astype(BF16), v_bd) + _dot_nt(qh[:, sl].astype(BF16), st_bd.astype(BF16)))
        kv = _dot_tn(iv[:, sl].astype(BF16), kh[:, sl].astype(BF16))
        st_bd = st_bd * lam[:, sl] + kv * bd
        states.append(st_bd[0:HEAD] + st_bd[HEAD:PAIR_W])
    return jnp.concatenate(outs, axis=1), jnp.concatenate(states, axis=1)


REC_CHUNKS_PER_STEP = 8
REC_ROWS = REC_CHUNKS_PER_STEP * REC_CHUNK


def _hgrn_specs(order, blocks):
    return [pl.BlockSpec((REC_ROWS, REC_W), lambda i, b=b: (order(i), b)) for b in blocks]


def _chunk_rows(j):
    return pl.ds(pl.multiple_of(j * REC_CHUNK, REC_CHUNK), REC_CHUNK)


def _hgrn_fwd(proj, lb, z_blk, reverse):
    S = proj.shape[0]
    nb = S // REC_ROWS
    order = (lambda i: nb - 1 - i) if reverse else (lambda i: i)

    def body(q_ref, z_ref, v_ref, lb_ref, o_ref, st_ref, st_scr):
        @pl.when(pl.program_id(0) == 0)
        def _():
            st_scr[...] = jnp.zeros_like(st_scr)

        def step(t, carry):
            j = REC_CHUNKS_PER_STEP - 1 - t if reverse else t
            rows = _chunk_rows(j)
            st = st_scr[...]
            st_ref[j] = st
            o, st_new = _hgrn_chunk(q_ref[rows, :], z_ref[rows, :], v_ref[rows, :], lb_ref[...], st, reverse, False)
            o_ref[rows, :] = o
            st_scr[...] = st_new
            return carry

        lax.fori_loop(0, REC_CHUNKS_PER_STEP, step, 0, unroll=True)

    return pl.pallas_call(
        body, name="hgrn_rev_fwd" if reverse else "hgrn_fwd_fwd", grid=(nb,),
        in_specs=_hgrn_specs(order, (QR, z_blk, IR)) + [pl.BlockSpec((1, REC_W), lambda i: (0, 0))],
        out_specs=[pl.BlockSpec((REC_ROWS, REC_W), lambda i: (order(i), 0)),
                   pl.BlockSpec((REC_CHUNKS_PER_STEP, HEAD, REC_W), lambda i: (order(i), 0, 0))],
        out_shape=[jax.ShapeDtypeStruct((S, REC_W), F32), jax.ShapeDtypeStruct((S // REC_CHUNK, HEAD, REC_W), F32)],
        scratch_shapes=[pltpu.VMEM((HEAD, REC_W), F32)],
        compiler_params=_cparams(("arbitrary",)),
    )(proj, proj, proj, lb)


def _hgrn_bwd(proj, lb, states, go, z_blk, reverse, other=None):
    S = proj.shape[0]
    nb = S // REC_ROWS
    order = (lambda i: i) if reverse else (lambda i: nb - 1 - i)
    n_other = 0 if other is None else 2

    def body(*refs):
        q_ref, z_ref, v_ref, lb_ref, st_ref, go_ref = refs[:6]
        other_refs = refs[6:6 + n_other]
        gq_ref, gz_ref, gv_ref, glb_ref, gst_scr = refs[6 + n_other:]

        @pl.when(pl.program_id(0) == 0)
        def _():
            gst_scr[...] = jnp.zeros_like(gst_scr)
            glb_ref[...] = jnp.zeros_like(glb_ref)

        chunk = functools.partial(_hgrn_chunk, reverse=reverse, precise=True)

        def step(t, carry):
            j = t if reverse else REC_CHUNKS_PER_STEP - 1 - t
            rows = _chunk_rows(j)
            _, vjp = jax.vjp(chunk, q_ref[rows, :], z_ref[rows, :], v_ref[rows, :], lb_ref[...], st_ref[j])
            gq, gz, gv, glb, gst = vjp((go_ref[rows, :], gst_scr[...]))
            if other_refs:
                gq = gq + other_refs[0][rows, :]
                gv = gv + other_refs[1][rows, :]
            gq_ref[rows, :] = gq.astype(gq_ref.dtype)
            gz_ref[rows, :] = gz.astype(gz_ref.dtype)
            gv_ref[rows, :] = gv.astype(gv_ref.dtype)
            glb_ref[...] += glb
            gst_scr[...] = gst
            return carry

        lax.fori_loop(0, REC_CHUNKS_PER_STEP, step, 0, unroll=4)

    row_spec = pl.BlockSpec((REC_ROWS, REC_W), lambda i: (order(i), 0))
    return pl.pallas_call(
        body, name="hgrn_rev_bwd" if reverse else "hgrn_fwd_bwd", grid=(nb,),
        in_specs=(_hgrn_specs(order, (QR, z_blk, IR)) + [pl.BlockSpec((1, REC_W), lambda i: (0, 0))]
                  + [pl.BlockSpec((REC_CHUNKS_PER_STEP, HEAD, REC_W), lambda i: (order(i), 0, 0)), row_spec]
                  + [row_spec] * n_other),
        out_specs=[row_spec] * 3 + [pl.BlockSpec((1, REC_W), lambda i: (0, 0))],
        out_shape=([jax.ShapeDtypeStruct((S, REC_W), dt) for dt in (BF16 if other else F32, BF16, BF16 if other else F32)]
                   + [jax.ShapeDtypeStruct((1, REC_W), F32)]),
        scratch_shapes=[pltpu.VMEM((HEAD, REC_W), F32)],
        compiler_params=_cparams(("arbitrary",)),
    )(proj, proj, proj, lb, states, go, *(other or ()))


def _hgrn_post_f(of, ob, gr, rnw):
    o = of + ob
    ms = _dot_nn(o * o, _block_diag_mask() * (1.0 / HEAD), HP)
    return o * lax.rsqrt(ms + EPS) * rnw * (gr * jax.nn.sigmoid(gr))


def _hgrn_post_fwd(of, ob, proj, rnw):
    S = of.shape[0]

    def fn(i, vals, p, o, a):
        o[0][...] = _hgrn_post_f(vals[0], vals[1], vals[2], p[0][...]).astype(BF16)

    tiles = [(of, REC_W, _c0, False), (ob, REC_W, _c0, False), (proj, REC_W, lambda j: GR, False)]
    return _rowwise("hgrn_post_fwd", fn, S, 512, tiles, _row_params(rnw), [(REC_W, BF16, _c0, REC_W)])[0]


def _hgrn_post_bwd(of, ob, proj, gmixed, rnw):
    S = of.shape[0]

    def fn(i, vals, p, o, a):
        _, vjp = jax.vjp(_hgrn_post_f, vals[0], vals[1], vals[2], p[0][...])
        go, _, ggr, grnw = vjp(vals[3])
        o[0][...] = go
        o[1][...] = ggr.astype(BF16)
        a[0][...] += grnw

    tiles = [(of, REC_W, _c0, False), (ob, REC_W, _c0, False), (proj, REC_W, lambda j: GR, False),
             (gmixed, REC_W, lambda j: 1, False)]
    return _rowwise("hgrn_post_bwd", fn, S, 512, tiles, _row_params(rnw),
                    [(REC_W, F32, _c0, REC_W), (REC_W, BF16, _c0, REC_W)], [(1, REC_W, _c0, REC_W)])


def _lower_bounds_f(g0, g1):
    m = jnp.maximum(g0, g1)
    e0, e1 = jnp.exp(g0 - m), jnp.exp(g1 - m)
    return e1 / (e0 + e1)


def _adamw(name, w, m, v, gparts):
    R, C = w.shape
    P = gparts.shape[0]
    tr = R if R * C * 4 * (P + 7) * 2 <= VMEM_LIMIT_BYTES // 2 else _pick(R, (256, 128, 64, 32, 16, 8))

    def body(w_ref, m_ref, v_ref, gp_ref, g_ref, d_ref, nm_ref, nv_ref):
        g = gp_ref[0].astype(F32)
        for p in range(1, P):
            g = g + gp_ref[p].astype(F32)
        w_ = w_ref[...]
        nm = ADAM_B1 * m_ref[...] + (1.0 - ADAM_B1) * g
        nv = ADAM_B2 * v_ref[...] + (1.0 - ADAM_B2) * jnp.square(g)
        m_hat = nm / (1.0 - ADAM_B1 ** ADAM_STEP)
        v_hat = nv / (1.0 - ADAM_B2 ** ADAM_STEP)
        g_ref[...] = g
        d_ref[...] = -ADAM_LR * (m_hat / (jnp.sqrt(v_hat) + ADAM_EPS) + ADAM_WD * w_)
        nm_ref[...] = nm
        nv_ref[...] = nv

    spec = pl.BlockSpec((tr, C), lambda i: (i, 0))
    return pl.pallas_call(
        body, name=name, grid=(R // tr,),
        in_specs=[spec, spec, spec, pl.BlockSpec((P, tr, C), lambda i: (0, i, 0))],
        out_specs=[spec] * 4, out_shape=[jax.ShapeDtypeStruct((R, C), F32)] * 4,
        compiler_params=_cparams(("parallel",)),
    )(w, m, v, gparts)


def _place():
    return lax.axis_index("x"), lax.axis_index("y"), lax.axis_index("c")


def _index_of(p):
    return 4 * p[0] + 2 * p[1] + p[2]


def _allgather_small(name, rows):
    m_per, n = rows.shape

    def body(x_ref, out_ref, send_sems, recv_sems, local_sem):
        x, y, c = _place()
        me, sibling = (x, y, c), (x, y, 1 - c)
        chips = [(1 - x, y), (x, 1 - y), (1 - x, 1 - y)]

        def blk(p):
            return out_ref.at[pl.ds(_index_of(p) * m_per, m_per), :]

        def copy(k, block, to, src=None):
            return pltpu.make_async_remote_copy(
                src_ref=blk(block) if src is None else src, dst_ref=blk(block),
                send_sem=send_sems.at[k], recv_sem=recv_sems.at[k], device_id=to, device_id_type=MESH)

        mine = pltpu.make_async_copy(x_ref, blk(me), local_sem)
        mine.start()
        first = [copy(0, me, sibling, src=x_ref)]
        first += [copy(1 + j, me, (*chip, c), src=x_ref) for j, chip in enumerate(chips)]
        for cp in first:
            cp.start()
        passed = [copy(4 + j, (*chip, c), sibling) for j, chip in enumerate(chips)]
        for j, chip in enumerate(chips):
            copy(1 + j, (*chip, c), me).wait_recv()
            passed[j].start()
        copy(0, sibling, me).wait_recv()
        for j, chip in enumerate(chips):
            copy(4 + j, (*chip, 1 - c), me).wait_recv()
        for cp in first + passed:
            cp.wait_send()
        mine.wait()

    return pl.pallas_call(
        body, name=name,
        out_shape=jax.ShapeDtypeStruct((N_DEV * m_per, n), rows.dtype),
        in_specs=[pl.BlockSpec(memory_space=pltpu.VMEM)],
        out_specs=pl.BlockSpec(memory_space=pltpu.VMEM),
        scratch_shapes=[pltpu.SemaphoreType.DMA((7,)), pltpu.SemaphoreType.DMA((7,)), pltpu.SemaphoreType.DMA],
        compiler_params=_cparams(),
    )(rows)


def _allgather_big(name, arrs):
    na = len(arrs)

    def body(*refs):
        ins, outs = refs[:na], refs[na:2 * na]
        send_sems, recv_sems, local_sems = refs[2 * na:]
        x, y, c = _place()
        me, sibling = (x, y, c), (x, y, 1 - c)
        chips = [(1 - x, y), (x, 1 - y), (1 - x, 1 - y)]

        def copy(a, k, block, to, src=None):
            dst = outs[a].at[_index_of(block)]
            return pltpu.make_async_remote_copy(
                src_ref=dst if src is None else src, dst_ref=dst,
                send_sem=send_sems.at[a, k], recv_sem=recv_sems.at[a, k], device_id=to, device_id_type=MESH)

        mine = [pltpu.make_async_copy(ins[a], outs[a].at[_index_of(me)], local_sems.at[a]) for a in range(na)]
        for cp in mine:
            cp.start()
        sent = []
        for a in range(na):
            sent.append(copy(a, 0, me, sibling, src=ins[a]))
            sent += [copy(a, 1 + j, me, (*chip, c), src=ins[a]) for j, chip in enumerate(chips)]
        for cp in sent:
            cp.start()
        for j, chip in enumerate(chips):
            for a in range(na):
                copy(a, 1 + j, (*chip, c), me).wait_recv()
                fwd = copy(a, 4 + j, (*chip, c), sibling)
                fwd.start()
                sent.append(fwd)
        for a in range(na):
            copy(a, 0, sibling, me).wait_recv()
            for j, chip in enumerate(chips):
                copy(a, 4 + j, (*chip, 1 - c), me).wait_recv()
        for cp in sent:
            cp.wait_send()
        for cp in mine:
            cp.wait()

    any_spec = pl.BlockSpec(memory_space=pl.ANY)
    return pl.pallas_call(
        body, name=name,
        out_shape=[jax.ShapeDtypeStruct((N_DEV,) + a.shape, a.dtype) for a in arrs],
        in_specs=[any_spec] * na, out_specs=[any_spec] * na,
        scratch_shapes=[pltpu.SemaphoreType.DMA((na, 7)), pltpu.SemaphoreType.DMA((na, 7)), pltpu.SemaphoreType.DMA((na,))],
        compiler_params=_cparams(),
    )(*arrs)


N_CHIPS = 4


def _scatter_to_sibling(name, parts):
    na = len(parts)

    def body(*refs):
        ins, outs = refs[:na], refs[na:2 * na]
        send_sems, recv_sems = refs[2 * na:]
        x, y, c = _place()
        sibling = (x, y, 1 - c)
        sent = []
        for a in range(na):
            for q in range(N_CHIPS):
                sent.append(pltpu.make_async_remote_copy(
                    src_ref=ins[a].at[2 * q + (1 - c)], dst_ref=outs[a].at[q],
                    send_sem=send_sems.at[a, q], recv_sem=recv_sems.at[a, q], device_id=sibling, device_id_type=MESH))
        for cp in sent:
            cp.start()
        for cp in sent:
            cp.wait_recv()
        for cp in sent:
            cp.wait_send()

    any_spec = pl.BlockSpec(memory_space=pl.ANY)
    return pl.pallas_call(
        body, name=name,
        out_shape=[jax.ShapeDtypeStruct((N_CHIPS,) + p.shape[1:], p.dtype) for p in parts],
        in_specs=[any_spec] * na, out_specs=[any_spec] * na,
        scratch_shapes=[pltpu.SemaphoreType.DMA((na, N_CHIPS)), pltpu.SemaphoreType.DMA((na, N_CHIPS))],
        compiler_params=_cparams(),
    )(*parts)


def _pair_sum(name, parts, recv):
    _, R, C = parts.shape
    tr = _pick(R, (256, 128, 64, 32, 16))

    def body(p_ref, r_ref, o_ref):
        c = lax.axis_index("c")
        o_ref[...] = (p_ref[c].astype(F32) + r_ref[...].astype(F32)).astype(BF16)

    return pl.pallas_call(
        body, name=name, grid=(N_CHIPS, R // tr),
        in_specs=[pl.BlockSpec((None, 2, tr, C), lambda q, i: (q, 0, i, 0)), pl.BlockSpec((None, tr, C), lambda q, i: (q, i, 0))],
        out_specs=pl.BlockSpec((None, tr, C), lambda q, i: (q, i, 0)),
        out_shape=jax.ShapeDtypeStruct((N_CHIPS, R, C), BF16),
        compiler_params=_cparams(("parallel", "parallel")),
    )(parts.reshape(N_CHIPS, 2, R, C), recv)


def _scatter_to_chips(name, sums):
    na = len(sums)

    def body(*refs):
        ins, outs = refs[:na], refs[na:2 * na]
        send_sems, recv_sems, local_sems = refs[2 * na:]
        x, y, c = _place()
        me = 2 * x + y
        chips = [(1 - x, y), (x, 1 - y), (1 - x, 1 - y)]
        mine = [pltpu.make_async_copy(ins[a].at[me], outs[a].at[me], local_sems.at[a]) for a in range(na)]
        for cp in mine:
            cp.start()
        sent = []
        for a in range(na):
            for k, (qx, qy) in enumerate(chips):
                sent.append(pltpu.make_async_remote_copy(
                    src_ref=ins[a].at[2 * qx + qy], dst_ref=outs[a].at[me],
                    send_sem=send_sems.at[a, k], recv_sem=recv_sems.at[a, k], device_id=(qx, qy, c), device_id_type=MESH))
        for cp in sent:
            cp.start()
        for a in range(na):
            for k, (qx, qy) in enumerate(chips):
                slot = outs[a].at[2 * qx + qy]
                pltpu.make_async_remote_copy(
                    src_ref=slot, dst_ref=slot, send_sem=send_sems.at[a, k], recv_sem=recv_sems.at[a, k],
                    device_id=(qx, qy, c), device_id_type=MESH).wait_recv()
        for cp in sent:
            cp.wait_send()
        for cp in mine:
            cp.wait()

    any_spec = pl.BlockSpec(memory_space=pl.ANY)
    return pl.pallas_call(
        body, name=name,
        out_shape=[jax.ShapeDtypeStruct(p.shape, p.dtype) for p in sums],
        in_specs=[any_spec] * na, out_specs=[any_spec] * na,
        scratch_shapes=[pltpu.SemaphoreType.DMA((na, 3)), pltpu.SemaphoreType.DMA((na, 3)), pltpu.SemaphoreType.DMA((na,))],
        compiler_params=_cparams(),
    )(*sums)


def _gather_row(name, vec, width):
    n = vec.shape[0]
    rows = jnp.pad(vec, (0, width - n)).reshape(SUBLANES_F32, width // SUBLANES_F32)
    return _allgather_small(name, rows).reshape(N_DEV, width)[:, :n]


def _layer_fwd(x, mod, w):
    sh1, sc1, g1, sh2, sc2, g2 = [mod[i:i + 1] for i in range(N_MOD)]
    S = x.shape[0]
    h1 = _normmod_fwd(x, w["norm1_w"], sc1, sh1)
    proj = _matmul("proj_in", h1, w["w_in"], "nn")
    a1, a_out = _conv_a_fwd(proj, w["conv_a_w"], w["conv_a_b"], w["ln_a_w"], w["ln_a_b"])
    qkv = _qkv_views(proj)
    os, ls = zip(*[_attn_fwd(v, dil) for v, dil in zip(qkv, DILATIONS)])
    att, lse, att_b = _attn_merge(os, ls)
    of, st_f = _hgrn_fwd(proj, w["lb_f"], ZF, False)
    ob, st_b = _hgrn_fwd(proj, w["lb_b"], ZB, True)
    rec = _hgrn_post_fwd(of, ob, proj, w["rec_norm_w"])
    mixed = jnp.concatenate([att_b, rec, a_out], axis=1)
    y1 = _matmul("proj_out", mixed, w["w_out"], "nn")
    x2, h2 = _gate_add_normmod(x, y1, g1, w["norm2_w"], sc2, sh2)
    u = _matmul("ffn_up", h2, w["w_up"], "nn")
    act = _ffn_mid_fwd(u, w["conv_f_w"])
    y2 = _matmul("ffn_down", act, w["w_down"], "nn")
    x3 = _gate_add(x2, y2, g2)
    saved = dict(x=x, h1=h1, proj=proj, a1=a1, qkv=qkv, att=att, lse=lse, of=of, ob=ob, st_f=st_f, st_b=st_b,
                 mixed=mixed, y1=y1, x2=x2, h2=h2, u=u, act=act, y2=y2)
    return x3, saved


def _layer_bwd(gx3, mod, w, s):
    sh1, sc1, g1, sh2, sc2, g2 = [mod[i:i + 1] for i in range(N_MOD)]
    S = gx3.shape[0]
    g = {}
    gy2, gg2 = _gate_bwd(gx3, s["y2"], g2)
    gact = _matmul("ffn_down_dx", gy2, w["w_down"], "nt")
    g["w_down"] = _matmul("ffn_down_dw", s["act"], gy2, "tn")
    gu, g["conv_f_w"] = _ffn_mid_bwd(s["u"], gact, w["conv_f_w"])
    gh2 = _matmul_nt_pieces("ffn_up_dx", gu, (w["w_up"][:, :D_FF], w["w_up"][:, D_FF:]))
    g["w_up"] = jnp.concatenate([_matmul("ffn_up_dw", s["h2"], t, "tn") for t in gu], axis=1)
    gx2, gy1, g["norm2_w"], gsc2, gsh2, gg1 = _normmod_gate_bwd(s["x2"], gh2, gx3, w["norm2_w"], sc2, sh2, s["y1"], g1)
    gmixed = _matmul("proj_out_dx", gy1, w["w_out"], "nt")
    g["w_out"] = _matmul("proj_out_dw", s["mixed"], gy1, "tn")
    go, ggr, g["rec_norm_w"] = _hgrn_post_bwd(s["of"], s["ob"], s["proj"], gmixed, w["rec_norm_w"])
    gq_f, gz_f, gv_f, g["lb_f"] = _hgrn_bwd(s["proj"], w["lb_f"], s["st_f"], go, ZF, False)
    gq_r, gz_b, gv_r, g["lb_b"] = _hgrn_bwd(s["proj"], w["lb_b"], s["st_b"], go, ZB, True, other=(gq_f, gv_f))
    dos, lds = _attn_bwd_prep(gmixed, s["att"], s["lse"])
    gqkv = zip(*[_attn_bwd(v, do, ld, dil) for v, do, ld, dil in zip(s["qkv"], dos, lds, DILATIONS)])
    gq_a, gk_a, gv_a = [_sum3_bf16(lst, S, ATT_W) for lst in gqkv]
    gav, gag, gcw, g["conv_a_b"], g["ln_a_w"], g["ln_a_b"] = _conv_a_bwd(
        s["proj"], s["a1"], gmixed, w["conv_a_w"], w["ln_a_w"], w["ln_a_b"])
    g["conv_a_w"] = gcw[:CONV_W]
    gproj = jnp.concatenate([gq_a, gk_a, gv_a, gq_r, gz_f, gz_b, gv_r, ggr, gav, gag,
                             jnp.zeros((S, IN_COLS_PAD - IN_COLS), BF16)], axis=1)
    gh1 = _matmul("proj_in_dx", gproj, w["w_in"], "nt")
    g["w_in"] = _matmul("proj_in_dw", s["h1"], gproj, "tn")
    gx, g["norm1_w"], gsc1, gsh1 = _normmod_bwd(s["x"], gh1, gx2, w["norm1_w"], sc1, sh1)
    gmod = jnp.concatenate([gsh1, gsc1, gg1, gsh2, gsc2, gg2], axis=0)
    return gx, gmod, g


def _permute_in_cols(t):
    pad = jnp.zeros(t.shape[:-1] + (IN_COLS_PAD - IN_COLS,), t.dtype)
    return jnp.concatenate([t[..., CONV_COLS:], t[..., :CONV_COLS], pad], axis=-1)


def _unpermute_in_cols(t):
    return jnp.concatenate([t[..., IN_COLS - CONV_COLS:IN_COLS], t[..., :IN_COLS - CONV_COLS]], axis=-1)


def _cols_from_gathered(t, lead):
    nd = t.ndim
    perm = tuple(range(1, nd - 1)) + (0, nd - 1)
    t = t.transpose(perm)
    return t.reshape(t.shape[:-2] + (t.shape[-2] * t.shape[-1],))


def _cols_to_parts(t):
    L, R, C = t.shape
    return t.reshape(L * R, N_DEV, C // N_DEV).transpose(1, 0, 2)


SMALL_REPL = (("norm1_w", D), ("conv_a_b", CONV_CH), ("ln_a_w", CONV_CH), ("ln_a_b", CONV_CH),
              ("rec_norm_w", REC_W), ("norm2_w", D))


def kernel(x, c, w_ada, b_ada, norm1_w, w_in, conv_a_w, conv_a_b, ln_a_w, ln_a_b, lb_gamma, rec_norm_w, w_out, norm2_w, w_up, conv_f_w, w_down, final_norm_w, loss_target, m_w_ada, m_b_ada, m_norm1_w, m_w_in, m_conv_a_w, m_conv_a_b, m_ln_a_w, m_ln_a_b, m_lb_gamma, m_rec_norm_w, m_w_out, m_norm2_w, m_w_up, m_conv_f_w, m_w_down, m_final_norm_w, v_w_ada, v_b_ada, v_norm1_w, v_w_in, v_conv_a_w, v_conv_a_b, v_ln_a_w, v_ln_a_b, v_lb_gamma, v_rec_norm_w, v_w_out, v_norm2_w, v_w_up, v_conv_f_w, v_w_down, v_final_norm_w):
    px, py, pc = _place()
    me = _index_of((px, py, pc))
    xs, tgt = x[0], loss_target[0]
    S = xs.shape[0]
    ada_cols = w_ada.shape[2]

    big = [w_in.reshape(DEPTH * D, -1), w_up.reshape(DEPTH * D, -1), w_out.reshape(-1, D), w_down.reshape(-1, D)]
    g_in, g_up, g_out, g_down = _allgather_big("gather_weights", [t.astype(BF16) for t in big])
    w_in_f = _permute_in_cols(_cols_from_gathered(g_in.reshape(N_DEV, DEPTH, D, -1), 1))
    w_up_f = _cols_from_gathered(g_up.reshape(N_DEV, DEPTH, D, -1), 1)
    w_out_f = g_out.reshape(N_DEV, DEPTH, D // N_DEV, D).transpose(1, 0, 2, 3).reshape(DEPTH, D, D)
    w_out_f = jnp.concatenate([w_out_f[:, CONV_CH:], w_out_f[:, :CONV_CH]], axis=1)
    w_down_f = g_down.reshape(N_DEV, DEPTH, D_FF // N_DEV, D).transpose(1, 0, 2, 3).reshape(DEPTH, D_FF, D)

    small_in = jnp.concatenate([c.reshape(-1), conv_a_w.reshape(-1), lb_gamma.reshape(-1), conv_f_w.reshape(-1)])
    gs = _gather_row("gather_small", small_in, 8192)
    o1 = D
    o2 = o1 + conv_a_w.size
    o3 = o2 + lb_gamma.size
    c_all = gs[:, :o1]
    conv_a_f = _cols_from_gathered(gs[:, o1:o2].reshape(N_DEV, DEPTH, CONV_W, -1), 1)
    lb_gamma_f = _cols_from_gathered(gs[:, o2:o3].reshape(N_DEV, DEPTH, 2, -1), 1)
    conv_f_f = _cols_from_gathered(gs[:, o3:].reshape(N_DEV, DEPTH, FFN_CONV_W, -1), 1)
    conv_a_pad = jnp.pad(conv_a_f, ((0, 0), (0, CONV_W_PAD - CONV_W), (0, 0)))

    b_loc = lax.dynamic_slice_in_dim(b_ada, me * ada_cols, ada_cols, axis=1)

    def mod_fn(c_all_, w_, b_):
        cond = c_all_ * jax.nn.sigmoid(c_all_)
        return (jnp.concatenate([_dot_nn(cond, w_[l], HP) + b_[l] for l in range(DEPTH)], axis=1),)

    (mod_part,) = _vmem_call("ada_mod", mod_fn, [c_all, w_ada, b_loc[:, None, :]], [((N_DEV, DEPTH * ada_cols), F32)])
    gm = _allgather_small("gather_mod", mod_part).reshape(N_DEV, N_DEV, DEPTH, ada_cols)
    mod = lax.dynamic_index_in_dim(gm, me, axis=1, keepdims=False)
    mod = mod.transpose(1, 0, 2).reshape(DEPTH, N_MOD, D)

    (lb1,) = _vmem_call("lower_bounds", lambda a, b: (_lower_bounds_f(a, b),), [lb_gamma_f[0], lb_gamma_f[1]], [((2, REC_W), F32)])
    lb4 = jnp.concatenate([jnp.zeros_like(lb1), lb1], axis=0)

    def layer_weights(l):
        row = lambda t: t[l].reshape(1, -1)
        return dict(norm1_w=row(norm1_w), w_in=w_in_f[l], conv_a_w=conv_a_pad[l], conv_a_b=row(conv_a_b), ln_a_w=row(ln_a_w),
                    ln_a_b=row(ln_a_b), lb_f=lb4[2 * l:2 * l + 1], lb_b=lb4[2 * l + 1:2 * l + 2], rec_norm_w=row(rec_norm_w),
                    w_out=w_out_f[l], norm2_w=row(norm2_w), w_up=w_up_f[l], conv_f_w=conv_f_f[l], w_down=w_down_f[l])

    ws = [layer_weights(l) for l in range(DEPTH)]
    h, saved = xs, []
    for l in range(DEPTH):
        h, s = _layer_fwd(h, mod[l], ws[l])
        saved.append(s)
    gh, g_final, loss_row = _loss_head(h, tgt, final_norm_w.reshape(1, D))
    loss = lax.psum(loss_row[0, 0], ("x", "y", "c"))
    gmods, gws = [None] * DEPTH, [None] * DEPTH
    for l in reversed(range(DEPTH)):
        gh, gmods[l], gws[l] = _layer_bwd(gh, mod[l], ws[l], saved[l])
    grad_x = gh[None]

    glb1 = jnp.concatenate([gws[1]["lb_f"], gws[1]["lb_b"]], axis=0)

    def lb_bwd_fn(a, b, g1):
        _, vjp = jax.vjp(_lower_bounds_f, a, b)
        return vjp(g1)

    g_lb_gamma = jnp.stack(_vmem_call("lower_bounds_bwd", lb_bwd_fn, [lb_gamma_f[0], lb_gamma_f[1], glb1], [((2, REC_W), F32)] * 2))
    pieces = [jnp.stack(gmods).reshape(-1)]
    for l in range(DEPTH):
        pieces += [gws[l][n].reshape(-1) for n, _ in SMALL_REPL]
    pieces += [g_final.reshape(-1)]
    pieces += [jnp.stack([gws[l]["conv_a_w"] for l in range(DEPTH)]).reshape(-1), g_lb_gamma.reshape(-1),
               jnp.stack([gws[l]["conv_f_w"] for l in range(DEPTH)]).reshape(-1)]
    small_g = jnp.concatenate(pieces)
    n_small = small_g.shape[0]
    gsm = _gather_row("gather_small_grads", small_g, 71680)
    n_mod = DEPTH * N_MOD * D
    gmod_all = gsm[:, :n_mod].reshape(N_DEV, DEPTH, N_MOD * D)
    gmod_loc = lax.dynamic_slice_in_dim(gmod_all, me * ada_cols, ada_cols, axis=2).transpose(1, 0, 2)

    def small_fn(gsm_, c_all_, gm_):
        cond = c_all_ * jax.nn.sigmoid(c_all_)
        gw = jnp.concatenate([_dot_tn(cond, gm_[l], HP) for l in range(DEPTH)], axis=0)
        return jnp.sum(gsm_, axis=0, keepdims=True), gw

    tot, g_w_ada = _vmem_call("small_grads", small_fn, [gsm, c_all, gmod_loc],
                              [((1, n_small), F32), ((DEPTH * D, ada_cols), F32)])
    tot = tot[0]
    grads = {"w_ada": g_w_ada.reshape(DEPTH, D, ada_cols), "b_ada": tot[:n_mod].reshape(DEPTH, N_MOD * D)}
    pos = n_mod
    per_layer = {n: [] for n, _ in SMALL_REPL}
    for l in range(DEPTH):
        for n, width in SMALL_REPL:
            per_layer[n].append(tot[pos:pos + width])
            pos += width
    for n, _ in SMALL_REPL:
        grads[n] = jnp.stack(per_layer[n])
    grads["final_norm_w"] = tot[pos:pos + D]
    pos += D
    n_ca, n_lb, n_cf = DEPTH * CONV_W * CONV_CH, DEPTH * 2 * REC_W, DEPTH * FFN_CONV_W * 2 * D_FF
    g_ca = tot[pos:pos + n_ca].reshape(DEPTH, CONV_W, CONV_CH)
    g_lb = tot[pos + n_ca:pos + n_ca + n_lb].reshape(DEPTH, 2, REC_W)
    g_cf = tot[pos + n_ca + n_lb:pos + n_ca + n_lb + n_cf].reshape(DEPTH, FFN_CONV_W, 2 * D_FF)
    grads["conv_a_w"] = lax.dynamic_slice_in_dim(g_ca, me * conv_a_w.shape[2], conv_a_w.shape[2], axis=2)
    grads["lb_gamma"] = lax.dynamic_slice_in_dim(g_lb, me * lb_gamma.shape[2], lb_gamma.shape[2], axis=2)
    grads["conv_f_w"] = lax.dynamic_slice_in_dim(g_cf, me * conv_f_w.shape[2], conv_f_w.shape[2], axis=2)

    gw_in = _unpermute_in_cols(jnp.stack([gws[l]["w_in"] for l in range(DEPTH)]))
    gw_up = jnp.stack([gws[l]["w_up"] for l in range(DEPTH)])
    gw_out = jnp.stack([gws[l]["w_out"] for l in range(DEPTH)])
    gw_out = jnp.concatenate([gw_out[:, D - CONV_CH:], gw_out[:, :D - CONV_CH]], axis=1)
    gw_down = jnp.stack([gws[l]["w_down"] for l in range(DEPTH)])
    rows_to_parts = lambda t: t.reshape(DEPTH, N_DEV, -1, D).transpose(1, 0, 2, 3).reshape(N_DEV, -1, D)
    parts = [_cols_to_parts(gw_in), _cols_to_parts(gw_up), rows_to_parts(gw_out), rows_to_parts(gw_down)]
    parts = [t.astype(BF16) for t in parts]
    from_sibling = _scatter_to_sibling("scatter_sibling", parts)
    sums = [_pair_sum("pair_sum", p, r) for p, r in zip(parts, from_sibling)]
    r_in, r_up, r_out, r_down = _scatter_to_chips("scatter_chips", sums)

    given = dict(w_ada=(w_ada, m_w_ada, v_w_ada), b_ada=(b_ada, m_b_ada, v_b_ada), norm1_w=(norm1_w, m_norm1_w, v_norm1_w),
                 w_in=(w_in, m_w_in, v_w_in), conv_a_w=(conv_a_w, m_conv_a_w, v_conv_a_w), conv_a_b=(conv_a_b, m_conv_a_b, v_conv_a_b),
                 ln_a_w=(ln_a_w, m_ln_a_w, v_ln_a_w), ln_a_b=(ln_a_b, m_ln_a_b, v_ln_a_b), lb_gamma=(lb_gamma, m_lb_gamma, v_lb_gamma),
                 rec_norm_w=(rec_norm_w, m_rec_norm_w, v_rec_norm_w), w_out=(w_out, m_w_out, v_w_out),
                 norm2_w=(norm2_w, m_norm2_w, v_norm2_w), w_up=(w_up, m_w_up, v_w_up), conv_f_w=(conv_f_w, m_conv_f_w, v_conv_f_w),
                 w_down=(w_down, m_w_down, v_w_down), final_norm_w=(final_norm_w, m_final_norm_w, v_final_norm_w))
    big_parts = dict(w_in=r_in, w_up=r_up, w_out=r_out, w_down=r_down)
    names = list(given)
    res = {}
    for n in names:
        w_, m_, v_ = given[n]
        shape = w_.shape
        C = shape[-1]
        two_d = lambda t: t.reshape(-1, C)
        gp = big_parts[n] if n in big_parts else two_d(grads[n])[None]
        res[n] = [t.reshape(shape) for t in _adamw("adamw_" + n, two_d(w_), two_d(m_), two_d(v_), gp)]
    return (loss, grad_x, *[res[n][0] for n in names], *[res[n][1] for n in names],
            *[res[n][2] for n in names], *[res[n][3] for n in names])
```

```python
import functools

import jax
import jax.numpy as jnp
from jax import lax
from jax.experimental import pallas as pl
from jax.experimental.pallas import tpu as pltpu

F32 = jnp.float32
BF16 = jnp.bfloat16
HP = lax.Precision.HIGHEST
MESH = pl.DeviceIdType.MESH

N_DEV = 8
D = 1024
DEPTH = 2
CONV_CH = 256
CONV_W = 31
CONV_W_PAD = 32
ATT_W = 384
REC_W = 384
N_HEADS = 6
HEAD = 64
HEAD_SHIFT = 6
HALF_BAND = 64
ATT_BLK = 256
ATT_SUB = 128
DILATIONS = (1, 4, 16)
ALIBI_SLOPES = tuple(float(2.0 ** (-8.0 * (h + 1) / N_HEADS)) for h in range(N_HEADS))
MASK_VALUE = -1e30
REC_CHUNK = 64
EXP_CLAMP = 80.0
F_TINY = 1e-30
IN_COLS = 3584
IN_COLS_PAD = IN_COLS
QKV_BLOCKS = 3
D_FF = 2816
FFN_CONV_W = 3
N_MOD = 6
EPS = 1e-6
ADAM_LR, ADAM_B1, ADAM_B2, ADAM_EPS, ADAM_WD, ADAM_STEP = 0.001, 0.9, 0.999, 1e-08, 0.01, 10

VMEM_LIMIT_BYTES = 56 * 1024 * 1024
SUBLANES_F32 = 8
LANES = 128

QA, KA, VA, QR, ZF, ZB, IR, GR = range(8)
AV_BLK, AG_BLK = 12, 13
CONV_COLS = 2 * CONV_CH


def _cparams(sem=None):
    kw = dict(vmem_limit_bytes=VMEM_LIMIT_BYTES)
    if sem is not None:
        kw["dimension_semantics"] = sem
    return pltpu.CompilerParams(**kw)


def _iota(shape, dim):
    return lax.broadcasted_iota(jnp.int32, shape, dim)


def _dot(a, b, dims, precision=None):
    return lax.dot_general(a, b, (dims, ((), ())), precision=precision, preferred_element_type=F32)


def _dot_nn(a, b, precision=None):
    return _dot(a, b, ((1,), (0,)), precision)


def _dot_nt(a, b, precision=None):
    return _dot(a, b, ((1,), (1,)), precision)


def _dot_tn(a, b, precision=None):
    return _dot(a, b, ((0,), (0,)), precision)


def _c0(j):
    return 0


def _pick(n, cands):
    for c in cands:
        if n % c == 0:
            return c
    return n


MATMUL_OUT_TILE_BYTES = 8 * 1024 * 1024


def _div_lanes(n, cap):
    best = None
    for d in range(LANES, min(n, cap) + 1, LANES):
        if n % d == 0:
            best = d
    return best if best is not None else n


def _matmul_tiles(mode, M, N, K):
    if mode == "nn":
        tm = _pick(M, (1024, 512, 256, 128))
        return tm, _div_lanes(N, MATMUL_OUT_TILE_BYTES // (4 * tm)), K
    if mode == "nt":
        return _pick(M, (512, 256, 128)), N, K
    tm = _div_lanes(M, 1408)
    return tm, _div_lanes(N, MATMUL_OUT_TILE_BYTES // (4 * tm)), _pick(K, (1024, 512, 256))


def _matmul(name, a, b, mode, out_dtype=F32):
    if mode == "nn":
        (M, K), (_, N) = a.shape, b.shape
    elif mode == "nt":
        (M, K), (N, _) = a.shape, b.shape
    else:
        (K, M), (_, N) = a.shape, b.shape
    tm, tn, tk = _matmul_tiles(mode, M, N, K)
    nk = K // tk
    if mode == "nn":
        a_spec = pl.BlockSpec((tm, tk), lambda i, j, k: (i, k))
        b_spec = pl.BlockSpec((tk, tn), lambda i, j, k: (k, j))
        dims = ((1,), (0,))
    elif mode == "nt":
        a_spec = pl.BlockSpec((tm, tk), lambda i, j, k: (i, k))
        b_spec = pl.BlockSpec((tn, tk), lambda i, j, k: (j, k))
        dims = ((1,), (1,))
    else:
        a_spec = pl.BlockSpec((tk, tm), lambda i, j, k: (k, i))
        b_spec = pl.BlockSpec((tk, tn), lambda i, j, k: (k, j))
        dims = ((0,), (0,))

    def body_whole(a_ref, b_ref, o_ref):
        o_ref[...] = _dot(a_ref[...].astype(BF16), b_ref[...].astype(BF16), dims).astype(o_ref.dtype)

    def body(a_ref, b_ref, o_ref, acc_ref):
        k = pl.program_id(2)
        part = _dot(a_ref[...].astype(BF16), b_ref[...].astype(BF16), dims)

        @pl.when(k == 0)
        def _():
            acc_ref[...] = part

        @pl.when(k > 0)
        def _():
            acc_ref[...] += part

        @pl.when(k == nk - 1)
        def _():
            o_ref[...] = acc_ref[...].astype(o_ref.dtype)

    return pl.pallas_call(
        body_whole if nk == 1 else body, name=name, grid=(M // tm, N // tn, nk),
        in_specs=[a_spec, b_spec],
        out_specs=pl.BlockSpec((tm, tn), lambda i, j, k: (i, j)),
        out_shape=jax.ShapeDtypeStruct((M, N), out_dtype),
        scratch_shapes=[] if nk == 1 else [pltpu.VMEM((tm, tn), F32)],
        compiler_params=_cparams(("parallel", "parallel", "arbitrary")),
    )(a, b)


def _matmul_nt_pieces(name, a_pieces, b_pieces):
    M, N = a_pieces[0].shape[0], b_pieces[0].shape[0]
    tm = _pick(M, (512, 256, 128))
    n = len(a_pieces)

    def body(*refs):
        acc = _dot(refs[0][...].astype(BF16), refs[n][...].astype(BF16), ((1,), (1,)))
        for p in range(1, n):
            acc = acc + _dot(refs[p][...].astype(BF16), refs[n + p][...].astype(BF16), ((1,), (1,)))
        refs[2 * n][...] = acc

    in_specs = ([pl.BlockSpec((tm, a.shape[1]), lambda i: (i, 0)) for a in a_pieces]
                + [pl.BlockSpec(b.shape, lambda i: (0, 0)) for b in b_pieces])
    return pl.pallas_call(
        body, name=name, grid=(M // tm,), in_specs=in_specs,
        out_specs=pl.BlockSpec((tm, N), lambda i: (i, 0)), out_shape=jax.ShapeDtypeStruct((M, N), F32),
        compiler_params=_cparams(("parallel",)),
    )(*a_pieces, *b_pieces)


def _rowwise(name, fn, S, ts, tiles, params=(), outs=(), accs=(), halo=0, ncb=1):
    in_specs, args, scratch = [], [], []
    for arr, w, jm, with_halo in tiles:
        if isinstance(with_halo, int) and with_halo > 1:
            d = with_halo
            in_specs.append(pl.BlockSpec((ts // d, d * w), lambda j, i: (i, 0)))
            args.append(arr)
            scratch.append(pltpu.VMEM((w // LANES, ts, LANES), F32))
        elif with_halo:
            hb, nhb = ts // halo, S // halo
            in_specs += [
                pl.BlockSpec((halo, w), lambda j, i, jm=jm, hb=hb: (jnp.maximum(i * hb - 1, 0), jm(j))),
                pl.BlockSpec((ts, w), lambda j, i, jm=jm: (i, jm(j))),
                pl.BlockSpec((halo, w), lambda j, i, jm=jm, hb=hb, nhb=nhb: (jnp.minimum((i + 1) * hb, nhb - 1), jm(j))),
            ]
            args += [arr, arr, arr]
        else:
            in_specs.append(pl.BlockSpec((ts, w), lambda j, i, jm=jm: (i, jm(j))))
            args.append(arr)
    for arr, r, w, jm in params:
        in_specs.append(pl.BlockSpec((r, w), lambda j, i, jm=jm: (0, jm(j))))
        args.append(arr)
    out_specs, out_shape = [], []
    for w, dt, jm, tw, *dil in outs:
        if dil:
            out_specs.append(pl.BlockSpec((ts // dil[0], dil[0] * w), lambda j, i: (i, 0)))
            out_shape.append(jax.ShapeDtypeStruct((S // dil[0], dil[0] * w), dt))
            scratch += [pltpu.VMEM((ts, w), F32), pltpu.VMEM((w // LANES, ts, LANES), F32)]
        else:
            out_specs.append(pl.BlockSpec((ts, w), lambda j, i, jm=jm: (i, jm(j))))
            out_shape.append(jax.ShapeDtypeStruct((S, tw), dt))
    for r, w, jm, tw in accs:
        out_specs.append(pl.BlockSpec((r, w), lambda j, i, jm=jm: (0, jm(j))))
        out_shape.append(jax.ShapeDtypeStruct((r, tw), F32))
    n_tiles, n_params, n_outs, n_accs = len(tiles), len(params), len(outs), len(accs)

    def residue_rows(r, d):
        return pl.ds(r, ts // d, stride=d)

    def body(*refs):
        i = pl.program_id(1)
        n_io = len(in_specs) + n_outs + n_accs
        scr = list(refs[n_io:])
        refs = refs[:n_io]
        pos, vals = 0, []
        for _, w, _, with_halo in tiles:
            if isinstance(with_halo, int) and with_halo > 1:
                d, buf = with_halo, scr.pop(0)
                for r in range(d):
                    for c in range(w // LANES):
                        buf[c, residue_rows(r, d), :] = refs[pos][:, r * w + c * LANES:r * w + (c + 1) * LANES].astype(F32)
                vals.append(jnp.concatenate([buf[c] for c in range(w // LANES)], axis=1))
                pos += 1
            elif with_halo:
                before, after = refs[pos][...], refs[pos + 2][...]
                before = jnp.where(i > 0, before, jnp.zeros_like(before))
                after = jnp.where(i < S // ts - 1, after, jnp.zeros_like(after))
                vals.append(jnp.concatenate([before, refs[pos + 1][...], after], axis=0))
                pos += 3
            else:
                vals.append(refs[pos][...])
                pos += 1
        prefs = refs[pos:pos + n_params]
        orefs = list(refs[pos + n_params:pos + n_params + n_outs])
        arefs = refs[pos + n_params + n_outs:]
        staged = []
        for k, (w, _, _, _, *dil) in enumerate(outs):
            if dil:
                staged.append((orefs[k], scr.pop(0), scr.pop(0), w, dil[0]))
                orefs[k] = staged[-1][1]

        @pl.when(i == 0)
        def _():
            for r in arefs:
                r[...] = jnp.zeros_like(r)

        fn(i, vals, prefs, orefs, arefs)
        for out_ref, flat, buf, w, d in staged:
            for c in range(w // LANES):
                buf[c] = flat[:, c * LANES:(c + 1) * LANES]
                for r in range(d):
                    out_ref[:, r * w + c * LANES:r * w + (c + 1) * LANES] = buf[c, residue_rows(r, d), :].astype(out_ref.dtype)

    res = pl.pallas_call(
        body, name=name, grid=(ncb, S // ts),
        in_specs=in_specs, out_specs=out_specs, out_shape=out_shape, scratch_shapes=scratch,
        compiler_params=_cparams(("arbitrary", "arbitrary")),
    )(*args)
    return res


def _vmem_call(name, fn, ins, out_shapes):
    n_in = len(ins)

    def body(*refs):
        vals = fn(*[r[...] for r in refs[:n_in]])
        for r, v in zip(refs[n_in:], vals):
            r[...] = v.astype(r.dtype)

    return pl.pallas_call(
        body, name=name,
        out_shape=[jax.ShapeDtypeStruct(s, dt) for s, dt in out_shapes],
        compiler_params=_cparams(),
    )(*ins)


def _rms(x, w):
    return x * lax.rsqrt(jnp.mean(x * x, axis=-1, keepdims=True) + EPS) * w


def _normmod_f(x, nw, sc, sh):
    return _rms(x, nw) * (1.0 + sc) + sh


def _row_params(*vecs):
    return [(v, 1, v.shape[1], _c0) for v in vecs]


def _normmod_fwd(x, nw, sc, sh):
    S = x.shape[0]

    def fn(i, vals, p, o, a):
        o[0][...] = _normmod_f(vals[0], p[0][...], p[1][...], p[2][...]).astype(BF16)

    return _rowwise("normmod_fwd", fn, S, 512, [(x, D, _c0, False)], _row_params(nw, sc, sh), [(D, BF16, _c0, D)])[0]


def _normmod_bwd(x, gh, gres, nw, sc, sh):
    S = x.shape[0]

    def fn(i, vals, p, o, a):
        _, vjp = jax.vjp(_normmod_f, vals[0], p[0][...], p[1][...], p[2][...])
        gx, gnw, gsc, gsh = vjp(vals[1])
        o[0][...] = gx + vals[2]
        a[0][...] += gnw
        a[1][...] += gsc
        a[2][...] += gsh

    return _rowwise("normmod_bwd", fn, S, 512, [(x, D, _c0, False), (gh, D, _c0, False), (gres, D, _c0, False)],
                    _row_params(nw, sc, sh), [(D, F32, _c0, D)], [(1, D, _c0, D)] * 3)


def _gate_add(x, y, g):
    S = x.shape[0]

    def fn(i, vals, p, o, a):
        o[0][...] = vals[0] + p[0][...] * vals[1]

    return _rowwise("gate_add", fn, S, 512, [(x, D, _c0, False), (y, D, _c0, False)], _row_params(g), [(D, F32, _c0, D)])[0]


def _gate_bwd(gx, y, g):
    S = gx.shape[0]

    def fn(i, vals, p, o, a):
        o[0][...] = (vals[0] * p[0][...]).astype(BF16)
        a[0][...] += jnp.sum(vals[0] * vals[1], axis=0, keepdims=True)

    return _rowwise("gate_bwd", fn, S, 512, [(gx, D, _c0, False), (y, D, _c0, False)], _row_params(g),
                    [(D, BF16, _c0, D)], [(1, D, _c0, D)])


def _gate_add_normmod(x, y, g, nw, sc, sh):
    S = x.shape[0]

    def fn(i, vals, p, o, a):
        x2 = vals[0] + p[0][...] * vals[1]
        o[0][...] = x2
        o[1][...] = _normmod_f(x2, p[1][...], p[2][...], p[3][...]).astype(BF16)

    return _rowwise("gate_add_normmod", fn, S, 512, [(x, D, _c0, False), (y, D, _c0, False)], _row_params(g, nw, sc, sh),
                    [(D, F32, _c0, D), (D, BF16, _c0, D)])


def _normmod_gate_bwd(x, gh, gres, nw, sc, sh, y, g):
    S = x.shape[0]

    def fn(i, vals, p, o, a):
        _, vjp = jax.vjp(_normmod_f, vals[0], p[0][...], p[1][...], p[2][...])
        gx, gnw, gsc, gsh = vjp(vals[1])
        gx = gx + vals[2]
        o[0][...] = gx
        o[1][...] = (gx * p[3][...]).astype(BF16)
        a[0][...] += gnw
        a[1][...] += gsc
        a[2][...] += gsh
        a[3][...] += jnp.sum(gx * vals[3], axis=0, keepdims=True)

    tiles = [(t, D, _c0, False) for t in (x, gh, gres, y)]
    return _rowwise("normmod_gate_bwd", fn, S, 512, tiles, _row_params(nw, sc, sh, g),
                    [(D, F32, _c0, D), (D, BF16, _c0, D)], [(1, D, _c0, D)] * 4)


def _loss_head(x, tgt, fw):
    S = x.shape[0]

    def fn(i, vals, p, o, a):
        y, vjp = jax.vjp(_rms, vals[0], p[0][...])
        err = y - vals[1]
        gx, gfw = vjp(err * (1.0 / D))
        o[0][...] = gx
        a[0][...] += gfw
        part = 0.5 * jnp.sum(jnp.mean(err * err, axis=-1, keepdims=True), axis=0, keepdims=True)
        a[1][...] += jnp.broadcast_to(part, (1, LANES))

    return _rowwise("loss_head", fn, S, 256, [(x, D, _c0, False), (tgt, D, _c0, False)], _row_params(fw),
                    [(D, F32, _c0, D)], [(1, D, _c0, D), (1, LANES, _c0, LANES)])


CONV_HALO = 16
CONV_TS = 512


def _shifted(ext, shift, ts, halo):
    n = ext.shape[0]
    s = shift % n
    r = ext if s == 0 else pltpu.roll(ext, s, 0)
    return r[halo:halo + ts]


def _ln_silu(a, w, b):
    mu = jnp.mean(a, axis=-1, keepdims=True)
    var = jnp.mean(jnp.square(a - mu), axis=-1, keepdims=True)
    y = (a - mu) * lax.rsqrt(var + EPS) * w + b
    return y * jax.nn.sigmoid(y)


def _conv_a_fwd(proj, w_pad, b, lnw, lnb):
    S = proj.shape[0]
    ts, H = min(CONV_TS, S), CONV_HALO

    def fn(i, vals, p, o, a):
        a0 = vals[0] * jax.nn.sigmoid(vals[1])
        acc = jnp.zeros((ts, CONV_CH), F32) + p[1][...]
        for k in range(CONV_W):
            acc = acc + _shifted(a0, CONV_W // 2 - k, ts, H) * p[0][pl.ds(k, 1), :]
        o[0][...] = acc
        o[1][...] = _ln_silu(acc, p[2][...], p[3][...]).astype(BF16)

    tiles = [(proj, CONV_CH, lambda j: AV_BLK, True), (proj, CONV_CH, lambda j: AG_BLK, True)]
    params = [(w_pad, CONV_W_PAD, CONV_CH, _c0)] + _row_params(b, lnw, lnb)
    return _rowwise("conv_a_fwd", fn, S, ts, tiles, params, [(CONV_CH, F32, _c0, CONV_CH), (CONV_CH, BF16, _c0, CONV_CH)], halo=H)


def _conv_a_bwd(proj, a1, gmixed, w_pad, lnw, lnb):
    S = proj.shape[0]
    ts, H = min(CONV_TS, S), CONV_HALO

    def fn(i, vals, p, o, a):
        av, ag, a1e, ge = vals
        lw, lb = p[1][...], p[2][...]
        _, vjp_e = jax.vjp(lambda t: _ln_silu(t, lw, lb), a1e)
        (ga1e,) = vjp_e(ge)
        c = slice(H, H + ts)
        _, vjp_c = jax.vjp(_ln_silu, a1e[c], lw, lb)
        ga1, glw, glb = vjp_c(ge[c])
        a[1][...] += jnp.sum(ga1, axis=0, keepdims=True)
        a[2][...] += glw
        a[3][...] += glb
        sg = jax.nn.sigmoid(ag)
        a0 = av * sg
        ga0 = jnp.zeros((ts, CONV_CH), F32)
        for k in range(CONV_W):
            a[0][pl.ds(k, 1), :] += jnp.sum(ga1 * _shifted(a0, CONV_W // 2 - k, ts, H), axis=0, keepdims=True)
            ga0 = ga0 + _shifted(ga1e, k - CONV_W // 2, ts, H) * p[0][pl.ds(k, 1), :]
        sgc, avc = sg[c], av[c]
        o[0][...] = (ga0 * sgc).astype(BF16)
        o[1][...] = (ga0 * avc * sgc * (1.0 - sgc)).astype(BF16)

    tiles = [(proj, CONV_CH, lambda j: AV_BLK, True), (proj, CONV_CH, lambda j: AG_BLK, True),
             (a1, CONV_CH, _c0, True), (gmixed, CONV_CH, lambda j: 3, True)]
    params = [(w_pad, CONV_W_PAD, CONV_CH, _c0)] + _row_params(lnw, lnb)
    outs = [(CONV_CH, BF16, _c0, CONV_CH), (CONV_CH, BF16, _c0, CONV_CH)]
    accs = [(CONV_W_PAD, CONV_CH, _c0, CONV_CH)] + [(1, CONV_CH, _c0, CONV_CH)] * 3
    return _rowwise("conv_a_bwd", fn, S, ts, tiles, params, outs, accs, halo=H)


FFN_HALO = 8
FFN_TS = 512
FFN_TS_FWD = 1024
FFN_CB = 256
FFN_NCB = D_FF // FFN_CB


def _gelu_mul(g, v):
    return 0.5 * g * (1.0 + lax.erf(g * (2.0 ** -0.5))) * v


def _ffn_mid_fwd(u, cw):
    S = u.shape[0]
    ts, H = min(FFN_TS_FWD, S), FFN_HALO

    def conv(ext, w_ref):
        acc = jnp.zeros((ts, FFN_CB), F32)
        for k in range(FFN_CONV_W):
            acc = acc + _shifted(ext, 1 - k, ts, H) * w_ref[pl.ds(k, 1), :]
        return acc

    def fn(i, vals, p, o, a):
        o[0][...] = _gelu_mul(conv(vals[0], p[0]), conv(vals[1], p[1])).astype(BF16)

    gate, val = (lambda j: j), (lambda j: j + FFN_NCB)
    return _rowwise("ffn_mid_fwd", fn, S, ts, [(u, FFN_CB, gate, True), (u, FFN_CB, val, True)],
                    [(cw, FFN_CONV_W, FFN_CB, gate), (cw, FFN_CONV_W, FFN_CB, val)],
                    [(FFN_CB, BF16, gate, D_FF)], halo=H, ncb=FFN_NCB)[0]


def _ffn_mid_bwd(u, gact, cw):
    S = u.shape[0]
    ts, H = min(FFN_TS, S), FFN_HALO
    n = ts + 2 * H

    def fn(i, vals, p, o, a):
        ug, uv, ga = vals

        def conv_all(ue, w_ref):
            acc = jnp.zeros((n, FFN_CB), F32)
            for k in range(FFN_CONV_W):
                s = (1 - k) % n
                acc = acc + (ue if s == 0 else pltpu.roll(ue, s, 0)) * w_ref[pl.ds(k, 1), :]
            return acc

        _, vjp = jax.vjp(_gelu_mul, conv_all(ug, p[0]), conv_all(uv, p[1]))
        for half, (gc, ue) in enumerate(zip(vjp(ga), (ug, uv))):
            gu = jnp.zeros((ts, FFN_CB), F32)
            for k in range(FFN_CONV_W):
                gu = gu + _shifted(gc, k - 1, ts, H) * p[half][pl.ds(k, 1), :]
                a[half][pl.ds(k, 1), :] += jnp.sum(gc[H:H + ts] * _shifted(ue, 1 - k, ts, H), axis=0, keepdims=True)
            o[half][...] = gu.astype(BF16)

    gate, val = (lambda j: j), (lambda j: j + FFN_NCB)
    tiles = [(u, FFN_CB, gate, True), (u, FFN_CB, val, True), (gact, FFN_CB, gate, True)]
    params = [(cw, FFN_CONV_W, FFN_CB, gate), (cw, FFN_CONV_W, FFN_CB, val)]
    gu_gate, gu_val, gw_gate, gw_val = _rowwise("ffn_mid_bwd", fn, S, ts, tiles, params, [(FFN_CB, BF16, gate, D_FF)] * 2,
                                                [(FFN_CONV_W, FFN_CB, gate, D_FF)] * 2, halo=H, ncb=FFN_NCB)
    return (gu_gate, gu_val), jnp.concatenate([gw_gate, gw_val], axis=1)


LD_W = LANES
PAIR_W = 2 * HEAD
N_PAIRS = N_HEADS // 2


def _sub_view(t, d):
    S, C = t.shape
    return t.reshape(S // d, d * C)


def _sub_halo_specs(width, col, blk, hb, nhb):
    per = blk // hb
    return [
        pl.BlockSpec((hb, width), lambda r, i: (jnp.maximum(i * per - 1, 0), col(r))),
        pl.BlockSpec((blk, width), lambda r, i: (i, col(r))),
        pl.BlockSpec((hb, width), lambda r, i: (jnp.minimum((i + 1) * per, nhb - 1), col(r))),
    ]


def _pick_lane(t, lane):
    return jnp.sum(jnp.where(_iota((1, t.shape[1]), 1) == lane, t, 0.0), axis=1, keepdims=True)


def _pair_mask(h2):
    return (_iota((1, PAIR_W), 1) >> HEAD_SHIFT) == h2


def _cat_bf16(a, b, c):
    return jnp.concatenate([a[...], b[...], c[...]], axis=0).astype(BF16)


def _attn_fwd(view, dil):
    L = view.shape[0]
    blk, hb = min(ATT_BLK, L), HALF_BAND
    sub = min(ATT_SUB, blk)
    span = sub + 2 * hb

    def body(q_ref, kp, kc, kn, vp, vc, vn, o_ref, l_ref):
        i = pl.program_id(1)
        rel = _iota((sub, span), 1) - hb - _iota((sub, span), 0)
        band = jnp.abs(rel) <= hb
        dist = jnp.abs(rel).astype(F32) * float(dil)
        q_all, k_all, v_all = q_ref[...].astype(BF16), _cat_bf16(kp, kc, kn), _cat_bf16(vp, vc, vn)
        for r0 in range(0, blk, sub):
            kpos = i * blk + r0 - hb + _iota((sub, span), 1)
            valid = band & (kpos >= 0) & (kpos < L)
            q, k, v = q_all[r0:r0 + sub], k_all[r0:r0 + span], v_all[r0:r0 + span]
            lse = jnp.zeros((sub, LD_W), F32)
            for pr in range(N_PAIRS):
                sl = slice(pr * PAIR_W, (pr + 1) * PAIR_W)
                qp, kpair, vpair = q[:, sl], k[:, sl], v[:, sl]
                o = jnp.zeros((sub, PAIR_W), F32)
                for h2 in range(2):
                    h, mask = 2 * pr + h2, _pair_mask(h2)
                    s = _dot_nt(jnp.where(mask, qp, jnp.zeros_like(qp)), kpair) * (HEAD ** -0.5) - ALIBI_SLOPES[h] * dist
                    s = jnp.where(valid, s, MASK_VALUE)
                    m = jnp.max(s, axis=1, keepdims=True)
                    p = jnp.exp(s - m)
                    l = jnp.sum(p, axis=1, keepdims=True)
                    o = jnp.where(mask, _dot_nn(p.astype(BF16), vpair) / l, o)
                    lse = lse + jnp.where(_iota((1, LD_W), 1) == h, m + jnp.log(l), 0.0)
                o_ref[r0:r0 + sub, sl] = o
            l_ref[r0:r0 + sub, :] = lse

    nhb = L // hb
    in_specs = ([pl.BlockSpec((blk, ATT_W), lambda r, i: (i, r * QKV_BLOCKS +QA))]
                + _sub_halo_specs(ATT_W, lambda r: r * QKV_BLOCKS +KA, blk, hb, nhb)
                + _sub_halo_specs(ATT_W, lambda r: r * QKV_BLOCKS +VA, blk, hb, nhb))
    return pl.pallas_call(
        body, name=f"attn_fwd_d{dil}", grid=(dil, L // blk), in_specs=in_specs,
        out_specs=[pl.BlockSpec((blk, ATT_W), lambda r, i: (i, r)), pl.BlockSpec((blk, LD_W), lambda r, i: (i, r))],
        out_shape=[jax.ShapeDtypeStruct((L, dil * ATT_W), F32), jax.ShapeDtypeStruct((L, dil * LD_W), F32)],
        compiler_params=_cparams(("parallel", "parallel")),
    )(*([view] * 7))


def _attn_bwd(pview, gview, lview, dil):
    L = pview.shape[0]
    blk, hb = min(ATT_BLK, L), HALF_BAND
    sub = min(ATT_SUB, blk)
    span = sub + 2 * hb
    scale = HEAD ** -0.5

    def body(qp, qc, qn, kp, kc, kn, vp, vc, vn, gp, gc, gn, lp, lc, ln, dq_ref, dk_ref, dv_ref):
        i = pl.program_id(1)
        le_all = jnp.concatenate([lp[...], lc[...], ln[...]], axis=0)
        rel_q = _iota((sub, span), 1) - hb - _iota((sub, span), 0)
        band_q = jnp.abs(rel_q) <= hb
        dist_q = jnp.abs(rel_q).astype(F32) * float(dil)
        rel_k = _iota((span, sub), 1) + hb - _iota((span, sub), 0)
        band_k = jnp.abs(rel_k) <= hb
        dist_k = jnp.abs(rel_k).astype(F32) * float(dil)
        qe_all, ke_all, ve_all = _cat_bf16(qp, qc, qn), _cat_bf16(kp, kc, kn), _cat_bf16(vp, vc, vn)
        ge_all = _cat_bf16(gp, gc, gn)
        for r0 in range(0, blk, sub):
            kpos = i * blk + r0 - hb + _iota((sub, span), 1)
            valid_q = band_q & (kpos >= 0) & (kpos < L)
            qpos = i * blk + r0 - hb + _iota((span, sub), 0)
            valid_k = band_k & (qpos >= 0) & (qpos < L)
            ext, mid = slice(r0, r0 + span), slice(r0 + hb, r0 + hb + sub)
            l, le = le_all[mid], le_all[ext]
            for pr in range(N_PAIRS):
                sl = slice(pr * PAIR_W, (pr + 1) * PAIR_W)
                q, k, v, g = qe_all[mid, sl], ke_all[mid, sl], ve_all[mid, sl], ge_all[mid, sl]
                qe, ke, ve, ge = qe_all[ext, sl], ke_all[ext, sl], ve_all[ext, sl], ge_all[ext, sl]
                dq = jnp.zeros((sub, PAIR_W), F32)
                dk = jnp.zeros((sub, PAIR_W), F32)
                dv = jnp.zeros((sub, PAIR_W), F32)
                for h2 in range(2):
                    h, mask = 2 * pr + h2, _pair_mask(h2)
                    only = lambda t: jnp.where(mask, t, jnp.zeros_like(t))
                    s = _dot_nt(only(q), ke) * scale - ALIBI_SLOPES[h] * dist_q
                    p = jnp.where(valid_q, jnp.exp(s - _pick_lane(l, h)), 0.0)
                    ds = p * (_dot_nt(only(g), ve) - _pick_lane(l, 8 + h))
                    dq = jnp.where(mask, _dot_nn(ds.astype(BF16), ke), dq)
                    s = _dot_nt(only(qe), k) * scale - ALIBI_SLOPES[h] * dist_k
                    p = jnp.where(valid_k, jnp.exp(s - _pick_lane(le, h)), 0.0)
                    dv = jnp.where(mask, _dot_tn(p.astype(BF16), ge), dv)
                    ds = p * (_dot_nt(only(ge), v) - _pick_lane(le, 8 + h))
                    dk = jnp.where(mask, _dot_tn(ds.astype(BF16), qe), dk)
                dq_ref[r0:r0 + sub, sl] = (dq * scale).astype(BF16)
                dk_ref[r0:r0 + sub, sl] = (dk * scale).astype(BF16)
                dv_ref[r0:r0 + sub, sl] = dv.astype(BF16)

    nhb = L // hb
    in_specs = (_sub_halo_specs(ATT_W, lambda r: r * QKV_BLOCKS +QA, blk, hb, nhb)
                + _sub_halo_specs(ATT_W, lambda r: r * QKV_BLOCKS +KA, blk, hb, nhb)
                + _sub_halo_specs(ATT_W, lambda r: r * QKV_BLOCKS +VA, blk, hb, nhb)
                + _sub_halo_specs(ATT_W, lambda r: r, blk, hb, nhb) + _sub_halo_specs(LD_W, lambda r: r, blk, hb, nhb))
    o_spec = pl.BlockSpec((blk, ATT_W), lambda r, i: (i, r))
    return pl.pallas_call(
        body, name=f"attn_bwd_d{dil}", grid=(dil, L // blk), in_specs=in_specs,
        out_specs=[o_spec] * 3, out_shape=[jax.ShapeDtypeStruct((L, dil * ATT_W), BF16)] * 3,
        compiler_params=_cparams(("parallel", "parallel")),
    )(*([pview] * 9 + [gview] * 3 + [lview] * 3))


def _head_expand(t):
    e = ((_iota((LD_W, ATT_W), 1) >> HEAD_SHIFT) == _iota((LD_W, ATT_W), 0)).astype(F32)
    return _dot_nn(t, e, HP)


def _dil(d):
    return d if d > 1 else False


def _qkv_views(proj):
    S, w = proj.shape[0], QKV_BLOCKS * ATT_W

    def fn(i, vals, p, o, a):
        for k, d in enumerate(DILATIONS):
            o[k][...] = vals[0].astype(o[k].dtype)

    outs = [(w, BF16, _c0, w) + ((d,) if d > 1 else ()) for d in DILATIONS]
    return _rowwise("qkv_views", fn, S, 512, [(proj, w, _c0, False)], (), outs)


def _attn_merge(os, ls):
    S = os[0].shape[0]

    def fn(i, vals, p, o, a):
        o3, l3 = vals[:3], vals[3:]
        m = jnp.maximum(jnp.maximum(l3[0], l3[1]), l3[2])
        e3 = [jnp.exp(l - m) for l in l3]
        den = e3[0] + e3[1] + e3[2]
        out = jnp.zeros((o3[0].shape[0], ATT_W), F32)
        for ob, e in zip(o3, e3):
            out = out + _head_expand(e / den) * ob
        o[0][...] = out
        o[1][...] = m + jnp.log(den)
        o[2][...] = out.astype(BF16)

    tiles = ([(t, ATT_W, _c0, _dil(d)) for t, d in zip(os, DILATIONS)]
             + [(t, LD_W, _c0, _dil(d)) for t, d in zip(ls, DILATIONS)])
    return _rowwise("attn_merge", fn, S, 512, tiles, (),
                    [(ATT_W, F32, _c0, ATT_W), (LD_W, F32, _c0, LD_W), (ATT_W, BF16, _c0, ATT_W)])


def _attn_bwd_prep(gmixed, att, lse):
    S = att.shape[0]
    n = len(DILATIONS)

    def fn(i, vals, p, o, a):
        g, out, lse_row = vals
        place_d = ((_iota((ATT_W, LD_W), 0) >> HEAD_SHIFT) + 8 == _iota((ATT_W, LD_W), 1)).astype(F32)
        ld = jnp.where(_iota((1, LD_W), 1) < 8, lse_row, 0.0) + _dot_nn(g * out, place_d, HP)
        for k in range(n):
            o[k][...] = g.astype(o[k].dtype)
            o[n + k][...] = ld

    tiles = [(gmixed, ATT_W, _c0, False), (att, ATT_W, _c0, False), (lse, LD_W, _c0, False)]
    outs = ([(ATT_W, BF16, _c0, ATT_W) + ((d,) if d > 1 else ()) for d in DILATIONS]
            + [(LD_W, F32, _c0, LD_W) + ((d,) if d > 1 else ()) for d in DILATIONS])
    res = _rowwise("attn_bwd_prep", fn, S, 512, tiles, (), outs)
    return res[:n], res[n:]


def _sum3_bf16(views, S, width):
    def fn(i, vals, p, o, a):
        o[0][...] = (vals[0].astype(F32) + vals[1].astype(F32) + vals[2].astype(F32)).astype(BF16)

    tiles = [(t, width, _c0, _dil(d)) for t, d in zip(views, DILATIONS)]
    return _rowwise("sum3", fn, S, 512, tiles, (), [(width, BF16, _c0, width)])[0]


def _block_diag_mask():
    return ((_iota((REC_W, REC_W), 0) >> HEAD_SHIFT) == (_iota((REC_W, REC_W), 1) >> HEAD_SHIFT)).astype(F32)


def _hgrn_chunk(qr, z, iv, lb, st, reverse, precise):
    C = REC_CHUNK
    r, c = _iota((C, C), 0), _iota((C, C), 1)
    t_cum = (c >= r) if reverse else (c <= r)
    mid_row, last_row = (C // 2, 0) if reverse else (C // 2 - 1, C - 1)
    f = lb + (1.0 - lb) * jax.nn.sigmoid(z)
    logf = jnp.log(jnp.maximum(f, F_TINY))
    k = (1.0 - lb) * jax.nn.sigmoid(-z)
    q = qr * jax.nn.sigmoid(qr)
    b = _dot_nn(t_cum.astype(F32), logf, HP)
    row = _iota((C, 1), 0)
    bm = jnp.sum(jnp.where(row == mid_row, b, 0.0), axis=0, keepdims=True)
    bl = jnp.sum(jnp.where(row == last_row, b, 0.0), axis=0, keepdims=True)
    qt = q * jnp.exp(jnp.minimum(b - bm, EXP_CLAMP))
    kt = k * jnp.exp(jnp.minimum(bm - b, EXP_CLAMP))
    qh = q * jnp.exp(b)
    kh = k * jnp.exp(bl - b)
    lam = jnp.exp(bl)
    bd = ((_iota((PAIR_W, PAIR_W), 0) >> HEAD_SHIFT) == (_iota((PAIR_W, PAIR_W), 1) >> HEAD_SHIFT)).astype(F32)
    s_in = _iota((C, PAIR_W), 1) & (HEAD - 1)
    t_in = _iota((C, PAIR_W), 0)
    tri = (s_in >= t_in) if reverse else (s_in <= t_in)
    twice = lambda t: jnp.concatenate([t, t], axis=0)
    outs, states = [], []
    for pr in range(N_PAIRS):
        sl = slice(pr * PAIR_W, (pr + 1) * PAIR_W)
        k_bd = twice(kt[:, sl]) * bd
        v_bd = (twice(iv[:, sl]) * bd).astype(BF16)
        st_bd = twice(st[:, sl]) * bd
        if precise:
            scores = _dot_nt(qt[:, sl], k_bd, lax.Precision.HIGH)
        else:
            scores = _dot_nt(qt[:, sl].astype(BF16), k_bd.astype(BF16))
        a = jnp.where(tri, scores, 0.0)
        outs.append(_dot_nn(a.astype(BF16), v_bd) + _dot_nt(qh[:, sl].astype(BF16), st_bd.astype(BF16)))
        kv = _dot_tn(iv[:, sl].astype(BF16), kh[:, sl].astype(BF16))
        st_bd = st_bd * lam[:, sl] + kv * bd
        states.append(st_bd[0:HEAD] + st_bd[HEAD:PAIR_W])
    return jnp.concatenate(outs, axis=1), jnp.concatenate(states, axis=1)


REC_CHUNKS_PER_STEP = 8
REC_ROWS = REC_CHUNKS_PER_STEP * REC_CHUNK


def _hgrn_specs(order, blocks):
    return [pl.BlockSpec((REC_ROWS, REC_W), lambda i, b=b: (order(i), b)) for b in blocks]


def _chunk_rows(j):
    return pl.ds(pl.multiple_of(j * REC_CHUNK, REC_CHUNK), REC_CHUNK)


def _hgrn_fwd(proj, lb, z_blk, reverse):
    S = proj.shape[0]
    nb = S // REC_ROWS
    order = (lambda i: nb - 1 - i) if reverse else (lambda i: i)

    def body(q_ref, z_ref, v_ref, lb_ref, o_ref, st_ref, st_scr):
        @pl.when(pl.program_id(0) == 0)
        def _():
            st_scr[...] = jnp.zeros_like(st_scr)

        def step(t, carry):
            j = REC_CHUNKS_PER_STEP - 1 - t if reverse else t
            rows = _chunk_rows(j)
            st = st_scr[...]
            st_ref[j] = st
            o, st_new = _hgrn_chunk(q_ref[rows, :], z_ref[rows, :], v_ref[rows, :], lb_ref[...], st, reverse, False)
            o_ref[rows, :] = o
            st_scr[...] = st_new
            return carry

        lax.fori_loop(0, REC_CHUNKS_PER_STEP, step, 0, unroll=True)

    return pl.pallas_call(
        body, name="hgrn_rev_fwd" if reverse else "hgrn_fwd_fwd", grid=(nb,),
        in_specs=_hgrn_specs(order, (QR, z_blk, IR)) + [pl.BlockSpec((1, REC_W), lambda i: (0, 0))],
        out_specs=[pl.BlockSpec((REC_ROWS, REC_W), lambda i: (order(i), 0)),
                   pl.BlockSpec((REC_CHUNKS_PER_STEP, HEAD, REC_W), lambda i: (order(i), 0, 0))],
        out_shape=[jax.ShapeDtypeStruct((S, REC_W), F32), jax.ShapeDtypeStruct((S // REC_CHUNK, HEAD, REC_W), F32)],
        scratch_shapes=[pltpu.VMEM((HEAD, REC_W), F32)],
        compiler_params=_cparams(("arbitrary",)),
    )(proj, proj, proj, lb)


def _hgrn_bwd(proj, lb, states, go, z_blk, reverse, other=None):
    S = proj.shape[0]
    nb = S // REC_ROWS
    order = (lambda i: i) if reverse else (lambda i: nb - 1 - i)
    n_other = 0 if other is None else 2

    def body(*refs):
        q_ref, z_ref, v_ref, lb_ref, st_ref, go_ref = refs[:6]
        other_refs = refs[6:6 + n_other]
        gq_ref, gz_ref, gv_ref, glb_ref, gst_scr = refs[6 + n_other:]

        @pl.when(pl.program_id(0) == 0)
        def _():
            gst_scr[...] = jnp.zeros_like(gst_scr)
            glb_ref[...] = jnp.zeros_like(glb_ref)

        chunk = functools.partial(_hgrn_chunk, reverse=reverse, precise=True)

        def step(t, carry):
            j = t if reverse else REC_CHUNKS_PER_STEP - 1 - t
            rows = _chunk_rows(j)
            _, vjp = jax.vjp(chunk, q_ref[rows, :], z_ref[rows, :], v_ref[rows, :], lb_ref[...], st_ref[j])
            gq, gz, gv, glb, gst = vjp((go_ref[rows, :], gst_scr[...]))
            if other_refs:
                gq = gq + other_refs[0][rows, :]
                gv = gv + other_refs[1][rows, :]
            gq_ref[rows, :] = gq.astype(gq_ref.dtype)
            gz_ref[rows, :] = gz.astype(gz_ref.dtype)
            gv_ref[rows, :] = gv.astype(gv_ref.dtype)
            glb_ref[...] += glb
            gst_scr[...] = gst
            return carry

        lax.fori_loop(0, REC_CHUNKS_PER_STEP, step, 0, unroll=4)

    row_spec = pl.BlockSpec((REC_ROWS, REC_W), lambda i: (order(i), 0))
    return pl.pallas_call(
        body, name="hgrn_rev_bwd" if reverse else "hgrn_fwd_bwd", grid=(nb,),
        in_specs=(_hgrn_specs(order, (QR, z_blk, IR)) + [pl.BlockSpec((1, REC_W), lambda i: (0, 0))]
                  + [pl.BlockSpec((REC_CHUNKS_PER_STEP, HEAD, REC_W), lambda i: (order(i), 0, 0)), row_spec]
                  + [row_spec] * n_other),
        out_specs=[row_spec] * 3 + [pl.BlockSpec((1, REC_W), lambda i: (0, 0))],
        out_shape=([jax.ShapeDtypeStruct((S, REC_W), dt) for dt in (BF16 if other else F32, BF16, BF16 if other else F32)]
                   + [jax.ShapeDtypeStruct((1, REC_W), F32)]),
        scratch_shapes=[pltpu.VMEM((HEAD, REC_W), F32)],
        compiler_params=_cparams(("arbitrary",)),
    )(proj, proj, proj, lb, states, go, *(other or ()))


def _hgrn_post_f(of, ob, gr, rnw):
    o = of + ob
    ms = _dot_nn(o * o, _block_diag_mask() * (1.0 / HEAD), HP)
    return o * lax.rsqrt(ms + EPS) * rnw * (gr * jax.nn.sigmoid(gr))


def _hgrn_post_fwd(of, ob, proj, rnw):
    S = of.shape[0]

    def fn(i, vals, p, o, a):
        o[0][...] = _hgrn_post_f(vals[0], vals[1], vals[2], p[0][...]).astype(BF16)

    tiles = [(of, REC_W, _c0, False), (ob, REC_W, _c0, False), (proj, REC_W, lambda j: GR, False)]
    return _rowwise("hgrn_post_fwd", fn, S, 512, tiles, _row_params(rnw), [(REC_W, BF16, _c0, REC_W)])[0]


def _hgrn_post_bwd(of, ob, proj, gmixed, rnw):
    S = of.shape[0]

    def fn(i, vals, p, o, a):
        _, vjp = jax.vjp(_hgrn_post_f, vals[0], vals[1], vals[2], p[0][...])
        go, _, ggr, grnw = vjp(vals[3])
        o[0][...] = go
        o[1][...] = ggr.astype(BF16)
        a[0][...] += grnw

    tiles = [(of, REC_W, _c0, False), (ob, REC_W, _c0, False), (proj, REC_W, lambda j: GR, False),
             (gmixed, REC_W, lambda j: 1, False)]
    return _rowwise("hgrn_post_bwd", fn, S, 512, tiles, _row_params(rnw),
                    [(REC_W, F32, _c0, REC_W), (REC_W, BF16, _c0, REC_W)], [(1, REC_W, _c0, REC_W)])


def _lower_bounds_f(g0, g1):
    m = jnp.maximum(g0, g1)
    e0, e1 = jnp.exp(g0 - m), jnp.exp(g1 - m)
    return e1 / (e0 + e1)


def _adamw(name, w, m, v, gparts):
    R, C = w.shape
    P = gparts.shape[0]
    tr = R if R * C * 4 * (P + 7) * 2 <= VMEM_LIMIT_BYTES // 2 else _pick(R, (256, 128, 64, 32, 16, 8))

    def body(w_ref, m_ref, v_ref, gp_ref, g_ref, d_ref, nm_ref, nv_ref):
        g = gp_ref[0].astype(F32)
        for p in range(1, P):
            g = g + gp_ref[p].astype(F32)
        w_ = w_ref[...]
        nm = ADAM_B1 * m_ref[...] + (1.0 - ADAM_B1) * g
        nv = ADAM_B2 * v_ref[...] + (1.0 - ADAM_B2) * jnp.square(g)
        m_hat = nm / (1.0 - ADAM_B1 ** ADAM_STEP)
        v_hat = nv / (1.0 - ADAM_B2 ** ADAM_STEP)
        g_ref[...] = g
        d_ref[...] = -ADAM_LR * (m_hat / (jnp.sqrt(v_hat) + ADAM_EPS) + ADAM_WD * w_)
        nm_ref[...] = nm
        nv_ref[...] = nv

    spec = pl.BlockSpec((tr, C), lambda i: (i, 0))
    return pl.pallas_call(
        body, name=name, grid=(R // tr,),
        in_specs=[spec, spec, spec, pl.BlockSpec((P, tr, C), lambda i: (0, i, 0))],
        out_specs=[spec] * 4, out_shape=[jax.ShapeDtypeStruct((R, C), F32)] * 4,
        compiler_params=_cparams(("parallel",)),
    )(w, m, v, gparts)


def _place():
    return lax.axis_index("x"), lax.axis_index("y"), lax.axis_index("c")


def _index_of(p):
    return 4 * p[0] + 2 * p[1] + p[2]


def _allgather_small(name, rows):
    m_per, n = rows.shape

    def body(x_ref, out_ref, send_sems, recv_sems, local_sem):
        x, y, c = _place()
        me, sibling = (x, y, c), (x, y, 1 - c)
        chips = [(1 - x, y), (x, 1 - y), (1 - x, 1 - y)]

        def blk(p):
            return out_ref.at[pl.ds(_index_of(p) * m_per, m_per), :]

        def copy(k, block, to, src=None):
            return pltpu.make_async_remote_copy(
                src_ref=blk(block) if src is None else src, dst_ref=blk(block),
                send_sem=send_sems.at[k], recv_sem=recv_sems.at[k], device_id=to, device_id_type=MESH)

        mine = pltpu.make_async_copy(x_ref, blk(me), local_sem)
        mine.start()
        first = [copy(0, me, sibling, src=x_ref)]
        first += [copy(1 + j, me, (*chip, c), src=x_ref) for j, chip in enumerate(chips)]
        for cp in first:
            cp.start()
        passed = [copy(4 + j, (*chip, c), sibling) for j, chip in enumerate(chips)]
        for j, chip in enumerate(chips):
            copy(1 + j, (*chip, c), me).wait_recv()
            passed[j].start()
        copy(0, sibling, me).wait_recv()
        for j, chip in enumerate(chips):
            copy(4 + j, (*chip, 1 - c), me).wait_recv()
        for cp in first + passed:
            cp.wait_send()
        mine.wait()

    return pl.pallas_call(
        body, name=name,
        out_shape=jax.ShapeDtypeStruct((N_DEV * m_per, n), rows.dtype),
        in_specs=[pl.BlockSpec(memory_space=pltpu.VMEM)],
        out_specs=pl.BlockSpec(memory_space=pltpu.VMEM),
        scratch_shapes=[pltpu.SemaphoreType.DMA((7,)), pltpu.SemaphoreType.DMA((7,)), pltpu.SemaphoreType.DMA],
        compiler_params=_cparams(),
    )(rows)


def _allgather_big(name, arrs):
    na = len(arrs)

    def body(*refs):
        ins, outs = refs[:na], refs[na:2 * na]
        send_sems, recv_sems, local_sems = refs[2 * na:]
        x, y, c = _place()
        me, sibling = (x, y, c), (x, y, 1 - c)
        chips = [(1 - x, y), (x, 1 - y), (1 - x, 1 - y)]

        def copy(a, k, block, to, src=None):
            dst = outs[a].at[_index_of(block)]
            return pltpu.make_async_remote_copy(
                src_ref=dst if src is None else src, dst_ref=dst,
                send_sem=send_sems.at[a, k], recv_sem=recv_sems.at[a, k], device_id=to, device_id_type=MESH)

        mine = [pltpu.make_async_copy(ins[a], outs[a].at[_index_of(me)], local_sems.at[a]) for a in range(na)]
        for cp in mine:
            cp.start()
        sent = []
        for a in range(na):
            sent.append(copy(a, 0, me, sibling, src=ins[a]))
            sent += [copy(a, 1 + j, me, (*chip, c), src=ins[a]) for j, chip in enumerate(chips)]
        for cp in sent:
            cp.start()
        for j, chip in enumerate(chips):
            for a in range(na):
                copy(a, 1 + j, (*chip, c), me).wait_recv()
                fwd = copy(a, 4 + j, (*chip, c), sibling)
                fwd.start()
                sent.append(fwd)
        for a in range(na):
            copy(a, 0, sibling, me).wait_recv()
            for j, chip in enumerate(chips):
                copy(a, 4 + j, (*chip, 1 - c), me).wait_recv()
        for cp in sent:
            cp.wait_send()
        for cp in mine:
            cp.wait()

    any_spec = pl.BlockSpec(memory_space=pl.ANY)
    return pl.pallas_call(
        body, name=name,
        out_shape=[jax.ShapeDtypeStruct((N_DEV,) + a.shape, a.dtype) for a in arrs],
        in_specs=[any_spec] * na, out_specs=[any_spec] * na,
        scratch_shapes=[pltpu.SemaphoreType.DMA((na, 7)), pltpu.SemaphoreType.DMA((na, 7)), pltpu.SemaphoreType.DMA((na,))],
        compiler_params=_cparams(),
    )(*arrs)


N_CHIPS = 4


def _scatter_to_sibling(name, parts):
    na = len(parts)

    def body(*refs):
        ins, outs = refs[:na], refs[na:2 * na]
        send_sems, recv_sems = refs[2 * na:]
        x, y, c = _place()
        sibling = (x, y, 1 - c)
        sent = []
        for a in range(na):
            for q in range(N_CHIPS):
                sent.append(pltpu.make_async_remote_copy(
                    src_ref=ins[a].at[2 * q + (1 - c)], dst_ref=outs[a].at[q],
                    send_sem=send_sems.at[a, q], recv_sem=recv_sems.at[a, q], device_id=sibling, device_id_type=MESH))
        for cp in sent:
            cp.start()
        for cp in sent:
            cp.wait_recv()
        for cp in sent:
            cp.wait_send()

    any_spec = pl.BlockSpec(memory_space=pl.ANY)
    return pl.pallas_call(
        body, name=name,
        out_shape=[jax.ShapeDtypeStruct((N_CHIPS,) + p.shape[1:], p.dtype) for p in parts],
        in_specs=[any_spec] * na, out_specs=[any_spec] * na,
        scratch_shapes=[pltpu.SemaphoreType.DMA((na, N_CHIPS)), pltpu.SemaphoreType.DMA((na, N_CHIPS))],
        compiler_params=_cparams(),
    )(*parts)


def _pair_sum(name, parts, recv):
    _, R, C = parts.shape
    tr = _pick(R, (256, 128, 64, 32, 16))

    def body(p_ref, r_ref, o_ref):
        c = lax.axis_index("c")
        o_ref[...] = (p_ref[c].astype(F32) + r_ref[...].astype(F32)).astype(BF16)

    return pl.pallas_call(
        body, name=name, grid=(N_CHIPS, R // tr),
        in_specs=[pl.BlockSpec((None, 2, tr, C), lambda q, i: (q, 0, i, 0)), pl.BlockSpec((None, tr, C), lambda q, i: (q, i, 0))],
        out_specs=pl.BlockSpec((None, tr, C), lambda q, i: (q, i, 0)),
        out_shape=jax.ShapeDtypeStruct((N_CHIPS, R, C), BF16),
        compiler_params=_cparams(("parallel", "parallel")),
    )(parts.reshape(N_CHIPS, 2, R, C), recv)


def _scatter_to_chips(name, sums):
    na = len(sums)

    def body(*refs):
        ins, outs = refs[:na], refs[na:2 * na]
        send_sems, recv_sems, local_sems = refs[2 * na:]
        x, y, c = _place()
        me = 2 * x + y
        chips = [(1 - x, y), (x, 1 - y), (1 - x, 1 - y)]
        mine = [pltpu.make_async_copy(ins[a].at[me], outs[a].at[me], local_sems.at[a]) for a in range(na)]
        for cp in mine:
            cp.start()
        sent = []
        for a in range(na):
            for k, (qx, qy) in enumerate(chips):
                sent.append(pltpu.make_async_remote_copy(
                    src_ref=ins[a].at[2 * qx + qy], dst_ref=outs[a].at[me],
                    send_sem=send_sems.at[a, k], recv_sem=recv_sems.at[a, k], device_id=(qx, qy, c), device_id_type=MESH))
        for cp in sent:
            cp.start()
        for a in range(na):
            for k, (qx, qy) in enumerate(chips):
                slot = outs[a].at[2 * qx + qy]
                pltpu.make_async_remote_copy(
                    src_ref=slot, dst_ref=slot, send_sem=send_sems.at[a, k], recv_sem=recv_sems.at[a, k],
                    device_id=(qx, qy, c), device_id_type=MESH).wait_recv()
        for cp in sent:
            cp.wait_send()
        for cp in mine:
            cp.wait()

    any_spec = pl.BlockSpec(memory_space=pl.ANY)
    return pl.pallas_call(
        body, name=name,
        out_shape=[jax.ShapeDtypeStruct(p.shape, p.dtype) for p in sums],
        in_specs=[any_spec] * na, out_specs=[any_spec] * na,
        scratch_shapes=[pltpu.SemaphoreType.DMA((na, 3)), pltpu.SemaphoreType.DMA((na, 3)), pltpu.SemaphoreType.DMA((na,))],
        compiler_params=_cparams(),
    )(*sums)


def _gather_row(name, vec, width):
    n = vec.shape[0]
    rows = jnp.pad(vec, (0, width - n)).reshape(SUBLANES_F32, width // SUBLANES_F32)
    return _allgather_small(name, rows).reshape(N_DEV, width)[:, :n]


def _layer_fwd(x, mod, w):
    sh1, sc1, g1, sh2, sc2, g2 = [mod[i:i + 1] for i in range(N_MOD)]
    S = x.shape[0]
    h1 = _normmod_fwd(x, w["norm1_w"], sc1, sh1)
    proj = _matmul("proj_in", h1, w["w_in"], "nn")
    a1, a_out = _conv_a_fwd(proj, w["conv_a_w"], w["conv_a_b"], w["ln_a_w"], w["ln_a_b"])
    qkv = _qkv_views(proj)
    os, ls = zip(*[_attn_fwd(v, dil) for v, dil in zip(qkv, DILATIONS)])
    att, lse, att_b = _attn_merge(os, ls)
    of, st_f = _hgrn_fwd(proj, w["lb_f"], ZF, False)
    ob, st_b = _hgrn_fwd(proj, w["lb_b"], ZB, True)
    rec = _hgrn_post_fwd(of, ob, proj, w["rec_norm_w"])
    mixed = jnp.concatenate([att_b, rec, a_out], axis=1)
    y1 = _matmul("proj_out", mixed, w["w_out"], "nn")
    x2, h2 = _gate_add_normmod(x, y1, g1, w["norm2_w"], sc2, sh2)
    u = _matmul("ffn_up", h2, w["w_up"], "nn")
    act = _ffn_mid_fwd(u, w["conv_f_w"])
    y2 = _matmul("ffn_down", act, w["w_down"], "nn")
    x3 = _gate_add(x2, y2, g2)
    saved = dict(x=x, h1=h1, proj=proj, a1=a1, qkv=qkv, att=att, lse=lse, of=of, ob=ob, st_f=st_f, st_b=st_b,
                 mixed=mixed, y1=y1, x2=x2, h2=h2, u=u, act=act, y2=y2)
    return x3, saved


def _layer_bwd(gx3, mod, w, s):
    sh1, sc1, g1, sh2, sc2, g2 = [mod[i:i + 1] for i in range(N_MOD)]
    S = gx3.shape[0]
    g = {}
    gy2, gg2 = _gate_bwd(gx3, s["y2"], g2)
    gact = _matmul("ffn_down_dx", gy2, w["w_down"], "nt")
    g["w_down"] = _matmul("ffn_down_dw", s["act"], gy2, "tn")
    gu, g["conv_f_w"] = _ffn_mid_bwd(s["u"], gact, w["conv_f_w"])
    gh2 = _matmul_nt_pieces("ffn_up_dx", gu, (w["w_up"][:, :D_FF], w["w_up"][:, D_FF:]))
    g["w_up"] = jnp.concatenate([_matmul("ffn_up_dw", s["h2"], t, "tn") for t in gu], axis=1)
    gx2, gy1, g["norm2_w"], gsc2, gsh2, gg1 = _normmod_gate_bwd(s["x2"], gh2, gx3, w["norm2_w"], sc2, sh2, s["y1"], g1)
    gmixed = _matmul("proj_out_dx", gy1, w["w_out"], "nt")
    g["w_out"] = _matmul("proj_out_dw", s["mixed"], gy1, "tn")
    go, ggr, g["rec_norm_w"] = _hgrn_post_bwd(s["of"], s["ob"], s["proj"], gmixed, w["rec_norm_w"])
    gq_f, gz_f, gv_f, g["lb_f"] = _hgrn_bwd(s["proj"], w["lb_f"], s["st_f"], go, ZF, False)
    gq_r, gz_b, gv_r, g["lb_b"] = _hgrn_bwd(s["proj"], w["lb_b"], s["st_b"], go, ZB, True, other=(gq_f, gv_f))
    dos, lds = _attn_bwd_prep(gmixed, s["att"], s["lse"])
    gqkv = zip(*[_attn_bwd(v, do, ld, dil) for v, do, ld, dil in zip(s["qkv"], dos, lds, DILATIONS)])
    gq_a, gk_a, gv_a = [_sum3_bf16(lst, S, ATT_W) for lst in gqkv]
    gav, gag, gcw, g["conv_a_b"], g["ln_a_w"], g["ln_a_b"] = _conv_a_bwd(
        s["proj"], s["a1"], gmixed, w["conv_a_w"], w["ln_a_w"], w["ln_a_b"])
    g["conv_a_w"] = gcw[:CONV_W]
    gproj = jnp.concatenate([gq_a, gk_a, gv_a, gq_r, gz_f, gz_b, gv_r, ggr, gav, gag,
                             jnp.zeros((S, IN_COLS_PAD - IN_COLS), BF16)], axis=1)
    gh1 = _matmul("proj_in_dx", gproj, w["w_in"], "nt")
    g["w_in"] = _matmul("proj_in_dw", s["h1"], gproj, "tn")
    gx, g["norm1_w"], gsc1, gsh1 = _normmod_bwd(s["x"], gh1, gx2, w["norm1_w"], sc1, sh1)
    gmod = jnp.concatenate([gsh1, gsc1, gg1, gsh2, gsc2, gg2], axis=0)
    return gx, gmod, g


def _permute_in_cols(t):
    pad = jnp.zeros(t.shape[:-1] + (IN_COLS_PAD - IN_COLS,), t.dtype)
    return jnp.concatenate([t[..., CONV_COLS:], t[..., :CONV_COLS], pad], axis=-1)


def _unpermute_in_cols(t):
    return jnp.concatenate([t[..., IN_COLS - CONV_COLS:IN_COLS], t[..., :IN_COLS - CONV_COLS]], axis=-1)


def _cols_from_gathered(t, lead):
    nd = t.ndim
    perm = tuple(range(1, nd - 1)) + (0, nd - 1)
    t = t.transpose(perm)
    return t.reshape(t.shape[:-2] + (t.shape[-2] * t.shape[-1],))


def _cols_to_parts(t):
    L, R, C = t.shape
    return t.reshape(L * R, N_DEV, C // N_DEV).transpose(1, 0, 2)


SMALL_REPL = (("norm1_w", D), ("conv_a_b", CONV_CH), ("ln_a_w", CONV_CH), ("ln_a_b", CONV_CH),
              ("rec_norm_w", REC_W), ("norm2_w", D))


def kernel(x, c, w_ada, b_ada, norm1_w, w_in, conv_a_w, conv_a_b, ln_a_w, ln_a_b, lb_gamma, rec_norm_w, w_out, norm2_w, w_up, conv_f_w, w_down, final_norm_w, loss_target, m_w_ada, m_b_ada, m_norm1_w, m_w_in, m_conv_a_w, m_conv_a_b, m_ln_a_w, m_ln_a_b, m_lb_gamma, m_rec_norm_w, m_w_out, m_norm2_w, m_w_up, m_conv_f_w, m_w_down, m_final_norm_w, v_w_ada, v_b_ada, v_norm1_w, v_w_in, v_conv_a_w, v_conv_a_b, v_ln_a_w, v_ln_a_b, v_lb_gamma, v_rec_norm_w, v_w_out, v_norm2_w, v_w_up, v_conv_f_w, v_w_down, v_final_norm_w):
    px, py, pc = _place()
    me = _index_of((px, py, pc))
    xs, tgt = x[0], loss_target[0]
    S = xs.shape[0]
    ada_cols = w_ada.shape[2]

    big = [w_in.reshape(DEPTH * D, -1), w_up.reshape(DEPTH * D, -1), w_out.reshape(-1, D), w_down.reshape(-1, D)]
    g_in, g_up, g_out, g_down = _allgather_big("gather_weights", [t.astype(BF16) for t in big])
    w_in_f = _permute_in_cols(_cols_from_gathered(g_in.reshape(N_DEV, DEPTH, D, -1), 1))
    w_up_f = _cols_from_gathered(g_up.reshape(N_DEV, DEPTH, D, -1), 1)
    w_out_f = g_out.reshape(N_DEV, DEPTH, D // N_DEV, D).transpose(1, 0, 2, 3).reshape(DEPTH, D, D)
    w_out_f = jnp.concatenate([w_out_f[:, CONV_CH:], w_out_f[:, :CONV_CH]], axis=1)
    w_down_f = g_down.reshape(N_DEV, DEPTH, D_FF // N_DEV, D).transpose(1, 0, 2, 3).reshape(DEPTH, D_FF, D)

    small_in = jnp.concatenate([c.reshape(-1), conv_a_w.reshape(-1), lb_gamma.reshape(-1), conv_f_w.reshape(-1)])
    gs = _gather_row("gather_small", small_in, 8192)
    o1 = D
    o2 = o1 + conv_a_w.size
    o3 = o2 + lb_gamma.size
    c_all = gs[:, :o1]
    conv_a_f = _cols_from_gathered(gs[:, o1:o2].reshape(N_DEV, DEPTH, CONV_W, -1), 1)
    lb_gamma_f = _cols_from_gathered(gs[:, o2:o3].reshape(N_DEV, DEPTH, 2, -1), 1)
    conv_f_f = _cols_from_gathered(gs[:, o3:].reshape(N_DEV, DEPTH, FFN_CONV_W, -1), 1)
    conv_a_pad = jnp.pad(conv_a_f, ((0, 0), (0, CONV_W_PAD - CONV_W), (0, 0)))

    b_loc = lax.dynamic_slice_in_dim(b_ada, me * ada_cols, ada_cols, axis=1)

    def mod_fn(c_all_, w_, b_):
        cond = c_all_ * jax.nn.sigmoid(c_all_)
        return (jnp.concatenate([_dot_nn(cond, w_[l], HP) + b_[l] for l in range(DEPTH)], axis=1),)

    (mod_part,) = _vmem_call("ada_mod", mod_fn, [c_all, w_ada, b_loc[:, None, :]], [((N_DEV, DEPTH * ada_cols), F32)])
    gm = _allgather_small("gather_mod", mod_part).reshape(N_DEV, N_DEV, DEPTH, ada_cols)
    mod = lax.dynamic_index_in_dim(gm, me, axis=1, keepdims=False)
    mod = mod.transpose(1, 0, 2).reshape(DEPTH, N_MOD, D)

    (lb1,) = _vmem_call("lower_bounds", lambda a, b: (_lower_bounds_f(a, b),), [lb_gamma_f[0], lb_gamma_f[1]], [((2, REC_W), F32)])
    lb4 = jnp.concatenate([jnp.zeros_like(lb1), lb1], axis=0)

    def layer_weights(l):
        row = lambda t: t[l].reshape(1, -1)
        return dict(norm1_w=row(norm1_w), w_in=w_in_f[l], conv_a_w=conv_a_pad[l], conv_a_b=row(conv_a_b), ln_a_w=row(ln_a_w),
                    ln_a_b=row(ln_a_b), lb_f=lb4[2 * l:2 * l + 1], lb_b=lb4[2 * l + 1:2 * l + 2], rec_norm_w=row(rec_norm_w),
                    w_out=w_out_f[l], norm2_w=row(norm2_w), w_up=w_up_f[l], conv_f_w=conv_f_f[l], w_down=w_down_f[l])

    ws = [layer_weights(l) for l in range(DEPTH)]
    h, saved = xs, []
    for l in range(DEPTH):
        h, s = _layer_fwd(h, mod[l], ws[l])
        saved.append(s)
    gh, g_final, loss_row = _loss_head(h, tgt, final_norm_w.reshape(1, D))
    loss = lax.psum(loss_row[0, 0], ("x", "y", "c"))
    gmods, gws = [None] * DEPTH, [None] * DEPTH
    for l in reversed(range(DEPTH)):
        gh, gmods[l], gws[l] = _layer_bwd(gh, mod[l], ws[l], saved[l])
    grad_x = gh[None]

    glb1 = jnp.concatenate([gws[1]["lb_f"], gws[1]["lb_b"]], axis=0)

    def lb_bwd_fn(a, b, g1):
        _, vjp = jax.vjp(_lower_bounds_f, a, b)
        return vjp(g1)

    g_lb_gamma = jnp.stack(_vmem_call("lower_bounds_bwd", lb_bwd_fn, [lb_gamma_f[0], lb_gamma_f[1], glb1], [((2, REC_W), F32)] * 2))
    pieces = [jnp.stack(gmods).reshape(-1)]
    for l in range(DEPTH):
        pieces += [gws[l][n].reshape(-1) for n, _ in SMALL_REPL]
    pieces += [g_final.reshape(-1)]
    pieces += [jnp.stack([gws[l]["conv_a_w"] for l in range(DEPTH)]).reshape(-1), g_lb_gamma.reshape(-1),
               jnp.stack([gws[l]["conv_f_w"] for l in range(DEPTH)]).reshape(-1)]
    small_g = jnp.concatenate(pieces)
    n_small = small_g.shape[0]
    gsm = _gather_row("gather_small_grads", small_g, 71680)
    n_mod = DEPTH * N_MOD * D
    gmod_all = gsm[:, :n_mod].reshape(N_DEV, DEPTH, N_MOD * D)
    gmod_loc = lax.dynamic_slice_in_dim(gmod_all, me * ada_cols, ada_cols, axis=2).transpose(1, 0, 2)

    def small_fn(gsm_, c_all_, gm_):
        cond = c_all_ * jax.nn.sigmoid(c_all_)
        gw = jnp.concatenate([_dot_tn(cond, gm_[l], HP) for l in range(DEPTH)], axis=0)
        return jnp.sum(gsm_, axis=0, keepdims=True), gw

    tot, g_w_ada = _vmem_call("small_grads", small_fn, [gsm, c_all, gmod_loc],
                              [((1, n_small), F32), ((DEPTH * D, ada_cols), F32)])
    tot = tot[0]
    grads = {"w_ada": g_w_ada.reshape(DEPTH, D, ada_cols), "b_ada": tot[:n_mod].reshape(DEPTH, N_MOD * D)}
    pos = n_mod
    per_layer = {n: [] for n, _ in SMALL_REPL}
    for l in range(DEPTH):
        for n, width in SMALL_REPL:
            per_layer[n].append(tot[pos:pos + width])
            pos += width
    for n, _ in SMALL_REPL:
        grads[n] = jnp.stack(per_layer[n])
    grads["final_norm_w"] = tot[pos:pos + D]
    pos += D
    n_ca, n_lb, n_cf = DEPTH * CONV_W * CONV_CH, DEPTH * 2 * REC_W, DEPTH * FFN_CONV_W * 2 * D_FF
    g_ca = tot[pos:pos + n_ca].reshape(DEPTH, CONV_W, CONV_CH)
    g_lb = tot[pos + n_ca:pos + n_ca + n_lb].reshape(DEPTH, 2, REC_W)
    g_cf = tot[pos + n_ca + n_lb:pos + n_ca + n_lb + n_cf].reshape(DEPTH, FFN_CONV_W, 2 * D_FF)
    grads["conv_a_w"] = lax.dynamic_slice_in_dim(g_ca, me * conv_a_w.shape[2], conv_a_w.shape[2], axis=2)
    grads["lb_gamma"] = lax.dynamic_slice_in_dim(g_lb, me * lb_gamma.shape[2], lb_gamma.shape[2], axis=2)
    grads["conv_f_w"] = lax.dynamic_slice_in_dim(g_cf, me * conv_f_w.shape[2], conv_f_w.shape[2], axis=2)

    gw_in = _unpermute_in_cols(jnp.stack([gws[l]["w_in"] for l in range(DEPTH)]))
    gw_up = jnp.stack([gws[l]["w_up"] for l in range(DEPTH)])
    gw_out = jnp.stack([gws[l]["w_out"] for l in range(DEPTH)])
    gw_out = jnp.concatenate([gw_out[:, D - CONV_CH:], gw_out[:, :D - CONV_CH]], axis=1)
    gw_down = jnp.stack([gws[l]["w_down"] for l in range(DEPTH)])
    rows_to_parts = lambda t: t.reshape(DEPTH, N_DEV, -1, D).transpose(1, 0, 2, 3).reshape(N_DEV, -1, D)
    parts = [_cols_to_parts(gw_in), _cols_to_parts(gw_up), rows_to_parts(gw_out), rows_to_parts(gw_down)]
    parts = [t.astype(BF16) for t in parts]
    from_sibling = _scatter_to_sibling("scatter_sibling", parts)
    sums = [_pair_sum("pair_sum", p, r) for p, r in zip(parts, from_sibling)]
    r_in, r_up, r_out, r_down = _scatter_to_chips("scatter_chips", sums)

    given = dict(w_ada=(w_ada, m_w_ada, v_w_ada), b_ada=(b_ada, m_b_ada, v_b_ada), norm1_w=(norm1_w, m_norm1_w, v_norm1_w),
                 w_in=(w_in, m_w_in, v_w_in), conv_a_w=(conv_a_w, m_conv_a_w, v_conv_a_w), conv_a_b=(conv_a_b, m_conv_a_b, v_conv_a_b),
                 ln_a_w=(ln_a_w, m_ln_a_w, v_ln_a_w), ln_a_b=(ln_a_b, m_ln_a_b, v_ln_a_b), lb_gamma=(lb_gamma, m_lb_gamma, v_lb_gamma),
                 rec_norm_w=(rec_norm_w, m_rec_norm_w, v_rec_norm_w), w_out=(w_out, m_w_out, v_w_out),
                 norm2_w=(norm2_w, m_norm2_w, v_norm2_w), w_up=(w_up, m_w_up, v_w_up), conv_f_w=(conv_f_w, m_conv_f_w, v_conv_f_w),
                 w_down=(w_down, m_w_down, v_w_down), final_norm_w=(final_norm_w, m_final_norm_w, v_final_norm_w))
    big_parts = dict(w_in=r_in, w_up=r_up, w_out=r_out, w_down=r_down)
    names = list(given)
    res = {}
    for n in names:
        w_, m_, v_ = given[n]
        shape = w_.shape
        C = shape[-1]
        two_d = lambda t: t.reshape(-1, C)
        gp = big_parts[n] if n in big_parts else two_d(grads[n])[None]
        res[n] = [t.reshape(shape) for t in _adamw("adamw_" + n, two_d(w_), two_d(m_), two_d(v_), gp)]
    return (loss, grad_x, *[res[n][0] for n in names], *[res[n][1] for n in names],
            *[res[n][2] for n in names], *[res[n][3] for n in names])
```

```python
import functools

import jax
import jax.numpy as jnp
from jax import lax
from jax.experimental import pallas as pl
from jax.experimental.pallas import tpu as pltpu

F32 = jnp.float32
BF16 = jnp.bfloat16
HP = lax.Precision.HIGHEST
MESH = pl.DeviceIdType.MESH

N_DEV = 8
D = 1024
DEPTH = 2
CONV_CH = 256
CONV_W = 31
CONV_W_PAD = 32
ATT_W = 384
REC_W = 384
N_HEADS = 6
HEAD = 64
HEAD_SHIFT = 6
HALF_BAND = 64
ATT_BLK = 512
ATT_SUB = 128
DILATIONS = (1, 4, 16)
ALIBI_SLOPES = tuple(float(2.0 ** (-8.0 * (h + 1) / N_HEADS)) for h in range(N_HEADS))
MASK_VALUE = -1e30
REC_CHUNK = 64
EXP_CLAMP = 80.0
F_TINY = 1e-30
IN_COLS = 3584
IN_COLS_PAD = IN_COLS
QKV_BLOCKS = 3
D_FF = 2816
FFN_CONV_W = 3
N_MOD = 6
EPS = 1e-6
ADAM_LR, ADAM_B1, ADAM_B2, ADAM_EPS, ADAM_WD, ADAM_STEP = 0.001, 0.9, 0.999, 1e-08, 0.01, 10

VMEM_LIMIT_BYTES = 56 * 1024 * 1024
SUBLANES_F32 = 8
LANES = 128

QA, KA, VA, QR, ZF, ZB, IR, GR = range(8)
AV_BLK, AG_BLK = 12, 13
CONV_COLS = 2 * CONV_CH


def _cparams(sem=None):
    kw = dict(vmem_limit_bytes=VMEM_LIMIT_BYTES)
    if sem is not None:
        kw["dimension_semantics"] = sem
    return pltpu.CompilerParams(**kw)


def _iota(shape, dim):
    return lax.broadcasted_iota(jnp.int32, shape, dim)


def _dot(a, b, dims, precision=None):
    return lax.dot_general(a, b, (dims, ((), ())), precision=precision, preferred_element_type=F32)


def _dot_nn(a, b, precision=None):
    return _dot(a, b, ((1,), (0,)), precision)


def _dot_nt(a, b, precision=None):
    return _dot(a, b, ((1,), (1,)), precision)


def _dot_tn(a, b, precision=None):
    return _dot(a, b, ((0,), (0,)), precision)


def _c0(j):
    return 0


def _pick(n, cands):
    for c in cands:
        if n % c == 0:
            return c
    return n


MATMUL_OUT_TILE_BYTES = 8 * 1024 * 1024


def _div_lanes(n, cap):
    best = None
    for d in range(LANES, min(n, cap) + 1, LANES):
        if n % d == 0:
            best = d
    return best if best is not None else n


def _matmul_tiles(mode, M, N, K):
    if mode == "nn":
        tm = _pick(M, (1024, 512, 256, 128))
        return tm, _div_lanes(N, MATMUL_OUT_TILE_BYTES // (4 * tm)), K
    if mode == "nt":
        return _pick(M, (512, 256, 128)), N, K
    tm = _div_lanes(M, 1408)
    return tm, _div_lanes(N, MATMUL_OUT_TILE_BYTES // (4 * tm)), _pick(K, (1024, 512, 256))


def _matmul(name, a, b, mode, out_dtype=F32):
    if mode == "nn":
        (M, K), (_, N) = a.shape, b.shape
    elif mode == "nt":
        (M, K), (N, _) = a.shape, b.shape
    else:
        (K, M), (_, N) = a.shape, b.shape
    tm, tn, tk = _matmul_tiles(mode, M, N, K)
    nk = K // tk
    if mode == "nn":
        a_spec = pl.BlockSpec((tm, tk), lambda i, j, k: (i, k))
        b_spec = pl.BlockSpec((tk, tn), lambda i, j, k: (k, j))
        dims = ((1,), (0,))
    elif mode == "nt":
        a_spec = pl.BlockSpec((tm, tk), lambda i, j, k: (i, k))
        b_spec = pl.BlockSpec((tn, tk), lambda i, j, k: (j, k))
        dims = ((1,), (1,))
    else:
        a_spec = pl.BlockSpec((tk, tm), lambda i, j, k: (k, i))
        b_spec = pl.BlockSpec((tk, tn), lambda i, j, k: (k, j))
        dims = ((0,), (0,))

    def body_whole(a_ref, b_ref, o_ref):
        o_ref[...] = _dot(a_ref[...].astype(BF16), b_ref[...].astype(BF16), dims).astype(o_ref.dtype)

    def body(a_ref, b_ref, o_ref, acc_ref):
        k = pl.program_id(2)
        part = _dot(a_ref[...].astype(BF16), b_ref[...].astype(BF16), dims)

        @pl.when(k == 0)
        def _():
            acc_ref[...] = part

        @pl.when(k > 0)
        def _():
            acc_ref[...] += part

        @pl.when(k == nk - 1)
        def _():
            o_ref[...] = acc_ref[...].astype(o_ref.dtype)

    return pl.pallas_call(
        body_whole if nk == 1 else body, name=name, grid=(M // tm, N // tn, nk),
        in_specs=[a_spec, b_spec],
        out_specs=pl.BlockSpec((tm, tn), lambda i, j, k: (i, j)),
        out_shape=jax.ShapeDtypeStruct((M, N), out_dtype),
        scratch_shapes=[] if nk == 1 else [pltpu.VMEM((tm, tn), F32)],
        compiler_params=_cparams(("parallel", "parallel", "arbitrary")),
    )(a, b)


def _matmul_nt_pieces(name, a_pieces, b_pieces):
    M, N = a_pieces[0].shape[0], b_pieces[0].shape[0]
    tm = _pick(M, (512, 256, 128))
    n = len(a_pieces)

    def body(*refs):
        acc = _dot(refs[0][...].astype(BF16), refs[n][...].astype(BF16), ((1,), (1,)))
        for p in range(1, n):
            acc = acc + _dot(refs[p][...].astype(BF16), refs[n + p][...].astype(BF16), ((1,), (1,)))
        refs[2 * n][...] = acc

    in_specs = ([pl.BlockSpec((tm, a.shape[1]), lambda i: (i, 0)) for a in a_pieces]
                + [pl.BlockSpec(b.shape, lambda i: (0, 0)) for b in b_pieces])
    return pl.pallas_call(
        body, name=name, grid=(M // tm,), in_specs=in_specs,
        out_specs=pl.BlockSpec((tm, N), lambda i: (i, 0)), out_shape=jax.ShapeDtypeStruct((M, N), F32),
        compiler_params=_cparams(("parallel",)),
    )(*a_pieces, *b_pieces)


def _rowwise(name, fn, S, ts, tiles, params=(), outs=(), accs=(), halo=0, ncb=1):
    in_specs, args, scratch = [], [], []
    for arr, w, jm, with_halo in tiles:
        if isinstance(with_halo, int) and with_halo > 1:
            d = with_halo
            in_specs.append(pl.BlockSpec((ts // d, d * w), lambda j, i: (i, 0)))
            args.append(arr)
            scratch.append(pltpu.VMEM((w // LANES, ts, LANES), F32))
        elif with_halo:
            hb, nhb = ts // halo, S // halo
            in_specs += [
                pl.BlockSpec((halo, w), lambda j, i, jm=jm, hb=hb: (jnp.maximum(i * hb - 1, 0), jm(j))),
                pl.BlockSpec((ts, w), lambda j, i, jm=jm: (i, jm(j))),
                pl.BlockSpec((halo, w), lambda j, i, jm=jm, hb=hb, nhb=nhb: (jnp.minimum((i + 1) * hb, nhb - 1), jm(j))),
            ]
            args += [arr, arr, arr]
        else:
            in_specs.append(pl.BlockSpec((ts, w), lambda j, i, jm=jm: (i, jm(j))))
            args.append(arr)
    for arr, r, w, jm in params:
        in_specs.append(pl.BlockSpec((r, w), lambda j, i, jm=jm: (0, jm(j))))
        args.append(arr)
    out_specs, out_shape = [], []
    for w, dt, jm, tw, *dil in outs:
        if dil:
            out_specs.append(pl.BlockSpec((ts // dil[0], dil[0] * w), lambda j, i: (i, 0)))
            out_shape.append(jax.ShapeDtypeStruct((S // dil[0], dil[0] * w), dt))
            scratch += [pltpu.VMEM((ts, w), F32), pltpu.VMEM((w // LANES, ts, LANES), F32)]
        else:
            out_specs.append(pl.BlockSpec((ts, w), lambda j, i, jm=jm: (i, jm(j))))
            out_shape.append(jax.ShapeDtypeStruct((S, tw), dt))
    for r, w, jm, tw in accs:
        out_specs.append(pl.BlockSpec((r, w), lambda j, i, jm=jm: (0, jm(j))))
        out_shape.append(jax.ShapeDtypeStruct((r, tw), F32))
    n_tiles, n_params, n_outs, n_accs = len(tiles), len(params), len(outs), len(accs)

    def residue_rows(r, d):
        return pl.ds(r, ts // d, stride=d)

    def body(*refs):
        i = pl.program_id(1)
        n_io = len(in_specs) + n_outs + n_accs
        scr = list(refs[n_io:])
        refs = refs[:n_io]
        pos, vals = 0, []
        for _, w, _, with_halo in tiles:
            if isinstance(with_halo, int) and with_halo > 1:
                d, buf = with_halo, scr.pop(0)
                for r in range(d):
                    for c in range(w // LANES):
                        buf[c, residue_rows(r, d), :] = refs[pos][:, r * w + c * LANES:r * w + (c + 1) * LANES].astype(F32)
                vals.append(jnp.concatenate([buf[c] for c in range(w // LANES)], axis=1))
                pos += 1
            elif with_halo:
                before, after = refs[pos][...], refs[pos + 2][...]
                before = jnp.where(i > 0, before, jnp.zeros_like(before))
                after = jnp.where(i < S // ts - 1, after, jnp.zeros_like(after))
                vals.append(jnp.concatenate([before, refs[pos + 1][...], after], axis=0))
                pos += 3
            else:
                vals.append(refs[pos][...])
                pos += 1
        prefs = refs[pos:pos + n_params]
        orefs = list(refs[pos + n_params:pos + n_params + n_outs])
        arefs = refs[pos + n_params + n_outs:]
        staged = []
        for k, (w, _, _, _, *dil) in enumerate(outs):
            if dil:
                staged.append((orefs[k], scr.pop(0), scr.pop(0), w, dil[0]))
                orefs[k] = staged[-1][1]

        @pl.when(i == 0)
        def _():
            for r in arefs:
                r[...] = jnp.zeros_like(r)

        fn(i, vals, prefs, orefs, arefs)
        for out_ref, flat, buf, w, d in staged:
            for c in range(w // LANES):
                buf[c] = flat[:, c * LANES:(c + 1) * LANES]
                for r in range(d):
                    out_ref[:, r * w + c * LANES:r * w + (c + 1) * LANES] = buf[c, residue_rows(r, d), :].astype(out_ref.dtype)

    res = pl.pallas_call(
        body, name=name, grid=(ncb, S // ts),
        in_specs=in_specs, out_specs=out_specs, out_shape=out_shape, scratch_shapes=scratch,
        compiler_params=_cparams(("arbitrary", "arbitrary")),
    )(*args)
    return res


def _vmem_call(name, fn, ins, out_shapes):
    n_in = len(ins)

    def body(*refs):
        vals = fn(*[r[...] for r in refs[:n_in]])
        for r, v in zip(refs[n_in:], vals):
            r[...] = v.astype(r.dtype)

    return pl.pallas_call(
        body, name=name,
        out_shape=[jax.ShapeDtypeStruct(s, dt) for s, dt in out_shapes],
        compiler_params=_cparams(),
    )(*ins)


def _rms(x, w):
    return x * lax.rsqrt(jnp.mean(x * x, axis=-1, keepdims=True) + EPS) * w


def _normmod_f(x, nw, sc, sh):
    return _rms(x, nw) * (1.0 + sc) + sh


def _row_params(*vecs):
    return [(v, 1, v.shape[1], _c0) for v in vecs]


def _normmod_fwd(x, nw, sc, sh):
    S = x.shape[0]

    def fn(i, vals, p, o, a):
        o[0][...] = _normmod_f(vals[0], p[0][...], p[1][...], p[2][...]).astype(BF16)

    return _rowwise("normmod_fwd", fn, S, 512, [(x, D, _c0, False)], _row_params(nw, sc, sh), [(D, BF16, _c0, D)])[0]


def _normmod_bwd(x, gh, gres, nw, sc, sh):
    S = x.shape[0]

    def fn(i, vals, p, o, a):
        _, vjp = jax.vjp(_normmod_f, vals[0], p[0][...], p[1][...], p[2][...])
        gx, gnw, gsc, gsh = vjp(vals[1])
        o[0][...] = gx + vals[2]
        a[0][...] += gnw
        a[1][...] += gsc
        a[2][...] += gsh

    return _rowwise("normmod_bwd", fn, S, 512, [(x, D, _c0, False), (gh, D, _c0, False), (gres, D, _c0, False)],
                    _row_params(nw, sc, sh), [(D, F32, _c0, D)], [(1, D, _c0, D)] * 3)


def _gate_add(x, y, g):
    S = x.shape[0]

    def fn(i, vals, p, o, a):
        o[0][...] = vals[0] + p[0][...] * vals[1]

    return _rowwise("gate_add", fn, S, 512, [(x, D, _c0, False), (y, D, _c0, False)], _row_params(g), [(D, F32, _c0, D)])[0]


def _gate_bwd(gx, y, g):
    S = gx.shape[0]

    def fn(i, vals, p, o, a):
        o[0][...] = (vals[0] * p[0][...]).astype(BF16)
        a[0][...] += jnp.sum(vals[0] * vals[1], axis=0, keepdims=True)

    return _rowwise("gate_bwd", fn, S, 512, [(gx, D, _c0, False), (y, D, _c0, False)], _row_params(g),
                    [(D, BF16, _c0, D)], [(1, D, _c0, D)])


def _gate_add_normmod(x, y, g, nw, sc, sh):
    S = x.shape[0]

    def fn(i, vals, p, o, a):
        x2 = vals[0] + p[0][...] * vals[1]
        o[0][...] = x2
        o[1][...] = _normmod_f(x2, p[1][...], p[2][...], p[3][...]).astype(BF16)

    return _rowwise("gate_add_normmod", fn, S, 512, [(x, D, _c0, False), (y, D, _c0, False)], _row_params(g, nw, sc, sh),
                    [(D, F32, _c0, D), (D, BF16, _c0, D)])


def _normmod_gate_bwd(x, gh, gres, nw, sc, sh, y, g):
    S = x.shape[0]

    def fn(i, vals, p, o, a):
        _, vjp = jax.vjp(_normmod_f, vals[0], p[0][...], p[1][...], p[2][...])
        gx, gnw, gsc, gsh = vjp(vals[1])
        gx = gx + vals[2]
        o[0][...] = gx
        o[1][...] = (gx * p[3][...]).astype(BF16)
        a[0][...] += gnw
        a[1][...] += gsc
        a[2][...] += gsh
        a[3][...] += jnp.sum(gx * vals[3], axis=0, keepdims=True)

    tiles = [(t, D, _c0, False) for t in (x, gh, gres, y)]
    return _rowwise("normmod_gate_bwd", fn, S, 512, tiles, _row_params(nw, sc, sh, g),
                    [(D, F32, _c0, D), (D, BF16, _c0, D)], [(1, D, _c0, D)] * 4)


def _loss_head(x, tgt, fw):
    S = x.shape[0]

    def fn(i, vals, p, o, a):
        y, vjp = jax.vjp(_rms, vals[0], p[0][...])
        err = y - vals[1]
        gx, gfw = vjp(err * (1.0 / D))
        o[0][...] = gx
        a[0][...] += gfw
        part = 0.5 * jnp.sum(jnp.mean(err * err, axis=-1, keepdims=True), axis=0, keepdims=True)
        a[1][...] += jnp.broadcast_to(part, (1, LANES))

    return _rowwise("loss_head", fn, S, 256, [(x, D, _c0, False), (tgt, D, _c0, False)], _row_params(fw),
                    [(D, F32, _c0, D)], [(1, D, _c0, D), (1, LANES, _c0, LANES)])


CONV_HALO = 16
CONV_TS = 512


def _shifted(ext, shift, ts, halo):
    n = ext.shape[0]
    s = shift % n
    r = ext if s == 0 else pltpu.roll(ext, s, 0)
    return r[halo:halo + ts]


def _ln_silu(a, w, b):
    mu = jnp.mean(a, axis=-1, keepdims=True)
    var = jnp.mean(jnp.square(a - mu), axis=-1, keepdims=True)
    y = (a - mu) * lax.rsqrt(var + EPS) * w + b
    return y * jax.nn.sigmoid(y)


def _conv_a_fwd(proj, w_pad, b, lnw, lnb):
    S = proj.shape[0]
    ts, H = min(CONV_TS, S), CONV_HALO

    def fn(i, vals, p, o, a):
        a0 = vals[0] * jax.nn.sigmoid(vals[1])
        acc = jnp.zeros((ts, CONV_CH), F32) + p[1][...]
        for k in range(CONV_W):
            acc = acc + _shifted(a0, CONV_W // 2 - k, ts, H) * p[0][pl.ds(k, 1), :]
        o[0][...] = acc
        o[1][...] = _ln_silu(acc, p[2][...], p[3][...]).astype(BF16)

    tiles = [(proj, CONV_CH, lambda j: AV_BLK, True), (proj, CONV_CH, lambda j: AG_BLK, True)]
    params = [(w_pad, CONV_W_PAD, CONV_CH, _c0)] + _row_params(b, lnw, lnb)
    return _rowwise("conv_a_fwd", fn, S, ts, tiles, params, [(CONV_CH, F32, _c0, CONV_CH), (CONV_CH, BF16, _c0, CONV_CH)], halo=H)


def _conv_a_bwd(proj, a1, gmixed, w_pad, lnw, lnb):
    S = proj.shape[0]
    ts, H = min(CONV_TS, S), CONV_HALO

    def fn(i, vals, p, o, a):
        av, ag, a1e, ge = vals
        lw, lb = p[1][...], p[2][...]
        _, vjp_e = jax.vjp(lambda t: _ln_silu(t, lw, lb), a1e)
        (ga1e,) = vjp_e(ge)
        c = slice(H, H + ts)
        _, vjp_c = jax.vjp(_ln_silu, a1e[c], lw, lb)
        ga1, glw, glb = vjp_c(ge[c])
        a[1][...] += jnp.sum(ga1, axis=0, keepdims=True)
        a[2][...] += glw
        a[3][...] += glb
        sg = jax.nn.sigmoid(ag)
        a0 = av * sg
        ga0 = jnp.zeros((ts, CONV_CH), F32)
        for k in range(CONV_W):
            a[0][pl.ds(k, 1), :] += jnp.sum(ga1 * _shifted(a0, CONV_W // 2 - k, ts, H), axis=0, keepdims=True)
            ga0 = ga0 + _shifted(ga1e, k - CONV_W // 2, ts, H) * p[0][pl.ds(k, 1), :]
        sgc, avc = sg[c], av[c]
        o[0][...] = (ga0 * sgc).astype(BF16)
        o[1][...] = (ga0 * avc * sgc * (1.0 - sgc)).astype(BF16)

    tiles = [(proj, CONV_CH, lambda j: AV_BLK, True), (proj, CONV_CH, lambda j: AG_BLK, True),
             (a1, CONV_CH, _c0, True), (gmixed, CONV_CH, lambda j: 3, True)]
    params = [(w_pad, CONV_W_PAD, CONV_CH, _c0)] + _row_params(lnw, lnb)
    outs = [(CONV_CH, BF16, _c0, CONV_CH), (CONV_CH, BF16, _c0, CONV_CH)]
    accs = [(CONV_W_PAD, CONV_CH, _c0, CONV_CH)] + [(1, CONV_CH, _c0, CONV_CH)] * 3
    return _rowwise("conv_a_bwd", fn, S, ts, tiles, params, outs, accs, halo=H)


FFN_HALO = 8
FFN_TS = 512
FFN_TS_FWD = 1024
FFN_CB = 256
FFN_NCB = D_FF // FFN_CB


def _gelu_mul(g, v):
    return 0.5 * g * (1.0 + lax.erf(g * (2.0 ** -0.5))) * v


def _ffn_mid_fwd(u, cw):
    S = u.shape[0]
    ts, H = min(FFN_TS_FWD, S), FFN_HALO

    def conv(ext, w_ref):
        acc = jnp.zeros((ts, FFN_CB), F32)
        for k in range(FFN_CONV_W):
            acc = acc + _shifted(ext, 1 - k, ts, H) * w_ref[pl.ds(k, 1), :]
        return acc

    def fn(i, vals, p, o, a):
        o[0][...] = _gelu_mul(conv(vals[0], p[0]), conv(vals[1], p[1])).astype(BF16)

    gate, val = (lambda j: j), (lambda j: j + FFN_NCB)
    return _rowwise("ffn_mid_fwd", fn, S, ts, [(u, FFN_CB, gate, True), (u, FFN_CB, val, True)],
                    [(cw, FFN_CONV_W, FFN_CB, gate), (cw, FFN_CONV_W, FFN_CB, val)],
                    [(FFN_CB, BF16, gate, D_FF)], halo=H, ncb=FFN_NCB)[0]


def _ffn_mid_bwd(u, gact, cw):
    S = u.shape[0]
    ts, H = min(FFN_TS, S), FFN_HALO
    n = ts + 2 * H

    def fn(i, vals, p, o, a):
        ug, uv, ga = vals

        def conv_all(ue, w_ref):
            acc = jnp.zeros((n, FFN_CB), F32)
            for k in range(FFN_CONV_W):
                s = (1 - k) % n
                acc = acc + (ue if s == 0 else pltpu.roll(ue, s, 0)) * w_ref[pl.ds(k, 1), :]
            return acc

        _, vjp = jax.vjp(_gelu_mul, conv_all(ug, p[0]), conv_all(uv, p[1]))
        for half, (gc, ue) in enumerate(zip(vjp(ga), (ug, uv))):
            gu = jnp.zeros((ts, FFN_CB), F32)
            for k in range(FFN_CONV_W):
                gu = gu + _shifted(gc, k - 1, ts, H) * p[half][pl.ds(k, 1), :]
                a[half][pl.ds(k, 1), :] += jnp.sum(gc[H:H + ts] * _shifted(ue, 1 - k, ts, H), axis=0, keepdims=True)
            o[half][...] = gu.astype(BF16)

    gate, val = (lambda j: j), (lambda j: j + FFN_NCB)
    tiles = [(u, FFN_CB, gate, True), (u, FFN_CB, val, True), (gact, FFN_CB, gate, True)]
    params = [(cw, FFN_CONV_W, FFN_CB, gate), (cw, FFN_CONV_W, FFN_CB, val)]
    gu_gate, gu_val, gw_gate, gw_val = _rowwise("ffn_mid_bwd", fn, S, ts, tiles, params, [(FFN_CB, BF16, gate, D_FF)] * 2,
                                                [(FFN_CONV_W, FFN_CB, gate, D_FF)] * 2, halo=H, ncb=FFN_NCB)
    return (gu_gate, gu_val), jnp.concatenate([gw_gate, gw_val], axis=1)


LD_W = LANES
PAIR_W = 2 * HEAD
N_PAIRS = N_HEADS // 2


def _sub_view(t, d):
    S, C = t.shape
    return t.reshape(S // d, d * C)


def _sub_halo_specs(width, col, blk, hb, nhb):
    per = blk // hb
    return [
        pl.BlockSpec((hb, width), lambda r, i: (jnp.maximum(i * per - 1, 0), col(r))),
        pl.BlockSpec((blk, width), lambda r, i: (i, col(r))),
        pl.BlockSpec((hb, width), lambda r, i: (jnp.minimum((i + 1) * per, nhb - 1), col(r))),
    ]


def _pick_lane(t, lane):
    return jnp.sum(jnp.where(_iota((1, t.shape[1]), 1) == lane, t, 0.0), axis=1, keepdims=True)


def _pair_mask(h2):
    return (_iota((1, PAIR_W), 1) >> HEAD_SHIFT) == h2


def _cat_bf16(a, b, c):
    return jnp.concatenate([a[...], b[...], c[...]], axis=0).astype(BF16)


def _attn_fwd(view, dil):
    L = view.shape[0]
    blk, hb = min(ATT_BLK, L), HALF_BAND
    sub = min(ATT_SUB, blk)
    span = sub + 2 * hb

    def body(q_ref, kp, kc, kn, vp, vc, vn, o_ref, l_ref):
        i = pl.program_id(1)
        rel = _iota((sub, span), 1) - hb - _iota((sub, span), 0)
        band = jnp.abs(rel) <= hb
        dist = jnp.abs(rel).astype(F32) * float(dil)
        q_all, k_all, v_all = q_ref[...].astype(BF16), _cat_bf16(kp, kc, kn), _cat_bf16(vp, vc, vn)
        for r0 in range(0, blk, sub):
            kpos = i * blk + r0 - hb + _iota((sub, span), 1)
            valid = band & (kpos >= 0) & (kpos < L)
            q, k, v = q_all[r0:r0 + sub], k_all[r0:r0 + span], v_all[r0:r0 + span]
            lse = jnp.zeros((sub, LD_W), F32)
            for pr in range(N_PAIRS):
                sl = slice(pr * PAIR_W, (pr + 1) * PAIR_W)
                qp, kpair, vpair = q[:, sl], k[:, sl], v[:, sl]
                o = jnp.zeros((sub, PAIR_W), F32)
                for h2 in range(2):
                    h, mask = 2 * pr + h2, _pair_mask(h2)
                    s = _dot_nt(jnp.where(mask, qp, jnp.zeros_like(qp)), kpair) * (HEAD ** -0.5) - ALIBI_SLOPES[h] * dist
                    s = jnp.where(valid, s, MASK_VALUE)
                    m = jnp.max(s, axis=1, keepdims=True)
                    p = jnp.exp(s - m)
                    l = jnp.sum(p, axis=1, keepdims=True)
                    o = jnp.where(mask, _dot_nn(p.astype(BF16), vpair) / l, o)
                    lse = lse + jnp.where(_iota((1, LD_W), 1) == h, m + jnp.log(l), 0.0)
                o_ref[r0:r0 + sub, sl] = o
            l_ref[r0:r0 + sub, :] = lse

    nhb = L // hb
    in_specs = ([pl.BlockSpec((blk, ATT_W), lambda r, i: (i, r * QKV_BLOCKS +QA))]
                + _sub_halo_specs(ATT_W, lambda r: r * QKV_BLOCKS +KA, blk, hb, nhb)
                + _sub_halo_specs(ATT_W, lambda r: r * QKV_BLOCKS +VA, blk, hb, nhb))
    return pl.pallas_call(
        body, name=f"attn_fwd_d{dil}", grid=(dil, L // blk), in_specs=in_specs,
        out_specs=[pl.BlockSpec((blk, ATT_W), lambda r, i: (i, r)), pl.BlockSpec((blk, LD_W), lambda r, i: (i, r))],
        out_shape=[jax.ShapeDtypeStruct((L, dil * ATT_W), F32), jax.ShapeDtypeStruct((L, dil * LD_W), F32)],
        compiler_params=_cparams(("parallel", "parallel")),
    )(*([view] * 7))


def _attn_bwd(pview, gview, lview, dil):
    L = pview.shape[0]
    blk, hb = min(ATT_BLK, L), HALF_BAND
    sub = min(ATT_SUB, blk)
    span = sub + 2 * hb
    scale = HEAD ** -0.5

    def body(qp, qc, qn, kp, kc, kn, vp, vc, vn, gp, gc, gn, lp, lc, ln, dq_ref, dk_ref, dv_ref):
        i = pl.program_id(1)
        le_all = jnp.concatenate([lp[...], lc[...], ln[...]], axis=0)
        rel_q = _iota((sub, span), 1) - hb - _iota((sub, span), 0)
        band_q = jnp.abs(rel_q) <= hb
        dist_q = jnp.abs(rel_q).astype(F32) * float(dil)
        rel_k = _iota((span, sub), 1) + hb - _iota((span, sub), 0)
        band_k = jnp.abs(rel_k) <= hb
        dist_k = jnp.abs(rel_k).astype(F32) * float(dil)
        qe_all, ke_all, ve_all = _cat_bf16(qp, qc, qn), _cat_bf16(kp, kc, kn), _cat_bf16(vp, vc, vn)
        ge_all = _cat_bf16(gp, gc, gn)
        for r0 in range(0, blk, sub):
            kpos = i * blk + r0 - hb + _iota((sub, span), 1)
            valid_q = band_q & (kpos >= 0) & (kpos < L)
            qpos = i * blk + r0 - hb + _iota((span, sub), 0)
            valid_k = band_k & (qpos >= 0) & (qpos < L)
            ext, mid = slice(r0, r0 + span), slice(r0 + hb, r0 + hb + sub)
            l, le = le_all[mid], le_all[ext]
            for pr in range(N_PAIRS):
                sl = slice(pr * PAIR_W, (pr + 1) * PAIR_W)
                q, k, v, g = qe_all[mid, sl], ke_all[mid, sl], ve_all[mid, sl], ge_all[mid, sl]
                qe, ke, ve, ge = qe_all[ext, sl], ke_all[ext, sl], ve_all[ext, sl], ge_all[ext, sl]
                dq = jnp.zeros((sub, PAIR_W), F32)
                dk = jnp.zeros((sub, PAIR_W), F32)
                dv = jnp.zeros((sub, PAIR_W), F32)
                for h2 in range(2):
                    h, mask = 2 * pr + h2, _pair_mask(h2)
                    only = lambda t: jnp.where(mask, t, jnp.zeros_like(t))
                    s = _dot_nt(only(q), ke) * scale - ALIBI_SLOPES[h] * dist_q
                    p = jnp.where(valid_q, jnp.exp(s - _pick_lane(l, h)), 0.0)
                    ds = p * (_dot_nt(only(g), ve) - _pick_lane(l, 8 + h))
                    dq = jnp.where(mask, _dot_nn(ds.astype(BF16), ke), dq)
                    s = _dot_nt(only(qe), k) * scale - ALIBI_SLOPES[h] * dist_k
                    p = jnp.where(valid_k, jnp.exp(s - _pick_lane(le, h)), 0.0)
                    dv = jnp.where(mask, _dot_tn(p.astype(BF16), ge), dv)
                    ds = p * (_dot_nt(only(ge), v) - _pick_lane(le, 8 + h))
                    dk = jnp.where(mask, _dot_tn(ds.astype(BF16), qe), dk)
                dq_ref[r0:r0 + sub, sl] = (dq * scale).astype(BF16)
                dk_ref[r0:r0 + sub, sl] = (dk * scale).astype(BF16)
                dv_ref[r0:r0 + sub, sl] = dv.astype(BF16)

    nhb = L // hb
    in_specs = (_sub_halo_specs(ATT_W, lambda r: r * QKV_BLOCKS +QA, blk, hb, nhb)
                + _sub_halo_specs(ATT_W, lambda r: r * QKV_BLOCKS +KA, blk, hb, nhb)
                + _sub_halo_specs(ATT_W, lambda r: r * QKV_BLOCKS +VA, blk, hb, nhb)
                + _sub_halo_specs(ATT_W, lambda r: r, blk, hb, nhb) + _sub_halo_specs(LD_W, lambda r: r, blk, hb, nhb))
    o_spec = pl.BlockSpec((blk, ATT_W), lambda r, i: (i, r))
    return pl.pallas_call(
        body, name=f"attn_bwd_d{dil}", grid=(dil, L // blk), in_specs=in_specs,
        out_specs=[o_spec] * 3, out_shape=[jax.ShapeDtypeStruct((L, dil * ATT_W), BF16)] * 3,
        compiler_params=_cparams(("parallel", "parallel")),
    )(*([pview] * 9 + [gview] * 3 + [lview] * 3))


def _head_expand(t):
    e = ((_iota((LD_W, ATT_W), 1) >> HEAD_SHIFT) == _iota((LD_W, ATT_W), 0)).astype(F32)
    return _dot_nn(t, e, HP)


def _dil(d):
    return d if d > 1 else False


def _qkv_views(proj):
    S, w = proj.shape[0], QKV_BLOCKS * ATT_W

    def fn(i, vals, p, o, a):
        for k, d in enumerate(DILATIONS):
            o[k][...] = vals[0].astype(o[k].dtype)

    outs = [(w, BF16, _c0, w) + ((d,) if d > 1 else ()) for d in DILATIONS]
    return _rowwise("qkv_views", fn, S, 512, [(proj, w, _c0, False)], (), outs)


def _attn_merge(os, ls):
    S = os[0].shape[0]

    def fn(i, vals, p, o, a):
        o3, l3 = vals[:3], vals[3:]
        m = jnp.maximum(jnp.maximum(l3[0], l3[1]), l3[2])
        e3 = [jnp.exp(l - m) for l in l3]
        den = e3[0] + e3[1] + e3[2]
        out = jnp.zeros((o3[0].shape[0], ATT_W), F32)
        for ob, e in zip(o3, e3):
            out = out + _head_expand(e / den) * ob
        o[0][...] = out
        o[1][...] = m + jnp.log(den)
        o[2][...] = out.astype(BF16)

    tiles = ([(t, ATT_W, _c0, _dil(d)) for t, d in zip(os, DILATIONS)]
             + [(t, LD_W, _c0, _dil(d)) for t, d in zip(ls, DILATIONS)])
    return _rowwise("attn_merge", fn, S, 512, tiles, (),
                    [(ATT_W, F32, _c0, ATT_W), (LD_W, F32, _c0, LD_W), (ATT_W, BF16, _c0, ATT_W)])


def _attn_bwd_prep(gmixed, att, lse):
    S = att.shape[0]
    n = len(DILATIONS)

    def fn(i, vals, p, o, a):
        g, out, lse_row = vals
        place_d = ((_iota((ATT_W, LD_W), 0) >> HEAD_SHIFT) + 8 == _iota((ATT_W, LD_W), 1)).astype(F32)
        ld = jnp.where(_iota((1, LD_W), 1) < 8, lse_row, 0.0) + _dot_nn(g * out, place_d, HP)
        for k in range(n):
            o[k][...] = g.astype(o[k].dtype)
            o[n + k][...] = ld

    tiles = [(gmixed, ATT_W, _c0, False), (att, ATT_W, _c0, False), (lse, LD_W, _c0, False)]
    outs = ([(ATT_W, BF16, _c0, ATT_W) + ((d,) if d > 1 else ()) for d in DILATIONS]
            + [(LD_W, F32, _c0, LD_W) + ((d,) if d > 1 else ()) for d in DILATIONS])
    res = _rowwise("attn_bwd_prep", fn, S, 512, tiles, (), outs)
    return res[:n], res[n:]


def _sum3_bf16(views, S, width):
    def fn(i, vals, p, o, a):
        o[0][...] = (vals[0].astype(F32) + vals[1].astype(F32) + vals[2].astype(F32)).astype(BF16)

    tiles = [(t, width, _c0, _dil(d)) for t, d in zip(views, DILATIONS)]
    return _rowwise("sum3", fn, S, 512, tiles, (), [(width, BF16, _c0, width)])[0]


def _block_diag_mask():
    return ((_iota((REC_W, REC_W), 0) >> HEAD_SHIFT) == (_iota((REC_W, REC_W), 1) >> HEAD_SHIFT)).astype(F32)


def _hgrn_chunk(qr, z, iv, lb, st, reverse, precise):
    C = REC_CHUNK
    r, c = _iota((C, C), 0), _iota((C, C), 1)
    t_cum = (c >= r) if reverse else (c <= r)
    mid_row, last_row = (C // 2, 0) if reverse else (C // 2 - 1, C - 1)
    f = lb + (1.0 - lb) * jax.nn.sigmoid(z)
    logf = jnp.log(jnp.maximum(f, F_TINY))
    k = (1.0 - lb) * jax.nn.sigmoid(-z)
    q = qr * jax.nn.sigmoid(qr)
    b = _dot_nn(t_cum.astype(F32), logf, HP)
    row = _iota((C, 1), 0)
    bm = jnp.sum(jnp.where(row == mid_row, b, 0.0), axis=0, keepdims=True)
    bl = jnp.sum(jnp.where(row == last_row, b, 0.0), axis=0, keepdims=True)
    qt = q * jnp.exp(jnp.minimum(b - bm, EXP_CLAMP))
    kt = k * jnp.exp(jnp.minimum(bm - b, EXP_CLAMP))
    qh = q * jnp.exp(b)
    kh = k * jnp.exp(bl - b)
    lam = jnp.exp(bl)
    bd = ((_iota((PAIR_W, PAIR_W), 0) >> HEAD_SHIFT) == (_iota((PAIR_W, PAIR_W), 1) >> HEAD_SHIFT)).astype(F32)
    s_in = _iota((C, PAIR_W), 1) & (HEAD - 1)
    t_in = _iota((C, PAIR_W), 0)
    tri = (s_in >= t_in) if reverse else (s_in <= t_in)
    twice = lambda t: jnp.concatenate([t, t], axis=0)
    outs, states = [], []
    for pr in range(N_PAIRS):
        sl = slice(pr * PAIR_W, (pr + 1) * PAIR_W)
        k_bd = twice(kt[:, sl]) * bd
        v_bd = (twice(iv[:, sl]) * bd).astype(BF16)
        st_bd = twice(st[:, sl]) * bd
        if precise:
            scores = _dot_nt(qt[:, sl], k_bd, lax.Precision.HIGH)
        else:
            scores = _dot_nt(qt[:, sl].astype(BF16), k_bd.astype(BF16))
        a = jnp.where(tri, scores, 0.0)
        outs.append(_dot_nn(a.astype(BF16), v_bd) + _dot_nt(qh[:, sl].astype(BF16), st_bd.astype(BF16)))
        kv = _dot_tn(iv[:, sl].astype(BF16), kh[:, sl].astype(BF16))
        st_bd = st_bd * lam[:, sl] + kv * bd
        states.append(st_bd[0:HEAD] + st_bd[HEAD:PAIR_W])
    return jnp.concatenate(outs, axis=1), jnp.concatenate(states, axis=1)


REC_CHUNKS_PER_STEP = 8
REC_ROWS = REC_CHUNKS_PER_STEP * REC_CHUNK


def _hgrn_specs(order, blocks):
    return [pl.BlockSpec((REC_ROWS, REC_W), lambda i, b=b: (order(i), b)) for b in blocks]


def _chunk_rows(j):
    return pl.ds(pl.multiple_of(j * REC_CHUNK, REC_CHUNK), REC_CHUNK)


def _hgrn_fwd(proj, lb, z_blk, reverse):
    S = proj.shape[0]
    nb = S // REC_ROWS
    order = (lambda i: nb - 1 - i) if reverse else (lambda i: i)

    def body(q_ref, z_ref, v_ref, lb_ref, o_ref, st_ref, st_scr):
        @pl.when(pl.program_id(0) == 0)
        def _():
            st_scr[...] = jnp.zeros_like(st_scr)

        def step(t, carry):
            j = REC_CHUNKS_PER_STEP - 1 - t if reverse else t
            rows = _chunk_rows(j)
            st = st_scr[...]
            st_ref[j] = st
            o, st_new = _hgrn_chunk(q_ref[rows, :], z_ref[rows, :], v_ref[rows, :], lb_ref[...], st, reverse, False)
            o_ref[rows, :] = o
            st_scr[...] = st_new
            return carry

        lax.fori_loop(0, REC_CHUNKS_PER_STEP, step, 0, unroll=True)

    return pl.pallas_call(
        body, name="hgrn_rev_fwd" if reverse else "hgrn_fwd_fwd", grid=(nb,),
        in_specs=_hgrn_specs(order, (QR, z_blk, IR)) + [pl.BlockSpec((1, REC_W), lambda i: (0, 0))],
        out_specs=[pl.BlockSpec((REC_ROWS, REC_W), lambda i: (order(i), 0)),
                   pl.BlockSpec((REC_CHUNKS_PER_STEP, HEAD, REC_W), lambda i: (order(i), 0, 0))],
        out_shape=[jax.ShapeDtypeStruct((S, REC_W), F32), jax.ShapeDtypeStruct((S // REC_CHUNK, HEAD, REC_W), F32)],
        scratch_shapes=[pltpu.VMEM((HEAD, REC_W), F32)],
        compiler_params=_cparams(("arbitrary",)),
    )(proj, proj, proj, lb)


def _hgrn_bwd(proj, lb, states, go, z_blk, reverse, other=None):
    S = proj.shape[0]
    nb = S // REC_ROWS
    order = (lambda i: i) if reverse else (lambda i: nb - 1 - i)
    n_other = 0 if other is None else 2

    def body(*refs):
        q_ref, z_ref, v_ref, lb_ref, st_ref, go_ref = refs[:6]
        other_refs = refs[6:6 + n_other]
        gq_ref, gz_ref, gv_ref, glb_ref, gst_scr = refs[6 + n_other:]

        @pl.when(pl.program_id(0) == 0)
        def _():
            gst_scr[...] = jnp.zeros_like(gst_scr)
            glb_ref[...] = jnp.zeros_like(glb_ref)

        chunk = functools.partial(_hgrn_chunk, reverse=reverse, precise=True)

        def step(t, carry):
            j = t if reverse else REC_CHUNKS_PER_STEP - 1 - t
            rows = _chunk_rows(j)
            _, vjp = jax.vjp(chunk, q_ref[rows, :], z_ref[rows, :], v_ref[rows, :], lb_ref[...], st_ref[j])
            gq, gz, gv, glb, gst = vjp((go_ref[rows, :], gst_scr[...]))
            if other_refs:
                gq = gq + other_refs[0][rows, :]
                gv = gv + other_refs[1][rows, :]
            gq_ref[rows, :] = gq.astype(gq_ref.dtype)
            gz_ref[rows, :] = gz.astype(gz_ref.dtype)
            gv_ref[rows, :] = gv.astype(gv_ref.dtype)
            glb_ref[...] += glb
            gst_scr[...] = gst
            return carry

        lax.fori_loop(0, REC_CHUNKS_PER_STEP, step, 0, unroll=4)

    row_spec = pl.BlockSpec((REC_ROWS, REC_W), lambda i: (order(i), 0))
    return pl.pallas_call(
        body, name="hgrn_rev_bwd" if reverse else "hgrn_fwd_bwd", grid=(nb,),
        in_specs=(_hgrn_specs(order, (QR, z_blk, IR)) + [pl.BlockSpec((1, REC_W), lambda i: (0, 0))]
                  + [pl.BlockSpec((REC_CHUNKS_PER_STEP, HEAD, REC_W), lambda i: (order(i), 0, 0)), row_spec]
                  + [row_spec] * n_other),
        out_specs=[row_spec] * 3 + [pl.BlockSpec((1, REC_W), lambda i: (0, 0))],
        out_shape=([jax.ShapeDtypeStruct((S, REC_W), dt) for dt in (BF16 if other else F32, BF16, BF16 if other else F32)]
                   + [jax.ShapeDtypeStruct((1, REC_W), F32)]),
        scratch_shapes=[pltpu.VMEM((HEAD, REC_W), F32)],
        compiler_params=_cparams(("arbitrary",)),
    )(proj, proj, proj, lb, states, go, *(other or ()))


def _hgrn_post_f(of, ob, gr, rnw):
    o = of + ob
    ms = _dot_nn(o * o, _block_diag_mask() * (1.0 / HEAD), HP)
    return o * lax.rsqrt(ms + EPS) * rnw * (gr * jax.nn.sigmoid(gr))


def _hgrn_post_fwd(of, ob, proj, rnw):
    S = of.shape[0]

    def fn(i, vals, p, o, a):
        o[0][...] = _hgrn_post_f(vals[0], vals[1], vals[2], p[0][...]).astype(BF16)

    tiles = [(of, REC_W, _c0, False), (ob, REC_W, _c0, False), (proj, REC_W, lambda j: GR, False)]
    return _rowwise("hgrn_post_fwd", fn, S, 512, tiles, _row_params(rnw), [(REC_W, BF16, _c0, REC_W)])[0]


def _hgrn_post_bwd(of, ob, proj, gmixed, rnw):
    S = of.shape[0]

    def fn(i, vals, p, o, a):
        _, vjp = jax.vjp(_hgrn_post_f, vals[0], vals[1], vals[2], p[0][...])
        go, _, ggr, grnw = vjp(vals[3])
        o[0][...] = go
        o[1][...] = ggr.astype(BF16)
        a[0][...] += grnw

    tiles = [(of, REC_W, _c0, False), (ob, REC_W, _c0, False), (proj, REC_W, lambda j: GR, False),
             (gmixed, REC_W, lambda j: 1, False)]
    return _rowwise("hgrn_post_bwd", fn, S, 512, tiles, _row_params(rnw),
                    [(REC_W, F32, _c0, REC_W), (REC_W, BF16, _c0, REC_W)], [(1, REC_W, _c0, REC_W)])


def _lower_bounds_f(g0, g1):
    m = jnp.maximum(g0, g1)
    e0, e1 = jnp.exp(g0 - m), jnp.exp(g1 - m)
    return e1 / (e0 + e1)


def _adamw(name, w, m, v, gparts):
    R, C = w.shape
    P = gparts.shape[0]
    tr = R if R * C * 4 * (P + 7) * 2 <= VMEM_LIMIT_BYTES // 2 else _pick(R, (256, 128, 64, 32, 16, 8))

    def body(w_ref, m_ref, v_ref, gp_ref, g_ref, d_ref, nm_ref, nv_ref):
        g = gp_ref[0].astype(F32)
        for p in range(1, P):
            g = g + gp_ref[p].astype(F32)
        w_ = w_ref[...]
        nm = ADAM_B1 * m_ref[...] + (1.0 - ADAM_B1) * g
        nv = ADAM_B2 * v_ref[...] + (1.0 - ADAM_B2) * jnp.square(g)
        m_hat = nm / (1.0 - ADAM_B1 ** ADAM_STEP)
        v_hat = nv / (1.0 - ADAM_B2 ** ADAM_STEP)
        g_ref[...] = g
        d_ref[...] = -ADAM_LR * (m_hat / (jnp.sqrt(v_hat) + ADAM_EPS) + ADAM_WD * w_)
        nm_ref[...] = nm
        nv_ref[...] = nv

    spec = pl.BlockSpec((tr, C), lambda i: (i, 0))
    return pl.pallas_call(
        body, name=name, grid=(R // tr,),
        in_specs=[spec, spec, spec, pl.BlockSpec((P, tr, C), lambda i: (0, i, 0))],
        out_specs=[spec] * 4, out_shape=[jax.ShapeDtypeStruct((R, C), F32)] * 4,
        compiler_params=_cparams(("parallel",)),
    )(w, m, v, gparts)


def _place():
    return lax.axis_index("x"), lax.axis_index("y"), lax.axis_index("c")


def _index_of(p):
    return 4 * p[0] + 2 * p[1] + p[2]


def _allgather_small(name, rows):
    m_per, n = rows.shape

    def body(x_ref, out_ref, send_sems, recv_sems, local_sem):
        x, y, c = _place()
        me, sibling = (x, y, c), (x, y, 1 - c)
        chips = [(1 - x, y), (x, 1 - y), (1 - x, 1 - y)]

        def blk(p):
            return out_ref.at[pl.ds(_index_of(p) * m_per, m_per), :]

        def copy(k, block, to, src=None):
            return pltpu.make_async_remote_copy(
                src_ref=blk(block) if src is None else src, dst_ref=blk(block),
                send_sem=send_sems.at[k], recv_sem=recv_sems.at[k], device_id=to, device_id_type=MESH)

        mine = pltpu.make_async_copy(x_ref, blk(me), local_sem)
        mine.start()
        first = [copy(0, me, sibling, src=x_ref)]
        first += [copy(1 + j, me, (*chip, c), src=x_ref) for j, chip in enumerate(chips)]
        for cp in first:
            cp.start()
        passed = [copy(4 + j, (*chip, c), sibling) for j, chip in enumerate(chips)]
        for j, chip in enumerate(chips):
            copy(1 + j, (*chip, c), me).wait_recv()
            passed[j].start()
        copy(0, sibling, me).wait_recv()
        for j, chip in enumerate(chips):
            copy(4 + j, (*chip, 1 - c), me).wait_recv()
        for cp in first + passed:
            cp.wait_send()
        mine.wait()

    return pl.pallas_call(
        body, name=name,
        out_shape=jax.ShapeDtypeStruct((N_DEV * m_per, n), rows.dtype),
        in_specs=[pl.BlockSpec(memory_space=pltpu.VMEM)],
        out_specs=pl.BlockSpec(memory_space=pltpu.VMEM),
        scratch_shapes=[pltpu.SemaphoreType.DMA((7,)), pltpu.SemaphoreType.DMA((7,)), pltpu.SemaphoreType.DMA],
        compiler_params=_cparams(),
    )(rows)


def _allgather_big(name, arrs):
    na = len(arrs)

    def body(*refs):
        ins, outs = refs[:na], refs[na:2 * na]
        send_sems, recv_sems, local_sems = refs[2 * na:]
        x, y, c = _place()
        me, sibling = (x, y, c), (x, y, 1 - c)
        chips = [(1 - x, y), (x, 1 - y), (1 - x, 1 - y)]

        def copy(a, k, block, to, src=None):
            dst = outs[a].at[_index_of(block)]
            return pltpu.make_async_remote_copy(
                src_ref=dst if src is None else src, dst_ref=dst,
                send_sem=send_sems.at[a, k], recv_sem=recv_sems.at[a, k], device_id=to, device_id_type=MESH)

        mine = [pltpu.make_async_copy(ins[a], outs[a].at[_index_of(me)], local_sems.at[a]) for a in range(na)]
        for cp in mine:
            cp.start()
        sent = []
        for a in range(na):
            sent.append(copy(a, 0, me, sibling, src=ins[a]))
            sent += [copy(a, 1 + j, me, (*chip, c), src=ins[a]) for j, chip in enumerate(chips)]
        for cp in sent:
            cp.start()
        for j, chip in enumerate(chips):
            for a in range(na):
                copy(a, 1 + j, (*chip, c), me).wait_recv()
                fwd = copy(a, 4 + j, (*chip, c), sibling)
                fwd.start()
                sent.append(fwd)
        for a in range(na):
            copy(a, 0, sibling, me).wait_recv()
            for j, chip in enumerate(chips):
                copy(a, 4 + j, (*chip, 1 - c), me).wait_recv()
        for cp in sent:
            cp.wait_send()
        for cp in mine:
            cp.wait()

    any_spec = pl.BlockSpec(memory_space=pl.ANY)
    return pl.pallas_call(
        body, name=name,
        out_shape=[jax.ShapeDtypeStruct((N_DEV,) + a.shape, a.dtype) for a in arrs],
        in_specs=[any_spec] * na, out_specs=[any_spec] * na,
        scratch_shapes=[pltpu.SemaphoreType.DMA((na, 7)), pltpu.SemaphoreType.DMA((na, 7)), pltpu.SemaphoreType.DMA((na,))],
        compiler_params=_cparams(),
    )(*arrs)


N_CHIPS = 4


def _scatter_to_sibling(name, parts):
    na = len(parts)

    def body(*refs):
        ins, outs = refs[:na], refs[na:2 * na]
        send_sems, recv_sems = refs[2 * na:]
        x, y, c = _place()
        sibling = (x, y, 1 - c)
        sent = []
        for a in range(na):
            for q in range(N_CHIPS):
                sent.append(pltpu.make_async_remote_copy(
                    src_ref=ins[a].at[2 * q + (1 - c)], dst_ref=outs[a].at[q],
                    send_sem=send_sems.at[a, q], recv_sem=recv_sems.at[a, q], device_id=sibling, device_id_type=MESH))
        for cp in sent:
            cp.start()
        for cp in sent:
            cp.wait_recv()
        for cp in sent:
            cp.wait_send()

    any_spec = pl.BlockSpec(memory_space=pl.ANY)
    return pl.pallas_call(
        body, name=name,
        out_shape=[jax.ShapeDtypeStruct((N_CHIPS,) + p.shape[1:], p.dtype) for p in parts],
        in_specs=[any_spec] * na, out_specs=[any_spec] * na,
        scratch_shapes=[pltpu.SemaphoreType.DMA((na, N_CHIPS)), pltpu.SemaphoreType.DMA((na, N_CHIPS))],
        compiler_params=_cparams(),
    )(*parts)


def _pair_sum(name, parts, recv):
    _, R, C = parts.shape
    tr = _pick(R, (256, 128, 64, 32, 16))

    def body(p_ref, r_ref, o_ref):
        c = lax.axis_index("c")
        o_ref[...] = (p_ref[c].astype(F32) + r_ref[...].astype(F32)).astype(BF16)

    return pl.pallas_call(
        body, name=name, grid=(N_CHIPS, R // tr),
        in_specs=[pl.BlockSpec((None, 2, tr, C), lambda q, i: (q, 0, i, 0)), pl.BlockSpec((None, tr, C), lambda q, i: (q, i, 0))],
        out_specs=pl.BlockSpec((None, tr, C), lambda q, i: (q, i, 0)),
        out_shape=jax.ShapeDtypeStruct((N_CHIPS, R, C), BF16),
        compiler_params=_cparams(("parallel", "parallel")),
    )(parts.reshape(N_CHIPS, 2, R, C), recv)


def _scatter_to_chips(name, sums):
    na = len(sums)

    def body(*refs):
        ins, outs = refs[:na], refs[na:2 * na]
        send_sems, recv_sems, local_sems = refs[2 * na:]
        x, y, c = _place()
        me = 2 * x + y
        chips = [(1 - x, y), (x, 1 - y), (1 - x, 1 - y)]
        mine = [pltpu.make_async_copy(ins[a].at[me], outs[a].at[me], local_sems.at[a]) for a in range(na)]
        for cp in mine:
            cp.start()
        sent = []
        for a in range(na):
            for k, (qx, qy) in enumerate(chips):
                sent.append(pltpu.make_async_remote_copy(
                    src_ref=ins[a].at[2 * qx + qy], dst_ref=outs[a].at[me],
                    send_sem=send_sems.at[a, k], recv_sem=recv_sems.at[a, k], device_id=(qx, qy, c), device_id_type=MESH))
        for cp in sent:
            cp.start()
        for a in range(na):
            for k, (qx, qy) in enumerate(chips):
                slot = outs[a].at[2 * qx + qy]
                pltpu.make_async_remote_copy(
                    src_ref=slot, dst_ref=slot, send_sem=send_sems.at[a, k], recv_sem=recv_sems.at[a, k],
                    device_id=(qx, qy, c), device_id_type=MESH).wait_recv()
        for cp in sent:
            cp.wait_send()
        for cp in mine:
            cp.wait()

    any_spec = pl.BlockSpec(memory_space=pl.ANY)
    return pl.pallas_call(
        body, name=name,
        out_shape=[jax.ShapeDtypeStruct(p.shape, p.dtype) for p in sums],
        in_specs=[any_spec] * na, out_specs=[any_spec] * na,
        scratch_shapes=[pltpu.SemaphoreType.DMA((na, 3)), pltpu.SemaphoreType.DMA((na, 3)), pltpu.SemaphoreType.DMA((na,))],
        compiler_params=_cparams(),
    )(*sums)


def _gather_row(name, vec, width):
    n = vec.shape[0]
    rows = jnp.pad(vec, (0, width - n)).reshape(SUBLANES_F32, width // SUBLANES_F32)
    return _allgather_small(name, rows).reshape(N_DEV, width)[:, :n]


def _layer_fwd(x, mod, w):
    sh1, sc1, g1, sh2, sc2, g2 = [mod[i:i + 1] for i in range(N_MOD)]
    S = x.shape[0]
    h1 = _normmod_fwd(x, w["norm1_w"], sc1, sh1)
    proj = _matmul("proj_in", h1, w["w_in"], "nn")
    a1, a_out = _conv_a_fwd(proj, w["conv_a_w"], w["conv_a_b"], w["ln_a_w"], w["ln_a_b"])
    qkv = _qkv_views(proj)
    os, ls = zip(*[_attn_fwd(v, dil) for v, dil in zip(qkv, DILATIONS)])
    att, lse, att_b = _attn_merge(os, ls)
    of, st_f = _hgrn_fwd(proj, w["lb_f"], ZF, False)
    ob, st_b = _hgrn_fwd(proj, w["lb_b"], ZB, True)
    rec = _hgrn_post_fwd(of, ob, proj, w["rec_norm_w"])
    mixed = jnp.concatenate([att_b, rec, a_out], axis=1)
    y1 = _matmul("proj_out", mixed, w["w_out"], "nn")
    x2, h2 = _gate_add_normmod(x, y1, g1, w["norm2_w"], sc2, sh2)
    u = _matmul("ffn_up", h2, w["w_up"], "nn")
    act = _ffn_mid_fwd(u, w["conv_f_w"])
    y2 = _matmul("ffn_down", act, w["w_down"], "nn")
    x3 = _gate_add(x2, y2, g2)
    saved = dict(x=x, h1=h1, proj=proj, a1=a1, qkv=qkv, att=att, lse=lse, of=of, ob=ob, st_f=st_f, st_b=st_b,
                 mixed=mixed, y1=y1, x2=x2, h2=h2, u=u, act=act, y2=y2)
    return x3, saved


def _layer_bwd(gx3, mod, w, s):
    sh1, sc1, g1, sh2, sc2, g2 = [mod[i:i + 1] for i in range(N_MOD)]
    S = gx3.shape[0]
    g = {}
    gy2, gg2 = _gate_bwd(gx3, s["y2"], g2)
    gact = _matmul("ffn_down_dx", gy2, w["w_down"], "nt")
    g["w_down"] = _matmul("ffn_down_dw", s["act"], gy2, "tn")
    gu, g["conv_f_w"] = _ffn_mid_bwd(s["u"], gact, w["conv_f_w"])
    gh2 = _matmul_nt_pieces("ffn_up_dx", gu, (w["w_up"][:, :D_FF], w["w_up"][:, D_FF:]))
    g["w_up"] = jnp.concatenate([_matmul("ffn_up_dw", s["h2"], t, "tn") for t in gu], axis=1)
    gx2, gy1, g["norm2_w"], gsc2, gsh2, gg1 = _normmod_gate_bwd(s["x2"], gh2, gx3, w["norm2_w"], sc2, sh2, s["y1"], g1)
    gmixed = _matmul("proj_out_dx", gy1, w["w_out"], "nt")
    g["w_out"] = _matmul("proj_out_dw", s["mixed"], gy1, "tn")
    go, ggr, g["rec_norm_w"] = _hgrn_post_bwd(s["of"], s["ob"], s["proj"], gmixed, w["rec_norm_w"])
    gq_f, gz_f, gv_f, g["lb_f"] = _hgrn_bwd(s["proj"], w["lb_f"], s["st_f"], go, ZF, False)
    gq_r, gz_b, gv_r, g["lb_b"] = _hgrn_bwd(s["proj"], w["lb_b"], s["st_b"], go, ZB, True, other=(gq_f, gv_f))
    dos, lds = _attn_bwd_prep(gmixed, s["att"], s["lse"])
    gqkv = zip(*[_attn_bwd(v, do, ld, dil) for v, do, ld, dil in zip(s["qkv"], dos, lds, DILATIONS)])
    gq_a, gk_a, gv_a = [_sum3_bf16(lst, S, ATT_W) for lst in gqkv]
    gav, gag, gcw, g["conv_a_b"], g["ln_a_w"], g["ln_a_b"] = _conv_a_bwd(
        s["proj"], s["a1"], gmixed, w["conv_a_w"], w["ln_a_w"], w["ln_a_b"])
    g["conv_a_w"] = gcw[:CONV_W]
    gproj = jnp.concatenate([gq_a, gk_a, gv_a, gq_r, gz_f, gz_b, gv_r, ggr, gav, gag,
                             jnp.zeros((S, IN_COLS_PAD - IN_COLS), BF16)], axis=1)
    gh1 = _matmul("proj_in_dx", gproj, w["w_in"], "nt")
    g["w_in"] = _matmul("proj_in_dw", s["h1"], gproj, "tn")
    gx, g["norm1_w"], gsc1, gsh1 = _normmod_bwd(s["x"], gh1, gx2, w["norm1_w"], sc1, sh1)
    gmod = jnp.concatenate([gsh1, gsc1, gg1, gsh2, gsc2, gg2], axis=0)
    return gx, gmod, g


def _permute_in_cols(t):
    pad = jnp.zeros(t.shape[:-1] + (IN_COLS_PAD - IN_COLS,), t.dtype)
    return jnp.concatenate([t[..., CONV_COLS:], t[..., :CONV_COLS], pad], axis=-1)


def _unpermute_in_cols(t):
    return jnp.concatenate([t[..., IN_COLS - CONV_COLS:IN_COLS], t[..., :IN_COLS - CONV_COLS]], axis=-1)


def _cols_from_gathered(t, lead):
    nd = t.ndim
    perm = tuple(range(1, nd - 1)) + (0, nd - 1)
    t = t.transpose(perm)
    return t.reshape(t.shape[:-2] + (t.shape[-2] * t.shape[-1],))


def _cols_to_parts(t):
    L, R, C = t.shape
    return t.reshape(L * R, N_DEV, C // N_DEV).transpose(1, 0, 2)


SMALL_REPL = (("norm1_w", D), ("conv_a_b", CONV_CH), ("ln_a_w", CONV_CH), ("ln_a_b", CONV_CH),
              ("rec_norm_w", REC_W), ("norm2_w", D))


def kernel(x, c, w_ada, b_ada, norm1_w, w_in, conv_a_w, conv_a_b, ln_a_w, ln_a_b, lb_gamma, rec_norm_w, w_out, norm2_w, w_up, conv_f_w, w_down, final_norm_w, loss_target, m_w_ada, m_b_ada, m_norm1_w, m_w_in, m_conv_a_w, m_conv_a_b, m_ln_a_w, m_ln_a_b, m_lb_gamma, m_rec_norm_w, m_w_out, m_norm2_w, m_w_up, m_conv_f_w, m_w_down, m_final_norm_w, v_w_ada, v_b_ada, v_norm1_w, v_w_in, v_conv_a_w, v_conv_a_b, v_ln_a_w, v_ln_a_b, v_lb_gamma, v_rec_norm_w, v_w_out, v_norm2_w, v_w_up, v_conv_f_w, v_w_down, v_final_norm_w):
    px, py, pc = _place()
    me = _index_of((px, py, pc))
    xs, tgt = x[0], loss_target[0]
    S = xs.shape[0]
    ada_cols = w_ada.shape[2]

    big = [w_in.reshape(DEPTH * D, -1), w_up.reshape(DEPTH * D, -1), w_out.reshape(-1, D), w_down.reshape(-1, D)]
    g_in, g_up, g_out, g_down = _allgather_big("gather_weights", [t.astype(BF16) for t in big])
    w_in_f = _permute_in_cols(_cols_from_gathered(g_in.reshape(N_DEV, DEPTH, D, -1), 1))
    w_up_f = _cols_from_gathered(g_up.reshape(N_DEV, DEPTH, D, -1), 1)
    w_out_f = g_out.reshape(N_DEV, DEPTH, D // N_DEV, D).transpose(1, 0, 2, 3).reshape(DEPTH, D, D)
    w_out_f = jnp.concatenate([w_out_f[:, CONV_CH:], w_out_f[:, :CONV_CH]], axis=1)
    w_down_f = g_down.reshape(N_DEV, DEPTH, D_FF // N_DEV, D).transpose(1, 0, 2, 3).reshape(DEPTH, D_FF, D)

    small_in = jnp.concatenate([c.reshape(-1), conv_a_w.reshape(-1), lb_gamma.reshape(-1), conv_f_w.reshape(-1)])
    gs = _gather_row("gather_small", small_in, 8192)
    o1 = D
    o2 = o1 + conv_a_w.size
    o3 = o2 + lb_gamma.size
    c_all = gs[:, :o1]
    conv_a_f = _cols_from_gathered(gs[:, o1:o2].reshape(N_DEV, DEPTH, CONV_W, -1), 1)
    lb_gamma_f = _cols_from_gathered(gs[:, o2:o3].reshape(N_DEV, DEPTH, 2, -1), 1)
    conv_f_f = _cols_from_gathered(gs[:, o3:].reshape(N_DEV, DEPTH, FFN_CONV_W, -1), 1)
    conv_a_pad = jnp.pad(conv_a_f, ((0, 0), (0, CONV_W_PAD - CONV_W), (0, 0)))

    b_loc = lax.dynamic_slice_in_dim(b_ada, me * ada_cols, ada_cols, axis=1)

    def mod_fn(c_all_, w_, b_):
        cond = c_all_ * jax.nn.sigmoid(c_all_)
        return (jnp.concatenate([_dot_nn(cond, w_[l], HP) + b_[l] for l in range(DEPTH)], axis=1),)

    (mod_part,) = _vmem_call("ada_mod", mod_fn, [c_all, w_ada, b_loc[:, None, :]], [((N_DEV, DEPTH * ada_cols), F32)])
    gm = _allgather_small("gather_mod", mod_part).reshape(N_DEV, N_DEV, DEPTH, ada_cols)
    mod = lax.dynamic_index_in_dim(gm, me, axis=1, keepdims=False)
    mod = mod.transpose(1, 0, 2).reshape(DEPTH, N_MOD, D)

    (lb1,) = _vmem_call("lower_bounds", lambda a, b: (_lower_bounds_f(a, b),), [lb_gamma_f[0], lb_gamma_f[1]], [((2, REC_W), F32)])
    lb4 = jnp.concatenate([jnp.zeros_like(lb1), lb1], axis=0)

    def layer_weights(l):
        row = lambda t: t[l].reshape(1, -1)
        return dict(norm1_w=row(norm1_w), w_in=w_in_f[l], conv_a_w=conv_a_pad[l], conv_a_b=row(conv_a_b), ln_a_w=row(ln_a_w),
                    ln_a_b=row(ln_a_b), lb_f=lb4[2 * l:2 * l + 1], lb_b=lb4[2 * l + 1:2 * l + 2], rec_norm_w=row(rec_norm_w),
                    w_out=w_out_f[l], norm2_w=row(norm2_w), w_up=w_up_f[l], conv_f_w=conv_f_f[l], w_down=w_down_f[l])

    ws = [layer_weights(l) for l in range(DEPTH)]
    h, saved = xs, []
    for l in range(DEPTH):
        h, s = _layer_fwd(h, mod[l], ws[l])
        saved.append(s)
    gh, g_final, loss_row = _loss_head(h, tgt, final_norm_w.reshape(1, D))
    loss = lax.psum(loss_row[0, 0], ("x", "y", "c"))
    gmods, gws = [None] * DEPTH, [None] * DEPTH
    for l in reversed(range(DEPTH)):
        gh, gmods[l], gws[l] = _layer_bwd(gh, mod[l], ws[l], saved[l])
    grad_x = gh[None]

    glb1 = jnp.concatenate([gws[1]["lb_f"], gws[1]["lb_b"]], axis=0)

    def lb_bwd_fn(a, b, g1):
        _, vjp = jax.vjp(_lower_bounds_f, a, b)
        return vjp(g1)

    g_lb_gamma = jnp.stack(_vmem_call("lower_bounds_bwd", lb_bwd_fn, [lb_gamma_f[0], lb_gamma_f[1], glb1], [((2, REC_W), F32)] * 2))
    pieces = [jnp.stack(gmods).reshape(-1)]
    for l in range(DEPTH):
        pieces += [gws[l][n].reshape(-1) for n, _ in SMALL_REPL]
    pieces += [g_final.reshape(-1)]
    pieces += [jnp.stack([gws[l]["conv_a_w"] for l in range(DEPTH)]).reshape(-1), g_lb_gamma.reshape(-1),
               jnp.stack([gws[l]["conv_f_w"] for l in range(DEPTH)]).reshape(-1)]
    small_g = jnp.concatenate(pieces)
    n_small = small_g.shape[0]
    gsm = _gather_row("gather_small_grads", small_g, 71680)
    n_mod = DEPTH * N_MOD * D
    gmod_all = gsm[:, :n_mod].reshape(N_DEV, DEPTH, N_MOD * D)
    gmod_loc = lax.dynamic_slice_in_dim(gmod_all, me * ada_cols, ada_cols, axis=2).transpose(1, 0, 2)

    def small_fn(gsm_, c_all_, gm_):
        cond = c_all_ * jax.nn.sigmoid(c_all_)
        gw = jnp.concatenate([_dot_tn(cond, gm_[l], HP) for l in range(DEPTH)], axis=0)
        return jnp.sum(gsm_, axis=0, keepdims=True), gw

    tot, g_w_ada = _vmem_call("small_grads", small_fn, [gsm, c_all, gmod_loc],
                              [((1, n_small), F32), ((DEPTH * D, ada_cols), F32)])
    tot = tot[0]
    grads = {"w_ada": g_w_ada.reshape(DEPTH, D, ada_cols), "b_ada": tot[:n_mod].reshape(DEPTH, N_MOD * D)}
    pos = n_mod
    per_layer = {n: [] for n, _ in SMALL_REPL}
    for l in range(DEPTH):
        for n, width in SMALL_REPL:
            per_layer[n].append(tot[pos:pos + width])
            pos += width
    for n, _ in SMALL_REPL:
        grads[n] = jnp.stack(per_layer[n])
    grads["final_norm_w"] = tot[pos:pos + D]
    pos += D
    n_ca, n_lb, n_cf = DEPTH * CONV_W * CONV_CH, DEPTH * 2 * REC_W, DEPTH * FFN_CONV_W * 2 * D_FF
    g_ca = tot[pos:pos + n_ca].reshape(DEPTH, CONV_W, CONV_CH)
    g_lb = tot[pos + n_ca:pos + n_ca + n_lb].reshape(DEPTH, 2, REC_W)
    g_cf = tot[pos + n_ca + n_lb:pos + n_ca + n_lb + n_cf].reshape(DEPTH, FFN_CONV_W, 2 * D_FF)
    grads["conv_a_w"] = lax.dynamic_slice_in_dim(g_ca, me * conv_a_w.shape[2], conv_a_w.shape[2], axis=2)
    grads["lb_gamma"] = lax.dynamic_slice_in_dim(g_lb, me * lb_gamma.shape[2], lb_gamma.shape[2], axis=2)
    grads["conv_f_w"] = lax.dynamic_slice_in_dim(g_cf, me * conv_f_w.shape[2], conv_f_w.shape[2], axis=2)

    gw_in = _unpermute_in_cols(jnp.stack([gws[l]["w_in"] for l in range(DEPTH)]))
    gw_up = jnp.stack([gws[l]["w_up"] for l in range(DEPTH)])
    gw_out = jnp.stack([gws[l]["w_out"] for l in range(DEPTH)])
    gw_out = jnp.concatenate([gw_out[:, D - CONV_CH:], gw_out[:, :D - CONV_CH]], axis=1)
    gw_down = jnp.stack([gws[l]["w_down"] for l in range(DEPTH)])
    rows_to_parts = lambda t: t.reshape(DEPTH, N_DEV, -1, D).transpose(1, 0, 2, 3).reshape(N_DEV, -1, D)
    parts = [_cols_to_parts(gw_in), _cols_to_parts(gw_up), rows_to_parts(gw_out), rows_to_parts(gw_down)]
    parts = [t.astype(BF16) for t in parts]
    from_sibling = _scatter_to_sibling("scatter_sibling", parts)
    sums = [_pair_sum("pair_sum", p, r) for p, r in zip(parts, from_sibling)]
    r_in, r_up, r_out, r_down = _scatter_to_chips("scatter_chips", sums)

    given = dict(w_ada=(w_ada, m_w_ada, v_w_ada), b_ada=(b_ada, m_b_ada, v_b_ada), norm1_w=(norm1_w, m_norm1_w, v_norm1_w),
                 w_in=(w_in, m_w_in, v_w_in), conv_a_w=(conv_a_w, m_conv_a_w, v_conv_a_w), conv_a_b=(conv_a_b, m_conv_a_b, v_conv_a_b),
                 ln_a_w=(ln_a_w, m_ln_a_w, v_ln_a_w), ln_a_b=(ln_a_b, m_ln_a_b, v_ln_a_b), lb_gamma=(lb_gamma, m_lb_gamma, v_lb_gamma),
                 rec_norm_w=(rec_norm_w, m_rec_norm_w, v_rec_norm_w), w_out=(w_out, m_w_out, v_w_out),
                 norm2_w=(norm2_w, m_norm2_w, v_norm2_w), w_up=(w_up, m_w_up, v_w_up), conv_f_w=(conv_f_w, m_conv_f_w, v_conv_f_w),
                 w_down=(w_down, m_w_down, v_w_down), final_norm_w=(final_norm_w, m_final_norm_w, v_final_norm_w))
    big_parts = dict(w_in=r_in, w_up=r_up, w_out=r_out, w_down=r_down)
    names = list(given)
    res = {}
    for n in names:
        w_, m_, v_ = given[n]
        shape = w_.shape
        C = shape[-1]
        two_d = lambda t: t.reshape(-1, C)
        gp = big_parts[n] if n in big_parts else two_d(grads[n])[None]
        res[n] = [t.reshape(shape) for t in _adamw("adamw_" + n, two_d(w_), two_d(m_), two_d(v_), gp)]
    return (loss, grad_x, *[res[n][0] for n in names], *[res[n][1] for n in names],
            *[res[n][2] for n in names], *[res[n][3] for n in names])
```

```python
import functools

import jax
import jax.numpy as jnp
from jax import lax
from jax.experimental import pallas as pl
from jax.experimental.pallas import tpu as pltpu

F32 = jnp.float32
BF16 = jnp.bfloat16
HP = lax.Precision.HIGHEST
MESH = pl.DeviceIdType.MESH

N_DEV = 8
D = 1024
DEPTH = 2
CONV_CH = 256
CONV_W = 31
CONV_W_PAD = 32
ATT_W = 384
REC_W = 384
N_HEADS = 6
HEAD = 64
HEAD_SHIFT = 6
HALF_BAND = 64
ATT_BLK = 512
ATT_SUB = 128
DILATIONS = (1, 4, 16)
ALIBI_SLOPES = tuple(float(2.0 ** (-8.0 * (h + 1) / N_HEADS)) for h in range(N_HEADS))
MASK_VALUE = -1e30
REC_CHUNK = 64
EXP_CLAMP = 80.0
F_TINY = 1e-30
IN_COLS = 3584
IN_COLS_PAD = IN_COLS
QKV_BLOCKS = 3
D_FF = 2816
FFN_CONV_W = 3
N_MOD = 6
EPS = 1e-6
ADAM_LR, ADAM_B1, ADAM_B2, ADAM_EPS, ADAM_WD, ADAM_STEP = 0.001, 0.9, 0.999, 1e-08, 0.01, 10

VMEM_LIMIT_BYTES = 56 * 1024 * 1024
SUBLANES_F32 = 8
LANES = 128

QA, KA, VA, QR, ZF, ZB, IR, GR = range(8)
AV_BLK, AG_BLK = 12, 13
CONV_COLS = 2 * CONV_CH


def _cparams(sem=None):
    kw = dict(vmem_limit_bytes=VMEM_LIMIT_BYTES)
    if sem is not None:
        kw["dimension_semantics"] = sem
    return pltpu.CompilerParams(**kw)


def _iota(shape, dim):
    return lax.broadcasted_iota(jnp.int32, shape, dim)


def _dot(a, b, dims, precision=None):
    return lax.dot_general(a, b, (dims, ((), ())), precision=precision, preferred_element_type=F32)


def _dot_nn(a, b, precision=None):
    return _dot(a, b, ((1,), (0,)), precision)


def _dot_nt(a, b, precision=None):
    return _dot(a, b, ((1,), (1,)), precision)


def _dot_tn(a, b, precision=None):
    return _dot(a, b, ((0,), (0,)), precision)


def _c0(j):
    return 0


def _pick(n, cands):
    for c in cands:
        if n % c == 0:
            return c
    return n


MATMUL_OUT_TILE_BYTES = 8 * 1024 * 1024


def _div_lanes(n, cap):
    best = None
    for d in range(LANES, min(n, cap) + 1, LANES):
        if n % d == 0:
            best = d
    return best if best is not None else n


def _matmul_tiles(mode, M, N, K):
    if mode == "nn":
        tm = _pick(M, (1024, 512, 256, 128))
        return tm, _div_lanes(N, MATMUL_OUT_TILE_BYTES // (4 * tm)), K
    if mode == "nt":
        return _pick(M, (512, 256, 128)), N, K
    tm = _div_lanes(M, 1408)
    return tm, _div_lanes(N, MATMUL_OUT_TILE_BYTES // (4 * tm)), _pick(K, (1024, 512, 256))


def _matmul(name, a, b, mode, out_dtype=F32):
    if mode == "nn":
        (M, K), (_, N) = a.shape, b.shape
    elif mode == "nt":
        (M, K), (N, _) = a.shape, b.shape
    else:
        (K, M), (_, N) = a.shape, b.shape
    tm, tn, tk = _matmul_tiles(mode, M, N, K)
    nk = K // tk
    if mode == "nn":
        a_spec = pl.BlockSpec((tm, tk), lambda i, j, k: (i, k))
        b_spec = pl.BlockSpec((tk, tn), lambda i, j, k: (k, j))
        dims = ((1,), (0,))
    elif mode == "nt":
        a_spec = pl.BlockSpec((tm, tk), lambda i, j, k: (i, k))
        b_spec = pl.BlockSpec((tn, tk), lambda i, j, k: (j, k))
        dims = ((1,), (1,))
    else:
        a_spec = pl.BlockSpec((tk, tm), lambda i, j, k: (k, i))
        b_spec = pl.BlockSpec((tk, tn), lambda i, j, k: (k, j))
        dims = ((0,), (0,))

    def body_whole(a_ref, b_ref, o_ref):
        o_ref[...] = _dot(a_ref[...].astype(BF16), b_ref[...].astype(BF16), dims).astype(o_ref.dtype)

    def body(a_ref, b_ref, o_ref, acc_ref):
        k = pl.program_id(2)
        part = _dot(a_ref[...].astype(BF16), b_ref[...].astype(BF16), dims)

        @pl.when(k == 0)
        def _():
            acc_ref[...] = part

        @pl.when(k > 0)
        def _():
            acc_ref[...] += part

        @pl.when(k == nk - 1)
        def _():
            o_ref[...] = acc_ref[...].astype(o_ref.dtype)

    return pl.pallas_call(
        body_whole if nk == 1 else body, name=name, grid=(M // tm, N // tn, nk),
        in_specs=[a_spec, b_spec],
        out_specs=pl.BlockSpec((tm, tn), lambda i, j, k: (i, j)),
        out_shape=jax.ShapeDtypeStruct((M, N), out_dtype),
        scratch_shapes=[] if nk == 1 else [pltpu.VMEM((tm, tn), F32)],
        compiler_params=_cparams(("parallel", "parallel", "arbitrary")),
    )(a, b)


def _matmul_nt_pieces(name, a_pieces, b_pieces):
    M, N = a_pieces[0].shape[0], b_pieces[0].shape[0]
    tm = _pick(M, (512, 256, 128))
    n = len(a_pieces)

    def body(*refs):
        acc = _dot(refs[0][...].astype(BF16), refs[n][...].astype(BF16), ((1,), (1,)))
        for p in range(1, n):
            acc = acc + _dot(refs[p][...].astype(BF16), refs[n + p][...].astype(BF16), ((1,), (1,)))
        refs[2 * n][...] = acc

    in_specs = ([pl.BlockSpec((tm, a.shape[1]), lambda i: (i, 0)) for a in a_pieces]
                + [pl.BlockSpec(b.shape, lambda i: (0, 0)) for b in b_pieces])
    return pl.pallas_call(
        body, name=name, grid=(M // tm,), in_specs=in_specs,
        out_specs=pl.BlockSpec((tm, N), lambda i: (i, 0)), out_shape=jax.ShapeDtypeStruct((M, N), F32),
        compiler_params=_cparams(("parallel",)),
    )(*a_pieces, *b_pieces)


def _rowwise(name, fn, S, ts, tiles, params=(), outs=(), accs=(), halo=0, ncb=1):
    in_specs, args, scratch = [], [], []
    for arr, w, jm, with_halo in tiles:
        if isinstance(with_halo, int) and with_halo > 1:
            d = with_halo
            in_specs.append(pl.BlockSpec((ts // d, d * w), lambda j, i: (i, 0)))
            args.append(arr)
            scratch.append(pltpu.VMEM((w // LANES, ts, LANES), F32))
        elif with_halo:
            hb, nhb = ts // halo, S // halo
            in_specs += [
                pl.BlockSpec((halo, w), lambda j, i, jm=jm, hb=hb: (jnp.maximum(i * hb - 1, 0), jm(j))),
                pl.BlockSpec((ts, w), lambda j, i, jm=jm: (i, jm(j))),
                pl.BlockSpec((halo, w), lambda j, i, jm=jm, hb=hb, nhb=nhb: (jnp.minimum((i + 1) * hb, nhb - 1), jm(j))),
            ]
            args += [arr, arr, arr]
        else:
            in_specs.append(pl.BlockSpec((ts, w), lambda j, i, jm=jm: (i, jm(j))))
            args.append(arr)
    for arr, r, w, jm in params:
        in_specs.append(pl.BlockSpec((r, w), lambda j, i, jm=jm: (0, jm(j))))
        args.append(arr)
    out_specs, out_shape = [], []
    for w, dt, jm, tw, *dil in outs:
        if dil:
            out_specs.append(pl.BlockSpec((ts // dil[0], dil[0] * w), lambda j, i: (i, 0)))
            out_shape.append(jax.ShapeDtypeStruct((S // dil[0], dil[0] * w), dt))
            scratch += [pltpu.VMEM((ts, w), F32), pltpu.VMEM((w // LANES, ts, LANES), F32)]
        else:
            out_specs.append(pl.BlockSpec((ts, w), lambda j, i, jm=jm: (i, jm(j))))
            out_shape.append(jax.ShapeDtypeStruct((S, tw), dt))
    for r, w, jm, tw in accs:
        out_specs.append(pl.BlockSpec((r, w), lambda j, i, jm=jm: (0, jm(j))))
        out_shape.append(jax.ShapeDtypeStruct((r, tw), F32))
    n_tiles, n_params, n_outs, n_accs = len(tiles), len(params), len(outs), len(accs)

    def residue_rows(r, d):
        return pl.ds(r, ts // d, stride=d)

    def body(*refs):
        i = pl.program_id(1)
        n_io = len(in_specs) + n_outs + n_accs
        scr = list(refs[n_io:])
        refs = refs[:n_io]
        pos, vals = 0, []
        for _, w, _, with_halo in tiles:
            if isinstance(with_halo, int) and with_halo > 1:
                d, buf = with_halo, scr.pop(0)
                for r in range(d):
                    for c in range(w // LANES):
                        buf[c, residue_rows(r, d), :] = refs[pos][:, r * w + c * LANES:r * w + (c + 1) * LANES].astype(F32)
                vals.append(jnp.concatenate([buf[c] for c in range(w // LANES)], axis=1))
                pos += 1
            elif with_halo:
                before, after = refs[pos][...], refs[pos + 2][...]
                before = jnp.where(i > 0, before, jnp.zeros_like(before))
                after = jnp.where(i < S // ts - 1, after, jnp.zeros_like(after))
                vals.append(jnp.concatenate([before, refs[pos + 1][...], after], axis=0))
                pos += 3
            else:
                vals.append(refs[pos][...])
                pos += 1
        prefs = refs[pos:pos + n_params]
        orefs = list(refs[pos + n_params:pos + n_params + n_outs])
        arefs = refs[pos + n_params + n_outs:]
        staged = []
        for k, (w, _, _, _, *dil) in enumerate(outs):
            if dil:
                staged.append((orefs[k], scr.pop(0), scr.pop(0), w, dil[0]))
                orefs[k] = staged[-1][1]

        @pl.when(i == 0)
        def _():
            for r in arefs:
                r[...] = jnp.zeros_like(r)

        fn(i, vals, prefs, orefs, arefs)
        for out_ref, flat, buf, w, d in staged:
            for c in range(w // LANES):
                buf[c] = flat[:, c * LANES:(c + 1) * LANES]
                for r in range(d):
                    out_ref[:, r * w + c * LANES:r * w + (c + 1) * LANES] = buf[c, residue_rows(r, d), :].astype(out_ref.dtype)

    res = pl.pallas_call(
        body, name=name, grid=(ncb, S // ts),
        in_specs=in_specs, out_specs=out_specs, out_shape=out_shape, scratch_shapes=scratch,
        compiler_params=_cparams(("arbitrary", "arbitrary")),
    )(*args)
    return res


def _vmem_call(name, fn, ins, out_shapes):
    n_in = len(ins)

    def body(*refs):
        vals = fn(*[r[...] for r in refs[:n_in]])
        for r, v in zip(refs[n_in:], vals):
            r[...] = v.astype(r.dtype)

    return pl.pallas_call(
        body, name=name,
        out_shape=[jax.ShapeDtypeStruct(s, dt) for s, dt in out_shapes],
        compiler_params=_cparams(),
    )(*ins)


def _rms(x, w):
    return x * lax.rsqrt(jnp.mean(x * x, axis=-1, keepdims=True) + EPS) * w


def _normmod_f(x, nw, sc, sh):
    return _rms(x, nw) * (1.0 + sc) + sh


def _row_params(*vecs):
    return [(v, 1, v.shape[1], _c0) for v in vecs]


def _normmod_fwd(x, nw, sc, sh):
    S = x.shape[0]

    def fn(i, vals, p, o, a):
        o[0][...] = _normmod_f(vals[0], p[0][...], p[1][...], p[2][...]).astype(BF16)

    return _rowwise("normmod_fwd", fn, S, 512, [(x, D, _c0, False)], _row_params(nw, sc, sh), [(D, BF16, _c0, D)])[0]


def _normmod_bwd(x, gh, gres, nw, sc, sh):
    S = x.shape[0]

    def fn(i, vals, p, o, a):
        _, vjp = jax.vjp(_normmod_f, vals[0], p[0][...], p[1][...], p[2][...])
        gx, gnw, gsc, gsh = vjp(vals[1])
        o[0][...] = gx + vals[2]
        a[0][...] += gnw
        a[1][...] += gsc
        a[2][...] += gsh

    return _rowwise("normmod_bwd", fn, S, 512, [(x, D, _c0, False), (gh, D, _c0, False), (gres, D, _c0, False)],
                    _row_params(nw, sc, sh), [(D, F32, _c0, D)], [(1, D, _c0, D)] * 3)


def _gate_add(x, y, g):
    S = x.shape[0]

    def fn(i, vals, p, o, a):
        o[0][...] = vals[0] + p[0][...] * vals[1]

    return _rowwise("gate_add", fn, S, 512, [(x, D, _c0, False), (y, D, _c0, False)], _row_params(g), [(D, F32, _c0, D)])[0]


def _gate_bwd(gx, y, g):
    S = gx.shape[0]

    def fn(i, vals, p, o, a):
        o[0][...] = (vals[0] * p[0][...]).astype(BF16)
        a[0][...] += jnp.sum(vals[0] * vals[1], axis=0, keepdims=True)

    return _rowwise("gate_bwd", fn, S, 512, [(gx, D, _c0, False), (y, D, _c0, False)], _row_params(g),
                    [(D, BF16, _c0, D)], [(1, D, _c0, D)])


def _gate_add_normmod(x, y, g, nw, sc, sh):
    S = x.shape[0]

    def fn(i, vals, p, o, a):
        x2 = vals[0] + p[0][...] * vals[1]
        o[0][...] = x2
        o[1][...] = _normmod_f(x2, p[1][...], p[2][...], p[3][...]).astype(BF16)

    return _rowwise("gate_add_normmod", fn, S, 512, [(x, D, _c0, False), (y, D, _c0, False)], _row_params(g, nw, sc, sh),
                    [(D, F32, _c0, D), (D, BF16, _c0, D)])


def _normmod_gate_bwd(x, gh, gres, nw, sc, sh, y, g):
    S = x.shape[0]

    def fn(i, vals, p, o, a):
        _, vjp = jax.vjp(_normmod_f, vals[0], p[0][...], p[1][...], p[2][...])
        gx, gnw, gsc, gsh = vjp(vals[1])
        gx = gx + vals[2]
        o[0][...] = gx
        o[1][...] = (gx * p[3][...]).astype(BF16)
        a[0][...] += gnw
        a[1][...] += gsc
        a[2][...] += gsh
        a[3][...] += jnp.sum(gx * vals[3], axis=0, keepdims=True)

    tiles = [(t, D, _c0, False) for t in (x, gh, gres, y)]
    return _rowwise("normmod_gate_bwd", fn, S, 512, tiles, _row_params(nw, sc, sh, g),
                    [(D, F32, _c0, D), (D, BF16, _c0, D)], [(1, D, _c0, D)] * 4)


def _loss_head(x, tgt, fw):
    S = x.shape[0]

    def fn(i, vals, p, o, a):
        y, vjp = jax.vjp(_rms, vals[0], p[0][...])
        err = y - vals[1]
        gx, gfw = vjp(err * (1.0 / D))
        o[0][...] = gx
        a[0][...] += gfw
        part = 0.5 * jnp.sum(jnp.mean(err * err, axis=-1, keepdims=True), axis=0, keepdims=True)
        a[1][...] += jnp.broadcast_to(part, (1, LANES))

    return _rowwise("loss_head", fn, S, 256, [(x, D, _c0, False), (tgt, D, _c0, False)], _row_params(fw),
                    [(D, F32, _c0, D)], [(1, D, _c0, D), (1, LANES, _c0, LANES)])


CONV_HALO = 16
CONV_TS = 512


def _shifted(ext, shift, ts, halo):
    n = ext.shape[0]
    s = shift % n
    r = ext if s == 0 else pltpu.roll(ext, s, 0)
    return r[halo:halo + ts]


def _ln_silu(a, w, b):
    mu = jnp.mean(a, axis=-1, keepdims=True)
    var = jnp.mean(jnp.square(a - mu), axis=-1, keepdims=True)
    y = (a - mu) * lax.rsqrt(var + EPS) * w + b
    return y * jax.nn.sigmoid(y)


def _conv_a_fwd(proj, w_pad, b, lnw, lnb):
    S = proj.shape[0]
    ts, H = min(CONV_TS, S), CONV_HALO

    def fn(i, vals, p, o, a):
        a0 = vals[0] * jax.nn.sigmoid(vals[1])
        acc = jnp.zeros((ts, CONV_CH), F32) + p[1][...]
        for k in range(CONV_W):
            acc = acc + _shifted(a0, CONV_W // 2 - k, ts, H) * p[0][pl.ds(k, 1), :]
        o[0][...] = acc
        o[1][...] = _ln_silu(acc, p[2][...], p[3][...]).astype(BF16)

    tiles = [(proj, CONV_CH, lambda j: AV_BLK, True), (proj, CONV_CH, lambda j: AG_BLK, True)]
    params = [(w_pad, CONV_W_PAD, CONV_CH, _c0)] + _row_params(b, lnw, lnb)
    return _rowwise("conv_a_fwd", fn, S, ts, tiles, params, [(CONV_CH, F32, _c0, CONV_CH), (CONV_CH, BF16, _c0, CONV_CH)], halo=H)


def _conv_a_bwd(proj, a1, gmixed, w_pad, lnw, lnb):
    S = proj.shape[0]
    ts, H = min(CONV_TS, S), CONV_HALO

    def fn(i, vals, p, o, a):
        av, ag, a1e, ge = vals
        lw, lb = p[1][...], p[2][...]
        _, vjp_e = jax.vjp(lambda t: _ln_silu(t, lw, lb), a1e)
        (ga1e,) = vjp_e(ge)
        c = slice(H, H + ts)
        _, vjp_c = jax.vjp(_ln_silu, a1e[c], lw, lb)
        ga1, glw, glb = vjp_c(ge[c])
        a[1][...] += jnp.sum(ga1, axis=0, keepdims=True)
        a[2][...] += glw
        a[3][...] += glb
        sg = jax.nn.sigmoid(ag)
        a0 = av * sg
        ga0 = jnp.zeros((ts, CONV_CH), F32)
        for k in range(CONV_W):
            a[0][pl.ds(k, 1), :] += jnp.sum(ga1 * _shifted(a0, CONV_W // 2 - k, ts, H), axis=0, keepdims=True)
            ga0 = ga0 + _shifted(ga1e, k - CONV_W // 2, ts, H) * p[0][pl.ds(k, 1), :]
        sgc, avc = sg[c], av[c]
        o[0][...] = (ga0 * sgc).astype(BF16)
        o[1][...] = (ga0 * avc * sgc * (1.0 - sgc)).astype(BF16)

    tiles = [(proj, CONV_CH, lambda j: AV_BLK, True), (proj, CONV_CH, lambda j: AG_BLK, True),
             (a1, CONV_CH, _c0, True), (gmixed, CONV_CH, lambda j: 3, True)]
    params = [(w_pad, CONV_W_PAD, CONV_CH, _c0)] + _row_params(lnw, lnb)
    outs = [(CONV_CH, BF16, _c0, CONV_CH), (CONV_CH, BF16, _c0, CONV_CH)]
    accs = [(CONV_W_PAD, CONV_CH, _c0, CONV_CH)] + [(1, CONV_CH, _c0, CONV_CH)] * 3
    return _rowwise("conv_a_bwd", fn, S, ts, tiles, params, outs, accs, halo=H)


FFN_HALO = 8
FFN_TS = 512
FFN_TS_FWD = 2048
FFN_CB = 256
FFN_NCB = D_FF // FFN_CB


def _gelu_mul(g, v):
    return 0.5 * g * (1.0 + lax.erf(g * (2.0 ** -0.5))) * v


def _ffn_mid_fwd(u, cw):
    S = u.shape[0]
    ts, H = min(FFN_TS_FWD, S), FFN_HALO

    def conv(ext, w_ref):
        acc = jnp.zeros((ts, FFN_CB), F32)
        for k in range(FFN_CONV_W):
            acc = acc + _shifted(ext, 1 - k, ts, H) * w_ref[pl.ds(k, 1), :]
        return acc

    def fn(i, vals, p, o, a):
        o[0][...] = _gelu_mul(conv(vals[0], p[0]), conv(vals[1], p[1])).astype(BF16)

    gate, val = (lambda j: j), (lambda j: j + FFN_NCB)
    return _rowwise("ffn_mid_fwd", fn, S, ts, [(u, FFN_CB, gate, True), (u, FFN_CB, val, True)],
                    [(cw, FFN_CONV_W, FFN_CB, gate), (cw, FFN_CONV_W, FFN_CB, val)],
                    [(FFN_CB, BF16, gate, D_FF)], halo=H, ncb=FFN_NCB)[0]


def _ffn_mid_bwd(u, gact, cw):
    S = u.shape[0]
    ts, H = min(FFN_TS, S), FFN_HALO
    n = ts + 2 * H

    def fn(i, vals, p, o, a):
        ug, uv, ga = vals

        def conv_all(ue, w_ref):
            acc = jnp.zeros((n, FFN_CB), F32)
            for k in range(FFN_CONV_W):
                s = (1 - k) % n
                acc = acc + (ue if s == 0 else pltpu.roll(ue, s, 0)) * w_ref[pl.ds(k, 1), :]
            return acc

        _, vjp = jax.vjp(_gelu_mul, conv_all(ug, p[0]), conv_all(uv, p[1]))
        for half, (gc, ue) in enumerate(zip(vjp(ga), (ug, uv))):
            gu = jnp.zeros((ts, FFN_CB), F32)
            for k in range(FFN_CONV_W):
                gu = gu + _shifted(gc, k - 1, ts, H) * p[half][pl.ds(k, 1), :]
                a[half][pl.ds(k, 1), :] += jnp.sum(gc[H:H + ts] * _shifted(ue, 1 - k, ts, H), axis=0, keepdims=True)
            o[half][...] = gu.astype(BF16)

    gate, val = (lambda j: j), (lambda j: j + FFN_NCB)
    tiles = [(u, FFN_CB, gate, True), (u, FFN_CB, val, True), (gact, FFN_CB, gate, True)]
    params = [(cw, FFN_CONV_W, FFN_CB, gate), (cw, FFN_CONV_W, FFN_CB, val)]
    gu_gate, gu_val, gw_gate, gw_val = _rowwise("ffn_mid_bwd", fn, S, ts, tiles, params, [(FFN_CB, BF16, gate, D_FF)] * 2,
                                                [(FFN_CONV_W, FFN_CB, gate, D_FF)] * 2, halo=H, ncb=FFN_NCB)
    return (gu_gate, gu_val), jnp.concatenate([gw_gate, gw_val], axis=1)


LD_W = LANES
PAIR_W = 2 * HEAD
N_PAIRS = N_HEADS // 2


def _sub_view(t, d):
    S, C = t.shape
    return t.reshape(S // d, d * C)


def _sub_halo_specs(width, col, blk, hb, nhb):
    per = blk // hb
    return [
        pl.BlockSpec((hb, width), lambda r, i: (jnp.maximum(i * per - 1, 0), col(r))),
        pl.BlockSpec((blk, width), lambda r, i: (i, col(r))),
        pl.BlockSpec((hb, width), lambda r, i: (jnp.minimum((i + 1) * per, nhb - 1), col(r))),
    ]


def _pick_lane(t, lane):
    return jnp.sum(jnp.where(_iota((1, t.shape[1]), 1) == lane, t, 0.0), axis=1, keepdims=True)


def _pair_mask(h2):
    return (_iota((1, PAIR_W), 1) >> HEAD_SHIFT) == h2


def _cat_bf16(a, b, c):
    return jnp.concatenate([a[...], b[...], c[...]], axis=0).astype(BF16)


def _attn_fwd(view, dil):
    L = view.shape[0]
    blk, hb = min(ATT_BLK, L), HALF_BAND
    sub = min(ATT_SUB, blk)
    span = sub + 2 * hb

    def body(q_ref, kp, kc, kn, vp, vc, vn, o_ref, l_ref):
        i = pl.program_id(1)
        rel = _iota((sub, span), 1) - hb - _iota((sub, span), 0)
        band = jnp.abs(rel) <= hb
        dist = jnp.abs(rel).astype(F32) * float(dil)
        q_all, k_all, v_all = q_ref[...].astype(BF16), _cat_bf16(kp, kc, kn), _cat_bf16(vp, vc, vn)
        for r0 in range(0, blk, sub):
            kpos = i * blk + r0 - hb + _iota((sub, span), 1)
            valid = band & (kpos >= 0) & (kpos < L)
            q, k, v = q_all[r0:r0 + sub], k_all[r0:r0 + span], v_all[r0:r0 + span]
            lse = jnp.zeros((sub, LD_W), F32)
            for pr in range(N_PAIRS):
                sl = slice(pr * PAIR_W, (pr + 1) * PAIR_W)
                qp, kpair, vpair = q[:, sl], k[:, sl], v[:, sl]
                o = jnp.zeros((sub, PAIR_W), F32)
                for h2 in range(2):
                    h, mask = 2 * pr + h2, _pair_mask(h2)
                    s = _dot_nt(jnp.where(mask, qp, jnp.zeros_like(qp)), kpair) * (HEAD ** -0.5) - ALIBI_SLOPES[h] * dist
                    s = jnp.where(valid, s, MASK_VALUE)
                    m = jnp.max(s, axis=1, keepdims=True)
                    p = jnp.exp(s - m)
                    l = jnp.sum(p, axis=1, keepdims=True)
                    o = jnp.where(mask, _dot_nn(p.astype(BF16), vpair) / l, o)
                    lse = lse + jnp.where(_iota((1, LD_W), 1) == h, m + jnp.log(l), 0.0)
                o_ref[r0:r0 + sub, sl] = o
            l_ref[r0:r0 + sub, :] = lse

    nhb = L // hb
    in_specs = ([pl.BlockSpec((blk, ATT_W), lambda r, i: (i, r * QKV_BLOCKS +QA))]
                + _sub_halo_specs(ATT_W, lambda r: r * QKV_BLOCKS +KA, blk, hb, nhb)
                + _sub_halo_specs(ATT_W, lambda r: r * QKV_BLOCKS +VA, blk, hb, nhb))
    return pl.pallas_call(
        body, name=f"attn_fwd_d{dil}", grid=(dil, L // blk), in_specs=in_specs,
        out_specs=[pl.BlockSpec((blk, ATT_W), lambda r, i: (i, r)), pl.BlockSpec((blk, LD_W), lambda r, i: (i, r))],
        out_shape=[jax.ShapeDtypeStruct((L, dil * ATT_W), F32), jax.ShapeDtypeStruct((L, dil * LD_W), F32)],
        compiler_params=_cparams(("parallel", "parallel")),
    )(*([view] * 7))


def _attn_bwd(pview, gview, lview, dil):
    L = pview.shape[0]
    blk, hb = min(ATT_BLK, L), HALF_BAND
    sub = min(ATT_SUB, blk)
    span = sub + 2 * hb
    scale = HEAD ** -0.5

    def body(qp, qc, qn, kp, kc, kn, vp, vc, vn, gp, gc, gn, lp, lc, ln, dq_ref, dk_ref, dv_ref):
        i = pl.program_id(1)
        le_all = jnp.concatenate([lp[...], lc[...], ln[...]], axis=0)
        rel_q = _iota((sub, span), 1) - hb - _iota((sub, span), 0)
        band_q = jnp.abs(rel_q) <= hb
        dist_q = jnp.abs(rel_q).astype(F32) * float(dil)
        rel_k = _iota((span, sub), 1) + hb - _iota((span, sub), 0)
        band_k = jnp.abs(rel_k) <= hb
        dist_k = jnp.abs(rel_k).astype(F32) * float(dil)
        qe_all, ke_all, ve_all = _cat_bf16(qp, qc, qn), _cat_bf16(kp, kc, kn), _cat_bf16(vp, vc, vn)
        ge_all = _cat_bf16(gp, gc, gn)
        for r0 in range(0, blk, sub):
            kpos = i * blk + r0 - hb + _iota((sub, span), 1)
            valid_q = band_q & (kpos >= 0) & (kpos < L)
            qpos = i * blk + r0 - hb + _iota((span, sub), 0)
            valid_k = band_k & (qpos >= 0) & (qpos < L)
            ext, mid = slice(r0, r0 + span), slice(r0 + hb, r0 + hb + sub)
            l, le = le_all[mid], le_all[ext]
            for pr in range(N_PAIRS):
                sl = slice(pr * PAIR_W, (pr + 1) * PAIR_W)
                q, k, v, g = qe_all[mid, sl], ke_all[mid, sl], ve_all[mid, sl], ge_all[mid, sl]
                qe, ke, ve, ge = qe_all[ext, sl], ke_all[ext, sl], ve_all[ext, sl], ge_all[ext, sl]
                dq = jnp.zeros((sub, PAIR_W), F32)
                dk = jnp.zeros((sub, PAIR_W), F32)
                dv = jnp.zeros((sub, PAIR_W), F32)
                for h2 in range(2):
                    h, mask = 2 * pr + h2, _pair_mask(h2)
                    only = lambda t: jnp.where(mask, t, jnp.zeros_like(t))
                    s = _dot_nt(only(q), ke) * scale - ALIBI_SLOPES[h] * dist_q
                    p = jnp.where(valid_q, jnp.exp(s - _pick_lane(l, h)), 0.0)
                    ds = p * (_dot_nt(only(g), ve) - _pick_lane(l, 8 + h))
                    dq = jnp.where(mask, _dot_nn(ds.astype(BF16), ke), dq)
                    s = _dot_nt(only(qe), k) * scale - ALIBI_SLOPES[h] * dist_k
                    p = jnp.where(valid_k, jnp.exp(s - _pick_lane(le, h)), 0.0)
                    dv = jnp.where(mask, _dot_tn(p.astype(BF16), ge), dv)
                    ds = p * (_dot_nt(only(ge), v) - _pick_lane(le, 8 + h))
                    dk = jnp.where(mask, _dot_tn(ds.astype(BF16), qe), dk)
                dq_ref[r0:r0 + sub, sl] = (dq * scale).astype(BF16)
                dk_ref[r0:r0 + sub, sl] = (dk * scale).astype(BF16)
                dv_ref[r0:r0 + sub, sl] = dv.astype(BF16)

    nhb = L // hb
    in_specs = (_sub_halo_specs(ATT_W, lambda r: r * QKV_BLOCKS +QA, blk, hb, nhb)
                + _sub_halo_specs(ATT_W, lambda r: r * QKV_BLOCKS +KA, blk, hb, nhb)
                + _sub_halo_specs(ATT_W, lambda r: r * QKV_BLOCKS +VA, blk, hb, nhb)
                + _sub_halo_specs(ATT_W, lambda r: r, blk, hb, nhb) + _sub_halo_specs(LD_W, lambda r: r, blk, hb, nhb))
    o_spec = pl.BlockSpec((blk, ATT_W), lambda r, i: (i, r))
    return pl.pallas_call(
        body, name=f"attn_bwd_d{dil}", grid=(dil, L // blk), in_specs=in_specs,
        out_specs=[o_spec] * 3, out_shape=[jax.ShapeDtypeStruct((L, dil * ATT_W), BF16)] * 3,
        compiler_params=_cparams(("parallel", "parallel")),
    )(*([pview] * 9 + [gview] * 3 + [lview] * 3))


def _head_expand(t):
    e = ((_iota((LD_W, ATT_W), 1) >> HEAD_SHIFT) == _iota((LD_W, ATT_W), 0)).astype(F32)
    return _dot_nn(t, e, HP)


def _dil(d):
    return d if d > 1 else False


def _qkv_views(proj):
    S, w = proj.shape[0], QKV_BLOCKS * ATT_W

    def fn(i, vals, p, o, a):
        for k, d in enumerate(DILATIONS):
            o[k][...] = vals[0].astype(o[k].dtype)

    outs = [(w, BF16, _c0, w) + ((d,) if d > 1 else ()) for d in DILATIONS]
    return _rowwise("qkv_views", fn, S, 512, [(proj, w, _c0, False)], (), outs)


def _attn_merge(os, ls):
    S = os[0].shape[0]

    def fn(i, vals, p, o, a):
        o3, l3 = vals[:3], vals[3:]
        m = jnp.maximum(jnp.maximum(l3[0], l3[1]), l3[2])
        e3 = [jnp.exp(l - m) for l in l3]
        den = e3[0] + e3[1] + e3[2]
        out = jnp.zeros((o3[0].shape[0], ATT_W), F32)
        for ob, e in zip(o3, e3):
            out = out + _head_expand(e / den) * ob
        o[0][...] = out
        o[1][...] = m + jnp.log(den)
        o[2][...] = out.astype(BF16)

    tiles = ([(t, ATT_W, _c0, _dil(d)) for t, d in zip(os, DILATIONS)]
             + [(t, LD_W, _c0, _dil(d)) for t, d in zip(ls, DILATIONS)])
    return _rowwise("attn_merge", fn, S, 512, tiles, (),
                    [(ATT_W, F32, _c0, ATT_W), (LD_W, F32, _c0, LD_W), (ATT_W, BF16, _c0, ATT_W)])


def _attn_bwd_prep(gmixed, att, lse):
    S = att.shape[0]
    n = len(DILATIONS)

    def fn(i, vals, p, o, a):
        g, out, lse_row = vals
        place_d = ((_iota((ATT_W, LD_W), 0) >> HEAD_SHIFT) + 8 == _iota((ATT_W, LD_W), 1)).astype(F32)
        ld = jnp.where(_iota((1, LD_W), 1) < 8, lse_row, 0.0) + _dot_nn(g * out, place_d, HP)
        for k in range(n):
            o[k][...] = g.astype(o[k].dtype)
            o[n + k][...] = ld

    tiles = [(gmixed, ATT_W, _c0, False), (att, ATT_W, _c0, False), (lse, LD_W, _c0, False)]
    outs = ([(ATT_W, BF16, _c0, ATT_W) + ((d,) if d > 1 else ()) for d in DILATIONS]
            + [(LD_W, F32, _c0, LD_W) + ((d,) if d > 1 else ()) for d in DILATIONS])
    res = _rowwise("attn_bwd_prep", fn, S, 512, tiles, (), outs)
    return res[:n], res[n:]


def _sum3_bf16(views, S, width):
    def fn(i, vals, p, o, a):
        o[0][...] = (vals[0].astype(F32) + vals[1].astype(F32) + vals[2].astype(F32)).astype(BF16)

    tiles = [(t, width, _c0, _dil(d)) for t, d in zip(views, DILATIONS)]
    return _rowwise("sum3", fn, S, 512, tiles, (), [(width, BF16, _c0, width)])[0]


def _block_diag_mask():
    return ((_iota((REC_W, REC_W), 0) >> HEAD_SHIFT) == (_iota((REC_W, REC_W), 1) >> HEAD_SHIFT)).astype(F32)


def _hgrn_chunk(qr, z, iv, lb, st, reverse, precise):
    C = REC_CHUNK
    r, c = _iota((C, C), 0), _iota((C, C), 1)
    t_cum = (c >= r) if reverse else (c <= r)
    mid_row, last_row = (C // 2, 0) if reverse else (C // 2 - 1, C - 1)
    f = lb + (1.0 - lb) * jax.nn.sigmoid(z)
    logf = jnp.log(jnp.maximum(f, F_TINY))
    k = (1.0 - lb) * jax.nn.sigmoid(-z)
    q = qr * jax.nn.sigmoid(qr)
    b = _dot_nn(t_cum.astype(F32), logf, HP)
    row = _iota((C, 1), 0)
    bm = jnp.sum(jnp.where(row == mid_row, b, 0.0), axis=0, keepdims=True)
    bl = jnp.sum(jnp.where(row == last_row, b, 0.0), axis=0, keepdims=True)
    qt = q * jnp.exp(jnp.minimum(b - bm, EXP_CLAMP))
    kt = k * jnp.exp(jnp.minimum(bm - b, EXP_CLAMP))
    qh = q * jnp.exp(b)
    kh = k * jnp.exp(bl - b)
    lam = jnp.exp(bl)
    bd = ((_iota((PAIR_W, PAIR_W), 0) >> HEAD_SHIFT) == (_iota((PAIR_W, PAIR_W), 1) >> HEAD_SHIFT)).astype(F32)
    s_in = _iota((C, PAIR_W), 1) & (HEAD - 1)
    t_in = _iota((C, PAIR_W), 0)
    tri = (s_in >= t_in) if reverse else (s_in <= t_in)
    twice = lambda t: jnp.concatenate([t, t], axis=0)
    outs, states = [], []
    for pr in range(N_PAIRS):
        sl = slice(pr * PAIR_W, (pr + 1) * PAIR_W)
        k_bd = twice(kt[:, sl]) * bd
        v_bd = (twice(iv[:, sl]) * bd).astype(BF16)
        st_bd = twice(st[:, sl]) * bd
        if precise:
            scores = _dot_nt(qt[:, sl], k_bd, lax.Precision.HIGH)
        else:
            scores = _dot_nt(qt[:, sl].astype(BF16), k_bd.astype(BF16))
        a = jnp.where(tri, scores, 0.0)
        outs.append(_dot_nn(a.astype(BF16), v_bd) + _dot_nt(qh[:, sl].astype(BF16), st_bd.astype(BF16)))
        kv = _dot_tn(iv[:, sl].astype(BF16), kh[:, sl].astype(BF16))
        st_bd = st_bd * lam[:, sl] + kv * bd
        states.append(st_bd[0:HEAD] + st_bd[HEAD:PAIR_W])
    return jnp.concatenate(outs, axis=1), jnp.concatenate(states, axis=1)


REC_CHUNKS_PER_STEP = 8
REC_ROWS = REC_CHUNKS_PER_STEP * REC_CHUNK


def _hgrn_specs(order, blocks):
    return [pl.BlockSpec((REC_ROWS, REC_W), lambda i, b=b: (order(i), b)) for b in blocks]


def _chunk_rows(j):
    return pl.ds(pl.multiple_of(j * REC_CHUNK, REC_CHUNK), REC_CHUNK)


def _hgrn_fwd(proj, lb, z_blk, reverse):
    S = proj.shape[0]
    nb = S // REC_ROWS
    order = (lambda i: nb - 1 - i) if reverse else (lambda i: i)

    def body(q_ref, z_ref, v_ref, lb_ref, o_ref, st_ref, st_scr):
        @pl.when(pl.program_id(0) == 0)
        def _():
            st_scr[...] = jnp.zeros_like(st_scr)

        def step(t, carry):
            j = REC_CHUNKS_PER_STEP - 1 - t if reverse else t
            rows = _chunk_rows(j)
            st = st_scr[...]
            st_ref[j] = st
            o, st_new = _hgrn_chunk(q_ref[rows, :], z_ref[rows, :], v_ref[rows, :], lb_ref[...], st, reverse, False)
            o_ref[rows, :] = o
            st_scr[...] = st_new
            return carry

        lax.fori_loop(0, REC_CHUNKS_PER_STEP, step, 0, unroll=True)

    return pl.pallas_call(
        body, name="hgrn_rev_fwd" if reverse else "hgrn_fwd_fwd", grid=(nb,),
        in_specs=_hgrn_specs(order, (QR, z_blk, IR)) + [pl.BlockSpec((1, REC_W), lambda i: (0, 0))],
        out_specs=[pl.BlockSpec((REC_ROWS, REC_W), lambda i: (order(i), 0)),
                   pl.BlockSpec((REC_CHUNKS_PER_STEP, HEAD, REC_W), lambda i: (order(i), 0, 0))],
        out_shape=[jax.ShapeDtypeStruct((S, REC_W), F32), jax.ShapeDtypeStruct((S // REC_CHUNK, HEAD, REC_W), F32)],
        scratch_shapes=[pltpu.VMEM((HEAD, REC_W), F32)],
        compiler_params=_cparams(("arbitrary",)),
    )(proj, proj, proj, lb)


def _hgrn_bwd(proj, lb, states, go, z_blk, reverse, other=None):
    S = proj.shape[0]
    nb = S // REC_ROWS
    order = (lambda i: i) if reverse else (lambda i: nb - 1 - i)
    n_other = 0 if other is None else 2

    def body(*refs):
        q_ref, z_ref, v_ref, lb_ref, st_ref, go_ref = refs[:6]
        other_refs = refs[6:6 + n_other]
        gq_ref, gz_ref, gv_ref, glb_ref, gst_scr = refs[6 + n_other:]

        @pl.when(pl.program_id(0) == 0)
        def _():
            gst_scr[...] = jnp.zeros_like(gst_scr)
            glb_ref[...] = jnp.zeros_like(glb_ref)

        chunk = functools.partial(_hgrn_chunk, reverse=reverse, precise=True)

        def step(t, carry):
            j = t if reverse else REC_CHUNKS_PER_STEP - 1 - t
            rows = _chunk_rows(j)
            _, vjp = jax.vjp(chunk, q_ref[rows, :], z_ref[rows, :], v_ref[rows, :], lb_ref[...], st_ref[j])
            gq, gz, gv, glb, gst = vjp((go_ref[rows, :], gst_scr[...]))
            if other_refs:
                gq = gq + other_refs[0][rows, :]
                gv = gv + other_refs[1][rows, :]
            gq_ref[rows, :] = gq.astype(gq_ref.dtype)
            gz_ref[rows, :] = gz.astype(gz_ref.dtype)
            gv_ref[rows, :] = gv.astype(gv_ref.dtype)
            glb_ref[...] += glb
            gst_scr[...] = gst
            return carry

        lax.fori_loop(0, REC_CHUNKS_PER_STEP, step, 0, unroll=4)

    row_spec = pl.BlockSpec((REC_ROWS, REC_W), lambda i: (order(i), 0))
    return pl.pallas_call(
        body, name="hgrn_rev_bwd" if reverse else "hgrn_fwd_bwd", grid=(nb,),
        in_specs=(_hgrn_specs(order, (QR, z_blk, IR)) + [pl.BlockSpec((1, REC_W), lambda i: (0, 0))]
                  + [pl.BlockSpec((REC_CHUNKS_PER_STEP, HEAD, REC_W), lambda i: (order(i), 0, 0)), row_spec]
                  + [row_spec] * n_other),
        out_specs=[row_spec] * 3 + [pl.BlockSpec((1, REC_W), lambda i: (0, 0))],
        out_shape=([jax.ShapeDtypeStruct((S, REC_W), dt) for dt in (BF16 if other else F32, BF16, BF16 if other else F32)]
                   + [jax.ShapeDtypeStruct((1, REC_W), F32)]),
        scratch_shapes=[pltpu.VMEM((HEAD, REC_W), F32)],
        compiler_params=_cparams(("arbitrary",)),
    )(proj, proj, proj, lb, states, go, *(other or ()))


def _hgrn_post_f(of, ob, gr, rnw):
    o = of + ob
    ms = _dot_nn(o * o, _block_diag_mask() * (1.0 / HEAD), HP)
    return o * lax.rsqrt(ms + EPS) * rnw * (gr * jax.nn.sigmoid(gr))


def _hgrn_post_fwd(of, ob, proj, rnw):
    S = of.shape[0]

    def fn(i, vals, p, o, a):
        o[0][...] = _hgrn_post_f(vals[0], vals[1], vals[2], p[0][...]).astype(BF16)

    tiles = [(of, REC_W, _c0, False), (ob, REC_W, _c0, False), (proj, REC_W, lambda j: GR, False)]
    return _rowwise("hgrn_post_fwd", fn, S, 512, tiles, _row_params(rnw), [(REC_W, BF16, _c0, REC_W)])[0]


def _hgrn_post_bwd(of, ob, proj, gmixed, rnw):
    S = of.shape[0]

    def fn(i, vals, p, o, a):
        _, vjp = jax.vjp(_hgrn_post_f, vals[0], vals[1], vals[2], p[0][...])
        go, _, ggr, grnw = vjp(vals[3])
        o[0][...] = go
        o[1][...] = ggr.astype(BF16)
        a[0][...] += grnw

    tiles = [(of, REC_W, _c0, False), (ob, REC_W, _c0, False), (proj, REC_W, lambda j: GR, False),
             (gmixed, REC_W, lambda j: 1, False)]
    return _rowwise("hgrn_post_bwd", fn, S, 512, tiles, _row_params(rnw),
                    [(REC_W, F32, _c0, REC_W), (REC_W, BF16, _c0, REC_W)], [(1, REC_W, _c0, REC_W)])


def _lower_bounds_f(g0, g1):
    m = jnp.maximum(g0, g1)
    e0, e1 = jnp.exp(g0 - m), jnp.exp(g1 - m)
    return e1 / (e0 + e1)


def _adamw(name, w, m, v, gparts):
    R, C = w.shape
    P = gparts.shape[0]
    tr = R if R * C * 4 * (P + 7) * 2 <= VMEM_LIMIT_BYTES // 2 else _pick(R, (256, 128, 64, 32, 16, 8))

    def body(w_ref, m_ref, v_ref, gp_ref, g_ref, d_ref, nm_ref, nv_ref):
        g = gp_ref[0].astype(F32)
        for p in range(1, P):
            g = g + gp_ref[p].astype(F32)
        w_ = w_ref[...]
        nm = ADAM_B1 * m_ref[...] + (1.0 - ADAM_B1) * g
        nv = ADAM_B2 * v_ref[...] + (1.0 - ADAM_B2) * jnp.square(g)
        m_hat = nm / (1.0 - ADAM_B1 ** ADAM_STEP)
        v_hat = nv / (1.0 - ADAM_B2 ** ADAM_STEP)
        g_ref[...] = g
        d_ref[...] = -ADAM_LR * (m_hat / (jnp.sqrt(v_hat) + ADAM_EPS) + ADAM_WD * w_)
        nm_ref[...] = nm
        nv_ref[...] = nv

    spec = pl.BlockSpec((tr, C), lambda i: (i, 0))
    return pl.pallas_call(
        body, name=name, grid=(R // tr,),
        in_specs=[spec, spec, spec, pl.BlockSpec((P, tr, C), lambda i: (0, i, 0))],
        out_specs=[spec] * 4, out_shape=[jax.ShapeDtypeStruct((R, C), F32)] * 4,
        compiler_params=_cparams(("parallel",)),
    )(w, m, v, gparts)


def _place():
    return lax.axis_index("x"), lax.axis_index("y"), lax.axis_index("c")


def _index_of(p):
    return 4 * p[0] + 2 * p[1] + p[2]


def _allgather_small(name, rows):
    m_per, n = rows.shape

    def body(x_ref, out_ref, send_sems, recv_sems, local_sem):
        x, y, c = _place()
        me, sibling = (x, y, c), (x, y, 1 - c)
        chips = [(1 - x, y), (x, 1 - y), (1 - x, 1 - y)]

        def blk(p):
            return out_ref.at[pl.ds(_index_of(p) * m_per, m_per), :]

        def copy(k, block, to, src=None):
            return pltpu.make_async_remote_copy(
                src_ref=blk(block) if src is None else src, dst_ref=blk(block),
                send_sem=send_sems.at[k], recv_sem=recv_sems.at[k], device_id=to, device_id_type=MESH)

        mine = pltpu.make_async_copy(x_ref, blk(me), local_sem)
        mine.start()
        first = [copy(0, me, sibling, src=x_ref)]
        first += [copy(1 + j, me, (*chip, c), src=x_ref) for j, chip in enumerate(chips)]
        for cp in first:
            cp.start()
        passed = [copy(4 + j, (*chip, c), sibling) for j, chip in enumerate(chips)]
        for j, chip in enumerate(chips):
            copy(1 + j, (*chip, c), me).wait_recv()
            passed[j].start()
        copy(0, sibling, me).wait_recv()
        for j, chip in enumerate(chips):
            copy(4 + j, (*chip, 1 - c), me).wait_recv()
        for cp in first + passed:
            cp.wait_send()
        mine.wait()

    return pl.pallas_call(
        body, name=name,
        out_shape=jax.ShapeDtypeStruct((N_DEV * m_per, n), rows.dtype),
        in_specs=[pl.BlockSpec(memory_space=pltpu.VMEM)],
        out_specs=pl.BlockSpec(memory_space=pltpu.VMEM),
        scratch_shapes=[pltpu.SemaphoreType.DMA((7,)), pltpu.SemaphoreType.DMA((7,)), pltpu.SemaphoreType.DMA],
        compiler_params=_cparams(),
    )(rows)


def _allgather_big(name, arrs):
    na = len(arrs)

    def body(*refs):
        ins, outs = refs[:na], refs[na:2 * na]
        send_sems, recv_sems, local_sems = refs[2 * na:]
        x, y, c = _place()
        me, sibling = (x, y, c), (x, y, 1 - c)
        chips = [(1 - x, y), (x, 1 - y), (1 - x, 1 - y)]

        def copy(a, k, block, to, src=None):
            dst = outs[a].at[_index_of(block)]
            return pltpu.make_async_remote_copy(
                src_ref=dst if src is None else src, dst_ref=dst,
                send_sem=send_sems.at[a, k], recv_sem=recv_sems.at[a, k], device_id=to, device_id_type=MESH)

        mine = [pltpu.make_async_copy(ins[a], outs[a].at[_index_of(me)], local_sems.at[a]) for a in range(na)]
        for cp in mine:
            cp.start()
        sent = []
        for a in range(na):
            sent.append(copy(a, 0, me, sibling, src=ins[a]))
            sent += [copy(a, 1 + j, me, (*chip, c), src=ins[a]) for j, chip in enumerate(chips)]
        for cp in sent:
            cp.start()
        for j, chip in enumerate(chips):
            for a in range(na):
                copy(a, 1 + j, (*chip, c), me).wait_recv()
                fwd = copy(a, 4 + j, (*chip, c), sibling)
                fwd.start()
                sent.append(fwd)
        for a in range(na):
            copy(a, 0, sibling, me).wait_recv()
            for j, chip in enumerate(chips):
                copy(a, 4 + j, (*chip, 1 - c), me).wait_recv()
        for cp in sent:
            cp.wait_send()
        for cp in mine:
            cp.wait()

    any_spec = pl.BlockSpec(memory_space=pl.ANY)
    return pl.pallas_call(
        body, name=name,
        out_shape=[jax.ShapeDtypeStruct((N_DEV,) + a.shape, a.dtype) for a in arrs],
        in_specs=[any_spec] * na, out_specs=[any_spec] * na,
        scratch_shapes=[pltpu.SemaphoreType.DMA((na, 7)), pltpu.SemaphoreType.DMA((na, 7)), pltpu.SemaphoreType.DMA((na,))],
        compiler_params=_cparams(),
    )(*arrs)


N_CHIPS = 4


def _scatter_to_sibling(name, parts):
    na = len(parts)

    def body(*refs):
        ins, outs = refs[:na], refs[na:2 * na]
        send_sems, recv_sems = refs[2 * na:]
        x, y, c = _place()
        sibling = (x, y, 1 - c)
        sent = []
        for a in range(na):
            for q in range(N_CHIPS):
                sent.append(pltpu.make_async_remote_copy(
                    src_ref=ins[a].at[2 * q + (1 - c)], dst_ref=outs[a].at[q],
                    send_sem=send_sems.at[a, q], recv_sem=recv_sems.at[a, q], device_id=sibling, device_id_type=MESH))
        for cp in sent:
            cp.start()
        for cp in sent:
            cp.wait_recv()
        for cp in sent:
            cp.wait_send()

    any_spec = pl.BlockSpec(memory_space=pl.ANY)
    return pl.pallas_call(
        body, name=name,
        out_shape=[jax.ShapeDtypeStruct((N_CHIPS,) + p.shape[1:], p.dtype) for p in parts],
        in_specs=[any_spec] * na, out_specs=[any_spec] * na,
        scratch_shapes=[pltpu.SemaphoreType.DMA((na, N_CHIPS)), pltpu.SemaphoreType.DMA((na, N_CHIPS))],
        compiler_params=_cparams(),
    )(*parts)


def _pair_sum(name, parts, recv):
    _, R, C = parts.shape
    tr = _pick(R, (256, 128, 64, 32, 16))

    def body(p_ref, r_ref, o_ref):
        c = lax.axis_index("c")
        o_ref[...] = (p_ref[c].astype(F32) + r_ref[...].astype(F32)).astype(BF16)

    return pl.pallas_call(
        body, name=name, grid=(N_CHIPS, R // tr),
        in_specs=[pl.BlockSpec((None, 2, tr, C), lambda q, i: (q, 0, i, 0)), pl.BlockSpec((None, tr, C), lambda q, i: (q, i, 0))],
        out_specs=pl.BlockSpec((None, tr, C), lambda q, i: (q, i, 0)),
        out_shape=jax.ShapeDtypeStruct((N_CHIPS, R, C), BF16),
        compiler_params=_cparams(("parallel", "parallel")),
    )(parts.reshape(N_CHIPS, 2, R, C), recv)


def _scatter_to_chips(name, sums):
    na = len(sums)

    def body(*refs):
        ins, outs = refs[:na], refs[na:2 * na]
        send_sems, recv_sems, local_sems = refs[2 * na:]
        x, y, c = _place()
        me = 2 * x + y
        chips = [(1 - x, y), (x, 1 - y), (1 - x, 1 - y)]
        mine = [pltpu.make_async_copy(ins[a].at[me], outs[a].at[me], local_sems.at[a]) for a in range(na)]
        for cp in mine:
            cp.start()
        sent = []
        for a in range(na):
            for k, (qx, qy) in enumerate(chips):
                sent.append(pltpu.make_async_remote_copy(
                    src_ref=ins[a].at[2 * qx + qy], dst_ref=outs[a].at[me],
                    send_sem=send_sems.at[a, k], recv_sem=recv_sems.at[a, k], device_id=(qx, qy, c), device_id_type=MESH))
        for cp in sent:
            cp.start()
        for a in range(na):
            for k, (qx, qy) in enumerate(chips):
                slot = outs[a].at[2 * qx + qy]
                pltpu.make_async_remote_copy(
                    src_ref=slot, dst_ref=slot, send_sem=send_sems.at[a, k], recv_sem=recv_sems.at[a, k],
                    device_id=(qx, qy, c), device_id_type=MESH).wait_recv()
        for cp in sent:
            cp.wait_send()
        for cp in mine:
            cp.wait()

    any_spec = pl.BlockSpec(memory_space=pl.ANY)
    return pl.pallas_call(
        body, name=name,
        out_shape=[jax.ShapeDtypeStruct(p.shape, p.dtype) for p in sums],
        in_specs=[any_spec] * na, out_specs=[any_spec] * na,
        scratch_shapes=[pltpu.SemaphoreType.DMA((na, 3)), pltpu.SemaphoreType.DMA((na, 3)), pltpu.SemaphoreType.DMA((na,))],
        compiler_params=_cparams(),
    )(*sums)


def _gather_row(name, vec, width):
    n = vec.shape[0]
    rows = jnp.pad(vec, (0, width - n)).reshape(SUBLANES_F32, width // SUBLANES_F32)
    return _allgather_small(name, rows).reshape(N_DEV, width)[:, :n]


def _layer_fwd(x, mod, w):
    sh1, sc1, g1, sh2, sc2, g2 = [mod[i:i + 1] for i in range(N_MOD)]
    S = x.shape[0]
    h1 = _normmod_fwd(x, w["norm1_w"], sc1, sh1)
    proj = _matmul("proj_in", h1, w["w_in"], "nn")
    a1, a_out = _conv_a_fwd(proj, w["conv_a_w"], w["conv_a_b"], w["ln_a_w"], w["ln_a_b"])
    qkv = _qkv_views(proj)
    os, ls = zip(*[_attn_fwd(v, dil) for v, dil in zip(qkv, DILATIONS)])
    att, lse, att_b = _attn_merge(os, ls)
    of, st_f = _hgrn_fwd(proj, w["lb_f"], ZF, False)
    ob, st_b = _hgrn_fwd(proj, w["lb_b"], ZB, True)
    rec = _hgrn_post_fwd(of, ob, proj, w["rec_norm_w"])
    mixed = jnp.concatenate([att_b, rec, a_out], axis=1)
    y1 = _matmul("proj_out", mixed, w["w_out"], "nn")
    x2, h2 = _gate_add_normmod(x, y1, g1, w["norm2_w"], sc2, sh2)
    u = _matmul("ffn_up", h2, w["w_up"], "nn")
    act = _ffn_mid_fwd(u, w["conv_f_w"])
    y2 = _matmul("ffn_down", act, w["w_down"], "nn")
    x3 = _gate_add(x2, y2, g2)
    saved = dict(x=x, h1=h1, proj=proj, a1=a1, qkv=qkv, att=att, lse=lse, of=of, ob=ob, st_f=st_f, st_b=st_b,
                 mixed=mixed, y1=y1, x2=x2, h2=h2, u=u, act=act, y2=y2)
    return x3, saved


def _layer_bwd(gx3, mod, w, s):
    sh1, sc1, g1, sh2, sc2, g2 = [mod[i:i + 1] for i in range(N_MOD)]
    S = gx3.shape[0]
    g = {}
    gy2, gg2 = _gate_bwd(gx3, s["y2"], g2)
    gact = _matmul("ffn_down_dx", gy2, w["w_down"], "nt")
    g["w_down"] = _matmul("ffn_down_dw", s["act"], gy2, "tn")
    gu, g["conv_f_w"] = _ffn_mid_bwd(s["u"], gact, w["conv_f_w"])
    gh2 = _matmul_nt_pieces("ffn_up_dx", gu, (w["w_up"][:, :D_FF], w["w_up"][:, D_FF:]))
    g["w_up"] = jnp.concatenate([_matmul("ffn_up_dw", s["h2"], t, "tn") for t in gu], axis=1)
    gx2, gy1, g["norm2_w"], gsc2, gsh2, gg1 = _normmod_gate_bwd(s["x2"], gh2, gx3, w["norm2_w"], sc2, sh2, s["y1"], g1)
    gmixed = _matmul("proj_out_dx", gy1, w["w_out"], "nt")
    g["w_out"] = _matmul("proj_out_dw", s["mixed"], gy1, "tn")
    go, ggr, g["rec_norm_w"] = _hgrn_post_bwd(s["of"], s["ob"], s["proj"], gmixed, w["rec_norm_w"])
    gq_f, gz_f, gv_f, g["lb_f"] = _hgrn_bwd(s["proj"], w["lb_f"], s["st_f"], go, ZF, False)
    gq_r, gz_b, gv_r, g["lb_b"] = _hgrn_bwd(s["proj"], w["lb_b"], s["st_b"], go, ZB, True, other=(gq_f, gv_f))
    dos, lds = _attn_bwd_prep(gmixed, s["att"], s["lse"])
    gqkv = zip(*[_attn_bwd(v, do, ld, dil) for v, do, ld, dil in zip(s["qkv"], dos, lds, DILATIONS)])
    gq_a, gk_a, gv_a = [_sum3_bf16(lst, S, ATT_W) for lst in gqkv]
    gav, gag, gcw, g["conv_a_b"], g["ln_a_w"], g["ln_a_b"] = _conv_a_bwd(
        s["proj"], s["a1"], gmixed, w["conv_a_w"], w["ln_a_w"], w["ln_a_b"])
    g["conv_a_w"] = gcw[:CONV_W]
    gproj = jnp.concatenate([gq_a, gk_a, gv_a, gq_r, gz_f, gz_b, gv_r, ggr, gav, gag,
                             jnp.zeros((S, IN_COLS_PAD - IN_COLS), BF16)], axis=1)
    gh1 = _matmul("proj_in_dx", gproj, w["w_in"], "nt")
    g["w_in"] = _matmul("proj_in_dw", s["h1"], gproj, "tn")
    gx, g["norm1_w"], gsc1, gsh1 = _normmod_bwd(s["x"], gh1, gx2, w["norm1_w"], sc1, sh1)
    gmod = jnp.concatenate([gsh1, gsc1, gg1, gsh2, gsc2, gg2], axis=0)
    return gx, gmod, g


def _permute_in_cols(t):
    pad = jnp.zeros(t.shape[:-1] + (IN_COLS_PAD - IN_COLS,), t.dtype)
    return jnp.concatenate([t[..., CONV_COLS:], t[..., :CONV_COLS], pad], axis=-1)


def _unpermute_in_cols(t):
    return jnp.concatenate([t[..., IN_COLS - CONV_COLS:IN_COLS], t[..., :IN_COLS - CONV_COLS]], axis=-1)


def _cols_from_gathered(t, lead):
    nd = t.ndim
    perm = tuple(range(1, nd - 1)) + (0, nd - 1)
    t = t.transpose(perm)
    return t.reshape(t.shape[:-2] + (t.shape[-2] * t.shape[-1],))


def _cols_to_parts(t):
    L, R, C = t.shape
    return t.reshape(L * R, N_DEV, C // N_DEV).transpose(1, 0, 2)


SMALL_REPL = (("norm1_w", D), ("conv_a_b", CONV_CH), ("ln_a_w", CONV_CH), ("ln_a_b", CONV_CH),
              ("rec_norm_w", REC_W), ("norm2_w", D))


def kernel(x, c, w_ada, b_ada, norm1_w, w_in, conv_a_w, conv_a_b, ln_a_w, ln_a_b, lb_gamma, rec_norm_w, w_out, norm2_w, w_up, conv_f_w, w_down, final_norm_w, loss_target, m_w_ada, m_b_ada, m_norm1_w, m_w_in, m_conv_a_w, m_conv_a_b, m_ln_a_w, m_ln_a_b, m_lb_gamma, m_rec_norm_w, m_w_out, m_norm2_w, m_w_up, m_conv_f_w, m_w_down, m_final_norm_w, v_w_ada, v_b_ada, v_norm1_w, v_w_in, v_conv_a_w, v_conv_a_b, v_ln_a_w, v_ln_a_b, v_lb_gamma, v_rec_norm_w, v_w_out, v_norm2_w, v_w_up, v_conv_f_w, v_w_down, v_final_norm_w):
    px, py, pc = _place()
    me = _index_of((px, py, pc))
    xs, tgt = x[0], loss_target[0]
    S = xs.shape[0]
    ada_cols = w_ada.shape[2]

    big = [w_in.reshape(DEPTH * D, -1), w_up.reshape(DEPTH * D, -1), w_out.reshape(-1, D), w_down.reshape(-1, D)]
    g_in, g_up, g_out, g_down = _allgather_big("gather_weights", [t.astype(BF16) for t in big])
    w_in_f = _permute_in_cols(_cols_from_gathered(g_in.reshape(N_DEV, DEPTH, D, -1), 1))
    w_up_f = _cols_from_gathered(g_up.reshape(N_DEV, DEPTH, D, -1), 1)
    w_out_f = g_out.reshape(N_DEV, DEPTH, D // N_DEV, D).transpose(1, 0, 2, 3).reshape(DEPTH, D, D)
    w_out_f = jnp.concatenate([w_out_f[:, CONV_CH:], w_out_f[:, :CONV_CH]], axis=1)
    w_down_f = g_down.reshape(N_DEV, DEPTH, D_FF // N_DEV, D).transpose(1, 0, 2, 3).reshape(DEPTH, D_FF, D)

    small_in = jnp.concatenate([c.reshape(-1), conv_a_w.reshape(-1), lb_gamma.reshape(-1), conv_f_w.reshape(-1)])
    gs = _gather_row("gather_small", small_in, 8192)
    o1 = D
    o2 = o1 + conv_a_w.size
    o3 = o2 + lb_gamma.size
    c_all = gs[:, :o1]
    conv_a_f = _cols_from_gathered(gs[:, o1:o2].reshape(N_DEV, DEPTH, CONV_W, -1), 1)
    lb_gamma_f = _cols_from_gathered(gs[:, o2:o3].reshape(N_DEV, DEPTH, 2, -1), 1)
    conv_f_f = _cols_from_gathered(gs[:, o3:].reshape(N_DEV, DEPTH, FFN_CONV_W, -1), 1)
    conv_a_pad = jnp.pad(conv_a_f, ((0, 0), (0, CONV_W_PAD - CONV_W), (0, 0)))

    b_loc = lax.dynamic_slice_in_dim(b_ada, me * ada_cols, ada_cols, axis=1)

    def mod_fn(c_all_, w_, b_):
        cond = c_all_ * jax.nn.sigmoid(c_all_)
        return (jnp.concatenate([_dot_nn(cond, w_[l], HP) + b_[l] for l in range(DEPTH)], axis=1),)

    (mod_part,) = _vmem_call("ada_mod", mod_fn, [c_all, w_ada, b_loc[:, None, :]], [((N_DEV, DEPTH * ada_cols), F32)])
    gm = _allgather_small("gather_mod", mod_part).reshape(N_DEV, N_DEV, DEPTH, ada_cols)
    mod = lax.dynamic_index_in_dim(gm, me, axis=1, keepdims=False)
    mod = mod.transpose(1, 0, 2).reshape(DEPTH, N_MOD, D)

    (lb1,) = _vmem_call("lower_bounds", lambda a, b: (_lower_bounds_f(a, b),), [lb_gamma_f[0], lb_gamma_f[1]], [((2, REC_W), F32)])
    lb4 = jnp.concatenate([jnp.zeros_like(lb1), lb1], axis=0)

    def layer_weights(l):
        row = lambda t: t[l].reshape(1, -1)
        return dict(norm1_w=row(norm1_w), w_in=w_in_f[l], conv_a_w=conv_a_pad[l], conv_a_b=row(conv_a_b), ln_a_w=row(ln_a_w),
                    ln_a_b=row(ln_a_b), lb_f=lb4[2 * l:2 * l + 1], lb_b=lb4[2 * l + 1:2 * l + 2], rec_norm_w=row(rec_norm_w),
                    w_out=w_out_f[l], norm2_w=row(norm2_w), w_up=w_up_f[l], conv_f_w=conv_f_f[l], w_down=w_down_f[l])

    ws = [layer_weights(l) for l in range(DEPTH)]
    h, saved = xs, []
    for l in range(DEPTH):
        h, s = _layer_fwd(h, mod[l], ws[l])
        saved.append(s)
    gh, g_final, loss_row = _loss_head(h, tgt, final_norm_w.reshape(1, D))
    loss = lax.psum(loss_row[0, 0], ("x", "y", "c"))
    gmods, gws = [None] * DEPTH, [None] * DEPTH
    for l in reversed(range(DEPTH)):
        gh, gmods[l], gws[l] = _layer_bwd(gh, mod[l], ws[l], saved[l])
    grad_x = gh[None]

    glb1 = jnp.concatenate([gws[1]["lb_f"], gws[1]["lb_b"]], axis=0)

    def lb_bwd_fn(a, b, g1):
        _, vjp = jax.vjp(_lower_bounds_f, a, b)
        return vjp(g1)

    g_lb_gamma = jnp.stack(_vmem_call("lower_bounds_bwd", lb_bwd_fn, [lb_gamma_f[0], lb_gamma_f[1], glb1], [((2, REC_W), F32)] * 2))
    pieces = [jnp.stack(gmods).reshape(-1)]
    for l in range(DEPTH):
        pieces += [gws[l][n].reshape(-1) for n, _ in SMALL_REPL]
    pieces += [g_final.reshape(-1)]
    pieces += [jnp.stack([gws[l]["conv_a_w"] for l in range(DEPTH)]).reshape(-1), g_lb_gamma.reshape(-1),
               jnp.stack([gws[l]["conv_f_w"] for l in range(DEPTH)]).reshape(-1)]
    small_g = jnp.concatenate(pieces)
    n_small = small_g.shape[0]
    gsm = _gather_row("gather_small_grads", small_g, 71680)
    n_mod = DEPTH * N_MOD * D
    gmod_all = gsm[:, :n_mod].reshape(N_DEV, DEPTH, N_MOD * D)
    gmod_loc = lax.dynamic_slice_in_dim(gmod_all, me * ada_cols, ada_cols, axis=2).transpose(1, 0, 2)

    def small_fn(gsm_, c_all_, gm_):
        cond = c_all_ * jax.nn.sigmoid(c_all_)
        gw = jnp.concatenate([_dot_tn(cond, gm_[l], HP) for l in range(DEPTH)], axis=0)
        return jnp.sum(gsm_, axis=0, keepdims=True), gw

    tot, g_w_ada = _vmem_call("small_grads", small_fn, [gsm, c_all, gmod_loc],
                              [((1, n_small), F32), ((DEPTH * D, ada_cols), F32)])
    tot = tot[0]
    grads = {"w_ada": g_w_ada.reshape(DEPTH, D, ada_cols), "b_ada": tot[:n_mod].reshape(DEPTH, N_MOD * D)}
    pos = n_mod
    per_layer = {n: [] for n, _ in SMALL_REPL}
    for l in range(DEPTH):
        for n, width in SMALL_REPL:
            per_layer[n].append(tot[pos:pos + width])
            pos += width
    for n, _ in SMALL_REPL:
        grads[n] = jnp.stack(per_layer[n])
    grads["final_norm_w"] = tot[pos:pos + D]
    pos += D
    n_ca, n_lb, n_cf = DEPTH * CONV_W * CONV_CH, DEPTH * 2 * REC_W, DEPTH * FFN_CONV_W * 2 * D_FF
    g_ca = tot[pos:pos + n_ca].reshape(DEPTH, CONV_W, CONV_CH)
    g_lb = tot[pos + n_ca:pos + n_ca + n_lb].reshape(DEPTH, 2, REC_W)
    g_cf = tot[pos + n_ca + n_lb:pos + n_ca + n_lb + n_cf].reshape(DEPTH, FFN_CONV_W, 2 * D_FF)
    grads["conv_a_w"] = lax.dynamic_slice_in_dim(g_ca, me * conv_a_w.shape[2], conv_a_w.shape[2], axis=2)
    grads["lb_gamma"] = lax.dynamic_slice_in_dim(g_lb, me * lb_gamma.shape[2], lb_gamma.shape[2], axis=2)
    grads["conv_f_w"] = lax.dynamic_slice_in_dim(g_cf, me * conv_f_w.shape[2], conv_f_w.shape[2], axis=2)

    gw_in = _unpermute_in_cols(jnp.stack([gws[l]["w_in"] for l in range(DEPTH)]))
    gw_up = jnp.stack([gws[l]["w_up"] for l in range(DEPTH)])
    gw_out = jnp.stack([gws[l]["w_out"] for l in range(DEPTH)])
    gw_out = jnp.concatenate([gw_out[:, D - CONV_CH:], gw_out[:, :D - CONV_CH]], axis=1)
    gw_down = jnp.stack([gws[l]["w_down"] for l in range(DEPTH)])
    rows_to_parts = lambda t: t.reshape(DEPTH, N_DEV, -1, D).transpose(1, 0, 2, 3).reshape(N_DEV, -1, D)
    parts = [_cols_to_parts(gw_in), _cols_to_parts(gw_up), rows_to_parts(gw_out), rows_to_parts(gw_down)]
    parts = [t.astype(BF16) for t in parts]
    from_sibling = _scatter_to_sibling("scatter_sibling", parts)
    sums = [_pair_sum("pair_sum", p, r) for p, r in zip(parts, from_sibling)]
    r_in, r_up, r_out, r_down = _scatter_to_chips("scatter_chips", sums)

    given = dict(w_ada=(w_ada, m_w_ada, v_w_ada), b_ada=(b_ada, m_b_ada, v_b_ada), norm1_w=(norm1_w, m_norm1_w, v_norm1_w),
                 w_in=(w_in, m_w_in, v_w_in), conv_a_w=(conv_a_w, m_conv_a_w, v_conv_a_w), conv_a_b=(conv_a_b, m_conv_a_b, v_conv_a_b),
                 ln_a_w=(ln_a_w, m_ln_a_w, v_ln_a_w), ln_a_b=(ln_a_b, m_ln_a_b, v_ln_a_b), lb_gamma=(lb_gamma, m_lb_gamma, v_lb_gamma),
                 rec_norm_w=(rec_norm_w, m_rec_norm_w, v_rec_norm_w), w_out=(w_out, m_w_out, v_w_out),
                 norm2_w=(norm2_w, m_norm2_w, v_norm2_w), w_up=(w_up, m_w_up, v_w_up), conv_f_w=(conv_f_w, m_conv_f_w, v_conv_f_w),
                 w_down=(w_down, m_w_down, v_w_down), final_norm_w=(final_norm_w, m_final_norm_w, v_final_norm_w))
    big_parts = dict(w_in=r_in, w_up=r_up, w_out=r_out, w_down=r_down)
    names = list(given)
    res = {}
    for n in names:
        w_, m_, v_ = given[n]
        shape = w_.shape
        C = shape[-1]
        two_d = lambda t: t.reshape(-1, C)
        gp = big_parts[n] if n in big_parts else two_d(grads[n])[None]
        res[n] = [t.reshape(shape) for t in _adamw("adamw_" + n, two_d(w_), two_d(m_), two_d(v_), gp)]
    return (loss, grad_x, *[res[n][0] for n in names], *[res[n][1] for n in names],
            *[res[n][2] for n in names], *[res[n][3] for n in names])
```
